```python
import jax, jax.numpy as jnp
from jax import lax
import numpy as np

D_MODEL = 1024
BATCH = 4
SEQ = 8192
DEPTH = 1

N_META = 16
CHUNK = 64
N_PAD = CHUNK - N_META
EPS = 1e-6

GLA_HEADS = 4
GLA_DV = D_MODEL // 2 // GLA_HEADS
GLA_DK = GLA_DV // 2
GLA_GATE_RANK = 16
GLA_GATE_TAU = 16.0

RET_HEADS = 4
RET_DV = D_MODEL // 2 // RET_HEADS
RET_DK = RET_DV // 2
ROPE_BASE = 10000.0

IN_SPLITS = (
    GLA_HEADS * GLA_DK,
    GLA_HEADS * GLA_DK,
    GLA_HEADS * GLA_DV,
    GLA_HEADS * GLA_DV,
    GLA_GATE_RANK,
    RET_HEADS * RET_DK,
    RET_HEADS * RET_DK,
    RET_HEADS * RET_DV,
    RET_HEADS * RET_DV,
)
IN_WIDTH = sum(IN_SPLITS)
MIX_WIDTH = GLA_HEADS * GLA_DV + RET_HEADS * RET_DV

PEER_HEADS = 8
PEER_NKEYS = 128
PEER_NEXPERTS = PEER_NKEYS * PEER_NKEYS
PEER_DKEY = 256
PEER_TOPK = 16

kernel_name = "hybrid_gla_retnet_peer_meta"


def rmsnorm(x, g):
    xf = x.astype(jnp.float32)
    y = xf * lax.rsqrt(jnp.mean(xf * xf, axis=-1, keepdims=True) + EPS)
    return (y * g.astype(jnp.float32)).astype(x.dtype)


def to_chunks(t, n_heads, d):
    b, p, _ = t.shape
    return t.reshape(b, p // CHUNK, CHUNK, n_heads, d).transpose(0, 3, 1, 2, 4)


def from_chunks(t):
    b, h, n, c, d = t.shape
    return t.transpose(0, 2, 3, 1, 4).reshape(b, n * c, h * d)


def apply_rope(t, cos, sin):
    t1, t2 = jnp.split(t, 2, axis=-1)
    c = cos[None, :, None, :]
    s = sin[None, :, None, :]
    return jnp.concatenate([t1 * c - t2 * s, t1 * s + t2 * c], axis=-1)


def gla_chunked(q, k, v, log_a):
    bcum = jnp.cumsum(log_a, axis=-2)
    b_last = bcum[..., -1:, :]
    q_t = q * jnp.exp(bcum)
    k_t = k * jnp.exp(-bcum)
    causal = jnp.tril(jnp.ones((CHUNK, CHUNK), dtype=bool))
    att = jnp.einsum('bhnid,bhnjd->bhnij', q_t, k_t)
    att = jnp.where(causal, att, 0.0)
    o_intra = jnp.einsum('bhnij,bhnjv->bhniv', att, v)
    k_end = k * jnp.exp(b_last - bcum)
    chunk_kv = jnp.einsum('bhnjd,bhnjv->bhndv', k_end, v)
    chunk_decay = jnp.exp(b_last[..., 0, :])

    def step(state, inp):
        kv_n, a_n = inp
        return a_n[..., None] * state + kv_n, state

    b, h, n, c, dk = q.shape
    s0 = jnp.zeros((b, h, dk, v.shape[-1]), jnp.float32)
    _, s_prev = lax.scan(step, s0, (jnp.moveaxis(chunk_kv, 2, 0), jnp.moveaxis(chunk_decay, 2, 0)))
    s_prev = jnp.moveaxis(s_prev, 0, 2)
    o_inter = jnp.einsum('bhnid,bhndv->bhniv', q_t, s_prev)
    return o_intra + o_inter


def retention_chunked(q, k, v, log_gamma):
    idx = jnp.arange(CHUNK, dtype=jnp.float32)
    lg = log_gamma[:, None]
    diff = idx[:, None] - idx[None, :]
    dmat = jnp.where(diff >= 0, jnp.exp(lg[..., None] * jnp.maximum(diff, 0.0)), 0.0)
    att = jnp.einsum('bhnid,bhnjd->bhnij', q, k) * dmat[None, :, None]
    o_intra = jnp.einsum('bhnij,bhnjv->bhniv', att, v)
    xi = jnp.exp(lg * (idx + 1.0))
    zeta = jnp.exp(lg * (CHUNK - 1.0 - idx))
    chunk_kv = jnp.einsum('bhnjd,hj,bhnjv->bhndv', k, zeta, v)
    gamma_chunk = jnp.exp(log_gamma * CHUNK)[None, :, None, None]

    def step(state, kv_n):
        return gamma_chunk * state + kv_n, state

    b, h, n, c, dk = q.shape
    r0 = jnp.zeros((b, h, dk, v.shape[-1]), jnp.float32)
    _, r_prev = lax.scan(step, r0, jnp.moveaxis(chunk_kv, 2, 0))
    r_prev = jnp.moveaxis(r_prev, 0, 2)
    o_inter = jnp.einsum('bhnid,bhndv->bhniv', q, r_prev) * xi[None, :, None, :, None]
    return o_intra + o_inter


def head_rmsnorm(o, g):
    return o * lax.rsqrt(jnp.mean(o * o, axis=-1, keepdims=True) + EPS) * g


def head_layernorm(o, g, b):
    mu = jnp.mean(o, axis=-1, keepdims=True)
    var = jnp.mean(jnp.square(o - mu), axis=-1, keepdims=True)
    return (o - mu) * lax.rsqrt(var + EPS) * g + b


def token_mixers(xn, w_in, gate_w2, gate_b, gla_gain, ret_gain, ret_bias, w_out,
                 valid, cos, sin, log_gamma):
    f32 = jnp.float32
    split_points = np.cumsum(IN_SPLITS)[:-1].tolist()
    proj = xn @ w_in
    gq, gk, gv, gg, ga, rq, rk, rv, rg = jnp.split(proj, split_points, axis=-1)
    mask = valid[None, :, None]

    log_a = jax.nn.log_sigmoid((ga @ gate_w2 + gate_b).astype(f32)) / GLA_GATE_TAU
    q = to_chunks(gq.astype(f32) * (GLA_DK ** -0.5), GLA_HEADS, GLA_DK)
    k = to_chunks((gk * mask).astype(f32), GLA_HEADS, GLA_DK)
    v = to_chunks(gv.astype(f32), GLA_HEADS, GLA_DV)
    o_gla = gla_chunked(q, k, v, to_chunks(log_a, GLA_HEADS, GLA_DK))
    o_gla = from_chunks(head_rmsnorm(o_gla, gla_gain.astype(f32)))
    o_gla = o_gla * jax.nn.silu(gg.astype(f32))

    b, p, _ = xn.shape
    rq4 = apply_rope(rq.astype(f32).reshape(b, p, RET_HEADS, RET_DK), cos, sin)
    rk4 = apply_rope((rk * mask).astype(f32).reshape(b, p, RET_HEADS, RET_DK), cos, sin) * (RET_DK ** -0.5)
    q = to_chunks(rq4.reshape(b, p, -1), RET_HEADS, RET_DK)
    k = to_chunks(rk4.reshape(b, p, -1), RET_HEADS, RET_DK)
    v = to_chunks(rv.astype(f32), RET_HEADS, RET_DV)
    o_ret = retention_chunked(q, k, v, log_gamma)
    o_ret = from_chunks(head_layernorm(o_ret, ret_gain.astype(f32), ret_bias.astype(f32)))
    o_ret = o_ret * jax.nn.silu(rg.astype(f32))

    o = jnp.concatenate([o_gla, o_ret], axis=-1).astype(xn.dtype)
    return o @ w_out


def peer(xn, w_q, subkeys, u, v):
    b, p, d = xn.shape
    n_chunks = p // CHUNK
    xb = xn.reshape(b, n_chunks, CHUNK, d).transpose(1, 0, 2, 3).reshape(n_chunks, b * CHUNK, d)

    def block(xt):
        t = xt.shape[0]
        q = (xt @ w_q).reshape(t, PEER_HEADS, 2, PEER_DKEY // 2)
        s = jnp.einsum('thcd,hcnd->thcn', q, subkeys).astype(jnp.float32)
        s_top, i_top = lax.top_k(s, PEER_TOPK)
        cand = s_top[:, :, 0, :, None] + s_top[:, :, 1, None, :]
        cand_idx = i_top[:, :, 0, :, None] * PEER_NKEYS + i_top[:, :, 1, None, :]
        best, pos = lax.top_k(cand.reshape(t, PEER_HEADS, PEER_TOPK * PEER_TOPK), PEER_TOPK)
        idx = jnp.take_along_axis(cand_idx.reshape(t, PEER_HEADS, PEER_TOPK * PEER_TOPK), pos, axis=-1)
        g = jax.nn.softmax(best, axis=-1)
        u_sel = u[idx]
        act = jax.nn.gelu(jnp.einsum('td,thkd->thk', xt, u_sel).astype(jnp.float32), approximate=False)
        coef = (g * act).astype(xt.dtype)
        v_sel = v[idx]
        return jnp.einsum('thk,thkd->td', coef, v_sel)

    out = lax.map(block, xb)
    return out.reshape(n_chunks, b, CHUNK, d).transpose(1, 0, 2, 3).reshape(b, p, d)


def setup_inputs(seed: int = 0) -> dict:
    key = jax.random.key(seed)
    ks = jax.random.split(key, 16)
    f32 = jnp.float32
    nrm = lambda k, shape, scale: jax.random.normal(k, shape, f32) * scale
    return {
        "x": nrm(ks[0], (BATCH, SEQ, D_MODEL), 1.0),
        "meta_tokens": nrm(ks[1], (N_META, D_MODEL), 1.0),
        "norm1_gain": 1.0 + nrm(ks[2], (DEPTH, D_MODEL), 0.02),
        "w_in": nrm(ks[3], (DEPTH, D_MODEL, IN_WIDTH), D_MODEL ** -0.5),
        "gla_gate_w2": nrm(ks[4], (DEPTH, GLA_GATE_RANK, GLA_HEADS * GLA_DK), GLA_GATE_RANK ** -0.5),
        "gla_gate_b": nrm(ks[5], (DEPTH, GLA_HEADS * GLA_DK), 0.1),
        "gla_norm_gain": 1.0 + nrm(ks[6], (DEPTH, GLA_DV), 0.02),
        "ret_norm_gain": 1.0 + nrm(ks[7], (DEPTH, RET_DV), 0.02),
        "ret_norm_bias": nrm(ks[8], (DEPTH, RET_DV), 0.02),
        "w_out": nrm(ks[9], (DEPTH, MIX_WIDTH, D_MODEL), MIX_WIDTH ** -0.5),
        "norm2_gain": 1.0 + nrm(ks[10], (DEPTH, D_MODEL), 0.02),
        "peer_w_q": nrm(ks[11], (DEPTH, D_MODEL, PEER_HEADS * PEER_DKEY), D_MODEL ** -0.5),
        "peer_subkeys": nrm(ks[12], (DEPTH, PEER_HEADS, 2, PEER_NKEYS, PEER_DKEY // 2), (PEER_DKEY // 2) ** -0.5),
        "peer_u": nrm(ks[13], (DEPTH, PEER_NEXPERTS, D_MODEL), D_MODEL ** -0.5),
        "peer_v": nrm(ks[14], (DEPTH, PEER_NEXPERTS, D_MODEL), PEER_HEADS ** -0.5),
        "final_gain": 1.0 + nrm(ks[15], (D_MODEL,), 0.02),
    }


def reference(x, meta_tokens, norm1_gain, w_in, gla_gate_w2, gla_gate_b, gla_norm_gain,
              ret_norm_gain, ret_norm_bias, w_out, norm2_gain, peer_w_q, peer_subkeys,
              peer_u, peer_v, final_gain):
    b, s, d = x.shape
    p = s + CHUNK
    pad = jnp.zeros((b, N_PAD, d), x.dtype)
    meta = jnp.broadcast_to(meta_tokens.astype(x.dtype)[None], (b, N_META, d))
    h = jnp.concatenate([pad, meta, x], axis=1)

    pos = (jnp.arange(p) - N_PAD).astype(jnp.float32)
    valid = (pos >= 0).astype(x.dtype)
    inv_freq = ROPE_BASE ** (-jnp.arange(0, RET_DK, 2, dtype=jnp.float32) / RET_DK)
    ang = pos[:, None] * inv_freq[None, :]
    cos, sin = jnp.cos(ang), jnp.sin(ang)
    log_gamma = jnp.log(1.0 - jnp.power(2.0, -5.0 - jnp.arange(RET_HEADS, dtype=jnp.float32)))

    for l in range(DEPTH):
        h = h + token_mixers(rmsnorm(h, norm1_gain[l]), w_in[l], gla_gate_w2[l], gla_gate_b[l],
                             gla_norm_gain[l], ret_norm_gain[l], ret_norm_bias[l], w_out[l],
                             valid, cos, sin, log_gamma).astype(h.dtype)
        h = h + peer(rmsnorm(h, norm2_gain[l]), peer_w_q[l], peer_subkeys[l],
                     peer_u[l], peer_v[l]).astype(h.dtype)

    h = rmsnorm(h, final_gain)
    return h[:, CHUNK:]
```

```python
import functools
import math

import jax
import jax.numpy as jnp
import numpy as np
from jax import lax
from jax.experimental import pallas as pl
from jax.experimental.pallas import tpu as pltpu
from jax.experimental.pallas import tpu_sc as plsc

F32 = jnp.float32
BF16 = jnp.bfloat16

D_MODEL = 1024
N_META = 16
CHUNK = 64
N_PAD = CHUNK - N_META
EPS = 1e-6

HEADS = 4
DV = 128
DK = 64
LANES = 128
GATE_RANK = 16
GATE_TAU = 16.0
ROPE_BASE = 10000.0
IN_SPLITS = (HEADS * DK, HEADS * DK, HEADS * DV, HEADS * DV, GATE_RANK,
             HEADS * DK, HEADS * DK, HEADS * DV, HEADS * DV)

PEER_HEADS = 8
PEER_NKEYS = 128
PEER_DKEY = 256
PEER_TOPK = 16
PEER_SEL = PEER_HEADS * PEER_TOPK

HW = HEADS * LANES
OFF_GQ, OFF_GK, OFF_GV, OFF_GG = 0, HW, 2 * HW, 3 * HW
OFF_RQ, OFF_RK, OFF_RV, OFF_RG = 4 * HW, 5 * HW, 6 * HW, 7 * HW
OFF_RQR, OFF_RKR, OFF_GA = 8 * HW, 9 * HW, 10 * HW
PROJ_W = 10 * HW + 2 * LANES
PROJ_TN = PROJ_W // 3

LOG_GAMMA = tuple(math.log(1.0 - 2.0 ** (-5.0 - h)) for h in range(HEADS))

SC_CORES, SC_SUBCORES, SC_LANES = 2, 16, 16
SC_WORKERS = SC_CORES * SC_SUBCORES
SC_TOK_BATCH = 8
SC_ROWS = 32
SC_STEPS = SC_TOK_BATCH * PEER_SEL // SC_ROWS
SC_PARTS = PEER_SEL // SC_ROWS

VMEM_LIMIT = 48 * 1024 * 1024


def _dot(a, b, **kw):
    return jnp.dot(a, b, preferred_element_type=F32, **kw)


def _dot_nt(a, b):
    return lax.dot_general(a, b, (((1,), (1,)), ((), ())), preferred_element_type=F32)


def _dot_tn(a, b):
    return lax.dot_general(a, b, (((0,), (0,)), ((), ())), preferred_element_type=F32)


def _rms(x, g):
    return x * lax.rsqrt(jnp.mean(x * x, axis=-1, keepdims=True) + EPS) * g


def _silu(x):
    return x * jax.nn.sigmoid(x)


def _norm_proj_kernel(x_ref, g_ref, w_ref, o_ref):
    xn = _rms(x_ref[...], g_ref[...]).astype(BF16)
    o_ref[...] = _dot(xn, w_ref[...])


def _norm_proj(x2d, gain, w_all):
    n = x2d.shape[0]
    tm = min(512, n)
    return pl.pallas_call(
        _norm_proj_kernel,
        grid=(n // tm, PROJ_W // PROJ_TN),
        in_specs=[
            pl.BlockSpec((tm, D_MODEL), lambda i, j: (i, 0)),
            pl.BlockSpec((1, D_MODEL), lambda i, j: (0, 0)),
            pl.BlockSpec((D_MODEL, PROJ_TN), lambda i, j: (0, j)),
        ],
        out_specs=pl.BlockSpec((tm, PROJ_TN), lambda i, j: (i, j)),
        out_shape=jax.ShapeDtypeStruct((n, PROJ_W), F32),
        compiler_params=pltpu.CompilerParams(
            dimension_semantics=("parallel", "parallel"), vmem_limit_bytes=VMEM_LIMIT),
        name="norm_in_proj",
    )(x2d, gain, w_all)


def _mixer_kernel(proj_ref, cos_ref, sin_ref, w2_ref, gb_ref, ggain_ref, rgain_ref, rbias_ref,
                  s0_ref, r0_ref, o_ref, s_out_ref, r_out_ref, s_scr, r_scr, *, n_invalid, rows_blk):
    j = pl.program_id(1)

    @pl.when(j == 0)
    def _():
        s_scr[...] = s0_ref[...]
        r_scr[...] = r0_ref[...]

    row = lax.broadcasted_iota(jnp.int32, (CHUNK, CHUNK), 0)
    col = lax.broadcasted_iota(jnp.int32, (CHUNK, CHUNK), 1)
    causal = row >= col
    tril = causal.astype(F32)
    diff = jnp.maximum(row - col, 0).astype(F32)
    rowf = lax.broadcasted_iota(jnp.int32, (CHUNK, LANES), 0).astype(F32)
    row1 = lax.broadcasted_iota(jnp.int32, (CHUNK, 1), 0)

    def chunk_body(c, carry):
        r0 = pl.multiple_of(c * CHUNK, CHUNK)
        rows = pl.ds(r0, CHUNK)
        valid = ((j * rows_blk + r0 + row1) >= n_invalid).astype(F32)

        ga = proj_ref[rows, OFF_GA:OFF_GA + LANES].astype(BF16)
        pre = _dot(ga, w2_ref[...]) + gb_ref[...]
        log_a = jax.nn.log_sigmoid(pre) * (1.0 / GATE_TAU)
        bcum = _dot(tril, log_a, precision=lax.Precision.HIGHEST)
        b_last = bcum[CHUNK - 1:CHUNK, :]
        gq = proj_ref[rows, OFF_GQ:OFF_GQ + HW] * (DK ** -0.5)
        gk = proj_ref[rows, OFF_GK:OFF_GK + HW] * valid
        q_t = (gq * jnp.exp(bcum)).astype(BF16)
        k_t = (gk * jnp.exp(-bcum)).astype(BF16)
        k_end = (gk * jnp.exp(b_last - bcum)).astype(BF16)
        decay = jnp.exp(b_last)
        for h in range(HEADS):
            sl = slice(h * LANES, (h + 1) * LANES)
            v = proj_ref[rows, OFF_GV + h * DV:OFF_GV + (h + 1) * DV].astype(BF16)
            att = jnp.where(causal, _dot_nt(q_t[:, sl], k_t[:, sl]), 0.0).astype(BF16)
            s_prev = s_scr[h]
            o = _dot(att, v) + _dot_nt(q_t[:, sl], s_prev.astype(BF16))
            s_scr[h] = s_prev * decay[:, sl] + _dot_tn(v, k_end[:, sl])
            o = o * lax.rsqrt(jnp.mean(o * o, axis=-1, keepdims=True) + EPS) * ggain_ref[...]
            gate = proj_ref[rows, OFF_GG + h * DV:OFF_GG + (h + 1) * DV]
            o_ref[rows, h * DV:(h + 1) * DV] = (o * _silu(gate)).astype(o_ref.dtype)

        cosb = cos_ref[rows, :]
        sinb = sin_ref[rows, :]
        rq = proj_ref[rows, OFF_RQ:OFF_RQ + HW]
        rqr = proj_ref[rows, OFF_RQR:OFF_RQR + HW]
        rk = proj_ref[rows, OFF_RK:OFF_RK + HW] * valid
        rkr = proj_ref[rows, OFF_RKR:OFF_RKR + HW] * valid
        q_r = (rq * cosb + rqr * sinb).astype(BF16)
        k_r = (rk * cosb + rkr * sinb) * (DK ** -0.5)
        for h in range(HEADS):
            sl = slice(h * LANES, (h + 1) * LANES)
            lg = LOG_GAMMA[h]
            dmat = jnp.where(causal, jnp.exp(lg * diff), 0.0)
            xi = jnp.exp(lg * (rowf + 1.0))
            zeta = jnp.exp(lg * (CHUNK - 1.0 - rowf))
            v = proj_ref[rows, OFF_RV + h * DV:OFF_RV + (h + 1) * DV].astype(BF16)
            k_h = k_r[:, sl]
            att = (_dot_nt(q_r[:, sl], k_h.astype(BF16)) * dmat).astype(BF16)
            r_prev = r_scr[h]
            o = _dot(att, v) + _dot_nt(q_r[:, sl], r_prev.astype(BF16)) * xi
            r_scr[h] = r_prev * math.exp(lg * CHUNK) + _dot_tn(v, (k_h * zeta).astype(BF16))
            mu = jnp.mean(o, axis=-1, keepdims=True)
            var = jnp.mean(jnp.square(o - mu), axis=-1, keepdims=True)
            o = (o - mu) * lax.rsqrt(var + EPS) * rgain_ref[...] + rbias_ref[...]
            gate = proj_ref[rows, OFF_RG + h * DV:OFF_RG + (h + 1) * DV]
            o_ref[rows, HW + h * DV:HW + (h + 1) * DV] = (o * _silu(gate)).astype(o_ref.dtype)
        return carry

    lax.fori_loop(0, rows_blk // CHUNK, chunk_body, 0)
    s_out_ref[...] = s_scr[...]
    r_out_ref[...] = r_scr[...]


def _mixer(proj, cosf, sinf, w2p, gbp, ggain, rgain, rbias, s0, r0, *, batch, n_invalid):
    rows = proj.shape[0] // batch
    rows_blk = min(256, rows)
    nblk = rows // rows_blk
    const2 = lambda b, j: (0, 0)
    const3 = lambda b, j: (0, 0, 0)
    state = jax.ShapeDtypeStruct((HEADS, DV, LANES), F32)
    return pl.pallas_call(
        functools.partial(_mixer_kernel, n_invalid=n_invalid, rows_blk=rows_blk),
        grid=(batch, nblk),
        in_specs=[
            pl.BlockSpec((rows_blk, PROJ_W), lambda b, j: (b * nblk + j, 0)),
            pl.BlockSpec((rows_blk, HW), lambda b, j: (j, 0)),
            pl.BlockSpec((rows_blk, HW), lambda b, j: (j, 0)),
            pl.BlockSpec((LANES, HW), const2),
            pl.BlockSpec((1, HW), const2),
            pl.BlockSpec((1, DV), const2),
            pl.BlockSpec((1, DV), const2),
            pl.BlockSpec((1, DV), const2),
            pl.BlockSpec((HEADS, DV, LANES), const3),
            pl.BlockSpec((HEADS, DV, LANES), const3),
        ],
        out_specs=[
            pl.BlockSpec((rows_blk, D_MODEL), lambda b, j: (b * nblk + j, 0)),
            pl.BlockSpec((HEADS, DV, LANES), const3),
            pl.BlockSpec((HEADS, DV, LANES), const3),
        ],
        out_shape=[jax.ShapeDtypeStruct((batch * rows, D_MODEL), BF16), state, state],
        scratch_shapes=[pltpu.VMEM((HEADS, DV, LANES), F32), pltpu.VMEM((HEADS, DV, LANES), F32)],
        compiler_params=pltpu.CompilerParams(
            dimension_semantics=("arbitrary", "arbitrary"), vmem_limit_bytes=VMEM_LIMIT),
        name="token_mixers",
    )(proj, cosf, sinf, w2p, gbp, ggain, rgain, rbias, s0, r0)


def _out_proj_kernel(o_ref, x_ref, w_ref, g_ref, h_ref, xn_ref):
    h = x_ref[...] + _dot(o_ref[...], w_ref[...])
    h_ref[...] = h
    xn_ref[...] = _rms(h, g_ref[...])


def _out_proj(o, x2d, w_out, gain2):
    n = x2d.shape[0]
    tm = min(512, n)
    blk = pl.BlockSpec((tm, D_MODEL), lambda i: (i, 0))
    return pl.pallas_call(
        _out_proj_kernel,
        grid=(n // tm,),
        in_specs=[blk, blk, pl.BlockSpec((D_MODEL, D_MODEL), lambda i: (0, 0)),
                  pl.BlockSpec((1, D_MODEL), lambda i: (0, 0))],
        out_specs=[blk, blk],
        out_shape=[jax.ShapeDtypeStruct((n, D_MODEL), F32)] * 2,
        compiler_params=pltpu.CompilerParams(
            dimension_semantics=("parallel",), vmem_limit_bytes=VMEM_LIMIT),
        name="out_proj_norm",
    )(o, x2d, w_out, gain2)


def _top_k_rows(s, payload=None):
    n = s.shape[0]
    rowi = lax.broadcasted_iota(jnp.int32, s.shape, 0)
    vals, picks = [], []
    for _ in range(PEER_TOPK):
        m = jnp.max(s, axis=0, keepdims=True)
        am = jnp.min(jnp.where(s == m, rowi, n), axis=0, keepdims=True)
        sel = rowi == am
        vals.append(m)
        if payload is None:
            picks.append(am)
        else:
            picks.append(jnp.min(jnp.where(sel, payload, jnp.iinfo(jnp.int32).max), axis=0, keepdims=True))
        s = jnp.where(sel, -jnp.inf, s)
    return jnp.concatenate(vals, axis=0), jnp.concatenate(picks, axis=0)


def _route_kernel(xn_ref, wq_ref, sk_ref, idx_ref, g_ref, q_scr, i_scr, g_scr):
    tr = xn_ref.shape[0]
    q = _dot(xn_ref[...].astype(BF16), wq_ref[...])
    for hc in range(2 * PEER_HEADS):
        q_scr[hc] = q[:, hc * LANES:(hc + 1) * LANES].astype(BF16)

    def sub_body(sb, carry):
        rows = pl.ds(pl.multiple_of(sb * LANES, LANES), LANES)

        def head_body(h, c2):
            s0 = _dot_nt(sk_ref[2 * h], q_scr[2 * h, rows, :])
            s1 = _dot_nt(sk_ref[2 * h + 1], q_scr[2 * h + 1, rows, :])
            v0, i0 = _top_k_rows(s0)
            v1, i1 = _top_k_rows(s1)
            cand = jnp.concatenate([v0[a:a + 1] + v1 for a in range(PEER_TOPK)], axis=0)
            cidx = jnp.concatenate([i0[a:a + 1] * PEER_NKEYS + i1 for a in range(PEER_TOPK)], axis=0)
            best, e = _top_k_rows(cand, cidx)
            ex = jnp.exp(best - best[0:1])
            i_scr[h] = e
            g_scr[h] = ex / jnp.sum(ex, axis=0, keepdims=True)
            return c2

        lax.fori_loop(0, PEER_HEADS, head_body, 0)
        idx_ref[rows, :] = i_scr[...].reshape(PEER_SEL, LANES).T
        g_ref[rows, :] = g_scr[...].reshape(PEER_SEL, LANES).T
        return carry

    lax.fori_loop(0, tr // LANES, sub_body, 0)


def _route(xn2, w_q, subkeys):
    n = xn2.shape[0]
    tr = min(512, n)
    qw = 2 * PEER_HEADS * LANES
    return pl.pallas_call(
        _route_kernel,
        grid=(n // tr,),
        in_specs=[pl.BlockSpec((tr, D_MODEL), lambda i: (i, 0)),
                  pl.BlockSpec((D_MODEL, qw), lambda i: (0, 0)),
                  pl.BlockSpec((2 * PEER_HEADS, PEER_NKEYS, LANES), lambda i: (0, 0, 0))],
        out_specs=[pl.BlockSpec((tr, PEER_SEL), lambda i: (i, 0))] * 2,
        out_shape=[jax.ShapeDtypeStruct((n, PEER_SEL), jnp.int32),
                   jax.ShapeDtypeStruct((n, PEER_SEL), F32)],
        scratch_shapes=[pltpu.VMEM((2 * PEER_HEADS, tr, LANES), BF16),
                        pltpu.VMEM((PEER_HEADS, PEER_TOPK, LANES), jnp.int32),
                        pltpu.VMEM((PEER_HEADS, PEER_TOPK, LANES), F32)],
        compiler_params=pltpu.CompilerParams(
            dimension_semantics=("parallel",), vmem_limit_bytes=VMEM_LIMIT),
        name="peer_route",
    )(xn2, w_q, subkeys)


def _sc_worker_id():
    return lax.axis_index("s") * SC_CORES + lax.axis_index("c")


def _sc_mesh():
    return plsc.VectorSubcoreMesh(core_axis_name="c", subcore_axis_name="s")


def _sc_gather_loop(table_hbm, idx_v, rows0, rows1, sem0, sem1, compute):
    def gather(s, buf, sem):
        off = pl.multiple_of(s * SC_ROWS, SC_ROWS)
        return pltpu.make_async_copy(table_hbm.at[idx_v.at[pl.ds(off, SC_ROWS)]], buf, sem)

    gather(0, rows0, sem0).start()

    def step2(i, carry):
        s = 2 * i
        gather(s + 1, rows1, sem1).start()
        gather(s, rows0, sem0).wait()
        compute(s, rows0)

        @pl.when(s + 2 < SC_STEPS)
        def _():
            gather(s + 2, rows0, sem0).start()

        gather(s + 1, rows1, sem1).wait()
        compute(s + 1, rows1)
        return carry

    lax.fori_loop(0, SC_STEPS // 2, step2, 0)


def _peer_dot_sc(u, idx_flat, xn2):
    t = xn2.shape[0]
    tpw = t // SC_WORKERS

    @functools.partial(
        pl.kernel,
        mesh=_sc_mesh(),
        out_type=jax.ShapeDtypeStruct((t, PEER_SEL * SC_LANES), F32),
        scratch_types=[
            pltpu.VMEM((SC_TOK_BATCH * PEER_SEL,), jnp.int32),
            pltpu.VMEM((SC_TOK_BATCH, D_MODEL), F32),
            pltpu.VMEM((SC_ROWS, D_MODEL), F32),
            pltpu.VMEM((SC_ROWS, D_MODEL), F32),
            pltpu.VMEM((SC_TOK_BATCH, PEER_SEL * SC_LANES), F32),
            pltpu.SemaphoreType.DMA,
            pltpu.SemaphoreType.DMA,
        ],
        compiler_params=pltpu.CompilerParams(needs_layout_passes=False),
        name="peer_dot",
    )
    def body(u_hbm, idx_hbm, x_hbm, out_hbm, idx_v, x_v, rows0, rows1, out_v, sem0, sem1):
        tok0 = _sc_worker_id() * tpw

        def compute(s, rows):
            tl = s // SC_PARTS
            rb = (s % SC_PARTS) * SC_ROWS
            for rg in range(SC_ROWS // 16):
                def cbody(c, acc):
                    off = pl.multiple_of(c * SC_LANES, SC_LANES)
                    xv = x_v[tl, pl.ds(off, SC_LANES)]
                    return tuple(acc[r] + rows[rg * 16 + r, pl.ds(off, SC_LANES)] * xv for r in range(16))

                acc = lax.fori_loop(0, D_MODEL // SC_LANES, cbody,
                                    tuple(jnp.zeros((SC_LANES,), F32) for _ in range(16)))
                for r in range(16):
                    o = pl.multiple_of((rb + rg * 16 + r) * SC_LANES, SC_LANES)
                    out_v[tl, pl.ds(o, SC_LANES)] = acc[r]

        def batch(tb, carry):
            t0 = pl.multiple_of(tok0 + tb * SC_TOK_BATCH, SC_TOK_BATCH)
            pltpu.sync_copy(idx_hbm.at[pl.ds(pl.multiple_of(t0 * PEER_SEL, SC_TOK_BATCH * PEER_SEL),
                                             SC_TOK_BATCH * PEER_SEL)], idx_v)
            pltpu.sync_copy(x_hbm.at[pl.ds(t0, SC_TOK_BATCH)], x_v)
            _sc_gather_loop(u_hbm, idx_v, rows0, rows1, sem0, sem1, compute)
            pltpu.sync_copy(out_v, out_hbm.at[pl.ds(t0, SC_TOK_BATCH)])
            return carry

        lax.fori_loop(0, tpw // SC_TOK_BATCH, batch, 0)

    return body(u, idx_flat, xn2)


def _peer_combine_sc(v, idx_flat, coef_b):
    t = coef_b.shape[0]
    tpw = t // SC_WORKERS
    group = 16 * SC_LANES

    @functools.partial(
        pl.kernel,
        mesh=_sc_mesh(),
        out_type=jax.ShapeDtypeStruct((t, D_MODEL), F32),
        scratch_types=[
            pltpu.VMEM((SC_TOK_BATCH * PEER_SEL,), jnp.int32),
            pltpu.VMEM((SC_TOK_BATCH, PEER_SEL * SC_LANES), F32),
            pltpu.VMEM((SC_ROWS, D_MODEL), F32),
            pltpu.VMEM((SC_ROWS, D_MODEL), F32),
            pltpu.VMEM((SC_TOK_BATCH, D_MODEL), F32),
            pltpu.SemaphoreType.DMA,
            pltpu.SemaphoreType.DMA,
        ],
        compiler_params=pltpu.CompilerParams(needs_layout_passes=False),
        name="peer_combine",
    )
    def body(v_hbm, idx_hbm, coef_hbm, out_hbm, idx_v, coef_v, rows0, rows1, out_v, sem0, sem1):
        tok0 = _sc_worker_id() * tpw

        def compute(s, rows):
            tl = s // SC_PARTS
            rb = (s % SC_PARTS) * SC_ROWS
            for g in range(D_MODEL // group):
                def rbody(r, acc):
                    co = pl.multiple_of((rb + r) * SC_LANES, SC_LANES)
                    cf = coef_v[tl, pl.ds(co, SC_LANES)]
                    return tuple(acc[k] + rows[r, pl.ds(g * group + k * SC_LANES, SC_LANES)] * cf
                                 for k in range(16))

                init = tuple(out_v[tl, pl.ds(g * group + k * SC_LANES, SC_LANES)] for k in range(16))
                acc = lax.fori_loop(0, SC_ROWS, rbody, init)
                for k in range(16):
                    out_v[tl, pl.ds(g * group + k * SC_LANES, SC_LANES)] = acc[k]

        def batch(tb, carry):
            t0 = pl.multiple_of(tok0 + tb * SC_TOK_BATCH, SC_TOK_BATCH)
            pltpu.sync_copy(idx_hbm.at[pl.ds(pl.multiple_of(t0 * PEER_SEL, SC_TOK_BATCH * PEER_SEL),
                                             SC_TOK_BATCH * PEER_SEL)], idx_v)
            pltpu.sync_copy(coef_hbm.at[pl.ds(t0, SC_TOK_BATCH)], coef_v)

            def zero(tl, c2):
                for k in range(D_MODEL // SC_LANES):
                    out_v[tl, pl.ds(k * SC_LANES, SC_LANES)] = jnp.zeros((SC_LANES,), F32)
                return c2

            lax.fori_loop(0, SC_TOK_BATCH, zero, 0)
            _sc_gather_loop(v_hbm, idx_v, rows0, rows1, sem0, sem1, compute)
            pltpu.sync_copy(out_v, out_hbm.at[pl.ds(t0, SC_TOK_BATCH)])
            return carry

        lax.fori_loop(0, tpw // SC_TOK_BATCH, batch, 0)

    return body(v, idx_flat, coef_b)


def _coef_kernel(part_ref, g_ref, e_ref, coef_ref):
    hp = lax.Precision.HIGHEST
    e = e_ref[...]
    act = lax.dot_general(part_ref[...], e, (((1,), (1,)), ((), ())),
                          precision=hp, preferred_element_type=F32)
    gelu = 0.5 * act * (1.0 + lax.erf(act * np.float32(math.sqrt(0.5))))
    coef_ref[...] = _dot(g_ref[...] * gelu, e, precision=hp)


def _coef(part, g):
    n = part.shape[0]
    tr = min(512, n)
    w = PEER_SEL * SC_LANES
    expand = (jnp.arange(w, dtype=jnp.int32)[None, :] // SC_LANES
              == jnp.arange(PEER_SEL, dtype=jnp.int32)[:, None]).astype(F32)
    return pl.pallas_call(
        _coef_kernel,
        grid=(n // tr,),
        in_specs=[pl.BlockSpec((tr, w), lambda i: (i, 0)),
                  pl.BlockSpec((tr, PEER_SEL), lambda i: (i, 0)),
                  pl.BlockSpec((PEER_SEL, w), lambda i: (0, 0))],
        out_specs=pl.BlockSpec((tr, w), lambda i: (i, 0)),
        out_shape=jax.ShapeDtypeStruct((n, w), F32),
        compiler_params=pltpu.CompilerParams(
            dimension_semantics=("parallel",), vmem_limit_bytes=VMEM_LIMIT),
        name="peer_coef",
    )(part, g, expand)


def _final_kernel(h_ref, p_ref, g_ref, o_ref):
    o_ref[...] = _rms(h_ref[...] + p_ref[...], g_ref[...])


def _final(h2, peer_out, gain):
    n = h2.shape[0]
    tm = min(1024, n)
    blk = pl.BlockSpec((tm, D_MODEL), lambda i: (i, 0))
    return pl.pallas_call(
        _final_kernel,
        grid=(n // tm,),
        in_specs=[blk, blk, pl.BlockSpec((1, D_MODEL), lambda i: (0, 0))],
        out_specs=blk,
        out_shape=jax.ShapeDtypeStruct((n, D_MODEL), F32),
        compiler_params=pltpu.CompilerParams(
            dimension_semantics=("parallel",), vmem_limit_bytes=VMEM_LIMIT),
        name="final_norm",
    )(h2, peer_out, gain)


def _pad_heads(w, dk):
    lead = w.shape[:-1]
    w = w.reshape(lead + (HEADS, dk))
    w = jnp.pad(w, [(0, 0)] * len(lead) + [(0, 0), (0, LANES - dk)])
    return w.reshape(lead + (HEADS * LANES,))


def _rot_heads(w, dk):
    lead = w.shape[:-1]
    w = w.reshape(lead + (HEADS, 2, dk // 2))
    w = jnp.stack([-w[..., 1, :], w[..., 0, :]], axis=-2)
    return w.reshape(lead + (HEADS * dk,))


def _pack_in_weights(w_in):
    gq, gk, gv, gg, ga, rq, rk, rv, rg = jnp.split(w_in, np.cumsum(IN_SPLITS)[:-1].tolist(), axis=-1)
    ga = jnp.pad(ga, ((0, 0), (0, 2 * LANES - GATE_RANK)))
    return jnp.concatenate(
        [_pad_heads(gq, DK), _pad_heads(gk, DK), gv, gg,
         _pad_heads(rq, DK), _pad_heads(rk, DK), rv, rg,
         _pad_heads(_rot_heads(rq, DK), DK), _pad_heads(_rot_heads(rk, DK), DK), ga],
        axis=-1).astype(BF16)


def _rope_tables(pos):
    inv_freq = ROPE_BASE ** (-jnp.arange(0, DK, 2, dtype=F32) / DK)
    ang = pos[:, None] * inv_freq[None, :]
    full = lambda t: _pad_heads(jnp.tile(t, (1, 2 * HEADS)), DK)
    return full(jnp.cos(ang)), full(jnp.sin(ang))


def kernel(x, meta_tokens, norm1_gain, w_in, gla_gate_w2, gla_gate_b, gla_norm_gain, ret_norm_gain, ret_norm_bias, w_out, norm2_gain, peer_w_q, peer_subkeys, peer_u, peer_v, final_gain):
    b, s, d = x.shape
    assert d == D_MODEL and s % 256 == 0 and (b * s) % (SC_WORKERS * SC_TOK_BATCH) == 0
    assert norm1_gain.shape[0] == 1, "single-layer block"
    x2d = x.reshape(b * s, d)

    w_all = _pack_in_weights(w_in[0])
    w2p = jnp.pad(_pad_heads(gla_gate_w2[0], DK), ((0, LANES - GATE_RANK), (0, 0))).astype(BF16)
    gbp = _pad_heads(gla_gate_b[0][None, :], DK)
    g1 = norm1_gain[0][None, :]
    ggain = gla_norm_gain[0][None, :]
    rgain = ret_norm_gain[0][None, :]
    rbias = ret_norm_bias[0][None, :]

    h0 = jnp.concatenate([jnp.zeros((N_PAD, d), x.dtype), meta_tokens.astype(x.dtype)], axis=0)
    cos0, sin0 = _rope_tables(jnp.arange(CHUNK, dtype=F32) - N_PAD)
    cos1, sin1 = _rope_tables(jnp.arange(s, dtype=F32) + N_META)
    zero_state = jnp.zeros((HEADS, DV, LANES), F32)
    _, s_meta, r_meta = _mixer(_norm_proj(h0, g1, w_all), cos0, sin0, w2p, gbp, ggain, rgain, rbias,
                               zero_state, zero_state, batch=1, n_invalid=N_PAD)
    o, _, _ = _mixer(_norm_proj(x2d, g1, w_all), cos1, sin1, w2p, gbp, ggain, rgain, rbias,
                     s_meta, r_meta, batch=b, n_invalid=0)

    h2, xn2 = _out_proj(o, x2d, w_out[0].astype(BF16), norm2_gain[0][None, :])

    subkeys = peer_subkeys[0].reshape(2 * PEER_HEADS, PEER_NKEYS, PEER_DKEY // 2).astype(BF16)
    idx, g = _route(xn2, peer_w_q[0].astype(BF16), subkeys)
    idx_flat = idx.reshape(-1)
    part = _peer_dot_sc(peer_u[0], idx_flat, xn2)
    coef_b = _coef(part, g)
    peer_out = _peer_combine_sc(peer_v[0], idx_flat, coef_b)

    out = _final(h2, peer_out, final_gain[None, :])
    return out.reshape(b, s, d)
```

```python
import functools
import math

import jax
import jax.numpy as jnp
import numpy as np
from jax import lax
from jax.experimental import pallas as pl
from jax.experimental.pallas import tpu as pltpu
from jax.experimental.pallas import tpu_sc as plsc

F32 = jnp.float32
BF16 = jnp.bfloat16

D_MODEL = 1024
N_META = 16
CHUNK = 64
N_PAD = CHUNK - N_META
EPS = 1e-6

HEADS = 4
DV = 128
DK = 64
LANES = 128
GATE_RANK = 16
GATE_TAU = 16.0
ROPE_BASE = 10000.0
IN_SPLITS = (HEADS * DK, HEADS * DK, HEADS * DV, HEADS * DV, GATE_RANK,
             HEADS * DK, HEADS * DK, HEADS * DV, HEADS * DV)

PEER_HEADS = 8
PEER_NKEYS = 128
PEER_DKEY = 256
PEER_TOPK = 16
PEER_SEL = PEER_HEADS * PEER_TOPK

HW = HEADS * LANES
OFF_GQ, OFF_GK, OFF_GV, OFF_GG = 0, HW, 2 * HW, 3 * HW
OFF_RQ, OFF_RK, OFF_RV, OFF_RG = 4 * HW, 5 * HW, 6 * HW, 7 * HW
OFF_RQR, OFF_RKR, OFF_GA = 8 * HW, 9 * HW, 10 * HW
PROJ_W = 10 * HW + 2 * LANES
PROJ_TN = PROJ_W // 3

LOG_GAMMA = tuple(math.log(1.0 - 2.0 ** (-5.0 - h)) for h in range(HEADS))

SC_CORES, SC_SUBCORES, SC_LANES = 2, 16, 16
SC_WORKERS = SC_CORES * SC_SUBCORES
SC_TOK_BATCH = 16
SC_ROWS = 8
SC_NBUF = 8
SC_GATHERS = SC_TOK_BATCH * PEER_SEL // SC_ROWS
ROW_TILES = D_MODEL // LANES

VMEM_LIMIT = 48 * 1024 * 1024


def _dot(a, b, **kw):
    return jnp.dot(a, b, preferred_element_type=F32, **kw)


def _dot_nt(a, b):
    return lax.dot_general(a, b, (((1,), (1,)), ((), ())), preferred_element_type=F32)


def _dot_tn(a, b):
    return lax.dot_general(a, b, (((0,), (0,)), ((), ())), preferred_element_type=F32)


def _rms(x, g):
    return x * lax.rsqrt(jnp.mean(x * x, axis=-1, keepdims=True) + EPS) * g


def _silu(x):
    return x * jax.nn.sigmoid(x)


def _norm_proj_kernel(x_ref, g_ref, w_ref, o_ref):
    xn = _rms(x_ref[...], g_ref[...]).astype(BF16)
    o_ref[...] = _dot(xn, w_ref[...])


def _norm_proj(x2d, gain, w_all):
    n = x2d.shape[0]
    tm = min(512, n)
    return pl.pallas_call(
        _norm_proj_kernel,
        grid=(n // tm, PROJ_W // PROJ_TN),
        in_specs=[
            pl.BlockSpec((tm, D_MODEL), lambda i, j: (i, 0)),
            pl.BlockSpec((1, D_MODEL), lambda i, j: (0, 0)),
            pl.BlockSpec((D_MODEL, PROJ_TN), lambda i, j: (0, j)),
        ],
        out_specs=pl.BlockSpec((tm, PROJ_TN), lambda i, j: (i, j)),
        out_shape=jax.ShapeDtypeStruct((n, PROJ_W), F32),
        compiler_params=pltpu.CompilerParams(
            dimension_semantics=("parallel", "parallel"), vmem_limit_bytes=VMEM_LIMIT),
        name="norm_in_proj",
    )(x2d, gain, w_all)


def _mixer_kernel(proj_ref, cos_ref, sin_ref, w2_ref, gb_ref, ggain_ref, rgain_ref, rbias_ref,
                  s0_ref, r0_ref, o_ref, s_out_ref, r_out_ref, s_scr, r_scr, *, n_invalid, rows_blk):
    j = pl.program_id(1)

    @pl.when(j == 0)
    def _():
        s_scr[...] = s0_ref[...]
        r_scr[...] = r0_ref[...]

    row = lax.broadcasted_iota(jnp.int32, (CHUNK, CHUNK), 0)
    col = lax.broadcasted_iota(jnp.int32, (CHUNK, CHUNK), 1)
    causal = row >= col
    tril = causal.astype(F32)
    diff = jnp.maximum(row - col, 0).astype(F32)
    rowf = lax.broadcasted_iota(jnp.int32, (CHUNK, LANES), 0).astype(F32)
    row1 = lax.broadcasted_iota(jnp.int32, (CHUNK, 1), 0)

    def chunk_body(c, carry):
        r0 = pl.multiple_of(c * CHUNK, CHUNK)
        rows = pl.ds(r0, CHUNK)
        valid = ((j * rows_blk + r0 + row1) >= n_invalid).astype(F32)

        ga = proj_ref[rows, OFF_GA:OFF_GA + LANES].astype(BF16)
        pre = _dot(ga, w2_ref[...]) + gb_ref[...]
        log_a = jax.nn.log_sigmoid(pre) * (1.0 / GATE_TAU)
        bcum = _dot(tril, log_a, precision=lax.Precision.HIGHEST)
        b_last = bcum[CHUNK - 1:CHUNK, :]
        gq = proj_ref[rows, OFF_GQ:OFF_GQ + HW] * (DK ** -0.5)
        gk = proj_ref[rows, OFF_GK:OFF_GK + HW] * valid
        q_t = (gq * jnp.exp(bcum)).astype(BF16)
        k_t = (gk * jnp.exp(-bcum)).astype(BF16)
        k_end = (gk * jnp.exp(b_last - bcum)).astype(BF16)
        decay = jnp.exp(b_last)
        for h in range(HEADS):
            sl = slice(h * LANES, (h + 1) * LANES)
            v = proj_ref[rows, OFF_GV + h * DV:OFF_GV + (h + 1) * DV].astype(BF16)
            att = jnp.where(causal, _dot_nt(q_t[:, sl], k_t[:, sl]), 0.0).astype(BF16)
            s_prev = s_scr[h]
            o = _dot(att, v) + _dot_nt(q_t[:, sl], s_prev.astype(BF16))
            s_scr[h] = s_prev * decay[:, sl] + _dot_tn(v, k_end[:, sl])
            o = o * lax.rsqrt(jnp.mean(o * o, axis=-1, keepdims=True) + EPS) * ggain_ref[...]
            gate = proj_ref[rows, OFF_GG + h * DV:OFF_GG + (h + 1) * DV]
            o_ref[rows, h * DV:(h + 1) * DV] = (o * _silu(gate)).astype(o_ref.dtype)

        cosb = cos_ref[rows, :]
        sinb = sin_ref[rows, :]
        rq = proj_ref[rows, OFF_RQ:OFF_RQ + HW]
        rqr = proj_ref[rows, OFF_RQR:OFF_RQR + HW]
        rk = proj_ref[rows, OFF_RK:OFF_RK + HW] * valid
        rkr = proj_ref[rows, OFF_RKR:OFF_RKR + HW] * valid
        q_r = (rq * cosb + rqr * sinb).astype(BF16)
        k_r = (rk * cosb + rkr * sinb) * (DK ** -0.5)
        for h in range(HEADS):
            sl = slice(h * LANES, (h + 1) * LANES)
            lg = LOG_GAMMA[h]
            dmat = jnp.where(causal, jnp.exp(lg * diff), 0.0)
            xi = jnp.exp(lg * (rowf + 1.0))
            zeta = jnp.exp(lg * (CHUNK - 1.0 - rowf))
            v = proj_ref[rows, OFF_RV + h * DV:OFF_RV + (h + 1) * DV].astype(BF16)
            k_h = k_r[:, sl]
            att = (_dot_nt(q_r[:, sl], k_h.astype(BF16)) * dmat).astype(BF16)
            r_prev = r_scr[h]
            o = _dot(att, v) + _dot_nt(q_r[:, sl], r_prev.astype(BF16)) * xi
            r_scr[h] = r_prev * math.exp(lg * CHUNK) + _dot_tn(v, (k_h * zeta).astype(BF16))
            mu = jnp.mean(o, axis=-1, keepdims=True)
            var = jnp.mean(jnp.square(o - mu), axis=-1, keepdims=True)
            o = (o - mu) * lax.rsqrt(var + EPS) * rgain_ref[...] + rbias_ref[...]
            gate = proj_ref[rows, OFF_RG + h * DV:OFF_RG + (h + 1) * DV]
            o_ref[rows, HW + h * DV:HW + (h + 1) * DV] = (o * _silu(gate)).astype(o_ref.dtype)
        return carry

    lax.fori_loop(0, rows_blk // CHUNK, chunk_body, 0)
    s_out_ref[...] = s_scr[...]
    r_out_ref[...] = r_scr[...]


def _mixer(proj, cosf, sinf, w2p, gbp, ggain, rgain, rbias, s0, r0, *, batch, n_invalid):
    rows = proj.shape[0] // batch
    rows_blk = min(256, rows)
    nblk = rows // rows_blk
    const2 = lambda b, j: (0, 0)
    const3 = lambda b, j: (0, 0, 0)
    state = jax.ShapeDtypeStruct((HEADS, DV, LANES), F32)
    return pl.pallas_call(
        functools.partial(_mixer_kernel, n_invalid=n_invalid, rows_blk=rows_blk),
        grid=(batch, nblk),
        in_specs=[
            pl.BlockSpec((rows_blk, PROJ_W), lambda b, j: (b * nblk + j, 0)),
            pl.BlockSpec((rows_blk, HW), lambda b, j: (j, 0)),
            pl.BlockSpec((rows_blk, HW), lambda b, j: (j, 0)),
            pl.BlockSpec((LANES, HW), const2),
            pl.BlockSpec((1, HW), const2),
            pl.BlockSpec((1, DV), const2),
            pl.BlockSpec((1, DV), const2),
            pl.BlockSpec((1, DV), const2),
            pl.BlockSpec((HEADS, DV, LANES), const3),
            pl.BlockSpec((HEADS, DV, LANES), const3),
        ],
        out_specs=[
            pl.BlockSpec((rows_blk, D_MODEL), lambda b, j: (b * nblk + j, 0)),
            pl.BlockSpec((HEADS, DV, LANES), const3),
            pl.BlockSpec((HEADS, DV, LANES), const3),
        ],
        out_shape=[jax.ShapeDtypeStruct((batch * rows, D_MODEL), BF16), state, state],
        scratch_shapes=[pltpu.VMEM((HEADS, DV, LANES), F32), pltpu.VMEM((HEADS, DV, LANES), F32)],
        compiler_params=pltpu.CompilerParams(
            dimension_semantics=("arbitrary", "arbitrary"), vmem_limit_bytes=VMEM_LIMIT),
        name="token_mixers",
    )(proj, cosf, sinf, w2p, gbp, ggain, rgain, rbias, s0, r0)


def _out_proj_kernel(o_ref, x_ref, w_ref, g_ref, h_ref, xn_ref):
    h = x_ref[...] + _dot(o_ref[...], w_ref[...])
    h_ref[...] = h
    xn_ref[...] = _rms(h, g_ref[...])


def _out_proj(o, x2d, w_out, gain2):
    n = x2d.shape[0]
    tm = min(512, n)
    blk = pl.BlockSpec((tm, D_MODEL), lambda i: (i, 0))
    return pl.pallas_call(
        _out_proj_kernel,
        grid=(n // tm,),
        in_specs=[blk, blk, pl.BlockSpec((D_MODEL, D_MODEL), lambda i: (0, 0)),
                  pl.BlockSpec((1, D_MODEL), lambda i: (0, 0))],
        out_specs=[blk, blk],
        out_shape=[jax.ShapeDtypeStruct((n, D_MODEL), F32)] * 2,
        compiler_params=pltpu.CompilerParams(
            dimension_semantics=("parallel",), vmem_limit_bytes=VMEM_LIMIT),
        name="out_proj_norm",
    )(o, x2d, w_out, gain2)


def _top_k_rows(s, payload=None):
    n = s.shape[0]
    rowi = lax.broadcasted_iota(jnp.int32, s.shape, 0)
    vals, picks = [], []
    for _ in range(PEER_TOPK):
        m = jnp.max(s, axis=0, keepdims=True)
        am = jnp.min(jnp.where(s == m, rowi, n), axis=0, keepdims=True)
        sel = rowi == am
        vals.append(m)
        if payload is None:
            picks.append(am)
        else:
            picks.append(jnp.min(jnp.where(sel, payload, jnp.iinfo(jnp.int32).max), axis=0, keepdims=True))
        s = jnp.where(sel, -jnp.inf, s)
    return jnp.concatenate(vals, axis=0), jnp.concatenate(picks, axis=0)


def _route_kernel(xn_ref, wq_ref, sk_ref, idx_ref, g_ref, q_scr, i_scr, g_scr):
    tr = xn_ref.shape[0]
    q = _dot(xn_ref[...].astype(BF16), wq_ref[...])
    for hc in range(2 * PEER_HEADS):
        q_scr[hc] = q[:, hc * LANES:(hc + 1) * LANES].astype(BF16)

    def sub_body(sb, carry):
        rows = pl.ds(pl.multiple_of(sb * LANES, LANES), LANES)

        def head_body(h, c2):
            s0 = _dot_nt(sk_ref[2 * h], q_scr[2 * h, rows, :])
            s1 = _dot_nt(sk_ref[2 * h + 1], q_scr[2 * h + 1, rows, :])
            v0, i0 = _top_k_rows(s0)
            v1, i1 = _top_k_rows(s1)
            cand = jnp.concatenate([v0[a:a + 1] + v1 for a in range(PEER_TOPK)], axis=0)
            cidx = jnp.concatenate([i0[a:a + 1] * PEER_NKEYS + i1 for a in range(PEER_TOPK)], axis=0)
            best, e = _top_k_rows(cand, cidx)
            ex = jnp.exp(best - best[0:1])
            i_scr[h] = e
            g_scr[h] = ex / jnp.sum(ex, axis=0, keepdims=True)
            return c2

        lax.fori_loop(0, PEER_HEADS, head_body, 0)
        idx_ref[rows, :] = i_scr[...].reshape(PEER_SEL, LANES).T
        g_ref[rows, :] = g_scr[...].reshape(PEER_SEL, LANES).T
        return carry

    lax.fori_loop(0, tr // LANES, sub_body, 0)


def _route(xn2, w_q, subkeys):
    n = xn2.shape[0]
    tr = min(512, n)
    qw = 2 * PEER_HEADS * LANES
    return pl.pallas_call(
        _route_kernel,
        grid=(n // tr,),
        in_specs=[pl.BlockSpec((tr, D_MODEL), lambda i: (i, 0)),
                  pl.BlockSpec((D_MODEL, qw), lambda i: (0, 0)),
                  pl.BlockSpec((2 * PEER_HEADS, PEER_NKEYS, LANES), lambda i: (0, 0, 0))],
        out_specs=[pl.BlockSpec((tr, PEER_SEL), lambda i: (i, 0))] * 2,
        out_shape=[jax.ShapeDtypeStruct((n, PEER_SEL), jnp.int32),
                   jax.ShapeDtypeStruct((n, PEER_SEL), F32)],
        scratch_shapes=[pltpu.VMEM((2 * PEER_HEADS, tr, LANES), BF16),
                        pltpu.VMEM((PEER_HEADS, PEER_TOPK, LANES), jnp.int32),
                        pltpu.VMEM((PEER_HEADS, PEER_TOPK, LANES), F32)],
        compiler_params=pltpu.CompilerParams(
            dimension_semantics=("parallel",), vmem_limit_bytes=VMEM_LIMIT),
        name="peer_route",
    )(xn2, w_q, subkeys)


def _sc_worker_id():
    return lax.axis_index("s") * SC_CORES + lax.axis_index("c")


def _sc_mesh():
    return plsc.VectorSubcoreMesh(core_axis_name="c", subcore_axis_name="s")


def _sc_gather_ring(table_hbm, idx_v, bufs, sems, unit, compute):
    def gather(g, k):
        off = pl.multiple_of(g * SC_ROWS, SC_ROWS)
        return pltpu.make_async_copy(table_hbm.at[idx_v.at[pl.ds(off, SC_ROWS)]], bufs[k], sems[k])

    for k in range(SC_NBUF):
        gather(k, k).start()

    units_per_iter = SC_NBUF // unit

    def ring_iter(i, carry):
        for m in range(units_per_iter):
            ks = range(m * unit, (m + 1) * unit)
            for k in ks:
                gather(i * SC_NBUF + k, k).wait()
            compute(i * units_per_iter + m, [bufs[k] for k in ks])
            for k in ks:
                g_next = (i + 1) * SC_NBUF + k

                @pl.when(g_next < SC_GATHERS)
                def _():
                    gather(g_next, k).start()
        return carry

    lax.fori_loop(0, SC_GATHERS // SC_NBUF, ring_iter, 0)


def _sc_scratch(staged_cols, out_cols):
    return ([pltpu.VMEM((SC_TOK_BATCH * PEER_SEL,), jnp.int32),
             pltpu.VMEM((SC_TOK_BATCH, staged_cols), F32),
             pltpu.VMEM((SC_TOK_BATCH, out_cols), F32)]
            + [pltpu.VMEM((SC_ROWS, ROW_TILES, LANES), F32) for _ in range(SC_NBUF)]
            + [pltpu.SemaphoreType.DMA for _ in range(SC_NBUF)])


def _sc_batches(tpw, idx_hbm, staged_hbm, out_hbm, idx_v, staged_v, out_v, run):
    tok0 = _sc_worker_id() * tpw

    def batch(tb, carry):
        t0 = pl.multiple_of(tok0 + tb * SC_TOK_BATCH, SC_TOK_BATCH)
        pltpu.sync_copy(idx_hbm.at[pl.ds(pl.multiple_of(t0 * PEER_SEL, SC_TOK_BATCH * PEER_SEL),
                                         SC_TOK_BATCH * PEER_SEL)], idx_v)
        pltpu.sync_copy(staged_hbm.at[pl.ds(t0, SC_TOK_BATCH)], staged_v)
        run()
        pltpu.sync_copy(out_v, out_hbm.at[pl.ds(t0, SC_TOK_BATCH)])
        return carry

    lax.fori_loop(0, tpw // SC_TOK_BATCH, batch, 0)


def _peer_dot_sc(u3, idx_flat, xn2):
    t = xn2.shape[0]
    unit = 16 // SC_ROWS
    units_per_tok = PEER_SEL // 16

    @functools.partial(
        pl.kernel,
        mesh=_sc_mesh(),
        out_type=jax.ShapeDtypeStruct((t, PEER_SEL * SC_LANES), F32),
        scratch_types=_sc_scratch(D_MODEL, PEER_SEL * SC_LANES),
        compiler_params=pltpu.CompilerParams(needs_layout_passes=False),
        name="peer_dot",
    )
    def body(u_hbm, idx_hbm, x_hbm, out_hbm, idx_v, x_v, out_v, *rest):
        bufs, sems = rest[:SC_NBUF], rest[SC_NBUF:]

        def compute(u, ub):
            tl = u // units_per_tok
            rb = (u % units_per_tok) * 16

            def cbody(c8, acc):
                acc = list(acc)
                base = pl.multiple_of(c8 * LANES, LANES)
                for l in range(LANES // SC_LANES):
                    xv = x_v[tl, pl.ds(base + l * SC_LANES, SC_LANES)]
                    for q, buf in enumerate(ub):
                        for r in range(SC_ROWS):
                            a = q * SC_ROWS + r
                            acc[a] = acc[a] + buf[r, c8, pl.ds(l * SC_LANES, SC_LANES)] * xv
                return tuple(acc)

            acc = lax.fori_loop(0, ROW_TILES, cbody, tuple(jnp.zeros((SC_LANES,), F32) for _ in range(16)))
            for a in range(16):
                o = pl.multiple_of((rb + a) * SC_LANES, SC_LANES)
                out_v[tl, pl.ds(o, SC_LANES)] = acc[a]

        _sc_batches(t // SC_WORKERS, idx_hbm, x_hbm, out_hbm, idx_v, x_v, out_v,
                    lambda: _sc_gather_ring(u_hbm, idx_v, bufs, sems, unit, compute))

    return body(u3, idx_flat, xn2)


def _peer_combine_sc(v3, idx_flat, coef_b):
    t = coef_b.shape[0]
    unit = 32 // SC_ROWS
    units_per_tok = PEER_SEL // 32
    group = 2 * LANES

    @functools.partial(
        pl.kernel,
        mesh=_sc_mesh(),
        out_type=jax.ShapeDtypeStruct((t, D_MODEL), F32),
        scratch_types=_sc_scratch(PEER_SEL * SC_LANES, D_MODEL),
        compiler_params=pltpu.CompilerParams(needs_layout_passes=False),
        name="peer_combine",
    )
    def body(v_hbm, idx_hbm, coef_hbm, out_hbm, idx_v, coef_v, out_v, *rest):
        bufs, sems = rest[:SC_NBUF], rest[SC_NBUF:]

        def compute(u, ub):
            tl = u // units_per_tok
            rb = (u % units_per_tok) * 32
            for g in range(D_MODEL // group):
                cols = [pl.ds(g * group + k * SC_LANES, SC_LANES) for k in range(16)]
                acc = tuple(out_v[tl, c] for c in cols)
                for q, buf in enumerate(ub):
                    def rbody(r, acc, q=q, buf=buf):
                        co = pl.multiple_of((rb + q * SC_ROWS + r) * SC_LANES, SC_LANES)
                        cf = coef_v[tl, pl.ds(co, SC_LANES)]
                        return tuple(
                            acc[k] + buf[r, 2 * g + k // 8, pl.ds((k % 8) * SC_LANES, SC_LANES)] * cf
                            for k in range(16))

                    acc = lax.fori_loop(0, SC_ROWS, rbody, acc)
                for k in range(16):
                    out_v[tl, cols[k]] = acc[k]

        def run():
            def zero(tl, c2):
                for k in range(D_MODEL // SC_LANES):
                    out_v[tl, pl.ds(k * SC_LANES, SC_LANES)] = jnp.zeros((SC_LANES,), F32)
                return c2

            lax.fori_loop(0, SC_TOK_BATCH, zero, 0)
            _sc_gather_ring(v_hbm, idx_v, bufs, sems, unit, compute)

        _sc_batches(t // SC_WORKERS, idx_hbm, coef_hbm, out_hbm, idx_v, coef_v, out_v, run)

    return body(v3, idx_flat, coef_b)


def _coef_kernel(part_ref, g_ref, e_ref, coef_ref):
    hp = lax.Precision.HIGHEST
    e = e_ref[...]
    act = lax.dot_general(part_ref[...], e, (((1,), (1,)), ((), ())),
                          precision=hp, preferred_element_type=F32)
    gelu = 0.5 * act * (1.0 + lax.erf(act * np.float32(math.sqrt(0.5))))
    coef_ref[...] = _dot(g_ref[...] * gelu, e, precision=hp)


def _coef(part, g):
    n = part.shape[0]
    tr = min(512, n)
    w = PEER_SEL * SC_LANES
    expand = (jnp.arange(w, dtype=jnp.int32)[None, :] // SC_LANES
              == jnp.arange(PEER_SEL, dtype=jnp.int32)[:, None]).astype(F32)
    return pl.pallas_call(
        _coef_kernel,
        grid=(n // tr,),
        in_specs=[pl.BlockSpec((tr, w), lambda i: (i, 0)),
                  pl.BlockSpec((tr, PEER_SEL), lambda i: (i, 0)),
                  pl.BlockSpec((PEER_SEL, w), lambda i: (0, 0))],
        out_specs=pl.BlockSpec((tr, w), lambda i: (i, 0)),
        out_shape=jax.ShapeDtypeStruct((n, w), F32),
        compiler_params=pltpu.CompilerParams(
            dimension_semantics=("parallel",), vmem_limit_bytes=VMEM_LIMIT),
        name="peer_coef",
    )(part, g, expand)


def _final_kernel(h_ref, p_ref, g_ref, o_ref):
    o_ref[...] = _rms(h_ref[...] + p_ref[...], g_ref[...])


def _final(h2, peer_out, gain):
    n = h2.shape[0]
    tm = min(1024, n)
    blk = pl.BlockSpec((tm, D_MODEL), lambda i: (i, 0))
    return pl.pallas_call(
        _final_kernel,
        grid=(n // tm,),
        in_specs=[blk, blk, pl.BlockSpec((1, D_MODEL), lambda i: (0, 0))],
        out_specs=blk,
        out_shape=jax.ShapeDtypeStruct((n, D_MODEL), F32),
        compiler_params=pltpu.CompilerParams(
            dimension_semantics=("parallel",), vmem_limit_bytes=VMEM_LIMIT),
        name="final_norm",
    )(h2, peer_out, gain)


def _pad_heads(w, dk):
    lead = w.shape[:-1]
    w = w.reshape(lead + (HEADS, dk))
    w = jnp.pad(w, [(0, 0)] * len(lead) + [(0, 0), (0, LANES - dk)])
    return w.reshape(lead + (HEADS * LANES,))


def _rot_heads(w, dk):
    lead = w.shape[:-1]
    w = w.reshape(lead + (HEADS, 2, dk // 2))
    w = jnp.stack([-w[..., 1, :], w[..., 0, :]], axis=-2)
    return w.reshape(lead + (HEADS * dk,))


def _pack_in_weights(w_in):
    gq, gk, gv, gg, ga, rq, rk, rv, rg = jnp.split(w_in, np.cumsum(IN_SPLITS)[:-1].tolist(), axis=-1)
    ga = jnp.pad(ga, ((0, 0), (0, 2 * LANES - GATE_RANK)))
    return jnp.concatenate(
        [_pad_heads(gq, DK), _pad_heads(gk, DK), gv, gg,
         _pad_heads(rq, DK), _pad_heads(rk, DK), rv, rg,
         _pad_heads(_rot_heads(rq, DK), DK), _pad_heads(_rot_heads(rk, DK), DK), ga],
        axis=-1).astype(BF16)


def _rope_tables(pos):
    inv_freq = ROPE_BASE ** (-jnp.arange(0, DK, 2, dtype=F32) / DK)
    ang = pos[:, None] * inv_freq[None, :]
    full = lambda t: _pad_heads(jnp.tile(t, (1, 2 * HEADS)), DK)
    return full(jnp.cos(ang)), full(jnp.sin(ang))


def kernel(x, meta_tokens, norm1_gain, w_in, gla_gate_w2, gla_gate_b, gla_norm_gain, ret_norm_gain, ret_norm_bias, w_out, norm2_gain, peer_w_q, peer_subkeys, peer_u, peer_v, final_gain):
    b, s, d = x.shape
    assert d == D_MODEL and s % 256 == 0 and (b * s) % (SC_WORKERS * SC_TOK_BATCH) == 0
    assert norm1_gain.shape[0] == 1, "single-layer block"
    x2d = x.reshape(b * s, d)

    w_all = _pack_in_weights(w_in[0])
    w2p = jnp.pad(_pad_heads(gla_gate_w2[0], DK), ((0, LANES - GATE_RANK), (0, 0))).astype(BF16)
    gbp = _pad_heads(gla_gate_b[0][None, :], DK)
    g1 = norm1_gain[0][None, :]
    ggain = gla_norm_gain[0][None, :]
    rgain = ret_norm_gain[0][None, :]
    rbias = ret_norm_bias[0][None, :]

    h0 = jnp.concatenate([jnp.zeros((N_PAD, d), x.dtype), meta_tokens.astype(x.dtype)], axis=0)
    cos0, sin0 = _rope_tables(jnp.arange(CHUNK, dtype=F32) - N_PAD)
    cos1, sin1 = _rope_tables(jnp.arange(s, dtype=F32) + N_META)
    zero_state = jnp.zeros((HEADS, DV, LANES), F32)
    _, s_meta, r_meta = _mixer(_norm_proj(h0, g1, w_all), cos0, sin0, w2p, gbp, ggain, rgain, rbias,
                               zero_state, zero_state, batch=1, n_invalid=N_PAD)
    o, _, _ = _mixer(_norm_proj(x2d, g1, w_all), cos1, sin1, w2p, gbp, ggain, rgain, rbias,
                     s_meta, r_meta, batch=b, n_invalid=0)

    h2, xn2 = _out_proj(o, x2d, w_out[0].astype(BF16), norm2_gain[0][None, :])

    subkeys = peer_subkeys[0].reshape(2 * PEER_HEADS, PEER_NKEYS, PEER_DKEY // 2).astype(BF16)
    idx, g = _route(xn2, peer_w_q[0].astype(BF16), subkeys)
    idx_flat = idx.reshape(-1)
    n_exp = peer_u.shape[1]
    part = _peer_dot_sc(peer_u[0].reshape(n_exp, ROW_TILES, LANES), idx_flat, xn2)
    coef_b = _coef(part, g)
    peer_out = _peer_combine_sc(peer_v[0].reshape(n_exp, ROW_TILES, LANES), idx_flat, coef_b)

    out = _final(h2, peer_out, final_gain[None, :])
    return out.reshape(b, s, d)
```

```python
import functools
import math

import jax
import jax.numpy as jnp
import numpy as np
from jax import lax
from jax.experimental import pallas as pl
from jax.experimental.pallas import tpu as pltpu
from jax.experimental.pallas import tpu_sc as plsc

F32 = jnp.float32
BF16 = jnp.bfloat16

D_MODEL = 1024
N_META = 16
CHUNK = 64
N_PAD = CHUNK - N_META
EPS = 1e-6

HEADS = 4
DV = 128
DK = 64
LANES = 128
GATE_RANK = 16
GATE_TAU = 16.0
ROPE_BASE = 10000.0
IN_SPLITS = (HEADS * DK, HEADS * DK, HEADS * DV, HEADS * DV, GATE_RANK,
             HEADS * DK, HEADS * DK, HEADS * DV, HEADS * DV)

PEER_HEADS = 8
PEER_NKEYS = 128
PEER_DKEY = 256
PEER_TOPK = 16
PEER_SEL = PEER_HEADS * PEER_TOPK

HW = HEADS * LANES
OFF_GQ, OFF_GK, OFF_GV, OFF_GG = 0, HW, 2 * HW, 3 * HW
OFF_RQ, OFF_RK, OFF_RV, OFF_RG = 4 * HW, 5 * HW, 6 * HW, 7 * HW
OFF_RQR, OFF_RKR, OFF_GA = 8 * HW, 9 * HW, 10 * HW
PROJ_W = 10 * HW + 2 * LANES
PROJ_TN = PROJ_W // 3

LOG_GAMMA = tuple(math.log(1.0 - 2.0 ** (-5.0 - h)) for h in range(HEADS))

SC_CORES, SC_SUBCORES, SC_LANES = 2, 16, 16
SC_WORKERS = SC_CORES * SC_SUBCORES
SC_TOK_BATCH = 16
SC_ROWS = 8
SC_NBUF = 8
SC_GATHERS = SC_TOK_BATCH * PEER_SEL // SC_ROWS
ROW_TILES = D_MODEL // LANES

SEGS_PER_SEQ = 2

VMEM_LIMIT = 48 * 1024 * 1024


def _dot(a, b, **kw):
    return jnp.dot(a, b, preferred_element_type=F32, **kw)


def _dot_nt(a, b):
    return lax.dot_general(a, b, (((1,), (1,)), ((), ())), preferred_element_type=F32)


def _dot_tn(a, b):
    return lax.dot_general(a, b, (((0,), (0,)), ((), ())), preferred_element_type=F32)


def _rms(x, g):
    return x * lax.rsqrt(jnp.mean(x * x, axis=-1, keepdims=True) + EPS) * g


def _silu(x):
    return x * jax.nn.sigmoid(x)


def _norm_proj_kernel(x_ref, g_ref, w_ref, o_ref):
    xn = _rms(x_ref[...], g_ref[...]).astype(BF16)
    o_ref[...] = _dot(xn, w_ref[...])


def _norm_proj(x2d, gain, w_all):
    n = x2d.shape[0]
    tm = min(512, n)
    return pl.pallas_call(
        _norm_proj_kernel,
        grid=(n // tm, PROJ_W // PROJ_TN),
        in_specs=[
            pl.BlockSpec((tm, D_MODEL), lambda i, j: (i, 0)),
            pl.BlockSpec((1, D_MODEL), lambda i, j: (0, 0)),
            pl.BlockSpec((D_MODEL, PROJ_TN), lambda i, j: (0, j)),
        ],
        out_specs=pl.BlockSpec((tm, PROJ_TN), lambda i, j: (i, j)),
        out_shape=jax.ShapeDtypeStruct((n, PROJ_W), F32),
        compiler_params=pltpu.CompilerParams(
            dimension_semantics=("parallel", "parallel"), vmem_limit_bytes=VMEM_LIMIT),
        name="norm_in_proj",
    )(x2d, gain, w_all)


def _mixer_kernel(proj_ref, cos_ref, sin_ref, w2_ref, gb_ref, ggain_ref, rgain_ref, rbias_ref,
                  s0_ref, r0_ref, o_ref, s_out_ref, r_out_ref, s_scr, r_scr, *, n_invalid, rows_blk):
    j = pl.program_id(1)

    @pl.when(j == 0)
    def _():
        s_scr[...] = s0_ref[...]
        r_scr[...] = r0_ref[...]

    row = lax.broadcasted_iota(jnp.int32, (CHUNK, CHUNK), 0)
    col = lax.broadcasted_iota(jnp.int32, (CHUNK, CHUNK), 1)
    causal = row >= col
    tril = causal.astype(F32)
    diff = jnp.maximum(row - col, 0).astype(F32)
    rowf = lax.broadcasted_iota(jnp.int32, (CHUNK, LANES), 0).astype(F32)
    row1 = lax.broadcasted_iota(jnp.int32, (CHUNK, 1), 0)

    def chunk_body(c, carry):
        r0 = pl.multiple_of(c * CHUNK, CHUNK)
        rows = pl.ds(r0, CHUNK)
        valid = ((j * rows_blk + r0 + row1) >= n_invalid).astype(F32)

        ga = proj_ref[rows, OFF_GA:OFF_GA + LANES].astype(BF16)
        pre = _dot(ga, w2_ref[...]) + gb_ref[...]
        log_a = jax.nn.log_sigmoid(pre) * (1.0 / GATE_TAU)
        bcum = _dot(tril, log_a, precision=lax.Precision.HIGHEST)
        b_last = bcum[CHUNK - 1:CHUNK, :]
        gq = proj_ref[rows, OFF_GQ:OFF_GQ + HW] * (DK ** -0.5)
        gk = proj_ref[rows, OFF_GK:OFF_GK + HW] * valid
        q_t = (gq * jnp.exp(bcum)).astype(BF16)
        k_t = (gk * jnp.exp(-bcum)).astype(BF16)
        k_end = (gk * jnp.exp(b_last - bcum)).astype(BF16)
        decay = jnp.exp(b_last)
        for h in range(HEADS):
            sl = slice(h * LANES, (h + 1) * LANES)
            v = proj_ref[rows, OFF_GV + h * DV:OFF_GV + (h + 1) * DV].astype(BF16)
            att = jnp.where(causal, _dot_nt(q_t[:, sl], k_t[:, sl]), 0.0).astype(BF16)
            s_prev = s_scr[h]
            o = _dot(att, v) + _dot_nt(q_t[:, sl], s_prev.astype(BF16))
            s_scr[h] = s_prev * decay[:, sl] + _dot_tn(v, k_end[:, sl])
            o = o * lax.rsqrt(jnp.mean(o * o, axis=-1, keepdims=True) + EPS) * ggain_ref[...]
            gate = proj_ref[rows, OFF_GG + h * DV:OFF_GG + (h + 1) * DV]
            o_ref[rows, h * DV:(h + 1) * DV] = (o * _silu(gate)).astype(o_ref.dtype)

        cosb = cos_ref[rows, :]
        sinb = sin_ref[rows, :]
        rq = proj_ref[rows, OFF_RQ:OFF_RQ + HW]
        rqr = proj_ref[rows, OFF_RQR:OFF_RQR + HW]
        rk = proj_ref[rows, OFF_RK:OFF_RK + HW] * valid
        rkr = proj_ref[rows, OFF_RKR:OFF_RKR + HW] * valid
        q_r = (rq * cosb + rqr * sinb).astype(BF16)
        k_r = (rk * cosb + rkr * sinb) * (DK ** -0.5)
        for h in range(HEADS):
            sl = slice(h * LANES, (h + 1) * LANES)
            lg = LOG_GAMMA[h]
            dmat = jnp.where(causal, jnp.exp(lg * diff), 0.0)
            xi = jnp.exp(lg * (rowf + 1.0))
            zeta = jnp.exp(lg * (CHUNK - 1.0 - rowf))
            v = proj_ref[rows, OFF_RV + h * DV:OFF_RV + (h + 1) * DV].astype(BF16)
            k_h = k_r[:, sl]
            att = (_dot_nt(q_r[:, sl], k_h.astype(BF16)) * dmat).astype(BF16)
            r_prev = r_scr[h]
            o = _dot(att, v) + _dot_nt(q_r[:, sl], r_prev.astype(BF16)) * xi
            r_scr[h] = r_prev * math.exp(lg * CHUNK) + _dot_tn(v, (k_h * zeta).astype(BF16))
            mu = jnp.mean(o, axis=-1, keepdims=True)
            var = jnp.mean(jnp.square(o - mu), axis=-1, keepdims=True)
            o = (o - mu) * lax.rsqrt(var + EPS) * rgain_ref[...] + rbias_ref[...]
            gate = proj_ref[rows, OFF_RG + h * DV:OFF_RG + (h + 1) * DV]
            o_ref[rows, HW + h * DV:HW + (h + 1) * DV] = (o * _silu(gate)).astype(o_ref.dtype)
        return carry

    lax.fori_loop(0, rows_blk // CHUNK, chunk_body, 0)
    s_out_ref[...] = s_scr[...]
    r_out_ref[...] = r_scr[...]


def _mixer(proj, cosf, sinf, w2p, gbp, ggain, rgain, rbias, s0, r0, *, batch, n_invalid):
    rows = proj.shape[0] // batch
    rows_blk = min(256, rows)
    nblk = rows // rows_blk
    const2 = lambda b, j: (0, 0)
    const3 = lambda b, j: (0, 0, 0)
    state = jax.ShapeDtypeStruct((HEADS, DV, LANES), F32)
    return pl.pallas_call(
        functools.partial(_mixer_kernel, n_invalid=n_invalid, rows_blk=rows_blk),
        grid=(batch, nblk),
        in_specs=[
            pl.BlockSpec((rows_blk, PROJ_W), lambda b, j: (b * nblk + j, 0)),
            pl.BlockSpec((rows_blk, HW), lambda b, j: (j, 0)),
            pl.BlockSpec((rows_blk, HW), lambda b, j: (j, 0)),
            pl.BlockSpec((LANES, HW), const2),
            pl.BlockSpec((1, HW), const2),
            pl.BlockSpec((1, DV), const2),
            pl.BlockSpec((1, DV), const2),
            pl.BlockSpec((1, DV), const2),
            pl.BlockSpec((HEADS, DV, LANES), const3),
            pl.BlockSpec((HEADS, DV, LANES), const3),
        ],
        out_specs=[
            pl.BlockSpec((rows_blk, D_MODEL), lambda b, j: (b * nblk + j, 0)),
            pl.BlockSpec((HEADS, DV, LANES), const3),
            pl.BlockSpec((HEADS, DV, LANES), const3),
        ],
        out_shape=[jax.ShapeDtypeStruct((batch * rows, D_MODEL), BF16), state, state],
        scratch_shapes=[pltpu.VMEM((HEADS, DV, LANES), F32), pltpu.VMEM((HEADS, DV, LANES), F32)],
        compiler_params=pltpu.CompilerParams(
            dimension_semantics=("arbitrary", "arbitrary"), vmem_limit_bytes=VMEM_LIMIT),
        name="token_mixers",
    )(proj, cosf, sinf, w2p, gbp, ggain, rgain, rbias, s0, r0)


def _out_proj_kernel(o_ref, x_ref, w_ref, g_ref, h_ref, xn_ref):
    h = x_ref[...] + _dot(o_ref[...], w_ref[...])
    h_ref[...] = h
    xn_ref[...] = _rms(h, g_ref[...])


def _out_proj(o, x2d, w_out, gain2):
    n = x2d.shape[0]
    tm = min(512, n)
    blk = pl.BlockSpec((tm, D_MODEL), lambda i: (i, 0))
    return pl.pallas_call(
        _out_proj_kernel,
        grid=(n // tm,),
        in_specs=[blk, blk, pl.BlockSpec((D_MODEL, D_MODEL), lambda i: (0, 0)),
                  pl.BlockSpec((1, D_MODEL), lambda i: (0, 0))],
        out_specs=[blk, blk],
        out_shape=[jax.ShapeDtypeStruct((n, D_MODEL), F32)] * 2,
        compiler_params=pltpu.CompilerParams(
            dimension_semantics=("parallel",), vmem_limit_bytes=VMEM_LIMIT),
        name="out_proj_norm",
    )(o, x2d, w_out, gain2)


def _top_k_rows(s, payload=None):
    n = s.shape[0]
    rowi = lax.broadcasted_iota(jnp.int32, s.shape, 0)
    vals, picks = [], []
    for _ in range(PEER_TOPK):
        m = jnp.max(s, axis=0, keepdims=True)
        am = jnp.min(jnp.where(s == m, rowi, n), axis=0, keepdims=True)
        sel = rowi == am
        vals.append(m)
        if payload is None:
            picks.append(am)
        else:
            picks.append(jnp.min(jnp.where(sel, payload, jnp.iinfo(jnp.int32).max), axis=0, keepdims=True))
        s = jnp.where(sel, -jnp.inf, s)
    return jnp.concatenate(vals, axis=0), jnp.concatenate(picks, axis=0)


def _route_kernel(xn_ref, wq_ref, sk_ref, idx_ref, g_ref, q_scr, i_scr, g_scr):
    tr = xn_ref.shape[0]
    q = _dot(xn_ref[...].astype(BF16), wq_ref[...])
    for hc in range(2 * PEER_HEADS):
        q_scr[hc] = q[:, hc * LANES:(hc + 1) * LANES].astype(BF16)

    def sub_body(sb, carry):
        rows = pl.ds(pl.multiple_of(sb * LANES, LANES), LANES)

        def head_body(h, c2):
            s0 = _dot_nt(sk_ref[2 * h], q_scr[2 * h, rows, :])
            s1 = _dot_nt(sk_ref[2 * h + 1], q_scr[2 * h + 1, rows, :])
            v0, i0 = _top_k_rows(s0)
            v1, i1 = _top_k_rows(s1)
            cand = jnp.concatenate([v0[a:a + 1] + v1 for a in range(PEER_TOPK)], axis=0)
            cidx = jnp.concatenate([i0[a:a + 1] * PEER_NKEYS + i1 for a in range(PEER_TOPK)], axis=0)
            best, e = _top_k_rows(cand, cidx)
            ex = jnp.exp(best - best[0:1])
            i_scr[h] = e
            g_scr[h] = ex / jnp.sum(ex, axis=0, keepdims=True)
            return c2

        lax.fori_loop(0, PEER_HEADS, head_body, 0)
        idx_ref[rows, :] = i_scr[...].reshape(PEER_SEL, LANES).T
        g_ref[rows, :] = g_scr[...].reshape(PEER_SEL, LANES).T
        return carry

    lax.fori_loop(0, tr // LANES, sub_body, 0)


def _route(xn2, w_q, subkeys):
    n = xn2.shape[0]
    tr = min(512, n)
    qw = 2 * PEER_HEADS * LANES
    return pl.pallas_call(
        _route_kernel,
        grid=(n // tr,),
        in_specs=[pl.BlockSpec((tr, D_MODEL), lambda i: (i, 0)),
                  pl.BlockSpec((D_MODEL, qw), lambda i: (0, 0)),
                  pl.BlockSpec((2 * PEER_HEADS, PEER_NKEYS, LANES), lambda i: (0, 0, 0))],
        out_specs=[pl.BlockSpec((tr, PEER_SEL), lambda i: (i, 0))] * 2,
        out_shape=[jax.ShapeDtypeStruct((n, PEER_SEL), jnp.int32),
                   jax.ShapeDtypeStruct((n, PEER_SEL), F32)],
        scratch_shapes=[pltpu.VMEM((2 * PEER_HEADS, tr, LANES), BF16),
                        pltpu.VMEM((PEER_HEADS, PEER_TOPK, LANES), jnp.int32),
                        pltpu.VMEM((PEER_HEADS, PEER_TOPK, LANES), F32)],
        compiler_params=pltpu.CompilerParams(
            dimension_semantics=("parallel",), vmem_limit_bytes=VMEM_LIMIT),
        name="peer_route",
    )(xn2, w_q, subkeys)


def _sc_worker_id():
    return lax.axis_index("s") * SC_CORES + lax.axis_index("c")


def _sc_mesh():
    return plsc.VectorSubcoreMesh(core_axis_name="c", subcore_axis_name="s")


def _sc_gather_ring(table_hbm, idx_v, bufs, sems, unit, compute):
    def gather(g, k):
        off = pl.multiple_of(g * SC_ROWS, SC_ROWS)
        return pltpu.make_async_copy(table_hbm.at[idx_v.at[pl.ds(off, SC_ROWS)]], bufs[k], sems[k])

    for k in range(SC_NBUF):
        gather(k, k).start()

    units_per_iter = SC_NBUF // unit

    def ring_iter(i, carry):
        for m in range(units_per_iter):
            ks = range(m * unit, (m + 1) * unit)
            for k in ks:
                gather(i * SC_NBUF + k, k).wait()
            compute(i * units_per_iter + m, [bufs[k] for k in ks])
            for k in ks:
                g_next = (i + 1) * SC_NBUF + k

                @pl.when(g_next < SC_GATHERS)
                def _():
                    gather(g_next, k).start()
        return carry

    lax.fori_loop(0, SC_GATHERS // SC_NBUF, ring_iter, 0)


def _sc_scratch(staged_cols, out_cols):
    return ([pltpu.VMEM((SC_TOK_BATCH * PEER_SEL,), jnp.int32),
             pltpu.VMEM((SC_TOK_BATCH, staged_cols), F32),
             pltpu.VMEM((SC_TOK_BATCH, out_cols), F32)]
            + [pltpu.VMEM((SC_ROWS, ROW_TILES, LANES), F32) for _ in range(SC_NBUF)]
            + [pltpu.SemaphoreType.DMA for _ in range(SC_NBUF)])


def _sc_batches(tpw, idx_hbm, staged_hbm, out_hbm, idx_v, staged_v, out_v, run):
    tok0 = _sc_worker_id() * tpw

    def batch(tb, carry):
        t0 = pl.multiple_of(tok0 + tb * SC_TOK_BATCH, SC_TOK_BATCH)
        pltpu.sync_copy(idx_hbm.at[pl.ds(pl.multiple_of(t0 * PEER_SEL, SC_TOK_BATCH * PEER_SEL),
                                         SC_TOK_BATCH * PEER_SEL)], idx_v)
        pltpu.sync_copy(staged_hbm.at[pl.ds(t0, SC_TOK_BATCH)], staged_v)
        run()
        pltpu.sync_copy(out_v, out_hbm.at[pl.ds(t0, SC_TOK_BATCH)])
        return carry

    lax.fori_loop(0, tpw // SC_TOK_BATCH, batch, 0)


def _peer_dot_sc(u3, idx_flat, xn2):
    t = xn2.shape[0]
    unit = 16 // SC_ROWS
    units_per_tok = PEER_SEL // 16

    @functools.partial(
        pl.kernel,
        mesh=_sc_mesh(),
        out_type=jax.ShapeDtypeStruct((t, PEER_SEL * SC_LANES), F32),
        scratch_types=_sc_scratch(D_MODEL, PEER_SEL * SC_LANES),
        compiler_params=pltpu.CompilerParams(needs_layout_passes=False),
        name="peer_dot",
    )
    def body(u_hbm, idx_hbm, x_hbm, out_hbm, idx_v, x_v, out_v, *rest):
        bufs, sems = rest[:SC_NBUF], rest[SC_NBUF:]

        def compute(u, ub):
            tl = u // units_per_tok
            rb = (u % units_per_tok) * 16

            def cbody(c8, acc):
                acc = list(acc)
                base = pl.multiple_of(c8 * LANES, LANES)
                for l in range(LANES // SC_LANES):
                    xv = x_v[tl, pl.ds(base + l * SC_LANES, SC_LANES)]
                    for q, buf in enumerate(ub):
                        for r in range(SC_ROWS):
                            a = q * SC_ROWS + r
                            acc[a] = acc[a] + buf[r, c8, pl.ds(l * SC_LANES, SC_LANES)] * xv
                return tuple(acc)

            acc = lax.fori_loop(0, ROW_TILES, cbody, tuple(jnp.zeros((SC_LANES,), F32) for _ in range(16)))
            for a in range(16):
                o = pl.multiple_of((rb + a) * SC_LANES, SC_LANES)
                out_v[tl, pl.ds(o, SC_LANES)] = acc[a]

        _sc_batches(t // SC_WORKERS, idx_hbm, x_hbm, out_hbm, idx_v, x_v, out_v,
                    lambda: _sc_gather_ring(u_hbm, idx_v, bufs, sems, unit, compute))

    return body(u3, idx_flat, xn2)


def _peer_combine_sc(v3, idx_flat, coef_b):
    t = coef_b.shape[0]
    unit = 32 // SC_ROWS
    units_per_tok = PEER_SEL // 32
    group = 2 * LANES

    @functools.partial(
        pl.kernel,
        mesh=_sc_mesh(),
        out_type=jax.ShapeDtypeStruct((t, D_MODEL), F32),
        scratch_types=_sc_scratch(PEER_SEL * SC_LANES, D_MODEL),
        compiler_params=pltpu.CompilerParams(needs_layout_passes=False),
        name="peer_combine",
    )
    def body(v_hbm, idx_hbm, coef_hbm, out_hbm, idx_v, coef_v, out_v, *rest):
        bufs, sems = rest[:SC_NBUF], rest[SC_NBUF:]

        def compute(u, ub):
            tl = u // units_per_tok
            rb = (u % units_per_tok) * 32
            for g in range(D_MODEL // group):
                cols = [pl.ds(g * group + k * SC_LANES, SC_LANES) for k in range(16)]

                def rbody(r, acc):
                    acc = list(acc)
                    for q, buf in enumerate(ub):
                        co = pl.multiple_of((rb + q * SC_ROWS + r) * SC_LANES, SC_LANES)
                        cf = coef_v[tl, pl.ds(co, SC_LANES)]
                        for k in range(16):
                            acc[k] = acc[k] + buf[r, 2 * g + k // 8, pl.ds((k % 8) * SC_LANES, SC_LANES)] * cf
                    return tuple(acc)

                acc = lax.fori_loop(0, SC_ROWS, rbody, tuple(out_v[tl, c] for c in cols))
                for k in range(16):
                    out_v[tl, cols[k]] = acc[k]

        def run():
            def zero(tl, c2):
                for k in range(D_MODEL // SC_LANES):
                    out_v[tl, pl.ds(k * SC_LANES, SC_LANES)] = jnp.zeros((SC_LANES,), F32)
                return c2

            lax.fori_loop(0, SC_TOK_BATCH, zero, 0)
            _sc_gather_ring(v_hbm, idx_v, bufs, sems, unit, compute)

        _sc_batches(t // SC_WORKERS, idx_hbm, coef_hbm, out_hbm, idx_v, coef_v, out_v, run)

    return body(v3, idx_flat, coef_b)


def _coef_kernel(part_ref, g_ref, e_ref, coef_ref):
    hp = lax.Precision.HIGHEST
    e = e_ref[...]
    act = lax.dot_general(part_ref[...], e, (((1,), (1,)), ((), ())),
                          precision=hp, preferred_element_type=F32)
    gelu = 0.5 * act * (1.0 + lax.erf(act * np.float32(math.sqrt(0.5))))
    coef_ref[...] = _dot(g_ref[...] * gelu, e, precision=hp)


def _coef(part, g):
    n = part.shape[0]
    tr = min(512, n)
    w = PEER_SEL * SC_LANES
    expand = (jnp.arange(w, dtype=jnp.int32)[None, :] // SC_LANES
              == jnp.arange(PEER_SEL, dtype=jnp.int32)[:, None]).astype(F32)
    return pl.pallas_call(
        _coef_kernel,
        grid=(n // tr,),
        in_specs=[pl.BlockSpec((tr, w), lambda i: (i, 0)),
                  pl.BlockSpec((tr, PEER_SEL), lambda i: (i, 0)),
                  pl.BlockSpec((PEER_SEL, w), lambda i: (0, 0))],
        out_specs=pl.BlockSpec((tr, w), lambda i: (i, 0)),
        out_shape=jax.ShapeDtypeStruct((n, w), F32),
        compiler_params=pltpu.CompilerParams(
            dimension_semantics=("parallel",), vmem_limit_bytes=VMEM_LIMIT),
        name="peer_coef",
    )(part, g, expand)


def _final_kernel(h_ref, p_ref, g_ref, o_ref):
    o_ref[...] = _rms(h_ref[...] + p_ref[...], g_ref[...])


def _final(h2, peer_out, gain):
    n = h2.shape[0]
    tm = min(1024, n)
    blk = pl.BlockSpec((tm, D_MODEL), lambda i: (i, 0))
    return pl.pallas_call(
        _final_kernel,
        grid=(n // tm,),
        in_specs=[blk, blk, pl.BlockSpec((1, D_MODEL), lambda i: (0, 0))],
        out_specs=blk,
        out_shape=jax.ShapeDtypeStruct((n, D_MODEL), F32),
        compiler_params=pltpu.CompilerParams(
            dimension_semantics=("parallel",), vmem_limit_bytes=VMEM_LIMIT),
        name="final_norm",
    )(h2, peer_out, gain)


def _pad_heads(w, dk):
    lead = w.shape[:-1]
    w = w.reshape(lead + (HEADS, dk))
    w = jnp.pad(w, [(0, 0)] * len(lead) + [(0, 0), (0, LANES - dk)])
    return w.reshape(lead + (HEADS * LANES,))


def _rot_heads(w, dk):
    lead = w.shape[:-1]
    w = w.reshape(lead + (HEADS, 2, dk // 2))
    w = jnp.stack([-w[..., 1, :], w[..., 0, :]], axis=-2)
    return w.reshape(lead + (HEADS * dk,))


def _pack_in_weights(w_in):
    gq, gk, gv, gg, ga, rq, rk, rv, rg = jnp.split(w_in, np.cumsum(IN_SPLITS)[:-1].tolist(), axis=-1)
    ga = jnp.pad(ga, ((0, 0), (0, 2 * LANES - GATE_RANK)))
    return jnp.concatenate(
        [_pad_heads(gq, DK), _pad_heads(gk, DK), gv, gg,
         _pad_heads(rq, DK), _pad_heads(rk, DK), rv, rg,
         _pad_heads(_rot_heads(rq, DK), DK), _pad_heads(_rot_heads(rk, DK), DK), ga],
        axis=-1).astype(BF16)


def _rope_tables(pos):
    inv_freq = ROPE_BASE ** (-jnp.arange(0, DK, 2, dtype=F32) / DK)
    ang = pos[:, None] * inv_freq[None, :]
    full = lambda t: _pad_heads(jnp.tile(t, (1, 2 * HEADS)), DK)
    return full(jnp.cos(ang)), full(jnp.sin(ang))


def kernel(x, meta_tokens, norm1_gain, w_in, gla_gate_w2, gla_gate_b, gla_norm_gain, ret_norm_gain, ret_norm_bias, w_out, norm2_gain, peer_w_q, peer_subkeys, peer_u, peer_v, final_gain):
    b, s, d = x.shape
    assert d == D_MODEL and (s // SEGS_PER_SEQ) % (SC_WORKERS * SC_TOK_BATCH) == 0
    assert norm1_gain.shape[0] == 1, "single-layer block"
    x2d = x.reshape(b * s, d)

    w_all = _pack_in_weights(w_in[0])
    w2p = jnp.pad(_pad_heads(gla_gate_w2[0], DK), ((0, LANES - GATE_RANK), (0, 0))).astype(BF16)
    gbp = _pad_heads(gla_gate_b[0][None, :], DK)
    g1 = norm1_gain[0][None, :]
    ggain = gla_norm_gain[0][None, :]
    rgain = ret_norm_gain[0][None, :]
    rbias = ret_norm_bias[0][None, :]

    h0 = jnp.concatenate([jnp.zeros((N_PAD, d), x.dtype), meta_tokens.astype(x.dtype)], axis=0)
    cos0, sin0 = _rope_tables(jnp.arange(CHUNK, dtype=F32) - N_PAD)
    cos1, sin1 = _rope_tables(jnp.arange(s, dtype=F32) + N_META)
    zero_state = jnp.zeros((HEADS, DV, LANES), F32)
    _, s_meta, r_meta = _mixer(_norm_proj(h0, g1, w_all), cos0, sin0, w2p, gbp, ggain, rgain, rbias,
                               zero_state, zero_state, batch=1, n_invalid=N_PAD)

    w_out_b = w_out[0].astype(BF16)
    g2 = norm2_gain[0][None, :]
    w_q = peer_w_q[0].astype(BF16)
    subkeys = peer_subkeys[0].reshape(2 * PEER_HEADS, PEER_NKEYS, PEER_DKEY // 2).astype(BF16)
    n_exp = peer_u.shape[1]
    u3 = peer_u[0].reshape(n_exp, ROW_TILES, LANES)
    v3 = peer_v[0].reshape(n_exp, ROW_TILES, LANES)
    gf = final_gain[None, :]

    seg = s // SEGS_PER_SEQ
    outs = []
    for bi in range(b):
        s_state, r_state = s_meta, r_meta
        for si in range(SEGS_PER_SEQ):
            xs = x2d[bi * s + si * seg:bi * s + (si + 1) * seg]
            pos = slice(si * seg, (si + 1) * seg)
            o, s_state, r_state = _mixer(_norm_proj(xs, g1, w_all), cos1[pos], sin1[pos], w2p, gbp,
                                         ggain, rgain, rbias, s_state, r_state, batch=1, n_invalid=0)
            h2, xn2 = _out_proj(o, xs, w_out_b, g2)
            idx, g = _route(xn2, w_q, subkeys)
            idx_flat = idx.reshape(-1)
            part = _peer_dot_sc(u3, idx_flat, xn2)
            peer_out = _peer_combine_sc(v3, idx_flat, _coef(part, g))
            outs.append(_final(h2, peer_out, gf))
    return jnp.concatenate(outs, axis=0).reshape(b, s, d)
```

```python
import functools
import math

import jax
import jax.numpy as jnp
import numpy as np
from jax import lax
from jax.experimental import pallas as pl
from jax.experimental.pallas import tpu as pltpu
from jax.experimental.pallas import tpu_sc as plsc

F32 = jnp.float32
BF16 = jnp.bfloat16

D_MODEL = 1024
N_META = 16
CHUNK = 64
N_PAD = CHUNK - N_META
EPS = 1e-6

HEADS = 4
DV = 128
DK = 64
LANES = 128
GATE_RANK = 16
GATE_TAU = 16.0
ROPE_BASE = 10000.0
IN_SPLITS = (HEADS * DK, HEADS * DK, HEADS * DV, HEADS * DV, GATE_RANK,
             HEADS * DK, HEADS * DK, HEADS * DV, HEADS * DV)

PEER_HEADS = 8
PEER_NKEYS = 128
PEER_DKEY = 256
PEER_TOPK = 16
PEER_SEL = PEER_HEADS * PEER_TOPK

HW = HEADS * LANES
OFF_GQ, OFF_GK, OFF_GV, OFF_GG = 0, HW, 2 * HW, 3 * HW
OFF_RQ, OFF_RK, OFF_RV, OFF_RG = 4 * HW, 5 * HW, 6 * HW, 7 * HW
OFF_RQR, OFF_RKR, OFF_GA = 8 * HW, 9 * HW, 10 * HW
PROJ_W = 10 * HW + 2 * LANES
PROJ_TN = PROJ_W // 3

LOG_GAMMA = tuple(math.log(1.0 - 2.0 ** (-5.0 - h)) for h in range(HEADS))

SC_CORES, SC_SUBCORES, SC_LANES = 2, 16, 16
SC_WORKERS = SC_CORES * SC_SUBCORES
SC_TOK_BATCH = 16
SC_ROWS = 16
SC_NBUF = 8
SC_GATHERS = SC_TOK_BATCH * PEER_SEL // SC_ROWS
HALF = D_MODEL // 2
ROW_TILES = HALF // LANES
HI_MASK = 0xFFFF0000

SEGS_PER_SEQ = 2

VMEM_LIMIT = 48 * 1024 * 1024


def _dot(a, b, **kw):
    return jnp.dot(a, b, preferred_element_type=F32, **kw)


def _dot_nt(a, b):
    return lax.dot_general(a, b, (((1,), (1,)), ((), ())), preferred_element_type=F32)


def _dot_tn(a, b):
    return lax.dot_general(a, b, (((0,), (0,)), ((), ())), preferred_element_type=F32)


def _rms(x, g):
    return x * lax.rsqrt(jnp.mean(x * x, axis=-1, keepdims=True) + EPS) * g


def _silu(x):
    return x * jax.nn.sigmoid(x)


def _norm_proj_kernel(x_ref, g_ref, w_ref, o_ref):
    xn = _rms(x_ref[...], g_ref[...]).astype(BF16)
    o_ref[...] = _dot(xn, w_ref[...])


def _norm_proj(x2d, gain, w_all):
    n = x2d.shape[0]
    tm = min(512, n)
    return pl.pallas_call(
        _norm_proj_kernel,
        grid=(n // tm, PROJ_W // PROJ_TN),
        in_specs=[
            pl.BlockSpec((tm, D_MODEL), lambda i, j: (i, 0)),
            pl.BlockSpec((1, D_MODEL), lambda i, j: (0, 0)),
            pl.BlockSpec((D_MODEL, PROJ_TN), lambda i, j: (0, j)),
        ],
        out_specs=pl.BlockSpec((tm, PROJ_TN), lambda i, j: (i, j)),
        out_shape=jax.ShapeDtypeStruct((n, PROJ_W), F32),
        compiler_params=pltpu.CompilerParams(
            dimension_semantics=("parallel", "parallel"), vmem_limit_bytes=VMEM_LIMIT),
        name="norm_in_proj",
    )(x2d, gain, w_all)


def _mixer_kernel(proj_ref, cos_ref, sin_ref, w2_ref, gb_ref, ggain_ref, rgain_ref, rbias_ref,
                  s0_ref, r0_ref, o_ref, s_out_ref, r_out_ref, s_scr, r_scr, *, n_invalid, rows_blk):
    j = pl.program_id(1)

    @pl.when(j == 0)
    def _():
        s_scr[...] = s0_ref[...]
        r_scr[...] = r0_ref[...]

    row = lax.broadcasted_iota(jnp.int32, (CHUNK, CHUNK), 0)
    col = lax.broadcasted_iota(jnp.int32, (CHUNK, CHUNK), 1)
    causal = row >= col
    tril = causal.astype(F32)
    diff = jnp.maximum(row - col, 0).astype(F32)
    rowf = lax.broadcasted_iota(jnp.int32, (CHUNK, LANES), 0).astype(F32)
    row1 = lax.broadcasted_iota(jnp.int32, (CHUNK, 1), 0)

    def chunk_body(c, carry):
        r0 = pl.multiple_of(c * CHUNK, CHUNK)
        rows = pl.ds(r0, CHUNK)
        valid = ((j * rows_blk + r0 + row1) >= n_invalid).astype(F32)

        ga = proj_ref[rows, OFF_GA:OFF_GA + LANES].astype(BF16)
        pre = _dot(ga, w2_ref[...]) + gb_ref[...]
        log_a = jax.nn.log_sigmoid(pre) * (1.0 / GATE_TAU)
        bcum = _dot(tril, log_a, precision=lax.Precision.HIGHEST)
        b_last = bcum[CHUNK - 1:CHUNK, :]
        gq = proj_ref[rows, OFF_GQ:OFF_GQ + HW] * (DK ** -0.5)
        gk = proj_ref[rows, OFF_GK:OFF_GK + HW] * valid
        q_t = (gq * jnp.exp(bcum)).astype(BF16)
        k_t = (gk * jnp.exp(-bcum)).astype(BF16)
        k_end = (gk * jnp.exp(b_last - bcum)).astype(BF16)
        decay = jnp.exp(b_last)
        for h in range(HEADS):
            sl = slice(h * LANES, (h + 1) * LANES)
            v = proj_ref[rows, OFF_GV + h * DV:OFF_GV + (h + 1) * DV].astype(BF16)
            att = jnp.where(causal, _dot_nt(q_t[:, sl], k_t[:, sl]), 0.0).astype(BF16)
            s_prev = s_scr[h]
            o = _dot(att, v) + _dot_nt(q_t[:, sl], s_prev.astype(BF16))
            s_scr[h] = s_prev * decay[:, sl] + _dot_tn(v, k_end[:, sl])
            o = o * lax.rsqrt(jnp.mean(o * o, axis=-1, keepdims=True) + EPS) * ggain_ref[...]
            gate = proj_ref[rows, OFF_GG + h * DV:OFF_GG + (h + 1) * DV]
            o_ref[rows, h * DV:(h + 1) * DV] = (o * _silu(gate)).astype(o_ref.dtype)

        cosb = cos_ref[rows, :]
        sinb = sin_ref[rows, :]
        rq = proj_ref[rows, OFF_RQ:OFF_RQ + HW]
        rqr = proj_ref[rows, OFF_RQR:OFF_RQR + HW]
        rk = proj_ref[rows, OFF_RK:OFF_RK + HW] * valid
        rkr = proj_ref[rows, OFF_RKR:OFF_RKR + HW] * valid
        q_r = (rq * cosb + rqr * sinb).astype(BF16)
        k_r = (rk * cosb + rkr * sinb) * (DK ** -0.5)
        for h in range(HEADS):
            sl = slice(h * LANES, (h + 1) * LANES)
            lg = LOG_GAMMA[h]
            dmat = jnp.where(causal, jnp.exp(lg * diff), 0.0)
            xi = jnp.exp(lg * (rowf + 1.0))
            zeta = jnp.exp(lg * (CHUNK - 1.0 - rowf))
            v = proj_ref[rows, OFF_RV + h * DV:OFF_RV + (h + 1) * DV].astype(BF16)
            k_h = k_r[:, sl]
            att = (_dot_nt(q_r[:, sl], k_h.astype(BF16)) * dmat).astype(BF16)
            r_prev = r_scr[h]
            o = _dot(att, v) + _dot_nt(q_r[:, sl], r_prev.astype(BF16)) * xi
            r_scr[h] = r_prev * math.exp(lg * CHUNK) + _dot_tn(v, (k_h * zeta).astype(BF16))
            mu = jnp.mean(o, axis=-1, keepdims=True)
            var = jnp.mean(jnp.square(o - mu), axis=-1, keepdims=True)
            o = (o - mu) * lax.rsqrt(var + EPS) * rgain_ref[...] + rbias_ref[...]
            gate = proj_ref[rows, OFF_RG + h * DV:OFF_RG + (h + 1) * DV]
            o_ref[rows, HW + h * DV:HW + (h + 1) * DV] = (o * _silu(gate)).astype(o_ref.dtype)
        return carry

    lax.fori_loop(0, rows_blk // CHUNK, chunk_body, 0)
    s_out_ref[...] = s_scr[...]
    r_out_ref[...] = r_scr[...]


def _mixer(proj, cosf, sinf, w2p, gbp, ggain, rgain, rbias, s0, r0, *, batch, n_invalid):
    rows = proj.shape[0] // batch
    rows_blk = min(256, rows)
    nblk = rows // rows_blk
    const2 = lambda b, j: (0, 0)
    const3 = lambda b, j: (0, 0, 0)
    state = jax.ShapeDtypeStruct((HEADS, DV, LANES), F32)
    return pl.pallas_call(
        functools.partial(_mixer_kernel, n_invalid=n_invalid, rows_blk=rows_blk),
        grid=(batch, nblk),
        in_specs=[
            pl.BlockSpec((rows_blk, PROJ_W), lambda b, j: (b * nblk + j, 0)),
            pl.BlockSpec((rows_blk, HW), lambda b, j: (j, 0)),
            pl.BlockSpec((rows_blk, HW), lambda b, j: (j, 0)),
            pl.BlockSpec((LANES, HW), const2),
            pl.BlockSpec((1, HW), const2),
            pl.BlockSpec((1, DV), const2),
            pl.BlockSpec((1, DV), const2),
            pl.BlockSpec((1, DV), const2),
            pl.BlockSpec((HEADS, DV, LANES), const3),
            pl.BlockSpec((HEADS, DV, LANES), const3),
        ],
        out_specs=[
            pl.BlockSpec((rows_blk, D_MODEL), lambda b, j: (b * nblk + j, 0)),
            pl.BlockSpec((HEADS, DV, LANES), const3),
            pl.BlockSpec((HEADS, DV, LANES), const3),
        ],
        out_shape=[jax.ShapeDtypeStruct((batch * rows, D_MODEL), BF16), state, state],
        scratch_shapes=[pltpu.VMEM((HEADS, DV, LANES), F32), pltpu.VMEM((HEADS, DV, LANES), F32)],
        compiler_params=pltpu.CompilerParams(
            dimension_semantics=("arbitrary", "arbitrary"), vmem_limit_bytes=VMEM_LIMIT),
        name="token_mixers",
    )(proj, cosf, sinf, w2p, gbp, ggain, rgain, rbias, s0, r0)


def _out_proj_kernel(o_ref, x_ref, w_ref, g_ref, h_ref, xn_ref):
    h = x_ref[...] + _dot(o_ref[...], w_ref[...])
    h_ref[...] = h
    xn_ref[...] = _rms(h, g_ref[...])


def _out_proj(o, x2d, w_out, gain2):
    n = x2d.shape[0]
    tm = min(512, n)
    blk = pl.BlockSpec((tm, D_MODEL), lambda i: (i, 0))
    return pl.pallas_call(
        _out_proj_kernel,
        grid=(n // tm,),
        in_specs=[blk, blk, pl.BlockSpec((D_MODEL, D_MODEL), lambda i: (0, 0)),
                  pl.BlockSpec((1, D_MODEL), lambda i: (0, 0))],
        out_specs=[blk, blk],
        out_shape=[jax.ShapeDtypeStruct((n, D_MODEL), F32)] * 2,
        compiler_params=pltpu.CompilerParams(
            dimension_semantics=("parallel",), vmem_limit_bytes=VMEM_LIMIT),
        name="out_proj_norm",
    )(o, x2d, w_out, gain2)


def _top_k_rows(s, payload=None):
    n = s.shape[0]
    rowi = lax.broadcasted_iota(jnp.int32, s.shape, 0)
    vals, picks = [], []
    for _ in range(PEER_TOPK):
        m = jnp.max(s, axis=0, keepdims=True)
        am = jnp.min(jnp.where(s == m, rowi, n), axis=0, keepdims=True)
        sel = rowi == am
        vals.append(m)
        if payload is None:
            picks.append(am)
        else:
            picks.append(jnp.min(jnp.where(sel, payload, jnp.iinfo(jnp.int32).max), axis=0, keepdims=True))
        s = jnp.where(sel, -jnp.inf, s)
    return jnp.concatenate(vals, axis=0), jnp.concatenate(picks, axis=0)


def _route_kernel(xn_ref, wq_ref, sk_ref, idx_ref, g_ref, q_scr, i_scr, g_scr):
    tr = xn_ref.shape[0]
    q = _dot(xn_ref[...].astype(BF16), wq_ref[...])
    for hc in range(2 * PEER_HEADS):
        q_scr[hc] = q[:, hc * LANES:(hc + 1) * LANES].astype(BF16)

    def sub_body(sb, carry):
        rows = pl.ds(pl.multiple_of(sb * LANES, LANES), LANES)

        def head_body(h, c2):
            s0 = _dot_nt(sk_ref[2 * h], q_scr[2 * h, rows, :])
            s1 = _dot_nt(sk_ref[2 * h + 1], q_scr[2 * h + 1, rows, :])
            v0, i0 = _top_k_rows(s0)
            v1, i1 = _top_k_rows(s1)
            cand = jnp.concatenate([v0[a:a + 1] + v1 for a in range(PEER_TOPK)], axis=0)
            cidx = jnp.concatenate([i0[a:a + 1] * PEER_NKEYS + i1 for a in range(PEER_TOPK)], axis=0)
            best, e = _top_k_rows(cand, cidx)
            ex = jnp.exp(best - best[0:1])
            i_scr[h] = e
            g_scr[h] = ex / jnp.sum(ex, axis=0, keepdims=True)
            return c2

        lax.fori_loop(0, PEER_HEADS, head_body, 0)
        idx_ref[rows, :] = i_scr[...].reshape(PEER_SEL, LANES).T
        g_ref[rows, :] = g_scr[...].reshape(PEER_SEL, LANES).T
        return carry

    lax.fori_loop(0, tr // LANES, sub_body, 0)


def _route(xn2, w_q, subkeys):
    n = xn2.shape[0]
    tr = min(512, n)
    qw = 2 * PEER_HEADS * LANES
    return pl.pallas_call(
        _route_kernel,
        grid=(n // tr,),
        in_specs=[pl.BlockSpec((tr, D_MODEL), lambda i: (i, 0)),
                  pl.BlockSpec((D_MODEL, qw), lambda i: (0, 0)),
                  pl.BlockSpec((2 * PEER_HEADS, PEER_NKEYS, LANES), lambda i: (0, 0, 0))],
        out_specs=[pl.BlockSpec((tr, PEER_SEL), lambda i: (i, 0))] * 2,
        out_shape=[jax.ShapeDtypeStruct((n, PEER_SEL), jnp.int32),
                   jax.ShapeDtypeStruct((n, PEER_SEL), F32)],
        scratch_shapes=[pltpu.VMEM((2 * PEER_HEADS, tr, LANES), BF16),
                        pltpu.VMEM((PEER_HEADS, PEER_TOPK, LANES), jnp.int32),
                        pltpu.VMEM((PEER_HEADS, PEER_TOPK, LANES), F32)],
        compiler_params=pltpu.CompilerParams(
            dimension_semantics=("parallel",), vmem_limit_bytes=VMEM_LIMIT),
        name="peer_route",
    )(xn2, w_q, subkeys)


def _sc_worker_id():
    return lax.axis_index("s") * SC_CORES + lax.axis_index("c")


def _sc_mesh():
    return plsc.VectorSubcoreMesh(core_axis_name="c", subcore_axis_name="s")


def _sc_gather_ring(table_hbm, idx_v, bufs, sems, unit, compute):
    def gather(g, k):
        off = pl.multiple_of(g * SC_ROWS, SC_ROWS)
        return pltpu.make_async_copy(table_hbm.at[idx_v.at[pl.ds(off, SC_ROWS)]], bufs[k], sems[k])

    for k in range(SC_NBUF):
        gather(k, k).start()

    units_per_iter = SC_NBUF // unit

    def ring_iter(i, carry):
        for m in range(units_per_iter):
            ks = range(m * unit, (m + 1) * unit)
            for k in ks:
                gather(i * SC_NBUF + k, k).wait()
            compute(i * units_per_iter + m, [bufs[k] for k in ks])
            for k in ks:
                g_next = (i + 1) * SC_NBUF + k

                @pl.when(g_next < SC_GATHERS)
                def _():
                    gather(g_next, k).start()
        return carry

    lax.fori_loop(0, SC_GATHERS // SC_NBUF, ring_iter, 0)


def _pack_table(t):
    bits = lax.bitcast_convert_type(t.astype(jnp.bfloat16), jnp.uint16).astype(jnp.uint32)
    words = bits[:, :HALF] | (bits[:, HALF:] << 16)
    return words.reshape(t.shape[0], ROW_TILES, LANES)


def _unpack_words(w):
    lo = lax.bitcast_convert_type(w << 16, F32)
    hi = lax.bitcast_convert_type(w & jnp.uint32(HI_MASK), F32)
    return lo, hi


def _sc_scratch(staged_cols, out_cols):
    return ([pltpu.VMEM((SC_TOK_BATCH * PEER_SEL,), jnp.int32),
             pltpu.VMEM((SC_TOK_BATCH, staged_cols), F32),
             pltpu.VMEM((SC_TOK_BATCH, out_cols), F32)]
            + [pltpu.VMEM((SC_ROWS, ROW_TILES, LANES), jnp.uint32) for _ in range(SC_NBUF)]
            + [pltpu.SemaphoreType.DMA for _ in range(SC_NBUF)])


def _sc_batches(tpw, idx_hbm, staged_hbm, out_hbm, idx_v, staged_v, out_v, run):
    tok0 = _sc_worker_id() * tpw

    def batch(tb, carry):
        t0 = pl.multiple_of(tok0 + tb * SC_TOK_BATCH, SC_TOK_BATCH)
        pltpu.sync_copy(idx_hbm.at[pl.ds(pl.multiple_of(t0 * PEER_SEL, SC_TOK_BATCH * PEER_SEL),
                                         SC_TOK_BATCH * PEER_SEL)], idx_v)
        pltpu.sync_copy(staged_hbm.at[pl.ds(t0, SC_TOK_BATCH)], staged_v)
        run()
        pltpu.sync_copy(out_v, out_hbm.at[pl.ds(t0, SC_TOK_BATCH)])
        return carry

    lax.fori_loop(0, tpw // SC_TOK_BATCH, batch, 0)


def _peer_dot_sc(u3, idx_flat, xn2):
    t = xn2.shape[0]
    unit = 16 // SC_ROWS
    units_per_tok = PEER_SEL // 16

    @functools.partial(
        pl.kernel,
        mesh=_sc_mesh(),
        out_type=jax.ShapeDtypeStruct((t, PEER_SEL * SC_LANES), F32),
        scratch_types=_sc_scratch(D_MODEL, PEER_SEL * SC_LANES),
        compiler_params=pltpu.CompilerParams(needs_layout_passes=False),
        name="peer_dot",
    )
    def body(u_hbm, idx_hbm, x_hbm, out_hbm, idx_v, x_v, out_v, *rest):
        bufs, sems = rest[:SC_NBUF], rest[SC_NBUF:]

        def compute(u, ub):
            tl = u // units_per_tok
            rb = (u % units_per_tok) * 16

            def cbody(c, acc):
                acc = list(acc)
                c4 = c // (LANES // SC_LANES)
                lane0 = pl.multiple_of((c % (LANES // SC_LANES)) * SC_LANES, SC_LANES)
                col = pl.multiple_of(c * SC_LANES, SC_LANES)
                x_lo = x_v[tl, pl.ds(col, SC_LANES)]
                x_hi = x_v[tl, pl.ds(HALF + col, SC_LANES)]
                for q, buf in enumerate(ub):
                    for r in range(SC_ROWS):
                        a = q * SC_ROWS + r
                        lo, hi = _unpack_words(buf[r, c4, pl.ds(lane0, SC_LANES)])
                        acc[a] = acc[a] + lo * x_lo + hi * x_hi
                return tuple(acc)

            acc = lax.fori_loop(0, HALF // SC_LANES, cbody,
                                tuple(jnp.zeros((SC_LANES,), F32) for _ in range(16)))
            for a in range(16):
                o = pl.multiple_of((rb + a) * SC_LANES, SC_LANES)
                out_v[tl, pl.ds(o, SC_LANES)] = acc[a]

        _sc_batches(t // SC_WORKERS, idx_hbm, x_hbm, out_hbm, idx_v, x_v, out_v,
                    lambda: _sc_gather_ring(u_hbm, idx_v, bufs, sems, unit, compute))

    return body(u3, idx_flat, xn2)


def _peer_combine_sc(v3, idx_flat, coef_b):
    t = coef_b.shape[0]
    unit = 32 // SC_ROWS
    units_per_tok = PEER_SEL // 32

    @functools.partial(
        pl.kernel,
        mesh=_sc_mesh(),
        out_type=jax.ShapeDtypeStruct((t, D_MODEL), F32),
        scratch_types=_sc_scratch(PEER_SEL * SC_LANES, D_MODEL),
        compiler_params=pltpu.CompilerParams(needs_layout_passes=False),
        name="peer_combine",
    )
    def body(v_hbm, idx_hbm, coef_hbm, out_hbm, idx_v, coef_v, out_v, *rest):
        bufs, sems = rest[:SC_NBUF], rest[SC_NBUF:]

        def compute(u, ub):
            tl = u // units_per_tok
            rb = (u % units_per_tok) * 32
            for g in range(ROW_TILES):
                cols = ([pl.ds(g * LANES + l * SC_LANES, SC_LANES) for l in range(8)]
                        + [pl.ds(HALF + g * LANES + l * SC_LANES, SC_LANES) for l in range(8)])

                def rbody(r, acc):
                    acc = list(acc)
                    for q, buf in enumerate(ub):
                        co = pl.multiple_of((rb + q * SC_ROWS + r) * SC_LANES, SC_LANES)
                        cf = coef_v[tl, pl.ds(co, SC_LANES)]
                        for l in range(8):
                            lo, hi = _unpack_words(buf[r, g, pl.ds(l * SC_LANES, SC_LANES)])
                            acc[l] = acc[l] + lo * cf
                            acc[8 + l] = acc[8 + l] + hi * cf
                    return tuple(acc)

                acc = lax.fori_loop(0, SC_ROWS, rbody, tuple(out_v[tl, c] for c in cols))
                for k in range(16):
                    out_v[tl, cols[k]] = acc[k]

        def run():
            def zero(tl, c2):
                for k in range(D_MODEL // SC_LANES):
                    out_v[tl, pl.ds(k * SC_LANES, SC_LANES)] = jnp.zeros((SC_LANES,), F32)
                return c2

            lax.fori_loop(0, SC_TOK_BATCH, zero, 0)
            _sc_gather_ring(v_hbm, idx_v, bufs, sems, unit, compute)

        _sc_batches(t // SC_WORKERS, idx_hbm, coef_hbm, out_hbm, idx_v, coef_v, out_v, run)

    return body(v3, idx_flat, coef_b)


def _coef_kernel(part_ref, g_ref, e_ref, coef_ref):
    hp = lax.Precision.HIGHEST
    e = e_ref[...]
    act = lax.dot_general(part_ref[...], e, (((1,), (1,)), ((), ())),
                          precision=hp, preferred_element_type=F32)
    gelu = 0.5 * act * (1.0 + lax.erf(act * np.float32(math.sqrt(0.5))))
    coef_ref[...] = _dot(g_ref[...] * gelu, e, precision=hp)


def _coef(part, g):
    n = part.shape[0]
    tr = min(512, n)
    w = PEER_SEL * SC_LANES
    expand = (jnp.arange(w, dtype=jnp.int32)[None, :] // SC_LANES
              == jnp.arange(PEER_SEL, dtype=jnp.int32)[:, None]).astype(F32)
    return pl.pallas_call(
        _coef_kernel,
        grid=(n // tr,),
        in_specs=[pl.BlockSpec((tr, w), lambda i: (i, 0)),
                  pl.BlockSpec((tr, PEER_SEL), lambda i: (i, 0)),
                  pl.BlockSpec((PEER_SEL, w), lambda i: (0, 0))],
        out_specs=pl.BlockSpec((tr, w), lambda i: (i, 0)),
        out_shape=jax.ShapeDtypeStruct((n, w), F32),
        compiler_params=pltpu.CompilerParams(
            dimension_semantics=("parallel",), vmem_limit_bytes=VMEM_LIMIT),
        name="peer_coef",
    )(part, g, expand)


def _final_kernel(h_ref, p_ref, g_ref, o_ref):
    o_ref[...] = _rms(h_ref[...] + p_ref[...], g_ref[...])


def _final(h2, peer_out, gain):
    n = h2.shape[0]
    tm = min(1024, n)
    blk = pl.BlockSpec((tm, D_MODEL), lambda i: (i, 0))
    return pl.pallas_call(
        _final_kernel,
        grid=(n // tm,),
        in_specs=[blk, blk, pl.BlockSpec((1, D_MODEL), lambda i: (0, 0))],
        out_specs=blk,
        out_shape=jax.ShapeDtypeStruct((n, D_MODEL), F32),
        compiler_params=pltpu.CompilerParams(
            dimension_semantics=("parallel",), vmem_limit_bytes=VMEM_LIMIT),
        name="final_norm",
    )(h2, peer_out, gain)


def _pad_heads(w, dk):
    lead = w.shape[:-1]
    w = w.reshape(lead + (HEADS, dk))
    w = jnp.pad(w, [(0, 0)] * len(lead) + [(0, 0), (0, LANES - dk)])
    return w.reshape(lead + (HEADS * LANES,))


def _rot_heads(w, dk):
    lead = w.shape[:-1]
    w = w.reshape(lead + (HEADS, 2, dk // 2))
    w = jnp.stack([-w[..., 1, :], w[..., 0, :]], axis=-2)
    return w.reshape(lead + (HEADS * dk,))


def _pack_in_weights(w_in):
    gq, gk, gv, gg, ga, rq, rk, rv, rg = jnp.split(w_in, np.cumsum(IN_SPLITS)[:-1].tolist(), axis=-1)
    ga = jnp.pad(ga, ((0, 0), (0, 2 * LANES - GATE_RANK)))
    return jnp.concatenate(
        [_pad_heads(gq, DK), _pad_heads(gk, DK), gv, gg,
         _pad_heads(rq, DK), _pad_heads(rk, DK), rv, rg,
         _pad_heads(_rot_heads(rq, DK), DK), _pad_heads(_rot_heads(rk, DK), DK), ga],
        axis=-1).astype(BF16)


def _rope_tables(pos):
    inv_freq = ROPE_BASE ** (-jnp.arange(0, DK, 2, dtype=F32) / DK)
    ang = pos[:, None] * inv_freq[None, :]
    full = lambda t: _pad_heads(jnp.tile(t, (1, 2 * HEADS)), DK)
    return full(jnp.cos(ang)), full(jnp.sin(ang))


def kernel(x, meta_tokens, norm1_gain, w_in, gla_gate_w2, gla_gate_b, gla_norm_gain, ret_norm_gain, ret_norm_bias, w_out, norm2_gain, peer_w_q, peer_subkeys, peer_u, peer_v, final_gain):
    b, s, d = x.shape
    assert d == D_MODEL and (s // SEGS_PER_SEQ) % (SC_WORKERS * SC_TOK_BATCH) == 0
    assert norm1_gain.shape[0] == 1, "single-layer block"
    x2d = x.reshape(b * s, d)

    w_all = _pack_in_weights(w_in[0])
    w2p = jnp.pad(_pad_heads(gla_gate_w2[0], DK), ((0, LANES - GATE_RANK), (0, 0))).astype(BF16)
    gbp = _pad_heads(gla_gate_b[0][None, :], DK)
    g1 = norm1_gain[0][None, :]
    ggain = gla_norm_gain[0][None, :]
    rgain = ret_norm_gain[0][None, :]
    rbias = ret_norm_bias[0][None, :]

    h0 = jnp.concatenate([jnp.zeros((N_PAD, d), x.dtype), meta_tokens.astype(x.dtype)], axis=0)
    cos0, sin0 = _rope_tables(jnp.arange(CHUNK, dtype=F32) - N_PAD)
    cos1, sin1 = _rope_tables(jnp.arange(s, dtype=F32) + N_META)
    zero_state = jnp.zeros((HEADS, DV, LANES), F32)
    _, s_meta, r_meta = _mixer(_norm_proj(h0, g1, w_all), cos0, sin0, w2p, gbp, ggain, rgain, rbias,
                               zero_state, zero_state, batch=1, n_invalid=N_PAD)

    w_out_b = w_out[0].astype(BF16)
    g2 = norm2_gain[0][None, :]
    w_q = peer_w_q[0].astype(BF16)
    subkeys = peer_subkeys[0].reshape(2 * PEER_HEADS, PEER_NKEYS, PEER_DKEY // 2).astype(BF16)
    u3 = _pack_table(peer_u[0])
    v3 = _pack_table(peer_v[0])
    gf = final_gain[None, :]

    seg = s // SEGS_PER_SEQ
    outs = []
    for bi in range(b):
        s_state, r_state = s_meta, r_meta
        for si in range(SEGS_PER_SEQ):
            xs = x2d[bi * s + si * seg:bi * s + (si + 1) * seg]
            pos = slice(si * seg, (si + 1) * seg)
            o, s_state, r_state = _mixer(_norm_proj(xs, g1, w_all), cos1[pos], sin1[pos], w2p, gbp,
                                         ggain, rgain, rbias, s_state, r_state, batch=1, n_invalid=0)
            h2, xn2 = _out_proj(o, xs, w_out_b, g2)
            idx, g = _route(xn2, w_q, subkeys)
            idx_flat = idx.reshape(-1)
            part = _peer_dot_sc(u3, idx_flat, xn2)
            peer_out = _peer_combine_sc(v3, idx_flat, _coef(part, g))
            outs.append(_final(h2, peer_out, gf))
    return jnp.concatenate(outs, axis=0).reshape(b, s, d)
```

```python
import functools
import math

import jax
import jax.numpy as jnp
import numpy as np
from jax import lax
from jax.experimental import pallas as pl
from jax.experimental.pallas import tpu as pltpu
from jax.experimental.pallas import tpu_sc as plsc

F32 = jnp.float32
BF16 = jnp.bfloat16

D_MODEL = 1024
N_META = 16
CHUNK = 64
N_PAD = CHUNK - N_META
EPS = 1e-6

HEADS = 4
DV = 128
DK = 64
LANES = 128
GATE_RANK = 16
GATE_TAU = 16.0
ROPE_BASE = 10000.0
IN_SPLITS = (HEADS * DK, HEADS * DK, HEADS * DV, HEADS * DV, GATE_RANK,
             HEADS * DK, HEADS * DK, HEADS * DV, HEADS * DV)

PEER_HEADS = 8
PEER_NKEYS = 128
PEER_DKEY = 256
PEER_TOPK = 16
PEER_SEL = PEER_HEADS * PEER_TOPK

HW = HEADS * LANES
OFF_GQ, OFF_GK, OFF_GV, OFF_GG = 0, HW, 2 * HW, 3 * HW
OFF_RQ, OFF_RK, OFF_RV, OFF_RG = 4 * HW, 5 * HW, 6 * HW, 7 * HW
OFF_RQR, OFF_RKR, OFF_GA = 8 * HW, 9 * HW, 10 * HW
PROJ_W = 10 * HW + 2 * LANES
PROJ_TN = PROJ_W // 3

LOG_GAMMA = tuple(math.log(1.0 - 2.0 ** (-5.0 - h)) for h in range(HEADS))

SC_CORES, SC_SUBCORES, SC_LANES = 2, 16, 16
SC_WORKERS = SC_CORES * SC_SUBCORES
SC_TOK_BATCH = 16
SC_ROWS = 16
SC_NBUF = 8
SC_GATHERS = SC_TOK_BATCH * PEER_SEL // SC_ROWS
HALF = D_MODEL // 2
ROW_TILES = HALF // LANES

SEGS_PER_SEQ = 2

VMEM_LIMIT = 48 * 1024 * 1024


def _dot(a, b, **kw):
    return jnp.dot(a, b, preferred_element_type=F32, **kw)


def _dot_nt(a, b):
    return lax.dot_general(a, b, (((1,), (1,)), ((), ())), preferred_element_type=F32)


def _dot_tn(a, b):
    return lax.dot_general(a, b, (((0,), (0,)), ((), ())), preferred_element_type=F32)


def _rms(x, g):
    return x * lax.rsqrt(jnp.mean(x * x, axis=-1, keepdims=True) + EPS) * g


def _silu(x):
    return x * jax.nn.sigmoid(x)


def _norm_proj_kernel(x_ref, g_ref, w_ref, o_ref):
    xn = _rms(x_ref[...], g_ref[...]).astype(BF16)
    o_ref[...] = _dot(xn, w_ref[...])


def _norm_proj(x2d, gain, w_all):
    n = x2d.shape[0]
    tm = min(512, n)
    return pl.pallas_call(
        _norm_proj_kernel,
        grid=(n // tm, PROJ_W // PROJ_TN),
        in_specs=[
            pl.BlockSpec((tm, D_MODEL), lambda i, j: (i, 0)),
            pl.BlockSpec((1, D_MODEL), lambda i, j: (0, 0)),
            pl.BlockSpec((D_MODEL, PROJ_TN), lambda i, j: (0, j)),
        ],
        out_specs=pl.BlockSpec((tm, PROJ_TN), lambda i, j: (i, j)),
        out_shape=jax.ShapeDtypeStruct((n, PROJ_W), F32),
        compiler_params=pltpu.CompilerParams(
            dimension_semantics=("parallel", "parallel"), vmem_limit_bytes=VMEM_LIMIT),
        name="norm_in_proj",
    )(x2d, gain, w_all)


def _mixer_kernel(proj_ref, cos_ref, sin_ref, w2_ref, gb_ref, ggain_ref, rgain_ref, rbias_ref,
                  s0_ref, r0_ref, o_ref, s_out_ref, r_out_ref, s_scr, r_scr, *, n_invalid, rows_blk):
    j = pl.program_id(1)

    @pl.when(j == 0)
    def _():
        s_scr[...] = s0_ref[...]
        r_scr[...] = r0_ref[...]

    row = lax.broadcasted_iota(jnp.int32, (CHUNK, CHUNK), 0)
    col = lax.broadcasted_iota(jnp.int32, (CHUNK, CHUNK), 1)
    causal = row >= col
    tril = causal.astype(F32)
    diff = jnp.maximum(row - col, 0).astype(F32)
    rowf = lax.broadcasted_iota(jnp.int32, (CHUNK, LANES), 0).astype(F32)
    row1 = lax.broadcasted_iota(jnp.int32, (CHUNK, 1), 0)

    def chunk_body(c, carry):
        r0 = pl.multiple_of(c * CHUNK, CHUNK)
        rows = pl.ds(r0, CHUNK)
        valid = ((j * rows_blk + r0 + row1) >= n_invalid).astype(F32)

        ga = proj_ref[rows, OFF_GA:OFF_GA + LANES].astype(BF16)
        pre = _dot(ga, w2_ref[...]) + gb_ref[...]
        log_a = jax.nn.log_sigmoid(pre) * (1.0 / GATE_TAU)
        bcum = _dot(tril, log_a, precision=lax.Precision.HIGHEST)
        b_last = bcum[CHUNK - 1:CHUNK, :]
        gq = proj_ref[rows, OFF_GQ:OFF_GQ + HW] * (DK ** -0.5)
        gk = proj_ref[rows, OFF_GK:OFF_GK + HW] * valid
        q_t = (gq * jnp.exp(bcum)).astype(BF16)
        k_t = (gk * jnp.exp(-bcum)).astype(BF16)
        k_end = (gk * jnp.exp(b_last - bcum)).astype(BF16)
        decay = jnp.exp(b_last)
        for h in range(HEADS):
            sl = slice(h * LANES, (h + 1) * LANES)
            v = proj_ref[rows, OFF_GV + h * DV:OFF_GV + (h + 1) * DV].astype(BF16)
            att = jnp.where(causal, _dot_nt(q_t[:, sl], k_t[:, sl]), 0.0).astype(BF16)
            s_prev = s_scr[h]
            o = _dot(att, v) + _dot_nt(q_t[:, sl], s_prev.astype(BF16))
            s_scr[h] = s_prev * decay[:, sl] + _dot_tn(v, k_end[:, sl])
            o = o * lax.rsqrt(jnp.mean(o * o, axis=-1, keepdims=True) + EPS) * ggain_ref[...]
            gate = proj_ref[rows, OFF_GG + h * DV:OFF_GG + (h + 1) * DV]
            o_ref[rows, h * DV:(h + 1) * DV] = (o * _silu(gate)).astype(o_ref.dtype)

        cosb = cos_ref[rows, :]
        sinb = sin_ref[rows, :]
        rq = proj_ref[rows, OFF_RQ:OFF_RQ + HW]
        rqr = proj_ref[rows, OFF_RQR:OFF_RQR + HW]
        rk = proj_ref[rows, OFF_RK:OFF_RK + HW] * valid
        rkr = proj_ref[rows, OFF_RKR:OFF_RKR + HW] * valid
        q_r = (rq * cosb + rqr * sinb).astype(BF16)
        k_r = (rk * cosb + rkr * sinb) * (DK ** -0.5)
        for h in range(HEADS):
            sl = slice(h * LANES, (h + 1) * LANES)
            lg = LOG_GAMMA[h]
            dmat = jnp.where(causal, jnp.exp(lg * diff), 0.0)
            xi = jnp.exp(lg * (rowf + 1.0))
            zeta = jnp.exp(lg * (CHUNK - 1.0 - rowf))
            v = proj_ref[rows, OFF_RV + h * DV:OFF_RV + (h + 1) * DV].astype(BF16)
            k_h = k_r[:, sl]
            att = (_dot_nt(q_r[:, sl], k_h.astype(BF16)) * dmat).astype(BF16)
            r_prev = r_scr[h]
            o = _dot(att, v) + _dot_nt(q_r[:, sl], r_prev.astype(BF16)) * xi
            r_scr[h] = r_prev * math.exp(lg * CHUNK) + _dot_tn(v, (k_h * zeta).astype(BF16))
            mu = jnp.mean(o, axis=-1, keepdims=True)
            var = jnp.mean(jnp.square(o - mu), axis=-1, keepdims=True)
            o = (o - mu) * lax.rsqrt(var + EPS) * rgain_ref[...] + rbias_ref[...]
            gate = proj_ref[rows, OFF_RG + h * DV:OFF_RG + (h + 1) * DV]
            o_ref[rows, HW + h * DV:HW + (h + 1) * DV] = (o * _silu(gate)).astype(o_ref.dtype)
        return carry

    lax.fori_loop(0, rows_blk // CHUNK, chunk_body, 0)
    s_out_ref[...] = s_scr[...]
    r_out_ref[...] = r_scr[...]


def _mixer(proj, cosf, sinf, w2p, gbp, ggain, rgain, rbias, s0, r0, *, batch, n_invalid):
    rows = proj.shape[0] // batch
    rows_blk = min(256, rows)
    nblk = rows // rows_blk
    const2 = lambda b, j: (0, 0)
    const3 = lambda b, j: (0, 0, 0)
    state = jax.ShapeDtypeStruct((HEADS, DV, LANES), F32)
    return pl.pallas_call(
        functools.partial(_mixer_kernel, n_invalid=n_invalid, rows_blk=rows_blk),
        grid=(batch, nblk),
        in_specs=[
            pl.BlockSpec((rows_blk, PROJ_W), lambda b, j: (b * nblk + j, 0)),
            pl.BlockSpec((rows_blk, HW), lambda b, j: (j, 0)),
            pl.BlockSpec((rows_blk, HW), lambda b, j: (j, 0)),
            pl.BlockSpec((LANES, HW), const2),
            pl.BlockSpec((1, HW), const2),
            pl.BlockSpec((1, DV), const2),
            pl.BlockSpec((1, DV), const2),
            pl.BlockSpec((1, DV), const2),
            pl.BlockSpec((HEADS, DV, LANES), const3),
            pl.BlockSpec((HEADS, DV, LANES), const3),
        ],
        out_specs=[
            pl.BlockSpec((rows_blk, D_MODEL), lambda b, j: (b * nblk + j, 0)),
            pl.BlockSpec((HEADS, DV, LANES), const3),
            pl.BlockSpec((HEADS, DV, LANES), const3),
        ],
        out_shape=[jax.ShapeDtypeStruct((batch * rows, D_MODEL), BF16), state, state],
        scratch_shapes=[pltpu.VMEM((HEADS, DV, LANES), F32), pltpu.VMEM((HEADS, DV, LANES), F32)],
        compiler_params=pltpu.CompilerParams(
            dimension_semantics=("arbitrary", "arbitrary"), vmem_limit_bytes=VMEM_LIMIT),
        name="token_mixers",
    )(proj, cosf, sinf, w2p, gbp, ggain, rgain, rbias, s0, r0)


def _out_proj_kernel(o_ref, x_ref, w_ref, g_ref, h_ref, xn_ref):
    h = x_ref[...] + _dot(o_ref[...], w_ref[...])
    h_ref[...] = h
    xn_ref[...] = _rms(h, g_ref[...])


def _out_proj(o, x2d, w_out, gain2):
    n = x2d.shape[0]
    tm = min(512, n)
    blk = pl.BlockSpec((tm, D_MODEL), lambda i: (i, 0))
    return pl.pallas_call(
        _out_proj_kernel,
        grid=(n // tm,),
        in_specs=[blk, blk, pl.BlockSpec((D_MODEL, D_MODEL), lambda i: (0, 0)),
                  pl.BlockSpec((1, D_MODEL), lambda i: (0, 0))],
        out_specs=[blk, blk],
        out_shape=[jax.ShapeDtypeStruct((n, D_MODEL), F32)] * 2,
        compiler_params=pltpu.CompilerParams(
            dimension_semantics=("parallel",), vmem_limit_bytes=VMEM_LIMIT),
        name="out_proj_norm",
    )(o, x2d, w_out, gain2)


def _top_k_rows(s, payload=None):
    n = s.shape[0]
    rowi = lax.broadcasted_iota(jnp.int32, s.shape, 0)
    vals, picks = [], []
    for _ in range(PEER_TOPK):
        m = jnp.max(s, axis=0, keepdims=True)
        am = jnp.min(jnp.where(s == m, rowi, n), axis=0, keepdims=True)
        sel = rowi == am
        vals.append(m)
        if payload is None:
            picks.append(am)
        else:
            picks.append(jnp.min(jnp.where(sel, payload, jnp.iinfo(jnp.int32).max), axis=0, keepdims=True))
        s = jnp.where(sel, -jnp.inf, s)
    return jnp.concatenate(vals, axis=0), jnp.concatenate(picks, axis=0)


def _route_kernel(xn_ref, wq_ref, sk_ref, idx_ref, g_ref, q_scr, i_scr, g_scr):
    tr = xn_ref.shape[0]
    q = _dot(xn_ref[...].astype(BF16), wq_ref[...])
    for hc in range(2 * PEER_HEADS):
        q_scr[hc] = q[:, hc * LANES:(hc + 1) * LANES].astype(BF16)

    def sub_body(sb, carry):
        rows = pl.ds(pl.multiple_of(sb * LANES, LANES), LANES)

        def head_body(h, c2):
            s0 = _dot_nt(sk_ref[2 * h], q_scr[2 * h, rows, :])
            s1 = _dot_nt(sk_ref[2 * h + 1], q_scr[2 * h + 1, rows, :])
            v0, i0 = _top_k_rows(s0)
            v1, i1 = _top_k_rows(s1)
            cand = jnp.concatenate([v0[a:a + 1] + v1 for a in range(PEER_TOPK)], axis=0)
            cidx = jnp.concatenate([i0[a:a + 1] * PEER_NKEYS + i1 for a in range(PEER_TOPK)], axis=0)
            best, e = _top_k_rows(cand, cidx)
            ex = jnp.exp(best - best[0:1])
            i_scr[h] = e
            g_scr[h] = ex / jnp.sum(ex, axis=0, keepdims=True)
            return c2

        lax.fori_loop(0, PEER_HEADS, head_body, 0)
        idx_ref[rows, :] = i_scr[...].reshape(PEER_SEL, LANES).T
        g_ref[rows, :] = g_scr[...].reshape(PEER_SEL, LANES).T
        return carry

    lax.fori_loop(0, tr // LANES, sub_body, 0)


def _route(xn2, w_q, subkeys):
    n = xn2.shape[0]
    tr = min(512, n)
    qw = 2 * PEER_HEADS * LANES
    return pl.pallas_call(
        _route_kernel,
        grid=(n // tr,),
        in_specs=[pl.BlockSpec((tr, D_MODEL), lambda i: (i, 0)),
                  pl.BlockSpec((D_MODEL, qw), lambda i: (0, 0)),
                  pl.BlockSpec((2 * PEER_HEADS, PEER_NKEYS, LANES), lambda i: (0, 0, 0))],
        out_specs=[pl.BlockSpec((tr, PEER_SEL), lambda i: (i, 0))] * 2,
        out_shape=[jax.ShapeDtypeStruct((n, PEER_SEL), jnp.int32),
                   jax.ShapeDtypeStruct((n, PEER_SEL), F32)],
        scratch_shapes=[pltpu.VMEM((2 * PEER_HEADS, tr, LANES), BF16),
                        pltpu.VMEM((PEER_HEADS, PEER_TOPK, LANES), jnp.int32),
                        pltpu.VMEM((PEER_HEADS, PEER_TOPK, LANES), F32)],
        compiler_params=pltpu.CompilerParams(
            dimension_semantics=("parallel",), vmem_limit_bytes=VMEM_LIMIT),
        name="peer_route",
    )(xn2, w_q, subkeys)


def _sc_worker_id():
    return lax.axis_index("s") * SC_CORES + lax.axis_index("c")


def _sc_mesh():
    return plsc.VectorSubcoreMesh(core_axis_name="c", subcore_axis_name="s")


def _sc_gather_ring(table_hbm, idx_v, bufs, sems, unit, compute):
    def gather(g, k):
        off = pl.multiple_of(g * SC_ROWS, SC_ROWS)
        return pltpu.make_async_copy(table_hbm.at[idx_v.at[pl.ds(off, SC_ROWS)]], bufs[k], sems[k])

    for k in range(SC_NBUF):
        gather(k, k).start()

    units_per_iter = SC_NBUF // unit

    def ring_iter(i, carry):
        for m in range(units_per_iter):
            ks = range(m * unit, (m + 1) * unit)
            for k in ks:
                gather(i * SC_NBUF + k, k).wait()
            compute(i * units_per_iter + m, [bufs[k] for k in ks])
            for k in ks:
                g_next = (i + 1) * SC_NBUF + k

                @pl.when(g_next < SC_GATHERS)
                def _():
                    gather(g_next, k).start()
        return carry

    lax.fori_loop(0, SC_GATHERS // SC_NBUF, ring_iter, 0)


def _pack_table(t):
    lo = lax.bitcast_convert_type(t[:, :HALF].astype(jnp.bfloat16), jnp.uint16).astype(jnp.int32)
    hi_bits = lax.bitcast_convert_type(t[:, HALF:], jnp.int32)
    sign = hi_bits & jnp.int32(-2 ** 31)
    mag = jnp.minimum(hi_bits & jnp.int32(2 ** 31 - 1), jnp.int32(0x7F000000))
    upper = jnp.maximum(mag + (1 << 15) - lo, 0) >> 16
    words = sign | (upper << 16) | lo
    return lax.bitcast_convert_type(words, jnp.uint32).reshape(t.shape[0], ROW_TILES, LANES)


def _unpack_words(w):
    lo = lax.bitcast_convert_type(w << 16, F32)
    hi = lax.bitcast_convert_type(w, F32)
    return lo, hi


def _sc_scratch(staged_cols, out_cols):
    return ([pltpu.VMEM((SC_TOK_BATCH * PEER_SEL,), jnp.int32),
             pltpu.VMEM((SC_TOK_BATCH, staged_cols), F32),
             pltpu.VMEM((SC_TOK_BATCH, out_cols), F32)]
            + [pltpu.VMEM((SC_ROWS, ROW_TILES, LANES), jnp.uint32) for _ in range(SC_NBUF)]
            + [pltpu.SemaphoreType.DMA for _ in range(SC_NBUF)])


def _sc_batches(tpw, idx_hbm, staged_hbm, out_hbm, idx_v, staged_v, out_v, run):
    tok0 = _sc_worker_id() * tpw

    def batch(tb, carry):
        t0 = pl.multiple_of(tok0 + tb * SC_TOK_BATCH, SC_TOK_BATCH)
        pltpu.sync_copy(idx_hbm.at[pl.ds(pl.multiple_of(t0 * PEER_SEL, SC_TOK_BATCH * PEER_SEL),
                                         SC_TOK_BATCH * PEER_SEL)], idx_v)
        pltpu.sync_copy(staged_hbm.at[pl.ds(t0, SC_TOK_BATCH)], staged_v)
        run()
        pltpu.sync_copy(out_v, out_hbm.at[pl.ds(t0, SC_TOK_BATCH)])
        return carry

    lax.fori_loop(0, tpw // SC_TOK_BATCH, batch, 0)


def _peer_dot_sc(u3, idx_flat, xn2):
    t = xn2.shape[0]
    unit = 16 // SC_ROWS
    units_per_tok = PEER_SEL // 16

    @functools.partial(
        pl.kernel,
        mesh=_sc_mesh(),
        out_type=jax.ShapeDtypeStruct((t, PEER_SEL * SC_LANES), F32),
        scratch_types=_sc_scratch(D_MODEL, PEER_SEL * SC_LANES),
        compiler_params=pltpu.CompilerParams(needs_layout_passes=False),
        name="peer_dot",
    )
    def body(u_hbm, idx_hbm, x_hbm, out_hbm, idx_v, x_v, out_v, *rest):
        bufs, sems = rest[:SC_NBUF], rest[SC_NBUF:]

        def compute(u, ub):
            tl = u // units_per_tok
            rb = (u % units_per_tok) * 16

            def cbody(c, acc):
                acc = list(acc)
                c4 = c // (LANES // SC_LANES)
                lane0 = pl.multiple_of((c % (LANES // SC_LANES)) * SC_LANES, SC_LANES)
                col = pl.multiple_of(c * SC_LANES, SC_LANES)
                x_lo = x_v[tl, pl.ds(col, SC_LANES)]
                x_hi = x_v[tl, pl.ds(HALF + col, SC_LANES)]
                for q, buf in enumerate(ub):
                    for r in range(SC_ROWS):
                        a = q * SC_ROWS + r
                        lo, hi = _unpack_words(buf[r, c4, pl.ds(lane0, SC_LANES)])
                        acc[a] = acc[a] + lo * x_lo + hi * x_hi
                return tuple(acc)

            acc = lax.fori_loop(0, HALF // SC_LANES, cbody,
                                tuple(jnp.zeros((SC_LANES,), F32) for _ in range(16)))
            for a in range(16):
                o = pl.multiple_of((rb + a) * SC_LANES, SC_LANES)
                out_v[tl, pl.ds(o, SC_LANES)] = acc[a]

        _sc_batches(t // SC_WORKERS, idx_hbm, x_hbm, out_hbm, idx_v, x_v, out_v,
                    lambda: _sc_gather_ring(u_hbm, idx_v, bufs, sems, unit, compute))

    return body(u3, idx_flat, xn2)


def _peer_combine_sc(v3, idx_flat, coef_b):
    t = coef_b.shape[0]
    unit = 32 // SC_ROWS
    units_per_tok = PEER_SEL // 32

    @functools.partial(
        pl.kernel,
        mesh=_sc_mesh(),
        out_type=jax.ShapeDtypeStruct((t, D_MODEL), F32),
        scratch_types=_sc_scratch(PEER_SEL * SC_LANES, D_MODEL),
        compiler_params=pltpu.CompilerParams(needs_layout_passes=False),
        name="peer_combine",
    )
    def body(v_hbm, idx_hbm, coef_hbm, out_hbm, idx_v, coef_v, out_v, *rest):
        bufs, sems = rest[:SC_NBUF], rest[SC_NBUF:]

        def compute(u, ub):
            tl = u // units_per_tok
            rb = (u % units_per_tok) * 32
            for g in range(ROW_TILES):
                cols = ([pl.ds(g * LANES + l * SC_LANES, SC_LANES) for l in range(8)]
                        + [pl.ds(HALF + g * LANES + l * SC_LANES, SC_LANES) for l in range(8)])

                def rbody(r, acc):
                    acc = list(acc)
                    for q, buf in enumerate(ub):
                        co = pl.multiple_of((rb + q * SC_ROWS + r) * SC_LANES, SC_LANES)
                        cf = coef_v[tl, pl.ds(co, SC_LANES)]
                        for l in range(8):
                            lo, hi = _unpack_words(buf[r, g, pl.ds(l * SC_LANES, SC_LANES)])
                            acc[l] = acc[l] + lo * cf
                            acc[8 + l] = acc[8 + l] + hi * cf
                    return tuple(acc)

                acc = lax.fori_loop(0, SC_ROWS, rbody, tuple(out_v[tl, c] for c in cols))
                for k in range(16):
                    out_v[tl, cols[k]] = acc[k]

        def run():
            def zero(tl, c2):
                for k in range(D_MODEL // SC_LANES):
                    out_v[tl, pl.ds(k * SC_LANES, SC_LANES)] = jnp.zeros((SC_LANES,), F32)
                return c2

            lax.fori_loop(0, SC_TOK_BATCH, zero, 0)
            _sc_gather_ring(v_hbm, idx_v, bufs, sems, unit, compute)

        _sc_batches(t // SC_WORKERS, idx_hbm, coef_hbm, out_hbm, idx_v, coef_v, out_v, run)

    return body(v3, idx_flat, coef_b)


def _coef_kernel(part_ref, g_ref, e_ref, coef_ref):
    hp = lax.Precision.HIGHEST
    e = e_ref[...]
    act = lax.dot_general(part_ref[...], e, (((1,), (1,)), ((), ())),
                          precision=hp, preferred_element_type=F32)
    gelu = 0.5 * act * (1.0 + lax.erf(act * np.float32(math.sqrt(0.5))))
    coef_ref[...] = _dot(g_ref[...] * gelu, e, precision=hp)


def _coef(part, g):
    n = part.shape[0]
    tr = min(512, n)
    w = PEER_SEL * SC_LANES
    expand = (jnp.arange(w, dtype=jnp.int32)[None, :] // SC_LANES
              == jnp.arange(PEER_SEL, dtype=jnp.int32)[:, None]).astype(F32)
    return pl.pallas_call(
        _coef_kernel,
        grid=(n // tr,),
        in_specs=[pl.BlockSpec((tr, w), lambda i: (i, 0)),
                  pl.BlockSpec((tr, PEER_SEL), lambda i: (i, 0)),
                  pl.BlockSpec((PEER_SEL, w), lambda i: (0, 0))],
        out_specs=pl.BlockSpec((tr, w), lambda i: (i, 0)),
        out_shape=jax.ShapeDtypeStruct((n, w), F32),
        compiler_params=pltpu.CompilerParams(
            dimension_semantics=("parallel",), vmem_limit_bytes=VMEM_LIMIT),
        name="peer_coef",
    )(part, g, expand)


def _final_kernel(h_ref, p_ref, g_ref, o_ref):
    o_ref[...] = _rms(h_ref[...] + p_ref[...], g_ref[...])


def _final(h2, peer_out, gain):
    n = h2.shape[0]
    tm = min(1024, n)
    blk = pl.BlockSpec((tm, D_MODEL), lambda i: (i, 0))
    return pl.pallas_call(
        _final_kernel,
        grid=(n // tm,),
        in_specs=[blk, blk, pl.BlockSpec((1, D_MODEL), lambda i: (0, 0))],
        out_specs=blk,
        out_shape=jax.ShapeDtypeStruct((n, D_MODEL), F32),
        compiler_params=pltpu.CompilerParams(
            dimension_semantics=("parallel",), vmem_limit_bytes=VMEM_LIMIT),
        name="final_norm",
    )(h2, peer_out, gain)


def _pad_heads(w, dk):
    lead = w.shape[:-1]
    w = w.reshape(lead + (HEADS, dk))
    w = jnp.pad(w, [(0, 0)] * len(lead) + [(0, 0), (0, LANES - dk)])
    return w.reshape(lead + (HEADS * LANES,))


def _rot_heads(w, dk):
    lead = w.shape[:-1]
    w = w.reshape(lead + (HEADS, 2, dk // 2))
    w = jnp.stack([-w[..., 1, :], w[..., 0, :]], axis=-2)
    return w.reshape(lead + (HEADS * dk,))


def _pack_in_weights(w_in):
    gq, gk, gv, gg, ga, rq, rk, rv, rg = jnp.split(w_in, np.cumsum(IN_SPLITS)[:-1].tolist(), axis=-1)
    ga = jnp.pad(ga, ((0, 0), (0, 2 * LANES - GATE_RANK)))
    return jnp.concatenate(
        [_pad_heads(gq, DK), _pad_heads(gk, DK), gv, gg,
         _pad_heads(rq, DK), _pad_heads(rk, DK), rv, rg,
         _pad_heads(_rot_heads(rq, DK), DK), _pad_heads(_rot_heads(rk, DK), DK), ga],
        axis=-1).astype(BF16)


def _rope_tables(pos):
    inv_freq = ROPE_BASE ** (-jnp.arange(0, DK, 2, dtype=F32) / DK)
    ang = pos[:, None] * inv_freq[None, :]
    full = lambda t: _pad_heads(jnp.tile(t, (1, 2 * HEADS)), DK)
    return full(jnp.cos(ang)), full(jnp.sin(ang))


def kernel(x, meta_tokens, norm1_gain, w_in, gla_gate_w2, gla_gate_b, gla_norm_gain, ret_norm_gain, ret_norm_bias, w_out, norm2_gain, peer_w_q, peer_subkeys, peer_u, peer_v, final_gain):
    b, s, d = x.shape
    assert d == D_MODEL and (s // SEGS_PER_SEQ) % (SC_WORKERS * SC_TOK_BATCH) == 0
    assert norm1_gain.shape[0] == 1, "single-layer block"
    x2d = x.reshape(b * s, d)

    w_all = _pack_in_weights(w_in[0])
    w2p = jnp.pad(_pad_heads(gla_gate_w2[0], DK), ((0, LANES - GATE_RANK), (0, 0))).astype(BF16)
    gbp = _pad_heads(gla_gate_b[0][None, :], DK)
    g1 = norm1_gain[0][None, :]
    ggain = gla_norm_gain[0][None, :]
    rgain = ret_norm_gain[0][None, :]
    rbias = ret_norm_bias[0][None, :]

    h0 = jnp.concatenate([jnp.zeros((N_PAD, d), x.dtype), meta_tokens.astype(x.dtype)], axis=0)
    cos0, sin0 = _rope_tables(jnp.arange(CHUNK, dtype=F32) - N_PAD)
    cos1, sin1 = _rope_tables(jnp.arange(s, dtype=F32) + N_META)
    zero_state = jnp.zeros((HEADS, DV, LANES), F32)
    _, s_meta, r_meta = _mixer(_norm_proj(h0, g1, w_all), cos0, sin0, w2p, gbp, ggain, rgain, rbias,
                               zero_state, zero_state, batch=1, n_invalid=N_PAD)

    w_out_b = w_out[0].astype(BF16)
    g2 = norm2_gain[0][None, :]
    w_q = peer_w_q[0].astype(BF16)
    subkeys = peer_subkeys[0].reshape(2 * PEER_HEADS, PEER_NKEYS, PEER_DKEY // 2).astype(BF16)
    u3 = _pack_table(peer_u[0])
    v3 = _pack_table(peer_v[0])
    gf = final_gain[None, :]

    seg = s // SEGS_PER_SEQ
    outs = []
    for bi in range(b):
        s_state, r_state = s_meta, r_meta
        for si in range(SEGS_PER_SEQ):
            xs = x2d[bi * s + si * seg:bi * s + (si + 1) * seg]
            pos = slice(si * seg, (si + 1) * seg)
            o, s_state, r_state = _mixer(_norm_proj(xs, g1, w_all), cos1[pos], sin1[pos], w2p, gbp,
                                         ggain, rgain, rbias, s_state, r_state, batch=1, n_invalid=0)
            h2, xn2 = _out_proj(o, xs, w_out_b, g2)
            idx, g = _route(xn2, w_q, subkeys)
            idx_flat = idx.reshape(-1)
            part = _peer_dot_sc(u3, idx_flat, xn2)
            peer_out = _peer_combine_sc(v3, idx_flat, _coef(part, g))
            outs.append(_final(h2, peer_out, gf))
    return jnp.concatenate(outs, axis=0).reshape(b, s, d)
```

```python
import functools
import math

import jax
import jax.numpy as jnp
import numpy as np
from jax import lax
from jax.experimental import pallas as pl
from jax.experimental.pallas import tpu as pltpu
from jax.experimental.pallas import tpu_sc as plsc

F32 = jnp.float32
BF16 = jnp.bfloat16

D_MODEL = 1024
N_META = 16
CHUNK = 64
N_PAD = CHUNK - N_META
EPS = 1e-6

HEADS = 4
DV = 128
DK = 64
LANES = 128
GATE_RANK = 16
GATE_TAU = 16.0
ROPE_BASE = 10000.0
IN_SPLITS = (HEADS * DK, HEADS * DK, HEADS * DV, HEADS * DV, GATE_RANK,
             HEADS * DK, HEADS * DK, HEADS * DV, HEADS * DV)

PEER_HEADS = 8
PEER_NKEYS = 128
PEER_DKEY = 256
PEER_TOPK = 16
PEER_SEL = PEER_HEADS * PEER_TOPK

HW = HEADS * LANES
OFF_GQ, OFF_GK, OFF_GV, OFF_GG = 0, HW, 2 * HW, 3 * HW
OFF_RQ, OFF_RK, OFF_RV, OFF_RG = 4 * HW, 5 * HW, 6 * HW, 7 * HW
OFF_RQR, OFF_RKR, OFF_GA = 8 * HW, 9 * HW, 10 * HW
PROJ_W = 10 * HW + 2 * LANES
PROJ_TN = PROJ_W // 3

LOG_GAMMA = tuple(math.log(1.0 - 2.0 ** (-5.0 - h)) for h in range(HEADS))

SC_CORES, SC_SUBCORES, SC_LANES = 2, 16, 16
SC_WORKERS = SC_CORES * SC_SUBCORES
SC_TOK_BATCH = 16
SC_ROWS = 16
SC_NBUF = 8
SC_GATHERS = SC_TOK_BATCH * PEER_SEL // SC_ROWS
HALF = D_MODEL // 2
ROW_TILES = HALF // LANES

SEGS_PER_SEQ = 2

VMEM_LIMIT = 48 * 1024 * 1024


def _dot(a, b, **kw):
    return jnp.dot(a, b, preferred_element_type=F32, **kw)


def _dot_nt(a, b):
    return lax.dot_general(a, b, (((1,), (1,)), ((), ())), preferred_element_type=F32)


def _dot_tn(a, b):
    return lax.dot_general(a, b, (((0,), (0,)), ((), ())), preferred_element_type=F32)


def _rms(x, g):
    return x * lax.rsqrt(jnp.mean(x * x, axis=-1, keepdims=True) + EPS) * g


def _silu(x):
    return x * jax.nn.sigmoid(x)


def _norm_proj_kernel(x_ref, g_ref, w_ref, o_ref):
    xn = _rms(x_ref[...], g_ref[...]).astype(BF16)
    o_ref[...] = _dot(xn, w_ref[...])


def _norm_proj(x2d, gain, w_all):
    n = x2d.shape[0]
    tm = min(512, n)
    return pl.pallas_call(
        _norm_proj_kernel,
        grid=(n // tm, PROJ_W // PROJ_TN),
        in_specs=[
            pl.BlockSpec((tm, D_MODEL), lambda i, j: (i, 0)),
            pl.BlockSpec((1, D_MODEL), lambda i, j: (0, 0)),
            pl.BlockSpec((D_MODEL, PROJ_TN), lambda i, j: (0, j)),
        ],
        out_specs=pl.BlockSpec((tm, PROJ_TN), lambda i, j: (i, j)),
        out_shape=jax.ShapeDtypeStruct((n, PROJ_W), F32),
        compiler_params=pltpu.CompilerParams(
            dimension_semantics=("parallel", "parallel"), vmem_limit_bytes=VMEM_LIMIT),
        name="norm_in_proj",
    )(x2d, gain, w_all)


def _mixer_kernel(proj_ref, cos_ref, sin_ref, w2_ref, gb_ref, ggain_ref, rgain_ref, rbias_ref,
                  s0_ref, r0_ref, o_ref, s_out_ref, r_out_ref, s_scr, r_scr, *, n_invalid, rows_blk):
    j = pl.program_id(1)

    @pl.when(j == 0)
    def _():
        s_scr[...] = s0_ref[...]
        r_scr[...] = r0_ref[...]

    row = lax.broadcasted_iota(jnp.int32, (CHUNK, CHUNK), 0)
    col = lax.broadcasted_iota(jnp.int32, (CHUNK, CHUNK), 1)
    causal = row >= col
    tril = causal.astype(F32)
    diff = jnp.maximum(row - col, 0).astype(F32)
    rowf = lax.broadcasted_iota(jnp.int32, (CHUNK, LANES), 0).astype(F32)
    row1 = lax.broadcasted_iota(jnp.int32, (CHUNK, 1), 0)

    def chunk_body(c, carry):
        r0 = pl.multiple_of(c * CHUNK, CHUNK)
        rows = pl.ds(r0, CHUNK)
        valid = ((j * rows_blk + r0 + row1) >= n_invalid).astype(F32)

        ga = proj_ref[rows, OFF_GA:OFF_GA + LANES].astype(BF16)
        pre = _dot(ga, w2_ref[...]) + gb_ref[...]
        log_a = jax.nn.log_sigmoid(pre) * (1.0 / GATE_TAU)
        bcum = _dot(tril, log_a, precision=lax.Precision.HIGHEST)
        b_last = bcum[CHUNK - 1:CHUNK, :]
        gq = proj_ref[rows, OFF_GQ:OFF_GQ + HW] * (DK ** -0.5)
        gk = proj_ref[rows, OFF_GK:OFF_GK + HW] * valid
        q_t = (gq * jnp.exp(bcum)).astype(BF16)
        k_t = (gk * jnp.exp(-bcum)).astype(BF16)
        k_end = (gk * jnp.exp(b_last - bcum)).astype(BF16)
        decay = jnp.exp(b_last)
        for h in range(HEADS):
            sl = slice(h * LANES, (h + 1) * LANES)
            v = proj_ref[rows, OFF_GV + h * DV:OFF_GV + (h + 1) * DV].astype(BF16)
            att = jnp.where(causal, _dot_nt(q_t[:, sl], k_t[:, sl]), 0.0).astype(BF16)
            s_prev = s_scr[h]
            o = _dot(att, v) + _dot_nt(q_t[:, sl], s_prev.astype(BF16))
            s_scr[h] = s_prev * decay[:, sl] + _dot_tn(v, k_end[:, sl])
            o = o * lax.rsqrt(jnp.mean(o * o, axis=-1, keepdims=True) + EPS) * ggain_ref[...]
            gate = proj_ref[rows, OFF_GG + h * DV:OFF_GG + (h + 1) * DV]
            o_ref[rows, h * DV:(h + 1) * DV] = (o * _silu(gate)).astype(o_ref.dtype)

        cosb = cos_ref[rows, :]
        sinb = sin_ref[rows, :]
        rq = proj_ref[rows, OFF_RQ:OFF_RQ + HW]
        rqr = proj_ref[rows, OFF_RQR:OFF_RQR + HW]
        rk = proj_ref[rows, OFF_RK:OFF_RK + HW] * valid
        rkr = proj_ref[rows, OFF_RKR:OFF_RKR + HW] * valid
        q_r = (rq * cosb + rqr * sinb).astype(BF16)
        k_r = (rk * cosb + rkr * sinb) * (DK ** -0.5)
        for h in range(HEADS):
            sl = slice(h * LANES, (h + 1) * LANES)
            lg = LOG_GAMMA[h]
            dmat = jnp.where(causal, jnp.exp(lg * diff), 0.0)
            xi = jnp.exp(lg * (rowf + 1.0))
            zeta = jnp.exp(lg * (CHUNK - 1.0 - rowf))
            v = proj_ref[rows, OFF_RV + h * DV:OFF_RV + (h + 1) * DV].astype(BF16)
            k_h = k_r[:, sl]
            att = (_dot_nt(q_r[:, sl], k_h.astype(BF16)) * dmat).astype(BF16)
            r_prev = r_scr[h]
            o = _dot(att, v) + _dot_nt(q_r[:, sl], r_prev.astype(BF16)) * xi
            r_scr[h] = r_prev * math.exp(lg * CHUNK) + _dot_tn(v, (k_h * zeta).astype(BF16))
            mu = jnp.mean(o, axis=-1, keepdims=True)
            var = jnp.mean(jnp.square(o - mu), axis=-1, keepdims=True)
            o = (o - mu) * lax.rsqrt(var + EPS) * rgain_ref[...] + rbias_ref[...]
            gate = proj_ref[rows, OFF_RG + h * DV:OFF_RG + (h + 1) * DV]
            o_ref[rows, HW + h * DV:HW + (h + 1) * DV] = (o * _silu(gate)).astype(o_ref.dtype)
        return carry

    lax.fori_loop(0, rows_blk // CHUNK, chunk_body, 0)
    s_out_ref[...] = s_scr[...]
    r_out_ref[...] = r_scr[...]


def _mixer(proj, cosf, sinf, w2p, gbp, ggain, rgain, rbias, s0, r0, *, batch, n_invalid):
    rows = proj.shape[0] // batch
    rows_blk = min(256, rows)
    nblk = rows // rows_blk
    const2 = lambda b, j: (0, 0)
    const3 = lambda b, j: (0, 0, 0)
    state = jax.ShapeDtypeStruct((HEADS, DV, LANES), F32)
    return pl.pallas_call(
        functools.partial(_mixer_kernel, n_invalid=n_invalid, rows_blk=rows_blk),
        grid=(batch, nblk),
        in_specs=[
            pl.BlockSpec((rows_blk, PROJ_W), lambda b, j: (b * nblk + j, 0)),
            pl.BlockSpec((rows_blk, HW), lambda b, j: (j, 0)),
            pl.BlockSpec((rows_blk, HW), lambda b, j: (j, 0)),
            pl.BlockSpec((LANES, HW), const2),
            pl.BlockSpec((1, HW), const2),
            pl.BlockSpec((1, DV), const2),
            pl.BlockSpec((1, DV), const2),
            pl.BlockSpec((1, DV), const2),
            pl.BlockSpec((HEADS, DV, LANES), const3),
            pl.BlockSpec((HEADS, DV, LANES), const3),
        ],
        out_specs=[
            pl.BlockSpec((rows_blk, D_MODEL), lambda b, j: (b * nblk + j, 0)),
            pl.BlockSpec((HEADS, DV, LANES), const3),
            pl.BlockSpec((HEADS, DV, LANES), const3),
        ],
        out_shape=[jax.ShapeDtypeStruct((batch * rows, D_MODEL), BF16), state, state],
        scratch_shapes=[pltpu.VMEM((HEADS, DV, LANES), F32), pltpu.VMEM((HEADS, DV, LANES), F32)],
        compiler_params=pltpu.CompilerParams(
            dimension_semantics=("arbitrary", "arbitrary"), vmem_limit_bytes=VMEM_LIMIT),
        name="token_mixers",
    )(proj, cosf, sinf, w2p, gbp, ggain, rgain, rbias, s0, r0)


def _out_proj_kernel(o_ref, x_ref, w_ref, g_ref, h_ref, xn_ref):
    h = x_ref[...] + _dot(o_ref[...], w_ref[...])
    h_ref[...] = h
    xn_ref[...] = _rms(h, g_ref[...])


def _out_proj(o, x2d, w_out, gain2):
    n = x2d.shape[0]
    tm = min(512, n)
    blk = pl.BlockSpec((tm, D_MODEL), lambda i: (i, 0))
    return pl.pallas_call(
        _out_proj_kernel,
        grid=(n // tm,),
        in_specs=[blk, blk, pl.BlockSpec((D_MODEL, D_MODEL), lambda i: (0, 0)),
                  pl.BlockSpec((1, D_MODEL), lambda i: (0, 0))],
        out_specs=[blk, blk],
        out_shape=[jax.ShapeDtypeStruct((n, D_MODEL), F32)] * 2,
        compiler_params=pltpu.CompilerParams(
            dimension_semantics=("parallel",), vmem_limit_bytes=VMEM_LIMIT),
        name="out_proj_norm",
    )(o, x2d, w_out, gain2)


def _top_k_rows(arrays, payloads=None):
    arrays = list(arrays)
    n = arrays[0].shape[0]
    rowi = lax.broadcasted_iota(jnp.int32, arrays[0].shape, 0)
    vals = [[] for _ in arrays]
    picks = [[] for _ in arrays]
    for _ in range(PEER_TOPK):
        for k, s in enumerate(arrays):
            m = jnp.max(s, axis=0, keepdims=True)
            am = jnp.min(jnp.where(s == m, rowi, n), axis=0, keepdims=True)
            sel = rowi == am
            vals[k].append(m)
            if payloads is None:
                picks[k].append(am)
            else:
                picks[k].append(jnp.min(jnp.where(sel, payloads[k], jnp.iinfo(jnp.int32).max),
                                        axis=0, keepdims=True))
            arrays[k] = jnp.where(sel, -jnp.inf, s)
    return [(jnp.concatenate(v, axis=0), jnp.concatenate(p, axis=0)) for v, p in zip(vals, picks)]


def _candidate_cells():
    cells = [(a, b) for a in range(PEER_TOPK) for b in range(PEER_TOPK) if (a + 1) * (b + 1) <= PEER_TOPK]
    n = -(-len(cells) // 8) * 8
    sel_a = np.zeros((n, PEER_TOPK), np.float32)
    sel_b = np.zeros((n, PEER_TOPK), np.float32)
    pad = np.full((n, 1), -np.inf, np.float32)
    for r, (a, b) in enumerate(cells):
        sel_a[r, a] = 1.0
        sel_b[r, b] = 1.0
        pad[r, 0] = 0.0
    return sel_a, sel_b, pad


def _route_kernel(xn_ref, wq_ref, sk_ref, sa_ref, sb_ref, pad_ref, idx_ref, g_ref, q_scr, i_scr, g_scr):
    tr = xn_ref.shape[0]
    q = _dot(xn_ref[...].astype(BF16), wq_ref[...])
    for hc in range(2 * PEER_HEADS):
        q_scr[hc] = q[:, hc * LANES:(hc + 1) * LANES].astype(BF16)

    def sub_body(sb, carry):
        rows = pl.ds(pl.multiple_of(sb * LANES, LANES), LANES)

        def head_pair(hp, c2):
            heads = [2 * hp, 2 * hp + 1]
            scores = [_dot_nt(sk_ref[2 * h + c], q_scr[2 * h + c, rows, :])
                      for h in heads for c in range(2)]
            first = _top_k_rows(scores)
            sa, sb = sa_ref[...], sb_ref[...]
            cands, cidxs = [], []
            for k in range(len(heads)):
                (v0, i0), (v1, i1) = first[2 * k], first[2 * k + 1]
                cands.append((_dot(sa, v0, precision=lax.Precision.HIGHEST)
                              + _dot(sb, v1, precision=lax.Precision.HIGHEST)) + pad_ref[...])
                cidxs.append((_dot(sa, i0.astype(F32)) * PEER_NKEYS
                              + _dot(sb, i1.astype(F32))).astype(jnp.int32))
            for h, (best, e) in zip(heads, _top_k_rows(cands, cidxs)):
                ex = jnp.exp(best - best[0:1])
                i_scr[h] = e
                g_scr[h] = ex / jnp.sum(ex, axis=0, keepdims=True)
            return c2

        lax.fori_loop(0, PEER_HEADS // 2, head_pair, 0)
        idx_ref[rows, :] = i_scr[...].reshape(PEER_SEL, LANES).T
        g_ref[rows, :] = g_scr[...].reshape(PEER_SEL, LANES).T
        return carry

    lax.fori_loop(0, tr // LANES, sub_body, 0)


def _route(xn2, w_q, subkeys):
    n = xn2.shape[0]
    sel_a, sel_b, pad = _candidate_cells()
    tr = min(512, n)
    qw = 2 * PEER_HEADS * LANES
    return pl.pallas_call(
        _route_kernel,
        grid=(n // tr,),
        in_specs=[pl.BlockSpec((tr, D_MODEL), lambda i: (i, 0)),
                  pl.BlockSpec((D_MODEL, qw), lambda i: (0, 0)),
                  pl.BlockSpec((2 * PEER_HEADS, PEER_NKEYS, LANES), lambda i: (0, 0, 0)),
                  pl.BlockSpec(sel_a.shape, lambda i: (0, 0)),
                  pl.BlockSpec(sel_b.shape, lambda i: (0, 0)),
                  pl.BlockSpec(pad.shape, lambda i: (0, 0))],
        out_specs=[pl.BlockSpec((tr, PEER_SEL), lambda i: (i, 0))] * 2,
        out_shape=[jax.ShapeDtypeStruct((n, PEER_SEL), jnp.int32),
                   jax.ShapeDtypeStruct((n, PEER_SEL), F32)],
        scratch_shapes=[pltpu.VMEM((2 * PEER_HEADS, tr, LANES), BF16),
                        pltpu.VMEM((PEER_HEADS, PEER_TOPK, LANES), jnp.int32),
                        pltpu.VMEM((PEER_HEADS, PEER_TOPK, LANES), F32)],
        compiler_params=pltpu.CompilerParams(
            dimension_semantics=("parallel",), vmem_limit_bytes=VMEM_LIMIT),
        name="peer_route",
    )(xn2, w_q, subkeys, sel_a, sel_b, pad)


def _sc_worker_id():
    return lax.axis_index("s") * SC_CORES + lax.axis_index("c")


def _sc_mesh():
    return plsc.VectorSubcoreMesh(core_axis_name="c", subcore_axis_name="s")


def _sc_gather_ring(table_hbm, idx_v, bufs, sems, unit, compute):
    def gather(g, k):
        off = pl.multiple_of(g * SC_ROWS, SC_ROWS)
        return pltpu.make_async_copy(table_hbm.at[idx_v.at[pl.ds(off, SC_ROWS)]], bufs[k], sems[k])

    for k in range(SC_NBUF):
        gather(k, k).start()

    units_per_iter = SC_NBUF // unit

    def ring_iter(i, carry):
        for m in range(units_per_iter):
            ks = range(m * unit, (m + 1) * unit)
            for k in ks:
                gather(i * SC_NBUF + k, k).wait()
            compute(i * units_per_iter + m, [bufs[k] for k in ks])
            for k in ks:
                g_next = (i + 1) * SC_NBUF + k

                @pl.when(g_next < SC_GATHERS)
                def _():
                    gather(g_next, k).start()
        return carry

    lax.fori_loop(0, SC_GATHERS // SC_NBUF, ring_iter, 0)


def _pack_table(t):
    lo = lax.bitcast_convert_type(t[:, :HALF].astype(jnp.bfloat16), jnp.uint16).astype(jnp.int32)
    hi_bits = lax.bitcast_convert_type(t[:, HALF:], jnp.int32)
    sign = hi_bits & jnp.int32(-2 ** 31)
    mag = jnp.minimum(hi_bits & jnp.int32(2 ** 31 - 1), jnp.int32(0x7F000000))
    upper = jnp.maximum(mag + (1 << 15) - lo, 0) >> 16
    words = sign | (upper << 16) | lo
    return lax.bitcast_convert_type(words, jnp.uint32).reshape(t.shape[0], ROW_TILES, LANES)


def _unpack_words(w):
    lo = lax.bitcast_convert_type(w << 16, F32)
    hi = lax.bitcast_convert_type(w, F32)
    return lo, hi


def _sc_scratch(staged_cols, out_cols):
    return ([pltpu.VMEM((SC_TOK_BATCH * PEER_SEL,), jnp.int32),
             pltpu.VMEM((SC_TOK_BATCH, staged_cols), F32),
             pltpu.VMEM((SC_TOK_BATCH, out_cols), F32)]
            + [pltpu.VMEM((SC_ROWS, ROW_TILES, LANES), jnp.uint32) for _ in range(SC_NBUF)]
            + [pltpu.SemaphoreType.DMA for _ in range(SC_NBUF)])


def _sc_batches(tpw, idx_hbm, staged_hbm, out_hbm, idx_v, staged_v, out_v, run):
    tok0 = _sc_worker_id() * tpw

    def batch(tb, carry):
        t0 = pl.multiple_of(tok0 + tb * SC_TOK_BATCH, SC_TOK_BATCH)
        pltpu.sync_copy(idx_hbm.at[pl.ds(pl.multiple_of(t0 * PEER_SEL, SC_TOK_BATCH * PEER_SEL),
                                         SC_TOK_BATCH * PEER_SEL)], idx_v)
        pltpu.sync_copy(staged_hbm.at[pl.ds(t0, SC_TOK_BATCH)], staged_v)
        run()
        pltpu.sync_copy(out_v, out_hbm.at[pl.ds(t0, SC_TOK_BATCH)])
        return carry

    lax.fori_loop(0, tpw // SC_TOK_BATCH, batch, 0)


def _peer_dot_sc(u3, idx_flat, xn2):
    t = xn2.shape[0]
    unit = 16 // SC_ROWS
    units_per_tok = PEER_SEL // 16

    @functools.partial(
        pl.kernel,
        mesh=_sc_mesh(),
        out_type=jax.ShapeDtypeStruct((t, PEER_SEL * SC_LANES), F32),
        scratch_types=_sc_scratch(D_MODEL, PEER_SEL * SC_LANES),
        compiler_params=pltpu.CompilerParams(needs_layout_passes=False),
        name="peer_dot",
    )
    def body(u_hbm, idx_hbm, x_hbm, out_hbm, idx_v, x_v, out_v, *rest):
        bufs, sems = rest[:SC_NBUF], rest[SC_NBUF:]

        def compute(u, ub):
            tl = u // units_per_tok
            rb = (u % units_per_tok) * 16

            def cbody(c, acc):
                acc = list(acc)
                c4 = c // (LANES // SC_LANES)
                lane0 = pl.multiple_of((c % (LANES // SC_LANES)) * SC_LANES, SC_LANES)
                col = pl.multiple_of(c * SC_LANES, SC_LANES)
                x_lo = x_v[tl, pl.ds(col, SC_LANES)]
                x_hi = x_v[tl, pl.ds(HALF + col, SC_LANES)]
                for q, buf in enumerate(ub):
                    for r in range(SC_ROWS):
                        a = q * SC_ROWS + r
                        lo, hi = _unpack_words(buf[r, c4, pl.ds(lane0, SC_LANES)])
                        acc[a] = acc[a] + lo * x_lo + hi * x_hi
                return tuple(acc)

            acc = lax.fori_loop(0, HALF // SC_LANES, cbody,
                                tuple(jnp.zeros((SC_LANES,), F32) for _ in range(16)))
            for a in range(16):
                o = pl.multiple_of((rb + a) * SC_LANES, SC_LANES)
                out_v[tl, pl.ds(o, SC_LANES)] = acc[a]

        _sc_batches(t // SC_WORKERS, idx_hbm, x_hbm, out_hbm, idx_v, x_v, out_v,
                    lambda: _sc_gather_ring(u_hbm, idx_v, bufs, sems, unit, compute))

    return body(u3, idx_flat, xn2)


def _peer_combine_sc(v3, idx_flat, coef_b):
    t = coef_b.shape[0]
    unit = 32 // SC_ROWS
    units_per_tok = PEER_SEL // 32

    @functools.partial(
        pl.kernel,
        mesh=_sc_mesh(),
        out_type=jax.ShapeDtypeStruct((t, D_MODEL), F32),
        scratch_types=_sc_scratch(PEER_SEL * SC_LANES, D_MODEL),
        compiler_params=pltpu.CompilerParams(needs_layout_passes=False),
        name="peer_combine",
    )
    def body(v_hbm, idx_hbm, coef_hbm, out_hbm, idx_v, coef_v, out_v, *rest):
        bufs, sems = rest[:SC_NBUF], rest[SC_NBUF:]

        def compute(u, ub):
            tl = u // units_per_tok
            rb = (u % units_per_tok) * 32
            for g in range(ROW_TILES):
                cols = ([pl.ds(g * LANES + l * SC_LANES, SC_LANES) for l in range(8)]
                        + [pl.ds(HALF + g * LANES + l * SC_LANES, SC_LANES) for l in range(8)])

                def rbody(r, acc):
                    acc = list(acc)
                    for q, buf in enumerate(ub):
                        co = pl.multiple_of((rb + q * SC_ROWS + r) * SC_LANES, SC_LANES)
                        cf = coef_v[tl, pl.ds(co, SC_LANES)]
                        for l in range(8):
                            lo, hi = _unpack_words(buf[r, g, pl.ds(l * SC_LANES, SC_LANES)])
                            acc[l] = acc[l] + lo * cf
                            acc[8 + l] = acc[8 + l] + hi * cf
                    return tuple(acc)

                acc = lax.fori_loop(0, SC_ROWS, rbody, tuple(out_v[tl, c] for c in cols))
                for k in range(16):
                    out_v[tl, cols[k]] = acc[k]

        def run():
            def zero(tl, c2):
                for k in range(D_MODEL // SC_LANES):
                    out_v[tl, pl.ds(k * SC_LANES, SC_LANES)] = jnp.zeros((SC_LANES,), F32)
                return c2

            lax.fori_loop(0, SC_TOK_BATCH, zero, 0)
            _sc_gather_ring(v_hbm, idx_v, bufs, sems, unit, compute)

        _sc_batches(t // SC_WORKERS, idx_hbm, coef_hbm, out_hbm, idx_v, coef_v, out_v, run)

    return body(v3, idx_flat, coef_b)


def _coef_kernel(part_ref, g_ref, e_ref, coef_ref):
    hp = lax.Precision.HIGHEST
    e = e_ref[...]
    act = lax.dot_general(part_ref[...], e, (((1,), (1,)), ((), ())),
                          precision=hp, preferred_element_type=F32)
    gelu = 0.5 * act * (1.0 + lax.erf(act * np.float32(math.sqrt(0.5))))
    coef_ref[...] = _dot(g_ref[...] * gelu, e, precision=hp)


def _coef(part, g):
    n = part.shape[0]
    tr = min(512, n)
    w = PEER_SEL * SC_LANES
    expand = (jnp.arange(w, dtype=jnp.int32)[None, :] // SC_LANES
              == jnp.arange(PEER_SEL, dtype=jnp.int32)[:, None]).astype(F32)
    return pl.pallas_call(
        _coef_kernel,
        grid=(n // tr,),
        in_specs=[pl.BlockSpec((tr, w), lambda i: (i, 0)),
                  pl.BlockSpec((tr, PEER_SEL), lambda i: (i, 0)),
                  pl.BlockSpec((PEER_SEL, w), lambda i: (0, 0))],
        out_specs=pl.BlockSpec((tr, w), lambda i: (i, 0)),
        out_shape=jax.ShapeDtypeStruct((n, w), F32),
        compiler_params=pltpu.CompilerParams(
            dimension_semantics=("parallel",), vmem_limit_bytes=VMEM_LIMIT),
        name="peer_coef",
    )(part, g, expand)


def _final_kernel(h_ref, p_ref, g_ref, o_ref):
    o_ref[...] = _rms(h_ref[...] + p_ref[...], g_ref[...])


def _final(h2, peer_out, gain):
    n = h2.shape[0]
    tm = min(1024, n)
    blk = pl.BlockSpec((tm, D_MODEL), lambda i: (i, 0))
    return pl.pallas_call(
        _final_kernel,
        grid=(n // tm,),
        in_specs=[blk, blk, pl.BlockSpec((1, D_MODEL), lambda i: (0, 0))],
        out_specs=blk,
        out_shape=jax.ShapeDtypeStruct((n, D_MODEL), F32),
        compiler_params=pltpu.CompilerParams(
            dimension_semantics=("parallel",), vmem_limit_bytes=VMEM_LIMIT),
        name="final_norm",
    )(h2, peer_out, gain)


def _pad_heads(w, dk):
    lead = w.shape[:-1]
    w = w.reshape(lead + (HEADS, dk))
    w = jnp.pad(w, [(0, 0)] * len(lead) + [(0, 0), (0, LANES - dk)])
    return w.reshape(lead + (HEADS * LANES,))


def _rot_heads(w, dk):
    lead = w.shape[:-1]
    w = w.reshape(lead + (HEADS, 2, dk // 2))
    w = jnp.stack([-w[..., 1, :], w[..., 0, :]], axis=-2)
    return w.reshape(lead + (HEADS * dk,))


def _pack_in_weights(w_in):
    gq, gk, gv, gg, ga, rq, rk, rv, rg = jnp.split(w_in, np.cumsum(IN_SPLITS)[:-1].tolist(), axis=-1)
    ga = jnp.pad(ga, ((0, 0), (0, 2 * LANES - GATE_RANK)))
    return jnp.concatenate(
        [_pad_heads(gq, DK), _pad_heads(gk, DK), gv, gg,
         _pad_heads(rq, DK), _pad_heads(rk, DK), rv, rg,
         _pad_heads(_rot_heads(rq, DK), DK), _pad_heads(_rot_heads(rk, DK), DK), ga],
        axis=-1).astype(BF16)


def _rope_tables(pos):
    inv_freq = ROPE_BASE ** (-jnp.arange(0, DK, 2, dtype=F32) / DK)
    ang = pos[:, None] * inv_freq[None, :]
    full = lambda t: _pad_heads(jnp.tile(t, (1, 2 * HEADS)), DK)
    return full(jnp.cos(ang)), full(jnp.sin(ang))


def kernel(x, meta_tokens, norm1_gain, w_in, gla_gate_w2, gla_gate_b, gla_norm_gain, ret_norm_gain, ret_norm_bias, w_out, norm2_gain, peer_w_q, peer_subkeys, peer_u, peer_v, final_gain):
    b, s, d = x.shape
    assert d == D_MODEL and (s // SEGS_PER_SEQ) % (SC_WORKERS * SC_TOK_BATCH) == 0
    assert norm1_gain.shape[0] == 1, "single-layer block"
    x2d = x.reshape(b * s, d)

    w_all = _pack_in_weights(w_in[0])
    w2p = jnp.pad(_pad_heads(gla_gate_w2[0], DK), ((0, LANES - GATE_RANK), (0, 0))).astype(BF16)
    gbp = _pad_heads(gla_gate_b[0][None, :], DK)
    g1 = norm1_gain[0][None, :]
    ggain = gla_norm_gain[0][None, :]
    rgain = ret_norm_gain[0][None, :]
    rbias = ret_norm_bias[0][None, :]

    h0 = jnp.concatenate([jnp.zeros((N_PAD, d), x.dtype), meta_tokens.astype(x.dtype)], axis=0)
    cos0, sin0 = _rope_tables(jnp.arange(CHUNK, dtype=F32) - N_PAD)
    cos1, sin1 = _rope_tables(jnp.arange(s, dtype=F32) + N_META)
    zero_state = jnp.zeros((HEADS, DV, LANES), F32)
    _, s_meta, r_meta = _mixer(_norm_proj(h0, g1, w_all), cos0, sin0, w2p, gbp, ggain, rgain, rbias,
                               zero_state, zero_state, batch=1, n_invalid=N_PAD)

    w_out_b = w_out[0].astype(BF16)
    g2 = norm2_gain[0][None, :]
    w_q = peer_w_q[0].astype(BF16)
    subkeys = peer_subkeys[0].reshape(2 * PEER_HEADS, PEER_NKEYS, PEER_DKEY // 2).astype(BF16)
    u3 = _pack_table(peer_u[0])
    v3 = _pack_table(peer_v[0])
    gf = final_gain[None, :]

    seg = s // SEGS_PER_SEQ
    outs = []
    for bi in range(b):
        s_state, r_state = s_meta, r_meta
        for si in range(SEGS_PER_SEQ):
            xs = x2d[bi * s + si * seg:bi * s + (si + 1) * seg]
            pos = slice(si * seg, (si + 1) * seg)
            o, s_state, r_state = _mixer(_norm_proj(xs, g1, w_all), cos1[pos], sin1[pos], w2p, gbp,
                                         ggain, rgain, rbias, s_state, r_state, batch=1, n_invalid=0)
            h2, xn2 = _out_proj(o, xs, w_out_b, g2)
            idx, g = _route(xn2, w_q, subkeys)
            idx_flat = idx.reshape(-1)
            part = _peer_dot_sc(u3, idx_flat, xn2)
            peer_out = _peer_combine_sc(v3, idx_flat, _coef(part, g))
            outs.append(_final(h2, peer_out, gf))
    return jnp.concatenate(outs, axis=0).reshape(b, s, d)
```

```python
import functools
import math

import jax
import jax.numpy as jnp
import numpy as np
from jax import lax
from jax.experimental import pallas as pl
from jax.experimental.pallas import tpu as pltpu
from jax.experimental.pallas import tpu_sc as plsc

F32 = jnp.float32
BF16 = jnp.bfloat16

D_MODEL = 1024
N_META = 16
CHUNK = 64
N_PAD = CHUNK - N_META
EPS = 1e-6

HEADS = 4
DV = 128
DK = 64
LANES = 128
GATE_RANK = 16
GATE_TAU = 16.0
ROPE_BASE = 10000.0
IN_SPLITS = (HEADS * DK, HEADS * DK, HEADS * DV, HEADS * DV, GATE_RANK,
             HEADS * DK, HEADS * DK, HEADS * DV, HEADS * DV)

PEER_HEADS = 8
PEER_NKEYS = 128
PEER_DKEY = 256
PEER_TOPK = 16
PEER_SEL = PEER_HEADS * PEER_TOPK

HW = HEADS * LANES
OFF_GQ, OFF_GK, OFF_GV, OFF_GG = 0, HW, 2 * HW, 3 * HW
OFF_RQ, OFF_RK, OFF_RV, OFF_RG = 4 * HW, 5 * HW, 6 * HW, 7 * HW
OFF_RQR, OFF_RKR, OFF_GA = 8 * HW, 9 * HW, 10 * HW
PROJ_W = 10 * HW + 2 * LANES
PROJ_TN = PROJ_W // 3

LOG_GAMMA = tuple(math.log(1.0 - 2.0 ** (-5.0 - h)) for h in range(HEADS))

SC_CORES, SC_SUBCORES, SC_LANES = 2, 16, 16
SC_WORKERS = SC_CORES * SC_SUBCORES
SC_TOK_BATCH = 64
SC_ROWS = 8
SC_NBUF = 8
SC_GATHERS = SC_TOK_BATCH * PEER_SEL // SC_ROWS
HALF = D_MODEL // 2
ROW_TILES = HALF // LANES

SEGS_PER_SEQ = 2

VMEM_LIMIT = 48 * 1024 * 1024


def _dot(a, b, **kw):
    return jnp.dot(a, b, preferred_element_type=F32, **kw)


def _dot_nt(a, b):
    return lax.dot_general(a, b, (((1,), (1,)), ((), ())), preferred_element_type=F32)


def _dot_tn(a, b):
    return lax.dot_general(a, b, (((0,), (0,)), ((), ())), preferred_element_type=F32)


def _rms(x, g):
    return x * lax.rsqrt(jnp.mean(x * x, axis=-1, keepdims=True) + EPS) * g


def _silu(x):
    return x * jax.nn.sigmoid(x)


def _norm_proj_kernel(x_ref, g_ref, w_ref, o_ref):
    xn = _rms(x_ref[...], g_ref[...]).astype(BF16)
    o_ref[...] = _dot(xn, w_ref[...])


def _norm_proj(x2d, gain, w_all):
    n = x2d.shape[0]
    tm = min(512, n)
    return pl.pallas_call(
        _norm_proj_kernel,
        grid=(n // tm, PROJ_W // PROJ_TN),
        in_specs=[
            pl.BlockSpec((tm, D_MODEL), lambda i, j: (i, 0)),
            pl.BlockSpec((1, D_MODEL), lambda i, j: (0, 0)),
            pl.BlockSpec((D_MODEL, PROJ_TN), lambda i, j: (0, j)),
        ],
        out_specs=pl.BlockSpec((tm, PROJ_TN), lambda i, j: (i, j)),
        out_shape=jax.ShapeDtypeStruct((n, PROJ_W), F32),
        compiler_params=pltpu.CompilerParams(
            dimension_semantics=("parallel", "parallel"), vmem_limit_bytes=VMEM_LIMIT),
        name="norm_in_proj",
    )(x2d, gain, w_all)


def _mixer_kernel(proj_ref, cos_ref, sin_ref, w2_ref, gb_ref, ggain_ref, rgain_ref, rbias_ref,
                  s0_ref, r0_ref, o_ref, s_out_ref, r_out_ref, s_scr, r_scr, *, n_invalid, rows_blk):
    j = pl.program_id(1)

    @pl.when(j == 0)
    def _():
        s_scr[...] = s0_ref[...]
        r_scr[...] = r0_ref[...]

    row = lax.broadcasted_iota(jnp.int32, (CHUNK, CHUNK), 0)
    col = lax.broadcasted_iota(jnp.int32, (CHUNK, CHUNK), 1)
    causal = row >= col
    tril = causal.astype(F32)
    diff = jnp.maximum(row - col, 0).astype(F32)
    rowf = lax.broadcasted_iota(jnp.int32, (CHUNK, LANES), 0).astype(F32)
    row1 = lax.broadcasted_iota(jnp.int32, (CHUNK, 1), 0)

    def chunk_body(c, carry):
        r0 = pl.multiple_of(c * CHUNK, CHUNK)
        rows = pl.ds(r0, CHUNK)
        valid = ((j * rows_blk + r0 + row1) >= n_invalid).astype(F32)

        ga = proj_ref[rows, OFF_GA:OFF_GA + LANES].astype(BF16)
        pre = _dot(ga, w2_ref[...]) + gb_ref[...]
        log_a = jax.nn.log_sigmoid(pre) * (1.0 / GATE_TAU)
        bcum = _dot(tril, log_a, precision=lax.Precision.HIGHEST)
        b_last = bcum[CHUNK - 1:CHUNK, :]
        gq = proj_ref[rows, OFF_GQ:OFF_GQ + HW] * (DK ** -0.5)
        gk = proj_ref[rows, OFF_GK:OFF_GK + HW] * valid
        q_t = (gq * jnp.exp(bcum)).astype(BF16)
        k_t = (gk * jnp.exp(-bcum)).astype(BF16)
        k_end = (gk * jnp.exp(b_last - bcum)).astype(BF16)
        decay = jnp.exp(b_last)
        for h in range(HEADS):
            sl = slice(h * LANES, (h + 1) * LANES)
            v = proj_ref[rows, OFF_GV + h * DV:OFF_GV + (h + 1) * DV].astype(BF16)
            att = jnp.where(causal, _dot_nt(q_t[:, sl], k_t[:, sl]), 0.0).astype(BF16)
            s_prev = s_scr[h]
            o = _dot(att, v) + _dot_nt(q_t[:, sl], s_prev.astype(BF16))
            s_scr[h] = s_prev * decay[:, sl] + _dot_tn(v, k_end[:, sl])
            o = o * lax.rsqrt(jnp.mean(o * o, axis=-1, keepdims=True) + EPS) * ggain_ref[...]
            gate = proj_ref[rows, OFF_GG + h * DV:OFF_GG + (h + 1) * DV]
            o_ref[rows, h * DV:(h + 1) * DV] = (o * _silu(gate)).astype(o_ref.dtype)

        cosb = cos_ref[rows, :]
        sinb = sin_ref[rows, :]
        rq = proj_ref[rows, OFF_RQ:OFF_RQ + HW]
        rqr = proj_ref[rows, OFF_RQR:OFF_RQR + HW]
        rk = proj_ref[rows, OFF_RK:OFF_RK + HW] * valid
        rkr = proj_ref[rows, OFF_RKR:OFF_RKR + HW] * valid
        q_r = (rq * cosb + rqr * sinb).astype(BF16)
        k_r = (rk * cosb + rkr * sinb) * (DK ** -0.5)
        for h in range(HEADS):
            sl = slice(h * LANES, (h + 1) * LANES)
            lg = LOG_GAMMA[h]
            dmat = jnp.where(causal, jnp.exp(lg * diff), 0.0)
            xi = jnp.exp(lg * (rowf + 1.0))
            zeta = jnp.exp(lg * (CHUNK - 1.0 - rowf))
            v = proj_ref[rows, OFF_RV + h * DV:OFF_RV + (h + 1) * DV].astype(BF16)
            k_h = k_r[:, sl]
            att = (_dot_nt(q_r[:, sl], k_h.astype(BF16)) * dmat).astype(BF16)
            r_prev = r_scr[h]
            o = _dot(att, v) + _dot_nt(q_r[:, sl], r_prev.astype(BF16)) * xi
            r_scr[h] = r_prev * math.exp(lg * CHUNK) + _dot_tn(v, (k_h * zeta).astype(BF16))
            mu = jnp.mean(o, axis=-1, keepdims=True)
            var = jnp.mean(jnp.square(o - mu), axis=-1, keepdims=True)
            o = (o - mu) * lax.rsqrt(var + EPS) * rgain_ref[...] + rbias_ref[...]
            gate = proj_ref[rows, OFF_RG + h * DV:OFF_RG + (h + 1) * DV]
            o_ref[rows, HW + h * DV:HW + (h + 1) * DV] = (o * _silu(gate)).astype(o_ref.dtype)
        return carry

    lax.fori_loop(0, rows_blk // CHUNK, chunk_body, 0)
    s_out_ref[...] = s_scr[...]
    r_out_ref[...] = r_scr[...]


def _mixer(proj, cosf, sinf, w2p, gbp, ggain, rgain, rbias, s0, r0, *, batch, n_invalid):
    rows = proj.shape[0] // batch
    rows_blk = min(256, rows)
    nblk = rows // rows_blk
    const2 = lambda b, j: (0, 0)
    const3 = lambda b, j: (0, 0, 0)
    state = jax.ShapeDtypeStruct((HEADS, DV, LANES), F32)
    return pl.pallas_call(
        functools.partial(_mixer_kernel, n_invalid=n_invalid, rows_blk=rows_blk),
        grid=(batch, nblk),
        in_specs=[
            pl.BlockSpec((rows_blk, PROJ_W), lambda b, j: (b * nblk + j, 0)),
            pl.BlockSpec((rows_blk, HW), lambda b, j: (j, 0)),
            pl.BlockSpec((rows_blk, HW), lambda b, j: (j, 0)),
            pl.BlockSpec((LANES, HW), const2),
            pl.BlockSpec((1, HW), const2),
            pl.BlockSpec((1, DV), const2),
            pl.BlockSpec((1, DV), const2),
            pl.BlockSpec((1, DV), const2),
            pl.BlockSpec((HEADS, DV, LANES), const3),
            pl.BlockSpec((HEADS, DV, LANES), const3),
        ],
        out_specs=[
            pl.BlockSpec((rows_blk, D_MODEL), lambda b, j: (b * nblk + j, 0)),
            pl.BlockSpec((HEADS, DV, LANES), const3),
            pl.BlockSpec((HEADS, DV, LANES), const3),
        ],
        out_shape=[jax.ShapeDtypeStruct((batch * rows, D_MODEL), BF16), state, state],
        scratch_shapes=[pltpu.VMEM((HEADS, DV, LANES), F32), pltpu.VMEM((HEADS, DV, LANES), F32)],
        compiler_params=pltpu.CompilerParams(
            dimension_semantics=("arbitrary", "arbitrary"), vmem_limit_bytes=VMEM_LIMIT),
        name="token_mixers",
    )(proj, cosf, sinf, w2p, gbp, ggain, rgain, rbias, s0, r0)


def _out_proj_kernel(o_ref, x_ref, w_ref, g_ref, h_ref, xn_ref):
    h = x_ref[...] + _dot(o_ref[...], w_ref[...])
    h_ref[...] = h
    xn_ref[...] = _rms(h, g_ref[...])


def _out_proj(o, x2d, w_out, gain2):
    n = x2d.shape[0]
    tm = min(512, n)
    blk = pl.BlockSpec((tm, D_MODEL), lambda i: (i, 0))
    return pl.pallas_call(
        _out_proj_kernel,
        grid=(n // tm,),
        in_specs=[blk, blk, pl.BlockSpec((D_MODEL, D_MODEL), lambda i: (0, 0)),
                  pl.BlockSpec((1, D_MODEL), lambda i: (0, 0))],
        out_specs=[blk, blk],
        out_shape=[jax.ShapeDtypeStruct((n, D_MODEL), F32)] * 2,
        compiler_params=pltpu.CompilerParams(
            dimension_semantics=("parallel",), vmem_limit_bytes=VMEM_LIMIT),
        name="out_proj_norm",
    )(o, x2d, w_out, gain2)


def _top_k_rows(arrays, payloads=None):
    arrays = list(arrays)
    n = arrays[0].shape[0]
    rowi = lax.broadcasted_iota(jnp.int32, arrays[0].shape, 0)
    vals = [[] for _ in arrays]
    picks = [[] for _ in arrays]
    for _ in range(PEER_TOPK):
        for k, s in enumerate(arrays):
            m = jnp.max(s, axis=0, keepdims=True)
            am = jnp.min(jnp.where(s == m, rowi, n), axis=0, keepdims=True)
            sel = rowi == am
            vals[k].append(m)
            if payloads is None:
                picks[k].append(am)
            else:
                picks[k].append(jnp.min(jnp.where(sel, payloads[k], jnp.iinfo(jnp.int32).max),
                                        axis=0, keepdims=True))
            arrays[k] = jnp.where(sel, -jnp.inf, s)
    return [(jnp.concatenate(v, axis=0), jnp.concatenate(p, axis=0)) for v, p in zip(vals, picks)]


def _candidate_cells():
    cells = [(a, b) for a in range(PEER_TOPK) for b in range(PEER_TOPK) if (a + 1) * (b + 1) <= PEER_TOPK]
    n = -(-len(cells) // 8) * 8
    sel_a = np.zeros((n, PEER_TOPK), np.float32)
    sel_b = np.zeros((n, PEER_TOPK), np.float32)
    pad = np.full((n, 1), -np.inf, np.float32)
    for r, (a, b) in enumerate(cells):
        sel_a[r, a] = 1.0
        sel_b[r, b] = 1.0
        pad[r, 0] = 0.0
    return sel_a, sel_b, pad


def _route_kernel(xn_ref, wq_ref, sk_ref, sa_ref, sb_ref, pad_ref, idx_ref, g_ref, q_scr, i_scr, g_scr):
    tr = xn_ref.shape[0]
    q = _dot(xn_ref[...].astype(BF16), wq_ref[...])
    for hc in range(2 * PEER_HEADS):
        q_scr[hc] = q[:, hc * LANES:(hc + 1) * LANES].astype(BF16)

    def sub_body(sb, carry):
        rows = pl.ds(pl.multiple_of(sb * LANES, LANES), LANES)

        def head_pair(hp, c2):
            heads = [2 * hp, 2 * hp + 1]
            scores = [_dot_nt(sk_ref[2 * h + c], q_scr[2 * h + c, rows, :])
                      for h in heads for c in range(2)]
            first = _top_k_rows(scores)
            sa, sb = sa_ref[...], sb_ref[...]
            cands, cidxs = [], []
            for k in range(len(heads)):
                (v0, i0), (v1, i1) = first[2 * k], first[2 * k + 1]
                cands.append((_dot(sa, v0, precision=lax.Precision.HIGHEST)
                              + _dot(sb, v1, precision=lax.Precision.HIGHEST)) + pad_ref[...])
                cidxs.append((_dot(sa, i0.astype(F32)) * PEER_NKEYS
                              + _dot(sb, i1.astype(F32))).astype(jnp.int32))
            for h, (best, e) in zip(heads, _top_k_rows(cands, cidxs)):
                ex = jnp.exp(best - best[0:1])
                i_scr[h] = e
                g_scr[h] = ex / jnp.sum(ex, axis=0, keepdims=True)
            return c2

        lax.fori_loop(0, PEER_HEADS // 2, head_pair, 0)
        idx_ref[rows, :] = i_scr[...].reshape(PEER_SEL, LANES).T
        g_ref[rows, :] = g_scr[...].reshape(PEER_SEL, LANES).T
        return carry

    lax.fori_loop(0, tr // LANES, sub_body, 0)


def _route(xn2, w_q, subkeys):
    n = xn2.shape[0]
    sel_a, sel_b, pad = _candidate_cells()
    tr = min(512, n)
    qw = 2 * PEER_HEADS * LANES
    return pl.pallas_call(
        _route_kernel,
        grid=(n // tr,),
        in_specs=[pl.BlockSpec((tr, D_MODEL), lambda i: (i, 0)),
                  pl.BlockSpec((D_MODEL, qw), lambda i: (0, 0)),
                  pl.BlockSpec((2 * PEER_HEADS, PEER_NKEYS, LANES), lambda i: (0, 0, 0)),
                  pl.BlockSpec(sel_a.shape, lambda i: (0, 0)),
                  pl.BlockSpec(sel_b.shape, lambda i: (0, 0)),
                  pl.BlockSpec(pad.shape, lambda i: (0, 0))],
        out_specs=[pl.BlockSpec((tr, PEER_SEL), lambda i: (i, 0))] * 2,
        out_shape=[jax.ShapeDtypeStruct((n, PEER_SEL), jnp.int32),
                   jax.ShapeDtypeStruct((n, PEER_SEL), F32)],
        scratch_shapes=[pltpu.VMEM((2 * PEER_HEADS, tr, LANES), BF16),
                        pltpu.VMEM((PEER_HEADS, PEER_TOPK, LANES), jnp.int32),
                        pltpu.VMEM((PEER_HEADS, PEER_TOPK, LANES), F32)],
        compiler_params=pltpu.CompilerParams(
            dimension_semantics=("parallel",), vmem_limit_bytes=VMEM_LIMIT),
        name="peer_route",
    )(xn2, w_q, subkeys, sel_a, sel_b, pad)


def _sc_worker_id():
    return lax.axis_index("s") * SC_CORES + lax.axis_index("c")


def _sc_mesh():
    return plsc.VectorSubcoreMesh(core_axis_name="c", subcore_axis_name="s")


def _sc_gather_ring(table_hbm, idx_v, bufs, sems, unit, compute):
    def gather(g, k):
        off = pl.multiple_of(g * SC_ROWS, SC_ROWS)
        return pltpu.make_async_copy(table_hbm.at[idx_v.at[pl.ds(off, SC_ROWS)]], bufs[k], sems[k])

    for k in range(SC_NBUF):
        gather(k, k).start()

    units_per_iter = SC_NBUF // unit

    def ring_iter(i, carry):
        for m in range(units_per_iter):
            ks = range(m * unit, (m + 1) * unit)
            for k in ks:
                gather(i * SC_NBUF + k, k).wait()
            compute(i * units_per_iter + m, [bufs[k] for k in ks])
            for k in ks:
                g_next = (i + 1) * SC_NBUF + k

                @pl.when(g_next < SC_GATHERS)
                def _():
                    gather(g_next, k).start()
        return carry

    lax.fori_loop(0, SC_GATHERS // SC_NBUF, ring_iter, 0)


def _pack_table(t):
    lo = lax.bitcast_convert_type(t[:, :HALF].astype(jnp.bfloat16), jnp.uint16).astype(jnp.int32)
    hi_bits = lax.bitcast_convert_type(t[:, HALF:], jnp.int32)
    sign = hi_bits & jnp.int32(-2 ** 31)
    mag = jnp.minimum(hi_bits & jnp.int32(2 ** 31 - 1), jnp.int32(0x7F000000))
    upper = jnp.maximum(mag + (1 << 15) - lo, 0) >> 16
    words = sign | (upper << 16) | lo
    return lax.bitcast_convert_type(words, jnp.uint32).reshape(t.shape[0], ROW_TILES, LANES)


def _unpack_words(w):
    lo = lax.bitcast_convert_type(w << 16, F32)
    hi = lax.bitcast_convert_type(w, F32)
    return lo, hi


def _sc_scratch(staged_cols, out_cols):
    return ([pltpu.VMEM((SC_TOK_BATCH * PEER_SEL,), jnp.int32),
             pltpu.VMEM((SC_TOK_BATCH, staged_cols), F32),
             pltpu.VMEM((SC_TOK_BATCH, out_cols), F32)]
            + [pltpu.VMEM((SC_ROWS, ROW_TILES, LANES), jnp.uint32) for _ in range(SC_NBUF)]
            + [pltpu.SemaphoreType.DMA for _ in range(SC_NBUF)])


def _sc_batches(tpw, idx_hbm, staged_hbm, out_hbm, idx_v, staged_v, out_v, run):
    tok0 = _sc_worker_id() * tpw

    def batch(tb, carry):
        t0 = pl.multiple_of(tok0 + tb * SC_TOK_BATCH, SC_TOK_BATCH)
        pltpu.sync_copy(idx_hbm.at[pl.ds(pl.multiple_of(t0 * PEER_SEL, SC_TOK_BATCH * PEER_SEL),
                                         SC_TOK_BATCH * PEER_SEL)], idx_v)
        pltpu.sync_copy(staged_hbm.at[pl.ds(t0, SC_TOK_BATCH)], staged_v)
        run()
        pltpu.sync_copy(out_v, out_hbm.at[pl.ds(t0, SC_TOK_BATCH)])
        return carry

    lax.fori_loop(0, tpw // SC_TOK_BATCH, batch, 0)


def _peer_dot_sc(u3, idx_flat, xn2):
    t = xn2.shape[0]
    unit = 16 // SC_ROWS
    units_per_tok = PEER_SEL // 16

    @functools.partial(
        pl.kernel,
        mesh=_sc_mesh(),
        out_type=jax.ShapeDtypeStruct((t, PEER_SEL), F32),
        scratch_types=[pltpu.VMEM((16 * SC_LANES,), F32)] + _sc_scratch(D_MODEL, PEER_SEL),
        compiler_params=pltpu.CompilerParams(needs_layout_passes=False),
        name="peer_dot",
    )
    def body(u_hbm, idx_hbm, x_hbm, out_hbm, red_v, idx_v, x_v, out_v, *rest):
        bufs, sems = rest[:SC_NBUF], rest[SC_NBUF:]

        def compute(u, ub):
            tl = u // units_per_tok
            rb = (u % units_per_tok) * 16

            def cbody(c, acc):
                acc = list(acc)
                c4 = c // (LANES // SC_LANES)
                lane0 = pl.multiple_of((c % (LANES // SC_LANES)) * SC_LANES, SC_LANES)
                col = pl.multiple_of(c * SC_LANES, SC_LANES)
                x_lo = x_v[tl, pl.ds(col, SC_LANES)]
                x_hi = x_v[tl, pl.ds(HALF + col, SC_LANES)]
                for q, buf in enumerate(ub):
                    for r in range(SC_ROWS):
                        a = q * SC_ROWS + r
                        lo, hi = _unpack_words(buf[r, c4, pl.ds(lane0, SC_LANES)])
                        acc[a] = acc[a] + lo * x_lo + hi * x_hi
                return tuple(acc)

            acc = lax.fori_loop(0, HALF // SC_LANES, cbody,
                                tuple(jnp.zeros((SC_LANES,), F32) for _ in range(16)))
            for a in range(16):
                red_v[pl.ds(a * SC_LANES, SC_LANES)] = acc[a]
            lane_base = lax.iota(jnp.int32, SC_LANES) * SC_LANES
            tot = plsc.load_gather(red_v, [lane_base])
            for k in range(1, SC_LANES):
                tot = tot + plsc.load_gather(red_v, [lane_base + k])
            out_v[tl, pl.ds(pl.multiple_of(rb, 16), 16)] = tot

        _sc_batches(t // SC_WORKERS, idx_hbm, x_hbm, out_hbm, idx_v, x_v, out_v,
                    lambda: _sc_gather_ring(u_hbm, idx_v, bufs, sems, unit, compute))

    return body(u3, idx_flat, xn2)


def _peer_combine_sc(v3, idx_flat, coef_b):
    t = coef_b.shape[0]
    unit = 32 // SC_ROWS
    units_per_tok = PEER_SEL // 32

    @functools.partial(
        pl.kernel,
        mesh=_sc_mesh(),
        out_type=jax.ShapeDtypeStruct((t, D_MODEL), F32),
        scratch_types=_sc_scratch(PEER_SEL, D_MODEL),
        compiler_params=pltpu.CompilerParams(needs_layout_passes=False),
        name="peer_combine",
    )
    def body(v_hbm, idx_hbm, coef_hbm, out_hbm, idx_v, coef_v, out_v, *rest):
        bufs, sems = rest[:SC_NBUF], rest[SC_NBUF:]

        def compute(u, ub):
            tl = u // units_per_tok
            rb = (u % units_per_tok) * 32
            for g in range(ROW_TILES):
                cols = ([pl.ds(g * LANES + l * SC_LANES, SC_LANES) for l in range(8)]
                        + [pl.ds(HALF + g * LANES + l * SC_LANES, SC_LANES) for l in range(8)])

                def rbody(r, acc):
                    acc = list(acc)
                    for q, buf in enumerate(ub):
                        cf = plsc.load_gather(coef_v, [jnp.full((SC_LANES,), tl, jnp.int32),
                                                       jnp.full((SC_LANES,), rb + q * SC_ROWS + r, jnp.int32)])
                        for l in range(8):
                            lo, hi = _unpack_words(buf[r, g, pl.ds(l * SC_LANES, SC_LANES)])
                            acc[l] = acc[l] + lo * cf
                            acc[8 + l] = acc[8 + l] + hi * cf
                    return tuple(acc)

                acc = lax.fori_loop(0, SC_ROWS, rbody, tuple(out_v[tl, c] for c in cols))
                for k in range(16):
                    out_v[tl, cols[k]] = acc[k]

        def run():
            def zero(tl, c2):
                for k in range(D_MODEL // SC_LANES):
                    out_v[tl, pl.ds(k * SC_LANES, SC_LANES)] = jnp.zeros((SC_LANES,), F32)
                return c2

            lax.fori_loop(0, SC_TOK_BATCH, zero, 0)
            _sc_gather_ring(v_hbm, idx_v, bufs, sems, unit, compute)

        _sc_batches(t // SC_WORKERS, idx_hbm, coef_hbm, out_hbm, idx_v, coef_v, out_v, run)

    return body(v3, idx_flat, coef_b)


def _coef_kernel(act_ref, g_ref, coef_ref):
    act = act_ref[...]
    gelu = 0.5 * act * (1.0 + lax.erf(act * np.float32(math.sqrt(0.5))))
    coef_ref[...] = g_ref[...] * gelu


def _coef(act, g):
    n = act.shape[0]
    tr = min(1024, n)
    blk = pl.BlockSpec((tr, PEER_SEL), lambda i: (i, 0))
    return pl.pallas_call(
        _coef_kernel,
        grid=(n // tr,),
        in_specs=[blk, blk],
        out_specs=blk,
        out_shape=jax.ShapeDtypeStruct((n, PEER_SEL), F32),
        compiler_params=pltpu.CompilerParams(
            dimension_semantics=("parallel",), vmem_limit_bytes=VMEM_LIMIT),
        name="peer_coef",
    )(act, g)


def _final_kernel(h_ref, p_ref, g_ref, o_ref):
    o_ref[...] = _rms(h_ref[...] + p_ref[...], g_ref[...])


def _final(h2, peer_out, gain):
    n = h2.shape[0]
    tm = min(1024, n)
    blk = pl.BlockSpec((tm, D_MODEL), lambda i: (i, 0))
    return pl.pallas_call(
        _final_kernel,
        grid=(n // tm,),
        in_specs=[blk, blk, pl.BlockSpec((1, D_MODEL), lambda i: (0, 0))],
        out_specs=blk,
        out_shape=jax.ShapeDtypeStruct((n, D_MODEL), F32),
        compiler_params=pltpu.CompilerParams(
            dimension_semantics=("parallel",), vmem_limit_bytes=VMEM_LIMIT),
        name="final_norm",
    )(h2, peer_out, gain)


def _pad_heads(w, dk):
    lead = w.shape[:-1]
    w = w.reshape(lead + (HEADS, dk))
    w = jnp.pad(w, [(0, 0)] * len(lead) + [(0, 0), (0, LANES - dk)])
    return w.reshape(lead + (HEADS * LANES,))


def _rot_heads(w, dk):
    lead = w.shape[:-1]
    w = w.reshape(lead + (HEADS, 2, dk // 2))
    w = jnp.stack([-w[..., 1, :], w[..., 0, :]], axis=-2)
    return w.reshape(lead + (HEADS * dk,))


def _pack_in_weights(w_in):
    gq, gk, gv, gg, ga, rq, rk, rv, rg = jnp.split(w_in, np.cumsum(IN_SPLITS)[:-1].tolist(), axis=-1)
    ga = jnp.pad(ga, ((0, 0), (0, 2 * LANES - GATE_RANK)))
    return jnp.concatenate(
        [_pad_heads(gq, DK), _pad_heads(gk, DK), gv, gg,
         _pad_heads(rq, DK), _pad_heads(rk, DK), rv, rg,
         _pad_heads(_rot_heads(rq, DK), DK), _pad_heads(_rot_heads(rk, DK), DK), ga],
        axis=-1).astype(BF16)


def _rope_tables(pos):
    inv_freq = ROPE_BASE ** (-jnp.arange(0, DK, 2, dtype=F32) / DK)
    ang = pos[:, None] * inv_freq[None, :]
    full = lambda t: _pad_heads(jnp.tile(t, (1, 2 * HEADS)), DK)
    return full(jnp.cos(ang)), full(jnp.sin(ang))


def kernel(x, meta_tokens, norm1_gain, w_in, gla_gate_w2, gla_gate_b, gla_norm_gain, ret_norm_gain, ret_norm_bias, w_out, norm2_gain, peer_w_q, peer_subkeys, peer_u, peer_v, final_gain):
    b, s, d = x.shape
    assert d == D_MODEL and (s // SEGS_PER_SEQ) % (SC_WORKERS * SC_TOK_BATCH) == 0
    assert norm1_gain.shape[0] == 1, "single-layer block"
    x2d = x.reshape(b * s, d)

    w_all = _pack_in_weights(w_in[0])
    w2p = jnp.pad(_pad_heads(gla_gate_w2[0], DK), ((0, LANES - GATE_RANK), (0, 0))).astype(BF16)
    gbp = _pad_heads(gla_gate_b[0][None, :], DK)
    g1 = norm1_gain[0][None, :]
    ggain = gla_norm_gain[0][None, :]
    rgain = ret_norm_gain[0][None, :]
    rbias = ret_norm_bias[0][None, :]

    h0 = jnp.concatenate([jnp.zeros((N_PAD, d), x.dtype), meta_tokens.astype(x.dtype)], axis=0)
    cos0, sin0 = _rope_tables(jnp.arange(CHUNK, dtype=F32) - N_PAD)
    cos1, sin1 = _rope_tables(jnp.arange(s, dtype=F32) + N_META)
    zero_state = jnp.zeros((HEADS, DV, LANES), F32)
    _, s_meta, r_meta = _mixer(_norm_proj(h0, g1, w_all), cos0, sin0, w2p, gbp, ggain, rgain, rbias,
                               zero_state, zero_state, batch=1, n_invalid=N_PAD)

    w_out_b = w_out[0].astype(BF16)
    g2 = norm2_gain[0][None, :]
    w_q = peer_w_q[0].astype(BF16)
    subkeys = peer_subkeys[0].reshape(2 * PEER_HEADS, PEER_NKEYS, PEER_DKEY // 2).astype(BF16)
    u3 = _pack_table(peer_u[0])
    v3 = _pack_table(peer_v[0])
    gf = final_gain[None, :]

    seg = s // SEGS_PER_SEQ
    outs = []
    for bi in range(b):
        s_state, r_state = s_meta, r_meta
        for si in range(SEGS_PER_SEQ):
            xs = x2d[bi * s + si * seg:bi * s + (si + 1) * seg]
            pos = slice(si * seg, (si + 1) * seg)
            o, s_state, r_state = _mixer(_norm_proj(xs, g1, w_all), cos1[pos], sin1[pos], w2p, gbp,
                                         ggain, rgain, rbias, s_state, r_state, batch=1, n_invalid=0)
            h2, xn2 = _out_proj(o, xs, w_out_b, g2)
            idx, g = _route(xn2, w_q, subkeys)
            idx_flat = idx.reshape(-1)
            part = _peer_dot_sc(u3, idx_flat, xn2)
            peer_out = _peer_combine_sc(v3, idx_flat, _coef(part, g))
            outs.append(_final(h2, peer_out, gf))
    return jnp.concatenate(outs, axis=0).reshape(b, s, d)
```

```python
import functools
import math

import jax
import jax.numpy as jnp
import numpy as np
from jax import lax
from jax.experimental import pallas as pl
from jax.experimental.pallas import tpu as pltpu
from jax.experimental.pallas import tpu_sc as plsc

F32 = jnp.float32
BF16 = jnp.bfloat16

D_MODEL = 1024
N_META = 16
CHUNK = 64
N_PAD = CHUNK - N_META
EPS = 1e-6

HEADS = 4
DV = 128
DK = 64
LANES = 128
GATE_RANK = 16
GATE_TAU = 16.0
ROPE_BASE = 10000.0
IN_SPLITS = (HEADS * DK, HEADS * DK, HEADS * DV, HEADS * DV, GATE_RANK,
             HEADS * DK, HEADS * DK, HEADS * DV, HEADS * DV)

PEER_HEADS = 8
PEER_NKEYS = 128
PEER_DKEY = 256
PEER_TOPK = 16
PEER_SEL = PEER_HEADS * PEER_TOPK

HW = HEADS * LANES
OFF_GQ, OFF_GK, OFF_GV, OFF_GG = 0, HW, 2 * HW, 3 * HW
OFF_RQ, OFF_RK, OFF_RV, OFF_RG = 4 * HW, 5 * HW, 6 * HW, 7 * HW
OFF_RQR, OFF_RKR, OFF_GA = 8 * HW, 9 * HW, 10 * HW
PROJ_W = 10 * HW + 2 * LANES
PROJ_TN = PROJ_W // 3

LOG_GAMMA = tuple(math.log(1.0 - 2.0 ** (-5.0 - h)) for h in range(HEADS))

SC_CORES, SC_SUBCORES, SC_LANES = 2, 16, 16
SC_WORKERS = SC_CORES * SC_SUBCORES
SC_TOK_BATCH = 16
SC_ROWS = 16
SC_NBUF = 8
SC_GATHERS = SC_TOK_BATCH * PEER_SEL // SC_ROWS
HALF = D_MODEL // 2
ROW_TILES = HALF // LANES

SEGS_PER_SEQ = 2
SC_LOOKAHEAD = 2

VMEM_LIMIT = 48 * 1024 * 1024


def _dot(a, b, **kw):
    return jnp.dot(a, b, preferred_element_type=F32, **kw)


def _dot_nt(a, b):
    return lax.dot_general(a, b, (((1,), (1,)), ((), ())), preferred_element_type=F32)


def _dot_tn(a, b):
    return lax.dot_general(a, b, (((0,), (0,)), ((), ())), preferred_element_type=F32)


def _rms(x, g):
    return x * lax.rsqrt(jnp.mean(x * x, axis=-1, keepdims=True) + EPS) * g


def _silu(x):
    return x * jax.nn.sigmoid(x)


def _norm_proj_kernel(x_ref, g_ref, w_ref, o_ref):
    xn = _rms(x_ref[...], g_ref[...]).astype(BF16)
    o_ref[...] = _dot(xn, w_ref[...])


def _norm_proj(x2d, gain, w_all):
    n = x2d.shape[0]
    tm = min(512, n)
    return pl.pallas_call(
        _norm_proj_kernel,
        grid=(n // tm, PROJ_W // PROJ_TN),
        in_specs=[
            pl.BlockSpec((tm, D_MODEL), lambda i, j: (i, 0)),
            pl.BlockSpec((1, D_MODEL), lambda i, j: (0, 0)),
            pl.BlockSpec((D_MODEL, PROJ_TN), lambda i, j: (0, j)),
        ],
        out_specs=pl.BlockSpec((tm, PROJ_TN), lambda i, j: (i, j)),
        out_shape=jax.ShapeDtypeStruct((n, PROJ_W), F32),
        compiler_params=pltpu.CompilerParams(
            dimension_semantics=("parallel", "parallel"), vmem_limit_bytes=VMEM_LIMIT),
        name="norm_in_proj",
    )(x2d, gain, w_all)


def _mixer_kernel(proj_ref, cos_ref, sin_ref, w2_ref, gb_ref, ggain_ref, rgain_ref, rbias_ref,
                  s0_ref, r0_ref, o_ref, s_out_ref, r_out_ref, s_scr, r_scr, *, n_invalid, rows_blk):
    j = pl.program_id(1)

    @pl.when(j == 0)
    def _():
        s_scr[...] = s0_ref[...]
        r_scr[...] = r0_ref[...]

    row = lax.broadcasted_iota(jnp.int32, (CHUNK, CHUNK), 0)
    col = lax.broadcasted_iota(jnp.int32, (CHUNK, CHUNK), 1)
    causal = row >= col
    tril = causal.astype(F32)
    diff = jnp.maximum(row - col, 0).astype(F32)
    rowf = lax.broadcasted_iota(jnp.int32, (CHUNK, LANES), 0).astype(F32)
    row1 = lax.broadcasted_iota(jnp.int32, (CHUNK, 1), 0)

    def chunk_body(c, carry):
        r0 = pl.multiple_of(c * CHUNK, CHUNK)
        rows = pl.ds(r0, CHUNK)
        valid = ((j * rows_blk + r0 + row1) >= n_invalid).astype(F32)

        ga = proj_ref[rows, OFF_GA:OFF_GA + LANES].astype(BF16)
        pre = _dot(ga, w2_ref[...]) + gb_ref[...]
        log_a = jax.nn.log_sigmoid(pre) * (1.0 / GATE_TAU)
        bcum = _dot(tril, log_a, precision=lax.Precision.HIGHEST)
        b_last = bcum[CHUNK - 1:CHUNK, :]
        gq = proj_ref[rows, OFF_GQ:OFF_GQ + HW] * (DK ** -0.5)
        gk = proj_ref[rows, OFF_GK:OFF_GK + HW] * valid
        q_t = (gq * jnp.exp(bcum)).astype(BF16)
        k_t = (gk * jnp.exp(-bcum)).astype(BF16)
        k_end = (gk * jnp.exp(b_last - bcum)).astype(BF16)
        decay = jnp.exp(b_last)
        for h in range(HEADS):
            sl = slice(h * LANES, (h + 1) * LANES)
            v = proj_ref[rows, OFF_GV + h * DV:OFF_GV + (h + 1) * DV].astype(BF16)
            att = jnp.where(causal, _dot_nt(q_t[:, sl], k_t[:, sl]), 0.0).astype(BF16)
            s_prev = s_scr[h]
            o = _dot(att, v) + _dot_nt(q_t[:, sl], s_prev.astype(BF16))
            s_scr[h] = s_prev * decay[:, sl] + _dot_tn(v, k_end[:, sl])
            o = o * lax.rsqrt(jnp.mean(o * o, axis=-1, keepdims=True) + EPS) * ggain_ref[...]
            gate = proj_ref[rows, OFF_GG + h * DV:OFF_GG + (h + 1) * DV]
            o_ref[rows, h * DV:(h + 1) * DV] = (o * _silu(gate)).astype(o_ref.dtype)

        cosb = cos_ref[rows, :]
        sinb = sin_ref[rows, :]
        rq = proj_ref[rows, OFF_RQ:OFF_RQ + HW]
        rqr = proj_ref[rows, OFF_RQR:OFF_RQR + HW]
        rk = proj_ref[rows, OFF_RK:OFF_RK + HW] * valid
        rkr = proj_ref[rows, OFF_RKR:OFF_RKR + HW] * valid
        q_r = (rq * cosb + rqr * sinb).astype(BF16)
        k_r = (rk * cosb + rkr * sinb) * (DK ** -0.5)
        for h in range(HEADS):
            sl = slice(h * LANES, (h + 1) * LANES)
            lg = LOG_GAMMA[h]
            dmat = jnp.where(causal, jnp.exp(lg * diff), 0.0)
            xi = jnp.exp(lg * (rowf + 1.0))
            zeta = jnp.exp(lg * (CHUNK - 1.0 - rowf))
            v = proj_ref[rows, OFF_RV + h * DV:OFF_RV + (h + 1) * DV].astype(BF16)
            k_h = k_r[:, sl]
            att = (_dot_nt(q_r[:, sl], k_h.astype(BF16)) * dmat).astype(BF16)
            r_prev = r_scr[h]
            o = _dot(att, v) + _dot_nt(q_r[:, sl], r_prev.astype(BF16)) * xi
            r_scr[h] = r_prev * math.exp(lg * CHUNK) + _dot_tn(v, (k_h * zeta).astype(BF16))
            mu = jnp.mean(o, axis=-1, keepdims=True)
            var = jnp.mean(jnp.square(o - mu), axis=-1, keepdims=True)
            o = (o - mu) * lax.rsqrt(var + EPS) * rgain_ref[...] + rbias_ref[...]
            gate = proj_ref[rows, OFF_RG + h * DV:OFF_RG + (h + 1) * DV]
            o_ref[rows, HW + h * DV:HW + (h + 1) * DV] = (o * _silu(gate)).astype(o_ref.dtype)
        return carry

    lax.fori_loop(0, rows_blk // CHUNK, chunk_body, 0)
    s_out_ref[...] = s_scr[...]
    r_out_ref[...] = r_scr[...]


def _mixer(proj, cosf, sinf, w2p, gbp, ggain, rgain, rbias, s0, r0, *, batch, n_invalid):
    rows = proj.shape[0] // batch
    rows_blk = min(256, rows)
    nblk = rows // rows_blk
    const2 = lambda b, j: (0, 0)
    const3 = lambda b, j: (0, 0, 0)
    state = jax.ShapeDtypeStruct((HEADS, DV, LANES), F32)
    return pl.pallas_call(
        functools.partial(_mixer_kernel, n_invalid=n_invalid, rows_blk=rows_blk),
        grid=(batch, nblk),
        in_specs=[
            pl.BlockSpec((rows_blk, PROJ_W), lambda b, j: (b * nblk + j, 0)),
            pl.BlockSpec((rows_blk, HW), lambda b, j: (j, 0)),
            pl.BlockSpec((rows_blk, HW), lambda b, j: (j, 0)),
            pl.BlockSpec((LANES, HW), const2),
            pl.BlockSpec((1, HW), const2),
            pl.BlockSpec((1, DV), const2),
            pl.BlockSpec((1, DV), const2),
            pl.BlockSpec((1, DV), const2),
            pl.BlockSpec((HEADS, DV, LANES), const3),
            pl.BlockSpec((HEADS, DV, LANES), const3),
        ],
        out_specs=[
            pl.BlockSpec((rows_blk, D_MODEL), lambda b, j: (b * nblk + j, 0)),
            pl.BlockSpec((HEADS, DV, LANES), const3),
            pl.BlockSpec((HEADS, DV, LANES), const3),
        ],
        out_shape=[jax.ShapeDtypeStruct((batch * rows, D_MODEL), BF16), state, state],
        scratch_shapes=[pltpu.VMEM((HEADS, DV, LANES), F32), pltpu.VMEM((HEADS, DV, LANES), F32)],
        compiler_params=pltpu.CompilerParams(
            dimension_semantics=("arbitrary", "arbitrary"), vmem_limit_bytes=VMEM_LIMIT),
        name="token_mixers",
    )(proj, cosf, sinf, w2p, gbp, ggain, rgain, rbias, s0, r0)


def _out_proj_kernel(o_ref, x_ref, w_ref, g_ref, h_ref, xn_ref):
    h = x_ref[...] + _dot(o_ref[...], w_ref[...])
    h_ref[...] = h
    xn_ref[...] = _rms(h, g_ref[...])


def _out_proj(o, x2d, w_out, gain2):
    n = x2d.shape[0]
    tm = min(512, n)
    blk = pl.BlockSpec((tm, D_MODEL), lambda i: (i, 0))
    return pl.pallas_call(
        _out_proj_kernel,
        grid=(n // tm,),
        in_specs=[blk, blk, pl.BlockSpec((D_MODEL, D_MODEL), lambda i: (0, 0)),
                  pl.BlockSpec((1, D_MODEL), lambda i: (0, 0))],
        out_specs=[blk, blk],
        out_shape=[jax.ShapeDtypeStruct((n, D_MODEL), F32)] * 2,
        compiler_params=pltpu.CompilerParams(
            dimension_semantics=("parallel",), vmem_limit_bytes=VMEM_LIMIT),
        name="out_proj_norm",
    )(o, x2d, w_out, gain2)


def _top_k_rows(arrays, payloads=None):
    arrays = list(arrays)
    n = arrays[0].shape[0]
    rowi = lax.broadcasted_iota(jnp.int32, arrays[0].shape, 0)
    vals = [[] for _ in arrays]
    picks = [[] for _ in arrays]
    for _ in range(PEER_TOPK):
        for k, s in enumerate(arrays):
            m = jnp.max(s, axis=0, keepdims=True)
            am = jnp.min(jnp.where(s == m, rowi, n), axis=0, keepdims=True)
            sel = rowi == am
            vals[k].append(m)
            if payloads is None:
                picks[k].append(am)
            else:
                picks[k].append(jnp.min(jnp.where(sel, payloads[k], jnp.iinfo(jnp.int32).max),
                                        axis=0, keepdims=True))
            arrays[k] = jnp.where(sel, -jnp.inf, s)
    return [(jnp.concatenate(v, axis=0), jnp.concatenate(p, axis=0)) for v, p in zip(vals, picks)]


def _candidate_cells():
    cells = [(a, b) for a in range(PEER_TOPK) for b in range(PEER_TOPK) if (a + 1) * (b + 1) <= PEER_TOPK]
    n = -(-len(cells) // 8) * 8
    sel_a = np.zeros((n, PEER_TOPK), np.float32)
    sel_b = np.zeros((n, PEER_TOPK), np.float32)
    pad = np.full((n, 1), -np.inf, np.float32)
    for r, (a, b) in enumerate(cells):
        sel_a[r, a] = 1.0
        sel_b[r, b] = 1.0
        pad[r, 0] = 0.0
    return sel_a, sel_b, pad


def _route_kernel(xn_ref, wq_ref, sk_ref, sa_ref, sb_ref, pad_ref, idx_ref, g_ref, q_scr, i_scr, g_scr):
    tr = xn_ref.shape[0]
    q = _dot(xn_ref[...].astype(BF16), wq_ref[...])
    for hc in range(2 * PEER_HEADS):
        q_scr[hc] = q[:, hc * LANES:(hc + 1) * LANES].astype(BF16)

    def sub_body(sb, carry):
        rows = pl.ds(pl.multiple_of(sb * LANES, LANES), LANES)

        def head_pair(hp, c2):
            heads = [2 * hp, 2 * hp + 1]
            scores = [_dot_nt(sk_ref[2 * h + c], q_scr[2 * h + c, rows, :])
                      for h in heads for c in range(2)]
            first = _top_k_rows(scores)
            sa, sb = sa_ref[...], sb_ref[...]
            cands, cidxs = [], []
            for k in range(len(heads)):
                (v0, i0), (v1, i1) = first[2 * k], first[2 * k + 1]
                cands.append((_dot(sa, v0, precision=lax.Precision.HIGHEST)
                              + _dot(sb, v1, precision=lax.Precision.HIGHEST)) + pad_ref[...])
                cidxs.append((_dot(sa, i0.astype(F32)) * PEER_NKEYS
                              + _dot(sb, i1.astype(F32))).astype(jnp.int32))
            for h, (best, e) in zip(heads, _top_k_rows(cands, cidxs)):
                ex = jnp.exp(best - best[0:1])
                i_scr[h] = e
                g_scr[h] = ex / jnp.sum(ex, axis=0, keepdims=True)
            return c2

        lax.fori_loop(0, PEER_HEADS // 2, head_pair, 0)
        idx_ref[rows, :] = i_scr[...].reshape(PEER_SEL, LANES).T
        g_ref[rows, :] = g_scr[...].reshape(PEER_SEL, LANES).T
        return carry

    lax.fori_loop(0, tr // LANES, sub_body, 0)


def _route(xn2, w_q, subkeys):
    n = xn2.shape[0]
    sel_a, sel_b, pad = _candidate_cells()
    tr = min(512, n)
    qw = 2 * PEER_HEADS * LANES
    return pl.pallas_call(
        _route_kernel,
        grid=(n // tr,),
        in_specs=[pl.BlockSpec((tr, D_MODEL), lambda i: (i, 0)),
                  pl.BlockSpec((D_MODEL, qw), lambda i: (0, 0)),
                  pl.BlockSpec((2 * PEER_HEADS, PEER_NKEYS, LANES), lambda i: (0, 0, 0)),
                  pl.BlockSpec(sel_a.shape, lambda i: (0, 0)),
                  pl.BlockSpec(sel_b.shape, lambda i: (0, 0)),
                  pl.BlockSpec(pad.shape, lambda i: (0, 0))],
        out_specs=[pl.BlockSpec((tr, PEER_SEL), lambda i: (i, 0))] * 2,
        out_shape=[jax.ShapeDtypeStruct((n, PEER_SEL), jnp.int32),
                   jax.ShapeDtypeStruct((n, PEER_SEL), F32)],
        scratch_shapes=[pltpu.VMEM((2 * PEER_HEADS, tr, LANES), BF16),
                        pltpu.VMEM((PEER_HEADS, PEER_TOPK, LANES), jnp.int32),
                        pltpu.VMEM((PEER_HEADS, PEER_TOPK, LANES), F32)],
        compiler_params=pltpu.CompilerParams(
            dimension_semantics=("parallel",), vmem_limit_bytes=VMEM_LIMIT),
        name="peer_route",
    )(xn2, w_q, subkeys, sel_a, sel_b, pad)


def _sc_worker_id():
    return lax.axis_index("s") * SC_CORES + lax.axis_index("c")


def _sc_mesh():
    return plsc.VectorSubcoreMesh(core_axis_name="c", subcore_axis_name="s")


def _sc_gather_ring(table_hbm, idx_v, bufs, sems, unit, compute):
    def gather(g, k):
        off = pl.multiple_of(g * SC_ROWS, SC_ROWS)
        return pltpu.make_async_copy(table_hbm.at[idx_v.at[pl.ds(off, SC_ROWS)]], bufs[k], sems[k])

    for k in range(SC_NBUF):
        gather(k, k).start()

    units_per_iter = SC_NBUF // unit

    def ring_iter(i, carry):
        for m in range(units_per_iter):
            ks = range(m * unit, (m + 1) * unit)
            for k in ks:
                gather(i * SC_NBUF + k, k).wait()
            compute(i * units_per_iter + m, [bufs[k] for k in ks])
            for k in ks:
                g_next = (i + 1) * SC_NBUF + k

                @pl.when(g_next < SC_GATHERS)
                def _():
                    gather(g_next, k).start()
        return carry

    lax.fori_loop(0, SC_GATHERS // SC_NBUF, ring_iter, 0)


def _pack_table(t):
    lo = lax.bitcast_convert_type(t[:, :HALF].astype(jnp.bfloat16), jnp.uint16).astype(jnp.int32)
    hi_bits = lax.bitcast_convert_type(t[:, HALF:], jnp.int32)
    sign = hi_bits & jnp.int32(-2 ** 31)
    mag = jnp.minimum(hi_bits & jnp.int32(2 ** 31 - 1), jnp.int32(0x7F000000))
    upper = jnp.maximum(mag + (1 << 15) - lo, 0) >> 16
    words = sign | (upper << 16) | lo
    return lax.bitcast_convert_type(words, jnp.uint32).reshape(t.shape[0], ROW_TILES, LANES)


def _unpack_words(w):
    lo = lax.bitcast_convert_type(w << 16, F32)
    hi = lax.bitcast_convert_type(w, F32)
    return lo, hi


def _sc_scratch(staged_cols, out_cols):
    return ([pltpu.VMEM((SC_TOK_BATCH * PEER_SEL,), jnp.int32),
             pltpu.VMEM((SC_TOK_BATCH, staged_cols), F32),
             pltpu.VMEM((SC_TOK_BATCH, out_cols), F32)]
            + [pltpu.VMEM((SC_ROWS, ROW_TILES, LANES), jnp.uint32) for _ in range(SC_NBUF)]
            + [pltpu.SemaphoreType.DMA for _ in range(SC_NBUF)])


def _sc_batches(tpw, idx_hbm, staged_hbm, out_hbm, idx_v, staged_v, out_v, run):
    tok0 = _sc_worker_id() * tpw

    def batch(tb, carry):
        t0 = pl.multiple_of(tok0 + tb * SC_TOK_BATCH, SC_TOK_BATCH)
        pltpu.sync_copy(idx_hbm.at[pl.ds(pl.multiple_of(t0 * PEER_SEL, SC_TOK_BATCH * PEER_SEL),
                                         SC_TOK_BATCH * PEER_SEL)], idx_v)
        pltpu.sync_copy(staged_hbm.at[pl.ds(t0, SC_TOK_BATCH)], staged_v)
        run()
        pltpu.sync_copy(out_v, out_hbm.at[pl.ds(t0, SC_TOK_BATCH)])
        return carry

    lax.fori_loop(0, tpw // SC_TOK_BATCH, batch, 0)


def _peer_dot_sc(u3, idx_flat, xn2):
    t = xn2.shape[0]
    unit = 16 // SC_ROWS
    units_per_tok = PEER_SEL // 16

    @functools.partial(
        pl.kernel,
        mesh=_sc_mesh(),
        out_type=jax.ShapeDtypeStruct((t, PEER_SEL), F32),
        scratch_types=[pltpu.VMEM((16 * SC_LANES,), F32)] + _sc_scratch(D_MODEL, PEER_SEL),
        compiler_params=pltpu.CompilerParams(needs_layout_passes=False),
        name="peer_dot",
    )
    def body(u_hbm, idx_hbm, x_hbm, out_hbm, red_v, idx_v, x_v, out_v, *rest):
        bufs, sems = rest[:SC_NBUF], rest[SC_NBUF:]

        def compute(u, ub):
            tl = u // units_per_tok
            rb = (u % units_per_tok) * 16

            def cbody(c, acc):
                acc = list(acc)
                c4 = c // (LANES // SC_LANES)
                lane0 = pl.multiple_of((c % (LANES // SC_LANES)) * SC_LANES, SC_LANES)
                col = pl.multiple_of(c * SC_LANES, SC_LANES)
                x_lo = x_v[tl, pl.ds(col, SC_LANES)]
                x_hi = x_v[tl, pl.ds(HALF + col, SC_LANES)]
                for q, buf in enumerate(ub):
                    for r in range(SC_ROWS):
                        a = q * SC_ROWS + r
                        lo, hi = _unpack_words(buf[r, c4, pl.ds(lane0, SC_LANES)])
                        acc[a] = acc[a] + lo * x_lo + hi * x_hi
                return tuple(acc)

            acc = lax.fori_loop(0, HALF // SC_LANES, cbody,
                                tuple(jnp.zeros((SC_LANES,), F32) for _ in range(16)))
            for a in range(16):
                red_v[pl.ds(a * SC_LANES, SC_LANES)] = acc[a]
            lane_base = lax.iota(jnp.int32, SC_LANES) * SC_LANES
            tot = plsc.load_gather(red_v, [lane_base])
            for k in range(1, SC_LANES):
                tot = tot + plsc.load_gather(red_v, [lane_base + k])
            out_v[tl, pl.ds(pl.multiple_of(rb, 16), 16)] = tot

        _sc_batches(t // SC_WORKERS, idx_hbm, x_hbm, out_hbm, idx_v, x_v, out_v,
                    lambda: _sc_gather_ring(u_hbm, idx_v, bufs, sems, unit, compute))

    return body(u3, idx_flat, xn2)


def _peer_combine_sc(v3, idx_flat, coef_b):
    t = coef_b.shape[0]
    unit = 32 // SC_ROWS
    units_per_tok = PEER_SEL // 32

    @functools.partial(
        pl.kernel,
        mesh=_sc_mesh(),
        out_type=jax.ShapeDtypeStruct((t, D_MODEL), F32),
        scratch_types=_sc_scratch(PEER_SEL * SC_LANES, D_MODEL),
        compiler_params=pltpu.CompilerParams(needs_layout_passes=False),
        name="peer_combine",
    )
    def body(v_hbm, idx_hbm, coef_hbm, out_hbm, idx_v, coef_v, out_v, *rest):
        bufs, sems = rest[:SC_NBUF], rest[SC_NBUF:]

        def compute(u, ub):
            tl = u // units_per_tok
            rb = (u % units_per_tok) * 32
            for g in range(ROW_TILES):
                cols = ([pl.ds(g * LANES + l * SC_LANES, SC_LANES) for l in range(8)]
                        + [pl.ds(HALF + g * LANES + l * SC_LANES, SC_LANES) for l in range(8)])

                def rbody(r, acc):
                    acc = list(acc)
                    for q, buf in enumerate(ub):
                        co = pl.multiple_of((rb + q * SC_ROWS + r) * SC_LANES, SC_LANES)
                        cf = coef_v[tl, pl.ds(co, SC_LANES)]
                        for l in range(8):
                            lo, hi = _unpack_words(buf[r, g, pl.ds(l * SC_LANES, SC_LANES)])
                            acc[l] = acc[l] + lo * cf
                            acc[8 + l] = acc[8 + l] + hi * cf
                    return tuple(acc)

                acc = lax.fori_loop(0, SC_ROWS, rbody, tuple(out_v[tl, c] for c in cols))
                for k in range(16):
                    out_v[tl, cols[k]] = acc[k]

        def run():
            def zero(tl, c2):
                for k in range(D_MODEL // SC_LANES):
                    out_v[tl, pl.ds(k * SC_LANES, SC_LANES)] = jnp.zeros((SC_LANES,), F32)
                return c2

            lax.fori_loop(0, SC_TOK_BATCH, zero, 0)
            _sc_gather_ring(v_hbm, idx_v, bufs, sems, unit, compute)

        _sc_batches(t // SC_WORKERS, idx_hbm, coef_hbm, out_hbm, idx_v, coef_v, out_v, run)

    return body(v3, idx_flat, coef_b)


def _coef_kernel(act_ref, g_ref, e_ref, coef_ref):
    act = act_ref[...]
    gelu = 0.5 * act * (1.0 + lax.erf(act * np.float32(math.sqrt(0.5))))
    coef_ref[...] = _dot(g_ref[...] * gelu, e_ref[...], precision=lax.Precision.HIGHEST)


def _coef(act, g):
    n = act.shape[0]
    tr = min(512, n)
    w = PEER_SEL * SC_LANES
    expand = (jnp.arange(w, dtype=jnp.int32)[None, :] // SC_LANES
              == jnp.arange(PEER_SEL, dtype=jnp.int32)[:, None]).astype(F32)
    blk = pl.BlockSpec((tr, PEER_SEL), lambda i: (i, 0))
    return pl.pallas_call(
        _coef_kernel,
        grid=(n // tr,),
        in_specs=[blk, blk, pl.BlockSpec((PEER_SEL, w), lambda i: (0, 0))],
        out_specs=pl.BlockSpec((tr, w), lambda i: (i, 0)),
        out_shape=jax.ShapeDtypeStruct((n, w), F32),
        compiler_params=pltpu.CompilerParams(
            dimension_semantics=("parallel",), vmem_limit_bytes=VMEM_LIMIT),
        name="peer_coef",
    )(act, g, expand)


def _final_kernel(h_ref, p_ref, g_ref, o_ref):
    o_ref[...] = _rms(h_ref[...] + p_ref[...], g_ref[...])


def _final(h2, peer_out, gain):
    n = h2.shape[0]
    tm = min(1024, n)
    blk = pl.BlockSpec((tm, D_MODEL), lambda i: (i, 0))
    return pl.pallas_call(
        _final_kernel,
        grid=(n // tm,),
        in_specs=[blk, blk, pl.BlockSpec((1, D_MODEL), lambda i: (0, 0))],
        out_specs=blk,
        out_shape=jax.ShapeDtypeStruct((n, D_MODEL), F32),
        compiler_params=pltpu.CompilerParams(
            dimension_semantics=("parallel",), vmem_limit_bytes=VMEM_LIMIT),
        name="final_norm",
    )(h2, peer_out, gain)


def _pad_heads(w, dk):
    lead = w.shape[:-1]
    w = w.reshape(lead + (HEADS, dk))
    w = jnp.pad(w, [(0, 0)] * len(lead) + [(0, 0), (0, LANES - dk)])
    return w.reshape(lead + (HEADS * LANES,))


def _rot_heads(w, dk):
    lead = w.shape[:-1]
    w = w.reshape(lead + (HEADS, 2, dk // 2))
    w = jnp.stack([-w[..., 1, :], w[..., 0, :]], axis=-2)
    return w.reshape(lead + (HEADS * dk,))


def _pack_in_weights(w_in):
    gq, gk, gv, gg, ga, rq, rk, rv, rg = jnp.split(w_in, np.cumsum(IN_SPLITS)[:-1].tolist(), axis=-1)
    ga = jnp.pad(ga, ((0, 0), (0, 2 * LANES - GATE_RANK)))
    return jnp.concatenate(
        [_pad_heads(gq, DK), _pad_heads(gk, DK), gv, gg,
         _pad_heads(rq, DK), _pad_heads(rk, DK), rv, rg,
         _pad_heads(_rot_heads(rq, DK), DK), _pad_heads(_rot_heads(rk, DK), DK), ga],
        axis=-1).astype(BF16)


def _rope_tables(pos):
    inv_freq = ROPE_BASE ** (-jnp.arange(0, DK, 2, dtype=F32) / DK)
    ang = pos[:, None] * inv_freq[None, :]
    full = lambda t: _pad_heads(jnp.tile(t, (1, 2 * HEADS)), DK)
    return full(jnp.cos(ang)), full(jnp.sin(ang))


def kernel(x, meta_tokens, norm1_gain, w_in, gla_gate_w2, gla_gate_b, gla_norm_gain, ret_norm_gain, ret_norm_bias, w_out, norm2_gain, peer_w_q, peer_subkeys, peer_u, peer_v, final_gain):
    b, s, d = x.shape
    assert d == D_MODEL and (s // SEGS_PER_SEQ) % (SC_WORKERS * SC_TOK_BATCH) == 0
    assert norm1_gain.shape[0] == 1, "single-layer block"
    x2d = x.reshape(b * s, d)

    w_all = _pack_in_weights(w_in[0])
    w2p = jnp.pad(_pad_heads(gla_gate_w2[0], DK), ((0, LANES - GATE_RANK), (0, 0))).astype(BF16)
    gbp = _pad_heads(gla_gate_b[0][None, :], DK)
    g1 = norm1_gain[0][None, :]
    ggain = gla_norm_gain[0][None, :]
    rgain = ret_norm_gain[0][None, :]
    rbias = ret_norm_bias[0][None, :]

    h0 = jnp.concatenate([jnp.zeros((N_PAD, d), x.dtype), meta_tokens.astype(x.dtype)], axis=0)
    cos0, sin0 = _rope_tables(jnp.arange(CHUNK, dtype=F32) - N_PAD)
    cos1, sin1 = _rope_tables(jnp.arange(s, dtype=F32) + N_META)
    zero_state = jnp.zeros((HEADS, DV, LANES), F32)
    _, s_meta, r_meta = _mixer(_norm_proj(h0, g1, w_all), cos0, sin0, w2p, gbp, ggain, rgain, rbias,
                               zero_state, zero_state, batch=1, n_invalid=N_PAD)

    w_out_b = w_out[0].astype(BF16)
    g2 = norm2_gain[0][None, :]
    w_q = peer_w_q[0].astype(BF16)
    subkeys = peer_subkeys[0].reshape(2 * PEER_HEADS, PEER_NKEYS, PEER_DKEY // 2).astype(BF16)
    u3 = _pack_table(peer_u[0])
    v3 = _pack_table(peer_v[0])
    gf = final_gain[None, :]

    seg = s // SEGS_PER_SEQ
    outs, pending = [], []
    for bi in range(b):
        s_state, r_state = s_meta, r_meta
        for si in range(SEGS_PER_SEQ):
            xs = x2d[bi * s + si * seg:bi * s + (si + 1) * seg]
            pos = slice(si * seg, (si + 1) * seg)
            o, s_state, r_state = _mixer(_norm_proj(xs, g1, w_all), cos1[pos], sin1[pos], w2p, gbp,
                                         ggain, rgain, rbias, s_state, r_state, batch=1, n_invalid=0)
            h2, xn2 = _out_proj(o, xs, w_out_b, g2)
            idx, g = _route(xn2, w_q, subkeys)
            idx_flat = idx.reshape(-1)
            if len(pending) >= SC_LOOKAHEAD:
                h2_old, peer_old = pending.pop(0)
                idx_flat, peer_old = lax.optimization_barrier((idx_flat, peer_old))
                outs.append(_final(h2_old, peer_old, gf))
            act = _peer_dot_sc(u3, idx_flat, xn2)
            pending.append((h2, _peer_combine_sc(v3, idx_flat, _coef(act, g))))
    outs.extend(_final(h2_old, peer_old, gf) for h2_old, peer_old in pending)
    return jnp.concatenate(outs, axis=0).reshape(b, s, d)
```

```python
import functools
import math

import jax
import jax.numpy as jnp
import numpy as np
from jax import lax
from jax.experimental import pallas as pl
from jax.experimental.pallas import tpu as pltpu
from jax.experimental.pallas import tpu_sc as plsc

F32 = jnp.float32
BF16 = jnp.bfloat16

D_MODEL = 1024
N_META = 16
CHUNK = 64
N_PAD = CHUNK - N_META
EPS = 1e-6

HEADS = 4
DV = 128
DK = 64
LANES = 128
GATE_RANK = 16
GATE_TAU = 16.0
ROPE_BASE = 10000.0
IN_SPLITS = (HEADS * DK, HEADS * DK, HEADS * DV, HEADS * DV, GATE_RANK,
             HEADS * DK, HEADS * DK, HEADS * DV, HEADS * DV)

PEER_HEADS = 8
PEER_NKEYS = 128
PEER_DKEY = 256
PEER_TOPK = 16
PEER_SEL = PEER_HEADS * PEER_TOPK

HW = HEADS * LANES
OFF_GQ, OFF_GK, OFF_GV, OFF_GG = 0, HW, 2 * HW, 3 * HW
OFF_RQ, OFF_RK, OFF_RV, OFF_RG = 4 * HW, 5 * HW, 6 * HW, 7 * HW
OFF_RQR, OFF_RKR, OFF_GA = 8 * HW, 9 * HW, 10 * HW
PROJ_W = 10 * HW + 2 * LANES
PROJ_TN = PROJ_W // 3

LOG_GAMMA = tuple(math.log(1.0 - 2.0 ** (-5.0 - h)) for h in range(HEADS))

SC_CORES, SC_SUBCORES, SC_LANES = 2, 16, 16
SC_WORKERS = SC_CORES * SC_SUBCORES
SC_TOK_BATCH = 16
SC_ROWS = 16
SC_NBUF = 8
SC_GATHERS = SC_TOK_BATCH * PEER_SEL // SC_ROWS
HALF = D_MODEL // 2
ROW_TILES = HALF // LANES

SC_CALL_TOKENS = SC_WORKERS * SC_TOK_BATCH
SEGS_PER_SEQ = 2

VMEM_LIMIT = 48 * 1024 * 1024


def _row_block(n, target):
    blk = math.gcd(n, target)
    assert blk % 8 == 0 or blk == n, (n, target)
    return blk


def _dot(a, b, **kw):
    return jnp.dot(a, b, preferred_element_type=F32, **kw)


def _dot_nt(a, b):
    return lax.dot_general(a, b, (((1,), (1,)), ((), ())), preferred_element_type=F32)


def _dot_tn(a, b):
    return lax.dot_general(a, b, (((0,), (0,)), ((), ())), preferred_element_type=F32)


def _rms(x, g):
    return x * lax.rsqrt(jnp.mean(x * x, axis=-1, keepdims=True) + EPS) * g


def _silu(x):
    return x * jax.nn.sigmoid(x)


def _norm_proj_kernel(x_ref, g_ref, w_ref, o_ref):
    xn = _rms(x_ref[...], g_ref[...]).astype(BF16)
    o_ref[...] = _dot(xn, w_ref[...])


def _norm_proj(x2d, gain, w_all):
    n = x2d.shape[0]
    tm = _row_block(n, 512)
    return pl.pallas_call(
        _norm_proj_kernel,
        grid=(n // tm, PROJ_W // PROJ_TN),
        in_specs=[
            pl.BlockSpec((tm, D_MODEL), lambda i, j: (i, 0)),
            pl.BlockSpec((1, D_MODEL), lambda i, j: (0, 0)),
            pl.BlockSpec((D_MODEL, PROJ_TN), lambda i, j: (0, j)),
        ],
        out_specs=pl.BlockSpec((tm, PROJ_TN), lambda i, j: (i, j)),
        out_shape=jax.ShapeDtypeStruct((n, PROJ_W), F32),
        compiler_params=pltpu.CompilerParams(
            dimension_semantics=("parallel", "parallel"), vmem_limit_bytes=VMEM_LIMIT),
        name="norm_in_proj",
    )(x2d, gain, w_all)


def _mixer_kernel(proj_ref, cos_ref, sin_ref, w2_ref, gb_ref, ggain_ref, rgain_ref, rbias_ref,
                  s0_ref, r0_ref, o_ref, s_out_ref, r_out_ref, s_scr, r_scr, *, n_invalid, rows_blk):
    j = pl.program_id(1)

    @pl.when(j == 0)
    def _():
        s_scr[...] = s0_ref[...]
        r_scr[...] = r0_ref[...]

    row = lax.broadcasted_iota(jnp.int32, (CHUNK, CHUNK), 0)
    col = lax.broadcasted_iota(jnp.int32, (CHUNK, CHUNK), 1)
    causal = row >= col
    tril = causal.astype(F32)
    diff = jnp.maximum(row - col, 0).astype(F32)
    rowf = lax.broadcasted_iota(jnp.int32, (CHUNK, LANES), 0).astype(F32)
    row1 = lax.broadcasted_iota(jnp.int32, (CHUNK, 1), 0)

    def chunk_body(c, carry):
        r0 = pl.multiple_of(c * CHUNK, CHUNK)
        rows = pl.ds(r0, CHUNK)
        valid = ((j * rows_blk + r0 + row1) >= n_invalid).astype(F32)

        ga = proj_ref[rows, OFF_GA:OFF_GA + LANES].astype(BF16)
        pre = _dot(ga, w2_ref[...]) + gb_ref[...]
        log_a = jax.nn.log_sigmoid(pre) * (1.0 / GATE_TAU)
        bcum = _dot(tril, log_a, precision=lax.Precision.HIGHEST)
        b_last = bcum[CHUNK - 1:CHUNK, :]
        gq = proj_ref[rows, OFF_GQ:OFF_GQ + HW] * (DK ** -0.5)
        gk = proj_ref[rows, OFF_GK:OFF_GK + HW] * valid
        q_t = (gq * jnp.exp(bcum)).astype(BF16)
        k_t = (gk * jnp.exp(-bcum)).astype(BF16)
        k_end = (gk * jnp.exp(b_last - bcum)).astype(BF16)
        decay = jnp.exp(b_last)
        for h in range(HEADS):
            sl = slice(h * LANES, (h + 1) * LANES)
            v = proj_ref[rows, OFF_GV + h * DV:OFF_GV + (h + 1) * DV].astype(BF16)
            att = jnp.where(causal, _dot_nt(q_t[:, sl], k_t[:, sl]), 0.0).astype(BF16)
            s_prev = s_scr[h]
            o = _dot(att, v) + _dot_nt(q_t[:, sl], s_prev.astype(BF16))
            s_scr[h] = s_prev * decay[:, sl] + _dot_tn(v, k_end[:, sl])
            o = o * lax.rsqrt(jnp.mean(o * o, axis=-1, keepdims=True) + EPS) * ggain_ref[...]
            gate = proj_ref[rows, OFF_GG + h * DV:OFF_GG + (h + 1) * DV]
            o_ref[rows, h * DV:(h + 1) * DV] = (o * _silu(gate)).astype(o_ref.dtype)

        cosb = cos_ref[rows, :]
        sinb = sin_ref[rows, :]
        rq = proj_ref[rows, OFF_RQ:OFF_RQ + HW]
        rqr = proj_ref[rows, OFF_RQR:OFF_RQR + HW]
        rk = proj_ref[rows, OFF_RK:OFF_RK + HW] * valid
        rkr = proj_ref[rows, OFF_RKR:OFF_RKR + HW] * valid
        q_r = (rq * cosb + rqr * sinb).astype(BF16)
        k_r = (rk * cosb + rkr * sinb) * (DK ** -0.5)
        for h in range(HEADS):
            sl = slice(h * LANES, (h + 1) * LANES)
            lg = LOG_GAMMA[h]
            dmat = jnp.where(causal, jnp.exp(lg * diff), 0.0)
            xi = jnp.exp(lg * (rowf + 1.0))
            zeta = jnp.exp(lg * (CHUNK - 1.0 - rowf))
            v = proj_ref[rows, OFF_RV + h * DV:OFF_RV + (h + 1) * DV].astype(BF16)
            k_h = k_r[:, sl]
            att = (_dot_nt(q_r[:, sl], k_h.astype(BF16)) * dmat).astype(BF16)
            r_prev = r_scr[h]
            o = _dot(att, v) + _dot_nt(q_r[:, sl], r_prev.astype(BF16)) * xi
            r_scr[h] = r_prev * math.exp(lg * CHUNK) + _dot_tn(v, (k_h * zeta).astype(BF16))
            mu = jnp.mean(o, axis=-1, keepdims=True)
            var = jnp.mean(jnp.square(o - mu), axis=-1, keepdims=True)
            o = (o - mu) * lax.rsqrt(var + EPS) * rgain_ref[...] + rbias_ref[...]
            gate = proj_ref[rows, OFF_RG + h * DV:OFF_RG + (h + 1) * DV]
            o_ref[rows, HW + h * DV:HW + (h + 1) * DV] = (o * _silu(gate)).astype(o_ref.dtype)
        return carry

    lax.fori_loop(0, rows_blk // CHUNK, chunk_body, 0)
    s_out_ref[...] = s_scr[...]
    r_out_ref[...] = r_scr[...]


def _mixer(proj, cosf, sinf, w2p, gbp, ggain, rgain, rbias, s0, r0, *, batch, n_invalid):
    rows = proj.shape[0] // batch
    rows_blk = _row_block(rows, 256)
    nblk = rows // rows_blk
    const2 = lambda b, j: (0, 0)
    const3 = lambda b, j: (0, 0, 0)
    state = jax.ShapeDtypeStruct((HEADS, DV, LANES), F32)
    return pl.pallas_call(
        functools.partial(_mixer_kernel, n_invalid=n_invalid, rows_blk=rows_blk),
        grid=(batch, nblk),
        in_specs=[
            pl.BlockSpec((rows_blk, PROJ_W), lambda b, j: (b * nblk + j, 0)),
            pl.BlockSpec((rows_blk, HW), lambda b, j: (j, 0)),
            pl.BlockSpec((rows_blk, HW), lambda b, j: (j, 0)),
            pl.BlockSpec((LANES, HW), const2),
            pl.BlockSpec((1, HW), const2),
            pl.BlockSpec((1, DV), const2),
            pl.BlockSpec((1, DV), const2),
            pl.BlockSpec((1, DV), const2),
            pl.BlockSpec((HEADS, DV, LANES), const3),
            pl.BlockSpec((HEADS, DV, LANES), const3),
        ],
        out_specs=[
            pl.BlockSpec((rows_blk, D_MODEL), lambda b, j: (b * nblk + j, 0)),
            pl.BlockSpec((HEADS, DV, LANES), const3),
            pl.BlockSpec((HEADS, DV, LANES), const3),
        ],
        out_shape=[jax.ShapeDtypeStruct((batch * rows, D_MODEL), BF16), state, state],
        scratch_shapes=[pltpu.VMEM((HEADS, DV, LANES), F32), pltpu.VMEM((HEADS, DV, LANES), F32)],
        compiler_params=pltpu.CompilerParams(
            dimension_semantics=("arbitrary", "arbitrary"), vmem_limit_bytes=VMEM_LIMIT),
        name="token_mixers",
    )(proj, cosf, sinf, w2p, gbp, ggain, rgain, rbias, s0, r0)


def _out_proj_kernel(o_ref, x_ref, w_ref, g_ref, h_ref, xn_ref):
    h = x_ref[...] + _dot(o_ref[...], w_ref[...])
    h_ref[...] = h
    xn_ref[...] = _rms(h, g_ref[...])


def _out_proj(o, x2d, w_out, gain2):
    n = x2d.shape[0]
    tm = _row_block(n, 512)
    blk = pl.BlockSpec((tm, D_MODEL), lambda i: (i, 0))
    return pl.pallas_call(
        _out_proj_kernel,
        grid=(n // tm,),
        in_specs=[blk, blk, pl.BlockSpec((D_MODEL, D_MODEL), lambda i: (0, 0)),
                  pl.BlockSpec((1, D_MODEL), lambda i: (0, 0))],
        out_specs=[blk, blk],
        out_shape=[jax.ShapeDtypeStruct((n, D_MODEL), F32)] * 2,
        compiler_params=pltpu.CompilerParams(
            dimension_semantics=("parallel",), vmem_limit_bytes=VMEM_LIMIT),
        name="out_proj_norm",
    )(o, x2d, w_out, gain2)


def _top_k_rows(arrays, payloads=None):
    arrays = list(arrays)
    n = arrays[0].shape[0]
    rowi = lax.broadcasted_iota(jnp.int32, arrays[0].shape, 0)
    vals = [[] for _ in arrays]
    picks = [[] for _ in arrays]
    for _ in range(PEER_TOPK):
        for k, s in enumerate(arrays):
            m = jnp.max(s, axis=0, keepdims=True)
            am = jnp.min(jnp.where(s == m, rowi, n), axis=0, keepdims=True)
            sel = rowi == am
            vals[k].append(m)
            if payloads is None:
                picks[k].append(am)
            else:
                picks[k].append(jnp.min(jnp.where(sel, payloads[k], jnp.iinfo(jnp.int32).max),
                                        axis=0, keepdims=True))
            arrays[k] = jnp.where(sel, -jnp.inf, s)
    return [(jnp.concatenate(v, axis=0), jnp.concatenate(p, axis=0)) for v, p in zip(vals, picks)]


def _candidate_cells():
    cells = [(a, b) for a in range(PEER_TOPK) for b in range(PEER_TOPK) if (a + 1) * (b + 1) <= PEER_TOPK]
    n = -(-len(cells) // 8) * 8
    sel_a = np.zeros((n, PEER_TOPK), np.float32)
    sel_b = np.zeros((n, PEER_TOPK), np.float32)
    pad = np.full((n, 1), -np.inf, np.float32)
    for r, (a, b) in enumerate(cells):
        sel_a[r, a] = 1.0
        sel_b[r, b] = 1.0
        pad[r, 0] = 0.0
    return sel_a, sel_b, pad


def _route_kernel(xn_ref, wq_ref, sk_ref, sa_ref, sb_ref, pad_ref, idx_ref, g_ref, q_scr, i_scr, g_scr):
    tr = xn_ref.shape[0]
    q = _dot(xn_ref[...].astype(BF16), wq_ref[...])
    for hc in range(2 * PEER_HEADS):
        q_scr[hc] = q[:, hc * LANES:(hc + 1) * LANES].astype(BF16)

    def sub_body(sb, carry):
        rows = pl.ds(pl.multiple_of(sb * LANES, LANES), LANES)

        def head_pair(hp, c2):
            heads = [2 * hp, 2 * hp + 1]
            scores = [_dot_nt(sk_ref[2 * h + c], q_scr[2 * h + c, rows, :])
                      for h in heads for c in range(2)]
            first = _top_k_rows(scores)
            sa, sb = sa_ref[...], sb_ref[...]
            cands, cidxs = [], []
            for k in range(len(heads)):
                (v0, i0), (v1, i1) = first[2 * k], first[2 * k + 1]
                cands.append((_dot(sa, v0, precision=lax.Precision.HIGHEST)
                              + _dot(sb, v1, precision=lax.Precision.HIGHEST)) + pad_ref[...])
                cidxs.append((_dot(sa, i0.astype(F32)) * PEER_NKEYS
                              + _dot(sb, i1.astype(F32))).astype(jnp.int32))
            for h, (best, e) in zip(heads, _top_k_rows(cands, cidxs)):
                ex = jnp.exp(best - best[0:1])
                i_scr[h] = e
                g_scr[h] = ex / jnp.sum(ex, axis=0, keepdims=True)
            return c2

        lax.fori_loop(0, PEER_HEADS // 2, head_pair, 0)
        idx_ref[rows, :] = i_scr[...].reshape(PEER_SEL, LANES).T
        g_ref[rows, :] = g_scr[...].reshape(PEER_SEL, LANES).T
        return carry

    lax.fori_loop(0, tr // LANES, sub_body, 0)


def _route(xn2, w_q, subkeys):
    n = xn2.shape[0]
    sel_a, sel_b, pad = _candidate_cells()
    tr = _row_block(n, 512)
    qw = 2 * PEER_HEADS * LANES
    return pl.pallas_call(
        _route_kernel,
        grid=(n // tr,),
        in_specs=[pl.BlockSpec((tr, D_MODEL), lambda i: (i, 0)),
                  pl.BlockSpec((D_MODEL, qw), lambda i: (0, 0)),
                  pl.BlockSpec((2 * PEER_HEADS, PEER_NKEYS, LANES), lambda i: (0, 0, 0)),
                  pl.BlockSpec(sel_a.shape, lambda i: (0, 0)),
                  pl.BlockSpec(sel_b.shape, lambda i: (0, 0)),
                  pl.BlockSpec(pad.shape, lambda i: (0, 0))],
        out_specs=[pl.BlockSpec((tr, PEER_SEL), lambda i: (i, 0))] * 2,
        out_shape=[jax.ShapeDtypeStruct((n, PEER_SEL), jnp.int32),
                   jax.ShapeDtypeStruct((n, PEER_SEL), F32)],
        scratch_shapes=[pltpu.VMEM((2 * PEER_HEADS, tr, LANES), BF16),
                        pltpu.VMEM((PEER_HEADS, PEER_TOPK, LANES), jnp.int32),
                        pltpu.VMEM((PEER_HEADS, PEER_TOPK, LANES), F32)],
        compiler_params=pltpu.CompilerParams(
            dimension_semantics=("parallel",), vmem_limit_bytes=VMEM_LIMIT),
        name="peer_route",
    )(xn2, w_q, subkeys, sel_a, sel_b, pad)


def _sc_worker_id():
    return lax.axis_index("s") * SC_CORES + lax.axis_index("c")


def _sc_mesh():
    return plsc.VectorSubcoreMesh(core_axis_name="c", subcore_axis_name="s")


def _sc_gather_ring(table_hbm, idx_v, bufs, sems, unit, compute):
    def gather(g, k):
        off = pl.multiple_of(g * SC_ROWS, SC_ROWS)
        return pltpu.make_async_copy(table_hbm.at[idx_v.at[pl.ds(off, SC_ROWS)]], bufs[k], sems[k])

    for k in range(SC_NBUF):
        gather(k, k).start()

    units_per_iter = SC_NBUF // unit

    def ring_iter(i, carry):
        for m in range(units_per_iter):
            ks = range(m * unit, (m + 1) * unit)
            for k in ks:
                gather(i * SC_NBUF + k, k).wait()
            compute(i * units_per_iter + m, [bufs[k] for k in ks])
            for k in ks:
                g_next = (i + 1) * SC_NBUF + k

                @pl.when(g_next < SC_GATHERS)
                def _():
                    gather(g_next, k).start()
        return carry

    lax.fori_loop(0, SC_GATHERS // SC_NBUF, ring_iter, 0)


def _pack_table(t):
    lo = lax.bitcast_convert_type(t[:, :HALF].astype(jnp.bfloat16), jnp.uint16).astype(jnp.int32)
    hi_bits = lax.bitcast_convert_type(t[:, HALF:], jnp.int32)
    sign = hi_bits & jnp.int32(-2 ** 31)
    mag = jnp.minimum(hi_bits & jnp.int32(2 ** 31 - 1), jnp.int32(0x7F000000))
    upper = jnp.maximum(mag + (1 << 15) - lo, 0) >> 16
    words = sign | (upper << 16) | lo
    return lax.bitcast_convert_type(words, jnp.uint32).reshape(t.shape[0], ROW_TILES, LANES)


def _unpack_words(w):
    lo = lax.bitcast_convert_type(w << 16, F32)
    hi = lax.bitcast_convert_type(w, F32)
    return lo, hi


def _sc_scratch(staged_cols, out_cols):
    return ([pltpu.VMEM((SC_TOK_BATCH * PEER_SEL,), jnp.int32),
             pltpu.VMEM((SC_TOK_BATCH, staged_cols), F32),
             pltpu.VMEM((SC_TOK_BATCH, out_cols), F32)]
            + [pltpu.VMEM((SC_ROWS, ROW_TILES, LANES), jnp.uint32) for _ in range(SC_NBUF)]
            + [pltpu.SemaphoreType.DMA for _ in range(SC_NBUF)])


def _sc_batches(tpw, idx_hbm, staged_hbm, out_hbm, idx_v, staged_v, out_v, run):
    tok0 = _sc_worker_id() * tpw

    def batch(tb, carry):
        t0 = pl.multiple_of(tok0 + tb * SC_TOK_BATCH, SC_TOK_BATCH)
        pltpu.sync_copy(idx_hbm.at[pl.ds(pl.multiple_of(t0 * PEER_SEL, SC_TOK_BATCH * PEER_SEL),
                                         SC_TOK_BATCH * PEER_SEL)], idx_v)
        pltpu.sync_copy(staged_hbm.at[pl.ds(t0, SC_TOK_BATCH)], staged_v)
        run()
        pltpu.sync_copy(out_v, out_hbm.at[pl.ds(t0, SC_TOK_BATCH)])
        return carry

    lax.fori_loop(0, tpw // SC_TOK_BATCH, batch, 0)


def _peer_dot_sc(u3, idx_flat, xn2):
    t = xn2.shape[0]
    unit = 16 // SC_ROWS
    units_per_tok = PEER_SEL // 16

    @functools.partial(
        pl.kernel,
        mesh=_sc_mesh(),
        out_type=jax.ShapeDtypeStruct((t, PEER_SEL), F32),
        scratch_types=[pltpu.VMEM((16 * SC_LANES,), F32)] + _sc_scratch(D_MODEL, PEER_SEL),
        compiler_params=pltpu.CompilerParams(needs_layout_passes=False),
        name="peer_dot",
    )
    def body(u_hbm, idx_hbm, x_hbm, out_hbm, red_v, idx_v, x_v, out_v, *rest):
        bufs, sems = rest[:SC_NBUF], rest[SC_NBUF:]

        def compute(u, ub):
            tl = u // units_per_tok
            rb = (u % units_per_tok) * 16

            def cbody(c, acc):
                acc = list(acc)
                c4 = c // (LANES // SC_LANES)
                lane0 = pl.multiple_of((c % (LANES // SC_LANES)) * SC_LANES, SC_LANES)
                col = pl.multiple_of(c * SC_LANES, SC_LANES)
                x_lo = x_v[tl, pl.ds(col, SC_LANES)]
                x_hi = x_v[tl, pl.ds(HALF + col, SC_LANES)]
                for q, buf in enumerate(ub):
                    for r in range(SC_ROWS):
                        a = q * SC_ROWS + r
                        lo, hi = _unpack_words(buf[r, c4, pl.ds(lane0, SC_LANES)])
                        acc[a] = acc[a] + lo * x_lo + hi * x_hi
                return tuple(acc)

            acc = lax.fori_loop(0, HALF // SC_LANES, cbody,
                                tuple(jnp.zeros((SC_LANES,), F32) for _ in range(16)))
            for a in range(16):
                red_v[pl.ds(a * SC_LANES, SC_LANES)] = acc[a]
            lane_base = lax.iota(jnp.int32, SC_LANES) * SC_LANES
            tot = plsc.load_gather(red_v, [lane_base])
            for k in range(1, SC_LANES):
                tot = tot + plsc.load_gather(red_v, [lane_base + k])
            out_v[tl, pl.ds(pl.multiple_of(rb, 16), 16)] = tot

        _sc_batches(t // SC_WORKERS, idx_hbm, x_hbm, out_hbm, idx_v, x_v, out_v,
                    lambda: _sc_gather_ring(u_hbm, idx_v, bufs, sems, unit, compute))

    return body(u3, idx_flat, xn2)


def _peer_combine_sc(v3, idx_flat, coef_b):
    t = coef_b.shape[0]
    unit = 32 // SC_ROWS
    units_per_tok = PEER_SEL // 32

    @functools.partial(
        pl.kernel,
        mesh=_sc_mesh(),
        out_type=jax.ShapeDtypeStruct((t, D_MODEL), F32),
        scratch_types=_sc_scratch(PEER_SEL * SC_LANES, D_MODEL),
        compiler_params=pltpu.CompilerParams(needs_layout_passes=False),
        name="peer_combine",
    )
    def body(v_hbm, idx_hbm, coef_hbm, out_hbm, idx_v, coef_v, out_v, *rest):
        bufs, sems = rest[:SC_NBUF], rest[SC_NBUF:]

        def compute(u, ub):
            tl = u // units_per_tok
            rb = (u % units_per_tok) * 32
            for g in range(ROW_TILES):
                cols = ([pl.ds(g * LANES + l * SC_LANES, SC_LANES) for l in range(8)]
                        + [pl.ds(HALF + g * LANES + l * SC_LANES, SC_LANES) for l in range(8)])

                def rbody(r, acc):
                    acc = list(acc)
                    for q, buf in enumerate(ub):
                        co = pl.multiple_of((rb + q * SC_ROWS + r) * SC_LANES, SC_LANES)
                        cf = coef_v[tl, pl.ds(co, SC_LANES)]
                        for l in range(8):
                            lo, hi = _unpack_words(buf[r, g, pl.ds(l * SC_LANES, SC_LANES)])
                            acc[l] = acc[l] + lo * cf
                            acc[8 + l] = acc[8 + l] + hi * cf
                    return tuple(acc)

                acc = lax.fori_loop(0, SC_ROWS, rbody, tuple(out_v[tl, c] for c in cols))
                for k in range(16):
                    out_v[tl, cols[k]] = acc[k]

        def run():
            def zero(tl, c2):
                for k in range(D_MODEL // SC_LANES):
                    out_v[tl, pl.ds(k * SC_LANES, SC_LANES)] = jnp.zeros((SC_LANES,), F32)
                return c2

            lax.fori_loop(0, SC_TOK_BATCH, zero, 0)
            _sc_gather_ring(v_hbm, idx_v, bufs, sems, unit, compute)

        _sc_batches(t // SC_WORKERS, idx_hbm, coef_hbm, out_hbm, idx_v, coef_v, out_v, run)

    return body(v3, idx_flat, coef_b)


def _coef_kernel(act_ref, g_ref, e_ref, coef_ref):
    act = act_ref[...]
    gelu = 0.5 * act * (1.0 + lax.erf(act * np.float32(math.sqrt(0.5))))
    coef_ref[...] = _dot(g_ref[...] * gelu, e_ref[...], precision=lax.Precision.HIGHEST)


def _coef(act, g):
    n = act.shape[0]
    tr = _row_block(n, 512)
    w = PEER_SEL * SC_LANES
    expand = (jnp.arange(w, dtype=jnp.int32)[None, :] // SC_LANES
              == jnp.arange(PEER_SEL, dtype=jnp.int32)[:, None]).astype(F32)
    blk = pl.BlockSpec((tr, PEER_SEL), lambda i: (i, 0))
    return pl.pallas_call(
        _coef_kernel,
        grid=(n // tr,),
        in_specs=[blk, blk, pl.BlockSpec((PEER_SEL, w), lambda i: (0, 0))],
        out_specs=pl.BlockSpec((tr, w), lambda i: (i, 0)),
        out_shape=jax.ShapeDtypeStruct((n, w), F32),
        compiler_params=pltpu.CompilerParams(
            dimension_semantics=("parallel",), vmem_limit_bytes=VMEM_LIMIT),
        name="peer_coef",
    )(act, g, expand)


def _final_kernel(h_ref, p_ref, g_ref, o_ref):
    o_ref[...] = _rms(h_ref[...] + p_ref[...], g_ref[...])


def _final(h2, peer_out, gain):
    n = h2.shape[0]
    tm = _row_block(n, 1024)
    blk = pl.BlockSpec((tm, D_MODEL), lambda i: (i, 0))
    return pl.pallas_call(
        _final_kernel,
        grid=(n // tm,),
        in_specs=[blk, blk, pl.BlockSpec((1, D_MODEL), lambda i: (0, 0))],
        out_specs=blk,
        out_shape=jax.ShapeDtypeStruct((n, D_MODEL), F32),
        compiler_params=pltpu.CompilerParams(
            dimension_semantics=("parallel",), vmem_limit_bytes=VMEM_LIMIT),
        name="final_norm",
    )(h2, peer_out, gain)


def _pad_heads(w, dk):
    lead = w.shape[:-1]
    w = w.reshape(lead + (HEADS, dk))
    w = jnp.pad(w, [(0, 0)] * len(lead) + [(0, 0), (0, LANES - dk)])
    return w.reshape(lead + (HEADS * LANES,))


def _rot_heads(w, dk):
    lead = w.shape[:-1]
    w = w.reshape(lead + (HEADS, 2, dk // 2))
    w = jnp.stack([-w[..., 1, :], w[..., 0, :]], axis=-2)
    return w.reshape(lead + (HEADS * dk,))


def _pack_in_weights(w_in):
    gq, gk, gv, gg, ga, rq, rk, rv, rg = jnp.split(w_in, np.cumsum(IN_SPLITS)[:-1].tolist(), axis=-1)
    ga = jnp.pad(ga, ((0, 0), (0, 2 * LANES - GATE_RANK)))
    return jnp.concatenate(
        [_pad_heads(gq, DK), _pad_heads(gk, DK), gv, gg,
         _pad_heads(rq, DK), _pad_heads(rk, DK), rv, rg,
         _pad_heads(_rot_heads(rq, DK), DK), _pad_heads(_rot_heads(rk, DK), DK), ga],
        axis=-1).astype(BF16)


def _rope_tables(pos):
    inv_freq = ROPE_BASE ** (-jnp.arange(0, DK, 2, dtype=F32) / DK)
    ang = pos[:, None] * inv_freq[None, :]
    full = lambda t: _pad_heads(jnp.tile(t, (1, 2 * HEADS)), DK)
    return full(jnp.cos(ang)), full(jnp.sin(ang))


def _segment_lengths(s, first):
    seg = s // SEGS_PER_SEQ
    lengths = [seg] * SEGS_PER_SEQ
    if first:
        head = [seg // 8, 3 * seg // 8, seg // 2]
        assert sum(head) == seg and all(n % SC_CALL_TOKENS == 0 for n in head)
        lengths = head + lengths[1:]
    return lengths


def kernel(x, meta_tokens, norm1_gain, w_in, gla_gate_w2, gla_gate_b, gla_norm_gain, ret_norm_gain, ret_norm_bias, w_out, norm2_gain, peer_w_q, peer_subkeys, peer_u, peer_v, final_gain):
    b, s, d = x.shape
    assert d == D_MODEL and (s // SEGS_PER_SEQ) % SC_CALL_TOKENS == 0
    assert norm1_gain.shape[0] == 1, "single-layer block"
    x2d = x.reshape(b * s, d)

    w_all = _pack_in_weights(w_in[0])
    w2p = jnp.pad(_pad_heads(gla_gate_w2[0], DK), ((0, LANES - GATE_RANK), (0, 0))).astype(BF16)
    gbp = _pad_heads(gla_gate_b[0][None, :], DK)
    g1 = norm1_gain[0][None, :]
    ggain = gla_norm_gain[0][None, :]
    rgain = ret_norm_gain[0][None, :]
    rbias = ret_norm_bias[0][None, :]

    h0 = jnp.concatenate([jnp.zeros((N_PAD, d), x.dtype), meta_tokens.astype(x.dtype)], axis=0)
    cos0, sin0 = _rope_tables(jnp.arange(CHUNK, dtype=F32) - N_PAD)
    cos1, sin1 = _rope_tables(jnp.arange(s, dtype=F32) + N_META)
    zero_state = jnp.zeros((HEADS, DV, LANES), F32)
    _, s_meta, r_meta = _mixer(_norm_proj(h0, g1, w_all), cos0, sin0, w2p, gbp, ggain, rgain, rbias,
                               zero_state, zero_state, batch=1, n_invalid=N_PAD)

    w_out_b = w_out[0].astype(BF16)
    g2 = norm2_gain[0][None, :]
    w_q = peer_w_q[0].astype(BF16)
    subkeys = peer_subkeys[0].reshape(2 * PEER_HEADS, PEER_NKEYS, PEER_DKEY // 2).astype(BF16)
    u3 = _pack_table(peer_u[0])
    v3 = _pack_table(peer_v[0])
    gf = final_gain[None, :]

    outs = []
    for bi in range(b):
        s_state, r_state = s_meta, r_meta
        start = 0
        for seg in _segment_lengths(s, first=bi == 0):
            xs = x2d[bi * s + start:bi * s + start + seg]
            pos = slice(start, start + seg)
            start += seg
            o, s_state, r_state = _mixer(_norm_proj(xs, g1, w_all), cos1[pos], sin1[pos], w2p, gbp,
                                         ggain, rgain, rbias, s_state, r_state, batch=1, n_invalid=0)
            h2, xn2 = _out_proj(o, xs, w_out_b, g2)
            idx, g = _route(xn2, w_q, subkeys)
            idx_flat = idx.reshape(-1)
            act = _peer_dot_sc(u3, idx_flat, xn2)
            peer_out = _peer_combine_sc(v3, idx_flat, _coef(act, g))
            outs.append(_final(h2, peer_out, gf))
    return jnp.concatenate(outs, axis=0).reshape(b, s, d)
```

```python
import functools
import math

import jax
import jax.numpy as jnp
import numpy as np
from jax import lax
from jax.experimental import pallas as pl
from jax.experimental.pallas import tpu as pltpu
from jax.experimental.pallas import tpu_sc as plsc

F32 = jnp.float32
BF16 = jnp.bfloat16

D_MODEL = 1024
N_META = 16
CHUNK = 64
N_PAD = CHUNK - N_META
EPS = 1e-6

HEADS = 4
DV = 128
DK = 64
LANES = 128
GATE_RANK = 16
GATE_TAU = 16.0
ROPE_BASE = 10000.0
IN_SPLITS = (HEADS * DK, HEADS * DK, HEADS * DV, HEADS * DV, GATE_RANK,
             HEADS * DK, HEADS * DK, HEADS * DV, HEADS * DV)

PEER_HEADS = 8
PEER_NKEYS = 128
PEER_DKEY = 256
PEER_TOPK = 16
PEER_SEL = PEER_HEADS * PEER_TOPK

HW = HEADS * LANES
OFF_GQ, OFF_GK, OFF_GV, OFF_GG = 0, HW, 2 * HW, 3 * HW
OFF_RQ, OFF_RK, OFF_RV, OFF_RG = 4 * HW, 5 * HW, 6 * HW, 7 * HW
OFF_RQR, OFF_RKR, OFF_GA = 8 * HW, 9 * HW, 10 * HW
PROJ_W = 10 * HW + 2 * LANES
PROJ_TN = PROJ_W // 3

LOG_GAMMA = tuple(math.log(1.0 - 2.0 ** (-5.0 - h)) for h in range(HEADS))

SC_CORES, SC_SUBCORES, SC_LANES = 2, 16, 16
SC_WORKERS = SC_CORES * SC_SUBCORES
SC_TOK_BATCH = 16
SC_ROWS = 16
SC_NBUF = 8
SC_GATHERS = SC_TOK_BATCH * PEER_SEL // SC_ROWS
HALF = D_MODEL // 2
ROW_TILES = HALF // LANES

SC_CALL_TOKENS = SC_WORKERS * SC_TOK_BATCH
SEGS_PER_SEQ = 2

VMEM_LIMIT = 48 * 1024 * 1024


def _row_block(n, target):
    blk = math.gcd(n, target)
    assert blk % 8 == 0 or blk == n, (n, target)
    return blk


def _dot(a, b, **kw):
    return jnp.dot(a, b, preferred_element_type=F32, **kw)


def _dot_nt(a, b):
    return lax.dot_general(a, b, (((1,), (1,)), ((), ())), preferred_element_type=F32)


def _dot_tn(a, b):
    return lax.dot_general(a, b, (((0,), (0,)), ((), ())), preferred_element_type=F32)


def _rms(x, g):
    return x * lax.rsqrt(jnp.mean(x * x, axis=-1, keepdims=True) + EPS) * g


def _silu(x):
    return x * jax.nn.sigmoid(x)


def _norm_proj_kernel(x_ref, g_ref, w_ref, o_ref):
    xn = _rms(x_ref[...], g_ref[...]).astype(BF16)
    o_ref[...] = _dot(xn, w_ref[...])


def _norm_proj(x2d, gain, w_all):
    n = x2d.shape[0]
    tm = _row_block(n, 512)
    return pl.pallas_call(
        _norm_proj_kernel,
        grid=(n // tm, PROJ_W // PROJ_TN),
        in_specs=[
            pl.BlockSpec((tm, D_MODEL), lambda i, j: (i, 0)),
            pl.BlockSpec((1, D_MODEL), lambda i, j: (0, 0)),
            pl.BlockSpec((D_MODEL, PROJ_TN), lambda i, j: (0, j)),
        ],
        out_specs=pl.BlockSpec((tm, PROJ_TN), lambda i, j: (i, j)),
        out_shape=jax.ShapeDtypeStruct((n, PROJ_W), F32),
        compiler_params=pltpu.CompilerParams(
            dimension_semantics=("parallel", "parallel"), vmem_limit_bytes=VMEM_LIMIT),
        name="norm_in_proj",
    )(x2d, gain, w_all)


def _mixer_kernel(proj_ref, cos_ref, sin_ref, w2_ref, gb_ref, ggain_ref, rgain_ref, rbias_ref,
                  s0_ref, r0_ref, o_ref, s_out_ref, r_out_ref, s_scr, r_scr, *, n_invalid, rows_blk):
    j = pl.program_id(1)

    @pl.when(j == 0)
    def _():
        s_scr[...] = s0_ref[...]
        r_scr[...] = r0_ref[...]

    row = lax.broadcasted_iota(jnp.int32, (CHUNK, CHUNK), 0)
    col = lax.broadcasted_iota(jnp.int32, (CHUNK, CHUNK), 1)
    causal = row >= col
    tril = causal.astype(F32)
    diff = jnp.maximum(row - col, 0).astype(F32)
    rowf = lax.broadcasted_iota(jnp.int32, (CHUNK, LANES), 0).astype(F32)
    row1 = lax.broadcasted_iota(jnp.int32, (CHUNK, 1), 0)

    def chunk_body(c, carry):
        r0 = pl.multiple_of(c * CHUNK, CHUNK)
        rows = pl.ds(r0, CHUNK)
        valid = ((j * rows_blk + r0 + row1) >= n_invalid).astype(F32)

        ga = proj_ref[rows, OFF_GA:OFF_GA + LANES].astype(BF16)
        pre = _dot(ga, w2_ref[...]) + gb_ref[...]
        log_a = jax.nn.log_sigmoid(pre) * (1.0 / GATE_TAU)
        bcum = _dot(tril, log_a, precision=lax.Precision.HIGHEST)
        b_last = bcum[CHUNK - 1:CHUNK, :]
        gq = proj_ref[rows, OFF_GQ:OFF_GQ + HW] * (DK ** -0.5)
        gk = proj_ref[rows, OFF_GK:OFF_GK + HW] * valid
        q_t = (gq * jnp.exp(bcum)).astype(BF16)
        k_t = (gk * jnp.exp(-bcum)).astype(BF16)
        k_end = (gk * jnp.exp(b_last - bcum)).astype(BF16)
        decay = jnp.exp(b_last)
        for h in range(HEADS):
            sl = slice(h * LANES, (h + 1) * LANES)
            v = proj_ref[rows, OFF_GV + h * DV:OFF_GV + (h + 1) * DV].astype(BF16)
            att = jnp.where(causal, _dot_nt(q_t[:, sl], k_t[:, sl]), 0.0).astype(BF16)
            s_prev = s_scr[h]
            o = _dot(att, v) + _dot_nt(q_t[:, sl], s_prev.astype(BF16))
            s_scr[h] = s_prev * decay[:, sl] + _dot_tn(v, k_end[:, sl])
            o = o * lax.rsqrt(jnp.mean(o * o, axis=-1, keepdims=True) + EPS) * ggain_ref[...]
            gate = proj_ref[rows, OFF_GG + h * DV:OFF_GG + (h + 1) * DV]
            o_ref[rows, h * DV:(h + 1) * DV] = (o * _silu(gate)).astype(o_ref.dtype)

        cosb = cos_ref[rows, :]
        sinb = sin_ref[rows, :]
        rq = proj_ref[rows, OFF_RQ:OFF_RQ + HW]
        rqr = proj_ref[rows, OFF_RQR:OFF_RQR + HW]
        rk = proj_ref[rows, OFF_RK:OFF_RK + HW] * valid
        rkr = proj_ref[rows, OFF_RKR:OFF_RKR + HW] * valid
        q_r = (rq * cosb + rqr * sinb).astype(BF16)
        k_r = (rk * cosb + rkr * sinb) * (DK ** -0.5)
        for h in range(HEADS):
            sl = slice(h * LANES, (h + 1) * LANES)
            lg = LOG_GAMMA[h]
            dmat = jnp.where(causal, jnp.exp(lg * diff), 0.0)
            xi = jnp.exp(lg * (rowf + 1.0))
            zeta = jnp.exp(lg * (CHUNK - 1.0 - rowf))
            v = proj_ref[rows, OFF_RV + h * DV:OFF_RV + (h + 1) * DV].astype(BF16)
            k_h = k_r[:, sl]
            att = (_dot_nt(q_r[:, sl], k_h.astype(BF16)) * dmat).astype(BF16)
            r_prev = r_scr[h]
            o = _dot(att, v) + _dot_nt(q_r[:, sl], r_prev.astype(BF16)) * xi
            r_scr[h] = r_prev * math.exp(lg * CHUNK) + _dot_tn(v, (k_h * zeta).astype(BF16))
            mu = jnp.mean(o, axis=-1, keepdims=True)
            var = jnp.mean(jnp.square(o - mu), axis=-1, keepdims=True)
            o = (o - mu) * lax.rsqrt(var + EPS) * rgain_ref[...] + rbias_ref[...]
            gate = proj_ref[rows, OFF_RG + h * DV:OFF_RG + (h + 1) * DV]
            o_ref[rows, HW + h * DV:HW + (h + 1) * DV] = (o * _silu(gate)).astype(o_ref.dtype)
        return carry

    lax.fori_loop(0, rows_blk // CHUNK, chunk_body, 0)
    s_out_ref[...] = s_scr[...]
    r_out_ref[...] = r_scr[...]


def _mixer(proj, cosf, sinf, w2p, gbp, ggain, rgain, rbias, s0, r0, *, batch, n_invalid):
    rows = proj.shape[0] // batch
    rows_blk = _row_block(rows, 256)
    nblk = rows // rows_blk
    const2 = lambda b, j: (0, 0)
    const3 = lambda b, j: (0, 0, 0)
    state = jax.ShapeDtypeStruct((HEADS, DV, LANES), F32)
    return pl.pallas_call(
        functools.partial(_mixer_kernel, n_invalid=n_invalid, rows_blk=rows_blk),
        grid=(batch, nblk),
        in_specs=[
            pl.BlockSpec((rows_blk, PROJ_W), lambda b, j: (b * nblk + j, 0)),
            pl.BlockSpec((rows_blk, HW), lambda b, j: (j, 0)),
            pl.BlockSpec((rows_blk, HW), lambda b, j: (j, 0)),
            pl.BlockSpec((LANES, HW), const2),
            pl.BlockSpec((1, HW), const2),
            pl.BlockSpec((1, DV), const2),
            pl.BlockSpec((1, DV), const2),
            pl.BlockSpec((1, DV), const2),
            pl.BlockSpec((HEADS, DV, LANES), const3),
            pl.BlockSpec((HEADS, DV, LANES), const3),
        ],
        out_specs=[
            pl.BlockSpec((rows_blk, D_MODEL), lambda b, j: (b * nblk + j, 0)),
            pl.BlockSpec((HEADS, DV, LANES), const3),
            pl.BlockSpec((HEADS, DV, LANES), const3),
        ],
        out_shape=[jax.ShapeDtypeStruct((batch * rows, D_MODEL), BF16), state, state],
        scratch_shapes=[pltpu.VMEM((HEADS, DV, LANES), F32), pltpu.VMEM((HEADS, DV, LANES), F32)],
        compiler_params=pltpu.CompilerParams(
            dimension_semantics=("arbitrary", "arbitrary"), vmem_limit_bytes=VMEM_LIMIT),
        name="token_mixers",
    )(proj, cosf, sinf, w2p, gbp, ggain, rgain, rbias, s0, r0)


def _out_proj_kernel(o_ref, x_ref, w_ref, g_ref, h_ref, xn_ref):
    h = x_ref[...] + _dot(o_ref[...], w_ref[...])
    h_ref[...] = h
    xn_ref[...] = _rms(h, g_ref[...])


def _out_proj(o, x2d, w_out, gain2):
    n = x2d.shape[0]
    tm = _row_block(n, 512)
    blk = pl.BlockSpec((tm, D_MODEL), lambda i: (i, 0))
    return pl.pallas_call(
        _out_proj_kernel,
        grid=(n // tm,),
        in_specs=[blk, blk, pl.BlockSpec((D_MODEL, D_MODEL), lambda i: (0, 0)),
                  pl.BlockSpec((1, D_MODEL), lambda i: (0, 0))],
        out_specs=[blk, blk],
        out_shape=[jax.ShapeDtypeStruct((n, D_MODEL), F32)] * 2,
        compiler_params=pltpu.CompilerParams(
            dimension_semantics=("parallel",), vmem_limit_bytes=VMEM_LIMIT),
        name="out_proj_norm",
    )(o, x2d, w_out, gain2)


def _top_k_rows(arrays, payloads=None):
    arrays = list(arrays)
    n = arrays[0].shape[0]
    rowi = lax.broadcasted_iota(jnp.int32, arrays[0].shape, 0)
    vals = [[] for _ in arrays]
    picks = [[] for _ in arrays]
    for _ in range(PEER_TOPK):
        for k, s in enumerate(arrays):
            m = jnp.max(s, axis=0, keepdims=True)
            am = jnp.min(jnp.where(s == m, rowi, n), axis=0, keepdims=True)
            sel = rowi == am
            vals[k].append(m)
            if payloads is None:
                picks[k].append(am)
            else:
                picks[k].append(jnp.min(jnp.where(sel, payloads[k], jnp.iinfo(jnp.int32).max),
                                        axis=0, keepdims=True))
            arrays[k] = jnp.where(sel, -jnp.inf, s)
    return [(jnp.concatenate(v, axis=0), jnp.concatenate(p, axis=0)) for v, p in zip(vals, picks)]


def _candidate_cells():
    cells = [(a, b) for a in range(PEER_TOPK) for b in range(PEER_TOPK) if (a + 1) * (b + 1) <= PEER_TOPK]
    n = -(-len(cells) // 8) * 8
    sel_a = np.zeros((n, PEER_TOPK), np.float32)
    sel_b = np.zeros((n, PEER_TOPK), np.float32)
    pad = np.full((n, 1), -np.inf, np.float32)
    for r, (a, b) in enumerate(cells):
        sel_a[r, a] = 1.0
        sel_b[r, b] = 1.0
        pad[r, 0] = 0.0
    return sel_a, sel_b, pad


def _route_kernel(xn_ref, wq_ref, sk_ref, sa_ref, sb_ref, pad_ref, idx_ref, g_ref, q_scr, i_scr, g_scr):
    tr = xn_ref.shape[0]
    q = _dot(xn_ref[...].astype(BF16), wq_ref[...])
    for hc in range(2 * PEER_HEADS):
        q_scr[hc] = q[:, hc * LANES:(hc + 1) * LANES].astype(BF16)

    def sub_body(sb, carry):
        rows = pl.ds(pl.multiple_of(sb * LANES, LANES), LANES)

        def head_pair(hp, c2):
            heads = [2 * hp, 2 * hp + 1]
            scores = [_dot_nt(sk_ref[2 * h + c], q_scr[2 * h + c, rows, :])
                      for h in heads for c in range(2)]
            first = _top_k_rows(scores)
            sa, sb = sa_ref[...], sb_ref[...]
            cands, cidxs = [], []
            for k in range(len(heads)):
                (v0, i0), (v1, i1) = first[2 * k], first[2 * k + 1]
                cands.append((_dot(sa, v0, precision=lax.Precision.HIGHEST)
                              + _dot(sb, v1, precision=lax.Precision.HIGHEST)) + pad_ref[...])
                cidxs.append((_dot(sa, i0.astype(F32)) * PEER_NKEYS
                              + _dot(sb, i1.astype(F32))).astype(jnp.int32))
            for h, (best, e) in zip(heads, _top_k_rows(cands, cidxs)):
                ex = jnp.exp(best - best[0:1])
                i_scr[h] = e
                g_scr[h] = ex / jnp.sum(ex, axis=0, keepdims=True)
            return c2

        lax.fori_loop(0, PEER_HEADS // 2, head_pair, 0)
        idx_ref[rows, :] = i_scr[...].reshape(PEER_SEL, LANES).T
        g_ref[rows, :] = g_scr[...].reshape(PEER_SEL, LANES).T
        return carry

    lax.fori_loop(0, tr // LANES, sub_body, 0)


def _route(xn2, w_q, subkeys):
    n = xn2.shape[0]
    sel_a, sel_b, pad = _candidate_cells()
    tr = _row_block(n, 512)
    qw = 2 * PEER_HEADS * LANES
    return pl.pallas_call(
        _route_kernel,
        grid=(n // tr,),
        in_specs=[pl.BlockSpec((tr, D_MODEL), lambda i: (i, 0)),
                  pl.BlockSpec((D_MODEL, qw), lambda i: (0, 0)),
                  pl.BlockSpec((2 * PEER_HEADS, PEER_NKEYS, LANES), lambda i: (0, 0, 0)),
                  pl.BlockSpec(sel_a.shape, lambda i: (0, 0)),
                  pl.BlockSpec(sel_b.shape, lambda i: (0, 0)),
                  pl.BlockSpec(pad.shape, lambda i: (0, 0))],
        out_specs=[pl.BlockSpec((tr, PEER_SEL), lambda i: (i, 0))] * 2,
        out_shape=[jax.ShapeDtypeStruct((n, PEER_SEL), jnp.int32),
                   jax.ShapeDtypeStruct((n, PEER_SEL), F32)],
        scratch_shapes=[pltpu.VMEM((2 * PEER_HEADS, tr, LANES), BF16),
                        pltpu.VMEM((PEER_HEADS, PEER_TOPK, LANES), jnp.int32),
                        pltpu.VMEM((PEER_HEADS, PEER_TOPK, LANES), F32)],
        compiler_params=pltpu.CompilerParams(
            dimension_semantics=("parallel",), vmem_limit_bytes=VMEM_LIMIT),
        name="peer_route",
    )(xn2, w_q, subkeys, sel_a, sel_b, pad)


def _sc_worker_id():
    return lax.axis_index("s") * SC_CORES + lax.axis_index("c")


def _sc_mesh():
    return plsc.VectorSubcoreMesh(core_axis_name="c", subcore_axis_name="s")


def _sc_gather_ring(table_hbm, idx_v, bufs, sems, unit, compute):
    def gather(g, k):
        off = pl.multiple_of(g * SC_ROWS, SC_ROWS)
        return pltpu.make_async_copy(table_hbm.at[idx_v.at[pl.ds(off, SC_ROWS)]], bufs[k], sems[k])

    for k in range(SC_NBUF):
        gather(k, k).start()

    units_per_iter = SC_NBUF // unit

    def ring_iter(i, carry):
        for m in range(units_per_iter):
            ks = range(m * unit, (m + 1) * unit)
            for k in ks:
                gather(i * SC_NBUF + k, k).wait()
            compute(i * units_per_iter + m, [bufs[k] for k in ks])
            for k in ks:
                g_next = (i + 1) * SC_NBUF + k

                @pl.when(g_next < SC_GATHERS)
                def _():
                    gather(g_next, k).start()
        return carry

    lax.fori_loop(0, SC_GATHERS // SC_NBUF, ring_iter, 0)


def _pack_table(t):
    lo = lax.bitcast_convert_type(t[:, :HALF].astype(jnp.bfloat16), jnp.uint16).astype(jnp.int32)
    hi_bits = lax.bitcast_convert_type(t[:, HALF:], jnp.int32)
    sign = hi_bits & jnp.int32(-2 ** 31)
    mag = jnp.minimum(hi_bits & jnp.int32(2 ** 31 - 1), jnp.int32(0x7F000000))
    upper = jnp.maximum(mag + (1 << 15) - lo, 0) >> 16
    words = sign | (upper << 16) | lo
    return lax.bitcast_convert_type(words, jnp.uint32).reshape(t.shape[0], ROW_TILES, LANES)


def _unpack_words(w):
    lo = lax.bitcast_convert_type(w << 16, F32)
    hi = lax.bitcast_convert_type(w, F32)
    return lo, hi


def _sc_scratch(staged_cols, out_cols):
    return ([pltpu.VMEM((SC_TOK_BATCH * PEER_SEL,), jnp.int32),
             pltpu.VMEM((SC_TOK_BATCH, staged_cols), F32),
             pltpu.VMEM((SC_TOK_BATCH, out_cols), F32)]
            + [pltpu.VMEM((SC_ROWS, ROW_TILES, LANES), jnp.uint32) for _ in range(SC_NBUF)]
            + [pltpu.SemaphoreType.DMA for _ in range(SC_NBUF)])


def _sc_batches(tpw, idx_hbm, staged_hbm, out_hbm, idx_v, staged_v, out_v, run):
    tok0 = _sc_worker_id() * tpw

    def batch(tb, carry):
        t0 = pl.multiple_of(tok0 + tb * SC_TOK_BATCH, SC_TOK_BATCH)
        pltpu.sync_copy(idx_hbm.at[pl.ds(pl.multiple_of(t0 * PEER_SEL, SC_TOK_BATCH * PEER_SEL),
                                         SC_TOK_BATCH * PEER_SEL)], idx_v)
        pltpu.sync_copy(staged_hbm.at[pl.ds(t0, SC_TOK_BATCH)], staged_v)
        run()
        pltpu.sync_copy(out_v, out_hbm.at[pl.ds(t0, SC_TOK_BATCH)])
        return carry

    lax.fori_loop(0, tpw // SC_TOK_BATCH, batch, 0)


def _peer_dot_sc(u3, idx_flat, xn2):
    t = xn2.shape[0]
    unit = 16 // SC_ROWS
    units_per_tok = PEER_SEL // 16

    @functools.partial(
        pl.kernel,
        mesh=_sc_mesh(),
        out_type=jax.ShapeDtypeStruct((t, PEER_SEL), F32),
        scratch_types=[pltpu.VMEM((16 * SC_LANES,), F32)] + _sc_scratch(D_MODEL, PEER_SEL),
        compiler_params=pltpu.CompilerParams(needs_layout_passes=False),
        name="peer_dot",
    )
    def body(u_hbm, idx_hbm, x_hbm, out_hbm, red_v, idx_v, x_v, out_v, *rest):
        bufs, sems = rest[:SC_NBUF], rest[SC_NBUF:]

        def compute(u, ub):
            tl = u // units_per_tok
            rb = (u % units_per_tok) * 16

            def cbody(c, acc):
                acc = list(acc)
                c4 = c // (LANES // SC_LANES)
                lane0 = pl.multiple_of((c % (LANES // SC_LANES)) * SC_LANES, SC_LANES)
                col = pl.multiple_of(c * SC_LANES, SC_LANES)
                x_lo = x_v[tl, pl.ds(col, SC_LANES)]
                x_hi = x_v[tl, pl.ds(HALF + col, SC_LANES)]
                for q, buf in enumerate(ub):
                    for r in range(SC_ROWS):
                        a = q * SC_ROWS + r
                        lo, hi = _unpack_words(buf[r, c4, pl.ds(lane0, SC_LANES)])
                        acc[a] = acc[a] + lo * x_lo + hi * x_hi
                return tuple(acc)

            acc = lax.fori_loop(0, HALF // SC_LANES, cbody,
                                tuple(jnp.zeros((SC_LANES,), F32) for _ in range(16)))
            for a in range(16):
                red_v[pl.ds(a * SC_LANES, SC_LANES)] = acc[a]
            lane_base = lax.iota(jnp.int32, SC_LANES) * SC_LANES
            tot = plsc.load_gather(red_v, [lane_base])
            for k in range(1, SC_LANES):
                tot = tot + plsc.load_gather(red_v, [lane_base + k])
            out_v[tl, pl.ds(pl.multiple_of(rb, 16), 16)] = tot

        _sc_batches(t // SC_WORKERS, idx_hbm, x_hbm, out_hbm, idx_v, x_v, out_v,
                    lambda: _sc_gather_ring(u_hbm, idx_v, bufs, sems, unit, compute))

    return body(u3, idx_flat, xn2)


def _peer_combine_sc(v3, idx_flat, coef_b):
    t = coef_b.shape[0]
    unit = 32 // SC_ROWS
    units_per_tok = PEER_SEL // 32

    @functools.partial(
        pl.kernel,
        mesh=_sc_mesh(),
        out_type=jax.ShapeDtypeStruct((t, D_MODEL), F32),
        scratch_types=_sc_scratch(PEER_SEL * SC_LANES, D_MODEL),
        compiler_params=pltpu.CompilerParams(needs_layout_passes=False),
        name="peer_combine",
    )
    def body(v_hbm, idx_hbm, coef_hbm, out_hbm, idx_v, coef_v, out_v, *rest):
        bufs, sems = rest[:SC_NBUF], rest[SC_NBUF:]

        def compute(u, ub):
            tl = u // units_per_tok
            rb = (u % units_per_tok) * 32
            for g in range(ROW_TILES):
                cols = ([pl.ds(g * LANES + l * SC_LANES, SC_LANES) for l in range(8)]
                        + [pl.ds(HALF + g * LANES + l * SC_LANES, SC_LANES) for l in range(8)])

                def rbody(r, acc):
                    acc = list(acc)
                    for q, buf in enumerate(ub):
                        co = pl.multiple_of((rb + q * SC_ROWS + r) * SC_LANES, SC_LANES)
                        cf = coef_v[tl, pl.ds(co, SC_LANES)]
                        for l in range(8):
                            lo, hi = _unpack_words(buf[r, g, pl.ds(l * SC_LANES, SC_LANES)])
                            acc[l] = acc[l] + lo * cf
                            acc[8 + l] = acc[8 + l] + hi * cf
                    return tuple(acc)

                acc = lax.fori_loop(0, SC_ROWS, rbody, tuple(out_v[tl, c] for c in cols))
                for k in range(16):
                    out_v[tl, cols[k]] = acc[k]

        def run():
            def zero(tl, c2):
                for k in range(D_MODEL // SC_LANES):
                    out_v[tl, pl.ds(k * SC_LANES, SC_LANES)] = jnp.zeros((SC_LANES,), F32)
                return c2

            lax.fori_loop(0, SC_TOK_BATCH, zero, 0)
            _sc_gather_ring(v_hbm, idx_v, bufs, sems, unit, compute)

        _sc_batches(t // SC_WORKERS, idx_hbm, coef_hbm, out_hbm, idx_v, coef_v, out_v, run)

    return body(v3, idx_flat, coef_b)


def _coef_kernel(act_ref, g_ref, e_ref, coef_ref):
    act = act_ref[...]
    gelu = 0.5 * act * (1.0 + lax.erf(act * np.float32(math.sqrt(0.5))))
    coef_ref[...] = _dot(g_ref[...] * gelu, e_ref[...], precision=lax.Precision.HIGHEST)


def _coef(act, g):
    n = act.shape[0]
    tr = _row_block(n, 512)
    w = PEER_SEL * SC_LANES
    expand = (jnp.arange(w, dtype=jnp.int32)[None, :] // SC_LANES
              == jnp.arange(PEER_SEL, dtype=jnp.int32)[:, None]).astype(F32)
    blk = pl.BlockSpec((tr, PEER_SEL), lambda i: (i, 0))
    return pl.pallas_call(
        _coef_kernel,
        grid=(n // tr,),
        in_specs=[blk, blk, pl.BlockSpec((PEER_SEL, w), lambda i: (0, 0))],
        out_specs=pl.BlockSpec((tr, w), lambda i: (i, 0)),
        out_shape=jax.ShapeDtypeStruct((n, w), F32),
        compiler_params=pltpu.CompilerParams(
            dimension_semantics=("parallel",), vmem_limit_bytes=VMEM_LIMIT),
        name="peer_coef",
    )(act, g, expand)


def _final_kernel(h_ref, p_ref, g_ref, o_ref):
    o_ref[...] = _rms(h_ref[...] + p_ref[...], g_ref[...])


def _final(h2, peer_out, gain):
    n = h2.shape[0]
    tm = _row_block(n, 1024)
    blk = pl.BlockSpec((tm, D_MODEL), lambda i: (i, 0))
    return pl.pallas_call(
        _final_kernel,
        grid=(n // tm,),
        in_specs=[blk, blk, pl.BlockSpec((1, D_MODEL), lambda i: (0, 0))],
        out_specs=blk,
        out_shape=jax.ShapeDtypeStruct((n, D_MODEL), F32),
        compiler_params=pltpu.CompilerParams(
            dimension_semantics=("parallel",), vmem_limit_bytes=VMEM_LIMIT),
        name="final_norm",
    )(h2, peer_out, gain)


def _pad_heads(w, dk):
    lead = w.shape[:-1]
    w = w.reshape(lead + (HEADS, dk))
    w = jnp.pad(w, [(0, 0)] * len(lead) + [(0, 0), (0, LANES - dk)])
    return w.reshape(lead + (HEADS * LANES,))


def _rot_heads(w, dk):
    lead = w.shape[:-1]
    w = w.reshape(lead + (HEADS, 2, dk // 2))
    w = jnp.stack([-w[..., 1, :], w[..., 0, :]], axis=-2)
    return w.reshape(lead + (HEADS * dk,))


def _pack_in_weights(w_in):
    gq, gk, gv, gg, ga, rq, rk, rv, rg = jnp.split(w_in, np.cumsum(IN_SPLITS)[:-1].tolist(), axis=-1)
    ga = jnp.pad(ga, ((0, 0), (0, 2 * LANES - GATE_RANK)))
    return jnp.concatenate(
        [_pad_heads(gq, DK), _pad_heads(gk, DK), gv, gg,
         _pad_heads(rq, DK), _pad_heads(rk, DK), rv, rg,
         _pad_heads(_rot_heads(rq, DK), DK), _pad_heads(_rot_heads(rk, DK), DK), ga],
        axis=-1).astype(BF16)


def _rope_tables(pos):
    inv_freq = ROPE_BASE ** (-jnp.arange(0, DK, 2, dtype=F32) / DK)
    ang = pos[:, None] * inv_freq[None, :]
    full = lambda t: _pad_heads(jnp.tile(t, (1, 2 * HEADS)), DK)
    return full(jnp.cos(ang)), full(jnp.sin(ang))


def _segment_lengths(s, first):
    seg = s // SEGS_PER_SEQ
    lengths = [seg] * SEGS_PER_SEQ
    if first:
        head = [seg // 8, 3 * seg // 8, seg // 2]
        assert sum(head) == seg and all(n % SC_CALL_TOKENS == 0 for n in head)
        lengths = head + lengths[1:]
    return lengths


def kernel(x, meta_tokens, norm1_gain, w_in, gla_gate_w2, gla_gate_b, gla_norm_gain, ret_norm_gain, ret_norm_bias, w_out, norm2_gain, peer_w_q, peer_subkeys, peer_u, peer_v, final_gain):
    b, s, d = x.shape
    assert d == D_MODEL and (s // SEGS_PER_SEQ) % SC_CALL_TOKENS == 0
    assert norm1_gain.shape[0] == 1, "single-layer block"
    x2d = x.reshape(b * s, d)

    w_all = _pack_in_weights(w_in[0])
    w2p = jnp.pad(_pad_heads(gla_gate_w2[0], DK), ((0, LANES - GATE_RANK), (0, 0))).astype(BF16)
    gbp = _pad_heads(gla_gate_b[0][None, :], DK)
    g1 = norm1_gain[0][None, :]
    ggain = gla_norm_gain[0][None, :]
    rgain = ret_norm_gain[0][None, :]
    rbias = ret_norm_bias[0][None, :]

    h0 = jnp.concatenate([jnp.zeros((N_PAD, d), x.dtype), meta_tokens.astype(x.dtype)], axis=0)
    cos0, sin0 = _rope_tables(jnp.arange(CHUNK, dtype=F32) - N_PAD)
    cos1, sin1 = _rope_tables(jnp.arange(s, dtype=F32) + N_META)
    zero_state = jnp.zeros((HEADS, DV, LANES), F32)
    _, s_meta, r_meta = _mixer(_norm_proj(h0, g1, w_all), cos0, sin0, w2p, gbp, ggain, rgain, rbias,
                               zero_state, zero_state, batch=1, n_invalid=N_PAD)

    w_out_b = w_out[0].astype(BF16)
    g2 = norm2_gain[0][None, :]
    w_q = peer_w_q[0].astype(BF16)
    subkeys = peer_subkeys[0].reshape(2 * PEER_HEADS, PEER_NKEYS, PEER_DKEY // 2).astype(BF16)
    u3 = _pack_table(peer_u[0])
    v3 = _pack_table(peer_v[0])
    gf = final_gain[None, :]

    outs = []
    prev_g = None
    for bi in range(b):
        s_state, r_state = s_meta, r_meta
        start = 0
        for seg in _segment_lengths(s, first=bi == 0):
            xs = x2d[bi * s + start:bi * s + start + seg]
            pos = slice(start, start + seg)
            start += seg
            if prev_g is not None:
                xs, _ = lax.optimization_barrier((xs, prev_g))
            o, s_state, r_state = _mixer(_norm_proj(xs, g1, w_all), cos1[pos], sin1[pos], w2p, gbp,
                                         ggain, rgain, rbias, s_state, r_state, batch=1, n_invalid=0)
            h2, xn2 = _out_proj(o, xs, w_out_b, g2)
            idx, g = _route(xn2, w_q, subkeys)
            prev_g = g
            idx_flat = idx.reshape(-1)
            act = _peer_dot_sc(u3, idx_flat, xn2)
            peer_out = _peer_combine_sc(v3, idx_flat, _coef(act, g))
            outs.append(_final(h2, peer_out, gf))
    return jnp.concatenate(outs, axis=0).reshape(b, s, d)
```

```python
import functools
import math

import jax
import jax.numpy as jnp
import numpy as np
from jax import lax
from jax.experimental import pallas as pl
from jax.experimental.pallas import tpu as pltpu
from jax.experimental.pallas import tpu_sc as plsc

F32 = jnp.float32
BF16 = jnp.bfloat16

D_MODEL = 1024
N_META = 16
CHUNK = 64
N_PAD = CHUNK - N_META
EPS = 1e-6

HEADS = 4
DV = 128
DK = 64
LANES = 128
GATE_RANK = 16
GATE_TAU = 16.0
ROPE_BASE = 10000.0
IN_SPLITS = (HEADS * DK, HEADS * DK, HEADS * DV, HEADS * DV, GATE_RANK,
             HEADS * DK, HEADS * DK, HEADS * DV, HEADS * DV)

PEER_HEADS = 8
PEER_NKEYS = 128
PEER_DKEY = 256
PEER_TOPK = 16
PEER_SEL = PEER_HEADS * PEER_TOPK

HW = HEADS * LANES
OFF_GQ, OFF_GK, OFF_GV, OFF_GG = 0, HW, 2 * HW, 3 * HW
OFF_RQ, OFF_RK, OFF_RV, OFF_RG = 4 * HW, 5 * HW, 6 * HW, 7 * HW
OFF_RQR, OFF_RKR, OFF_GA = 8 * HW, 9 * HW, 10 * HW
PROJ_W = 10 * HW + 2 * LANES
PROJ_TN = PROJ_W // 3

LOG_GAMMA = tuple(math.log(1.0 - 2.0 ** (-5.0 - h)) for h in range(HEADS))

SC_CORES, SC_SUBCORES, SC_LANES = 2, 16, 16
SC_WORKERS = SC_CORES * SC_SUBCORES
SC_TOK_BATCH = 16
SC_ROWS = 16
SC_NBUF = 8
SC_GATHERS = SC_TOK_BATCH * PEER_SEL // SC_ROWS
HALF = D_MODEL // 2
ROW_TILES = HALF // LANES

SC_CALL_TOKENS = SC_WORKERS * SC_TOK_BATCH
SEGS_PER_SEQ = 2

VMEM_LIMIT = 48 * 1024 * 1024


def _row_block(n, target):
    blk = math.gcd(n, target)
    assert blk % 8 == 0 or blk == n, (n, target)
    return blk


def _dot(a, b, **kw):
    return jnp.dot(a, b, preferred_element_type=F32, **kw)


def _dot_nt(a, b):
    return lax.dot_general(a, b, (((1,), (1,)), ((), ())), preferred_element_type=F32)


def _dot_tn(a, b):
    return lax.dot_general(a, b, (((0,), (0,)), ((), ())), preferred_element_type=F32)


def _rms(x, g):
    return x * lax.rsqrt(jnp.mean(x * x, axis=-1, keepdims=True) + EPS) * g


def _silu(x):
    return x * jax.nn.sigmoid(x)


def _norm_proj_kernel(x_ref, g_ref, w_ref, o_ref):
    xn = _rms(x_ref[...], g_ref[...]).astype(BF16)
    o_ref[...] = _dot(xn, w_ref[...])


def _norm_proj(x2d, gain, w_all):
    n = x2d.shape[0]
    tm = _row_block(n, 512)
    return pl.pallas_call(
        _norm_proj_kernel,
        grid=(n // tm, PROJ_W // PROJ_TN),
        in_specs=[
            pl.BlockSpec((tm, D_MODEL), lambda i, j: (i, 0)),
            pl.BlockSpec((1, D_MODEL), lambda i, j: (0, 0)),
            pl.BlockSpec((D_MODEL, PROJ_TN), lambda i, j: (0, j)),
        ],
        out_specs=pl.BlockSpec((tm, PROJ_TN), lambda i, j: (i, j)),
        out_shape=jax.ShapeDtypeStruct((n, PROJ_W), F32),
        compiler_params=pltpu.CompilerParams(
            dimension_semantics=("parallel", "parallel"), vmem_limit_bytes=VMEM_LIMIT),
        name="norm_in_proj",
    )(x2d, gain, w_all)


def _mixer_kernel(proj_ref, cos_ref, sin_ref, w2_ref, gb_ref, ggain_ref, rgain_ref, rbias_ref,
                  s0_ref, r0_ref, o_ref, s_out_ref, r_out_ref, s_scr, r_scr, *, n_invalid, rows_blk):
    j = pl.program_id(1)

    @pl.when(j == 0)
    def _():
        s_scr[...] = s0_ref[...]
        r_scr[...] = r0_ref[...]

    row = lax.broadcasted_iota(jnp.int32, (CHUNK, CHUNK), 0)
    col = lax.broadcasted_iota(jnp.int32, (CHUNK, CHUNK), 1)
    causal = row >= col
    tril = causal.astype(F32)
    diff = jnp.maximum(row - col, 0).astype(F32)
    rowf = lax.broadcasted_iota(jnp.int32, (CHUNK, LANES), 0).astype(F32)
    row1 = lax.broadcasted_iota(jnp.int32, (CHUNK, 1), 0)

    def chunk_body(c, carry):
        r0 = pl.multiple_of(c * CHUNK, CHUNK)
        rows = pl.ds(r0, CHUNK)
        valid = ((j * rows_blk + r0 + row1) >= n_invalid).astype(F32)

        ga = proj_ref[rows, OFF_GA:OFF_GA + LANES].astype(BF16)
        pre = _dot(ga, w2_ref[...]) + gb_ref[...]
        log_a = jax.nn.log_sigmoid(pre) * (1.0 / GATE_TAU)
        bcum = _dot(tril, log_a, precision=lax.Precision.HIGHEST)
        b_last = bcum[CHUNK - 1:CHUNK, :]
        gq = proj_ref[rows, OFF_GQ:OFF_GQ + HW] * (DK ** -0.5)
        gk = proj_ref[rows, OFF_GK:OFF_GK + HW] * valid
        q_t = (gq * jnp.exp(bcum)).astype(BF16)
        k_t = (gk * jnp.exp(-bcum)).astype(BF16)
        k_end = (gk * jnp.exp(b_last - bcum)).astype(BF16)
        decay = jnp.exp(b_last)
        for h in range(HEADS):
            sl = slice(h * LANES, (h + 1) * LANES)
            v = proj_ref[rows, OFF_GV + h * DV:OFF_GV + (h + 1) * DV].astype(BF16)
            att = jnp.where(causal, _dot_nt(q_t[:, sl], k_t[:, sl]), 0.0).astype(BF16)
            s_prev = s_scr[h]
            o = _dot(att, v) + _dot_nt(q_t[:, sl], s_prev.astype(BF16))
            s_scr[h] = s_prev * decay[:, sl] + _dot_tn(v, k_end[:, sl])
            o = o * lax.rsqrt(jnp.mean(o * o, axis=-1, keepdims=True) + EPS) * ggain_ref[...]
            gate = proj_ref[rows, OFF_GG + h * DV:OFF_GG + (h + 1) * DV]
            o_ref[rows, h * DV:(h + 1) * DV] = (o * _silu(gate)).astype(o_ref.dtype)

        cosb = cos_ref[rows, :]
        sinb = sin_ref[rows, :]
        rq = proj_ref[rows, OFF_RQ:OFF_RQ + HW]
        rqr = proj_ref[rows, OFF_RQR:OFF_RQR + HW]
        rk = proj_ref[rows, OFF_RK:OFF_RK + HW] * valid
        rkr = proj_ref[rows, OFF_RKR:OFF_RKR + HW] * valid
        q_r = (rq * cosb + rqr * sinb).astype(BF16)
        k_r = (rk * cosb + rkr * sinb) * (DK ** -0.5)
        for h in range(HEADS):
            sl = slice(h * LANES, (h + 1) * LANES)
            lg = LOG_GAMMA[h]
            dmat = jnp.where(causal, jnp.exp(lg * diff), 0.0)
            xi = jnp.exp(lg * (rowf + 1.0))
            zeta = jnp.exp(lg * (CHUNK - 1.0 - rowf))
            v = proj_ref[rows, OFF_RV + h * DV:OFF_RV + (h + 1) * DV].astype(BF16)
            k_h = k_r[:, sl]
            att = (_dot_nt(q_r[:, sl], k_h.astype(BF16)) * dmat).astype(BF16)
            r_prev = r_scr[h]
            o = _dot(att, v) + _dot_nt(q_r[:, sl], r_prev.astype(BF16)) * xi
            r_scr[h] = r_prev * math.exp(lg * CHUNK) + _dot_tn(v, (k_h * zeta).astype(BF16))
            mu = jnp.mean(o, axis=-1, keepdims=True)
            var = jnp.mean(jnp.square(o - mu), axis=-1, keepdims=True)
            o = (o - mu) * lax.rsqrt(var + EPS) * rgain_ref[...] + rbias_ref[...]
            gate = proj_ref[rows, OFF_RG + h * DV:OFF_RG + (h + 1) * DV]
            o_ref[rows, HW + h * DV:HW + (h + 1) * DV] = (o * _silu(gate)).astype(o_ref.dtype)
        return carry

    lax.fori_loop(0, rows_blk // CHUNK, chunk_body, 0)
    s_out_ref[...] = s_scr[...]
    r_out_ref[...] = r_scr[...]


def _mixer(proj, cosf, sinf, w2p, gbp, ggain, rgain, rbias, s0, r0, *, batch, n_invalid):
    rows = proj.shape[0] // batch
    rows_blk = _row_block(rows, 256)
    nblk = rows // rows_blk
    const2 = lambda b, j: (0, 0)
    const3 = lambda b, j: (0, 0, 0)
    state = jax.ShapeDtypeStruct((HEADS, DV, LANES), F32)
    return pl.pallas_call(
        functools.partial(_mixer_kernel, n_invalid=n_invalid, rows_blk=rows_blk),
        grid=(batch, nblk),
        in_specs=[
            pl.BlockSpec((rows_blk, PROJ_W), lambda b, j: (b * nblk + j, 0)),
            pl.BlockSpec((rows_blk, HW), lambda b, j: (j, 0)),
            pl.BlockSpec((rows_blk, HW), lambda b, j: (j, 0)),
            pl.BlockSpec((LANES, HW), const2),
            pl.BlockSpec((1, HW), const2),
            pl.BlockSpec((1, DV), const2),
            pl.BlockSpec((1, DV), const2),
            pl.BlockSpec((1, DV), const2),
            pl.BlockSpec((HEADS, DV, LANES), const3),
            pl.BlockSpec((HEADS, DV, LANES), const3),
        ],
        out_specs=[
            pl.BlockSpec((rows_blk, D_MODEL), lambda b, j: (b * nblk + j, 0)),
            pl.BlockSpec((HEADS, DV, LANES), const3),
            pl.BlockSpec((HEADS, DV, LANES), const3),
        ],
        out_shape=[jax.ShapeDtypeStruct((batch * rows, D_MODEL), BF16), state, state],
        scratch_shapes=[pltpu.VMEM((HEADS, DV, LANES), F32), pltpu.VMEM((HEADS, DV, LANES), F32)],
        compiler_params=pltpu.CompilerParams(
            dimension_semantics=("arbitrary", "arbitrary"), vmem_limit_bytes=VMEM_LIMIT),
        name="token_mixers",
    )(proj, cosf, sinf, w2p, gbp, ggain, rgain, rbias, s0, r0)


def _out_proj_kernel(o_ref, x_ref, w_ref, g_ref, h_ref, xn_ref):
    h = x_ref[...] + _dot(o_ref[...], w_ref[...])
    h_ref[...] = h
    xn_ref[...] = _rms(h, g_ref[...])


def _out_proj(o, x2d, w_out, gain2):
    n = x2d.shape[0]
    tm = _row_block(n, 512)
    blk = pl.BlockSpec((tm, D_MODEL), lambda i: (i, 0))
    return pl.pallas_call(
        _out_proj_kernel,
        grid=(n // tm,),
        in_specs=[blk, blk, pl.BlockSpec((D_MODEL, D_MODEL), lambda i: (0, 0)),
                  pl.BlockSpec((1, D_MODEL), lambda i: (0, 0))],
        out_specs=[blk, blk],
        out_shape=[jax.ShapeDtypeStruct((n, D_MODEL), F32)] * 2,
        compiler_params=pltpu.CompilerParams(
            dimension_semantics=("parallel",), vmem_limit_bytes=VMEM_LIMIT),
        name="out_proj_norm",
    )(o, x2d, w_out, gain2)


def _top_k_rows(arrays, payloads=None):
    arrays = list(arrays)
    n = arrays[0].shape[0]
    rowi = lax.broadcasted_iota(jnp.int32, arrays[0].shape, 0)
    vals = [[] for _ in arrays]
    picks = [[] for _ in arrays]
    for _ in range(PEER_TOPK):
        for k, s in enumerate(arrays):
            m = jnp.max(s, axis=0, keepdims=True)
            am = jnp.min(jnp.where(s == m, rowi, n), axis=0, keepdims=True)
            sel = rowi == am
            vals[k].append(m)
            if payloads is None:
                picks[k].append(am)
            else:
                picks[k].append(jnp.min(jnp.where(sel, payloads[k], jnp.iinfo(jnp.int32).max),
                                        axis=0, keepdims=True))
            arrays[k] = jnp.where(sel, -jnp.inf, s)
    return [(jnp.concatenate(v, axis=0), jnp.concatenate(p, axis=0)) for v, p in zip(vals, picks)]


def _candidate_cells():
    cells = [(a, b) for a in range(PEER_TOPK) for b in range(PEER_TOPK) if (a + 1) * (b + 1) <= PEER_TOPK]
    n = -(-len(cells) // 8) * 8
    sel_a = np.zeros((n, PEER_TOPK), np.float32)
    sel_b = np.zeros((n, PEER_TOPK), np.float32)
    pad = np.full((n, 1), -np.inf, np.float32)
    for r, (a, b) in enumerate(cells):
        sel_a[r, a] = 1.0
        sel_b[r, b] = 1.0
        pad[r, 0] = 0.0
    return sel_a, sel_b, pad


def _route_kernel(xn_ref, wq_ref, sk_ref, sa_ref, sb_ref, pad_ref, idx_ref, g_ref, q_scr, i_scr, g_scr):
    tr = xn_ref.shape[0]
    q = _dot(xn_ref[...].astype(BF16), wq_ref[...])
    for hc in range(2 * PEER_HEADS):
        q_scr[hc] = q[:, hc * LANES:(hc + 1) * LANES].astype(BF16)

    def sub_body(sb, carry):
        rows = pl.ds(pl.multiple_of(sb * LANES, LANES), LANES)

        def head_pair(hp, c2):
            heads = [2 * hp, 2 * hp + 1]
            scores = [_dot_nt(sk_ref[2 * h + c], q_scr[2 * h + c, rows, :])
                      for h in heads for c in range(2)]
            first = _top_k_rows(scores)
            sa, sb = sa_ref[...], sb_ref[...]
            cands, cidxs = [], []
            for k in range(len(heads)):
                (v0, i0), (v1, i1) = first[2 * k], first[2 * k + 1]
                cands.append((_dot(sa, v0, precision=lax.Precision.HIGHEST)
                              + _dot(sb, v1, precision=lax.Precision.HIGHEST)) + pad_ref[...])
                cidxs.append((_dot(sa, i0.astype(F32)) * PEER_NKEYS
                              + _dot(sb, i1.astype(F32))).astype(jnp.int32))
            for h, (best, e) in zip(heads, _top_k_rows(cands, cidxs)):
                ex = jnp.exp(best - best[0:1])
                i_scr[h] = e
                g_scr[h] = ex / jnp.sum(ex, axis=0, keepdims=True)
            return c2

        lax.fori_loop(0, PEER_HEADS // 2, head_pair, 0)
        idx_ref[rows, :] = i_scr[...].reshape(PEER_SEL, LANES).T
        g_ref[rows, :] = g_scr[...].reshape(PEER_SEL, LANES).T
        return carry

    lax.fori_loop(0, tr // LANES, sub_body, 0)


def _route(xn2, w_q, subkeys):
    n = xn2.shape[0]
    sel_a, sel_b, pad = _candidate_cells()
    tr = _row_block(n, 512)
    qw = 2 * PEER_HEADS * LANES
    return pl.pallas_call(
        _route_kernel,
        grid=(n // tr,),
        in_specs=[pl.BlockSpec((tr, D_MODEL), lambda i: (i, 0)),
                  pl.BlockSpec((D_MODEL, qw), lambda i: (0, 0)),
                  pl.BlockSpec((2 * PEER_HEADS, PEER_NKEYS, LANES), lambda i: (0, 0, 0)),
                  pl.BlockSpec(sel_a.shape, lambda i: (0, 0)),
                  pl.BlockSpec(sel_b.shape, lambda i: (0, 0)),
                  pl.BlockSpec(pad.shape, lambda i: (0, 0))],
        out_specs=[pl.BlockSpec((tr, PEER_SEL), lambda i: (i, 0))] * 2,
        out_shape=[jax.ShapeDtypeStruct((n, PEER_SEL), jnp.int32),
                   jax.ShapeDtypeStruct((n, PEER_SEL), F32)],
        scratch_shapes=[pltpu.VMEM((2 * PEER_HEADS, tr, LANES), BF16),
                        pltpu.VMEM((PEER_HEADS, PEER_TOPK, LANES), jnp.int32),
                        pltpu.VMEM((PEER_HEADS, PEER_TOPK, LANES), F32)],
        compiler_params=pltpu.CompilerParams(
            dimension_semantics=("parallel",), vmem_limit_bytes=VMEM_LIMIT),
        name="peer_route",
    )(xn2, w_q, subkeys, sel_a, sel_b, pad)


def _sc_worker_id():
    return lax.axis_index("s") * SC_CORES + lax.axis_index("c")


def _sc_mesh():
    return plsc.VectorSubcoreMesh(core_axis_name="c", subcore_axis_name="s")


def _sc_gather_ring(table_hbm, idx_v, bufs, sems, unit, compute):
    def gather(g, k):
        off = pl.multiple_of(g * SC_ROWS, SC_ROWS)
        return pltpu.make_async_copy(table_hbm.at[idx_v.at[pl.ds(off, SC_ROWS)]], bufs[k], sems[k])

    for k in range(SC_NBUF):
        gather(k, k).start()

    units_per_iter = SC_NBUF // unit

    def ring_iter(i, carry):
        for m in range(units_per_iter):
            ks = range(m * unit, (m + 1) * unit)
            for k in ks:
                gather(i * SC_NBUF + k, k).wait()
            compute(i * units_per_iter + m, [bufs[k] for k in ks])
            for k in ks:
                g_next = (i + 1) * SC_NBUF + k

                @pl.when(g_next < SC_GATHERS)
                def _():
                    gather(g_next, k).start()
        return carry

    lax.fori_loop(0, SC_GATHERS // SC_NBUF, ring_iter, 0)


def _pack_table(t):
    lo = lax.bitcast_convert_type(t[:, :HALF].astype(jnp.bfloat16), jnp.uint16).astype(jnp.int32)
    hi_bits = lax.bitcast_convert_type(t[:, HALF:], jnp.int32)
    sign = hi_bits & jnp.int32(-2 ** 31)
    mag = jnp.minimum(hi_bits & jnp.int32(2 ** 31 - 1), jnp.int32(0x7F000000))
    upper = jnp.maximum(mag + (1 << 15) - lo, 0) >> 16
    words = sign | (upper << 16) | lo
    return lax.bitcast_convert_type(words, jnp.uint32).reshape(t.shape[0], ROW_TILES, LANES)


def _unpack_words(w):
    lo = lax.bitcast_convert_type(w << 16, F32)
    hi = lax.bitcast_convert_type(w, F32)
    return lo, hi


def _sc_scratch(staged_cols, out_cols):
    return ([pltpu.VMEM((SC_TOK_BATCH * PEER_SEL,), jnp.int32),
             pltpu.VMEM((SC_TOK_BATCH, staged_cols), F32),
             pltpu.VMEM((SC_TOK_BATCH, out_cols), F32)]
            + [pltpu.VMEM((SC_ROWS, ROW_TILES, LANES), jnp.uint32) for _ in range(SC_NBUF)]
            + [pltpu.SemaphoreType.DMA for _ in range(SC_NBUF)])


def _sc_batches(tpw, idx_hbm, staged_hbm, out_hbm, idx_v, staged_v, out_v, run):
    tok0 = _sc_worker_id() * tpw

    def batch(tb, carry):
        t0 = pl.multiple_of(tok0 + tb * SC_TOK_BATCH, SC_TOK_BATCH)
        pltpu.sync_copy(idx_hbm.at[pl.ds(pl.multiple_of(t0 * PEER_SEL, SC_TOK_BATCH * PEER_SEL),
                                         SC_TOK_BATCH * PEER_SEL)], idx_v)
        pltpu.sync_copy(staged_hbm.at[pl.ds(t0, SC_TOK_BATCH)], staged_v)
        run()
        pltpu.sync_copy(out_v, out_hbm.at[pl.ds(t0, SC_TOK_BATCH)])
        return carry

    lax.fori_loop(0, tpw // SC_TOK_BATCH, batch, 0)


def _peer_dot_sc(u3, idx_flat, xn2):
    t = xn2.shape[0]
    unit = 16 // SC_ROWS
    units_per_tok = PEER_SEL // 16

    @functools.partial(
        pl.kernel,
        mesh=_sc_mesh(),
        out_type=jax.ShapeDtypeStruct((t, PEER_SEL), F32),
        scratch_types=[pltpu.VMEM((16 * SC_LANES,), F32)] + _sc_scratch(D_MODEL, PEER_SEL),
        compiler_params=pltpu.CompilerParams(needs_layout_passes=False),
        name="peer_dot",
    )
    def body(u_hbm, idx_hbm, x_hbm, out_hbm, red_v, idx_v, x_v, out_v, *rest):
        bufs, sems = rest[:SC_NBUF], rest[SC_NBUF:]

        def compute(u, ub):
            tl = u // units_per_tok
            rb = (u % units_per_tok) * 16

            def cbody(c, acc):
                acc = list(acc)
                c4 = c // (LANES // SC_LANES)
                lane0 = pl.multiple_of((c % (LANES // SC_LANES)) * SC_LANES, SC_LANES)
                col = pl.multiple_of(c * SC_LANES, SC_LANES)
                x_lo = x_v[tl, pl.ds(col, SC_LANES)]
                x_hi = x_v[tl, pl.ds(HALF + col, SC_LANES)]
                for q, buf in enumerate(ub):
                    for r in range(SC_ROWS):
                        a = q * SC_ROWS + r
                        lo, hi = _unpack_words(buf[r, c4, pl.ds(lane0, SC_LANES)])
                        acc[a] = acc[a] + lo * x_lo + hi * x_hi
                return tuple(acc)

            acc = lax.fori_loop(0, HALF // SC_LANES, cbody,
                                tuple(jnp.zeros((SC_LANES,), F32) for _ in range(16)))
            for a in range(16):
                red_v[pl.ds(a * SC_LANES, SC_LANES)] = acc[a]
            lane_base = lax.iota(jnp.int32, SC_LANES) * SC_LANES
            tot = plsc.load_gather(red_v, [lane_base])
            for k in range(1, SC_LANES):
                tot = tot + plsc.load_gather(red_v, [lane_base + k])
            out_v[tl, pl.ds(pl.multiple_of(rb, 16), 16)] = tot

        _sc_batches(t // SC_WORKERS, idx_hbm, x_hbm, out_hbm, idx_v, x_v, out_v,
                    lambda: _sc_gather_ring(u_hbm, idx_v, bufs, sems, unit, compute))

    return body(u3, idx_flat, xn2)


def _peer_combine_sc(v3, idx_flat, coef_b):
    t = coef_b.shape[0]
    unit = 32 // SC_ROWS
    units_per_tok = PEER_SEL // 32

    @functools.partial(
        pl.kernel,
        mesh=_sc_mesh(),
        out_type=jax.ShapeDtypeStruct((t, D_MODEL), F32),
        scratch_types=_sc_scratch(PEER_SEL * SC_LANES, D_MODEL),
        compiler_params=pltpu.CompilerParams(needs_layout_passes=False),
        name="peer_combine",
    )
    def body(v_hbm, idx_hbm, coef_hbm, out_hbm, idx_v, coef_v, out_v, *rest):
        bufs, sems = rest[:SC_NBUF], rest[SC_NBUF:]

        def compute(u, ub):
            tl = u // units_per_tok
            rb = (u % units_per_tok) * 32
            for g in range(ROW_TILES):
                cols = ([pl.ds(g * LANES + l * SC_LANES, SC_LANES) for l in range(8)]
                        + [pl.ds(HALF + g * LANES + l * SC_LANES, SC_LANES) for l in range(8)])

                def rbody(r, acc):
                    acc = list(acc)
                    for q, buf in enumerate(ub):
                        co = pl.multiple_of((rb + q * SC_ROWS + r) * SC_LANES, SC_LANES)
                        cf = coef_v[tl, pl.ds(co, SC_LANES)]
                        for l in range(8):
                            lo, hi = _unpack_words(buf[r, g, pl.ds(l * SC_LANES, SC_LANES)])
                            acc[l] = acc[l] + lo * cf
                            acc[8 + l] = acc[8 + l] + hi * cf
                    return tuple(acc)

                acc = lax.fori_loop(0, SC_ROWS, rbody, tuple(out_v[tl, c] for c in cols))
                for k in range(16):
                    out_v[tl, cols[k]] = acc[k]

        def run():
            def zero(tl, c2):
                for k in range(D_MODEL // SC_LANES):
                    out_v[tl, pl.ds(k * SC_LANES, SC_LANES)] = jnp.zeros((SC_LANES,), F32)
                return c2

            lax.fori_loop(0, SC_TOK_BATCH, zero, 0)
            _sc_gather_ring(v_hbm, idx_v, bufs, sems, unit, compute)

        _sc_batches(t // SC_WORKERS, idx_hbm, coef_hbm, out_hbm, idx_v, coef_v, out_v, run)

    return body(v3, idx_flat, coef_b)


def _coef_kernel(act_ref, g_ref, e_ref, coef_ref):
    act = act_ref[...]
    gelu = 0.5 * act * (1.0 + lax.erf(act * np.float32(math.sqrt(0.5))))
    coef_ref[...] = _dot(g_ref[...] * gelu, e_ref[...], precision=lax.Precision.HIGHEST)


def _coef(act, g):
    n = act.shape[0]
    tr = _row_block(n, 512)
    w = PEER_SEL * SC_LANES
    expand = (jnp.arange(w, dtype=jnp.int32)[None, :] // SC_LANES
              == jnp.arange(PEER_SEL, dtype=jnp.int32)[:, None]).astype(F32)
    blk = pl.BlockSpec((tr, PEER_SEL), lambda i: (i, 0))
    return pl.pallas_call(
        _coef_kernel,
        grid=(n // tr,),
        in_specs=[blk, blk, pl.BlockSpec((PEER_SEL, w), lambda i: (0, 0))],
        out_specs=pl.BlockSpec((tr, w), lambda i: (i, 0)),
        out_shape=jax.ShapeDtypeStruct((n, w), F32),
        compiler_params=pltpu.CompilerParams(
            dimension_semantics=("parallel",), vmem_limit_bytes=VMEM_LIMIT),
        name="peer_coef",
    )(act, g, expand)


def _final_kernel(h_ref, p_ref, g_ref, o_ref):
    o_ref[...] = _rms(h_ref[...] + p_ref[...], g_ref[...])


def _final(h2, peer_out, gain):
    n = h2.shape[0]
    tm = _row_block(n, 1024)
    blk = pl.BlockSpec((tm, D_MODEL), lambda i: (i, 0))
    return pl.pallas_call(
        _final_kernel,
        grid=(n // tm,),
        in_specs=[blk, blk, pl.BlockSpec((1, D_MODEL), lambda i: (0, 0))],
        out_specs=blk,
        out_shape=jax.ShapeDtypeStruct((n, D_MODEL), F32),
        compiler_params=pltpu.CompilerParams(
            dimension_semantics=("parallel",), vmem_limit_bytes=VMEM_LIMIT),
        name="final_norm",
    )(h2, peer_out, gain)


def _pad_heads(w, dk):
    lead = w.shape[:-1]
    w = w.reshape(lead + (HEADS, dk))
    w = jnp.pad(w, [(0, 0)] * len(lead) + [(0, 0), (0, LANES - dk)])
    return w.reshape(lead + (HEADS * LANES,))


def _rot_heads(w, dk):
    lead = w.shape[:-1]
    w = w.reshape(lead + (HEADS, 2, dk // 2))
    w = jnp.stack([-w[..., 1, :], w[..., 0, :]], axis=-2)
    return w.reshape(lead + (HEADS * dk,))


def _pack_in_weights(w_in):
    gq, gk, gv, gg, ga, rq, rk, rv, rg = jnp.split(w_in, np.cumsum(IN_SPLITS)[:-1].tolist(), axis=-1)
    ga = jnp.pad(ga, ((0, 0), (0, 2 * LANES - GATE_RANK)))
    return jnp.concatenate(
        [_pad_heads(gq, DK), _pad_heads(gk, DK), gv, gg,
         _pad_heads(rq, DK), _pad_heads(rk, DK), rv, rg,
         _pad_heads(_rot_heads(rq, DK), DK), _pad_heads(_rot_heads(rk, DK), DK), ga],
        axis=-1).astype(BF16)


def _rope_tables(pos):
    inv_freq = ROPE_BASE ** (-jnp.arange(0, DK, 2, dtype=F32) / DK)
    ang = pos[:, None] * inv_freq[None, :]
    full = lambda t: _pad_heads(jnp.tile(t, (1, 2 * HEADS)), DK)
    return full(jnp.cos(ang)), full(jnp.sin(ang))


def _segment_lengths(s, first):
    seg = s // SEGS_PER_SEQ
    lengths = [seg] * SEGS_PER_SEQ
    if first:
        head = [seg // 8, 3 * seg // 8, seg // 2]
        assert sum(head) == seg and all(n % SC_CALL_TOKENS == 0 for n in head)
        lengths = head + lengths[1:]
    return lengths


def kernel(x, meta_tokens, norm1_gain, w_in, gla_gate_w2, gla_gate_b, gla_norm_gain, ret_norm_gain, ret_norm_bias, w_out, norm2_gain, peer_w_q, peer_subkeys, peer_u, peer_v, final_gain):
    b, s, d = x.shape
    assert d == D_MODEL and (s // SEGS_PER_SEQ) % SC_CALL_TOKENS == 0
    assert norm1_gain.shape[0] == 1, "single-layer block"
    x2d = x.reshape(b * s, d)

    w_all = _pack_in_weights(w_in[0])
    w2p = jnp.pad(_pad_heads(gla_gate_w2[0], DK), ((0, LANES - GATE_RANK), (0, 0))).astype(BF16)
    gbp = _pad_heads(gla_gate_b[0][None, :], DK)
    g1 = norm1_gain[0][None, :]
    ggain = gla_norm_gain[0][None, :]
    rgain = ret_norm_gain[0][None, :]
    rbias = ret_norm_bias[0][None, :]

    h0 = jnp.concatenate([jnp.zeros((N_PAD, d), x.dtype), meta_tokens.astype(x.dtype)], axis=0)
    cos0, sin0 = _rope_tables(jnp.arange(CHUNK, dtype=F32) - N_PAD)
    cos1, sin1 = _rope_tables(jnp.arange(s, dtype=F32) + N_META)
    zero_state = jnp.zeros((HEADS, DV, LANES), F32)
    _, s_meta, r_meta = _mixer(_norm_proj(h0, g1, w_all), cos0, sin0, w2p, gbp, ggain, rgain, rbias,
                               zero_state, zero_state, batch=1, n_invalid=N_PAD)

    w_out_b = w_out[0].astype(BF16)
    g2 = norm2_gain[0][None, :]
    w_q = peer_w_q[0].astype(BF16)
    subkeys = peer_subkeys[0].reshape(2 * PEER_HEADS, PEER_NKEYS, PEER_DKEY // 2).astype(BF16)
    u3 = _pack_table(peer_u[0])
    v3 = _pack_table(peer_v[0])
    gf = final_gain[None, :]

    outs = []
    done = []
    for bi in range(b):
        s_state, r_state = s_meta, r_meta
        start = 0
        for seg in _segment_lengths(s, first=bi == 0):
            xs = x2d[bi * s + start:bi * s + start + seg]
            pos = slice(start, start + seg)
            start += seg
            if done:
                ties = (done[-1][0],) + ((done[-2][1],) if len(done) > 1 else ())
                xs, _ = lax.optimization_barrier((xs, ties))
            o, s_state, r_state = _mixer(_norm_proj(xs, g1, w_all), cos1[pos], sin1[pos], w2p, gbp,
                                         ggain, rgain, rbias, s_state, r_state, batch=1, n_invalid=0)
            h2, xn2 = _out_proj(o, xs, w_out_b, g2)
            idx, g = _route(xn2, w_q, subkeys)
            idx_flat = idx.reshape(-1)
            coef_b = _coef(_peer_dot_sc(u3, idx_flat, xn2), g)
            done.append((g, coef_b))
            peer_out = _peer_combine_sc(v3, idx_flat, coef_b)
            outs.append(_final(h2, peer_out, gf))
    return jnp.concatenate(outs, axis=0).reshape(b, s, d)
```

```python
import functools
import math

import jax
import jax.numpy as jnp
import numpy as np
from jax import lax
from jax.experimental import pallas as pl
from jax.experimental.pallas import tpu as pltpu
from jax.experimental.pallas import tpu_sc as plsc

F32 = jnp.float32
BF16 = jnp.bfloat16

D_MODEL = 1024
N_META = 16
CHUNK = 64
N_PAD = CHUNK - N_META
EPS = 1e-6

HEADS = 4
DV = 128
DK = 64
LANES = 128
GATE_RANK = 16
GATE_TAU = 16.0
ROPE_BASE = 10000.0
IN_SPLITS = (HEADS * DK, HEADS * DK, HEADS * DV, HEADS * DV, GATE_RANK,
             HEADS * DK, HEADS * DK, HEADS * DV, HEADS * DV)

PEER_HEADS = 8
PEER_NKEYS = 128
PEER_DKEY = 256
PEER_TOPK = 16
PEER_SEL = PEER_HEADS * PEER_TOPK

HW = HEADS * LANES
OFF_GQ, OFF_GK, OFF_GV, OFF_GG = 0, HW, 2 * HW, 3 * HW
OFF_RQ, OFF_RK, OFF_RV, OFF_RG = 4 * HW, 5 * HW, 6 * HW, 7 * HW
OFF_RQR, OFF_RKR, OFF_GA = 8 * HW, 9 * HW, 10 * HW
PROJ_W = 10 * HW + 2 * LANES
PROJ_TN = PROJ_W // 3

LOG_GAMMA = tuple(math.log(1.0 - 2.0 ** (-5.0 - h)) for h in range(HEADS))

SC_CORES, SC_SUBCORES, SC_LANES = 2, 16, 16
SC_WORKERS = SC_CORES * SC_SUBCORES
SC_TOK_BATCH = 16
SC_ROWS = 16
SC_NBUF = 8
SC_GATHERS = SC_TOK_BATCH * PEER_SEL // SC_ROWS
HALF = D_MODEL // 2
ROW_TILES = HALF // LANES

SC_CALL_TOKENS = SC_WORKERS * SC_TOK_BATCH
SEGS_PER_SEQ = 2
SC_LOOKAHEAD = 2

VMEM_LIMIT = 48 * 1024 * 1024


def _row_block(n, target):
    blk = math.gcd(n, target)
    assert blk % 8 == 0 or blk == n, (n, target)
    return blk


def _dot(a, b, **kw):
    return jnp.dot(a, b, preferred_element_type=F32, **kw)


def _dot_nt(a, b):
    return lax.dot_general(a, b, (((1,), (1,)), ((), ())), preferred_element_type=F32)


def _dot_tn(a, b):
    return lax.dot_general(a, b, (((0,), (0,)), ((), ())), preferred_element_type=F32)


def _rms(x, g):
    return x * lax.rsqrt(jnp.mean(x * x, axis=-1, keepdims=True) + EPS) * g


def _silu(x):
    return x * jax.nn.sigmoid(x)


def _norm_proj_kernel(x_ref, g_ref, w_ref, o_ref):
    xn = _rms(x_ref[...], g_ref[...]).astype(BF16)
    o_ref[...] = _dot(xn, w_ref[...])


def _norm_proj(x2d, gain, w_all):
    n = x2d.shape[0]
    tm = _row_block(n, 512)
    return pl.pallas_call(
        _norm_proj_kernel,
        grid=(n // tm, PROJ_W // PROJ_TN),
        in_specs=[
            pl.BlockSpec((tm, D_MODEL), lambda i, j: (i, 0)),
            pl.BlockSpec((1, D_MODEL), lambda i, j: (0, 0)),
            pl.BlockSpec((D_MODEL, PROJ_TN), lambda i, j: (0, j)),
        ],
        out_specs=pl.BlockSpec((tm, PROJ_TN), lambda i, j: (i, j)),
        out_shape=jax.ShapeDtypeStruct((n, PROJ_W), F32),
        compiler_params=pltpu.CompilerParams(
            dimension_semantics=("parallel", "parallel"), vmem_limit_bytes=VMEM_LIMIT),
        name="norm_in_proj",
    )(x2d, gain, w_all)


def _mixer_kernel(proj_ref, cos_ref, sin_ref, w2_ref, gb_ref, ggain_ref, rgain_ref, rbias_ref,
                  s0_ref, r0_ref, o_ref, s_out_ref, r_out_ref, s_scr, r_scr, *, n_invalid, rows_blk):
    j = pl.program_id(1)

    @pl.when(j == 0)
    def _():
        s_scr[...] = s0_ref[...]
        r_scr[...] = r0_ref[...]

    row = lax.broadcasted_iota(jnp.int32, (CHUNK, CHUNK), 0)
    col = lax.broadcasted_iota(jnp.int32, (CHUNK, CHUNK), 1)
    causal = row >= col
    tril = causal.astype(F32)
    diff = jnp.maximum(row - col, 0).astype(F32)
    rowf = lax.broadcasted_iota(jnp.int32, (CHUNK, LANES), 0).astype(F32)
    row1 = lax.broadcasted_iota(jnp.int32, (CHUNK, 1), 0)

    def chunk_body(c, carry):
        r0 = pl.multiple_of(c * CHUNK, CHUNK)
        rows = pl.ds(r0, CHUNK)
        valid = ((j * rows_blk + r0 + row1) >= n_invalid).astype(F32)

        ga = proj_ref[rows, OFF_GA:OFF_GA + LANES].astype(BF16)
        pre = _dot(ga, w2_ref[...]) + gb_ref[...]
        log_a = jax.nn.log_sigmoid(pre) * (1.0 / GATE_TAU)
        bcum = _dot(tril, log_a, precision=lax.Precision.HIGHEST)
        b_last = bcum[CHUNK - 1:CHUNK, :]
        gq = proj_ref[rows, OFF_GQ:OFF_GQ + HW] * (DK ** -0.5)
        gk = proj_ref[rows, OFF_GK:OFF_GK + HW] * valid
        q_t = (gq * jnp.exp(bcum)).astype(BF16)
        k_t = (gk * jnp.exp(-bcum)).astype(BF16)
        k_end = (gk * jnp.exp(b_last - bcum)).astype(BF16)
        decay = jnp.exp(b_last)
        for h in range(HEADS):
            sl = slice(h * LANES, (h + 1) * LANES)
            v = proj_ref[rows, OFF_GV + h * DV:OFF_GV + (h + 1) * DV].astype(BF16)
            att = jnp.where(causal, _dot_nt(q_t[:, sl], k_t[:, sl]), 0.0).astype(BF16)
            s_prev = s_scr[h]
            o = _dot(att, v) + _dot_nt(q_t[:, sl], s_prev.astype(BF16))
            s_scr[h] = s_prev * decay[:, sl] + _dot_tn(v, k_end[:, sl])
            o = o * lax.rsqrt(jnp.mean(o * o, axis=-1, keepdims=True) + EPS) * ggain_ref[...]
            gate = proj_ref[rows, OFF_GG + h * DV:OFF_GG + (h + 1) * DV]
            o_ref[rows, h * DV:(h + 1) * DV] = (o * _silu(gate)).astype(o_ref.dtype)

        cosb = cos_ref[rows, :]
        sinb = sin_ref[rows, :]
        rq = proj_ref[rows, OFF_RQ:OFF_RQ + HW]
        rqr = proj_ref[rows, OFF_RQR:OFF_RQR + HW]
        rk = proj_ref[rows, OFF_RK:OFF_RK + HW] * valid
        rkr = proj_ref[rows, OFF_RKR:OFF_RKR + HW] * valid
        q_r = (rq * cosb + rqr * sinb).astype(BF16)
        k_r = (rk * cosb + rkr * sinb) * (DK ** -0.5)
        for h in range(HEADS):
            sl = slice(h * LANES, (h + 1) * LANES)
            lg = LOG_GAMMA[h]
            dmat = jnp.where(causal, jnp.exp(lg * diff), 0.0)
            xi = jnp.exp(lg * (rowf + 1.0))
            zeta = jnp.exp(lg * (CHUNK - 1.0 - rowf))
            v = proj_ref[rows, OFF_RV + h * DV:OFF_RV + (h + 1) * DV].astype(BF16)
            k_h = k_r[:, sl]
            att = (_dot_nt(q_r[:, sl], k_h.astype(BF16)) * dmat).astype(BF16)
            r_prev = r_scr[h]
            o = _dot(att, v) + _dot_nt(q_r[:, sl], r_prev.astype(BF16)) * xi
            r_scr[h] = r_prev * math.exp(lg * CHUNK) + _dot_tn(v, (k_h * zeta).astype(BF16))
            mu = jnp.mean(o, axis=-1, keepdims=True)
            var = jnp.mean(jnp.square(o - mu), axis=-1, keepdims=True)
            o = (o - mu) * lax.rsqrt(var + EPS) * rgain_ref[...] + rbias_ref[...]
            gate = proj_ref[rows, OFF_RG + h * DV:OFF_RG + (h + 1) * DV]
            o_ref[rows, HW + h * DV:HW + (h + 1) * DV] = (o * _silu(gate)).astype(o_ref.dtype)
        return carry

    lax.fori_loop(0, rows_blk // CHUNK, chunk_body, 0)
    s_out_ref[...] = s_scr[...]
    r_out_ref[...] = r_scr[...]


def _mixer(proj, cosf, sinf, w2p, gbp, ggain, rgain, rbias, s0, r0, *, batch, n_invalid):
    rows = proj.shape[0] // batch
    rows_blk = _row_block(rows, 256)
    nblk = rows // rows_blk
    const2 = lambda b, j: (0, 0)
    const3 = lambda b, j: (0, 0, 0)
    state = jax.ShapeDtypeStruct((HEADS, DV, LANES), F32)
    return pl.pallas_call(
        functools.partial(_mixer_kernel, n_invalid=n_invalid, rows_blk=rows_blk),
        grid=(batch, nblk),
        in_specs=[
            pl.BlockSpec((rows_blk, PROJ_W), lambda b, j: (b * nblk + j, 0)),
            pl.BlockSpec((rows_blk, HW), lambda b, j: (j, 0)),
            pl.BlockSpec((rows_blk, HW), lambda b, j: (j, 0)),
            pl.BlockSpec((LANES, HW), const2),
            pl.BlockSpec((1, HW), const2),
            pl.BlockSpec((1, DV), const2),
            pl.BlockSpec((1, DV), const2),
            pl.BlockSpec((1, DV), const2),
            pl.BlockSpec((HEADS, DV, LANES), const3),
            pl.BlockSpec((HEADS, DV, LANES), const3),
        ],
        out_specs=[
            pl.BlockSpec((rows_blk, D_MODEL), lambda b, j: (b * nblk + j, 0)),
            pl.BlockSpec((HEADS, DV, LANES), const3),
            pl.BlockSpec((HEADS, DV, LANES), const3),
        ],
        out_shape=[jax.ShapeDtypeStruct((batch * rows, D_MODEL), BF16), state, state],
        scratch_shapes=[pltpu.VMEM((HEADS, DV, LANES), F32), pltpu.VMEM((HEADS, DV, LANES), F32)],
        compiler_params=pltpu.CompilerParams(
            dimension_semantics=("arbitrary", "arbitrary"), vmem_limit_bytes=VMEM_LIMIT),
        name="token_mixers",
    )(proj, cosf, sinf, w2p, gbp, ggain, rgain, rbias, s0, r0)


def _out_proj_kernel(o_ref, x_ref, w_ref, g_ref, h_ref, xn_ref):
    h = x_ref[...] + _dot(o_ref[...], w_ref[...])
    h_ref[...] = h
    xn_ref[...] = _rms(h, g_ref[...])


def _out_proj(o, x2d, w_out, gain2):
    n = x2d.shape[0]
    tm = _row_block(n, 512)
    blk = pl.BlockSpec((tm, D_MODEL), lambda i: (i, 0))
    return pl.pallas_call(
        _out_proj_kernel,
        grid=(n // tm,),
        in_specs=[blk, blk, pl.BlockSpec((D_MODEL, D_MODEL), lambda i: (0, 0)),
                  pl.BlockSpec((1, D_MODEL), lambda i: (0, 0))],
        out_specs=[blk, blk],
        out_shape=[jax.ShapeDtypeStruct((n, D_MODEL), F32)] * 2,
        compiler_params=pltpu.CompilerParams(
            dimension_semantics=("parallel",), vmem_limit_bytes=VMEM_LIMIT),
        name="out_proj_norm",
    )(o, x2d, w_out, gain2)


def _top_k_rows(arrays, payloads=None):
    arrays = list(arrays)
    n = arrays[0].shape[0]
    rowi = lax.broadcasted_iota(jnp.int32, arrays[0].shape, 0)
    vals = [[] for _ in arrays]
    picks = [[] for _ in arrays]
    for _ in range(PEER_TOPK):
        for k, s in enumerate(arrays):
            m = jnp.max(s, axis=0, keepdims=True)
            am = jnp.min(jnp.where(s == m, rowi, n), axis=0, keepdims=True)
            sel = rowi == am
            vals[k].append(m)
            if payloads is None:
                picks[k].append(am)
            else:
                picks[k].append(jnp.min(jnp.where(sel, payloads[k], jnp.iinfo(jnp.int32).max),
                                        axis=0, keepdims=True))
            arrays[k] = jnp.where(sel, -jnp.inf, s)
    return [(jnp.concatenate(v, axis=0), jnp.concatenate(p, axis=0)) for v, p in zip(vals, picks)]


def _candidate_cells():
    cells = [(a, b) for a in range(PEER_TOPK) for b in range(PEER_TOPK) if (a + 1) * (b + 1) <= PEER_TOPK]
    n = -(-len(cells) // 8) * 8
    sel_a = np.zeros((n, PEER_TOPK), np.float32)
    sel_b = np.zeros((n, PEER_TOPK), np.float32)
    pad = np.full((n, 1), -np.inf, np.float32)
    for r, (a, b) in enumerate(cells):
        sel_a[r, a] = 1.0
        sel_b[r, b] = 1.0
        pad[r, 0] = 0.0
    return sel_a, sel_b, pad


def _route_kernel(xn_ref, wq_ref, sk_ref, sa_ref, sb_ref, pad_ref, idx_ref, g_ref, q_scr, i_scr, g_scr):
    tr = xn_ref.shape[0]
    q = _dot(xn_ref[...].astype(BF16), wq_ref[...])
    for hc in range(2 * PEER_HEADS):
        q_scr[hc] = q[:, hc * LANES:(hc + 1) * LANES].astype(BF16)

    def sub_body(sb, carry):
        rows = pl.ds(pl.multiple_of(sb * LANES, LANES), LANES)

        def head_pair(hp, c2):
            heads = [2 * hp, 2 * hp + 1]
            scores = [_dot_nt(sk_ref[2 * h + c], q_scr[2 * h + c, rows, :])
                      for h in heads for c in range(2)]
            first = _top_k_rows(scores)
            sa, sb = sa_ref[...], sb_ref[...]
            cands, cidxs = [], []
            for k in range(len(heads)):
                (v0, i0), (v1, i1) = first[2 * k], first[2 * k + 1]
                cands.append((_dot(sa, v0, precision=lax.Precision.HIGHEST)
                              + _dot(sb, v1, precision=lax.Precision.HIGHEST)) + pad_ref[...])
                cidxs.append((_dot(sa, i0.astype(F32)) * PEER_NKEYS
                              + _dot(sb, i1.astype(F32))).astype(jnp.int32))
            for h, (best, e) in zip(heads, _top_k_rows(cands, cidxs)):
                ex = jnp.exp(best - best[0:1])
                i_scr[h] = e
                g_scr[h] = ex / jnp.sum(ex, axis=0, keepdims=True)
            return c2

        lax.fori_loop(0, PEER_HEADS // 2, head_pair, 0)
        idx_ref[rows, :] = i_scr[...].reshape(PEER_SEL, LANES).T
        g_ref[rows, :] = g_scr[...].reshape(PEER_SEL, LANES).T
        return carry

    lax.fori_loop(0, tr // LANES, sub_body, 0)


def _route(xn2, w_q, subkeys):
    n = xn2.shape[0]
    sel_a, sel_b, pad = _candidate_cells()
    tr = _row_block(n, 512)
    qw = 2 * PEER_HEADS * LANES
    return pl.pallas_call(
        _route_kernel,
        grid=(n // tr,),
        in_specs=[pl.BlockSpec((tr, D_MODEL), lambda i: (i, 0)),
                  pl.BlockSpec((D_MODEL, qw), lambda i: (0, 0)),
                  pl.BlockSpec((2 * PEER_HEADS, PEER_NKEYS, LANES), lambda i: (0, 0, 0)),
                  pl.BlockSpec(sel_a.shape, lambda i: (0, 0)),
                  pl.BlockSpec(sel_b.shape, lambda i: (0, 0)),
                  pl.BlockSpec(pad.shape, lambda i: (0, 0))],
        out_specs=[pl.BlockSpec((tr, PEER_SEL), lambda i: (i, 0))] * 2,
        out_shape=[jax.ShapeDtypeStruct((n, PEER_SEL), jnp.int32),
                   jax.ShapeDtypeStruct((n, PEER_SEL), F32)],
        scratch_shapes=[pltpu.VMEM((2 * PEER_HEADS, tr, LANES), BF16),
                        pltpu.VMEM((PEER_HEADS, PEER_TOPK, LANES), jnp.int32),
                        pltpu.VMEM((PEER_HEADS, PEER_TOPK, LANES), F32)],
        compiler_params=pltpu.CompilerParams(
            dimension_semantics=("parallel",), vmem_limit_bytes=VMEM_LIMIT),
        name="peer_route",
    )(xn2, w_q, subkeys, sel_a, sel_b, pad)


def _sc_worker_id():
    return lax.axis_index("s") * SC_CORES + lax.axis_index("c")


def _sc_mesh():
    return plsc.VectorSubcoreMesh(core_axis_name="c", subcore_axis_name="s")


def _sc_gather_ring(table_hbm, idx_v, bufs, sems, unit, compute):
    def gather(g, k):
        off = pl.multiple_of(g * SC_ROWS, SC_ROWS)
        return pltpu.make_async_copy(table_hbm.at[idx_v.at[pl.ds(off, SC_ROWS)]], bufs[k], sems[k])

    for k in range(SC_NBUF):
        gather(k, k).start()

    units_per_iter = SC_NBUF // unit

    def ring_iter(i, carry):
        for m in range(units_per_iter):
            ks = range(m * unit, (m + 1) * unit)
            for k in ks:
                gather(i * SC_NBUF + k, k).wait()
            compute(i * units_per_iter + m, [bufs[k] for k in ks])
            for k in ks:
                g_next = (i + 1) * SC_NBUF + k

                @pl.when(g_next < SC_GATHERS)
                def _():
                    gather(g_next, k).start()
        return carry

    lax.fori_loop(0, SC_GATHERS // SC_NBUF, ring_iter, 0)


def _pack_table(t):
    lo = lax.bitcast_convert_type(t[:, :HALF].astype(jnp.bfloat16), jnp.uint16).astype(jnp.int32)
    hi_bits = lax.bitcast_convert_type(t[:, HALF:], jnp.int32)
    sign = hi_bits & jnp.int32(-2 ** 31)
    mag = jnp.minimum(hi_bits & jnp.int32(2 ** 31 - 1), jnp.int32(0x7F000000))
    upper = jnp.maximum(mag + (1 << 15) - lo, 0) >> 16
    words = sign | (upper << 16) | lo
    return lax.bitcast_convert_type(words, jnp.uint32).reshape(t.shape[0], ROW_TILES, LANES)


def _unpack_words(w):
    lo = lax.bitcast_convert_type(w << 16, F32)
    hi = lax.bitcast_convert_type(w, F32)
    return lo, hi


def _sc_scratch(staged_cols, out_cols):
    return ([pltpu.VMEM((SC_TOK_BATCH * PEER_SEL,), jnp.int32),
             pltpu.VMEM((SC_TOK_BATCH, staged_cols), F32),
             pltpu.VMEM((SC_TOK_BATCH, out_cols), F32)]
            + [pltpu.VMEM((SC_ROWS, ROW_TILES, LANES), jnp.uint32) for _ in range(SC_NBUF)]
            + [pltpu.SemaphoreType.DMA for _ in range(SC_NBUF)])


def _sc_batches(tpw, idx_hbm, staged_hbm, out_hbm, idx_v, staged_v, out_v, run):
    tok0 = _sc_worker_id() * tpw

    def batch(tb, carry):
        t0 = pl.multiple_of(tok0 + tb * SC_TOK_BATCH, SC_TOK_BATCH)
        pltpu.sync_copy(idx_hbm.at[pl.ds(pl.multiple_of(t0 * PEER_SEL, SC_TOK_BATCH * PEER_SEL),
                                         SC_TOK_BATCH * PEER_SEL)], idx_v)
        pltpu.sync_copy(staged_hbm.at[pl.ds(t0, SC_TOK_BATCH)], staged_v)
        run()
        pltpu.sync_copy(out_v, out_hbm.at[pl.ds(t0, SC_TOK_BATCH)])
        return carry

    lax.fori_loop(0, tpw // SC_TOK_BATCH, batch, 0)


def _peer_dot_sc(u3, idx_flat, xn2):
    t = xn2.shape[0]
    unit = 16 // SC_ROWS
    units_per_tok = PEER_SEL // 16

    @functools.partial(
        pl.kernel,
        mesh=_sc_mesh(),
        out_type=jax.ShapeDtypeStruct((t, PEER_SEL), F32),
        scratch_types=[pltpu.VMEM((16 * SC_LANES,), F32)] + _sc_scratch(D_MODEL, PEER_SEL),
        compiler_params=pltpu.CompilerParams(needs_layout_passes=False),
        name="peer_dot",
    )
    def body(u_hbm, idx_hbm, x_hbm, out_hbm, red_v, idx_v, x_v, out_v, *rest):
        bufs, sems = rest[:SC_NBUF], rest[SC_NBUF:]

        def compute(u, ub):
            tl = u // units_per_tok
            rb = (u % units_per_tok) * 16

            def cbody(c, acc):
                acc = list(acc)
                c4 = c // (LANES // SC_LANES)
                lane0 = pl.multiple_of((c % (LANES // SC_LANES)) * SC_LANES, SC_LANES)
                col = pl.multiple_of(c * SC_LANES, SC_LANES)
                x_lo = x_v[tl, pl.ds(col, SC_LANES)]
                x_hi = x_v[tl, pl.ds(HALF + col, SC_LANES)]
                for q, buf in enumerate(ub):
                    for r in range(SC_ROWS):
                        a = q * SC_ROWS + r
                        lo, hi = _unpack_words(buf[r, c4, pl.ds(lane0, SC_LANES)])
                        acc[a] = acc[a] + lo * x_lo + hi * x_hi
                return tuple(acc)

            acc = lax.fori_loop(0, HALF // SC_LANES, cbody,
                                tuple(jnp.zeros((SC_LANES,), F32) for _ in range(16)))
            for a in range(16):
                red_v[pl.ds(a * SC_LANES, SC_LANES)] = acc[a]
            lane_base = lax.iota(jnp.int32, SC_LANES) * SC_LANES
            tot = plsc.load_gather(red_v, [lane_base])
            for k in range(1, SC_LANES):
                tot = tot + plsc.load_gather(red_v, [lane_base + k])
            out_v[tl, pl.ds(pl.multiple_of(rb, 16), 16)] = tot

        _sc_batches(t // SC_WORKERS, idx_hbm, x_hbm, out_hbm, idx_v, x_v, out_v,
                    lambda: _sc_gather_ring(u_hbm, idx_v, bufs, sems, unit, compute))

    return body(u3, idx_flat, xn2)


def _peer_combine_sc(v3, idx_flat, coef_b):
    t = coef_b.shape[0]
    unit = 32 // SC_ROWS
    units_per_tok = PEER_SEL // 32

    @functools.partial(
        pl.kernel,
        mesh=_sc_mesh(),
        out_type=jax.ShapeDtypeStruct((t, D_MODEL), F32),
        scratch_types=_sc_scratch(PEER_SEL * SC_LANES, D_MODEL),
        compiler_params=pltpu.CompilerParams(needs_layout_passes=False),
        name="peer_combine",
    )
    def body(v_hbm, idx_hbm, coef_hbm, out_hbm, idx_v, coef_v, out_v, *rest):
        bufs, sems = rest[:SC_NBUF], rest[SC_NBUF:]

        def compute(u, ub):
            tl = u // units_per_tok
            rb = (u % units_per_tok) * 32
            for g in range(ROW_TILES):
                cols = ([pl.ds(g * LANES + l * SC_LANES, SC_LANES) for l in range(8)]
                        + [pl.ds(HALF + g * LANES + l * SC_LANES, SC_LANES) for l in range(8)])

                def rbody(r, acc):
                    acc = list(acc)
                    for q, buf in enumerate(ub):
                        co = pl.multiple_of((rb + q * SC_ROWS + r) * SC_LANES, SC_LANES)
                        cf = coef_v[tl, pl.ds(co, SC_LANES)]
                        for l in range(8):
                            lo, hi = _unpack_words(buf[r, g, pl.ds(l * SC_LANES, SC_LANES)])
                            acc[l] = acc[l] + lo * cf
                            acc[8 + l] = acc[8 + l] + hi * cf
                    return tuple(acc)

                acc = lax.fori_loop(0, SC_ROWS, rbody, tuple(out_v[tl, c] for c in cols))
                for k in range(16):
                    out_v[tl, cols[k]] = acc[k]

        def run():
            def zero(tl, c2):
                for k in range(D_MODEL // SC_LANES):
                    out_v[tl, pl.ds(k * SC_LANES, SC_LANES)] = jnp.zeros((SC_LANES,), F32)
                return c2

            lax.fori_loop(0, SC_TOK_BATCH, zero, 0)
            _sc_gather_ring(v_hbm, idx_v, bufs, sems, unit, compute)

        _sc_batches(t // SC_WORKERS, idx_hbm, coef_hbm, out_hbm, idx_v, coef_v, out_v, run)

    return body(v3, idx_flat, coef_b)


def _coef_kernel(act_ref, g_ref, e_ref, coef_ref):
    act = act_ref[...]
    gelu = 0.5 * act * (1.0 + lax.erf(act * np.float32(math.sqrt(0.5))))
    coef_ref[...] = _dot(g_ref[...] * gelu, e_ref[...], precision=lax.Precision.HIGHEST)


def _coef(act, g):
    n = act.shape[0]
    tr = _row_block(n, 512)
    w = PEER_SEL * SC_LANES
    expand = (jnp.arange(w, dtype=jnp.int32)[None, :] // SC_LANES
              == jnp.arange(PEER_SEL, dtype=jnp.int32)[:, None]).astype(F32)
    blk = pl.BlockSpec((tr, PEER_SEL), lambda i: (i, 0))
    return pl.pallas_call(
        _coef_kernel,
        grid=(n // tr,),
        in_specs=[blk, blk, pl.BlockSpec((PEER_SEL, w), lambda i: (0, 0))],
        out_specs=pl.BlockSpec((tr, w), lambda i: (i, 0)),
        out_shape=jax.ShapeDtypeStruct((n, w), F32),
        compiler_params=pltpu.CompilerParams(
            dimension_semantics=("parallel",), vmem_limit_bytes=VMEM_LIMIT),
        name="peer_coef",
    )(act, g, expand)


def _final_kernel(h_ref, p_ref, g_ref, o_ref):
    o_ref[...] = _rms(h_ref[...] + p_ref[...], g_ref[...])


def _final(h2, peer_out, gain):
    n = h2.shape[0]
    tm = _row_block(n, 1024)
    blk = pl.BlockSpec((tm, D_MODEL), lambda i: (i, 0))
    return pl.pallas_call(
        _final_kernel,
        grid=(n // tm,),
        in_specs=[blk, blk, pl.BlockSpec((1, D_MODEL), lambda i: (0, 0))],
        out_specs=blk,
        out_shape=jax.ShapeDtypeStruct((n, D_MODEL), F32),
        compiler_params=pltpu.CompilerParams(
            dimension_semantics=("parallel",), vmem_limit_bytes=VMEM_LIMIT),
        name="final_norm",
    )(h2, peer_out, gain)


def _pad_heads(w, dk):
    lead = w.shape[:-1]
    w = w.reshape(lead + (HEADS, dk))
    w = jnp.pad(w, [(0, 0)] * len(lead) + [(0, 0), (0, LANES - dk)])
    return w.reshape(lead + (HEADS * LANES,))


def _rot_heads(w, dk):
    lead = w.shape[:-1]
    w = w.reshape(lead + (HEADS, 2, dk // 2))
    w = jnp.stack([-w[..., 1, :], w[..., 0, :]], axis=-2)
    return w.reshape(lead + (HEADS * dk,))


def _pack_in_weights(w_in):
    gq, gk, gv, gg, ga, rq, rk, rv, rg = jnp.split(w_in, np.cumsum(IN_SPLITS)[:-1].tolist(), axis=-1)
    ga = jnp.pad(ga, ((0, 0), (0, 2 * LANES - GATE_RANK)))
    return jnp.concatenate(
        [_pad_heads(gq, DK), _pad_heads(gk, DK), gv, gg,
         _pad_heads(rq, DK), _pad_heads(rk, DK), rv, rg,
         _pad_heads(_rot_heads(rq, DK), DK), _pad_heads(_rot_heads(rk, DK), DK), ga],
        axis=-1).astype(BF16)


def _rope_tables(pos):
    inv_freq = ROPE_BASE ** (-jnp.arange(0, DK, 2, dtype=F32) / DK)
    ang = pos[:, None] * inv_freq[None, :]
    full = lambda t: _pad_heads(jnp.tile(t, (1, 2 * HEADS)), DK)
    return full(jnp.cos(ang)), full(jnp.sin(ang))


def _segment_lengths(s, first):
    seg = s // SEGS_PER_SEQ
    lengths = [seg] * SEGS_PER_SEQ
    if first:
        head = [seg // 8, 3 * seg // 8, seg // 2]
        assert sum(head) == seg and all(n % SC_CALL_TOKENS == 0 for n in head)
        lengths = head + lengths[1:]
    return lengths


def kernel(x, meta_tokens, norm1_gain, w_in, gla_gate_w2, gla_gate_b, gla_norm_gain, ret_norm_gain, ret_norm_bias, w_out, norm2_gain, peer_w_q, peer_subkeys, peer_u, peer_v, final_gain):
    b, s, d = x.shape
    assert d == D_MODEL and (s // SEGS_PER_SEQ) % SC_CALL_TOKENS == 0
    assert norm1_gain.shape[0] == 1, "single-layer block"
    x2d = x.reshape(b * s, d)

    w_all = _pack_in_weights(w_in[0])
    w2p = jnp.pad(_pad_heads(gla_gate_w2[0], DK), ((0, LANES - GATE_RANK), (0, 0))).astype(BF16)
    gbp = _pad_heads(gla_gate_b[0][None, :], DK)
    g1 = norm1_gain[0][None, :]
    ggain = gla_norm_gain[0][None, :]
    rgain = ret_norm_gain[0][None, :]
    rbias = ret_norm_bias[0][None, :]

    h0 = jnp.concatenate([jnp.zeros((N_PAD, d), x.dtype), meta_tokens.astype(x.dtype)], axis=0)
    cos0, sin0 = _rope_tables(jnp.arange(CHUNK, dtype=F32) - N_PAD)
    cos1, sin1 = _rope_tables(jnp.arange(s, dtype=F32) + N_META)
    zero_state = jnp.zeros((HEADS, DV, LANES), F32)
    _, s_meta, r_meta = _mixer(_norm_proj(h0, g1, w_all), cos0, sin0, w2p, gbp, ggain, rgain, rbias,
                               zero_state, zero_state, batch=1, n_invalid=N_PAD)

    w_out_b = w_out[0].astype(BF16)
    g2 = norm2_gain[0][None, :]
    w_q = peer_w_q[0].astype(BF16)
    subkeys = peer_subkeys[0].reshape(2 * PEER_HEADS, PEER_NKEYS, PEER_DKEY // 2).astype(BF16)
    u3 = _pack_table(peer_u[0])
    v3 = _pack_table(peer_v[0])
    gf = final_gain[None, :]

    outs = []
    prev_g = None
    pending = []
    for bi in range(b):
        s_state, r_state = s_meta, r_meta
        start = 0
        for seg in _segment_lengths(s, first=bi == 0):
            xs = x2d[bi * s + start:bi * s + start + seg]
            pos = slice(start, start + seg)
            start += seg
            if prev_g is not None:
                xs, _ = lax.optimization_barrier((xs, prev_g))
            o, s_state, r_state = _mixer(_norm_proj(xs, g1, w_all), cos1[pos], sin1[pos], w2p, gbp,
                                         ggain, rgain, rbias, s_state, r_state, batch=1, n_invalid=0)
            h2, xn2 = _out_proj(o, xs, w_out_b, g2)
            idx, g = _route(xn2, w_q, subkeys)
            prev_g = g
            idx_flat = idx.reshape(-1)
            if len(pending) >= SC_LOOKAHEAD:
                h2_old, peer_old = pending.pop(0)
                idx_flat, peer_old = lax.optimization_barrier((idx_flat, peer_old))
                outs.append(_final(h2_old, peer_old, gf))
            act = _peer_dot_sc(u3, idx_flat, xn2)
            pending.append((h2, _peer_combine_sc(v3, idx_flat, _coef(act, g))))
    outs.extend(_final(h2_old, peer_old, gf) for h2_old, peer_old in pending)
    return jnp.concatenate(outs, axis=0).reshape(b, s, d)
```

```python
import functools
import math

import jax
import jax.numpy as jnp
import numpy as np
from jax import lax
from jax.experimental import pallas as pl
from jax.experimental.pallas import tpu as pltpu
from jax.experimental.pallas import tpu_sc as plsc

F32 = jnp.float32
BF16 = jnp.bfloat16

D_MODEL = 1024
N_META = 16
CHUNK = 64
N_PAD = CHUNK - N_META
EPS = 1e-6

HEADS = 4
DV = 128
DK = 64
LANES = 128
GATE_RANK = 16
GATE_TAU = 16.0
ROPE_BASE = 10000.0
IN_SPLITS = (HEADS * DK, HEADS * DK, HEADS * DV, HEADS * DV, GATE_RANK,
             HEADS * DK, HEADS * DK, HEADS * DV, HEADS * DV)

PEER_HEADS = 8
PEER_NKEYS = 128
PEER_DKEY = 256
PEER_TOPK = 16
PEER_SEL = PEER_HEADS * PEER_TOPK

HW = HEADS * LANES
OFF_GQ, OFF_GK, OFF_GV, OFF_GG = 0, HW, 2 * HW, 3 * HW
OFF_RQ, OFF_RK, OFF_RV, OFF_RG = 4 * HW, 5 * HW, 6 * HW, 7 * HW
OFF_RQR, OFF_RKR, OFF_GA = 8 * HW, 9 * HW, 10 * HW
PROJ_W = 10 * HW + 2 * LANES
PROJ_TN = PROJ_W // 3

LOG_GAMMA = tuple(math.log(1.0 - 2.0 ** (-5.0 - h)) for h in range(HEADS))

SC_CORES, SC_SUBCORES, SC_LANES = 2, 16, 16
SC_WORKERS = SC_CORES * SC_SUBCORES
SC_TOK_BATCH = 16
SC_ROWS = 16
SC_NBUF = 8
SC_GATHERS = SC_TOK_BATCH * PEER_SEL // SC_ROWS
HALF = D_MODEL // 2
ROW_TILES = HALF // LANES

SC_CALL_TOKENS = SC_WORKERS * SC_TOK_BATCH
SEGS_PER_SEQ = 2

VMEM_LIMIT = 48 * 1024 * 1024


def _row_block(n, target):
    blk = math.gcd(n, target)
    assert blk % 8 == 0 or blk == n, (n, target)
    return blk


def _dot(a, b, **kw):
    return jnp.dot(a, b, preferred_element_type=F32, **kw)


def _dot_nt(a, b):
    return lax.dot_general(a, b, (((1,), (1,)), ((), ())), preferred_element_type=F32)


def _dot_tn(a, b):
    return lax.dot_general(a, b, (((0,), (0,)), ((), ())), preferred_element_type=F32)


def _rms(x, g):
    return x * lax.rsqrt(jnp.mean(x * x, axis=-1, keepdims=True) + EPS) * g


def _silu(x):
    return x * jax.nn.sigmoid(x)


def _norm_proj_kernel(x_ref, g_ref, w_ref, o_ref):
    xn = _rms(x_ref[...], g_ref[...]).astype(BF16)
    o_ref[...] = _dot(xn, w_ref[...])


def _norm_proj(x2d, gain, w_all):
    n = x2d.shape[0]
    tm = _row_block(n, 512)
    return pl.pallas_call(
        _norm_proj_kernel,
        grid=(n // tm, PROJ_W // PROJ_TN),
        in_specs=[
            pl.BlockSpec((tm, D_MODEL), lambda i, j: (i, 0)),
            pl.BlockSpec((1, D_MODEL), lambda i, j: (0, 0)),
            pl.BlockSpec((D_MODEL, PROJ_TN), lambda i, j: (0, j)),
        ],
        out_specs=pl.BlockSpec((tm, PROJ_TN), lambda i, j: (i, j)),
        out_shape=jax.ShapeDtypeStruct((n, PROJ_W), F32),
        compiler_params=pltpu.CompilerParams(
            dimension_semantics=("parallel", "parallel"), vmem_limit_bytes=VMEM_LIMIT),
        name="norm_in_proj",
    )(x2d, gain, w_all)


def _mixer_kernel(proj_ref, cos_ref, sin_ref, w2_ref, gb_ref, ggain_ref, rgain_ref, rbias_ref,
                  s0_ref, r0_ref, o_ref, s_out_ref, r_out_ref, s_scr, r_scr, *, n_invalid, rows_blk):
    j = pl.program_id(1)

    @pl.when(j == 0)
    def _():
        s_scr[...] = s0_ref[...]
        r_scr[...] = r0_ref[...]

    row = lax.broadcasted_iota(jnp.int32, (CHUNK, CHUNK), 0)
    col = lax.broadcasted_iota(jnp.int32, (CHUNK, CHUNK), 1)
    causal = row >= col
    tril = causal.astype(F32)
    diff = jnp.maximum(row - col, 0).astype(F32)
    rowf = lax.broadcasted_iota(jnp.int32, (CHUNK, LANES), 0).astype(F32)
    row1 = lax.broadcasted_iota(jnp.int32, (CHUNK, 1), 0)

    def chunk_body(c, carry):
        r0 = pl.multiple_of(c * CHUNK, CHUNK)
        rows = pl.ds(r0, CHUNK)
        valid = ((j * rows_blk + r0 + row1) >= n_invalid).astype(F32)

        ga = proj_ref[rows, OFF_GA:OFF_GA + LANES].astype(BF16)
        pre = _dot(ga, w2_ref[...]) + gb_ref[...]
        log_a = jax.nn.log_sigmoid(pre) * (1.0 / GATE_TAU)
        bcum = _dot(tril, log_a, precision=lax.Precision.HIGHEST)
        b_last = bcum[CHUNK - 1:CHUNK, :]
        gq = proj_ref[rows, OFF_GQ:OFF_GQ + HW] * (DK ** -0.5)
        gk = proj_ref[rows, OFF_GK:OFF_GK + HW] * valid
        q_t = (gq * jnp.exp(bcum)).astype(BF16)
        k_t = (gk * jnp.exp(-bcum)).astype(BF16)
        k_end = (gk * jnp.exp(b_last - bcum)).astype(BF16)
        decay = jnp.exp(b_last)
        for h in range(HEADS):
            sl = slice(h * LANES, (h + 1) * LANES)
            v = proj_ref[rows, OFF_GV + h * DV:OFF_GV + (h + 1) * DV].astype(BF16)
            att = jnp.where(causal, _dot_nt(q_t[:, sl], k_t[:, sl]), 0.0).astype(BF16)
            s_prev = s_scr[h]
            o = _dot(att, v) + _dot_nt(q_t[:, sl], s_prev.astype(BF16))
            s_scr[h] = s_prev * decay[:, sl] + _dot_tn(v, k_end[:, sl])
            o = o * lax.rsqrt(jnp.mean(o * o, axis=-1, keepdims=True) + EPS) * ggain_ref[...]
            gate = proj_ref[rows, OFF_GG + h * DV:OFF_GG + (h + 1) * DV]
            o_ref[rows, h * DV:(h + 1) * DV] = (o * _silu(gate)).astype(o_ref.dtype)

        cosb = cos_ref[rows, :]
        sinb = sin_ref[rows, :]
        rq = proj_ref[rows, OFF_RQ:OFF_RQ + HW]
        rqr = proj_ref[rows, OFF_RQR:OFF_RQR + HW]
        rk = proj_ref[rows, OFF_RK:OFF_RK + HW] * valid
        rkr = proj_ref[rows, OFF_RKR:OFF_RKR + HW] * valid
        q_r = (rq * cosb + rqr * sinb).astype(BF16)
        k_r = (rk * cosb + rkr * sinb) * (DK ** -0.5)
        for h in range(HEADS):
            sl = slice(h * LANES, (h + 1) * LANES)
            lg = LOG_GAMMA[h]
            dmat = jnp.where(causal, jnp.exp(lg * diff), 0.0)
            xi = jnp.exp(lg * (rowf + 1.0))
            zeta = jnp.exp(lg * (CHUNK - 1.0 - rowf))
            v = proj_ref[rows, OFF_RV + h * DV:OFF_RV + (h + 1) * DV].astype(BF16)
            k_h = k_r[:, sl]
            att = (_dot_nt(q_r[:, sl], k_h.astype(BF16)) * dmat).astype(BF16)
            r_prev = r_scr[h]
            o = _dot(att, v) + _dot_nt(q_r[:, sl], r_prev.astype(BF16)) * xi
            r_scr[h] = r_prev * math.exp(lg * CHUNK) + _dot_tn(v, (k_h * zeta).astype(BF16))
            mu = jnp.mean(o, axis=-1, keepdims=True)
            var = jnp.mean(jnp.square(o - mu), axis=-1, keepdims=True)
            o = (o - mu) * lax.rsqrt(var + EPS) * rgain_ref[...] + rbias_ref[...]
            gate = proj_ref[rows, OFF_RG + h * DV:OFF_RG + (h + 1) * DV]
            o_ref[rows, HW + h * DV:HW + (h + 1) * DV] = (o * _silu(gate)).astype(o_ref.dtype)
        return carry

    lax.fori_loop(0, rows_blk // CHUNK, chunk_body, 0)
    s_out_ref[...] = s_scr[...]
    r_out_ref[...] = r_scr[...]


def _mixer(proj, cosf, sinf, w2p, gbp, ggain, rgain, rbias, s0, r0, *, batch, n_invalid):
    rows = proj.shape[0] // batch
    rows_blk = _row_block(rows, 256)
    nblk = rows // rows_blk
    const2 = lambda b, j: (0, 0)
    const3 = lambda b, j: (0, 0, 0)
    state = jax.ShapeDtypeStruct((HEADS, DV, LANES), F32)
    return pl.pallas_call(
        functools.partial(_mixer_kernel, n_invalid=n_invalid, rows_blk=rows_blk),
        grid=(batch, nblk),
        in_specs=[
            pl.BlockSpec((rows_blk, PROJ_W), lambda b, j: (b * nblk + j, 0)),
            pl.BlockSpec((rows_blk, HW), lambda b, j: (j, 0)),
            pl.BlockSpec((rows_blk, HW), lambda b, j: (j, 0)),
            pl.BlockSpec((LANES, HW), const2),
            pl.BlockSpec((1, HW), const2),
            pl.BlockSpec((1, DV), const2),
            pl.BlockSpec((1, DV), const2),
            pl.BlockSpec((1, DV), const2),
            pl.BlockSpec((HEADS, DV, LANES), const3),
            pl.BlockSpec((HEADS, DV, LANES), const3),
        ],
        out_specs=[
            pl.BlockSpec((rows_blk, D_MODEL), lambda b, j: (b * nblk + j, 0)),
            pl.BlockSpec((HEADS, DV, LANES), const3),
            pl.BlockSpec((HEADS, DV, LANES), const3),
        ],
        out_shape=[jax.ShapeDtypeStruct((batch * rows, D_MODEL), BF16), state, state],
        scratch_shapes=[pltpu.VMEM((HEADS, DV, LANES), F32), pltpu.VMEM((HEADS, DV, LANES), F32)],
        compiler_params=pltpu.CompilerParams(
            dimension_semantics=("arbitrary", "arbitrary"), vmem_limit_bytes=VMEM_LIMIT),
        name="token_mixers",
    )(proj, cosf, sinf, w2p, gbp, ggain, rgain, rbias, s0, r0)


def _out_proj_kernel(o_ref, x_ref, w_ref, g_ref, h_ref, xn_ref, xp_ref):
    h = x_ref[...] + _dot(o_ref[...], w_ref[...])
    h_ref[...] = h
    xn = _rms(h, g_ref[...])
    xn_ref[...] = xn
    xp_ref[...] = _pack_pairs(xn[:, :HALF], xn[:, HALF:])


def _out_proj(o, x2d, w_out, gain2):
    n = x2d.shape[0]
    tm = _row_block(n, 512)
    blk = pl.BlockSpec((tm, D_MODEL), lambda i: (i, 0))
    return pl.pallas_call(
        _out_proj_kernel,
        grid=(n // tm,),
        in_specs=[blk, blk, pl.BlockSpec((D_MODEL, D_MODEL), lambda i: (0, 0)),
                  pl.BlockSpec((1, D_MODEL), lambda i: (0, 0))],
        out_specs=[blk, blk, pl.BlockSpec((tm, HALF), lambda i: (i, 0))],
        out_shape=[jax.ShapeDtypeStruct((n, D_MODEL), F32)] * 2 + [jax.ShapeDtypeStruct((n, HALF), jnp.uint32)],
        compiler_params=pltpu.CompilerParams(
            dimension_semantics=("parallel",), vmem_limit_bytes=VMEM_LIMIT),
        name="out_proj_norm",
    )(o, x2d, w_out, gain2)


def _top_k_rows(arrays, payloads=None):
    arrays = list(arrays)
    n = arrays[0].shape[0]
    rowi = lax.broadcasted_iota(jnp.int32, arrays[0].shape, 0)
    vals = [[] for _ in arrays]
    picks = [[] for _ in arrays]
    for _ in range(PEER_TOPK):
        for k, s in enumerate(arrays):
            m = jnp.max(s, axis=0, keepdims=True)
            am = jnp.min(jnp.where(s == m, rowi, n), axis=0, keepdims=True)
            sel = rowi == am
            vals[k].append(m)
            if payloads is None:
                picks[k].append(am)
            else:
                picks[k].append(jnp.min(jnp.where(sel, payloads[k], jnp.iinfo(jnp.int32).max),
                                        axis=0, keepdims=True))
            arrays[k] = jnp.where(sel, -jnp.inf, s)
    return [(jnp.concatenate(v, axis=0), jnp.concatenate(p, axis=0)) for v, p in zip(vals, picks)]


def _candidate_cells():
    cells = [(a, b) for a in range(PEER_TOPK) for b in range(PEER_TOPK) if (a + 1) * (b + 1) <= PEER_TOPK]
    n = -(-len(cells) // 8) * 8
    sel_a = np.zeros((n, PEER_TOPK), np.float32)
    sel_b = np.zeros((n, PEER_TOPK), np.float32)
    pad = np.full((n, 1), -np.inf, np.float32)
    for r, (a, b) in enumerate(cells):
        sel_a[r, a] = 1.0
        sel_b[r, b] = 1.0
        pad[r, 0] = 0.0
    return sel_a, sel_b, pad


def _route_kernel(xn_ref, wq_ref, sk_ref, sa_ref, sb_ref, pad_ref, idx_ref, g_ref, q_scr, i_scr, g_scr):
    tr = xn_ref.shape[0]
    q = _dot(xn_ref[...].astype(BF16), wq_ref[...])
    for hc in range(2 * PEER_HEADS):
        q_scr[hc] = q[:, hc * LANES:(hc + 1) * LANES].astype(BF16)

    def sub_body(sb, carry):
        rows = pl.ds(pl.multiple_of(sb * LANES, LANES), LANES)

        def head_pair(hp, c2):
            heads = [2 * hp, 2 * hp + 1]
            scores = [_dot_nt(sk_ref[2 * h + c], q_scr[2 * h + c, rows, :])
                      for h in heads for c in range(2)]
            first = _top_k_rows(scores)
            sa, sb = sa_ref[...], sb_ref[...]
            cands, cidxs = [], []
            for k in range(len(heads)):
                (v0, i0), (v1, i1) = first[2 * k], first[2 * k + 1]
                cands.append((_dot(sa, v0, precision=lax.Precision.HIGHEST)
                              + _dot(sb, v1, precision=lax.Precision.HIGHEST)) + pad_ref[...])
                cidxs.append((_dot(sa, i0.astype(F32)) * PEER_NKEYS
                              + _dot(sb, i1.astype(F32))).astype(jnp.int32))
            for h, (best, e) in zip(heads, _top_k_rows(cands, cidxs)):
                ex = jnp.exp(best - best[0:1])
                i_scr[h] = e
                g_scr[h] = ex / jnp.sum(ex, axis=0, keepdims=True)
            return c2

        lax.fori_loop(0, PEER_HEADS // 2, head_pair, 0)
        idx_ref[rows, :] = i_scr[...].reshape(PEER_SEL, LANES).T
        g_ref[rows, :] = g_scr[...].reshape(PEER_SEL, LANES).T
        return carry

    lax.fori_loop(0, tr // LANES, sub_body, 0)


def _route(xn2, w_q, subkeys):
    n = xn2.shape[0]
    sel_a, sel_b, pad = _candidate_cells()
    tr = _row_block(n, 512)
    qw = 2 * PEER_HEADS * LANES
    return pl.pallas_call(
        _route_kernel,
        grid=(n // tr,),
        in_specs=[pl.BlockSpec((tr, D_MODEL), lambda i: (i, 0)),
                  pl.BlockSpec((D_MODEL, qw), lambda i: (0, 0)),
                  pl.BlockSpec((2 * PEER_HEADS, PEER_NKEYS, LANES), lambda i: (0, 0, 0)),
                  pl.BlockSpec(sel_a.shape, lambda i: (0, 0)),
                  pl.BlockSpec(sel_b.shape, lambda i: (0, 0)),
                  pl.BlockSpec(pad.shape, lambda i: (0, 0))],
        out_specs=[pl.BlockSpec((tr, PEER_SEL), lambda i: (i, 0))] * 2,
        out_shape=[jax.ShapeDtypeStruct((n, PEER_SEL), jnp.int32),
                   jax.ShapeDtypeStruct((n, PEER_SEL), F32)],
        scratch_shapes=[pltpu.VMEM((2 * PEER_HEADS, tr, LANES), BF16),
                        pltpu.VMEM((PEER_HEADS, PEER_TOPK, LANES), jnp.int32),
                        pltpu.VMEM((PEER_HEADS, PEER_TOPK, LANES), F32)],
        compiler_params=pltpu.CompilerParams(
            dimension_semantics=("parallel",), vmem_limit_bytes=VMEM_LIMIT),
        name="peer_route",
    )(xn2, w_q, subkeys, sel_a, sel_b, pad)


def _sc_worker_id():
    return lax.axis_index("s") * SC_CORES + lax.axis_index("c")


def _sc_mesh():
    return plsc.VectorSubcoreMesh(core_axis_name="c", subcore_axis_name="s")


def _sc_gather_ring(table_hbm, idx_v, bufs, sems, unit, compute):
    def gather(g, k):
        off = pl.multiple_of(g * SC_ROWS, SC_ROWS)
        return pltpu.make_async_copy(table_hbm.at[idx_v.at[pl.ds(off, SC_ROWS)]], bufs[k], sems[k])

    for k in range(SC_NBUF):
        gather(k, k).start()

    units_per_iter = SC_NBUF // unit

    def ring_iter(i, carry):
        for m in range(units_per_iter):
            ks = range(m * unit, (m + 1) * unit)
            for k in ks:
                gather(i * SC_NBUF + k, k).wait()
            compute(i * units_per_iter + m, [bufs[k] for k in ks])
            for k in ks:
                g_next = (i + 1) * SC_NBUF + k

                @pl.when(g_next < SC_GATHERS)
                def _():
                    gather(g_next, k).start()
        return carry

    lax.fori_loop(0, SC_GATHERS // SC_NBUF, ring_iter, 0)


def _bf16_bits(x):
    b = lax.bitcast_convert_type(x, jnp.uint32)
    return (b + jnp.uint32(0x7FFF) + ((b >> 16) & jnp.uint32(1))) >> 16


def _pack_pairs(lo, hi):
    return _bf16_bits(lo) | (_bf16_bits(hi) << 16)


def _pack_table(t):
    return _pack_pairs(t[:, :HALF], t[:, HALF:]).reshape(t.shape[0], ROW_TILES, LANES)


def _mul_words(w, m):
    p = plsc.bitcast(plsc.bitcast(w, jnp.bfloat16) * plsc.bitcast(m, jnp.bfloat16), jnp.uint32)
    return lax.bitcast_convert_type(p << 16, F32), lax.bitcast_convert_type(p, F32)


def _sc_scratch(staged_cols, out_cols):
    return ([pltpu.VMEM((SC_TOK_BATCH * PEER_SEL,), jnp.int32),
             pltpu.VMEM((SC_TOK_BATCH, staged_cols), jnp.uint32),
             pltpu.VMEM((SC_TOK_BATCH, out_cols), F32)]
            + [pltpu.VMEM((SC_ROWS, ROW_TILES, LANES), jnp.uint32) for _ in range(SC_NBUF)]
            + [pltpu.SemaphoreType.DMA for _ in range(SC_NBUF)])


def _sc_batches(tpw, idx_hbm, staged_hbm, out_hbm, idx_v, staged_v, out_v, run):
    tok0 = _sc_worker_id() * tpw

    def batch(tb, carry):
        t0 = pl.multiple_of(tok0 + tb * SC_TOK_BATCH, SC_TOK_BATCH)
        pltpu.sync_copy(idx_hbm.at[pl.ds(pl.multiple_of(t0 * PEER_SEL, SC_TOK_BATCH * PEER_SEL),
                                         SC_TOK_BATCH * PEER_SEL)], idx_v)
        pltpu.sync_copy(staged_hbm.at[pl.ds(t0, SC_TOK_BATCH)], staged_v)
        run()
        pltpu.sync_copy(out_v, out_hbm.at[pl.ds(t0, SC_TOK_BATCH)])
        return carry

    lax.fori_loop(0, tpw // SC_TOK_BATCH, batch, 0)


def _peer_dot_sc(u3, idx_flat, xn2):
    t = xn2.shape[0]
    unit = 16 // SC_ROWS
    units_per_tok = PEER_SEL // 16

    @functools.partial(
        pl.kernel,
        mesh=_sc_mesh(),
        out_type=jax.ShapeDtypeStruct((t, PEER_SEL), F32),
        scratch_types=[pltpu.VMEM((16 * SC_LANES,), F32)] + _sc_scratch(HALF, PEER_SEL),
        compiler_params=pltpu.CompilerParams(needs_layout_passes=False),
        name="peer_dot",
    )
    def body(u_hbm, idx_hbm, x_hbm, out_hbm, red_v, idx_v, x_v, out_v, *rest):
        bufs, sems = rest[:SC_NBUF], rest[SC_NBUF:]

        def compute(u, ub):
            tl = u // units_per_tok
            rb = (u % units_per_tok) * 16

            def cbody(c, acc):
                acc = list(acc)
                c4 = c // (LANES // SC_LANES)
                lane0 = pl.multiple_of((c % (LANES // SC_LANES)) * SC_LANES, SC_LANES)
                col = pl.multiple_of(c * SC_LANES, SC_LANES)
                xw = x_v[tl, pl.ds(col, SC_LANES)]
                for q, buf in enumerate(ub):
                    for r in range(SC_ROWS):
                        a = q * SC_ROWS + r
                        lo, hi = _mul_words(buf[r, c4, pl.ds(lane0, SC_LANES)], xw)
                        acc[a] = acc[a] + lo + hi
                return tuple(acc)

            acc = lax.fori_loop(0, HALF // SC_LANES, cbody,
                                tuple(jnp.zeros((SC_LANES,), F32) for _ in range(16)))
            for a in range(16):
                red_v[pl.ds(a * SC_LANES, SC_LANES)] = acc[a]
            lane_base = lax.iota(jnp.int32, SC_LANES) * SC_LANES
            tot = plsc.load_gather(red_v, [lane_base])
            for k in range(1, SC_LANES):
                tot = tot + plsc.load_gather(red_v, [lane_base + k])
            out_v[tl, pl.ds(pl.multiple_of(rb, 16), 16)] = tot

        _sc_batches(t // SC_WORKERS, idx_hbm, x_hbm, out_hbm, idx_v, x_v, out_v,
                    lambda: _sc_gather_ring(u_hbm, idx_v, bufs, sems, unit, compute))

    return body(u3, idx_flat, xn2)


def _peer_combine_sc(v3, idx_flat, coef_b):
    t = coef_b.shape[0]
    unit = 32 // SC_ROWS
    units_per_tok = PEER_SEL // 32

    @functools.partial(
        pl.kernel,
        mesh=_sc_mesh(),
        out_type=jax.ShapeDtypeStruct((t, D_MODEL), F32),
        scratch_types=_sc_scratch(PEER_SEL * SC_LANES, D_MODEL),
        compiler_params=pltpu.CompilerParams(needs_layout_passes=False),
        name="peer_combine",
    )
    def body(v_hbm, idx_hbm, coef_hbm, out_hbm, idx_v, coef_v, out_v, *rest):
        bufs, sems = rest[:SC_NBUF], rest[SC_NBUF:]

        def compute(u, ub):
            tl = u // units_per_tok
            rb = (u % units_per_tok) * 32
            for g in range(ROW_TILES):
                cols = ([pl.ds(g * LANES + l * SC_LANES, SC_LANES) for l in range(8)]
                        + [pl.ds(HALF + g * LANES + l * SC_LANES, SC_LANES) for l in range(8)])

                def rbody(r, acc):
                    acc = list(acc)
                    for q, buf in enumerate(ub):
                        co = pl.multiple_of((rb + q * SC_ROWS + r) * SC_LANES, SC_LANES)
                        cf = coef_v[tl, pl.ds(co, SC_LANES)]
                        for l in range(8):
                            lo, hi = _mul_words(buf[r, g, pl.ds(l * SC_LANES, SC_LANES)], cf)
                            acc[l] = acc[l] + lo
                            acc[8 + l] = acc[8 + l] + hi
                    return tuple(acc)

                acc = lax.fori_loop(0, SC_ROWS, rbody, tuple(out_v[tl, c] for c in cols))
                for k in range(16):
                    out_v[tl, cols[k]] = acc[k]

        def run():
            def zero(tl, c2):
                for k in range(D_MODEL // SC_LANES):
                    out_v[tl, pl.ds(k * SC_LANES, SC_LANES)] = jnp.zeros((SC_LANES,), F32)
                return c2

            lax.fori_loop(0, SC_TOK_BATCH, zero, 0)
            _sc_gather_ring(v_hbm, idx_v, bufs, sems, unit, compute)

        _sc_batches(t // SC_WORKERS, idx_hbm, coef_hbm, out_hbm, idx_v, coef_v, out_v, run)

    return body(v3, idx_flat, coef_b)


def _coef_kernel(act_ref, g_ref, e_ref, coef_ref):
    act = act_ref[...]
    gelu = 0.5 * act * (1.0 + lax.erf(act * np.float32(math.sqrt(0.5))))
    c = _dot(g_ref[...] * gelu, e_ref[...], precision=lax.Precision.HIGHEST)
    coef_ref[...] = _pack_pairs(c, c)


def _coef(act, g):
    n = act.shape[0]
    tr = _row_block(n, 512)
    w = PEER_SEL * SC_LANES
    expand = (jnp.arange(w, dtype=jnp.int32)[None, :] // SC_LANES
              == jnp.arange(PEER_SEL, dtype=jnp.int32)[:, None]).astype(F32)
    blk = pl.BlockSpec((tr, PEER_SEL), lambda i: (i, 0))
    return pl.pallas_call(
        _coef_kernel,
        grid=(n // tr,),
        in_specs=[blk, blk, pl.BlockSpec((PEER_SEL, w), lambda i: (0, 0))],
        out_specs=pl.BlockSpec((tr, w), lambda i: (i, 0)),
        out_shape=jax.ShapeDtypeStruct((n, w), jnp.uint32),
        compiler_params=pltpu.CompilerParams(
            dimension_semantics=("parallel",), vmem_limit_bytes=VMEM_LIMIT),
        name="peer_coef",
    )(act, g, expand)


def _final_kernel(h_ref, p_ref, g_ref, o_ref):
    o_ref[...] = _rms(h_ref[...] + p_ref[...], g_ref[...])


def _final(h2, peer_out, gain):
    n = h2.shape[0]
    tm = _row_block(n, 1024)
    blk = pl.BlockSpec((tm, D_MODEL), lambda i: (i, 0))
    return pl.pallas_call(
        _final_kernel,
        grid=(n // tm,),
        in_specs=[blk, blk, pl.BlockSpec((1, D_MODEL), lambda i: (0, 0))],
        out_specs=blk,
        out_shape=jax.ShapeDtypeStruct((n, D_MODEL), F32),
        compiler_params=pltpu.CompilerParams(
            dimension_semantics=("parallel",), vmem_limit_bytes=VMEM_LIMIT),
        name="final_norm",
    )(h2, peer_out, gain)


def _pad_heads(w, dk):
    lead = w.shape[:-1]
    w = w.reshape(lead + (HEADS, dk))
    w = jnp.pad(w, [(0, 0)] * len(lead) + [(0, 0), (0, LANES - dk)])
    return w.reshape(lead + (HEADS * LANES,))


def _rot_heads(w, dk):
    lead = w.shape[:-1]
    w = w.reshape(lead + (HEADS, 2, dk // 2))
    w = jnp.stack([-w[..., 1, :], w[..., 0, :]], axis=-2)
    return w.reshape(lead + (HEADS * dk,))


def _pack_in_weights(w_in):
    gq, gk, gv, gg, ga, rq, rk, rv, rg = jnp.split(w_in, np.cumsum(IN_SPLITS)[:-1].tolist(), axis=-1)
    ga = jnp.pad(ga, ((0, 0), (0, 2 * LANES - GATE_RANK)))
    return jnp.concatenate(
        [_pad_heads(gq, DK), _pad_heads(gk, DK), gv, gg,
         _pad_heads(rq, DK), _pad_heads(rk, DK), rv, rg,
         _pad_heads(_rot_heads(rq, DK), DK), _pad_heads(_rot_heads(rk, DK), DK), ga],
        axis=-1).astype(BF16)


def _rope_tables(pos):
    inv_freq = ROPE_BASE ** (-jnp.arange(0, DK, 2, dtype=F32) / DK)
    ang = pos[:, None] * inv_freq[None, :]
    full = lambda t: _pad_heads(jnp.tile(t, (1, 2 * HEADS)), DK)
    return full(jnp.cos(ang)), full(jnp.sin(ang))


def kernel(x, meta_tokens, norm1_gain, w_in, gla_gate_w2, gla_gate_b, gla_norm_gain, ret_norm_gain, ret_norm_bias, w_out, norm2_gain, peer_w_q, peer_subkeys, peer_u, peer_v, final_gain):
    b, s, d = x.shape
    assert d == D_MODEL and (s // SEGS_PER_SEQ) % SC_CALL_TOKENS == 0
    assert norm1_gain.shape[0] == 1, "single-layer block"
    x2d = x.reshape(b * s, d)

    w_all = _pack_in_weights(w_in[0])
    w2p = jnp.pad(_pad_heads(gla_gate_w2[0], DK), ((0, LANES - GATE_RANK), (0, 0))).astype(BF16)
    gbp = _pad_heads(gla_gate_b[0][None, :], DK)
    g1 = norm1_gain[0][None, :]
    ggain = gla_norm_gain[0][None, :]
    rgain = ret_norm_gain[0][None, :]
    rbias = ret_norm_bias[0][None, :]

    h0 = jnp.concatenate([jnp.zeros((N_PAD, d), x.dtype), meta_tokens.astype(x.dtype)], axis=0)
    cos0, sin0 = _rope_tables(jnp.arange(CHUNK, dtype=F32) - N_PAD)
    cos1, sin1 = _rope_tables(jnp.arange(s, dtype=F32) + N_META)
    zero_state = jnp.zeros((HEADS, DV, LANES), F32)
    _, s_meta, r_meta = _mixer(_norm_proj(h0, g1, w_all), cos0, sin0, w2p, gbp, ggain, rgain, rbias,
                               zero_state, zero_state, batch=1, n_invalid=N_PAD)

    w_out_b = w_out[0].astype(BF16)
    g2 = norm2_gain[0][None, :]
    w_q = peer_w_q[0].astype(BF16)
    subkeys = peer_subkeys[0].reshape(2 * PEER_HEADS, PEER_NKEYS, PEER_DKEY // 2).astype(BF16)
    u3 = _pack_table(peer_u[0])
    v3 = _pack_table(peer_v[0])
    gf = final_gain[None, :]

    seg = s // SEGS_PER_SEQ
    outs = []
    for bi in range(b):
        s_state, r_state = s_meta, r_meta
        for si in range(SEGS_PER_SEQ):
            xs = x2d[bi * s + si * seg:bi * s + (si + 1) * seg]
            pos = slice(si * seg, (si + 1) * seg)
            o, s_state, r_state = _mixer(_norm_proj(xs, g1, w_all), cos1[pos], sin1[pos], w2p, gbp,
                                         ggain, rgain, rbias, s_state, r_state, batch=1, n_invalid=0)
            h2, xn2, xn2_pairs = _out_proj(o, xs, w_out_b, g2)
            idx, g = _route(xn2, w_q, subkeys)
            idx_flat = idx.reshape(-1)
            act = _peer_dot_sc(u3, idx_flat, xn2_pairs)
            peer_out = _peer_combine_sc(v3, idx_flat, _coef(act, g))
            outs.append(_final(h2, peer_out, gf))
    return jnp.concatenate(outs, axis=0).reshape(b, s, d)
```

```python
import functools
import math

import jax
import jax.numpy as jnp
import numpy as np
from jax import lax
from jax.experimental import pallas as pl
from jax.experimental.pallas import tpu as pltpu
from jax.experimental.pallas import tpu_sc as plsc

F32 = jnp.float32
BF16 = jnp.bfloat16

D_MODEL = 1024
N_META = 16
CHUNK = 64
N_PAD = CHUNK - N_META
EPS = 1e-6

HEADS = 4
DV = 128
DK = 64
LANES = 128
GATE_RANK = 16
GATE_TAU = 16.0
ROPE_BASE = 10000.0
IN_SPLITS = (HEADS * DK, HEADS * DK, HEADS * DV, HEADS * DV, GATE_RANK,
             HEADS * DK, HEADS * DK, HEADS * DV, HEADS * DV)

PEER_HEADS = 8
PEER_NKEYS = 128
PEER_DKEY = 256
PEER_TOPK = 16
PEER_SEL = PEER_HEADS * PEER_TOPK

HW = HEADS * LANES
OFF_GQ, OFF_GK, OFF_GV, OFF_GG = 0, HW, 2 * HW, 3 * HW
OFF_RQ, OFF_RK, OFF_RV, OFF_RG = 4 * HW, 5 * HW, 6 * HW, 7 * HW
OFF_RQR, OFF_RKR, OFF_GA = 8 * HW, 9 * HW, 10 * HW
PROJ_W = 10 * HW + 2 * LANES
PROJ_TN = PROJ_W // 3

LOG_GAMMA = tuple(math.log(1.0 - 2.0 ** (-5.0 - h)) for h in range(HEADS))

SC_CORES, SC_SUBCORES, SC_LANES = 2, 16, 16
SC_WORKERS = SC_CORES * SC_SUBCORES
SC_TOK_BATCH = 16
SC_ROWS = 16
SC_NBUF = 8
SC_GATHERS = SC_TOK_BATCH * PEER_SEL // SC_ROWS
HALF = D_MODEL // 2
ROW_TILES = HALF // LANES

SC_CALL_TOKENS = SC_WORKERS * SC_TOK_BATCH
SEGS_PER_SEQ = 2
SC_LOOKAHEAD = 2

VMEM_LIMIT = 48 * 1024 * 1024


def _row_block(n, target):
    blk = math.gcd(n, target)
    assert blk % 8 == 0 or blk == n, (n, target)
    return blk


def _dot(a, b, **kw):
    return jnp.dot(a, b, preferred_element_type=F32, **kw)


def _dot_nt(a, b):
    return lax.dot_general(a, b, (((1,), (1,)), ((), ())), preferred_element_type=F32)


def _dot_tn(a, b):
    return lax.dot_general(a, b, (((0,), (0,)), ((), ())), preferred_element_type=F32)


def _rms(x, g):
    return x * lax.rsqrt(jnp.mean(x * x, axis=-1, keepdims=True) + EPS) * g


def _silu(x):
    return x * jax.nn.sigmoid(x)


def _norm_proj_kernel(x_ref, g_ref, w_ref, o_ref):
    xn = _rms(x_ref[...], g_ref[...]).astype(BF16)
    o_ref[...] = _dot(xn, w_ref[...])


def _norm_proj(x2d, gain, w_all):
    n = x2d.shape[0]
    tm = _row_block(n, 512)
    return pl.pallas_call(
        _norm_proj_kernel,
        grid=(n // tm, PROJ_W // PROJ_TN),
        in_specs=[
            pl.BlockSpec((tm, D_MODEL), lambda i, j: (i, 0)),
            pl.BlockSpec((1, D_MODEL), lambda i, j: (0, 0)),
            pl.BlockSpec((D_MODEL, PROJ_TN), lambda i, j: (0, j)),
        ],
        out_specs=pl.BlockSpec((tm, PROJ_TN), lambda i, j: (i, j)),
        out_shape=jax.ShapeDtypeStruct((n, PROJ_W), F32),
        compiler_params=pltpu.CompilerParams(
            dimension_semantics=("parallel", "parallel"), vmem_limit_bytes=VMEM_LIMIT),
        name="norm_in_proj",
    )(x2d, gain, w_all)


def _mixer_kernel(proj_ref, cos_ref, sin_ref, w2_ref, gb_ref, ggain_ref, rgain_ref, rbias_ref,
                  s0_ref, r0_ref, o_ref, s_out_ref, r_out_ref, s_scr, r_scr, *, n_invalid, rows_blk):
    j = pl.program_id(1)

    @pl.when(j == 0)
    def _():
        s_scr[...] = s0_ref[...]
        r_scr[...] = r0_ref[...]

    row = lax.broadcasted_iota(jnp.int32, (CHUNK, CHUNK), 0)
    col = lax.broadcasted_iota(jnp.int32, (CHUNK, CHUNK), 1)
    causal = row >= col
    tril = causal.astype(F32)
    diff = jnp.maximum(row - col, 0).astype(F32)
    rowf = lax.broadcasted_iota(jnp.int32, (CHUNK, LANES), 0).astype(F32)
    row1 = lax.broadcasted_iota(jnp.int32, (CHUNK, 1), 0)

    def chunk_body(c, carry):
        r0 = pl.multiple_of(c * CHUNK, CHUNK)
        rows = pl.ds(r0, CHUNK)
        valid = ((j * rows_blk + r0 + row1) >= n_invalid).astype(F32)

        ga = proj_ref[rows, OFF_GA:OFF_GA + LANES].astype(BF16)
        pre = _dot(ga, w2_ref[...]) + gb_ref[...]
        log_a = jax.nn.log_sigmoid(pre) * (1.0 / GATE_TAU)
        bcum = _dot(tril, log_a, precision=lax.Precision.HIGHEST)
        b_last = bcum[CHUNK - 1:CHUNK, :]
        gq = proj_ref[rows, OFF_GQ:OFF_GQ + HW] * (DK ** -0.5)
        gk = proj_ref[rows, OFF_GK:OFF_GK + HW] * valid
        q_t = (gq * jnp.exp(bcum)).astype(BF16)
        k_t = (gk * jnp.exp(-bcum)).astype(BF16)
        k_end = (gk * jnp.exp(b_last - bcum)).astype(BF16)
        decay = jnp.exp(b_last)
        for h in range(HEADS):
            sl = slice(h * LANES, (h + 1) * LANES)
            v = proj_ref[rows, OFF_GV + h * DV:OFF_GV + (h + 1) * DV].astype(BF16)
            att = jnp.where(causal, _dot_nt(q_t[:, sl], k_t[:, sl]), 0.0).astype(BF16)
            s_prev = s_scr[h]
            o = _dot(att, v) + _dot_nt(q_t[:, sl], s_prev.astype(BF16))
            s_scr[h] = s_prev * decay[:, sl] + _dot_tn(v, k_end[:, sl])
            o = o * lax.rsqrt(jnp.mean(o * o, axis=-1, keepdims=True) + EPS) * ggain_ref[...]
            gate = proj_ref[rows, OFF_GG + h * DV:OFF_GG + (h + 1) * DV]
            o_ref[rows, h * DV:(h + 1) * DV] = (o * _silu(gate)).astype(o_ref.dtype)

        cosb = cos_ref[rows, :]
        sinb = sin_ref[rows, :]
        rq = proj_ref[rows, OFF_RQ:OFF_RQ + HW]
        rqr = proj_ref[rows, OFF_RQR:OFF_RQR + HW]
        rk = proj_ref[rows, OFF_RK:OFF_RK + HW] * valid
        rkr = proj_ref[rows, OFF_RKR:OFF_RKR + HW] * valid
        q_r = (rq * cosb + rqr * sinb).astype(BF16)
        k_r = (rk * cosb + rkr * sinb) * (DK ** -0.5)
        for h in range(HEADS):
            sl = slice(h * LANES, (h + 1) * LANES)
            lg = LOG_GAMMA[h]
            dmat = jnp.where(causal, jnp.exp(lg * diff), 0.0)
            xi = jnp.exp(lg * (rowf + 1.0))
            zeta = jnp.exp(lg * (CHUNK - 1.0 - rowf))
            v = proj_ref[rows, OFF_RV + h * DV:OFF_RV + (h + 1) * DV].astype(BF16)
            k_h = k_r[:, sl]
            att = (_dot_nt(q_r[:, sl], k_h.astype(BF16)) * dmat).astype(BF16)
            r_prev = r_scr[h]
            o = _dot(att, v) + _dot_nt(q_r[:, sl], r_prev.astype(BF16)) * xi
            r_scr[h] = r_prev * math.exp(lg * CHUNK) + _dot_tn(v, (k_h * zeta).astype(BF16))
            mu = jnp.mean(o, axis=-1, keepdims=True)
            var = jnp.mean(jnp.square(o - mu), axis=-1, keepdims=True)
            o = (o - mu) * lax.rsqrt(var + EPS) * rgain_ref[...] + rbias_ref[...]
            gate = proj_ref[rows, OFF_RG + h * DV:OFF_RG + (h + 1) * DV]
            o_ref[rows, HW + h * DV:HW + (h + 1) * DV] = (o * _silu(gate)).astype(o_ref.dtype)
        return carry

    lax.fori_loop(0, rows_blk // CHUNK, chunk_body, 0)
    s_out_ref[...] = s_scr[...]
    r_out_ref[...] = r_scr[...]


def _mixer(proj, cosf, sinf, w2p, gbp, ggain, rgain, rbias, s0, r0, *, batch, n_invalid):
    rows = proj.shape[0] // batch
    rows_blk = _row_block(rows, 256)
    nblk = rows // rows_blk
    const2 = lambda b, j: (0, 0)
    const3 = lambda b, j: (0, 0, 0)
    state = jax.ShapeDtypeStruct((HEADS, DV, LANES), F32)
    return pl.pallas_call(
        functools.partial(_mixer_kernel, n_invalid=n_invalid, rows_blk=rows_blk),
        grid=(batch, nblk),
        in_specs=[
            pl.BlockSpec((rows_blk, PROJ_W), lambda b, j: (b * nblk + j, 0)),
            pl.BlockSpec((rows_blk, HW), lambda b, j: (j, 0)),
            pl.BlockSpec((rows_blk, HW), lambda b, j: (j, 0)),
            pl.BlockSpec((LANES, HW), const2),
            pl.BlockSpec((1, HW), const2),
            pl.BlockSpec((1, DV), const2),
            pl.BlockSpec((1, DV), const2),
            pl.BlockSpec((1, DV), const2),
            pl.BlockSpec((HEADS, DV, LANES), const3),
            pl.BlockSpec((HEADS, DV, LANES), const3),
        ],
        out_specs=[
            pl.BlockSpec((rows_blk, D_MODEL), lambda b, j: (b * nblk + j, 0)),
            pl.BlockSpec((HEADS, DV, LANES), const3),
            pl.BlockSpec((HEADS, DV, LANES), const3),
        ],
        out_shape=[jax.ShapeDtypeStruct((batch * rows, D_MODEL), BF16), state, state],
        scratch_shapes=[pltpu.VMEM((HEADS, DV, LANES), F32), pltpu.VMEM((HEADS, DV, LANES), F32)],
        compiler_params=pltpu.CompilerParams(
            dimension_semantics=("arbitrary", "arbitrary"), vmem_limit_bytes=VMEM_LIMIT),
        name="token_mixers",
    )(proj, cosf, sinf, w2p, gbp, ggain, rgain, rbias, s0, r0)


def _out_proj_kernel(o_ref, x_ref, w_ref, g_ref, h_ref, xn_ref, xp_ref):
    h = x_ref[...] + _dot(o_ref[...], w_ref[...])
    h_ref[...] = h
    xn = _rms(h, g_ref[...])
    xn_ref[...] = xn
    xp_ref[...] = _pack_pairs(xn[:, :HALF], xn[:, HALF:])


def _out_proj(o, x2d, w_out, gain2):
    n = x2d.shape[0]
    tm = _row_block(n, 512)
    blk = pl.BlockSpec((tm, D_MODEL), lambda i: (i, 0))
    return pl.pallas_call(
        _out_proj_kernel,
        grid=(n // tm,),
        in_specs=[blk, blk, pl.BlockSpec((D_MODEL, D_MODEL), lambda i: (0, 0)),
                  pl.BlockSpec((1, D_MODEL), lambda i: (0, 0))],
        out_specs=[blk, blk, pl.BlockSpec((tm, HALF), lambda i: (i, 0))],
        out_shape=[jax.ShapeDtypeStruct((n, D_MODEL), F32)] * 2 + [jax.ShapeDtypeStruct((n, HALF), jnp.uint32)],
        compiler_params=pltpu.CompilerParams(
            dimension_semantics=("parallel",), vmem_limit_bytes=VMEM_LIMIT),
        name="out_proj_norm",
    )(o, x2d, w_out, gain2)


def _top_k_rows(arrays, payloads=None):
    arrays = list(arrays)
    n = arrays[0].shape[0]
    rowi = lax.broadcasted_iota(jnp.int32, arrays[0].shape, 0)
    vals = [[] for _ in arrays]
    picks = [[] for _ in arrays]
    for _ in range(PEER_TOPK):
        for k, s in enumerate(arrays):
            m = jnp.max(s, axis=0, keepdims=True)
            am = jnp.min(jnp.where(s == m, rowi, n), axis=0, keepdims=True)
            sel = rowi == am
            vals[k].append(m)
            if payloads is None:
                picks[k].append(am)
            else:
                picks[k].append(jnp.min(jnp.where(sel, payloads[k], jnp.iinfo(jnp.int32).max),
                                        axis=0, keepdims=True))
            arrays[k] = jnp.where(sel, -jnp.inf, s)
    return [(jnp.concatenate(v, axis=0), jnp.concatenate(p, axis=0)) for v, p in zip(vals, picks)]


def _candidate_cells():
    cells = [(a, b) for a in range(PEER_TOPK) for b in range(PEER_TOPK) if (a + 1) * (b + 1) <= PEER_TOPK]
    n = -(-len(cells) // 8) * 8
    sel_a = np.zeros((n, PEER_TOPK), np.float32)
    sel_b = np.zeros((n, PEER_TOPK), np.float32)
    pad = np.full((n, 1), -np.inf, np.float32)
    for r, (a, b) in enumerate(cells):
        sel_a[r, a] = 1.0
        sel_b[r, b] = 1.0
        pad[r, 0] = 0.0
    return sel_a, sel_b, pad


def _route_kernel(xn_ref, wq_ref, sk_ref, sa_ref, sb_ref, pad_ref, idx_ref, g_ref, q_scr, i_scr, g_scr):
    tr = xn_ref.shape[0]
    q = _dot(xn_ref[...].astype(BF16), wq_ref[...])
    for hc in range(2 * PEER_HEADS):
        q_scr[hc] = q[:, hc * LANES:(hc + 1) * LANES].astype(BF16)

    def sub_body(sb, carry):
        rows = pl.ds(pl.multiple_of(sb * LANES, LANES), LANES)

        def head_pair(hp, c2):
            heads = [2 * hp, 2 * hp + 1]
            scores = [_dot_nt(sk_ref[2 * h + c], q_scr[2 * h + c, rows, :])
                      for h in heads for c in range(2)]
            first = _top_k_rows(scores)
            sa, sb = sa_ref[...], sb_ref[...]
            cands, cidxs = [], []
            for k in range(len(heads)):
                (v0, i0), (v1, i1) = first[2 * k], first[2 * k + 1]
                cands.append((_dot(sa, v0, precision=lax.Precision.HIGHEST)
                              + _dot(sb, v1, precision=lax.Precision.HIGHEST)) + pad_ref[...])
                cidxs.append((_dot(sa, i0.astype(F32)) * PEER_NKEYS
                              + _dot(sb, i1.astype(F32))).astype(jnp.int32))
            for h, (best, e) in zip(heads, _top_k_rows(cands, cidxs)):
                ex = jnp.exp(best - best[0:1])
                i_scr[h] = e
                g_scr[h] = ex / jnp.sum(ex, axis=0, keepdims=True)
            return c2

        lax.fori_loop(0, PEER_HEADS // 2, head_pair, 0)
        idx_ref[rows, :] = i_scr[...].reshape(PEER_SEL, LANES).T
        g_ref[rows, :] = g_scr[...].reshape(PEER_SEL, LANES).T
        return carry

    lax.fori_loop(0, tr // LANES, sub_body, 0)


def _route(xn2, w_q, subkeys):
    n = xn2.shape[0]
    sel_a, sel_b, pad = _candidate_cells()
    tr = _row_block(n, 512)
    qw = 2 * PEER_HEADS * LANES
    return pl.pallas_call(
        _route_kernel,
        grid=(n // tr,),
        in_specs=[pl.BlockSpec((tr, D_MODEL), lambda i: (i, 0)),
                  pl.BlockSpec((D_MODEL, qw), lambda i: (0, 0)),
                  pl.BlockSpec((2 * PEER_HEADS, PEER_NKEYS, LANES), lambda i: (0, 0, 0)),
                  pl.BlockSpec(sel_a.shape, lambda i: (0, 0)),
                  pl.BlockSpec(sel_b.shape, lambda i: (0, 0)),
                  pl.BlockSpec(pad.shape, lambda i: (0, 0))],
        out_specs=[pl.BlockSpec((tr, PEER_SEL), lambda i: (i, 0))] * 2,
        out_shape=[jax.ShapeDtypeStruct((n, PEER_SEL), jnp.int32),
                   jax.ShapeDtypeStruct((n, PEER_SEL), F32)],
        scratch_shapes=[pltpu.VMEM((2 * PEER_HEADS, tr, LANES), BF16),
                        pltpu.VMEM((PEER_HEADS, PEER_TOPK, LANES), jnp.int32),
                        pltpu.VMEM((PEER_HEADS, PEER_TOPK, LANES), F32)],
        compiler_params=pltpu.CompilerParams(
            dimension_semantics=("parallel",), vmem_limit_bytes=VMEM_LIMIT),
        name="peer_route",
    )(xn2, w_q, subkeys, sel_a, sel_b, pad)


def _sc_worker_id():
    return lax.axis_index("s") * SC_CORES + lax.axis_index("c")


def _sc_mesh():
    return plsc.VectorSubcoreMesh(core_axis_name="c", subcore_axis_name="s")


def _sc_gather_ring(table_hbm, idx_v, bufs, sems, unit, compute):
    def gather(g, k):
        off = pl.multiple_of(g * SC_ROWS, SC_ROWS)
        return pltpu.make_async_copy(table_hbm.at[idx_v.at[pl.ds(off, SC_ROWS)]], bufs[k], sems[k])

    for k in range(SC_NBUF):
        gather(k, k).start()

    units_per_iter = SC_NBUF // unit

    def ring_iter(i, carry):
        for m in range(units_per_iter):
            ks = range(m * unit, (m + 1) * unit)
            for k in ks:
                gather(i * SC_NBUF + k, k).wait()
            compute(i * units_per_iter + m, [bufs[k] for k in ks])
            for k in ks:
                g_next = (i + 1) * SC_NBUF + k

                @pl.when(g_next < SC_GATHERS)
                def _():
                    gather(g_next, k).start()
        return carry

    lax.fori_loop(0, SC_GATHERS // SC_NBUF, ring_iter, 0)


def _bf16_bits(x):
    b = lax.bitcast_convert_type(x, jnp.uint32)
    return (b + jnp.uint32(0x7FFF) + ((b >> 16) & jnp.uint32(1))) >> 16


def _pack_pairs(lo, hi):
    return _bf16_bits(lo) | (_bf16_bits(hi) << 16)


def _pack_table(t):
    return _pack_pairs(t[:, :HALF], t[:, HALF:]).reshape(t.shape[0], ROW_TILES, LANES)


def _mul_words(w, m):
    p = plsc.bitcast(plsc.bitcast(w, jnp.bfloat16) * plsc.bitcast(m, jnp.bfloat16), jnp.uint32)
    return lax.bitcast_convert_type(p << 16, F32), lax.bitcast_convert_type(p, F32)


def _sc_scratch(staged_cols, out_cols):
    return ([pltpu.VMEM((SC_TOK_BATCH * PEER_SEL,), jnp.int32),
             pltpu.VMEM((SC_TOK_BATCH, staged_cols), jnp.uint32),
             pltpu.VMEM((SC_TOK_BATCH, out_cols), F32)]
            + [pltpu.VMEM((SC_ROWS, ROW_TILES, LANES), jnp.uint32) for _ in range(SC_NBUF)]
            + [pltpu.SemaphoreType.DMA for _ in range(SC_NBUF)])


def _sc_batches(tpw, idx_hbm, staged_hbm, out_hbm, idx_v, staged_v, out_v, run):
    tok0 = _sc_worker_id() * tpw

    def batch(tb, carry):
        t0 = pl.multiple_of(tok0 + tb * SC_TOK_BATCH, SC_TOK_BATCH)
        pltpu.sync_copy(idx_hbm.at[pl.ds(pl.multiple_of(t0 * PEER_SEL, SC_TOK_BATCH * PEER_SEL),
                                         SC_TOK_BATCH * PEER_SEL)], idx_v)
        pltpu.sync_copy(staged_hbm.at[pl.ds(t0, SC_TOK_BATCH)], staged_v)
        run()
        pltpu.sync_copy(out_v, out_hbm.at[pl.ds(t0, SC_TOK_BATCH)])
        return carry

    lax.fori_loop(0, tpw // SC_TOK_BATCH, batch, 0)


def _peer_dot_sc(u3, idx_flat, xn2):
    t = xn2.shape[0]
    unit = 16 // SC_ROWS
    units_per_tok = PEER_SEL // 16

    @functools.partial(
        pl.kernel,
        mesh=_sc_mesh(),
        out_type=jax.ShapeDtypeStruct((t, PEER_SEL), F32),
        scratch_types=[pltpu.VMEM((16 * SC_LANES,), F32)] + _sc_scratch(HALF, PEER_SEL),
        compiler_params=pltpu.CompilerParams(needs_layout_passes=False),
        name="peer_dot",
    )
    def body(u_hbm, idx_hbm, x_hbm, out_hbm, red_v, idx_v, x_v, out_v, *rest):
        bufs, sems = rest[:SC_NBUF], rest[SC_NBUF:]

        def compute(u, ub):
            tl = u // units_per_tok
            rb = (u % units_per_tok) * 16

            def cbody(c, acc):
                acc = list(acc)
                c4 = c // (LANES // SC_LANES)
                lane0 = pl.multiple_of((c % (LANES // SC_LANES)) * SC_LANES, SC_LANES)
                col = pl.multiple_of(c * SC_LANES, SC_LANES)
                xw = x_v[tl, pl.ds(col, SC_LANES)]
                for q, buf in enumerate(ub):
                    for r in range(SC_ROWS):
                        a = q * SC_ROWS + r
                        lo, hi = _mul_words(buf[r, c4, pl.ds(lane0, SC_LANES)], xw)
                        acc[a] = acc[a] + lo + hi
                return tuple(acc)

            acc = lax.fori_loop(0, HALF // SC_LANES, cbody,
                                tuple(jnp.zeros((SC_LANES,), F32) for _ in range(16)))
            for a in range(16):
                red_v[pl.ds(a * SC_LANES, SC_LANES)] = acc[a]
            lane_base = lax.iota(jnp.int32, SC_LANES) * SC_LANES
            tot = plsc.load_gather(red_v, [lane_base])
            for k in range(1, SC_LANES):
                tot = tot + plsc.load_gather(red_v, [lane_base + k])
            out_v[tl, pl.ds(pl.multiple_of(rb, 16), 16)] = tot

        _sc_batches(t // SC_WORKERS, idx_hbm, x_hbm, out_hbm, idx_v, x_v, out_v,
                    lambda: _sc_gather_ring(u_hbm, idx_v, bufs, sems, unit, compute))

    return body(u3, idx_flat, xn2)


def _peer_combine_sc(v3, idx_flat, coef_b):
    t = coef_b.shape[0]
    unit = 32 // SC_ROWS
    units_per_tok = PEER_SEL // 32

    @functools.partial(
        pl.kernel,
        mesh=_sc_mesh(),
        out_type=jax.ShapeDtypeStruct((t, D_MODEL), F32),
        scratch_types=_sc_scratch(PEER_SEL * SC_LANES, D_MODEL),
        compiler_params=pltpu.CompilerParams(needs_layout_passes=False),
        name="peer_combine",
    )
    def body(v_hbm, idx_hbm, coef_hbm, out_hbm, idx_v, coef_v, out_v, *rest):
        bufs, sems = rest[:SC_NBUF], rest[SC_NBUF:]

        def compute(u, ub):
            tl = u // units_per_tok
            rb = (u % units_per_tok) * 32
            for g in range(ROW_TILES):
                cols = ([pl.ds(g * LANES + l * SC_LANES, SC_LANES) for l in range(8)]
                        + [pl.ds(HALF + g * LANES + l * SC_LANES, SC_LANES) for l in range(8)])

                def rbody(r, acc):
                    acc = list(acc)
                    for q, buf in enumerate(ub):
                        co = pl.multiple_of((rb + q * SC_ROWS + r) * SC_LANES, SC_LANES)
                        cf = coef_v[tl, pl.ds(co, SC_LANES)]
                        for l in range(8):
                            lo, hi = _mul_words(buf[r, g, pl.ds(l * SC_LANES, SC_LANES)], cf)
                            acc[l] = acc[l] + lo
                            acc[8 + l] = acc[8 + l] + hi
                    return tuple(acc)

                acc = lax.fori_loop(0, SC_ROWS, rbody, tuple(out_v[tl, c] for c in cols))
                for k in range(16):
                    out_v[tl, cols[k]] = acc[k]

        def run():
            def zero(tl, c2):
                for k in range(D_MODEL // SC_LANES):
                    out_v[tl, pl.ds(k * SC_LANES, SC_LANES)] = jnp.zeros((SC_LANES,), F32)
                return c2

            lax.fori_loop(0, SC_TOK_BATCH, zero, 0)
            _sc_gather_ring(v_hbm, idx_v, bufs, sems, unit, compute)

        _sc_batches(t // SC_WORKERS, idx_hbm, coef_hbm, out_hbm, idx_v, coef_v, out_v, run)

    return body(v3, idx_flat, coef_b)


def _coef_kernel(act_ref, g_ref, e_ref, coef_ref):
    act = act_ref[...]
    gelu = 0.5 * act * (1.0 + lax.erf(act * np.float32(math.sqrt(0.5))))
    c = _dot(g_ref[...] * gelu, e_ref[...], precision=lax.Precision.HIGHEST)
    coef_ref[...] = _pack_pairs(c, c)


def _coef(act, g):
    n = act.shape[0]
    tr = _row_block(n, 512)
    w = PEER_SEL * SC_LANES
    expand = (jnp.arange(w, dtype=jnp.int32)[None, :] // SC_LANES
              == jnp.arange(PEER_SEL, dtype=jnp.int32)[:, None]).astype(F32)
    blk = pl.BlockSpec((tr, PEER_SEL), lambda i: (i, 0))
    return pl.pallas_call(
        _coef_kernel,
        grid=(n // tr,),
        in_specs=[blk, blk, pl.BlockSpec((PEER_SEL, w), lambda i: (0, 0))],
        out_specs=pl.BlockSpec((tr, w), lambda i: (i, 0)),
        out_shape=jax.ShapeDtypeStruct((n, w), jnp.uint32),
        compiler_params=pltpu.CompilerParams(
            dimension_semantics=("parallel",), vmem_limit_bytes=VMEM_LIMIT),
        name="peer_coef",
    )(act, g, expand)


def _final_kernel(h_ref, p_ref, g_ref, o_ref):
    o_ref[...] = _rms(h_ref[...] + p_ref[...], g_ref[...])


def _final(h2, peer_out, gain):
    n = h2.shape[0]
    tm = _row_block(n, 1024)
    blk = pl.BlockSpec((tm, D_MODEL), lambda i: (i, 0))
    return pl.pallas_call(
        _final_kernel,
        grid=(n // tm,),
        in_specs=[blk, blk, pl.BlockSpec((1, D_MODEL), lambda i: (0, 0))],
        out_specs=blk,
        out_shape=jax.ShapeDtypeStruct((n, D_MODEL), F32),
        compiler_params=pltpu.CompilerParams(
            dimension_semantics=("parallel",), vmem_limit_bytes=VMEM_LIMIT),
        name="final_norm",
    )(h2, peer_out, gain)


def _pad_heads(w, dk):
    lead = w.shape[:-1]
    w = w.reshape(lead + (HEADS, dk))
    w = jnp.pad(w, [(0, 0)] * len(lead) + [(0, 0), (0, LANES - dk)])
    return w.reshape(lead + (HEADS * LANES,))


def _rot_heads(w, dk):
    lead = w.shape[:-1]
    w = w.reshape(lead + (HEADS, 2, dk // 2))
    w = jnp.stack([-w[..., 1, :], w[..., 0, :]], axis=-2)
    return w.reshape(lead + (HEADS * dk,))


def _pack_in_weights(w_in):
    gq, gk, gv, gg, ga, rq, rk, rv, rg = jnp.split(w_in, np.cumsum(IN_SPLITS)[:-1].tolist(), axis=-1)
    ga = jnp.pad(ga, ((0, 0), (0, 2 * LANES - GATE_RANK)))
    return jnp.concatenate(
        [_pad_heads(gq, DK), _pad_heads(gk, DK), gv, gg,
         _pad_heads(rq, DK), _pad_heads(rk, DK), rv, rg,
         _pad_heads(_rot_heads(rq, DK), DK), _pad_heads(_rot_heads(rk, DK), DK), ga],
        axis=-1).astype(BF16)


def _rope_tables(pos):
    inv_freq = ROPE_BASE ** (-jnp.arange(0, DK, 2, dtype=F32) / DK)
    ang = pos[:, None] * inv_freq[None, :]
    full = lambda t: _pad_heads(jnp.tile(t, (1, 2 * HEADS)), DK)
    return full(jnp.cos(ang)), full(jnp.sin(ang))


def _segment_lengths(s, first):
    seg = s // SEGS_PER_SEQ
    lengths = [seg] * SEGS_PER_SEQ
    if first:
        head = [seg // 8, 3 * seg // 8, seg // 2]
        assert sum(head) == seg and all(n % SC_CALL_TOKENS == 0 for n in head)
        lengths = head + lengths[1:]
    return lengths


def kernel(x, meta_tokens, norm1_gain, w_in, gla_gate_w2, gla_gate_b, gla_norm_gain, ret_norm_gain, ret_norm_bias, w_out, norm2_gain, peer_w_q, peer_subkeys, peer_u, peer_v, final_gain):
    b, s, d = x.shape
    assert d == D_MODEL and (s // SEGS_PER_SEQ) % SC_CALL_TOKENS == 0
    assert norm1_gain.shape[0] == 1, "single-layer block"
    x2d = x.reshape(b * s, d)

    w_all = _pack_in_weights(w_in[0])
    w2p = jnp.pad(_pad_heads(gla_gate_w2[0], DK), ((0, LANES - GATE_RANK), (0, 0))).astype(BF16)
    gbp = _pad_heads(gla_gate_b[0][None, :], DK)
    g1 = norm1_gain[0][None, :]
    ggain = gla_norm_gain[0][None, :]
    rgain = ret_norm_gain[0][None, :]
    rbias = ret_norm_bias[0][None, :]

    h0 = jnp.concatenate([jnp.zeros((N_PAD, d), x.dtype), meta_tokens.astype(x.dtype)], axis=0)
    cos0, sin0 = _rope_tables(jnp.arange(CHUNK, dtype=F32) - N_PAD)
    cos1, sin1 = _rope_tables(jnp.arange(s, dtype=F32) + N_META)
    zero_state = jnp.zeros((HEADS, DV, LANES), F32)
    _, s_meta, r_meta = _mixer(_norm_proj(h0, g1, w_all), cos0, sin0, w2p, gbp, ggain, rgain, rbias,
                               zero_state, zero_state, batch=1, n_invalid=N_PAD)

    w_out_b = w_out[0].astype(BF16)
    g2 = norm2_gain[0][None, :]
    w_q = peer_w_q[0].astype(BF16)
    subkeys = peer_subkeys[0].reshape(2 * PEER_HEADS, PEER_NKEYS, PEER_DKEY // 2).astype(BF16)
    u3 = _pack_table(peer_u[0])
    v3 = _pack_table(peer_v[0])
    gf = final_gain[None, :]

    outs = []
    prev_g = None
    pending = []
    for bi in range(b):
        s_state, r_state = s_meta, r_meta
        start = 0
        for seg in _segment_lengths(s, first=bi == 0):
            xs = x2d[bi * s + start:bi * s + start + seg]
            pos = slice(start, start + seg)
            start += seg
            if prev_g is not None:
                xs, _ = lax.optimization_barrier((xs, prev_g))
            o, s_state, r_state = _mixer(_norm_proj(xs, g1, w_all), cos1[pos], sin1[pos], w2p, gbp,
                                         ggain, rgain, rbias, s_state, r_state, batch=1, n_invalid=0)
            h2, xn2, xn2_pairs = _out_proj(o, xs, w_out_b, g2)
            idx, g = _route(xn2, w_q, subkeys)
            prev_g = g
            idx_flat = idx.reshape(-1)
            if len(pending) >= SC_LOOKAHEAD:
                h2_old, peer_old = pending.pop(0)
                idx_flat, peer_old = lax.optimization_barrier((idx_flat, peer_old))
                outs.append(_final(h2_old, peer_old, gf))
            act = _peer_dot_sc(u3, idx_flat, xn2_pairs)
            pending.append((h2, _peer_combine_sc(v3, idx_flat, _coef(act, g))))
    outs.extend(_final(h2_old, peer_old, gf) for h2_old, peer_old in pending)
    return jnp.concatenate(outs, axis=0).reshape(b, s, d)
```

```python
import functools
import math

import jax
import jax.numpy as jnp
import numpy as np
from jax import lax
from jax.experimental import pallas as pl
from jax.experimental.pallas import tpu as pltpu
from jax.experimental.pallas import tpu_sc as plsc

F32 = jnp.float32
BF16 = jnp.bfloat16

D_MODEL = 1024
N_META = 16
CHUNK = 64
N_PAD = CHUNK - N_META
EPS = 1e-6

HEADS = 4
DV = 128
DK = 64
LANES = 128
GATE_RANK = 16
GATE_TAU = 16.0
ROPE_BASE = 10000.0
IN_SPLITS = (HEADS * DK, HEADS * DK, HEADS * DV, HEADS * DV, GATE_RANK,
             HEADS * DK, HEADS * DK, HEADS * DV, HEADS * DV)

PEER_HEADS = 8
PEER_NKEYS = 128
PEER_DKEY = 256
PEER_TOPK = 16
PEER_SEL = PEER_HEADS * PEER_TOPK

HW = HEADS * LANES
OFF_GQ, OFF_GK, OFF_GV, OFF_GG = 0, HW, 2 * HW, 3 * HW
OFF_RQ, OFF_RK, OFF_RV, OFF_RG = 4 * HW, 5 * HW, 6 * HW, 7 * HW
OFF_RQR, OFF_RKR, OFF_GA = 8 * HW, 9 * HW, 10 * HW
PROJ_W = 10 * HW + 2 * LANES
PROJ_TN = PROJ_W // 3

LOG_GAMMA = tuple(math.log(1.0 - 2.0 ** (-5.0 - h)) for h in range(HEADS))

SC_CORES, SC_SUBCORES, SC_LANES = 2, 16, 16
SC_WORKERS = SC_CORES * SC_SUBCORES
SC_TOK_BATCH = 16
SC_ROWS = 16
SC_NBUF = 8
SC_GATHERS = SC_TOK_BATCH * PEER_SEL // SC_ROWS
HALF = D_MODEL // 2
ROW_TILES = HALF // LANES

SC_CALL_TOKENS = SC_WORKERS * SC_TOK_BATCH
SEGS_PER_SEQ = 2
SC_LOOKAHEAD = 2

VMEM_LIMIT = 48 * 1024 * 1024


def _row_block(n, target):
    blk = math.gcd(n, target)
    assert blk % 8 == 0 or blk == n, (n, target)
    return blk


def _dot(a, b, **kw):
    return jnp.dot(a, b, preferred_element_type=F32, **kw)


def _dot_nt(a, b):
    return lax.dot_general(a, b, (((1,), (1,)), ((), ())), preferred_element_type=F32)


def _dot_tn(a, b):
    return lax.dot_general(a, b, (((0,), (0,)), ((), ())), preferred_element_type=F32)


def _rms(x, g):
    return x * lax.rsqrt(jnp.mean(x * x, axis=-1, keepdims=True) + EPS) * g


def _silu(x):
    return x * jax.nn.sigmoid(x)


def _mixer_kernel(x_ref, g1_ref, wall_ref, cos_ref, sin_ref, w2_ref, gb_ref, ggain_ref, rgain_ref,
                  rbias_ref, s0_ref, r0_ref, o_ref, s_out_ref, r_out_ref, s_scr, r_scr, proj_ref,
                  *, n_invalid, rows_blk):
    j = pl.program_id(1)
    xn = _rms(x_ref[...], g1_ref[...]).astype(BF16)
    for jn in range(PROJ_W // PROJ_TN):
        cols = slice(jn * PROJ_TN, (jn + 1) * PROJ_TN)
        proj_ref[:, cols] = _dot(xn, wall_ref[:, cols])

    @pl.when(j == 0)
    def _():
        s_scr[...] = s0_ref[...]
        r_scr[...] = r0_ref[...]

    row = lax.broadcasted_iota(jnp.int32, (CHUNK, CHUNK), 0)
    col = lax.broadcasted_iota(jnp.int32, (CHUNK, CHUNK), 1)
    causal = row >= col
    tril = causal.astype(F32)
    diff = jnp.maximum(row - col, 0).astype(F32)
    rowf = lax.broadcasted_iota(jnp.int32, (CHUNK, LANES), 0).astype(F32)
    row1 = lax.broadcasted_iota(jnp.int32, (CHUNK, 1), 0)

    def chunk_body(c, carry):
        r0 = pl.multiple_of(c * CHUNK, CHUNK)
        rows = pl.ds(r0, CHUNK)
        valid = ((j * rows_blk + r0 + row1) >= n_invalid).astype(F32)

        ga = proj_ref[rows, OFF_GA:OFF_GA + LANES].astype(BF16)
        pre = _dot(ga, w2_ref[...]) + gb_ref[...]
        log_a = jax.nn.log_sigmoid(pre) * (1.0 / GATE_TAU)
        bcum = _dot(tril, log_a, precision=lax.Precision.HIGHEST)
        b_last = bcum[CHUNK - 1:CHUNK, :]
        gq = proj_ref[rows, OFF_GQ:OFF_GQ + HW] * (DK ** -0.5)
        gk = proj_ref[rows, OFF_GK:OFF_GK + HW] * valid
        q_t = (gq * jnp.exp(bcum)).astype(BF16)
        k_t = (gk * jnp.exp(-bcum)).astype(BF16)
        k_end = (gk * jnp.exp(b_last - bcum)).astype(BF16)
        decay = jnp.exp(b_last)
        for h in range(HEADS):
            sl = slice(h * LANES, (h + 1) * LANES)
            v = proj_ref[rows, OFF_GV + h * DV:OFF_GV + (h + 1) * DV].astype(BF16)
            att = jnp.where(causal, _dot_nt(q_t[:, sl], k_t[:, sl]), 0.0).astype(BF16)
            s_prev = s_scr[h]
            o = _dot(att, v) + _dot_nt(q_t[:, sl], s_prev.astype(BF16))
            s_scr[h] = s_prev * decay[:, sl] + _dot_tn(v, k_end[:, sl])
            o = o * lax.rsqrt(jnp.mean(o * o, axis=-1, keepdims=True) + EPS) * ggain_ref[...]
            gate = proj_ref[rows, OFF_GG + h * DV:OFF_GG + (h + 1) * DV]
            o_ref[rows, h * DV:(h + 1) * DV] = (o * _silu(gate)).astype(o_ref.dtype)

        cosb = cos_ref[rows, :]
        sinb = sin_ref[rows, :]
        rq = proj_ref[rows, OFF_RQ:OFF_RQ + HW]
        rqr = proj_ref[rows, OFF_RQR:OFF_RQR + HW]
        rk = proj_ref[rows, OFF_RK:OFF_RK + HW] * valid
        rkr = proj_ref[rows, OFF_RKR:OFF_RKR + HW] * valid
        q_r = (rq * cosb + rqr * sinb).astype(BF16)
        k_r = (rk * cosb + rkr * sinb) * (DK ** -0.5)
        for h in range(HEADS):
            sl = slice(h * LANES, (h + 1) * LANES)
            lg = LOG_GAMMA[h]
            dmat = jnp.where(causal, jnp.exp(lg * diff), 0.0)
            xi = jnp.exp(lg * (rowf + 1.0))
            zeta = jnp.exp(lg * (CHUNK - 1.0 - rowf))
            v = proj_ref[rows, OFF_RV + h * DV:OFF_RV + (h + 1) * DV].astype(BF16)
            k_h = k_r[:, sl]
            att = (_dot_nt(q_r[:, sl], k_h.astype(BF16)) * dmat).astype(BF16)
            r_prev = r_scr[h]
            o = _dot(att, v) + _dot_nt(q_r[:, sl], r_prev.astype(BF16)) * xi
            r_scr[h] = r_prev * math.exp(lg * CHUNK) + _dot_tn(v, (k_h * zeta).astype(BF16))
            mu = jnp.mean(o, axis=-1, keepdims=True)
            var = jnp.mean(jnp.square(o - mu), axis=-1, keepdims=True)
            o = (o - mu) * lax.rsqrt(var + EPS) * rgain_ref[...] + rbias_ref[...]
            gate = proj_ref[rows, OFF_RG + h * DV:OFF_RG + (h + 1) * DV]
            o_ref[rows, HW + h * DV:HW + (h + 1) * DV] = (o * _silu(gate)).astype(o_ref.dtype)
        return carry

    lax.fori_loop(0, rows_blk // CHUNK, chunk_body, 0)
    s_out_ref[...] = s_scr[...]
    r_out_ref[...] = r_scr[...]


def _mixer(x2d, g1, w_all, cosf, sinf, w2p, gbp, ggain, rgain, rbias, s0, r0, *, batch, n_invalid):
    rows = x2d.shape[0] // batch
    rows_blk = _row_block(rows, 256)
    nblk = rows // rows_blk
    const2 = lambda b, j: (0, 0)
    const3 = lambda b, j: (0, 0, 0)
    state = jax.ShapeDtypeStruct((HEADS, DV, LANES), F32)
    return pl.pallas_call(
        functools.partial(_mixer_kernel, n_invalid=n_invalid, rows_blk=rows_blk),
        grid=(batch, nblk),
        in_specs=[
            pl.BlockSpec((rows_blk, D_MODEL), lambda b, j: (b * nblk + j, 0)),
            pl.BlockSpec((1, D_MODEL), const2),
            pl.BlockSpec((D_MODEL, PROJ_W), const2),
            pl.BlockSpec((rows_blk, HW), lambda b, j: (j, 0)),
            pl.BlockSpec((rows_blk, HW), lambda b, j: (j, 0)),
            pl.BlockSpec((LANES, HW), const2),
            pl.BlockSpec((1, HW), const2),
            pl.BlockSpec((1, DV), const2),
            pl.BlockSpec((1, DV), const2),
            pl.BlockSpec((1, DV), const2),
            pl.BlockSpec((HEADS, DV, LANES), const3),
            pl.BlockSpec((HEADS, DV, LANES), const3),
        ],
        out_specs=[
            pl.BlockSpec((rows_blk, D_MODEL), lambda b, j: (b * nblk + j, 0)),
            pl.BlockSpec((HEADS, DV, LANES), const3),
            pl.BlockSpec((HEADS, DV, LANES), const3),
        ],
        out_shape=[jax.ShapeDtypeStruct((batch * rows, D_MODEL), BF16), state, state],
        scratch_shapes=[pltpu.VMEM((HEADS, DV, LANES), F32), pltpu.VMEM((HEADS, DV, LANES), F32),
                        pltpu.VMEM((rows_blk, PROJ_W), F32)],
        compiler_params=pltpu.CompilerParams(
            dimension_semantics=("arbitrary", "arbitrary"), vmem_limit_bytes=VMEM_LIMIT),
        name="token_mixers",
    )(x2d, g1, w_all, cosf, sinf, w2p, gbp, ggain, rgain, rbias, s0, r0)


def _out_proj_kernel(o_ref, x_ref, w_ref, g_ref, h_ref, xn_ref, xp_ref):
    h = x_ref[...] + _dot(o_ref[...], w_ref[...])
    h_ref[...] = h
    xn = _rms(h, g_ref[...])
    xn_ref[...] = xn
    xp_ref[...] = _pack_pairs(xn[:, :HALF], xn[:, HALF:])


def _out_proj(o, x2d, w_out, gain2):
    n = x2d.shape[0]
    tm = _row_block(n, 512)
    blk = pl.BlockSpec((tm, D_MODEL), lambda i: (i, 0))
    return pl.pallas_call(
        _out_proj_kernel,
        grid=(n // tm,),
        in_specs=[blk, blk, pl.BlockSpec((D_MODEL, D_MODEL), lambda i: (0, 0)),
                  pl.BlockSpec((1, D_MODEL), lambda i: (0, 0))],
        out_specs=[blk, blk, pl.BlockSpec((tm, HALF), lambda i: (i, 0))],
        out_shape=[jax.ShapeDtypeStruct((n, D_MODEL), F32)] * 2 + [jax.ShapeDtypeStruct((n, HALF), jnp.uint32)],
        compiler_params=pltpu.CompilerParams(
            dimension_semantics=("parallel",), vmem_limit_bytes=VMEM_LIMIT),
        name="out_proj_norm",
    )(o, x2d, w_out, gain2)


def _top_k_rows(arrays, payloads=None):
    arrays = list(arrays)
    n = arrays[0].shape[0]
    rowi = lax.broadcasted_iota(jnp.int32, arrays[0].shape, 0)
    vals = [[] for _ in arrays]
    picks = [[] for _ in arrays]
    for _ in range(PEER_TOPK):
        for k, s in enumerate(arrays):
            m = jnp.max(s, axis=0, keepdims=True)
            am = jnp.min(jnp.where(s == m, rowi, n), axis=0, keepdims=True)
            sel = rowi == am
            vals[k].append(m)
            if payloads is None:
                picks[k].append(am)
            else:
                picks[k].append(jnp.min(jnp.where(sel, payloads[k], jnp.iinfo(jnp.int32).max),
                                        axis=0, keepdims=True))
            arrays[k] = jnp.where(sel, -jnp.inf, s)
    return [(jnp.concatenate(v, axis=0), jnp.concatenate(p, axis=0)) for v, p in zip(vals, picks)]


def _candidate_cells():
    cells = [(a, b) for a in range(PEER_TOPK) for b in range(PEER_TOPK) if (a + 1) * (b + 1) <= PEER_TOPK]
    n = -(-len(cells) // 8) * 8
    sel_a = np.zeros((n, PEER_TOPK), np.float32)
    sel_b = np.zeros((n, PEER_TOPK), np.float32)
    pad = np.full((n, 1), -np.inf, np.float32)
    for r, (a, b) in enumerate(cells):
        sel_a[r, a] = 1.0
        sel_b[r, b] = 1.0
        pad[r, 0] = 0.0
    return sel_a, sel_b, pad


def _route_kernel(xn_ref, wq_ref, sk_ref, sa_ref, sb_ref, pad_ref, idx_ref, g_ref, q_scr, i_scr, g_scr):
    tr = xn_ref.shape[0]
    q = _dot(xn_ref[...].astype(BF16), wq_ref[...])
    for hc in range(2 * PEER_HEADS):
        q_scr[hc] = q[:, hc * LANES:(hc + 1) * LANES].astype(BF16)

    def sub_body(sb, carry):
        rows = pl.ds(pl.multiple_of(sb * LANES, LANES), LANES)

        def head_pair(hp, c2):
            heads = [2 * hp, 2 * hp + 1]
            scores = [_dot_nt(sk_ref[2 * h + c], q_scr[2 * h + c, rows, :])
                      for h in heads for c in range(2)]
            first = _top_k_rows(scores)
            sa, sb = sa_ref[...], sb_ref[...]
            cands, cidxs = [], []
            for k in range(len(heads)):
                (v0, i0), (v1, i1) = first[2 * k], first[2 * k + 1]
                cands.append((_dot(sa, v0, precision=lax.Precision.HIGHEST)
                              + _dot(sb, v1, precision=lax.Precision.HIGHEST)) + pad_ref[...])
                cidxs.append((_dot(sa, i0.astype(F32)) * PEER_NKEYS
                              + _dot(sb, i1.astype(F32))).astype(jnp.int32))
            for h, (best, e) in zip(heads, _top_k_rows(cands, cidxs)):
                ex = jnp.exp(best - best[0:1])
                i_scr[h] = e
                g_scr[h] = ex / jnp.sum(ex, axis=0, keepdims=True)
            return c2

        lax.fori_loop(0, PEER_HEADS // 2, head_pair, 0)
        idx_ref[rows, :] = i_scr[...].reshape(PEER_SEL, LANES).T
        g_ref[rows, :] = g_scr[...].reshape(PEER_SEL, LANES).T
        return carry

    lax.fori_loop(0, tr // LANES, sub_body, 0)


def _route(xn2, w_q, subkeys):
    n = xn2.shape[0]
    sel_a, sel_b, pad = _candidate_cells()
    tr = _row_block(n, 512)
    qw = 2 * PEER_HEADS * LANES
    return pl.pallas_call(
        _route_kernel,
        grid=(n // tr,),
        in_specs=[pl.BlockSpec((tr, D_MODEL), lambda i: (i, 0)),
                  pl.BlockSpec((D_MODEL, qw), lambda i: (0, 0)),
                  pl.BlockSpec((2 * PEER_HEADS, PEER_NKEYS, LANES), lambda i: (0, 0, 0)),
                  pl.BlockSpec(sel_a.shape, lambda i: (0, 0)),
                  pl.BlockSpec(sel_b.shape, lambda i: (0, 0)),
                  pl.BlockSpec(pad.shape, lambda i: (0, 0))],
        out_specs=[pl.BlockSpec((tr, PEER_SEL), lambda i: (i, 0))] * 2,
        out_shape=[jax.ShapeDtypeStruct((n, PEER_SEL), jnp.int32),
                   jax.ShapeDtypeStruct((n, PEER_SEL), F32)],
        scratch_shapes=[pltpu.VMEM((2 * PEER_HEADS, tr, LANES), BF16),
                        pltpu.VMEM((PEER_HEADS, PEER_TOPK, LANES), jnp.int32),
                        pltpu.VMEM((PEER_HEADS, PEER_TOPK, LANES), F32)],
        compiler_params=pltpu.CompilerParams(
            dimension_semantics=("parallel",), vmem_limit_bytes=VMEM_LIMIT),
        name="peer_route",
    )(xn2, w_q, subkeys, sel_a, sel_b, pad)


def _sc_worker_id():
    return lax.axis_index("s") * SC_CORES + lax.axis_index("c")


def _sc_mesh():
    return plsc.VectorSubcoreMesh(core_axis_name="c", subcore_axis_name="s")


def _sc_gather_ring(table_hbm, idx_v, bufs, sems, unit, compute):
    def gather(g, k):
        off = pl.multiple_of(g * SC_ROWS, SC_ROWS)
        return pltpu.make_async_copy(table_hbm.at[idx_v.at[pl.ds(off, SC_ROWS)]], bufs[k], sems[k])

    for k in range(SC_NBUF):
        gather(k, k).start()

    units_per_iter = SC_NBUF // unit

    def ring_iter(i, carry):
        for m in range(units_per_iter):
            ks = range(m * unit, (m + 1) * unit)
            for k in ks:
                gather(i * SC_NBUF + k, k).wait()
            compute(i * units_per_iter + m, [bufs[k] for k in ks])
            for k in ks:
                g_next = (i + 1) * SC_NBUF + k

                @pl.when(g_next < SC_GATHERS)
                def _():
                    gather(g_next, k).start()
        return carry

    lax.fori_loop(0, SC_GATHERS // SC_NBUF, ring_iter, 0)


def _bf16_bits(x):
    b = lax.bitcast_convert_type(x, jnp.uint32)
    return (b + jnp.uint32(0x7FFF) + ((b >> 16) & jnp.uint32(1))) >> 16


def _pack_pairs(lo, hi):
    return _bf16_bits(lo) | (_bf16_bits(hi) << 16)


def _pack_table(t):
    return _pack_pairs(t[:, :HALF], t[:, HALF:]).reshape(t.shape[0], ROW_TILES, LANES)


def _mul_words(w, m):
    p = plsc.bitcast(plsc.bitcast(w, jnp.bfloat16) * plsc.bitcast(m, jnp.bfloat16), jnp.uint32)
    return lax.bitcast_convert_type(p << 16, F32), lax.bitcast_convert_type(p, F32)


def _sc_scratch(staged_cols, out_cols):
    return ([pltpu.VMEM((SC_TOK_BATCH * PEER_SEL,), jnp.int32),
             pltpu.VMEM((SC_TOK_BATCH, staged_cols), jnp.uint32),
             pltpu.VMEM((SC_TOK_BATCH, out_cols), F32)]
            + [pltpu.VMEM((SC_ROWS, ROW_TILES, LANES), jnp.uint32) for _ in range(SC_NBUF)]
            + [pltpu.SemaphoreType.DMA for _ in range(SC_NBUF)])


def _sc_batches(tpw, idx_hbm, staged_hbm, out_hbm, idx_v, staged_v, out_v, run):
    tok0 = _sc_worker_id() * tpw

    def batch(tb, carry):
        t0 = pl.multiple_of(tok0 + tb * SC_TOK_BATCH, SC_TOK_BATCH)
        pltpu.sync_copy(idx_hbm.at[pl.ds(pl.multiple_of(t0 * PEER_SEL, SC_TOK_BATCH * PEER_SEL),
                                         SC_TOK_BATCH * PEER_SEL)], idx_v)
        pltpu.sync_copy(staged_hbm.at[pl.ds(t0, SC_TOK_BATCH)], staged_v)
        run()
        pltpu.sync_copy(out_v, out_hbm.at[pl.ds(t0, SC_TOK_BATCH)])
        return carry

    lax.fori_loop(0, tpw // SC_TOK_BATCH, batch, 0)


def _peer_dot_sc(u3, idx_flat, xn2):
    t = xn2.shape[0]
    unit = 16 // SC_ROWS
    units_per_tok = PEER_SEL // 16

    @functools.partial(
        pl.kernel,
        mesh=_sc_mesh(),
        out_type=jax.ShapeDtypeStruct((t, PEER_SEL), F32),
        scratch_types=[pltpu.VMEM((16 * SC_LANES,), F32)] + _sc_scratch(HALF, PEER_SEL),
        compiler_params=pltpu.CompilerParams(needs_layout_passes=False),
        name="peer_dot",
    )
    def body(u_hbm, idx_hbm, x_hbm, out_hbm, red_v, idx_v, x_v, out_v, *rest):
        bufs, sems = rest[:SC_NBUF], rest[SC_NBUF:]

        def compute(u, ub):
            tl = u // units_per_tok
            rb = (u % units_per_tok) * 16

            def cbody(c, acc):
                acc = list(acc)
                c4 = c // (LANES // SC_LANES)
                lane0 = pl.multiple_of((c % (LANES // SC_LANES)) * SC_LANES, SC_LANES)
                col = pl.multiple_of(c * SC_LANES, SC_LANES)
                xw = x_v[tl, pl.ds(col, SC_LANES)]
                for q, buf in enumerate(ub):
                    for r in range(SC_ROWS):
                        a = q * SC_ROWS + r
                        lo, hi = _mul_words(buf[r, c4, pl.ds(lane0, SC_LANES)], xw)
                        acc[a] = acc[a] + lo + hi
                return tuple(acc)

            acc = lax.fori_loop(0, HALF // SC_LANES, cbody,
                                tuple(jnp.zeros((SC_LANES,), F32) for _ in range(16)))
            for a in range(16):
                red_v[pl.ds(a * SC_LANES, SC_LANES)] = acc[a]
            lane_base = lax.iota(jnp.int32, SC_LANES) * SC_LANES
            tot = plsc.load_gather(red_v, [lane_base])
            for k in range(1, SC_LANES):
                tot = tot + plsc.load_gather(red_v, [lane_base + k])
            out_v[tl, pl.ds(pl.multiple_of(rb, 16), 16)] = tot

        _sc_batches(t // SC_WORKERS, idx_hbm, x_hbm, out_hbm, idx_v, x_v, out_v,
                    lambda: _sc_gather_ring(u_hbm, idx_v, bufs, sems, unit, compute))

    return body(u3, idx_flat, xn2)


def _peer_combine_sc(v3, idx_flat, coef_b):
    t = coef_b.shape[0]
    unit = 32 // SC_ROWS
    units_per_tok = PEER_SEL // 32

    @functools.partial(
        pl.kernel,
        mesh=_sc_mesh(),
        out_type=jax.ShapeDtypeStruct((t, D_MODEL), F32),
        scratch_types=_sc_scratch(PEER_SEL * SC_LANES, D_MODEL),
        compiler_params=pltpu.CompilerParams(needs_layout_passes=False),
        name="peer_combine",
    )
    def body(v_hbm, idx_hbm, coef_hbm, out_hbm, idx_v, coef_v, out_v, *rest):
        bufs, sems = rest[:SC_NBUF], rest[SC_NBUF:]

        def compute(u, ub):
            tl = u // units_per_tok
            rb = (u % units_per_tok) * 32
            for g in range(ROW_TILES):
                cols = ([pl.ds(g * LANES + l * SC_LANES, SC_LANES) for l in range(8)]
                        + [pl.ds(HALF + g * LANES + l * SC_LANES, SC_LANES) for l in range(8)])

                def rbody(r, acc):
                    acc = list(acc)
                    for q, buf in enumerate(ub):
                        co = pl.multiple_of((rb + q * SC_ROWS + r) * SC_LANES, SC_LANES)
                        cf = coef_v[tl, pl.ds(co, SC_LANES)]
                        for l in range(8):
                            lo, hi = _mul_words(buf[r, g, pl.ds(l * SC_LANES, SC_LANES)], cf)
                            acc[l] = acc[l] + lo
                            acc[8 + l] = acc[8 + l] + hi
                    return tuple(acc)

                acc = lax.fori_loop(0, SC_ROWS, rbody, tuple(out_v[tl, c] for c in cols))
                for k in range(16):
                    out_v[tl, cols[k]] = acc[k]

        def run():
            def zero(tl, c2):
                for k in range(D_MODEL // SC_LANES):
                    out_v[tl, pl.ds(k * SC_LANES, SC_LANES)] = jnp.zeros((SC_LANES,), F32)
                return c2

            lax.fori_loop(0, SC_TOK_BATCH, zero, 0)
            _sc_gather_ring(v_hbm, idx_v, bufs, sems, unit, compute)

        _sc_batches(t // SC_WORKERS, idx_hbm, coef_hbm, out_hbm, idx_v, coef_v, out_v, run)

    return body(v3, idx_flat, coef_b)


def _coef_kernel(act_ref, g_ref, e_ref, coef_ref):
    act = act_ref[...]
    gelu = 0.5 * act * (1.0 + lax.erf(act * np.float32(math.sqrt(0.5))))
    c = _dot(g_ref[...] * gelu, e_ref[...], precision=lax.Precision.HIGHEST)
    coef_ref[...] = _pack_pairs(c, c)


def _coef(act, g):
    n = act.shape[0]
    tr = _row_block(n, 512)
    w = PEER_SEL * SC_LANES
    expand = (jnp.arange(w, dtype=jnp.int32)[None, :] // SC_LANES
              == jnp.arange(PEER_SEL, dtype=jnp.int32)[:, None]).astype(F32)
    blk = pl.BlockSpec((tr, PEER_SEL), lambda i: (i, 0))
    return pl.pallas_call(
        _coef_kernel,
        grid=(n // tr,),
        in_specs=[blk, blk, pl.BlockSpec((PEER_SEL, w), lambda i: (0, 0))],
        out_specs=pl.BlockSpec((tr, w), lambda i: (i, 0)),
        out_shape=jax.ShapeDtypeStruct((n, w), jnp.uint32),
        compiler_params=pltpu.CompilerParams(
            dimension_semantics=("parallel",), vmem_limit_bytes=VMEM_LIMIT),
        name="peer_coef",
    )(act, g, expand)


def _final_kernel(h_ref, p_ref, g_ref, o_ref):
    o_ref[...] = _rms(h_ref[...] + p_ref[...], g_ref[...])


def _final(h2, peer_out, gain):
    n = h2.shape[0]
    tm = _row_block(n, 1024)
    blk = pl.BlockSpec((tm, D_MODEL), lambda i: (i, 0))
    return pl.pallas_call(
        _final_kernel,
        grid=(n // tm,),
        in_specs=[blk, blk, pl.BlockSpec((1, D_MODEL), lambda i: (0, 0))],
        out_specs=blk,
        out_shape=jax.ShapeDtypeStruct((n, D_MODEL), F32),
        compiler_params=pltpu.CompilerParams(
            dimension_semantics=("parallel",), vmem_limit_bytes=VMEM_LIMIT),
        name="final_norm",
    )(h2, peer_out, gain)


def _pad_heads(w, dk):
    lead = w.shape[:-1]
    w = w.reshape(lead + (HEADS, dk))
    w = jnp.pad(w, [(0, 0)] * len(lead) + [(0, 0), (0, LANES - dk)])
    return w.reshape(lead + (HEADS * LANES,))


def _rot_heads(w, dk):
    lead = w.shape[:-1]
    w = w.reshape(lead + (HEADS, 2, dk // 2))
    w = jnp.stack([-w[..., 1, :], w[..., 0, :]], axis=-2)
    return w.reshape(lead + (HEADS * dk,))


def _pack_in_weights(w_in):
    gq, gk, gv, gg, ga, rq, rk, rv, rg = jnp.split(w_in, np.cumsum(IN_SPLITS)[:-1].tolist(), axis=-1)
    ga = jnp.pad(ga, ((0, 0), (0, 2 * LANES - GATE_RANK)))
    return jnp.concatenate(
        [_pad_heads(gq, DK), _pad_heads(gk, DK), gv, gg,
         _pad_heads(rq, DK), _pad_heads(rk, DK), rv, rg,
         _pad_heads(_rot_heads(rq, DK), DK), _pad_heads(_rot_heads(rk, DK), DK), ga],
        axis=-1).astype(BF16)


def _rope_tables(pos):
    inv_freq = ROPE_BASE ** (-jnp.arange(0, DK, 2, dtype=F32) / DK)
    ang = pos[:, None] * inv_freq[None, :]
    full = lambda t: _pad_heads(jnp.tile(t, (1, 2 * HEADS)), DK)
    return full(jnp.cos(ang)), full(jnp.sin(ang))


def _segment_lengths(s, first):
    seg = s // SEGS_PER_SEQ
    lengths = [seg] * SEGS_PER_SEQ
    if first:
        head = [seg // 8, 3 * seg // 8, seg // 2]
        assert sum(head) == seg and all(n % SC_CALL_TOKENS == 0 for n in head)
        lengths = head + lengths[1:]
    return lengths


def kernel(x, meta_tokens, norm1_gain, w_in, gla_gate_w2, gla_gate_b, gla_norm_gain, ret_norm_gain, ret_norm_bias, w_out, norm2_gain, peer_w_q, peer_subkeys, peer_u, peer_v, final_gain):
    b, s, d = x.shape
    assert d == D_MODEL and (s // SEGS_PER_SEQ) % SC_CALL_TOKENS == 0
    assert norm1_gain.shape[0] == 1, "single-layer block"
    x2d = x.reshape(b * s, d)

    w_all = _pack_in_weights(w_in[0])
    w2p = jnp.pad(_pad_heads(gla_gate_w2[0], DK), ((0, LANES - GATE_RANK), (0, 0))).astype(BF16)
    gbp = _pad_heads(gla_gate_b[0][None, :], DK)
    g1 = norm1_gain[0][None, :]
    ggain = gla_norm_gain[0][None, :]
    rgain = ret_norm_gain[0][None, :]
    rbias = ret_norm_bias[0][None, :]

    h0 = jnp.concatenate([jnp.zeros((N_PAD, d), x.dtype), meta_tokens.astype(x.dtype)], axis=0)
    cos0, sin0 = _rope_tables(jnp.arange(CHUNK, dtype=F32) - N_PAD)
    cos1, sin1 = _rope_tables(jnp.arange(s, dtype=F32) + N_META)
    zero_state = jnp.zeros((HEADS, DV, LANES), F32)
    _, s_meta, r_meta = _mixer(h0, g1, w_all, cos0, sin0, w2p, gbp, ggain, rgain, rbias,
                               zero_state, zero_state, batch=1, n_invalid=N_PAD)

    w_out_b = w_out[0].astype(BF16)
    g2 = norm2_gain[0][None, :]
    w_q = peer_w_q[0].astype(BF16)
    subkeys = peer_subkeys[0].reshape(2 * PEER_HEADS, PEER_NKEYS, PEER_DKEY // 2).astype(BF16)
    u3 = _pack_table(peer_u[0])
    v3 = _pack_table(peer_v[0])
    gf = final_gain[None, :]

    outs = []
    prev_g = None
    pending = []
    for bi in range(b):
        s_state, r_state = s_meta, r_meta
        start = 0
        for seg in _segment_lengths(s, first=bi == 0):
            xs = x2d[bi * s + start:bi * s + start + seg]
            pos = slice(start, start + seg)
            start += seg
            if prev_g is not None:
                xs, _ = lax.optimization_barrier((xs, prev_g))
            o, s_state, r_state = _mixer(xs, g1, w_all, cos1[pos], sin1[pos], w2p, gbp,
                                         ggain, rgain, rbias, s_state, r_state, batch=1, n_invalid=0)
            h2, xn2, xn2_pairs = _out_proj(o, xs, w_out_b, g2)
            idx, g = _route(xn2, w_q, subkeys)
            prev_g = g
            idx_flat = idx.reshape(-1)
            if len(pending) >= SC_LOOKAHEAD:
                h2_old, peer_old = pending.pop(0)
                idx_flat, peer_old = lax.optimization_barrier((idx_flat, peer_old))
                outs.append(_final(h2_old, peer_old, gf))
            act = _peer_dot_sc(u3, idx_flat, xn2_pairs)
            pending.append((h2, _peer_combine_sc(v3, idx_flat, _coef(act, g))))
    outs.extend(_final(h2_old, peer_old, gf) for h2_old, peer_old in pending)
    return jnp.concatenate(outs, axis=0).reshape(b, s, d)
```

```python
import functools
import math

import jax
import jax.numpy as jnp
import numpy as np
from jax import lax
from jax.experimental import pallas as pl
from jax.experimental.pallas import tpu as pltpu
from jax.experimental.pallas import tpu_sc as plsc

F32 = jnp.float32
BF16 = jnp.bfloat16

D_MODEL = 1024
N_META = 16
CHUNK = 64
N_PAD = CHUNK - N_META
EPS = 1e-6

HEADS = 4
DV = 128
DK = 64
LANES = 128
GATE_RANK = 16
GATE_TAU = 16.0
ROPE_BASE = 10000.0
IN_SPLITS = (HEADS * DK, HEADS * DK, HEADS * DV, HEADS * DV, GATE_RANK,
             HEADS * DK, HEADS * DK, HEADS * DV, HEADS * DV)

PEER_HEADS = 8
PEER_NKEYS = 128
PEER_DKEY = 256
PEER_TOPK = 16
PEER_SEL = PEER_HEADS * PEER_TOPK

HW = HEADS * LANES
OFF_GQ, OFF_GK, OFF_GV, OFF_GG = 0, HW, 2 * HW, 3 * HW
OFF_RQ, OFF_RK, OFF_RV, OFF_RG = 4 * HW, 5 * HW, 6 * HW, 7 * HW
OFF_RQR, OFF_RKR, OFF_GA = 8 * HW, 9 * HW, 10 * HW
PROJ_W = 10 * HW + 2 * LANES
PROJ_TN = PROJ_W // 3

LOG_GAMMA = tuple(math.log(1.0 - 2.0 ** (-5.0 - h)) for h in range(HEADS))

SC_CORES, SC_SUBCORES, SC_LANES = 2, 16, 16
SC_WORKERS = SC_CORES * SC_SUBCORES
SC_TOK_BATCH = 16
SC_ROWS = 16
SC_NBUF = 8
SC_GATHERS = SC_TOK_BATCH * PEER_SEL // SC_ROWS
HALF = D_MODEL // 2
ROW_TILES = HALF // LANES

SC_CALL_TOKENS = SC_WORKERS * SC_TOK_BATCH
SEGS_PER_SEQ = 2
FIRST_SEQ_RAMP = (1, 1, 2, 3, 4, 5)
SC_LOOKAHEAD = 2

VMEM_LIMIT = 48 * 1024 * 1024


def _row_block(n, target):
    blk = math.gcd(n, target)
    assert blk % 8 == 0 or blk == n, (n, target)
    return blk


def _dot(a, b, **kw):
    return jnp.dot(a, b, preferred_element_type=F32, **kw)


def _dot_nt(a, b):
    return lax.dot_general(a, b, (((1,), (1,)), ((), ())), preferred_element_type=F32)


def _dot_tn(a, b):
    return lax.dot_general(a, b, (((0,), (0,)), ((), ())), preferred_element_type=F32)


def _rms(x, g):
    return x * lax.rsqrt(jnp.mean(x * x, axis=-1, keepdims=True) + EPS) * g


def _silu(x):
    return x * jax.nn.sigmoid(x)


def _mixer_kernel(x_ref, g1_ref, wall_ref, cos_ref, sin_ref, w2_ref, gb_ref, ggain_ref, rgain_ref,
                  rbias_ref, s0_ref, r0_ref, o_ref, s_out_ref, r_out_ref, s_scr, r_scr, proj_ref,
                  *, n_invalid, rows_blk):
    j = pl.program_id(1)
    xn = _rms(x_ref[...], g1_ref[...]).astype(BF16)
    for jn in range(PROJ_W // PROJ_TN):
        cols = slice(jn * PROJ_TN, (jn + 1) * PROJ_TN)
        proj_ref[:, cols] = _dot(xn, wall_ref[:, cols])

    @pl.when(j == 0)
    def _():
        s_scr[...] = s0_ref[...]
        r_scr[...] = r0_ref[...]

    row = lax.broadcasted_iota(jnp.int32, (CHUNK, CHUNK), 0)
    col = lax.broadcasted_iota(jnp.int32, (CHUNK, CHUNK), 1)
    causal = row >= col
    tril = causal.astype(F32)
    diff = jnp.maximum(row - col, 0).astype(F32)
    rowf = lax.broadcasted_iota(jnp.int32, (CHUNK, LANES), 0).astype(F32)
    row1 = lax.broadcasted_iota(jnp.int32, (CHUNK, 1), 0)

    def chunk_body(c, carry):
        r0 = pl.multiple_of(c * CHUNK, CHUNK)
        rows = pl.ds(r0, CHUNK)
        valid = ((j * rows_blk + r0 + row1) >= n_invalid).astype(F32)

        ga = proj_ref[rows, OFF_GA:OFF_GA + LANES].astype(BF16)
        pre = _dot(ga, w2_ref[...]) + gb_ref[...]
        log_a = jax.nn.log_sigmoid(pre) * (1.0 / GATE_TAU)
        bcum = _dot(tril, log_a, precision=lax.Precision.HIGHEST)
        b_last = bcum[CHUNK - 1:CHUNK, :]
        gq = proj_ref[rows, OFF_GQ:OFF_GQ + HW] * (DK ** -0.5)
        gk = proj_ref[rows, OFF_GK:OFF_GK + HW] * valid
        q_t = (gq * jnp.exp(bcum)).astype(BF16)
        k_t = (gk * jnp.exp(-bcum)).astype(BF16)
        k_end = (gk * jnp.exp(b_last - bcum)).astype(BF16)
        decay = jnp.exp(b_last)
        for h in range(HEADS):
            sl = slice(h * LANES, (h + 1) * LANES)
            v = proj_ref[rows, OFF_GV + h * DV:OFF_GV + (h + 1) * DV].astype(BF16)
            att = jnp.where(causal, _dot_nt(q_t[:, sl], k_t[:, sl]), 0.0).astype(BF16)
            s_prev = s_scr[h]
            o = _dot(att, v) + _dot_nt(q_t[:, sl], s_prev.astype(BF16))
            s_scr[h] = s_prev * decay[:, sl] + _dot_tn(v, k_end[:, sl])
            o = o * lax.rsqrt(jnp.mean(o * o, axis=-1, keepdims=True) + EPS) * ggain_ref[...]
            gate = proj_ref[rows, OFF_GG + h * DV:OFF_GG + (h + 1) * DV]
            o_ref[rows, h * DV:(h + 1) * DV] = (o * _silu(gate)).astype(o_ref.dtype)

        cosb = cos_ref[rows, :]
        sinb = sin_ref[rows, :]
        rq = proj_ref[rows, OFF_RQ:OFF_RQ + HW]
        rqr = proj_ref[rows, OFF_RQR:OFF_RQR + HW]
        rk = proj_ref[rows, OFF_RK:OFF_RK + HW] * valid
        rkr = proj_ref[rows, OFF_RKR:OFF_RKR + HW] * valid
        q_r = (rq * cosb + rqr * sinb).astype(BF16)
        k_r = (rk * cosb + rkr * sinb) * (DK ** -0.5)
        for h in range(HEADS):
            sl = slice(h * LANES, (h + 1) * LANES)
            lg = LOG_GAMMA[h]
            dmat = jnp.where(causal, jnp.exp(lg * diff), 0.0)
            xi = jnp.exp(lg * (rowf + 1.0))
            zeta = jnp.exp(lg * (CHUNK - 1.0 - rowf))
            v = proj_ref[rows, OFF_RV + h * DV:OFF_RV + (h + 1) * DV].astype(BF16)
            k_h = k_r[:, sl]
            att = (_dot_nt(q_r[:, sl], k_h.astype(BF16)) * dmat).astype(BF16)
            r_prev = r_scr[h]
            o = _dot(att, v) + _dot_nt(q_r[:, sl], r_prev.astype(BF16)) * xi
            r_scr[h] = r_prev * math.exp(lg * CHUNK) + _dot_tn(v, (k_h * zeta).astype(BF16))
            mu = jnp.mean(o, axis=-1, keepdims=True)
            var = jnp.mean(jnp.square(o - mu), axis=-1, keepdims=True)
            o = (o - mu) * lax.rsqrt(var + EPS) * rgain_ref[...] + rbias_ref[...]
            gate = proj_ref[rows, OFF_RG + h * DV:OFF_RG + (h + 1) * DV]
            o_ref[rows, HW + h * DV:HW + (h + 1) * DV] = (o * _silu(gate)).astype(o_ref.dtype)
        return carry

    lax.fori_loop(0, rows_blk // CHUNK, chunk_body, 0)
    s_out_ref[...] = s_scr[...]
    r_out_ref[...] = r_scr[...]


def _mixer(x2d, g1, w_all, cosf, sinf, w2p, gbp, ggain, rgain, rbias, s0, r0, *, batch, n_invalid):
    rows = x2d.shape[0] // batch
    rows_blk = _row_block(rows, 256)
    nblk = rows // rows_blk
    const2 = lambda b, j: (0, 0)
    const3 = lambda b, j: (0, 0, 0)
    state = jax.ShapeDtypeStruct((HEADS, DV, LANES), F32)
    return pl.pallas_call(
        functools.partial(_mixer_kernel, n_invalid=n_invalid, rows_blk=rows_blk),
        grid=(batch, nblk),
        in_specs=[
            pl.BlockSpec((rows_blk, D_MODEL), lambda b, j: (b * nblk + j, 0)),
            pl.BlockSpec((1, D_MODEL), const2),
            pl.BlockSpec((D_MODEL, PROJ_W), const2),
            pl.BlockSpec((rows_blk, HW), lambda b, j: (j, 0)),
            pl.BlockSpec((rows_blk, HW), lambda b, j: (j, 0)),
            pl.BlockSpec((LANES, HW), const2),
            pl.BlockSpec((1, HW), const2),
            pl.BlockSpec((1, DV), const2),
            pl.BlockSpec((1, DV), const2),
            pl.BlockSpec((1, DV), const2),
            pl.BlockSpec((HEADS, DV, LANES), const3),
            pl.BlockSpec((HEADS, DV, LANES), const3),
        ],
        out_specs=[
            pl.BlockSpec((rows_blk, D_MODEL), lambda b, j: (b * nblk + j, 0)),
            pl.BlockSpec((HEADS, DV, LANES), const3),
            pl.BlockSpec((HEADS, DV, LANES), const3),
        ],
        out_shape=[jax.ShapeDtypeStruct((batch * rows, D_MODEL), BF16), state, state],
        scratch_shapes=[pltpu.VMEM((HEADS, DV, LANES), F32), pltpu.VMEM((HEADS, DV, LANES), F32),
                        pltpu.VMEM((rows_blk, PROJ_W), F32)],
        compiler_params=pltpu.CompilerParams(
            dimension_semantics=("arbitrary", "arbitrary"), vmem_limit_bytes=VMEM_LIMIT),
        name="token_mixers",
    )(x2d, g1, w_all, cosf, sinf, w2p, gbp, ggain, rgain, rbias, s0, r0)


def _out_proj_kernel(o_ref, x_ref, w_ref, g_ref, h_ref, xn_ref, xp_ref):
    h = x_ref[...] + _dot(o_ref[...], w_ref[...])
    h_ref[...] = h
    xn = _rms(h, g_ref[...])
    xn_ref[...] = xn
    xp_ref[...] = _pack_pairs(xn[:, :HALF], xn[:, HALF:])


def _out_proj(o, x2d, w_out, gain2):
    n = x2d.shape[0]
    tm = _row_block(n, 512)
    blk = pl.BlockSpec((tm, D_MODEL), lambda i: (i, 0))
    return pl.pallas_call(
        _out_proj_kernel,
        grid=(n // tm,),
        in_specs=[blk, blk, pl.BlockSpec((D_MODEL, D_MODEL), lambda i: (0, 0)),
                  pl.BlockSpec((1, D_MODEL), lambda i: (0, 0))],
        out_specs=[blk, blk, pl.BlockSpec((tm, HALF), lambda i: (i, 0))],
        out_shape=[jax.ShapeDtypeStruct((n, D_MODEL), F32)] * 2 + [jax.ShapeDtypeStruct((n, HALF), jnp.uint32)],
        compiler_params=pltpu.CompilerParams(
            dimension_semantics=("parallel",), vmem_limit_bytes=VMEM_LIMIT),
        name="out_proj_norm",
    )(o, x2d, w_out, gain2)


def _top_k_rows(arrays, payloads=None):
    arrays = list(arrays)
    n = arrays[0].shape[0]
    rowi = lax.broadcasted_iota(jnp.int32, arrays[0].shape, 0)
    vals = [[] for _ in arrays]
    picks = [[] for _ in arrays]
    for _ in range(PEER_TOPK):
        for k, s in enumerate(arrays):
            m = jnp.max(s, axis=0, keepdims=True)
            am = jnp.min(jnp.where(s == m, rowi, n), axis=0, keepdims=True)
            sel = rowi == am
            vals[k].append(m)
            if payloads is None:
                picks[k].append(am)
            else:
                picks[k].append(jnp.min(jnp.where(sel, payloads[k], jnp.iinfo(jnp.int32).max),
                                        axis=0, keepdims=True))
            arrays[k] = jnp.where(sel, -jnp.inf, s)
    return [(jnp.concatenate(v, axis=0), jnp.concatenate(p, axis=0)) for v, p in zip(vals, picks)]


def _candidate_cells():
    cells = [(a, b) for a in range(PEER_TOPK) for b in range(PEER_TOPK) if (a + 1) * (b + 1) <= PEER_TOPK]
    n = -(-len(cells) // 8) * 8
    sel_a = np.zeros((n, PEER_TOPK), np.float32)
    sel_b = np.zeros((n, PEER_TOPK), np.float32)
    pad = np.full((n, 1), -np.inf, np.float32)
    for r, (a, b) in enumerate(cells):
        sel_a[r, a] = 1.0
        sel_b[r, b] = 1.0
        pad[r, 0] = 0.0
    return sel_a, sel_b, pad


def _route_kernel(xn_ref, wq_ref, sk_ref, sa_ref, sb_ref, pad_ref, idx_ref, g_ref, q_scr, i_scr, g_scr):
    tr = xn_ref.shape[0]
    q = _dot(xn_ref[...].astype(BF16), wq_ref[...])
    for hc in range(2 * PEER_HEADS):
        q_scr[hc] = q[:, hc * LANES:(hc + 1) * LANES].astype(BF16)

    def sub_body(sb, carry):
        rows = pl.ds(pl.multiple_of(sb * LANES, LANES), LANES)

        def head_pair(hp, c2):
            heads = [2 * hp, 2 * hp + 1]
            scores = [_dot_nt(sk_ref[2 * h + c], q_scr[2 * h + c, rows, :])
                      for h in heads for c in range(2)]
            first = _top_k_rows(scores)
            sa, sb = sa_ref[...], sb_ref[...]
            cands, cidxs = [], []
            for k in range(len(heads)):
                (v0, i0), (v1, i1) = first[2 * k], first[2 * k + 1]
                cands.append((_dot(sa, v0, precision=lax.Precision.HIGHEST)
                              + _dot(sb, v1, precision=lax.Precision.HIGHEST)) + pad_ref[...])
                cidxs.append((_dot(sa, i0.astype(F32)) * PEER_NKEYS
                              + _dot(sb, i1.astype(F32))).astype(jnp.int32))
            for h, (best, e) in zip(heads, _top_k_rows(cands, cidxs)):
                ex = jnp.exp(best - best[0:1])
                i_scr[h] = e
                g_scr[h] = ex / jnp.sum(ex, axis=0, keepdims=True)
            return c2

        lax.fori_loop(0, PEER_HEADS // 2, head_pair, 0)
        idx_ref[rows, :] = i_scr[...].reshape(PEER_SEL, LANES).T
        g_ref[rows, :] = g_scr[...].reshape(PEER_SEL, LANES).T
        return carry

    lax.fori_loop(0, tr // LANES, sub_body, 0)


def _route(xn2, w_q, subkeys):
    n = xn2.shape[0]
    sel_a, sel_b, pad = _candidate_cells()
    tr = _row_block(n, 512)
    qw = 2 * PEER_HEADS * LANES
    return pl.pallas_call(
        _route_kernel,
        grid=(n // tr,),
        in_specs=[pl.BlockSpec((tr, D_MODEL), lambda i: (i, 0)),
                  pl.BlockSpec((D_MODEL, qw), lambda i: (0, 0)),
                  pl.BlockSpec((2 * PEER_HEADS, PEER_NKEYS, LANES), lambda i: (0, 0, 0)),
                  pl.BlockSpec(sel_a.shape, lambda i: (0, 0)),
                  pl.BlockSpec(sel_b.shape, lambda i: (0, 0)),
                  pl.BlockSpec(pad.shape, lambda i: (0, 0))],
        out_specs=[pl.BlockSpec((tr, PEER_SEL), lambda i: (i, 0))] * 2,
        out_shape=[jax.ShapeDtypeStruct((n, PEER_SEL), jnp.int32),
                   jax.ShapeDtypeStruct((n, PEER_SEL), F32)],
        scratch_shapes=[pltpu.VMEM((2 * PEER_HEADS, tr, LANES), BF16),
                        pltpu.VMEM((PEER_HEADS, PEER_TOPK, LANES), jnp.int32),
                        pltpu.VMEM((PEER_HEADS, PEER_TOPK, LANES), F32)],
        compiler_params=pltpu.CompilerParams(
            dimension_semantics=("parallel",), vmem_limit_bytes=VMEM_LIMIT),
        name="peer_route",
    )(xn2, w_q, subkeys, sel_a, sel_b, pad)


def _sc_worker_id():
    return lax.axis_index("s") * SC_CORES + lax.axis_index("c")


def _sc_mesh():
    return plsc.VectorSubcoreMesh(core_axis_name="c", subcore_axis_name="s")


def _sc_gather_ring(table_hbm, idx_v, bufs, sems, unit, compute):
    def gather(g, k):
        off = pl.multiple_of(g * SC_ROWS, SC_ROWS)
        return pltpu.make_async_copy(table_hbm.at[idx_v.at[pl.ds(off, SC_ROWS)]], bufs[k], sems[k])

    for k in range(SC_NBUF):
        gather(k, k).start()

    units_per_iter = SC_NBUF // unit

    def ring_iter(i, carry):
        for m in range(units_per_iter):
            ks = range(m * unit, (m + 1) * unit)
            for k in ks:
                gather(i * SC_NBUF + k, k).wait()
            compute(i * units_per_iter + m, [bufs[k] for k in ks])
            for k in ks:
                g_next = (i + 1) * SC_NBUF + k

                @pl.when(g_next < SC_GATHERS)
                def _():
                    gather(g_next, k).start()
        return carry

    lax.fori_loop(0, SC_GATHERS // SC_NBUF, ring_iter, 0)


def _bf16_bits(x):
    b = lax.bitcast_convert_type(x, jnp.uint32)
    return (b + jnp.uint32(0x7FFF) + ((b >> 16) & jnp.uint32(1))) >> 16


def _pack_pairs(lo, hi):
    return _bf16_bits(lo) | (_bf16_bits(hi) << 16)


def _pack_table(t):
    return _pack_pairs(t[:, :HALF], t[:, HALF:]).reshape(t.shape[0], ROW_TILES, LANES)


def _mul_words(w, m):
    p = plsc.bitcast(plsc.bitcast(w, jnp.bfloat16) * plsc.bitcast(m, jnp.bfloat16), jnp.uint32)
    return lax.bitcast_convert_type(p << 16, F32), lax.bitcast_convert_type(p, F32)


def _sc_scratch(staged_cols, out_cols):
    return ([pltpu.VMEM((SC_TOK_BATCH * PEER_SEL,), jnp.int32),
             pltpu.VMEM((SC_TOK_BATCH, staged_cols), jnp.uint32),
             pltpu.VMEM((SC_TOK_BATCH, out_cols), F32)]
            + [pltpu.VMEM((SC_ROWS, ROW_TILES, LANES), jnp.uint32) for _ in range(SC_NBUF)]
            + [pltpu.SemaphoreType.DMA for _ in range(SC_NBUF)])


def _sc_batches(tpw, idx_hbm, staged_hbm, out_hbm, idx_v, staged_v, out_v, run):
    tok0 = _sc_worker_id() * tpw

    def batch(tb, carry):
        t0 = pl.multiple_of(tok0 + tb * SC_TOK_BATCH, SC_TOK_BATCH)
        pltpu.sync_copy(idx_hbm.at[pl.ds(pl.multiple_of(t0 * PEER_SEL, SC_TOK_BATCH * PEER_SEL),
                                         SC_TOK_BATCH * PEER_SEL)], idx_v)
        pltpu.sync_copy(staged_hbm.at[pl.ds(t0, SC_TOK_BATCH)], staged_v)
        run()
        pltpu.sync_copy(out_v, out_hbm.at[pl.ds(t0, SC_TOK_BATCH)])
        return carry

    lax.fori_loop(0, tpw // SC_TOK_BATCH, batch, 0)


def _peer_dot_sc(u3, idx_flat, xn2):
    t = xn2.shape[0]
    unit = 16 // SC_ROWS
    units_per_tok = PEER_SEL // 16

    @functools.partial(
        pl.kernel,
        mesh=_sc_mesh(),
        out_type=jax.ShapeDtypeStruct((t, PEER_SEL), F32),
        scratch_types=[pltpu.VMEM((16 * SC_LANES,), F32)] + _sc_scratch(HALF, PEER_SEL),
        compiler_params=pltpu.CompilerParams(needs_layout_passes=False),
        name="peer_dot",
    )
    def body(u_hbm, idx_hbm, x_hbm, out_hbm, red_v, idx_v, x_v, out_v, *rest):
        bufs, sems = rest[:SC_NBUF], rest[SC_NBUF:]

        def compute(u, ub):
            tl = u // units_per_tok
            rb = (u % units_per_tok) * 16

            def cbody(c, acc):
                acc = list(acc)
                c4 = c // (LANES // SC_LANES)
                lane0 = pl.multiple_of((c % (LANES // SC_LANES)) * SC_LANES, SC_LANES)
                col = pl.multiple_of(c * SC_LANES, SC_LANES)
                xw = x_v[tl, pl.ds(col, SC_LANES)]
                for q, buf in enumerate(ub):
                    for r in range(SC_ROWS):
                        a = q * SC_ROWS + r
                        lo, hi = _mul_words(buf[r, c4, pl.ds(lane0, SC_LANES)], xw)
                        acc[a] = acc[a] + lo + hi
                return tuple(acc)

            acc = lax.fori_loop(0, HALF // SC_LANES, cbody,
                                tuple(jnp.zeros((SC_LANES,), F32) for _ in range(16)))
            for a in range(16):
                red_v[pl.ds(a * SC_LANES, SC_LANES)] = acc[a]
            lane_base = lax.iota(jnp.int32, SC_LANES) * SC_LANES
            tot = plsc.load_gather(red_v, [lane_base])
            for k in range(1, SC_LANES):
                tot = tot + plsc.load_gather(red_v, [lane_base + k])
            out_v[tl, pl.ds(pl.multiple_of(rb, 16), 16)] = tot

        _sc_batches(t // SC_WORKERS, idx_hbm, x_hbm, out_hbm, idx_v, x_v, out_v,
                    lambda: _sc_gather_ring(u_hbm, idx_v, bufs, sems, unit, compute))

    return body(u3, idx_flat, xn2)


def _peer_combine_sc(v3, idx_flat, coef_b):
    t = coef_b.shape[0]
    unit = 32 // SC_ROWS
    units_per_tok = PEER_SEL // 32

    @functools.partial(
        pl.kernel,
        mesh=_sc_mesh(),
        out_type=jax.ShapeDtypeStruct((t, D_MODEL), F32),
        scratch_types=_sc_scratch(PEER_SEL * SC_LANES, D_MODEL),
        compiler_params=pltpu.CompilerParams(needs_layout_passes=False),
        name="peer_combine",
    )
    def body(v_hbm, idx_hbm, coef_hbm, out_hbm, idx_v, coef_v, out_v, *rest):
        bufs, sems = rest[:SC_NBUF], rest[SC_NBUF:]

        def compute(u, ub):
            tl = u // units_per_tok
            rb = (u % units_per_tok) * 32
            for g in range(ROW_TILES):
                cols = ([pl.ds(g * LANES + l * SC_LANES, SC_LANES) for l in range(8)]
                        + [pl.ds(HALF + g * LANES + l * SC_LANES, SC_LANES) for l in range(8)])

                def rbody(r, acc):
                    acc = list(acc)
                    for q, buf in enumerate(ub):
                        co = pl.multiple_of((rb + q * SC_ROWS + r) * SC_LANES, SC_LANES)
                        cf = coef_v[tl, pl.ds(co, SC_LANES)]
                        for l in range(8):
                            lo, hi = _mul_words(buf[r, g, pl.ds(l * SC_LANES, SC_LANES)], cf)
                            acc[l] = acc[l] + lo
                            acc[8 + l] = acc[8 + l] + hi
                    return tuple(acc)

                acc = lax.fori_loop(0, SC_ROWS, rbody, tuple(out_v[tl, c] for c in cols))
                for k in range(16):
                    out_v[tl, cols[k]] = acc[k]

        def run():
            def zero(tl, c2):
                for k in range(D_MODEL // SC_LANES):
                    out_v[tl, pl.ds(k * SC_LANES, SC_LANES)] = jnp.zeros((SC_LANES,), F32)
                return c2

            lax.fori_loop(0, SC_TOK_BATCH, zero, 0)
            _sc_gather_ring(v_hbm, idx_v, bufs, sems, unit, compute)

        _sc_batches(t // SC_WORKERS, idx_hbm, coef_hbm, out_hbm, idx_v, coef_v, out_v, run)

    return body(v3, idx_flat, coef_b)


def _coef_kernel(act_ref, g_ref, e_ref, coef_ref):
    act = act_ref[...]
    gelu = 0.5 * act * (1.0 + lax.erf(act * np.float32(math.sqrt(0.5))))
    c = _dot(g_ref[...] * gelu, e_ref[...], precision=lax.Precision.HIGHEST)
    coef_ref[...] = _pack_pairs(c, c)


def _coef(act, g):
    n = act.shape[0]
    tr = _row_block(n, 512)
    w = PEER_SEL * SC_LANES
    expand = (jnp.arange(w, dtype=jnp.int32)[None, :] // SC_LANES
              == jnp.arange(PEER_SEL, dtype=jnp.int32)[:, None]).astype(F32)
    blk = pl.BlockSpec((tr, PEER_SEL), lambda i: (i, 0))
    return pl.pallas_call(
        _coef_kernel,
        grid=(n // tr,),
        in_specs=[blk, blk, pl.BlockSpec((PEER_SEL, w), lambda i: (0, 0))],
        out_specs=pl.BlockSpec((tr, w), lambda i: (i, 0)),
        out_shape=jax.ShapeDtypeStruct((n, w), jnp.uint32),
        compiler_params=pltpu.CompilerParams(
            dimension_semantics=("parallel",), vmem_limit_bytes=VMEM_LIMIT),
        name="peer_coef",
    )(act, g, expand)


def _final_kernel(h_ref, p_ref, g_ref, o_ref):
    o_ref[...] = _rms(h_ref[...] + p_ref[...], g_ref[...])


def _final(h2, peer_out, gain):
    n = h2.shape[0]
    tm = _row_block(n, 1024)
    blk = pl.BlockSpec((tm, D_MODEL), lambda i: (i, 0))
    return pl.pallas_call(
        _final_kernel,
        grid=(n // tm,),
        in_specs=[blk, blk, pl.BlockSpec((1, D_MODEL), lambda i: (0, 0))],
        out_specs=blk,
        out_shape=jax.ShapeDtypeStruct((n, D_MODEL), F32),
        compiler_params=pltpu.CompilerParams(
            dimension_semantics=("parallel",), vmem_limit_bytes=VMEM_LIMIT),
        name="final_norm",
    )(h2, peer_out, gain)


def _pad_heads(w, dk):
    lead = w.shape[:-1]
    w = w.reshape(lead + (HEADS, dk))
    w = jnp.pad(w, [(0, 0)] * len(lead) + [(0, 0), (0, LANES - dk)])
    return w.reshape(lead + (HEADS * LANES,))


def _rot_heads(w, dk):
    lead = w.shape[:-1]
    w = w.reshape(lead + (HEADS, 2, dk // 2))
    w = jnp.stack([-w[..., 1, :], w[..., 0, :]], axis=-2)
    return w.reshape(lead + (HEADS * dk,))


def _pack_in_weights(w_in):
    gq, gk, gv, gg, ga, rq, rk, rv, rg = jnp.split(w_in, np.cumsum(IN_SPLITS)[:-1].tolist(), axis=-1)
    ga = jnp.pad(ga, ((0, 0), (0, 2 * LANES - GATE_RANK)))
    return jnp.concatenate(
        [_pad_heads(gq, DK), _pad_heads(gk, DK), gv, gg,
         _pad_heads(rq, DK), _pad_heads(rk, DK), rv, rg,
         _pad_heads(_rot_heads(rq, DK), DK), _pad_heads(_rot_heads(rk, DK), DK), ga],
        axis=-1).astype(BF16)


def _rope_tables(pos):
    inv_freq = ROPE_BASE ** (-jnp.arange(0, DK, 2, dtype=F32) / DK)
    ang = pos[:, None] * inv_freq[None, :]
    full = lambda t: _pad_heads(jnp.tile(t, (1, 2 * HEADS)), DK)
    return full(jnp.cos(ang)), full(jnp.sin(ang))


def _segment_lengths(s, first):
    if first:
        lengths = [s * k // 16 for k in FIRST_SEQ_RAMP]
        assert sum(lengths) == s
    else:
        lengths = [s // SEGS_PER_SEQ] * SEGS_PER_SEQ
    assert all(n % SC_CALL_TOKENS == 0 for n in lengths)
    return lengths


def kernel(x, meta_tokens, norm1_gain, w_in, gla_gate_w2, gla_gate_b, gla_norm_gain, ret_norm_gain, ret_norm_bias, w_out, norm2_gain, peer_w_q, peer_subkeys, peer_u, peer_v, final_gain):
    b, s, d = x.shape
    assert d == D_MODEL and (s // SEGS_PER_SEQ) % SC_CALL_TOKENS == 0
    assert norm1_gain.shape[0] == 1, "single-layer block"
    x2d = x.reshape(b * s, d)

    w_all = _pack_in_weights(w_in[0])
    w2p = jnp.pad(_pad_heads(gla_gate_w2[0], DK), ((0, LANES - GATE_RANK), (0, 0))).astype(BF16)
    gbp = _pad_heads(gla_gate_b[0][None, :], DK)
    g1 = norm1_gain[0][None, :]
    ggain = gla_norm_gain[0][None, :]
    rgain = ret_norm_gain[0][None, :]
    rbias = ret_norm_bias[0][None, :]

    h0 = jnp.concatenate([jnp.zeros((N_PAD, d), x.dtype), meta_tokens.astype(x.dtype)], axis=0)
    cos0, sin0 = _rope_tables(jnp.arange(CHUNK, dtype=F32) - N_PAD)
    cos1, sin1 = _rope_tables(jnp.arange(s, dtype=F32) + N_META)
    zero_state = jnp.zeros((HEADS, DV, LANES), F32)
    _, s_meta, r_meta = _mixer(h0, g1, w_all, cos0, sin0, w2p, gbp, ggain, rgain, rbias,
                               zero_state, zero_state, batch=1, n_invalid=N_PAD)

    w_out_b = w_out[0].astype(BF16)
    g2 = norm2_gain[0][None, :]
    w_q = peer_w_q[0].astype(BF16)
    subkeys = peer_subkeys[0].reshape(2 * PEER_HEADS, PEER_NKEYS, PEER_DKEY // 2).astype(BF16)
    u3 = _pack_table(peer_u[0])
    v3 = _pack_table(peer_v[0])
    gf = final_gain[None, :]

    outs = []
    prev_g = None
    pending = []
    for bi in range(b):
        s_state, r_state = s_meta, r_meta
        start = 0
        for seg in _segment_lengths(s, first=bi == 0):
            xs = x2d[bi * s + start:bi * s + start + seg]
            pos = slice(start, start + seg)
            start += seg
            if prev_g is not None:
                xs, _ = lax.optimization_barrier((xs, prev_g))
            o, s_state, r_state = _mixer(xs, g1, w_all, cos1[pos], sin1[pos], w2p, gbp,
                                         ggain, rgain, rbias, s_state, r_state, batch=1, n_invalid=0)
            h2, xn2, xn2_pairs = _out_proj(o, xs, w_out_b, g2)
            idx, g = _route(xn2, w_q, subkeys)
            prev_g = g
            idx_flat = idx.reshape(-1)
            if len(pending) >= SC_LOOKAHEAD:
                h2_old, peer_old = pending.pop(0)
                idx_flat, peer_old = lax.optimization_barrier((idx_flat, peer_old))
                outs.append(_final(h2_old, peer_old, gf))
            act = _peer_dot_sc(u3, idx_flat, xn2_pairs)
            pending.append((h2, _peer_combine_sc(v3, idx_flat, _coef(act, g))))
    outs.extend(_final(h2_old, peer_old, gf) for h2_old, peer_old in pending)
    return jnp.concatenate(outs, axis=0).reshape(b, s, d)
```

```python
import functools
import math

import jax
import jax.numpy as jnp
import numpy as np
from jax import lax
from jax.experimental import pallas as pl
from jax.experimental.pallas import tpu as pltpu
from jax.experimental.pallas import tpu_sc as plsc

F32 = jnp.float32
BF16 = jnp.bfloat16

D_MODEL = 1024
N_META = 16
CHUNK = 64
N_PAD = CHUNK - N_META
EPS = 1e-6

HEADS = 4
DV = 128
DK = 64
LANES = 128
GATE_RANK = 16
GATE_TAU = 16.0
ROPE_BASE = 10000.0
IN_SPLITS = (HEADS * DK, HEADS * DK, HEADS * DV, HEADS * DV, GATE_RANK,
             HEADS * DK, HEADS * DK, HEADS * DV, HEADS * DV)

PEER_HEADS = 8
PEER_NKEYS = 128
PEER_DKEY = 256
PEER_TOPK = 16
PEER_SEL = PEER_HEADS * PEER_TOPK

HW = HEADS * LANES
OFF_GQ, OFF_GK, OFF_GV, OFF_GG = 0, HW, 2 * HW, 3 * HW
OFF_RQ, OFF_RK, OFF_RV, OFF_RG = 4 * HW, 5 * HW, 6 * HW, 7 * HW
OFF_RQR, OFF_RKR, OFF_GA = 8 * HW, 9 * HW, 10 * HW
PROJ_W = 10 * HW + 2 * LANES
PROJ_TN = PROJ_W // 3

LOG_GAMMA = tuple(math.log(1.0 - 2.0 ** (-5.0 - h)) for h in range(HEADS))

SC_CORES, SC_SUBCORES, SC_LANES = 2, 16, 16
SC_WORKERS = SC_CORES * SC_SUBCORES
SC_TOK_BATCH = 16
SC_ROWS = 16
SC_NBUF = 8
SC_GATHERS = SC_TOK_BATCH * PEER_SEL // SC_ROWS
HALF = D_MODEL // 2
ROW_TILES = HALF // LANES

SC_CALL_TOKENS = SC_WORKERS * SC_TOK_BATCH
SEGS_PER_SEQ = 2
FIRST_SEQ_RAMP = (1, 1, 2, 3, 4, 5)
SC_LOOKAHEAD = 2

VMEM_LIMIT = 48 * 1024 * 1024


def _row_block(n, target):
    blk = math.gcd(n, target)
    assert blk % 8 == 0 or blk == n, (n, target)
    return blk


def _dot(a, b, **kw):
    return jnp.dot(a, b, preferred_element_type=F32, **kw)


def _dot_nt(a, b):
    return lax.dot_general(a, b, (((1,), (1,)), ((), ())), preferred_element_type=F32)


def _dot_tn(a, b):
    return lax.dot_general(a, b, (((0,), (0,)), ((), ())), preferred_element_type=F32)


def _rms(x, g):
    return x * lax.rsqrt(jnp.mean(x * x, axis=-1, keepdims=True) + EPS) * g


def _silu(x):
    return x * jax.nn.sigmoid(x)


def _mixer_kernel(x_ref, g1_ref, wall_ref, cos_ref, sin_ref, w2_ref, gb_ref, ggain_ref, rgain_ref,
                  rbias_ref, s0_ref, r0_ref, o_ref, s_out_ref, r_out_ref, s_scr, r_scr, proj_ref,
                  *, n_invalid, rows_blk):
    j = pl.program_id(1)
    xn = _rms(x_ref[...], g1_ref[...]).astype(BF16)
    for jn in range(PROJ_W // PROJ_TN):
        cols = slice(jn * PROJ_TN, (jn + 1) * PROJ_TN)
        proj_ref[:, cols] = _dot(xn, wall_ref[:, cols])

    @pl.when(j == 0)
    def _():
        s_scr[...] = s0_ref[...]
        r_scr[...] = r0_ref[...]

    row = lax.broadcasted_iota(jnp.int32, (CHUNK, CHUNK), 0)
    col = lax.broadcasted_iota(jnp.int32, (CHUNK, CHUNK), 1)
    causal = row >= col
    tril = causal.astype(F32)
    diff = jnp.maximum(row - col, 0).astype(F32)
    rowf = lax.broadcasted_iota(jnp.int32, (CHUNK, LANES), 0).astype(F32)
    row1 = lax.broadcasted_iota(jnp.int32, (CHUNK, 1), 0)

    def chunk_body(c, carry):
        r0 = pl.multiple_of(c * CHUNK, CHUNK)
        rows = pl.ds(r0, CHUNK)
        valid = ((j * rows_blk + r0 + row1) >= n_invalid).astype(F32)

        ga = proj_ref[rows, OFF_GA:OFF_GA + LANES].astype(BF16)
        pre = _dot(ga, w2_ref[...]) + gb_ref[...]
        log_a = jax.nn.log_sigmoid(pre) * (1.0 / GATE_TAU)
        bcum = _dot(tril, log_a, precision=lax.Precision.HIGHEST)
        b_last = bcum[CHUNK - 1:CHUNK, :]
        gq = proj_ref[rows, OFF_GQ:OFF_GQ + HW] * (DK ** -0.5)
        gk = proj_ref[rows, OFF_GK:OFF_GK + HW] * valid
        q_t = (gq * jnp.exp(bcum)).astype(BF16)
        k_t = (gk * jnp.exp(-bcum)).astype(BF16)
        k_end = (gk * jnp.exp(b_last - bcum)).astype(BF16)
        decay = jnp.exp(b_last)
        for h in range(HEADS):
            sl = slice(h * LANES, (h + 1) * LANES)
            v = proj_ref[rows, OFF_GV + h * DV:OFF_GV + (h + 1) * DV].astype(BF16)
            att = jnp.where(causal, _dot_nt(q_t[:, sl], k_t[:, sl]), 0.0).astype(BF16)
            s_prev = s_scr[h]
            o = _dot(att, v) + _dot_nt(q_t[:, sl], s_prev.astype(BF16))
            s_scr[h] = s_prev * decay[:, sl] + _dot_tn(v, k_end[:, sl])
            o = o * lax.rsqrt(jnp.mean(o * o, axis=-1, keepdims=True) + EPS) * ggain_ref[...]
            gate = proj_ref[rows, OFF_GG + h * DV:OFF_GG + (h + 1) * DV]
            o_ref[rows, h * DV:(h + 1) * DV] = (o * _silu(gate)).astype(o_ref.dtype)

        cosb = cos_ref[rows, :]
        sinb = sin_ref[rows, :]
        rq = proj_ref[rows, OFF_RQ:OFF_RQ + HW]
        rqr = proj_ref[rows, OFF_RQR:OFF_RQR + HW]
        rk = proj_ref[rows, OFF_RK:OFF_RK + HW] * valid
        rkr = proj_ref[rows, OFF_RKR:OFF_RKR + HW] * valid
        q_r = (rq * cosb + rqr * sinb).astype(BF16)
        k_r = (rk * cosb + rkr * sinb) * (DK ** -0.5)
        for h in range(HEADS):
            sl = slice(h * LANES, (h + 1) * LANES)
            lg = LOG_GAMMA[h]
            dmat = jnp.where(causal, jnp.exp(lg * diff), 0.0)
            xi = jnp.exp(lg * (rowf + 1.0))
            zeta = jnp.exp(lg * (CHUNK - 1.0 - rowf))
            v = proj_ref[rows, OFF_RV + h * DV:OFF_RV + (h + 1) * DV].astype(BF16)
            k_h = k_r[:, sl]
            att = (_dot_nt(q_r[:, sl], k_h.astype(BF16)) * dmat).astype(BF16)
            r_prev = r_scr[h]
            o = _dot(att, v) + _dot_nt(q_r[:, sl], r_prev.astype(BF16)) * xi
            r_scr[h] = r_prev * math.exp(lg * CHUNK) + _dot_tn(v, (k_h * zeta).astype(BF16))
            mu = jnp.mean(o, axis=-1, keepdims=True)
            var = jnp.mean(jnp.square(o - mu), axis=-1, keepdims=True)
            o = (o - mu) * lax.rsqrt(var + EPS) * rgain_ref[...] + rbias_ref[...]
            gate = proj_ref[rows, OFF_RG + h * DV:OFF_RG + (h + 1) * DV]
            o_ref[rows, HW + h * DV:HW + (h + 1) * DV] = (o * _silu(gate)).astype(o_ref.dtype)
        return carry

    lax.fori_loop(0, rows_blk // CHUNK, chunk_body, 0)
    s_out_ref[...] = s_scr[...]
    r_out_ref[...] = r_scr[...]


def _mixer(x2d, g1, w_all, cosf, sinf, w2p, gbp, ggain, rgain, rbias, s0, r0, *, batch, n_invalid):
    rows = x2d.shape[0] // batch
    rows_blk = _row_block(rows, 256)
    nblk = rows // rows_blk
    const2 = lambda b, j: (0, 0)
    const3 = lambda b, j: (0, 0, 0)
    state = jax.ShapeDtypeStruct((HEADS, DV, LANES), F32)
    return pl.pallas_call(
        functools.partial(_mixer_kernel, n_invalid=n_invalid, rows_blk=rows_blk),
        grid=(batch, nblk),
        in_specs=[
            pl.BlockSpec((rows_blk, D_MODEL), lambda b, j: (b * nblk + j, 0)),
            pl.BlockSpec((1, D_MODEL), const2),
            pl.BlockSpec((D_MODEL, PROJ_W), const2),
            pl.BlockSpec((rows_blk, HW), lambda b, j: (j, 0)),
            pl.BlockSpec((rows_blk, HW), lambda b, j: (j, 0)),
            pl.BlockSpec((LANES, HW), const2),
            pl.BlockSpec((1, HW), const2),
            pl.BlockSpec((1, DV), const2),
            pl.BlockSpec((1, DV), const2),
            pl.BlockSpec((1, DV), const2),
            pl.BlockSpec((HEADS, DV, LANES), const3),
            pl.BlockSpec((HEADS, DV, LANES), const3),
        ],
        out_specs=[
            pl.BlockSpec((rows_blk, D_MODEL), lambda b, j: (b * nblk + j, 0)),
            pl.BlockSpec((HEADS, DV, LANES), const3),
            pl.BlockSpec((HEADS, DV, LANES), const3),
        ],
        out_shape=[jax.ShapeDtypeStruct((batch * rows, D_MODEL), BF16), state, state],
        scratch_shapes=[pltpu.VMEM((HEADS, DV, LANES), F32), pltpu.VMEM((HEADS, DV, LANES), F32),
                        pltpu.VMEM((rows_blk, PROJ_W), F32)],
        compiler_params=pltpu.CompilerParams(
            dimension_semantics=("arbitrary", "arbitrary"), vmem_limit_bytes=VMEM_LIMIT),
        name="token_mixers",
    )(x2d, g1, w_all, cosf, sinf, w2p, gbp, ggain, rgain, rbias, s0, r0)


def _out_proj_kernel(o_ref, x_ref, w_ref, g_ref, h_ref, xn_ref, xp_ref):
    h = x_ref[...] + _dot(o_ref[...], w_ref[...])
    h_ref[...] = h
    xn = _rms(h, g_ref[...])
    xn_ref[...] = xn
    xp_ref[...] = _pack_pairs(xn[:, :HALF], xn[:, HALF:])


def _out_proj(o, x2d, w_out, gain2):
    n = x2d.shape[0]
    tm = _row_block(n, 512)
    blk = pl.BlockSpec((tm, D_MODEL), lambda i: (i, 0))
    return pl.pallas_call(
        _out_proj_kernel,
        grid=(n // tm,),
        in_specs=[blk, blk, pl.BlockSpec((D_MODEL, D_MODEL), lambda i: (0, 0)),
                  pl.BlockSpec((1, D_MODEL), lambda i: (0, 0))],
        out_specs=[blk, blk, pl.BlockSpec((tm, HALF), lambda i: (i, 0))],
        out_shape=[jax.ShapeDtypeStruct((n, D_MODEL), F32)] * 2 + [jax.ShapeDtypeStruct((n, HALF), jnp.uint32)],
        compiler_params=pltpu.CompilerParams(
            dimension_semantics=("parallel",), vmem_limit_bytes=VMEM_LIMIT),
        name="out_proj_norm",
    )(o, x2d, w_out, gain2)


def _top_k_rows(arrays, payloads=None):
    arrays = list(arrays)
    n = arrays[0].shape[0]
    rowi = lax.broadcasted_iota(jnp.int32, arrays[0].shape, 0)
    vals = [[] for _ in arrays]
    picks = [[] for _ in arrays]
    for _ in range(PEER_TOPK):
        for k, s in enumerate(arrays):
            m = jnp.max(s, axis=0, keepdims=True)
            am = jnp.min(jnp.where(s == m, rowi, n), axis=0, keepdims=True)
            sel = rowi == am
            vals[k].append(m)
            if payloads is None:
                picks[k].append(am)
            else:
                picks[k].append(jnp.min(jnp.where(sel, payloads[k], jnp.iinfo(jnp.int32).max),
                                        axis=0, keepdims=True))
            arrays[k] = jnp.where(sel, -jnp.inf, s)
    return [(jnp.concatenate(v, axis=0), jnp.concatenate(p, axis=0)) for v, p in zip(vals, picks)]


def _candidate_cells():
    cells = [(a, b) for a in range(PEER_TOPK) for b in range(PEER_TOPK) if (a + 1) * (b + 1) <= PEER_TOPK]
    n = -(-len(cells) // 8) * 8
    sel_a = np.zeros((n, PEER_TOPK), np.float32)
    sel_b = np.zeros((n, PEER_TOPK), np.float32)
    pad = np.full((n, 1), -np.inf, np.float32)
    for r, (a, b) in enumerate(cells):
        sel_a[r, a] = 1.0
        sel_b[r, b] = 1.0
        pad[r, 0] = 0.0
    return sel_a, sel_b, pad


def _route_kernel(xn_ref, wq_ref, sk_ref, sa_ref, sb_ref, pad_ref, idx_ref, g_ref, q_scr, i_scr, g_scr):
    tr = xn_ref.shape[0]
    q = _dot(xn_ref[...].astype(BF16), wq_ref[...])
    for hc in range(2 * PEER_HEADS):
        q_scr[hc] = q[:, hc * LANES:(hc + 1) * LANES].astype(BF16)

    def sub_body(sb, carry):
        rows = pl.ds(pl.multiple_of(sb * LANES, LANES), LANES)

        def head_pair(hp, c2):
            heads = [2 * hp, 2 * hp + 1]
            scores = [_dot_nt(sk_ref[2 * h + c], q_scr[2 * h + c, rows, :])
                      for h in heads for c in range(2)]
            first = _top_k_rows(scores)
            sa, sb = sa_ref[...], sb_ref[...]
            cands, cidxs = [], []
            for k in range(len(heads)):
                (v0, i0), (v1, i1) = first[2 * k], first[2 * k + 1]
                cands.append((_dot(sa, v0, precision=lax.Precision.HIGHEST)
                              + _dot(sb, v1, precision=lax.Precision.HIGHEST)) + pad_ref[...])
                cidxs.append((_dot(sa, i0.astype(F32)) * PEER_NKEYS
                              + _dot(sb, i1.astype(F32))).astype(jnp.int32))
            for h, (best, e) in zip(heads, _top_k_rows(cands, cidxs)):
                ex = jnp.exp(best - best[0:1])
                i_scr[h] = e
                g_scr[h] = ex / jnp.sum(ex, axis=0, keepdims=True)
            return c2

        lax.fori_loop(0, PEER_HEADS // 2, head_pair, 0)
        idx_ref[rows, :] = i_scr[...].reshape(PEER_SEL, LANES).T
        g_ref[rows, :] = g_scr[...].reshape(PEER_SEL, LANES).T
        return carry

    lax.fori_loop(0, tr // LANES, sub_body, 0)


def _route(xn2, w_q, subkeys):
    n = xn2.shape[0]
    sel_a, sel_b, pad = _candidate_cells()
    tr = _row_block(n, 512)
    qw = 2 * PEER_HEADS * LANES
    return pl.pallas_call(
        _route_kernel,
        grid=(n // tr,),
        in_specs=[pl.BlockSpec((tr, D_MODEL), lambda i: (i, 0)),
                  pl.BlockSpec((D_MODEL, qw), lambda i: (0, 0)),
                  pl.BlockSpec((2 * PEER_HEADS, PEER_NKEYS, LANES), lambda i: (0, 0, 0)),
                  pl.BlockSpec(sel_a.shape, lambda i: (0, 0)),
                  pl.BlockSpec(sel_b.shape, lambda i: (0, 0)),
                  pl.BlockSpec(pad.shape, lambda i: (0, 0))],
        out_specs=[pl.BlockSpec((tr, PEER_SEL), lambda i: (i, 0))] * 2,
        out_shape=[jax.ShapeDtypeStruct((n, PEER_SEL), jnp.int32),
                   jax.ShapeDtypeStruct((n, PEER_SEL), F32)],
        scratch_shapes=[pltpu.VMEM((2 * PEER_HEADS, tr, LANES), BF16),
                        pltpu.VMEM((PEER_HEADS, PEER_TOPK, LANES), jnp.int32),
                        pltpu.VMEM((PEER_HEADS, PEER_TOPK, LANES), F32)],
        compiler_params=pltpu.CompilerParams(
            dimension_semantics=("parallel",), vmem_limit_bytes=VMEM_LIMIT),
        name="peer_route",
    )(xn2, w_q, subkeys, sel_a, sel_b, pad)


def _sc_worker_id():
    return lax.axis_index("s") * SC_CORES + lax.axis_index("c")


def _sc_mesh():
    return plsc.VectorSubcoreMesh(core_axis_name="c", subcore_axis_name="s")


def _sc_gather_ring(table_hbm, idx_v, bufs, sems, unit, compute):
    def gather(g, k):
        off = pl.multiple_of(g * SC_ROWS, SC_ROWS)
        return pltpu.make_async_copy(table_hbm.at[idx_v.at[pl.ds(off, SC_ROWS)]], bufs[k], sems[k])

    for k in range(SC_NBUF):
        gather(k, k).start()

    units_per_iter = SC_NBUF // unit

    def ring_iter(i, carry):
        for m in range(units_per_iter):
            ks = range(m * unit, (m + 1) * unit)
            for k in ks:
                gather(i * SC_NBUF + k, k).wait()
            compute(i * units_per_iter + m, [bufs[k] for k in ks])
            for k in ks:
                g_next = (i + 1) * SC_NBUF + k

                @pl.when(g_next < SC_GATHERS)
                def _():
                    gather(g_next, k).start()
        return carry

    lax.fori_loop(0, SC_GATHERS // SC_NBUF, ring_iter, 0)


def _bf16_bits(x):
    b = lax.bitcast_convert_type(x, jnp.uint32)
    return (b + jnp.uint32(0x7FFF) + ((b >> 16) & jnp.uint32(1))) >> 16


def _pack_pairs(lo, hi):
    return _bf16_bits(lo) | (_bf16_bits(hi) << 16)


def _pack_table(t):
    return _pack_pairs(t[:, :HALF], t[:, HALF:]).reshape(t.shape[0], ROW_TILES, LANES)


def _mul_add_words(w1, m1, w2, m2):
    bf = lambda a: plsc.bitcast(a, jnp.bfloat16)
    p = plsc.bitcast(bf(w1) * bf(m1) + bf(w2) * bf(m2), jnp.uint32)
    return lax.bitcast_convert_type(p << 16, F32), lax.bitcast_convert_type(p, F32)


def _sc_scratch(staged_cols, out_cols):
    return ([pltpu.VMEM((SC_TOK_BATCH * PEER_SEL,), jnp.int32),
             pltpu.VMEM((SC_TOK_BATCH, staged_cols), jnp.uint32),
             pltpu.VMEM((SC_TOK_BATCH, out_cols), F32)]
            + [pltpu.VMEM((SC_ROWS, ROW_TILES, LANES), jnp.uint32) for _ in range(SC_NBUF)]
            + [pltpu.SemaphoreType.DMA for _ in range(SC_NBUF)])


def _sc_batches(tpw, idx_hbm, staged_hbm, out_hbm, idx_v, staged_v, out_v, run):
    tok0 = _sc_worker_id() * tpw

    def batch(tb, carry):
        t0 = pl.multiple_of(tok0 + tb * SC_TOK_BATCH, SC_TOK_BATCH)
        pltpu.sync_copy(idx_hbm.at[pl.ds(pl.multiple_of(t0 * PEER_SEL, SC_TOK_BATCH * PEER_SEL),
                                         SC_TOK_BATCH * PEER_SEL)], idx_v)
        pltpu.sync_copy(staged_hbm.at[pl.ds(t0, SC_TOK_BATCH)], staged_v)
        run()
        pltpu.sync_copy(out_v, out_hbm.at[pl.ds(t0, SC_TOK_BATCH)])
        return carry

    lax.fori_loop(0, tpw // SC_TOK_BATCH, batch, 0)


def _peer_dot_sc(u3, idx_flat, xn2):
    t = xn2.shape[0]
    unit = 16 // SC_ROWS
    units_per_tok = PEER_SEL // 16

    @functools.partial(
        pl.kernel,
        mesh=_sc_mesh(),
        out_type=jax.ShapeDtypeStruct((t, PEER_SEL), F32),
        scratch_types=[pltpu.VMEM((16 * SC_LANES,), F32)] + _sc_scratch(HALF, PEER_SEL),
        compiler_params=pltpu.CompilerParams(needs_layout_passes=False),
        name="peer_dot",
    )
    def body(u_hbm, idx_hbm, x_hbm, out_hbm, red_v, idx_v, x_v, out_v, *rest):
        bufs, sems = rest[:SC_NBUF], rest[SC_NBUF:]

        def compute(u, ub):
            tl = u // units_per_tok
            rb = (u % units_per_tok) * 16

            def cbody(c, acc):
                acc = list(acc)
                c4 = c // (LANES // (2 * SC_LANES))
                lane0 = pl.multiple_of((c % (LANES // (2 * SC_LANES))) * 2 * SC_LANES, 2 * SC_LANES)
                col = pl.multiple_of(c * 2 * SC_LANES, 2 * SC_LANES)
                xw1 = x_v[tl, pl.ds(col, SC_LANES)]
                xw2 = x_v[tl, pl.ds(col + SC_LANES, SC_LANES)]
                for q, buf in enumerate(ub):
                    for r in range(SC_ROWS):
                        a = q * SC_ROWS + r
                        lo, hi = _mul_add_words(buf[r, c4, pl.ds(lane0, SC_LANES)], xw1,
                                                buf[r, c4, pl.ds(lane0 + SC_LANES, SC_LANES)], xw2)
                        acc[a] = acc[a] + lo + hi
                return tuple(acc)

            acc = lax.fori_loop(0, HALF // (2 * SC_LANES), cbody,
                                tuple(jnp.zeros((SC_LANES,), F32) for _ in range(16)))
            for a in range(16):
                red_v[pl.ds(a * SC_LANES, SC_LANES)] = acc[a]
            lane_base = lax.iota(jnp.int32, SC_LANES) * SC_LANES
            tot = plsc.load_gather(red_v, [lane_base])
            for k in range(1, SC_LANES):
                tot = tot + plsc.load_gather(red_v, [lane_base + k])
            out_v[tl, pl.ds(pl.multiple_of(rb, 16), 16)] = tot

        _sc_batches(t // SC_WORKERS, idx_hbm, x_hbm, out_hbm, idx_v, x_v, out_v,
                    lambda: _sc_gather_ring(u_hbm, idx_v, bufs, sems, unit, compute))

    return body(u3, idx_flat, xn2)


def _peer_combine_sc(v3, idx_flat, coef_b):
    t = coef_b.shape[0]
    unit = 32 // SC_ROWS
    units_per_tok = PEER_SEL // 32

    @functools.partial(
        pl.kernel,
        mesh=_sc_mesh(),
        out_type=jax.ShapeDtypeStruct((t, D_MODEL), F32),
        scratch_types=_sc_scratch(PEER_SEL * SC_LANES, D_MODEL),
        compiler_params=pltpu.CompilerParams(needs_layout_passes=False),
        name="peer_combine",
    )
    def body(v_hbm, idx_hbm, coef_hbm, out_hbm, idx_v, coef_v, out_v, *rest):
        bufs, sems = rest[:SC_NBUF], rest[SC_NBUF:]

        def compute(u, ub):
            tl = u // units_per_tok
            rb = (u % units_per_tok) * 32
            for g in range(ROW_TILES):
                cols = ([pl.ds(g * LANES + l * SC_LANES, SC_LANES) for l in range(8)]
                        + [pl.ds(HALF + g * LANES + l * SC_LANES, SC_LANES) for l in range(8)])

                def rbody(rp, acc):
                    acc = list(acc)
                    r = 2 * rp
                    for q, buf in enumerate(ub):
                        co = pl.multiple_of((rb + q * SC_ROWS + r) * SC_LANES, 2 * SC_LANES)
                        cf1 = coef_v[tl, pl.ds(co, SC_LANES)]
                        cf2 = coef_v[tl, pl.ds(co + SC_LANES, SC_LANES)]
                        for l in range(8):
                            lo, hi = _mul_add_words(buf[r, g, pl.ds(l * SC_LANES, SC_LANES)], cf1,
                                                    buf[r + 1, g, pl.ds(l * SC_LANES, SC_LANES)], cf2)
                            acc[l] = acc[l] + lo
                            acc[8 + l] = acc[8 + l] + hi
                    return tuple(acc)

                acc = lax.fori_loop(0, SC_ROWS // 2, rbody, tuple(out_v[tl, c] for c in cols))
                for k in range(16):
                    out_v[tl, cols[k]] = acc[k]

        def run():
            def zero(tl, c2):
                for k in range(D_MODEL // SC_LANES):
                    out_v[tl, pl.ds(k * SC_LANES, SC_LANES)] = jnp.zeros((SC_LANES,), F32)
                return c2

            lax.fori_loop(0, SC_TOK_BATCH, zero, 0)
            _sc_gather_ring(v_hbm, idx_v, bufs, sems, unit, compute)

        _sc_batches(t // SC_WORKERS, idx_hbm, coef_hbm, out_hbm, idx_v, coef_v, out_v, run)

    return body(v3, idx_flat, coef_b)


def _coef_kernel(act_ref, g_ref, e_ref, coef_ref):
    act = act_ref[...]
    gelu = 0.5 * act * (1.0 + lax.erf(act * np.float32(math.sqrt(0.5))))
    c = _dot(g_ref[...] * gelu, e_ref[...], precision=lax.Precision.HIGHEST)
    coef_ref[...] = _pack_pairs(c, c)


def _coef(act, g):
    n = act.shape[0]
    tr = _row_block(n, 512)
    w = PEER_SEL * SC_LANES
    expand = (jnp.arange(w, dtype=jnp.int32)[None, :] // SC_LANES
              == jnp.arange(PEER_SEL, dtype=jnp.int32)[:, None]).astype(F32)
    blk = pl.BlockSpec((tr, PEER_SEL), lambda i: (i, 0))
    return pl.pallas_call(
        _coef_kernel,
        grid=(n // tr,),
        in_specs=[blk, blk, pl.BlockSpec((PEER_SEL, w), lambda i: (0, 0))],
        out_specs=pl.BlockSpec((tr, w), lambda i: (i, 0)),
        out_shape=jax.ShapeDtypeStruct((n, w), jnp.uint32),
        compiler_params=pltpu.CompilerParams(
            dimension_semantics=("parallel",), vmem_limit_bytes=VMEM_LIMIT),
        name="peer_coef",
    )(act, g, expand)


def _final_kernel(h_ref, p_ref, g_ref, o_ref):
    o_ref[...] = _rms(h_ref[...] + p_ref[...], g_ref[...])


def _final(h2, peer_out, gain):
    n = h2.shape[0]
    tm = _row_block(n, 1024)
    blk = pl.BlockSpec((tm, D_MODEL), lambda i: (i, 0))
    return pl.pallas_call(
        _final_kernel,
        grid=(n // tm,),
        in_specs=[blk, blk, pl.BlockSpec((1, D_MODEL), lambda i: (0, 0))],
        out_specs=blk,
        out_shape=jax.ShapeDtypeStruct((n, D_MODEL), F32),
        compiler_params=pltpu.CompilerParams(
            dimension_semantics=("parallel",), vmem_limit_bytes=VMEM_LIMIT),
        name="final_norm",
    )(h2, peer_out, gain)


def _pad_heads(w, dk):
    lead = w.shape[:-1]
    w = w.reshape(lead + (HEADS, dk))
    w = jnp.pad(w, [(0, 0)] * len(lead) + [(0, 0), (0, LANES - dk)])
    return w.reshape(lead + (HEADS * LANES,))


def _rot_heads(w, dk):
    lead = w.shape[:-1]
    w = w.reshape(lead + (HEADS, 2, dk // 2))
    w = jnp.stack([-w[..., 1, :], w[..., 0, :]], axis=-2)
    return w.reshape(lead + (HEADS * dk,))


def _pack_in_weights(w_in):
    gq, gk, gv, gg, ga, rq, rk, rv, rg = jnp.split(w_in, np.cumsum(IN_SPLITS)[:-1].tolist(), axis=-1)
    ga = jnp.pad(ga, ((0, 0), (0, 2 * LANES - GATE_RANK)))
    return jnp.concatenate(
        [_pad_heads(gq, DK), _pad_heads(gk, DK), gv, gg,
         _pad_heads(rq, DK), _pad_heads(rk, DK), rv, rg,
         _pad_heads(_rot_heads(rq, DK), DK), _pad_heads(_rot_heads(rk, DK), DK), ga],
        axis=-1).astype(BF16)


def _rope_tables(pos):
    inv_freq = ROPE_BASE ** (-jnp.arange(0, DK, 2, dtype=F32) / DK)
    ang = pos[:, None] * inv_freq[None, :]
    full = lambda t: _pad_heads(jnp.tile(t, (1, 2 * HEADS)), DK)
    return full(jnp.cos(ang)), full(jnp.sin(ang))


def _segment_lengths(s, first):
    if first:
        lengths = [s * k // 16 for k in FIRST_SEQ_RAMP]
        assert sum(lengths) == s
    else:
        lengths = [s // SEGS_PER_SEQ] * SEGS_PER_SEQ
    assert all(n % SC_CALL_TOKENS == 0 for n in lengths)
    return lengths


def kernel(x, meta_tokens, norm1_gain, w_in, gla_gate_w2, gla_gate_b, gla_norm_gain, ret_norm_gain, ret_norm_bias, w_out, norm2_gain, peer_w_q, peer_subkeys, peer_u, peer_v, final_gain):
    b, s, d = x.shape
    assert d == D_MODEL and (s // SEGS_PER_SEQ) % SC_CALL_TOKENS == 0
    assert norm1_gain.shape[0] == 1, "single-layer block"
    x2d = x.reshape(b * s, d)

    w_all = _pack_in_weights(w_in[0])
    w2p = jnp.pad(_pad_heads(gla_gate_w2[0], DK), ((0, LANES - GATE_RANK), (0, 0))).astype(BF16)
    gbp = _pad_heads(gla_gate_b[0][None, :], DK)
    g1 = norm1_gain[0][None, :]
    ggain = gla_norm_gain[0][None, :]
    rgain = ret_norm_gain[0][None, :]
    rbias = ret_norm_bias[0][None, :]

    h0 = jnp.concatenate([jnp.zeros((N_PAD, d), x.dtype), meta_tokens.astype(x.dtype)], axis=0)
    cos0, sin0 = _rope_tables(jnp.arange(CHUNK, dtype=F32) - N_PAD)
    cos1, sin1 = _rope_tables(jnp.arange(s, dtype=F32) + N_META)
    zero_state = jnp.zeros((HEADS, DV, LANES), F32)
    _, s_meta, r_meta = _mixer(h0, g1, w_all, cos0, sin0, w2p, gbp, ggain, rgain, rbias,
                               zero_state, zero_state, batch=1, n_invalid=N_PAD)

    w_out_b = w_out[0].astype(BF16)
    g2 = norm2_gain[0][None, :]
    w_q = peer_w_q[0].astype(BF16)
    subkeys = peer_subkeys[0].reshape(2 * PEER_HEADS, PEER_NKEYS, PEER_DKEY // 2).astype(BF16)
    u3 = _pack_table(peer_u[0])
    v3 = _pack_table(peer_v[0])
    gf = final_gain[None, :]

    outs = []
    prev_g = None
    pending = []
    for bi in range(b):
        s_state, r_state = s_meta, r_meta
        start = 0
        for seg in _segment_lengths(s, first=bi == 0):
            xs = x2d[bi * s + start:bi * s + start + seg]
            pos = slice(start, start + seg)
            start += seg
            if prev_g is not None:
                xs, _ = lax.optimization_barrier((xs, prev_g))
            o, s_state, r_state = _mixer(xs, g1, w_all, cos1[pos], sin1[pos], w2p, gbp,
                                         ggain, rgain, rbias, s_state, r_state, batch=1, n_invalid=0)
            h2, xn2, xn2_pairs = _out_proj(o, xs, w_out_b, g2)
            idx, g = _route(xn2, w_q, subkeys)
            prev_g = g
            idx_flat = idx.reshape(-1)
            if len(pending) >= SC_LOOKAHEAD:
                h2_old, peer_old = pending.pop(0)
                idx_flat, peer_old = lax.optimization_barrier((idx_flat, peer_old))
                outs.append(_final(h2_old, peer_old, gf))
            act = _peer_dot_sc(u3, idx_flat, xn2_pairs)
            pending.append((h2, _peer_combine_sc(v3, idx_flat, _coef(act, g))))
    outs.extend(_final(h2_old, peer_old, gf) for h2_old, peer_old in pending)
    return jnp.concatenate(outs, axis=0).reshape(b, s, d)
```

```python
import functools
import math

import jax
import jax.numpy as jnp
import numpy as np
from jax import lax
from jax.experimental import pallas as pl
from jax.experimental.pallas import tpu as pltpu
from jax.experimental.pallas import tpu_sc as plsc

F32 = jnp.float32
BF16 = jnp.bfloat16

D_MODEL = 1024
N_META = 16
CHUNK = 64
N_PAD = CHUNK - N_META
EPS = 1e-6

HEADS = 4
DV = 128
DK = 64
LANES = 128
GATE_RANK = 16
GATE_TAU = 16.0
ROPE_BASE = 10000.0
IN_SPLITS = (HEADS * DK, HEADS * DK, HEADS * DV, HEADS * DV, GATE_RANK,
             HEADS * DK, HEADS * DK, HEADS * DV, HEADS * DV)

PEER_HEADS = 8
PEER_NKEYS = 128
PEER_DKEY = 256
PEER_TOPK = 16
PEER_SEL = PEER_HEADS * PEER_TOPK

HW = HEADS * LANES
OFF_GQ, OFF_GK, OFF_GV, OFF_GG = 0, HW, 2 * HW, 3 * HW
OFF_RQ, OFF_RK, OFF_RV, OFF_RG = 4 * HW, 5 * HW, 6 * HW, 7 * HW
OFF_RQR, OFF_RKR, OFF_GA = 8 * HW, 9 * HW, 10 * HW
PROJ_W = 10 * HW + 2 * LANES
PROJ_TN = PROJ_W // 3

LOG_GAMMA = tuple(math.log(1.0 - 2.0 ** (-5.0 - h)) for h in range(HEADS))

SC_CORES, SC_SUBCORES, SC_LANES = 2, 16, 16
SC_WORKERS = SC_CORES * SC_SUBCORES
SC_TOK_BATCH = 16
SC_ROWS = 8
SC_NBUF = 16
SC_GATHERS = SC_TOK_BATCH * PEER_SEL // SC_ROWS
HALF = D_MODEL // 2
ROW_TILES = HALF // LANES

SC_CALL_TOKENS = SC_WORKERS * SC_TOK_BATCH
SEGS_PER_SEQ = 2
FIRST_SEQ_RAMP = (1, 1, 2, 3, 4, 5)
SC_LOOKAHEAD = 2

VMEM_LIMIT = 48 * 1024 * 1024


def _row_block(n, target):
    blk = math.gcd(n, target)
    assert blk % 8 == 0 or blk == n, (n, target)
    return blk


def _dot(a, b, **kw):
    return jnp.dot(a, b, preferred_element_type=F32, **kw)


def _dot_nt(a, b):
    return lax.dot_general(a, b, (((1,), (1,)), ((), ())), preferred_element_type=F32)


def _dot_tn(a, b):
    return lax.dot_general(a, b, (((0,), (0,)), ((), ())), preferred_element_type=F32)


def _rms(x, g):
    return x * lax.rsqrt(jnp.mean(x * x, axis=-1, keepdims=True) + EPS) * g


def _silu(x):
    return x * jax.nn.sigmoid(x)


def _mixer_kernel(x_ref, g1_ref, wall_ref, cos_ref, sin_ref, w2_ref, gb_ref, ggain_ref, rgain_ref,
                  rbias_ref, s0_ref, r0_ref, o_ref, s_out_ref, r_out_ref, s_scr, r_scr, proj_ref,
                  *, n_invalid, rows_blk):
    j = pl.program_id(1)
    xn = _rms(x_ref[...], g1_ref[...]).astype(BF16)
    for jn in range(PROJ_W // PROJ_TN):
        cols = slice(jn * PROJ_TN, (jn + 1) * PROJ_TN)
        proj_ref[:, cols] = _dot(xn, wall_ref[:, cols])

    @pl.when(j == 0)
    def _():
        s_scr[...] = s0_ref[...]
        r_scr[...] = r0_ref[...]

    row = lax.broadcasted_iota(jnp.int32, (CHUNK, CHUNK), 0)
    col = lax.broadcasted_iota(jnp.int32, (CHUNK, CHUNK), 1)
    causal = row >= col
    tril = causal.astype(F32)
    diff = jnp.maximum(row - col, 0).astype(F32)
    rowf = lax.broadcasted_iota(jnp.int32, (CHUNK, LANES), 0).astype(F32)
    row1 = lax.broadcasted_iota(jnp.int32, (CHUNK, 1), 0)

    def chunk_body(c, carry):
        r0 = pl.multiple_of(c * CHUNK, CHUNK)
        rows = pl.ds(r0, CHUNK)
        valid = ((j * rows_blk + r0 + row1) >= n_invalid).astype(F32)

        ga = proj_ref[rows, OFF_GA:OFF_GA + LANES].astype(BF16)
        pre = _dot(ga, w2_ref[...]) + gb_ref[...]
        log_a = jax.nn.log_sigmoid(pre) * (1.0 / GATE_TAU)
        bcum = _dot(tril, log_a, precision=lax.Precision.HIGHEST)
        b_last = bcum[CHUNK - 1:CHUNK, :]
        gq = proj_ref[rows, OFF_GQ:OFF_GQ + HW] * (DK ** -0.5)
        gk = proj_ref[rows, OFF_GK:OFF_GK + HW] * valid
        q_t = (gq * jnp.exp(bcum)).astype(BF16)
        k_t = (gk * jnp.exp(-bcum)).astype(BF16)
        k_end = (gk * jnp.exp(b_last - bcum)).astype(BF16)
        decay = jnp.exp(b_last)
        for h in range(HEADS):
            sl = slice(h * LANES, (h + 1) * LANES)
            v = proj_ref[rows, OFF_GV + h * DV:OFF_GV + (h + 1) * DV].astype(BF16)
            att = jnp.where(causal, _dot_nt(q_t[:, sl], k_t[:, sl]), 0.0).astype(BF16)
            s_prev = s_scr[h]
            o = _dot(att, v) + _dot_nt(q_t[:, sl], s_prev.astype(BF16))
            s_scr[h] = s_prev * decay[:, sl] + _dot_tn(v, k_end[:, sl])
            o = o * lax.rsqrt(jnp.mean(o * o, axis=-1, keepdims=True) + EPS) * ggain_ref[...]
            gate = proj_ref[rows, OFF_GG + h * DV:OFF_GG + (h + 1) * DV]
            o_ref[rows, h * DV:(h + 1) * DV] = (o * _silu(gate)).astype(o_ref.dtype)

        cosb = cos_ref[rows, :]
        sinb = sin_ref[rows, :]
        rq = proj_ref[rows, OFF_RQ:OFF_RQ + HW]
        rqr = proj_ref[rows, OFF_RQR:OFF_RQR + HW]
        rk = proj_ref[rows, OFF_RK:OFF_RK + HW] * valid
        rkr = proj_ref[rows, OFF_RKR:OFF_RKR + HW] * valid
        q_r = (rq * cosb + rqr * sinb).astype(BF16)
        k_r = (rk * cosb + rkr * sinb) * (DK ** -0.5)
        for h in range(HEADS):
            sl = slice(h * LANES, (h + 1) * LANES)
            lg = LOG_GAMMA[h]
            dmat = jnp.where(causal, jnp.exp(lg * diff), 0.0)
            xi = jnp.exp(lg * (rowf + 1.0))
            zeta = jnp.exp(lg * (CHUNK - 1.0 - rowf))
            v = proj_ref[rows, OFF_RV + h * DV:OFF_RV + (h + 1) * DV].astype(BF16)
            k_h = k_r[:, sl]
            att = (_dot_nt(q_r[:, sl], k_h.astype(BF16)) * dmat).astype(BF16)
            r_prev = r_scr[h]
            o = _dot(att, v) + _dot_nt(q_r[:, sl], r_prev.astype(BF16)) * xi
            r_scr[h] = r_prev * math.exp(lg * CHUNK) + _dot_tn(v, (k_h * zeta).astype(BF16))
            mu = jnp.mean(o, axis=-1, keepdims=True)
            var = jnp.mean(jnp.square(o - mu), axis=-1, keepdims=True)
            o = (o - mu) * lax.rsqrt(var + EPS) * rgain_ref[...] + rbias_ref[...]
            gate = proj_ref[rows, OFF_RG + h * DV:OFF_RG + (h + 1) * DV]
            o_ref[rows, HW + h * DV:HW + (h + 1) * DV] = (o * _silu(gate)).astype(o_ref.dtype)
        return carry

    lax.fori_loop(0, rows_blk // CHUNK, chunk_body, 0)
    s_out_ref[...] = s_scr[...]
    r_out_ref[...] = r_scr[...]


def _mixer(x2d, g1, w_all, cosf, sinf, w2p, gbp, ggain, rgain, rbias, s0, r0, *, batch, n_invalid):
    rows = x2d.shape[0] // batch
    rows_blk = _row_block(rows, 256)
    nblk = rows // rows_blk
    const2 = lambda b, j: (0, 0)
    const3 = lambda b, j: (0, 0, 0)
    state = jax.ShapeDtypeStruct((HEADS, DV, LANES), F32)
    return pl.pallas_call(
        functools.partial(_mixer_kernel, n_invalid=n_invalid, rows_blk=rows_blk),
        grid=(batch, nblk),
        in_specs=[
            pl.BlockSpec((rows_blk, D_MODEL), lambda b, j: (b * nblk + j, 0)),
            pl.BlockSpec((1, D_MODEL), const2),
            pl.BlockSpec((D_MODEL, PROJ_W), const2),
            pl.BlockSpec((rows_blk, HW), lambda b, j: (j, 0)),
            pl.BlockSpec((rows_blk, HW), lambda b, j: (j, 0)),
            pl.BlockSpec((LANES, HW), const2),
            pl.BlockSpec((1, HW), const2),
            pl.BlockSpec((1, DV), const2),
            pl.BlockSpec((1, DV), const2),
            pl.BlockSpec((1, DV), const2),
            pl.BlockSpec((HEADS, DV, LANES), const3),
            pl.BlockSpec((HEADS, DV, LANES), const3),
        ],
        out_specs=[
            pl.BlockSpec((rows_blk, D_MODEL), lambda b, j: (b * nblk + j, 0)),
            pl.BlockSpec((HEADS, DV, LANES), const3),
            pl.BlockSpec((HEADS, DV, LANES), const3),
        ],
        out_shape=[jax.ShapeDtypeStruct((batch * rows, D_MODEL), BF16), state, state],
        scratch_shapes=[pltpu.VMEM((HEADS, DV, LANES), F32), pltpu.VMEM((HEADS, DV, LANES), F32),
                        pltpu.VMEM((rows_blk, PROJ_W), F32)],
        compiler_params=pltpu.CompilerParams(
            dimension_semantics=("arbitrary", "arbitrary"), vmem_limit_bytes=VMEM_LIMIT),
        name="token_mixers",
    )(x2d, g1, w_all, cosf, sinf, w2p, gbp, ggain, rgain, rbias, s0, r0)


def _out_proj_kernel(o_ref, x_ref, w_ref, g_ref, h_ref, xn_ref, xp_ref):
    h = x_ref[...] + _dot(o_ref[...], w_ref[...])
    h_ref[...] = h
    xn = _rms(h, g_ref[...])
    xn_ref[...] = xn
    xp_ref[...] = _pack_pairs(xn[:, :HALF], xn[:, HALF:])


def _out_proj(o, x2d, w_out, gain2):
    n = x2d.shape[0]
    tm = _row_block(n, 512)
    blk = pl.BlockSpec((tm, D_MODEL), lambda i: (i, 0))
    return pl.pallas_call(
        _out_proj_kernel,
        grid=(n // tm,),
        in_specs=[blk, blk, pl.BlockSpec((D_MODEL, D_MODEL), lambda i: (0, 0)),
                  pl.BlockSpec((1, D_MODEL), lambda i: (0, 0))],
        out_specs=[blk, blk, pl.BlockSpec((tm, HALF), lambda i: (i, 0))],
        out_shape=[jax.ShapeDtypeStruct((n, D_MODEL), F32)] * 2 + [jax.ShapeDtypeStruct((n, HALF), jnp.uint32)],
        compiler_params=pltpu.CompilerParams(
            dimension_semantics=("parallel",), vmem_limit_bytes=VMEM_LIMIT),
        name="out_proj_norm",
    )(o, x2d, w_out, gain2)


def _top_k_rows(arrays, payloads=None):
    arrays = list(arrays)
    n = arrays[0].shape[0]
    rowi = lax.broadcasted_iota(jnp.int32, arrays[0].shape, 0)
    vals = [[] for _ in arrays]
    picks = [[] for _ in arrays]
    for _ in range(PEER_TOPK):
        for k, s in enumerate(arrays):
            m = jnp.max(s, axis=0, keepdims=True)
            am = jnp.min(jnp.where(s == m, rowi, n), axis=0, keepdims=True)
            sel = rowi == am
            vals[k].append(m)
            if payloads is None:
                picks[k].append(am)
            else:
                picks[k].append(jnp.min(jnp.where(sel, payloads[k], jnp.iinfo(jnp.int32).max),
                                        axis=0, keepdims=True))
            arrays[k] = jnp.where(sel, -jnp.inf, s)
    return [(jnp.concatenate(v, axis=0), jnp.concatenate(p, axis=0)) for v, p in zip(vals, picks)]


def _candidate_cells():
    cells = [(a, b) for a in range(PEER_TOPK) for b in range(PEER_TOPK) if (a + 1) * (b + 1) <= PEER_TOPK]
    n = -(-len(cells) // 8) * 8
    sel_a = np.zeros((n, PEER_TOPK), np.float32)
    sel_b = np.zeros((n, PEER_TOPK), np.float32)
    pad = np.full((n, 1), -np.inf, np.float32)
    for r, (a, b) in enumerate(cells):
        sel_a[r, a] = 1.0
        sel_b[r, b] = 1.0
        pad[r, 0] = 0.0
    return sel_a, sel_b, pad


def _route_kernel(xn_ref, wq_ref, sk_ref, sa_ref, sb_ref, pad_ref, idx_ref, g_ref, q_scr, i_scr, g_scr):
    tr = xn_ref.shape[0]
    q = _dot(xn_ref[...].astype(BF16), wq_ref[...])
    for hc in range(2 * PEER_HEADS):
        q_scr[hc] = q[:, hc * LANES:(hc + 1) * LANES].astype(BF16)

    def sub_body(sb, carry):
        rows = pl.ds(pl.multiple_of(sb * LANES, LANES), LANES)

        def head_pair(hp, c2):
            heads = [2 * hp, 2 * hp + 1]
            scores = [_dot_nt(sk_ref[2 * h + c], q_scr[2 * h + c, rows, :])
                      for h in heads for c in range(2)]
            first = _top_k_rows(scores)
            sa, sb = sa_ref[...], sb_ref[...]
            cands, cidxs = [], []
            for k in range(len(heads)):
                (v0, i0), (v1, i1) = first[2 * k], first[2 * k + 1]
                cands.append((_dot(sa, v0, precision=lax.Precision.HIGHEST)
                              + _dot(sb, v1, precision=lax.Precision.HIGHEST)) + pad_ref[...])
                cidxs.append((_dot(sa, i0.astype(F32)) * PEER_NKEYS
                              + _dot(sb, i1.astype(F32))).astype(jnp.int32))
            for h, (best, e) in zip(heads, _top_k_rows(cands, cidxs)):
                ex = jnp.exp(best - best[0:1])
                i_scr[h] = e
                g_scr[h] = ex / jnp.sum(ex, axis=0, keepdims=True)
            return c2

        lax.fori_loop(0, PEER_HEADS // 2, head_pair, 0)
        idx_ref[rows, :] = i_scr[...].reshape(PEER_SEL, LANES).T
        g_ref[rows, :] = g_scr[...].reshape(PEER_SEL, LANES).T
        return carry

    lax.fori_loop(0, tr // LANES, sub_body, 0)


def _route(xn2, w_q, subkeys):
    n = xn2.shape[0]
    sel_a, sel_b, pad = _candidate_cells()
    tr = _row_block(n, 512)
    qw = 2 * PEER_HEADS * LANES
    return pl.pallas_call(
        _route_kernel,
        grid=(n // tr,),
        in_specs=[pl.BlockSpec((tr, D_MODEL), lambda i: (i, 0)),
                  pl.BlockSpec((D_MODEL, qw), lambda i: (0, 0)),
                  pl.BlockSpec((2 * PEER_HEADS, PEER_NKEYS, LANES), lambda i: (0, 0, 0)),
                  pl.BlockSpec(sel_a.shape, lambda i: (0, 0)),
                  pl.BlockSpec(sel_b.shape, lambda i: (0, 0)),
                  pl.BlockSpec(pad.shape, lambda i: (0, 0))],
        out_specs=[pl.BlockSpec((tr, PEER_SEL), lambda i: (i, 0))] * 2,
        out_shape=[jax.ShapeDtypeStruct((n, PEER_SEL), jnp.int32),
                   jax.ShapeDtypeStruct((n, PEER_SEL), F32)],
        scratch_shapes=[pltpu.VMEM((2 * PEER_HEADS, tr, LANES), BF16),
                        pltpu.VMEM((PEER_HEADS, PEER_TOPK, LANES), jnp.int32),
                        pltpu.VMEM((PEER_HEADS, PEER_TOPK, LANES), F32)],
        compiler_params=pltpu.CompilerParams(
            dimension_semantics=("parallel",), vmem_limit_bytes=VMEM_LIMIT),
        name="peer_route",
    )(xn2, w_q, subkeys, sel_a, sel_b, pad)


def _sc_worker_id():
    return lax.axis_index("s") * SC_CORES + lax.axis_index("c")


def _sc_mesh():
    return plsc.VectorSubcoreMesh(core_axis_name="c", subcore_axis_name="s")


def _sc_gather_ring(table_hbm, idx_v, bufs, sems, unit, compute):
    def gather(g, k):
        off = pl.multiple_of(g * SC_ROWS, SC_ROWS)
        return pltpu.make_async_copy(table_hbm.at[idx_v.at[pl.ds(off, SC_ROWS)]], bufs[k], sems[k])

    for k in range(SC_NBUF):
        gather(k, k).start()

    units_per_iter = SC_NBUF // unit

    def ring_iter(i, carry):
        for m in range(units_per_iter):
            ks = range(m * unit, (m + 1) * unit)
            for k in ks:
                gather(i * SC_NBUF + k, k).wait()
            compute(i * units_per_iter + m, [bufs[k] for k in ks])
            for k in ks:
                g_next = (i + 1) * SC_NBUF + k

                @pl.when(g_next < SC_GATHERS)
                def _():
                    gather(g_next, k).start()
        return carry

    lax.fori_loop(0, SC_GATHERS // SC_NBUF, ring_iter, 0)


def _bf16_bits(x):
    b = lax.bitcast_convert_type(x, jnp.uint32)
    return (b + jnp.uint32(0x7FFF) + ((b >> 16) & jnp.uint32(1))) >> 16


def _pack_pairs(lo, hi):
    return _bf16_bits(lo) | (_bf16_bits(hi) << 16)


def _pack_table(t):
    return _pack_pairs(t[:, :HALF], t[:, HALF:]).reshape(t.shape[0], ROW_TILES, LANES)


def _mul_add_words(w1, m1, w2, m2):
    bf = lambda a: plsc.bitcast(a, jnp.bfloat16)
    p = plsc.bitcast(bf(w1) * bf(m1) + bf(w2) * bf(m2), jnp.uint32)
    return lax.bitcast_convert_type(p << 16, F32), lax.bitcast_convert_type(p, F32)


def _sc_scratch(staged_cols, out_cols):
    return ([pltpu.VMEM((SC_TOK_BATCH * PEER_SEL,), jnp.int32),
             pltpu.VMEM((SC_TOK_BATCH, staged_cols), jnp.uint32),
             pltpu.VMEM((SC_TOK_BATCH, out_cols), F32)]
            + [pltpu.VMEM((SC_ROWS, ROW_TILES, LANES), jnp.uint32) for _ in range(SC_NBUF)]
            + [pltpu.SemaphoreType.DMA for _ in range(SC_NBUF)])


def _sc_batches(tpw, idx_hbm, staged_hbm, out_hbm, idx_v, staged_v, out_v, run):
    tok0 = _sc_worker_id() * tpw

    def batch(tb, carry):
        t0 = pl.multiple_of(tok0 + tb * SC_TOK_BATCH, SC_TOK_BATCH)
        pltpu.sync_copy(idx_hbm.at[pl.ds(pl.multiple_of(t0 * PEER_SEL, SC_TOK_BATCH * PEER_SEL),
                                         SC_TOK_BATCH * PEER_SEL)], idx_v)
        pltpu.sync_copy(staged_hbm.at[pl.ds(t0, SC_TOK_BATCH)], staged_v)
        run()
        pltpu.sync_copy(out_v, out_hbm.at[pl.ds(t0, SC_TOK_BATCH)])
        return carry

    lax.fori_loop(0, tpw // SC_TOK_BATCH, batch, 0)


def _peer_dot_sc(u3, idx_flat, xn2):
    t = xn2.shape[0]
    unit = 16 // SC_ROWS
    units_per_tok = PEER_SEL // 16

    @functools.partial(
        pl.kernel,
        mesh=_sc_mesh(),
        out_type=jax.ShapeDtypeStruct((t, PEER_SEL), F32),
        scratch_types=[pltpu.VMEM((16 * SC_LANES,), F32)] + _sc_scratch(HALF, PEER_SEL),
        compiler_params=pltpu.CompilerParams(needs_layout_passes=False),
        name="peer_dot",
    )
    def body(u_hbm, idx_hbm, x_hbm, out_hbm, red_v, idx_v, x_v, out_v, *rest):
        bufs, sems = rest[:SC_NBUF], rest[SC_NBUF:]

        def compute(u, ub):
            tl = u // units_per_tok
            rb = (u % units_per_tok) * 16

            def cbody(c, acc):
                acc = list(acc)
                c4 = c // (LANES // (2 * SC_LANES))
                lane0 = pl.multiple_of((c % (LANES // (2 * SC_LANES))) * 2 * SC_LANES, 2 * SC_LANES)
                col = pl.multiple_of(c * 2 * SC_LANES, 2 * SC_LANES)
                xw1 = x_v[tl, pl.ds(col, SC_LANES)]
                xw2 = x_v[tl, pl.ds(col + SC_LANES, SC_LANES)]
                for q, buf in enumerate(ub):
                    for r in range(SC_ROWS):
                        a = q * SC_ROWS + r
                        lo, hi = _mul_add_words(buf[r, c4, pl.ds(lane0, SC_LANES)], xw1,
                                                buf[r, c4, pl.ds(lane0 + SC_LANES, SC_LANES)], xw2)
                        acc[a] = acc[a] + lo + hi
                return tuple(acc)

            acc = lax.fori_loop(0, HALF // (2 * SC_LANES), cbody,
                                tuple(jnp.zeros((SC_LANES,), F32) for _ in range(16)))
            for a in range(16):
                red_v[pl.ds(a * SC_LANES, SC_LANES)] = acc[a]
            lane_base = lax.iota(jnp.int32, SC_LANES) * SC_LANES
            tot = plsc.load_gather(red_v, [lane_base])
            for k in range(1, SC_LANES):
                tot = tot + plsc.load_gather(red_v, [lane_base + k])
            out_v[tl, pl.ds(pl.multiple_of(rb, 16), 16)] = tot

        _sc_batches(t // SC_WORKERS, idx_hbm, x_hbm, out_hbm, idx_v, x_v, out_v,
                    lambda: _sc_gather_ring(u_hbm, idx_v, bufs, sems, unit, compute))

    return body(u3, idx_flat, xn2)


def _peer_combine_sc(v3, idx_flat, coef_b):
    t = coef_b.shape[0]
    unit = 32 // SC_ROWS
    units_per_tok = PEER_SEL // 32

    @functools.partial(
        pl.kernel,
        mesh=_sc_mesh(),
        out_type=jax.ShapeDtypeStruct((t, D_MODEL), F32),
        scratch_types=_sc_scratch(PEER_SEL * SC_LANES, D_MODEL),
        compiler_params=pltpu.CompilerParams(needs_layout_passes=False),
        name="peer_combine",
    )
    def body(v_hbm, idx_hbm, coef_hbm, out_hbm, idx_v, coef_v, out_v, *rest):
        bufs, sems = rest[:SC_NBUF], rest[SC_NBUF:]

        def compute(u, ub):
            tl = u // units_per_tok
            rb = (u % units_per_tok) * 32
            for g in range(ROW_TILES):
                cols = ([pl.ds(g * LANES + l * SC_LANES, SC_LANES) for l in range(8)]
                        + [pl.ds(HALF + g * LANES + l * SC_LANES, SC_LANES) for l in range(8)])

                def rbody(rp, acc):
                    acc = list(acc)
                    r = 2 * rp
                    for q, buf in enumerate(ub):
                        co = pl.multiple_of((rb + q * SC_ROWS + r) * SC_LANES, 2 * SC_LANES)
                        cf1 = coef_v[tl, pl.ds(co, SC_LANES)]
                        cf2 = coef_v[tl, pl.ds(co + SC_LANES, SC_LANES)]
                        for l in range(8):
                            lo, hi = _mul_add_words(buf[r, g, pl.ds(l * SC_LANES, SC_LANES)], cf1,
                                                    buf[r + 1, g, pl.ds(l * SC_LANES, SC_LANES)], cf2)
                            acc[l] = acc[l] + lo
                            acc[8 + l] = acc[8 + l] + hi
                    return tuple(acc)

                acc = lax.fori_loop(0, SC_ROWS // 2, rbody, tuple(out_v[tl, c] for c in cols))
                for k in range(16):
                    out_v[tl, cols[k]] = acc[k]

        def run():
            def zero(tl, c2):
                for k in range(D_MODEL // SC_LANES):
                    out_v[tl, pl.ds(k * SC_LANES, SC_LANES)] = jnp.zeros((SC_LANES,), F32)
                return c2

            lax.fori_loop(0, SC_TOK_BATCH, zero, 0)
            _sc_gather_ring(v_hbm, idx_v, bufs, sems, unit, compute)

        _sc_batches(t // SC_WORKERS, idx_hbm, coef_hbm, out_hbm, idx_v, coef_v, out_v, run)

    return body(v3, idx_flat, coef_b)


def _coef_kernel(act_ref, g_ref, e_ref, coef_ref):
    act = act_ref[...]
    gelu = 0.5 * act * (1.0 + lax.erf(act * np.float32(math.sqrt(0.5))))
    c = _dot(g_ref[...] * gelu, e_ref[...], precision=lax.Precision.HIGHEST)
    coef_ref[...] = _pack_pairs(c, c)


def _coef(act, g):
    n = act.shape[0]
    tr = _row_block(n, 512)
    w = PEER_SEL * SC_LANES
    expand = (jnp.arange(w, dtype=jnp.int32)[None, :] // SC_LANES
              == jnp.arange(PEER_SEL, dtype=jnp.int32)[:, None]).astype(F32)
    blk = pl.BlockSpec((tr, PEER_SEL), lambda i: (i, 0))
    return pl.pallas_call(
        _coef_kernel,
        grid=(n // tr,),
        in_specs=[blk, blk, pl.BlockSpec((PEER_SEL, w), lambda i: (0, 0))],
        out_specs=pl.BlockSpec((tr, w), lambda i: (i, 0)),
        out_shape=jax.ShapeDtypeStruct((n, w), jnp.uint32),
        compiler_params=pltpu.CompilerParams(
            dimension_semantics=("parallel",), vmem_limit_bytes=VMEM_LIMIT),
        name="peer_coef",
    )(act, g, expand)


def _final_kernel(h_ref, p_ref, g_ref, o_ref):
    o_ref[...] = _rms(h_ref[...] + p_ref[...], g_ref[...])


def _final(h2, peer_out, gain):
    n = h2.shape[0]
    tm = _row_block(n, 1024)
    blk = pl.BlockSpec((tm, D_MODEL), lambda i: (i, 0))
    return pl.pallas_call(
        _final_kernel,
        grid=(n // tm,),
        in_specs=[blk, blk, pl.BlockSpec((1, D_MODEL), lambda i: (0, 0))],
        out_specs=blk,
        out_shape=jax.ShapeDtypeStruct((n, D_MODEL), F32),
        compiler_params=pltpu.CompilerParams(
            dimension_semantics=("parallel",), vmem_limit_bytes=VMEM_LIMIT),
        name="final_norm",
    )(h2, peer_out, gain)


def _pad_heads(w, dk):
    lead = w.shape[:-1]
    w = w.reshape(lead + (HEADS, dk))
    w = jnp.pad(w, [(0, 0)] * len(lead) + [(0, 0), (0, LANES - dk)])
    return w.reshape(lead + (HEADS * LANES,))


def _rot_heads(w, dk):
    lead = w.shape[:-1]
    w = w.reshape(lead + (HEADS, 2, dk // 2))
    w = jnp.stack([-w[..., 1, :], w[..., 0, :]], axis=-2)
    return w.reshape(lead + (HEADS * dk,))


def _pack_in_weights(w_in):
    gq, gk, gv, gg, ga, rq, rk, rv, rg = jnp.split(w_in, np.cumsum(IN_SPLITS)[:-1].tolist(), axis=-1)
    ga = jnp.pad(ga, ((0, 0), (0, 2 * LANES - GATE_RANK)))
    return jnp.concatenate(
        [_pad_heads(gq, DK), _pad_heads(gk, DK), gv, gg,
         _pad_heads(rq, DK), _pad_heads(rk, DK), rv, rg,
         _pad_heads(_rot_heads(rq, DK), DK), _pad_heads(_rot_heads(rk, DK), DK), ga],
        axis=-1).astype(BF16)


def _rope_tables(pos):
    inv_freq = ROPE_BASE ** (-jnp.arange(0, DK, 2, dtype=F32) / DK)
    ang = pos[:, None] * inv_freq[None, :]
    full = lambda t: _pad_heads(jnp.tile(t, (1, 2 * HEADS)), DK)
    return full(jnp.cos(ang)), full(jnp.sin(ang))


def _segment_lengths(s, first):
    if first:
        lengths = [s * k // 16 for k in FIRST_SEQ_RAMP]
        assert sum(lengths) == s
    else:
        lengths = [s // SEGS_PER_SEQ] * SEGS_PER_SEQ
    assert all(n % SC_CALL_TOKENS == 0 for n in lengths)
    return lengths


def kernel(x, meta_tokens, norm1_gain, w_in, gla_gate_w2, gla_gate_b, gla_norm_gain, ret_norm_gain, ret_norm_bias, w_out, norm2_gain, peer_w_q, peer_subkeys, peer_u, peer_v, final_gain):
    b, s, d = x.shape
    assert d == D_MODEL and (s // SEGS_PER_SEQ) % SC_CALL_TOKENS == 0
    assert norm1_gain.shape[0] == 1, "single-layer block"
    x2d = x.reshape(b * s, d)

    w_all = _pack_in_weights(w_in[0])
    w2p = jnp.pad(_pad_heads(gla_gate_w2[0], DK), ((0, LANES - GATE_RANK), (0, 0))).astype(BF16)
    gbp = _pad_heads(gla_gate_b[0][None, :], DK)
    g1 = norm1_gain[0][None, :]
    ggain = gla_norm_gain[0][None, :]
    rgain = ret_norm_gain[0][None, :]
    rbias = ret_norm_bias[0][None, :]

    h0 = jnp.concatenate([jnp.zeros((N_PAD, d), x.dtype), meta_tokens.astype(x.dtype)], axis=0)
    cos0, sin0 = _rope_tables(jnp.arange(CHUNK, dtype=F32) - N_PAD)
    cos1, sin1 = _rope_tables(jnp.arange(s, dtype=F32) + N_META)
    zero_state = jnp.zeros((HEADS, DV, LANES), F32)
    _, s_meta, r_meta = _mixer(h0, g1, w_all, cos0, sin0, w2p, gbp, ggain, rgain, rbias,
                               zero_state, zero_state, batch=1, n_invalid=N_PAD)

    w_out_b = w_out[0].astype(BF16)
    g2 = norm2_gain[0][None, :]
    w_q = peer_w_q[0].astype(BF16)
    subkeys = peer_subkeys[0].reshape(2 * PEER_HEADS, PEER_NKEYS, PEER_DKEY // 2).astype(BF16)
    u3 = _pack_table(peer_u[0])
    v3 = _pack_table(peer_v[0])
    gf = final_gain[None, :]

    outs = []
    prev_g = None
    pending = []
    for bi in range(b):
        s_state, r_state = s_meta, r_meta
        start = 0
        for seg in _segment_lengths(s, first=bi == 0):
            xs = x2d[bi * s + start:bi * s + start + seg]
            pos = slice(start, start + seg)
            start += seg
            if prev_g is not None:
                xs, _ = lax.optimization_barrier((xs, prev_g))
            o, s_state, r_state = _mixer(xs, g1, w_all, cos1[pos], sin1[pos], w2p, gbp,
                                         ggain, rgain, rbias, s_state, r_state, batch=1, n_invalid=0)
            h2, xn2, xn2_pairs = _out_proj(o, xs, w_out_b, g2)
            idx, g = _route(xn2, w_q, subkeys)
            prev_g = g
            idx_flat = idx.reshape(-1)
            if len(pending) >= SC_LOOKAHEAD:
                h2_old, peer_old = pending.pop(0)
                idx_flat, peer_old = lax.optimization_barrier((idx_flat, peer_old))
                outs.append(_final(h2_old, peer_old, gf))
            act = _peer_dot_sc(u3, idx_flat, xn2_pairs)
            pending.append((h2, _peer_combine_sc(v3, idx_flat, _coef(act, g))))
    outs.extend(_final(h2_old, peer_old, gf) for h2_old, peer_old in pending)
    return jnp.concatenate(outs, axis=0).reshape(b, s, d)
```

```python
import functools
import math

import jax
import jax.numpy as jnp
import numpy as np
from jax import lax
from jax.experimental import pallas as pl
from jax.experimental.pallas import tpu as pltpu
from jax.experimental.pallas import tpu_sc as plsc

F32 = jnp.float32
BF16 = jnp.bfloat16

D_MODEL = 1024
N_META = 16
CHUNK = 64
N_PAD = CHUNK - N_META
EPS = 1e-6

HEADS = 4
DV = 128
DK = 64
LANES = 128
GATE_RANK = 16
GATE_TAU = 16.0
ROPE_BASE = 10000.0
IN_SPLITS = (HEADS * DK, HEADS * DK, HEADS * DV, HEADS * DV, GATE_RANK,
             HEADS * DK, HEADS * DK, HEADS * DV, HEADS * DV)

PEER_HEADS = 8
PEER_NKEYS = 128
PEER_DKEY = 256
PEER_TOPK = 16
PEER_SEL = PEER_HEADS * PEER_TOPK

HW = HEADS * LANES
OFF_GQ, OFF_GK, OFF_GV, OFF_GG = 0, HW, 2 * HW, 3 * HW
OFF_RQ, OFF_RK, OFF_RV, OFF_RG = 4 * HW, 5 * HW, 6 * HW, 7 * HW
OFF_RQR, OFF_RKR, OFF_GA = 8 * HW, 9 * HW, 10 * HW
PROJ_W = 10 * HW + 2 * LANES
PROJ_TN = PROJ_W // 3

LOG_GAMMA = tuple(math.log(1.0 - 2.0 ** (-5.0 - h)) for h in range(HEADS))

SC_CORES, SC_SUBCORES, SC_LANES = 2, 16, 16
SC_WORKERS = SC_CORES * SC_SUBCORES
SC_TOK_BATCH = 16
SC_ROWS = 16
SC_NBUF = 8
SC_GATHERS = SC_TOK_BATCH * PEER_SEL // SC_ROWS
HALF = D_MODEL // 2
ROW_TILES = HALF // LANES

SC_CALL_TOKENS = SC_WORKERS * SC_TOK_BATCH
SEGS_PER_SEQ = 2
FIRST_SEQ_RAMP = (1, 1, 2, 3, 4, 5)
SC_LOOKAHEAD = 2

MIXER_ROWS = 256
DENSE_ROWS = 512
VMEM_LIMIT = 48 * 1024 * 1024


def _row_block(n, target):
    blk = math.gcd(n, target)
    assert blk % 8 == 0 or blk == n, (n, target)
    return blk


def _dot(a, b, **kw):
    return jnp.dot(a, b, preferred_element_type=F32, **kw)


def _dot_nt(a, b):
    return lax.dot_general(a, b, (((1,), (1,)), ((), ())), preferred_element_type=F32)


def _dot_tn(a, b):
    return lax.dot_general(a, b, (((0,), (0,)), ((), ())), preferred_element_type=F32)


def _rms(x, g):
    return x * lax.rsqrt(jnp.mean(x * x, axis=-1, keepdims=True) + EPS) * g


def _silu(x):
    return x * jax.nn.sigmoid(x)


def _mixer_kernel(x_ref, g1_ref, wall_ref, cos_ref, sin_ref, w2_ref, gb_ref, ggain_ref, rgain_ref,
                  rbias_ref, s0_ref, r0_ref, o_ref, s_out_ref, r_out_ref, s_scr, r_scr, proj_ref,
                  *, n_invalid, rows_blk):
    j = pl.program_id(1)
    xn = _rms(x_ref[...], g1_ref[...]).astype(BF16)
    for jn in range(PROJ_W // PROJ_TN):
        cols = slice(jn * PROJ_TN, (jn + 1) * PROJ_TN)
        proj_ref[:, cols] = _dot(xn, wall_ref[:, cols])

    @pl.when(j == 0)
    def _():
        s_scr[...] = s0_ref[...]
        r_scr[...] = r0_ref[...]

    row = lax.broadcasted_iota(jnp.int32, (CHUNK, CHUNK), 0)
    col = lax.broadcasted_iota(jnp.int32, (CHUNK, CHUNK), 1)
    causal = row >= col
    tril = causal.astype(F32)
    diff = jnp.maximum(row - col, 0).astype(F32)
    rowf = lax.broadcasted_iota(jnp.int32, (CHUNK, LANES), 0).astype(F32)
    row1 = lax.broadcasted_iota(jnp.int32, (CHUNK, 1), 0)

    def chunk_body(c, carry):
        r0 = pl.multiple_of(c * CHUNK, CHUNK)
        rows = pl.ds(r0, CHUNK)
        valid = ((j * rows_blk + r0 + row1) >= n_invalid).astype(F32)

        ga = proj_ref[rows, OFF_GA:OFF_GA + LANES].astype(BF16)
        pre = _dot(ga, w2_ref[...]) + gb_ref[...]
        log_a = jax.nn.log_sigmoid(pre) * (1.0 / GATE_TAU)
        bcum = _dot(tril, log_a, precision=lax.Precision.HIGHEST)
        b_last = bcum[CHUNK - 1:CHUNK, :]
        gq = proj_ref[rows, OFF_GQ:OFF_GQ + HW] * (DK ** -0.5)
        gk = proj_ref[rows, OFF_GK:OFF_GK + HW] * valid
        q_t = (gq * jnp.exp(bcum)).astype(BF16)
        k_t = (gk * jnp.exp(-bcum)).astype(BF16)
        k_end = (gk * jnp.exp(b_last - bcum)).astype(BF16)
        decay = jnp.exp(b_last)
        for h in range(HEADS):
            sl = slice(h * LANES, (h + 1) * LANES)
            v = proj_ref[rows, OFF_GV + h * DV:OFF_GV + (h + 1) * DV].astype(BF16)
            att = jnp.where(causal, _dot_nt(q_t[:, sl], k_t[:, sl]), 0.0).astype(BF16)
            s_prev = s_scr[h]
            o = _dot(att, v) + _dot_nt(q_t[:, sl], s_prev.astype(BF16))
            s_scr[h] = s_prev * decay[:, sl] + _dot_tn(v, k_end[:, sl])
            o = o * lax.rsqrt(jnp.mean(o * o, axis=-1, keepdims=True) + EPS) * ggain_ref[...]
            gate = proj_ref[rows, OFF_GG + h * DV:OFF_GG + (h + 1) * DV]
            o_ref[rows, h * DV:(h + 1) * DV] = (o * _silu(gate)).astype(o_ref.dtype)

        cosb = cos_ref[rows, :]
        sinb = sin_ref[rows, :]
        rq = proj_ref[rows, OFF_RQ:OFF_RQ + HW]
        rqr = proj_ref[rows, OFF_RQR:OFF_RQR + HW]
        rk = proj_ref[rows, OFF_RK:OFF_RK + HW] * valid
        rkr = proj_ref[rows, OFF_RKR:OFF_RKR + HW] * valid
        q_r = (rq * cosb + rqr * sinb).astype(BF16)
        k_r = (rk * cosb + rkr * sinb) * (DK ** -0.5)
        for h in range(HEADS):
            sl = slice(h * LANES, (h + 1) * LANES)
            lg = LOG_GAMMA[h]
            dmat = jnp.where(causal, jnp.exp(lg * diff), 0.0)
            xi = jnp.exp(lg * (rowf + 1.0))
            zeta = jnp.exp(lg * (CHUNK - 1.0 - rowf))
            v = proj_ref[rows, OFF_RV + h * DV:OFF_RV + (h + 1) * DV].astype(BF16)
            k_h = k_r[:, sl]
            att = (_dot_nt(q_r[:, sl], k_h.astype(BF16)) * dmat).astype(BF16)
            r_prev = r_scr[h]
            o = _dot(att, v) + _dot_nt(q_r[:, sl], r_prev.astype(BF16)) * xi
            r_scr[h] = r_prev * math.exp(lg * CHUNK) + _dot_tn(v, (k_h * zeta).astype(BF16))
            mu = jnp.mean(o, axis=-1, keepdims=True)
            var = jnp.mean(jnp.square(o - mu), axis=-1, keepdims=True)
            o = (o - mu) * lax.rsqrt(var + EPS) * rgain_ref[...] + rbias_ref[...]
            gate = proj_ref[rows, OFF_RG + h * DV:OFF_RG + (h + 1) * DV]
            o_ref[rows, HW + h * DV:HW + (h + 1) * DV] = (o * _silu(gate)).astype(o_ref.dtype)
        return carry

    lax.fori_loop(0, rows_blk // CHUNK, chunk_body, 0)
    s_out_ref[...] = s_scr[...]
    r_out_ref[...] = r_scr[...]


def _mixer(x2d, g1, w_all, cosf, sinf, w2p, gbp, ggain, rgain, rbias, s0, r0, *, batch, n_invalid):
    rows = x2d.shape[0] // batch
    rows_blk = _row_block(rows, MIXER_ROWS)
    nblk = rows // rows_blk
    const2 = lambda b, j: (0, 0)
    const3 = lambda b, j: (0, 0, 0)
    state = jax.ShapeDtypeStruct((HEADS, DV, LANES), F32)
    return pl.pallas_call(
        functools.partial(_mixer_kernel, n_invalid=n_invalid, rows_blk=rows_blk),
        grid=(batch, nblk),
        in_specs=[
            pl.BlockSpec((rows_blk, D_MODEL), lambda b, j: (b * nblk + j, 0)),
            pl.BlockSpec((1, D_MODEL), const2),
            pl.BlockSpec((D_MODEL, PROJ_W), const2),
            pl.BlockSpec((rows_blk, HW), lambda b, j: (j, 0)),
            pl.BlockSpec((rows_blk, HW), lambda b, j: (j, 0)),
            pl.BlockSpec((LANES, HW), const2),
            pl.BlockSpec((1, HW), const2),
            pl.BlockSpec((1, DV), const2),
            pl.BlockSpec((1, DV), const2),
            pl.BlockSpec((1, DV), const2),
            pl.BlockSpec((HEADS, DV, LANES), const3),
            pl.BlockSpec((HEADS, DV, LANES), const3),
        ],
        out_specs=[
            pl.BlockSpec((rows_blk, D_MODEL), lambda b, j: (b * nblk + j, 0)),
            pl.BlockSpec((HEADS, DV, LANES), const3),
            pl.BlockSpec((HEADS, DV, LANES), const3),
        ],
        out_shape=[jax.ShapeDtypeStruct((batch * rows, D_MODEL), BF16), state, state],
        scratch_shapes=[pltpu.VMEM((HEADS, DV, LANES), F32), pltpu.VMEM((HEADS, DV, LANES), F32),
                        pltpu.VMEM((rows_blk, PROJ_W), F32)],
        compiler_params=pltpu.CompilerParams(
            dimension_semantics=("arbitrary", "arbitrary"), vmem_limit_bytes=VMEM_LIMIT),
        name="token_mixers",
    )(x2d, g1, w_all, cosf, sinf, w2p, gbp, ggain, rgain, rbias, s0, r0)


def _out_proj_kernel(o_ref, x_ref, w_ref, g_ref, h_ref, xn_ref, xp_ref):
    h = x_ref[...] + _dot(o_ref[...], w_ref[...])
    h_ref[...] = h
    xn = _rms(h, g_ref[...])
    xn_ref[...] = xn
    xp_ref[...] = _pack_pairs(xn[:, :HALF], xn[:, HALF:])


def _out_proj(o, x2d, w_out, gain2):
    n = x2d.shape[0]
    tm = _row_block(n, DENSE_ROWS)
    blk = pl.BlockSpec((tm, D_MODEL), lambda i: (i, 0))
    return pl.pallas_call(
        _out_proj_kernel,
        grid=(n // tm,),
        in_specs=[blk, blk, pl.BlockSpec((D_MODEL, D_MODEL), lambda i: (0, 0)),
                  pl.BlockSpec((1, D_MODEL), lambda i: (0, 0))],
        out_specs=[blk, blk, pl.BlockSpec((tm, HALF), lambda i: (i, 0))],
        out_shape=[jax.ShapeDtypeStruct((n, D_MODEL), F32)] * 2 + [jax.ShapeDtypeStruct((n, HALF), jnp.uint32)],
        compiler_params=pltpu.CompilerParams(
            dimension_semantics=("parallel",), vmem_limit_bytes=VMEM_LIMIT),
        name="out_proj_norm",
    )(o, x2d, w_out, gain2)


def _top_k_rows(arrays, payloads=None):
    arrays = list(arrays)
    n = arrays[0].shape[0]
    rowi = lax.broadcasted_iota(jnp.int32, arrays[0].shape, 0)
    vals = [[] for _ in arrays]
    picks = [[] for _ in arrays]
    for _ in range(PEER_TOPK):
        for k, s in enumerate(arrays):
            m = jnp.max(s, axis=0, keepdims=True)
            am = jnp.min(jnp.where(s == m, rowi, n), axis=0, keepdims=True)
            sel = rowi == am
            vals[k].append(m)
            if payloads is None:
                picks[k].append(am)
            else:
                picks[k].append(jnp.min(jnp.where(sel, payloads[k], jnp.iinfo(jnp.int32).max),
                                        axis=0, keepdims=True))
            arrays[k] = jnp.where(sel, -jnp.inf, s)
    return [(jnp.concatenate(v, axis=0), jnp.concatenate(p, axis=0)) for v, p in zip(vals, picks)]


def _candidate_cells():
    cells = [(a, b) for a in range(PEER_TOPK) for b in range(PEER_TOPK) if (a + 1) * (b + 1) <= PEER_TOPK]
    n = -(-len(cells) // 8) * 8
    sel_a = np.zeros((n, PEER_TOPK), np.float32)
    sel_b = np.zeros((n, PEER_TOPK), np.float32)
    pad = np.full((n, 1), -np.inf, np.float32)
    for r, (a, b) in enumerate(cells):
        sel_a[r, a] = 1.0
        sel_b[r, b] = 1.0
        pad[r, 0] = 0.0
    return sel_a, sel_b, pad


def _route_kernel(xn_ref, wq_ref, sk_ref, sa_ref, sb_ref, pad_ref, idx_ref, g_ref, q_scr, i_scr, g_scr):
    tr = xn_ref.shape[0]
    q = _dot(xn_ref[...].astype(BF16), wq_ref[...])
    for hc in range(2 * PEER_HEADS):
        q_scr[hc] = q[:, hc * LANES:(hc + 1) * LANES].astype(BF16)

    def sub_body(sb, carry):
        rows = pl.ds(pl.multiple_of(sb * LANES, LANES), LANES)

        def head_pair(hp, c2):
            heads = [2 * hp, 2 * hp + 1]
            scores = [_dot_nt(sk_ref[2 * h + c], q_scr[2 * h + c, rows, :])
                      for h in heads for c in range(2)]
            first = _top_k_rows(scores)
            sa, sb = sa_ref[...], sb_ref[...]
            cands, cidxs = [], []
            for k in range(len(heads)):
                (v0, i0), (v1, i1) = first[2 * k], first[2 * k + 1]
                cands.append((_dot(sa, v0, precision=lax.Precision.HIGHEST)
                              + _dot(sb, v1, precision=lax.Precision.HIGHEST)) + pad_ref[...])
                cidxs.append((_dot(sa, i0.astype(F32)) * PEER_NKEYS
                              + _dot(sb, i1.astype(F32))).astype(jnp.int32))
            for h, (best, e) in zip(heads, _top_k_rows(cands, cidxs)):
                ex = jnp.exp(best - best[0:1])
                i_scr[h] = e
                g_scr[h] = ex / jnp.sum(ex, axis=0, keepdims=True)
            return c2

        lax.fori_loop(0, PEER_HEADS // 2, head_pair, 0)
        idx_ref[rows, :] = i_scr[...].reshape(PEER_SEL, LANES).T
        g_ref[rows, :] = g_scr[...].reshape(PEER_SEL, LANES).T
        return carry

    lax.fori_loop(0, tr // LANES, sub_body, 0)


def _route(xn2, w_q, subkeys):
    n = xn2.shape[0]
    sel_a, sel_b, pad = _candidate_cells()
    tr = _row_block(n, DENSE_ROWS)
    qw = 2 * PEER_HEADS * LANES
    return pl.pallas_call(
        _route_kernel,
        grid=(n // tr,),
        in_specs=[pl.BlockSpec((tr, D_MODEL), lambda i: (i, 0)),
                  pl.BlockSpec((D_MODEL, qw), lambda i: (0, 0)),
                  pl.BlockSpec((2 * PEER_HEADS, PEER_NKEYS, LANES), lambda i: (0, 0, 0)),
                  pl.BlockSpec(sel_a.shape, lambda i: (0, 0)),
                  pl.BlockSpec(sel_b.shape, lambda i: (0, 0)),
                  pl.BlockSpec(pad.shape, lambda i: (0, 0))],
        out_specs=[pl.BlockSpec((tr, PEER_SEL), lambda i: (i, 0))] * 2,
        out_shape=[jax.ShapeDtypeStruct((n, PEER_SEL), jnp.int32),
                   jax.ShapeDtypeStruct((n, PEER_SEL), F32)],
        scratch_shapes=[pltpu.VMEM((2 * PEER_HEADS, tr, LANES), BF16),
                        pltpu.VMEM((PEER_HEADS, PEER_TOPK, LANES), jnp.int32),
                        pltpu.VMEM((PEER_HEADS, PEER_TOPK, LANES), F32)],
        compiler_params=pltpu.CompilerParams(
            dimension_semantics=("parallel",), vmem_limit_bytes=VMEM_LIMIT),
        name="peer_route",
    )(xn2, w_q, subkeys, sel_a, sel_b, pad)


def _sc_worker_id():
    return lax.axis_index("s") * SC_CORES + lax.axis_index("c")


def _sc_mesh():
    return plsc.VectorSubcoreMesh(core_axis_name="c", subcore_axis_name="s")


def _sc_gather_ring(table_hbm, idx_v, bufs, sems, unit, compute):
    def gather(g, k):
        off = pl.multiple_of(g * SC_ROWS, SC_ROWS)
        return pltpu.make_async_copy(table_hbm.at[idx_v.at[pl.ds(off, SC_ROWS)]], bufs[k], sems[k])

    for k in range(SC_NBUF):
        gather(k, k).start()

    units_per_iter = SC_NBUF // unit

    def ring_iter(i, carry):
        for m in range(units_per_iter):
            ks = range(m * unit, (m + 1) * unit)
            for k in ks:
                gather(i * SC_NBUF + k, k).wait()
            compute(i * units_per_iter + m, [bufs[k] for k in ks])
            for k in ks:
                g_next = (i + 1) * SC_NBUF + k

                @pl.when(g_next < SC_GATHERS)
                def _():
                    gather(g_next, k).start()
        return carry

    lax.fori_loop(0, SC_GATHERS // SC_NBUF, ring_iter, 0)


def _bf16_bits(x):
    b = lax.bitcast_convert_type(x, jnp.uint32)
    return (b + jnp.uint32(0x7FFF) + ((b >> 16) & jnp.uint32(1))) >> 16


def _pack_pairs(lo, hi):
    return _bf16_bits(lo) | (_bf16_bits(hi) << 16)


def _pack_table(t):
    return _pack_pairs(t[:, :HALF], t[:, HALF:]).reshape(t.shape[0], ROW_TILES, LANES)


def _mul_add_words(w1, m1, w2, m2):
    bf = lambda a: plsc.bitcast(a, jnp.bfloat16)
    p = plsc.bitcast(bf(w1) * bf(m1) + bf(w2) * bf(m2), jnp.uint32)
    return lax.bitcast_convert_type(p << 16, F32), lax.bitcast_convert_type(p, F32)


def _sc_scratch(staged_cols, out_cols):
    return ([pltpu.VMEM((SC_TOK_BATCH * PEER_SEL,), jnp.int32),
             pltpu.VMEM((SC_TOK_BATCH, staged_cols), jnp.uint32),
             pltpu.VMEM((SC_TOK_BATCH, out_cols), F32)]
            + [pltpu.VMEM((SC_ROWS, ROW_TILES, LANES), jnp.uint32) for _ in range(SC_NBUF)]
            + [pltpu.SemaphoreType.DMA for _ in range(SC_NBUF)])


def _sc_batches(tpw, idx_hbm, staged_hbm, out_hbm, idx_v, staged_v, out_v, run):
    tok0 = _sc_worker_id() * tpw

    def batch(tb, carry):
        t0 = pl.multiple_of(tok0 + tb * SC_TOK_BATCH, SC_TOK_BATCH)
        pltpu.sync_copy(idx_hbm.at[pl.ds(pl.multiple_of(t0 * PEER_SEL, SC_TOK_BATCH * PEER_SEL),
                                         SC_TOK_BATCH * PEER_SEL)], idx_v)
        pltpu.sync_copy(staged_hbm.at[pl.ds(t0, SC_TOK_BATCH)], staged_v)
        run()
        pltpu.sync_copy(out_v, out_hbm.at[pl.ds(t0, SC_TOK_BATCH)])
        return carry

    lax.fori_loop(0, tpw // SC_TOK_BATCH, batch, 0)


def _peer_dot_sc(u3, idx_flat, xn2):
    t = xn2.shape[0]
    unit_rows = SC_LANES
    unit = unit_rows // SC_ROWS
    units_per_tok = PEER_SEL // unit_rows

    @functools.partial(
        pl.kernel,
        mesh=_sc_mesh(),
        out_type=jax.ShapeDtypeStruct((t, PEER_SEL), F32),
        scratch_types=[pltpu.VMEM((unit_rows * SC_LANES,), F32)] + _sc_scratch(HALF, PEER_SEL),
        compiler_params=pltpu.CompilerParams(needs_layout_passes=False),
        name="peer_dot",
    )
    def body(u_hbm, idx_hbm, x_hbm, out_hbm, red_v, idx_v, x_v, out_v, *rest):
        bufs, sems = rest[:SC_NBUF], rest[SC_NBUF:]

        def compute(u, ub):
            tl = u // units_per_tok
            rb = (u % units_per_tok) * unit_rows

            def cbody(c, acc):
                acc = list(acc)
                c4 = c // (LANES // (2 * SC_LANES))
                lane0 = pl.multiple_of((c % (LANES // (2 * SC_LANES))) * 2 * SC_LANES, 2 * SC_LANES)
                col = pl.multiple_of(c * 2 * SC_LANES, 2 * SC_LANES)
                xw1 = x_v[tl, pl.ds(col, SC_LANES)]
                xw2 = x_v[tl, pl.ds(col + SC_LANES, SC_LANES)]
                for q, buf in enumerate(ub):
                    for r in range(SC_ROWS):
                        a = q * SC_ROWS + r
                        lo, hi = _mul_add_words(buf[r, c4, pl.ds(lane0, SC_LANES)], xw1,
                                                buf[r, c4, pl.ds(lane0 + SC_LANES, SC_LANES)], xw2)
                        acc[a] = acc[a] + lo + hi
                return tuple(acc)

            acc = lax.fori_loop(0, HALF // (2 * SC_LANES), cbody,
                                tuple(jnp.zeros((SC_LANES,), F32) for _ in range(unit_rows)))
            for a in range(unit_rows):
                red_v[pl.ds(a * SC_LANES, SC_LANES)] = acc[a]
            lane_base = lax.iota(jnp.int32, SC_LANES) * SC_LANES
            tot = plsc.load_gather(red_v, [lane_base])
            for k in range(1, SC_LANES):
                tot = tot + plsc.load_gather(red_v, [lane_base + k])
            out_v[tl, pl.ds(pl.multiple_of(rb, unit_rows), unit_rows)] = tot

        _sc_batches(t // SC_WORKERS, idx_hbm, x_hbm, out_hbm, idx_v, x_v, out_v,
                    lambda: _sc_gather_ring(u_hbm, idx_v, bufs, sems, unit, compute))

    return body(u3, idx_flat, xn2)


def _peer_combine_sc(v3, idx_flat, coef_b):
    t = coef_b.shape[0]
    unit = 4
    units_per_tok = PEER_SEL // (unit * SC_ROWS)
    nvec = LANES // SC_LANES

    @functools.partial(
        pl.kernel,
        mesh=_sc_mesh(),
        out_type=jax.ShapeDtypeStruct((t, D_MODEL), F32),
        scratch_types=_sc_scratch(PEER_SEL * SC_LANES, D_MODEL),
        compiler_params=pltpu.CompilerParams(needs_layout_passes=False),
        name="peer_combine",
    )
    def body(v_hbm, idx_hbm, coef_hbm, out_hbm, idx_v, coef_v, out_v, *rest):
        bufs, sems = rest[:SC_NBUF], rest[SC_NBUF:]

        def compute(u, ub):
            tl = u // units_per_tok
            rb = (u % units_per_tok) * unit * SC_ROWS
            for g in range(ROW_TILES):
                cols = ([pl.ds(g * LANES + l * SC_LANES, SC_LANES) for l in range(nvec)]
                        + [pl.ds(HALF + g * LANES + l * SC_LANES, SC_LANES) for l in range(nvec)])

                def rbody(r, acc):
                    acc = list(acc)
                    for p in range(unit // 2):
                        co = pl.multiple_of((rb + 2 * p * SC_ROWS + r) * SC_LANES, SC_LANES)
                        cf1 = coef_v[tl, pl.ds(co, SC_LANES)]
                        cf2 = coef_v[tl, pl.ds(co + SC_ROWS * SC_LANES, SC_LANES)]
                        for l in range(nvec):
                            lo, hi = _mul_add_words(
                                ub[2 * p][r, g, pl.ds(l * SC_LANES, SC_LANES)], cf1,
                                ub[2 * p + 1][r, g, pl.ds(l * SC_LANES, SC_LANES)], cf2)
                            acc[l] = acc[l] + lo
                            acc[nvec + l] = acc[nvec + l] + hi
                    return tuple(acc)

                acc = lax.fori_loop(0, SC_ROWS, rbody, tuple(out_v[tl, c] for c in cols))
                for k, c in enumerate(cols):
                    out_v[tl, c] = acc[k]

        def run():
            def zero(tl, c2):
                for k in range(D_MODEL // SC_LANES):
                    out_v[tl, pl.ds(k * SC_LANES, SC_LANES)] = jnp.zeros((SC_LANES,), F32)
                return c2

            lax.fori_loop(0, SC_TOK_BATCH, zero, 0)
            _sc_gather_ring(v_hbm, idx_v, bufs, sems, unit, compute)

        _sc_batches(t // SC_WORKERS, idx_hbm, coef_hbm, out_hbm, idx_v, coef_v, out_v, run)

    return body(v3, idx_flat, coef_b)


def _coef_kernel(act_ref, g_ref, e_ref, coef_ref):
    act = act_ref[...]
    gelu = 0.5 * act * (1.0 + lax.erf(act * np.float32(math.sqrt(0.5))))
    c = _dot(g_ref[...] * gelu, e_ref[...], precision=lax.Precision.HIGHEST)
    coef_ref[...] = _pack_pairs(c, c)


def _coef(act, g):
    n = act.shape[0]
    tr = _row_block(n, DENSE_ROWS)
    w = PEER_SEL * SC_LANES
    expand = (jnp.arange(w, dtype=jnp.int32)[None, :] // SC_LANES
              == jnp.arange(PEER_SEL, dtype=jnp.int32)[:, None]).astype(F32)
    blk = pl.BlockSpec((tr, PEER_SEL), lambda i: (i, 0))
    return pl.pallas_call(
        _coef_kernel,
        grid=(n // tr,),
        in_specs=[blk, blk, pl.BlockSpec((PEER_SEL, w), lambda i: (0, 0))],
        out_specs=pl.BlockSpec((tr, w), lambda i: (i, 0)),
        out_shape=jax.ShapeDtypeStruct((n, w), jnp.uint32),
        compiler_params=pltpu.CompilerParams(
            dimension_semantics=("parallel",), vmem_limit_bytes=VMEM_LIMIT),
        name="peer_coef",
    )(act, g, expand)


def _final_kernel(h_ref, p_ref, g_ref, o_ref):
    o_ref[...] = _rms(h_ref[...] + p_ref[...], g_ref[...])


def _final(h2, peer_out, gain):
    n = h2.shape[0]
    tm = _row_block(n, 2 * DENSE_ROWS)
    blk = pl.BlockSpec((tm, D_MODEL), lambda i: (i, 0))
    return pl.pallas_call(
        _final_kernel,
        grid=(n // tm,),
        in_specs=[blk, blk, pl.BlockSpec((1, D_MODEL), lambda i: (0, 0))],
        out_specs=blk,
        out_shape=jax.ShapeDtypeStruct((n, D_MODEL), F32),
        compiler_params=pltpu.CompilerParams(
            dimension_semantics=("parallel",), vmem_limit_bytes=VMEM_LIMIT),
        name="final_norm",
    )(h2, peer_out, gain)


def _pad_heads(w, dk):
    lead = w.shape[:-1]
    w = w.reshape(lead + (HEADS, dk))
    w = jnp.pad(w, [(0, 0)] * len(lead) + [(0, 0), (0, LANES - dk)])
    return w.reshape(lead + (HEADS * LANES,))


def _rot_heads(w, dk):
    lead = w.shape[:-1]
    w = w.reshape(lead + (HEADS, 2, dk // 2))
    w = jnp.stack([-w[..., 1, :], w[..., 0, :]], axis=-2)
    return w.reshape(lead + (HEADS * dk,))


def _pack_in_weights(w_in):
    gq, gk, gv, gg, ga, rq, rk, rv, rg = jnp.split(w_in, np.cumsum(IN_SPLITS)[:-1].tolist(), axis=-1)
    ga = jnp.pad(ga, ((0, 0), (0, 2 * LANES - GATE_RANK)))
    return jnp.concatenate(
        [_pad_heads(gq, DK), _pad_heads(gk, DK), gv, gg,
         _pad_heads(rq, DK), _pad_heads(rk, DK), rv, rg,
         _pad_heads(_rot_heads(rq, DK), DK), _pad_heads(_rot_heads(rk, DK), DK), ga],
        axis=-1).astype(BF16)


def _rope_tables(pos):
    inv_freq = ROPE_BASE ** (-jnp.arange(0, DK, 2, dtype=F32) / DK)
    ang = pos[:, None] * inv_freq[None, :]
    full = lambda t: _pad_heads(jnp.tile(t, (1, 2 * HEADS)), DK)
    return full(jnp.cos(ang)), full(jnp.sin(ang))


def _segment_lengths(s, first):
    if first:
        lengths = [s * k // sum(FIRST_SEQ_RAMP) for k in FIRST_SEQ_RAMP]
        assert sum(lengths) == s
    else:
        lengths = [s // SEGS_PER_SEQ] * SEGS_PER_SEQ
    assert all(n % SC_CALL_TOKENS == 0 for n in lengths)
    return lengths


def kernel(x, meta_tokens, norm1_gain, w_in, gla_gate_w2, gla_gate_b, gla_norm_gain, ret_norm_gain, ret_norm_bias, w_out, norm2_gain, peer_w_q, peer_subkeys, peer_u, peer_v, final_gain):
    b, s, d = x.shape
    assert d == D_MODEL and (s // SEGS_PER_SEQ) % SC_CALL_TOKENS == 0
    assert norm1_gain.shape[0] == 1, "single-layer block"
    x2d = x.reshape(b * s, d)

    w_all = _pack_in_weights(w_in[0])
    w2p = jnp.pad(_pad_heads(gla_gate_w2[0], DK), ((0, LANES - GATE_RANK), (0, 0))).astype(BF16)
    gbp = _pad_heads(gla_gate_b[0][None, :], DK)
    g1 = norm1_gain[0][None, :]
    ggain = gla_norm_gain[0][None, :]
    rgain = ret_norm_gain[0][None, :]
    rbias = ret_norm_bias[0][None, :]

    h0 = jnp.concatenate([jnp.zeros((N_PAD, d), x.dtype), meta_tokens.astype(x.dtype)], axis=0)
    cos0, sin0 = _rope_tables(jnp.arange(CHUNK, dtype=F32) - N_PAD)
    cos1, sin1 = _rope_tables(jnp.arange(s, dtype=F32) + N_META)
    zero_state = jnp.zeros((HEADS, DV, LANES), F32)
    _, s_meta, r_meta = _mixer(h0, g1, w_all, cos0, sin0, w2p, gbp, ggain, rgain, rbias,
                               zero_state, zero_state, batch=1, n_invalid=N_PAD)

    w_out_b = w_out[0].astype(BF16)
    g2 = norm2_gain[0][None, :]
    w_q = peer_w_q[0].astype(BF16)
    subkeys = peer_subkeys[0].reshape(2 * PEER_HEADS, PEER_NKEYS, PEER_DKEY // 2).astype(BF16)
    u3 = _pack_table(peer_u[0])
    v3 = _pack_table(peer_v[0])
    gf = final_gain[None, :]

    outs = []
    prev_g = None
    pending = []
    for bi in range(b):
        s_state, r_state = s_meta, r_meta
        start = 0
        for seg in _segment_lengths(s, first=bi == 0):
            xs = x2d[bi * s + start:bi * s + start + seg]
            pos = slice(start, start + seg)
            start += seg
            if prev_g is not None:
                xs, _ = lax.optimization_barrier((xs, prev_g))
            o, s_state, r_state = _mixer(xs, g1, w_all, cos1[pos], sin1[pos], w2p, gbp,
                                         ggain, rgain, rbias, s_state, r_state, batch=1, n_invalid=0)
            h2, xn2, xn2_pairs = _out_proj(o, xs, w_out_b, g2)
            idx, g = _route(xn2, w_q, subkeys)
            prev_g = g
            idx_flat = idx.reshape(-1)
            if len(pending) >= SC_LOOKAHEAD:
                h2_old, peer_old = pending.pop(0)
                idx_flat, peer_old = lax.optimization_barrier((idx_flat, peer_old))
                outs.append(_final(h2_old, peer_old, gf))
            act = _peer_dot_sc(u3, idx_flat, xn2_pairs)
            pending.append((h2, _peer_combine_sc(v3, idx_flat, _coef(act, g))))
    outs.extend(_final(h2_old, peer_old, gf) for h2_old, peer_old in pending)
    return jnp.concatenate(outs, axis=0).reshape(b, s, d)
```

```python
import functools
import math

import jax
import jax.numpy as jnp
import numpy as np
from jax import lax
from jax.experimental import pallas as pl
from jax.experimental.pallas import tpu as pltpu
from jax.experimental.pallas import tpu_sc as plsc

F32 = jnp.float32
BF16 = jnp.bfloat16

D_MODEL = 1024
N_META = 16
CHUNK = 64
N_PAD = CHUNK - N_META
EPS = 1e-6

HEADS = 4
DV = 128
DK = 64
LANES = 128
GATE_RANK = 16
GATE_TAU = 16.0
ROPE_BASE = 10000.0
IN_SPLITS = (HEADS * DK, HEADS * DK, HEADS * DV, HEADS * DV, GATE_RANK,
             HEADS * DK, HEADS * DK, HEADS * DV, HEADS * DV)

PEER_HEADS = 8
PEER_NKEYS = 128
PEER_DKEY = 256
PEER_TOPK = 16
PEER_SEL = PEER_HEADS * PEER_TOPK

HW = HEADS * LANES
OFF_GQ, OFF_GK, OFF_GV, OFF_GG = 0, HW, 2 * HW, 3 * HW
OFF_RQ, OFF_RK, OFF_RV, OFF_RG = 4 * HW, 5 * HW, 6 * HW, 7 * HW
OFF_RQR, OFF_RKR, OFF_GA = 8 * HW, 9 * HW, 10 * HW
PROJ_W = 10 * HW + 2 * LANES
PROJ_TN = PROJ_W // 3

LOG_GAMMA = tuple(math.log(1.0 - 2.0 ** (-5.0 - h)) for h in range(HEADS))

SC_CORES, SC_SUBCORES, SC_LANES = 2, 16, 16
SC_WORKERS = SC_CORES * SC_SUBCORES
SC_TOK_BATCH = 16
SC_ROWS = 16
SC_NBUF = 8
SC_GATHERS = SC_TOK_BATCH * PEER_SEL // SC_ROWS
HALF = D_MODEL // 2
ROW_TILES = HALF // LANES

SC_CALL_TOKENS = SC_WORKERS * SC_TOK_BATCH
SEGS_PER_SEQ = 2
FIRST_SEQ_RAMP = (1, 1, 2, 3, 4, 5)
SC_LOOKAHEAD = 2

MIXER_ROWS = 256
DENSE_ROWS = 512
VMEM_LIMIT = 48 * 1024 * 1024


def _row_block(n, target):
    blk = math.gcd(n, target)
    assert blk % 8 == 0 or blk == n, (n, target)
    return blk


def _dot(a, b, **kw):
    return jnp.dot(a, b, preferred_element_type=F32, **kw)


def _dot_nt(a, b):
    return lax.dot_general(a, b, (((1,), (1,)), ((), ())), preferred_element_type=F32)


def _dot_tn(a, b):
    return lax.dot_general(a, b, (((0,), (0,)), ((), ())), preferred_element_type=F32)


def _rms(x, g):
    return x * lax.rsqrt(jnp.mean(x * x, axis=-1, keepdims=True) + EPS) * g


def _silu(x):
    return x * jax.nn.sigmoid(x)


def _mixer_kernel(x_ref, g1_ref, wall_ref, cos_ref, sin_ref, w2_ref, gb_ref, ggain_ref, rgain_ref,
                  rbias_ref, s0_ref, r0_ref, o_ref, s_out_ref, r_out_ref, s_scr, r_scr, proj_ref,
                  *, n_invalid, rows_blk):
    j = pl.program_id(1)
    xn = _rms(x_ref[...], g1_ref[...]).astype(BF16)
    for jn in range(PROJ_W // PROJ_TN):
        cols = slice(jn * PROJ_TN, (jn + 1) * PROJ_TN)
        proj_ref[:, cols] = _dot(xn, wall_ref[:, cols])

    @pl.when(j == 0)
    def _():
        s_scr[...] = s0_ref[...]
        r_scr[...] = r0_ref[...]

    row = lax.broadcasted_iota(jnp.int32, (CHUNK, CHUNK), 0)
    col = lax.broadcasted_iota(jnp.int32, (CHUNK, CHUNK), 1)
    causal = row >= col
    tril = causal.astype(F32)
    diff = jnp.maximum(row - col, 0).astype(F32)
    rowf = lax.broadcasted_iota(jnp.int32, (CHUNK, LANES), 0).astype(F32)
    row1 = lax.broadcasted_iota(jnp.int32, (CHUNK, 1), 0)

    def chunk_body(c, carry):
        r0 = pl.multiple_of(c * CHUNK, CHUNK)
        rows = pl.ds(r0, CHUNK)
        valid = ((j * rows_blk + r0 + row1) >= n_invalid).astype(F32)

        ga = proj_ref[rows, OFF_GA:OFF_GA + LANES].astype(BF16)
        pre = _dot(ga, w2_ref[...]) + gb_ref[...]
        log_a = jax.nn.log_sigmoid(pre) * (1.0 / GATE_TAU)
        bcum = _dot(tril, log_a, precision=lax.Precision.HIGHEST)
        b_last = bcum[CHUNK - 1:CHUNK, :]
        gq = proj_ref[rows, OFF_GQ:OFF_GQ + HW] * (DK ** -0.5)
        gk = proj_ref[rows, OFF_GK:OFF_GK + HW] * valid
        q_t = (gq * jnp.exp(bcum)).astype(BF16)
        k_t = (gk * jnp.exp(-bcum)).astype(BF16)
        k_end = (gk * jnp.exp(b_last - bcum)).astype(BF16)
        decay = jnp.exp(b_last)
        for h in range(HEADS):
            sl = slice(h * LANES, (h + 1) * LANES)
            v = proj_ref[rows, OFF_GV + h * DV:OFF_GV + (h + 1) * DV].astype(BF16)
            att = jnp.where(causal, _dot_nt(q_t[:, sl], k_t[:, sl]), 0.0).astype(BF16)
            s_prev = s_scr[h]
            o = _dot(att, v) + _dot_nt(q_t[:, sl], s_prev.astype(BF16))
            s_scr[h] = s_prev * decay[:, sl] + _dot_tn(v, k_end[:, sl])
            o = o * lax.rsqrt(jnp.mean(o * o, axis=-1, keepdims=True) + EPS) * ggain_ref[...]
            gate = proj_ref[rows, OFF_GG + h * DV:OFF_GG + (h + 1) * DV]
            o_ref[rows, h * DV:(h + 1) * DV] = (o * _silu(gate)).astype(o_ref.dtype)

        cosb = cos_ref[rows, :]
        sinb = sin_ref[rows, :]
        rq = proj_ref[rows, OFF_RQ:OFF_RQ + HW]
        rqr = proj_ref[rows, OFF_RQR:OFF_RQR + HW]
        rk = proj_ref[rows, OFF_RK:OFF_RK + HW] * valid
        rkr = proj_ref[rows, OFF_RKR:OFF_RKR + HW] * valid
        q_r = (rq * cosb + rqr * sinb).astype(BF16)
        k_r = (rk * cosb + rkr * sinb) * (DK ** -0.5)
        for h in range(HEADS):
            sl = slice(h * LANES, (h + 1) * LANES)
            lg = LOG_GAMMA[h]
            dmat = jnp.where(causal, jnp.exp(lg * diff), 0.0)
            xi = jnp.exp(lg * (rowf + 1.0))
            zeta = jnp.exp(lg * (CHUNK - 1.0 - rowf))
            v = proj_ref[rows, OFF_RV + h * DV:OFF_RV + (h + 1) * DV].astype(BF16)
            k_h = k_r[:, sl]
            att = (_dot_nt(q_r[:, sl], k_h.astype(BF16)) * dmat).astype(BF16)
            r_prev = r_scr[h]
            o = _dot(att, v) + _dot_nt(q_r[:, sl], r_prev.astype(BF16)) * xi
            r_scr[h] = r_prev * math.exp(lg * CHUNK) + _dot_tn(v, (k_h * zeta).astype(BF16))
            mu = jnp.mean(o, axis=-1, keepdims=True)
            var = jnp.mean(jnp.square(o - mu), axis=-1, keepdims=True)
            o = (o - mu) * lax.rsqrt(var + EPS) * rgain_ref[...] + rbias_ref[...]
            gate = proj_ref[rows, OFF_RG + h * DV:OFF_RG + (h + 1) * DV]
            o_ref[rows, HW + h * DV:HW + (h + 1) * DV] = (o * _silu(gate)).astype(o_ref.dtype)
        return carry

    lax.fori_loop(0, rows_blk // CHUNK, chunk_body, 0)
    s_out_ref[...] = s_scr[...]
    r_out_ref[...] = r_scr[...]


def _mixer(x2d, g1, w_all, cosf, sinf, w2p, gbp, ggain, rgain, rbias, s0, r0, *, batch, n_invalid):
    rows = x2d.shape[0] // batch
    rows_blk = _row_block(rows, MIXER_ROWS)
    nblk = rows // rows_blk
    const2 = lambda b, j: (0, 0)
    const3 = lambda b, j: (0, 0, 0)
    state = jax.ShapeDtypeStruct((HEADS, DV, LANES), F32)
    return pl.pallas_call(
        functools.partial(_mixer_kernel, n_invalid=n_invalid, rows_blk=rows_blk),
        grid=(batch, nblk),
        in_specs=[
            pl.BlockSpec((rows_blk, D_MODEL), lambda b, j: (b * nblk + j, 0)),
            pl.BlockSpec((1, D_MODEL), const2),
            pl.BlockSpec((D_MODEL, PROJ_W), const2),
            pl.BlockSpec((rows_blk, HW), lambda b, j: (j, 0)),
            pl.BlockSpec((rows_blk, HW), lambda b, j: (j, 0)),
            pl.BlockSpec((LANES, HW), const2),
            pl.BlockSpec((1, HW), const2),
            pl.BlockSpec((1, DV), const2),
            pl.BlockSpec((1, DV), const2),
            pl.BlockSpec((1, DV), const2),
            pl.BlockSpec((HEADS, DV, LANES), const3),
            pl.BlockSpec((HEADS, DV, LANES), const3),
        ],
        out_specs=[
            pl.BlockSpec((rows_blk, D_MODEL), lambda b, j: (b * nblk + j, 0)),
            pl.BlockSpec((HEADS, DV, LANES), const3),
            pl.BlockSpec((HEADS, DV, LANES), const3),
        ],
        out_shape=[jax.ShapeDtypeStruct((batch * rows, D_MODEL), BF16), state, state],
        scratch_shapes=[pltpu.VMEM((HEADS, DV, LANES), F32), pltpu.VMEM((HEADS, DV, LANES), F32),
                        pltpu.VMEM((rows_blk, PROJ_W), F32)],
        compiler_params=pltpu.CompilerParams(
            dimension_semantics=("arbitrary", "arbitrary"), vmem_limit_bytes=VMEM_LIMIT),
        name="token_mixers",
    )(x2d, g1, w_all, cosf, sinf, w2p, gbp, ggain, rgain, rbias, s0, r0)


def _out_proj_kernel(o_ref, x_ref, w_ref, g_ref, h_ref, xn_ref, xp_ref):
    h = x_ref[...] + _dot(o_ref[...], w_ref[...])
    h_ref[...] = h
    xn = _rms(h, g_ref[...])
    xn_ref[...] = xn
    xp_ref[...] = _pack_pairs(xn[:, :HALF], xn[:, HALF:])


def _out_proj(o, x2d, w_out, gain2):
    n = x2d.shape[0]
    tm = _row_block(n, DENSE_ROWS)
    blk = pl.BlockSpec((tm, D_MODEL), lambda i: (i, 0))
    return pl.pallas_call(
        _out_proj_kernel,
        grid=(n // tm,),
        in_specs=[blk, blk, pl.BlockSpec((D_MODEL, D_MODEL), lambda i: (0, 0)),
                  pl.BlockSpec((1, D_MODEL), lambda i: (0, 0))],
        out_specs=[blk, blk, pl.BlockSpec((tm, HALF), lambda i: (i, 0))],
        out_shape=[jax.ShapeDtypeStruct((n, D_MODEL), F32)] * 2 + [jax.ShapeDtypeStruct((n, HALF), jnp.uint32)],
        compiler_params=pltpu.CompilerParams(
            dimension_semantics=("parallel",), vmem_limit_bytes=VMEM_LIMIT),
        name="out_proj_norm",
    )(o, x2d, w_out, gain2)


def _top_k_rows(arrays, payloads=None):
    arrays = list(arrays)
    n = arrays[0].shape[0]
    rowi = lax.broadcasted_iota(jnp.int32, arrays[0].shape, 0)
    vals = [[] for _ in arrays]
    picks = [[] for _ in arrays]
    for _ in range(PEER_TOPK):
        for k, s in enumerate(arrays):
            m = jnp.max(s, axis=0, keepdims=True)
            am = jnp.min(jnp.where(s == m, rowi, n), axis=0, keepdims=True)
            sel = rowi == am
            vals[k].append(m)
            if payloads is None:
                picks[k].append(am)
            else:
                picks[k].append(jnp.min(jnp.where(sel, payloads[k], jnp.iinfo(jnp.int32).max),
                                        axis=0, keepdims=True))
            arrays[k] = jnp.where(sel, -jnp.inf, s)
    return [(jnp.concatenate(v, axis=0), jnp.concatenate(p, axis=0)) for v, p in zip(vals, picks)]


def _candidate_cells():
    cells = [(a, b) for a in range(PEER_TOPK) for b in range(PEER_TOPK) if (a + 1) * (b + 1) <= PEER_TOPK]
    n = -(-len(cells) // 8) * 8
    sel_a = np.zeros((n, PEER_TOPK), np.float32)
    sel_b = np.zeros((n, PEER_TOPK), np.float32)
    pad = np.full((n, 1), -np.inf, np.float32)
    for r, (a, b) in enumerate(cells):
        sel_a[r, a] = 1.0
        sel_b[r, b] = 1.0
        pad[r, 0] = 0.0
    return sel_a, sel_b, pad


def _route_kernel(xn_ref, wq_ref, sk_ref, sa_ref, sb_ref, pad_ref, idx_ref, g_ref, q_scr, i_scr, g_scr):
    tr = xn_ref.shape[0]
    q = _dot(xn_ref[...].astype(BF16), wq_ref[...])
    for hc in range(2 * PEER_HEADS):
        q_scr[hc] = q[:, hc * LANES:(hc + 1) * LANES].astype(BF16)

    def sub_body(sb, carry):
        rows = pl.ds(pl.multiple_of(sb * LANES, LANES), LANES)

        def head_pair(hp, c2):
            heads = [2 * hp, 2 * hp + 1]
            scores = [_dot_nt(sk_ref[2 * h + c], q_scr[2 * h + c, rows, :])
                      for h in heads for c in range(2)]
            first = _top_k_rows(scores)
            sa, sb = sa_ref[...], sb_ref[...]
            cands, cidxs = [], []
            for k in range(len(heads)):
                (v0, i0), (v1, i1) = first[2 * k], first[2 * k + 1]
                cands.append((_dot(sa, v0, precision=lax.Precision.HIGHEST)
                              + _dot(sb, v1, precision=lax.Precision.HIGHEST)) + pad_ref[...])
                cidxs.append((_dot(sa, i0.astype(F32)) * PEER_NKEYS
                              + _dot(sb, i1.astype(F32))).astype(jnp.int32))
            for h, (best, e) in zip(heads, _top_k_rows(cands, cidxs)):
                ex = jnp.exp(best - best[0:1])
                i_scr[h] = e
                g_scr[h] = ex / jnp.sum(ex, axis=0, keepdims=True)
            return c2

        lax.fori_loop(0, PEER_HEADS // 2, head_pair, 0)
        idx_ref[rows, :] = i_scr[...].reshape(PEER_SEL, LANES).T
        g_ref[rows, :] = g_scr[...].reshape(PEER_SEL, LANES).T
        return carry

    lax.fori_loop(0, tr // LANES, sub_body, 0)


def _route(xn2, w_q, subkeys):
    n = xn2.shape[0]
    sel_a, sel_b, pad = _candidate_cells()
    tr = _row_block(n, DENSE_ROWS)
    qw = 2 * PEER_HEADS * LANES
    return pl.pallas_call(
        _route_kernel,
        grid=(n // tr,),
        in_specs=[pl.BlockSpec((tr, D_MODEL), lambda i: (i, 0)),
                  pl.BlockSpec((D_MODEL, qw), lambda i: (0, 0)),
                  pl.BlockSpec((2 * PEER_HEADS, PEER_NKEYS, LANES), lambda i: (0, 0, 0)),
                  pl.BlockSpec(sel_a.shape, lambda i: (0, 0)),
                  pl.BlockSpec(sel_b.shape, lambda i: (0, 0)),
                  pl.BlockSpec(pad.shape, lambda i: (0, 0))],
        out_specs=[pl.BlockSpec((tr, PEER_SEL), lambda i: (i, 0))] * 2,
        out_shape=[jax.ShapeDtypeStruct((n, PEER_SEL), jnp.int32),
                   jax.ShapeDtypeStruct((n, PEER_SEL), F32)],
        scratch_shapes=[pltpu.VMEM((2 * PEER_HEADS, tr, LANES), BF16),
                        pltpu.VMEM((PEER_HEADS, PEER_TOPK, LANES), jnp.int32),
                        pltpu.VMEM((PEER_HEADS, PEER_TOPK, LANES), F32)],
        compiler_params=pltpu.CompilerParams(
            dimension_semantics=("parallel",), vmem_limit_bytes=VMEM_LIMIT),
        name="peer_route",
    )(xn2, w_q, subkeys, sel_a, sel_b, pad)


def _sc_worker_id():
    return lax.axis_index("s") * SC_CORES + lax.axis_index("c")


def _sc_mesh():
    return plsc.VectorSubcoreMesh(core_axis_name="c", subcore_axis_name="s")


def _sc_gather_ring(table_hbm, idx_v, bufs, sems, unit, compute):
    def gather(g, k):
        off = pl.multiple_of(g * SC_ROWS, SC_ROWS)
        return pltpu.make_async_copy(table_hbm.at[idx_v.at[pl.ds(off, SC_ROWS)]], bufs[k], sems[k])

    for k in range(SC_NBUF):
        gather(k, k).start()

    units_per_iter = SC_NBUF // unit

    def ring_iter(i, carry):
        for m in range(units_per_iter):
            ks = range(m * unit, (m + 1) * unit)
            for k in ks:
                gather(i * SC_NBUF + k, k).wait()
            compute(i * units_per_iter + m, [bufs[k] for k in ks])
            for k in ks:
                g_next = (i + 1) * SC_NBUF + k

                @pl.when(g_next < SC_GATHERS)
                def _():
                    gather(g_next, k).start()
        return carry

    lax.fori_loop(0, SC_GATHERS // SC_NBUF, ring_iter, 0)


def _bf16_bits(x):
    b = lax.bitcast_convert_type(x, jnp.uint32)
    return (b + jnp.uint32(0x7FFF) + ((b >> 16) & jnp.uint32(1))) >> 16


def _pack_pairs(lo, hi):
    return _bf16_bits(lo) | (_bf16_bits(hi) << 16)


def _pack_table(t):
    return _pack_pairs(t[:, :HALF], t[:, HALF:]).reshape(t.shape[0], ROW_TILES, LANES)


def _mul_add_words(w1, m1, w2, m2):
    bf = lambda a: plsc.bitcast(a, jnp.bfloat16)
    p = plsc.bitcast(bf(w1) * bf(m1) + bf(w2) * bf(m2), jnp.uint32)
    return lax.bitcast_convert_type(p << 16, F32), lax.bitcast_convert_type(p, F32)


def _sc_scratch(staged_cols, out_cols):
    return ([pltpu.VMEM((SC_TOK_BATCH * PEER_SEL,), jnp.int32),
             pltpu.VMEM((SC_TOK_BATCH, staged_cols), jnp.uint32),
             pltpu.VMEM((SC_TOK_BATCH, out_cols), F32)]
            + [pltpu.VMEM((SC_ROWS, ROW_TILES, LANES), jnp.uint32) for _ in range(SC_NBUF)]
            + [pltpu.SemaphoreType.DMA for _ in range(SC_NBUF)])


def _sc_batches(tpw, idx_hbm, staged_hbm, out_hbm, idx_v, staged_v, out_v, run):
    tok0 = _sc_worker_id() * tpw

    def batch(tb, carry):
        t0 = pl.multiple_of(tok0 + tb * SC_TOK_BATCH, SC_TOK_BATCH)
        pltpu.sync_copy(idx_hbm.at[pl.ds(pl.multiple_of(t0 * PEER_SEL, SC_TOK_BATCH * PEER_SEL),
                                         SC_TOK_BATCH * PEER_SEL)], idx_v)
        pltpu.sync_copy(staged_hbm.at[pl.ds(t0, SC_TOK_BATCH)], staged_v)
        run()
        pltpu.sync_copy(out_v, out_hbm.at[pl.ds(t0, SC_TOK_BATCH)])
        return carry

    lax.fori_loop(0, tpw // SC_TOK_BATCH, batch, 0)


def _peer_dot_sc(u3, idx_flat, xn2):
    t = xn2.shape[0]
    unit_rows = 16
    unit = unit_rows // SC_ROWS
    units_per_tok = PEER_SEL // unit_rows

    @functools.partial(
        pl.kernel,
        mesh=_sc_mesh(),
        out_type=jax.ShapeDtypeStruct((t, PEER_SEL * SC_LANES), F32),
        scratch_types=_sc_scratch(HALF, PEER_SEL * SC_LANES),
        compiler_params=pltpu.CompilerParams(needs_layout_passes=False),
        name="peer_dot",
    )
    def body(u_hbm, idx_hbm, x_hbm, out_hbm, idx_v, x_v, out_v, *rest):
        bufs, sems = rest[:SC_NBUF], rest[SC_NBUF:]

        def compute(u, ub):
            tl = u // units_per_tok
            rb = (u % units_per_tok) * unit_rows

            def cbody(c, acc):
                acc = list(acc)
                c4 = c // (LANES // (2 * SC_LANES))
                lane0 = pl.multiple_of((c % (LANES // (2 * SC_LANES))) * 2 * SC_LANES, 2 * SC_LANES)
                col = pl.multiple_of(c * 2 * SC_LANES, 2 * SC_LANES)
                xw1 = x_v[tl, pl.ds(col, SC_LANES)]
                xw2 = x_v[tl, pl.ds(col + SC_LANES, SC_LANES)]
                for q, buf in enumerate(ub):
                    for r in range(SC_ROWS):
                        a = q * SC_ROWS + r
                        lo, hi = _mul_add_words(buf[r, c4, pl.ds(lane0, SC_LANES)], xw1,
                                                buf[r, c4, pl.ds(lane0 + SC_LANES, SC_LANES)], xw2)
                        acc[a] = acc[a] + lo + hi
                return tuple(acc)

            acc = lax.fori_loop(0, HALF // (2 * SC_LANES), cbody,
                                tuple(jnp.zeros((SC_LANES,), F32) for _ in range(unit_rows)))
            for a in range(unit_rows):
                o = pl.multiple_of((rb + a) * SC_LANES, SC_LANES)
                out_v[tl, pl.ds(o, SC_LANES)] = acc[a]

        _sc_batches(t // SC_WORKERS, idx_hbm, x_hbm, out_hbm, idx_v, x_v, out_v,
                    lambda: _sc_gather_ring(u_hbm, idx_v, bufs, sems, unit, compute))

    return body(u3, idx_flat, xn2)


def _peer_combine_sc(v3, idx_flat, coef_b):
    t = coef_b.shape[0]
    unit = 2
    units_per_tok = PEER_SEL // (unit * SC_ROWS)
    nvec = LANES // SC_LANES

    @functools.partial(
        pl.kernel,
        mesh=_sc_mesh(),
        out_type=jax.ShapeDtypeStruct((t, D_MODEL), F32),
        scratch_types=_sc_scratch(PEER_SEL * SC_LANES, D_MODEL),
        compiler_params=pltpu.CompilerParams(needs_layout_passes=False),
        name="peer_combine",
    )
    def body(v_hbm, idx_hbm, coef_hbm, out_hbm, idx_v, coef_v, out_v, *rest):
        bufs, sems = rest[:SC_NBUF], rest[SC_NBUF:]

        def compute(u, ub):
            tl = u // units_per_tok
            rb = (u % units_per_tok) * unit * SC_ROWS
            for g in range(ROW_TILES):
                cols = ([pl.ds(g * LANES + l * SC_LANES, SC_LANES) for l in range(nvec)]
                        + [pl.ds(HALF + g * LANES + l * SC_LANES, SC_LANES) for l in range(nvec)])

                def rbody(r, acc):
                    acc = list(acc)
                    co = pl.multiple_of((rb + r) * SC_LANES, SC_LANES)
                    cf1 = coef_v[tl, pl.ds(co, SC_LANES)]
                    cf2 = coef_v[tl, pl.ds(co + SC_ROWS * SC_LANES, SC_LANES)]
                    for l in range(nvec):
                        lo, hi = _mul_add_words(ub[0][r, g, pl.ds(l * SC_LANES, SC_LANES)], cf1,
                                                ub[1][r, g, pl.ds(l * SC_LANES, SC_LANES)], cf2)
                        acc[l] = acc[l] + lo
                        acc[nvec + l] = acc[nvec + l] + hi
                    return tuple(acc)

                acc = lax.fori_loop(0, SC_ROWS, rbody, tuple(out_v[tl, c] for c in cols))
                for k, c in enumerate(cols):
                    out_v[tl, c] = acc[k]

        def run():
            def zero(tl, c2):
                for k in range(D_MODEL // SC_LANES):
                    out_v[tl, pl.ds(k * SC_LANES, SC_LANES)] = jnp.zeros((SC_LANES,), F32)
                return c2

            lax.fori_loop(0, SC_TOK_BATCH, zero, 0)
            _sc_gather_ring(v_hbm, idx_v, bufs, sems, unit, compute)

        _sc_batches(t // SC_WORKERS, idx_hbm, coef_hbm, out_hbm, idx_v, coef_v, out_v, run)

    return body(v3, idx_flat, coef_b)


def _coef_kernel(part_ref, g_ref, e_ref, coef_ref):
    act = lax.dot_general(part_ref[...], e_ref[...], (((1,), (1,)), ((), ())),
                          precision=lax.Precision.HIGHEST, preferred_element_type=F32)
    gelu = 0.5 * act * (1.0 + lax.erf(act * np.float32(math.sqrt(0.5))))
    c = _dot(g_ref[...] * gelu, e_ref[...], precision=lax.Precision.HIGHEST)
    coef_ref[...] = _pack_pairs(c, c)


def _coef(part, g):
    n = part.shape[0]
    tr = _row_block(n, DENSE_ROWS)
    w = PEER_SEL * SC_LANES
    expand = (jnp.arange(w, dtype=jnp.int32)[None, :] // SC_LANES
              == jnp.arange(PEER_SEL, dtype=jnp.int32)[:, None]).astype(F32)
    wide = pl.BlockSpec((tr, w), lambda i: (i, 0))
    return pl.pallas_call(
        _coef_kernel,
        grid=(n // tr,),
        in_specs=[wide, pl.BlockSpec((tr, PEER_SEL), lambda i: (i, 0)),
                  pl.BlockSpec((PEER_SEL, w), lambda i: (0, 0))],
        out_specs=wide,
        out_shape=jax.ShapeDtypeStruct((n, w), jnp.uint32),
        compiler_params=pltpu.CompilerParams(
            dimension_semantics=("parallel",), vmem_limit_bytes=VMEM_LIMIT),
        name="peer_coef",
    )(part, g, expand)


def _final_kernel(h_ref, p_ref, g_ref, o_ref):
    o_ref[...] = _rms(h_ref[...] + p_ref[...], g_ref[...])


def _final(h2, peer_out, gain):
    n = h2.shape[0]
    tm = _row_block(n, 2 * DENSE_ROWS)
    blk = pl.BlockSpec((tm, D_MODEL), lambda i: (i, 0))
    return pl.pallas_call(
        _final_kernel,
        grid=(n // tm,),
        in_specs=[blk, blk, pl.BlockSpec((1, D_MODEL), lambda i: (0, 0))],
        out_specs=blk,
        out_shape=jax.ShapeDtypeStruct((n, D_MODEL), F32),
        compiler_params=pltpu.CompilerParams(
            dimension_semantics=("parallel",), vmem_limit_bytes=VMEM_LIMIT),
        name="final_norm",
    )(h2, peer_out, gain)


def _pad_heads(w, dk):
    lead = w.shape[:-1]
    w = w.reshape(lead + (HEADS, dk))
    w = jnp.pad(w, [(0, 0)] * len(lead) + [(0, 0), (0, LANES - dk)])
    return w.reshape(lead + (HEADS * LANES,))


def _rot_heads(w, dk):
    lead = w.shape[:-1]
    w = w.reshape(lead + (HEADS, 2, dk // 2))
    w = jnp.stack([-w[..., 1, :], w[..., 0, :]], axis=-2)
    return w.reshape(lead + (HEADS * dk,))


def _pack_in_weights(w_in):
    gq, gk, gv, gg, ga, rq, rk, rv, rg = jnp.split(w_in, np.cumsum(IN_SPLITS)[:-1].tolist(), axis=-1)
    ga = jnp.pad(ga, ((0, 0), (0, 2 * LANES - GATE_RANK)))
    return jnp.concatenate(
        [_pad_heads(gq, DK), _pad_heads(gk, DK), gv, gg,
         _pad_heads(rq, DK), _pad_heads(rk, DK), rv, rg,
         _pad_heads(_rot_heads(rq, DK), DK), _pad_heads(_rot_heads(rk, DK), DK), ga],
        axis=-1).astype(BF16)


def _rope_tables(pos):
    inv_freq = ROPE_BASE ** (-jnp.arange(0, DK, 2, dtype=F32) / DK)
    ang = pos[:, None] * inv_freq[None, :]
    full = lambda t: _pad_heads(jnp.tile(t, (1, 2 * HEADS)), DK)
    return full(jnp.cos(ang)), full(jnp.sin(ang))


def _segment_lengths(s, first):
    if first:
        lengths = [s * k // sum(FIRST_SEQ_RAMP) for k in FIRST_SEQ_RAMP]
        assert sum(lengths) == s
    else:
        lengths = [s // SEGS_PER_SEQ] * SEGS_PER_SEQ
    assert all(n % SC_CALL_TOKENS == 0 for n in lengths)
    return lengths


def kernel(x, meta_tokens, norm1_gain, w_in, gla_gate_w2, gla_gate_b, gla_norm_gain, ret_norm_gain, ret_norm_bias, w_out, norm2_gain, peer_w_q, peer_subkeys, peer_u, peer_v, final_gain):
    b, s, d = x.shape
    assert d == D_MODEL and (s // SEGS_PER_SEQ) % SC_CALL_TOKENS == 0
    assert norm1_gain.shape[0] == 1, "single-layer block"
    x2d = x.reshape(b * s, d)

    w_all = _pack_in_weights(w_in[0])
    w2p = jnp.pad(_pad_heads(gla_gate_w2[0], DK), ((0, LANES - GATE_RANK), (0, 0))).astype(BF16)
    gbp = _pad_heads(gla_gate_b[0][None, :], DK)
    g1 = norm1_gain[0][None, :]
    ggain = gla_norm_gain[0][None, :]
    rgain = ret_norm_gain[0][None, :]
    rbias = ret_norm_bias[0][None, :]

    h0 = jnp.concatenate([jnp.zeros((N_PAD, d), x.dtype), meta_tokens.astype(x.dtype)], axis=0)
    cos0, sin0 = _rope_tables(jnp.arange(CHUNK, dtype=F32) - N_PAD)
    cos1, sin1 = _rope_tables(jnp.arange(s, dtype=F32) + N_META)
    zero_state = jnp.zeros((HEADS, DV, LANES), F32)
    _, s_meta, r_meta = _mixer(h0, g1, w_all, cos0, sin0, w2p, gbp, ggain, rgain, rbias,
                               zero_state, zero_state, batch=1, n_invalid=N_PAD)

    w_out_b = w_out[0].astype(BF16)
    g2 = norm2_gain[0][None, :]
    w_q = peer_w_q[0].astype(BF16)
    subkeys = peer_subkeys[0].reshape(2 * PEER_HEADS, PEER_NKEYS, PEER_DKEY // 2).astype(BF16)
    u3 = _pack_table(peer_u[0])
    v3 = _pack_table(peer_v[0])
    gf = final_gain[None, :]

    outs = []
    prev_g = None
    pending = []
    for bi in range(b):
        s_state, r_state = s_meta, r_meta
        start = 0
        for seg in _segment_lengths(s, first=bi == 0):
            xs = x2d[bi * s + start:bi * s + start + seg]
            pos = slice(start, start + seg)
            start += seg
            if prev_g is not None:
                xs, _ = lax.optimization_barrier((xs, prev_g))
            o, s_state, r_state = _mixer(xs, g1, w_all, cos1[pos], sin1[pos], w2p, gbp,
                                         ggain, rgain, rbias, s_state, r_state, batch=1, n_invalid=0)
            h2, xn2, xn2_pairs = _out_proj(o, xs, w_out_b, g2)
            idx, g = _route(xn2, w_q, subkeys)
            prev_g = g
            idx_flat = idx.reshape(-1)
            if len(pending) >= SC_LOOKAHEAD:
                h2_old, peer_old = pending.pop(0)
                idx_flat, peer_old = lax.optimization_barrier((idx_flat, peer_old))
                outs.append(_final(h2_old, peer_old, gf))
            act = _peer_dot_sc(u3, idx_flat, xn2_pairs)
            pending.append((h2, _peer_combine_sc(v3, idx_flat, _coef(act, g))))
    outs.extend(_final(h2_old, peer_old, gf) for h2_old, peer_old in pending)
    return jnp.concatenate(outs, axis=0).reshape(b, s, d)
```

```python
import functools
import math

import jax
import jax.numpy as jnp
import numpy as np
from jax import lax
from jax.experimental import pallas as pl
from jax.experimental.pallas import tpu as pltpu
from jax.experimental.pallas import tpu_sc as plsc

F32 = jnp.float32
BF16 = jnp.bfloat16

D_MODEL = 1024
N_META = 16
CHUNK = 64
N_PAD = CHUNK - N_META
EPS = 1e-6

HEADS = 4
DV = 128
DK = 64
LANES = 128
GATE_RANK = 16
GATE_TAU = 16.0
ROPE_BASE = 10000.0
IN_SPLITS = (HEADS * DK, HEADS * DK, HEADS * DV, HEADS * DV, GATE_RANK,
             HEADS * DK, HEADS * DK, HEADS * DV, HEADS * DV)

PEER_HEADS = 8
PEER_NKEYS = 128
PEER_DKEY = 256
PEER_TOPK = 16
PEER_SEL = PEER_HEADS * PEER_TOPK

HW = HEADS * LANES
OFF_GQ, OFF_GK, OFF_GV, OFF_GG = 0, HW, 2 * HW, 3 * HW
OFF_RQ, OFF_RK, OFF_RV, OFF_RG = 4 * HW, 5 * HW, 6 * HW, 7 * HW
OFF_RQR, OFF_RKR, OFF_GA = 8 * HW, 9 * HW, 10 * HW
PROJ_W = 10 * HW + 2 * LANES
PROJ_TN = PROJ_W // 3

LOG_GAMMA = tuple(math.log(1.0 - 2.0 ** (-5.0 - h)) for h in range(HEADS))

SC_CORES, SC_SUBCORES, SC_LANES = 2, 16, 16
SC_WORKERS = SC_CORES * SC_SUBCORES
SC_TOK_BATCH = 16
SC_ROWS = 16
SC_NBUF = 8
SC_GATHERS = SC_TOK_BATCH * PEER_SEL // SC_ROWS
HALF = D_MODEL // 2
ROW_TILES = HALF // LANES

SC_CALL_TOKENS = SC_WORKERS * SC_TOK_BATCH
SEGS_PER_SEQ = 2
FIRST_SEQ_RAMP = (1, 1, 2, 3, 4, 5)

MIXER_ROWS = 256
DENSE_ROWS = 512
VMEM_LIMIT = 48 * 1024 * 1024


def _row_block(n, target):
    blk = math.gcd(n, target)
    assert blk % 8 == 0 or blk == n, (n, target)
    return blk


def _dot(a, b, **kw):
    return jnp.dot(a, b, preferred_element_type=F32, **kw)


def _dot_nt(a, b):
    return lax.dot_general(a, b, (((1,), (1,)), ((), ())), preferred_element_type=F32)


def _dot_tn(a, b):
    return lax.dot_general(a, b, (((0,), (0,)), ((), ())), preferred_element_type=F32)


def _rms(x, g):
    return x * lax.rsqrt(jnp.mean(x * x, axis=-1, keepdims=True) + EPS) * g


def _silu(x):
    return x * jax.nn.sigmoid(x)


def _mixer_kernel(x_ref, g1_ref, wall_ref, cos_ref, sin_ref, w2_ref, gb_ref, ggain_ref, rgain_ref,
                  rbias_ref, s0_ref, r0_ref, o_ref, s_out_ref, r_out_ref, s_scr, r_scr, proj_ref,
                  *, n_invalid, rows_blk):
    j = pl.program_id(1)
    xn = _rms(x_ref[...], g1_ref[...]).astype(BF16)
    for jn in range(PROJ_W // PROJ_TN):
        cols = slice(jn * PROJ_TN, (jn + 1) * PROJ_TN)
        proj_ref[:, cols] = _dot(xn, wall_ref[:, cols])

    @pl.when(j == 0)
    def _():
        s_scr[...] = s0_ref[...]
        r_scr[...] = r0_ref[...]

    row = lax.broadcasted_iota(jnp.int32, (CHUNK, CHUNK), 0)
    col = lax.broadcasted_iota(jnp.int32, (CHUNK, CHUNK), 1)
    causal = row >= col
    tril = causal.astype(F32)
    diff = jnp.maximum(row - col, 0).astype(F32)
    rowf = lax.broadcasted_iota(jnp.int32, (CHUNK, LANES), 0).astype(F32)
    row1 = lax.broadcasted_iota(jnp.int32, (CHUNK, 1), 0)

    def chunk_body(c, carry):
        r0 = pl.multiple_of(c * CHUNK, CHUNK)
        rows = pl.ds(r0, CHUNK)
        valid = ((j * rows_blk + r0 + row1) >= n_invalid).astype(F32)

        ga = proj_ref[rows, OFF_GA:OFF_GA + LANES].astype(BF16)
        pre = _dot(ga, w2_ref[...]) + gb_ref[...]
        log_a = jax.nn.log_sigmoid(pre) * (1.0 / GATE_TAU)
        bcum = _dot(tril, log_a, precision=lax.Precision.HIGHEST)
        b_last = bcum[CHUNK - 1:CHUNK, :]
        gq = proj_ref[rows, OFF_GQ:OFF_GQ + HW] * (DK ** -0.5)
        gk = proj_ref[rows, OFF_GK:OFF_GK + HW] * valid
        q_t = (gq * jnp.exp(bcum)).astype(BF16)
        k_t = (gk * jnp.exp(-bcum)).astype(BF16)
        k_end = (gk * jnp.exp(b_last - bcum)).astype(BF16)
        decay = jnp.exp(b_last)
        for h in range(HEADS):
            sl = slice(h * LANES, (h + 1) * LANES)
            v = proj_ref[rows, OFF_GV + h * DV:OFF_GV + (h + 1) * DV].astype(BF16)
            att = jnp.where(causal, _dot_nt(q_t[:, sl], k_t[:, sl]), 0.0).astype(BF16)
            s_prev = s_scr[h]
            o = _dot(att, v) + _dot_nt(q_t[:, sl], s_prev.astype(BF16))
            s_scr[h] = s_prev * decay[:, sl] + _dot_tn(v, k_end[:, sl])
            o = o * lax.rsqrt(jnp.mean(o * o, axis=-1, keepdims=True) + EPS) * ggain_ref[...]
            gate = proj_ref[rows, OFF_GG + h * DV:OFF_GG + (h + 1) * DV]
            o_ref[rows, h * DV:(h + 1) * DV] = (o * _silu(gate)).astype(o_ref.dtype)

        cosb = cos_ref[rows, :]
        sinb = sin_ref[rows, :]
        rq = proj_ref[rows, OFF_RQ:OFF_RQ + HW]
        rqr = proj_ref[rows, OFF_RQR:OFF_RQR + HW]
        rk = proj_ref[rows, OFF_RK:OFF_RK + HW] * valid
        rkr = proj_ref[rows, OFF_RKR:OFF_RKR + HW] * valid
        q_r = (rq * cosb + rqr * sinb).astype(BF16)
        k_r = (rk * cosb + rkr * sinb) * (DK ** -0.5)
        for h in range(HEADS):
            sl = slice(h * LANES, (h + 1) * LANES)
            lg = LOG_GAMMA[h]
            dmat = jnp.where(causal, jnp.exp(lg * diff), 0.0)
            xi = jnp.exp(lg * (rowf + 1.0))
            zeta = jnp.exp(lg * (CHUNK - 1.0 - rowf))
            v = proj_ref[rows, OFF_RV + h * DV:OFF_RV + (h + 1) * DV].astype(BF16)
            k_h = k_r[:, sl]
            att = (_dot_nt(q_r[:, sl], k_h.astype(BF16)) * dmat).astype(BF16)
            r_prev = r_scr[h]
            o = _dot(att, v) + _dot_nt(q_r[:, sl], r_prev.astype(BF16)) * xi
            r_scr[h] = r_prev * math.exp(lg * CHUNK) + _dot_tn(v, (k_h * zeta).astype(BF16))
            mu = jnp.mean(o, axis=-1, keepdims=True)
            var = jnp.mean(jnp.square(o - mu), axis=-1, keepdims=True)
            o = (o - mu) * lax.rsqrt(var + EPS) * rgain_ref[...] + rbias_ref[...]
            gate = proj_ref[rows, OFF_RG + h * DV:OFF_RG + (h + 1) * DV]
            o_ref[rows, HW + h * DV:HW + (h + 1) * DV] = (o * _silu(gate)).astype(o_ref.dtype)
        return carry

    lax.fori_loop(0, rows_blk // CHUNK, chunk_body, 0)
    s_out_ref[...] = s_scr[...]
    r_out_ref[...] = r_scr[...]


def _mixer(x2d, g1, w_all, cosf, sinf, w2p, gbp, ggain, rgain, rbias, s0, r0, *, batch, n_invalid):
    rows = x2d.shape[0] // batch
    rows_blk = _row_block(rows, MIXER_ROWS)
    nblk = rows // rows_blk
    const2 = lambda b, j: (0, 0)
    const3 = lambda b, j: (0, 0, 0)
    state = jax.ShapeDtypeStruct((HEADS, DV, LANES), F32)
    return pl.pallas_call(
        functools.partial(_mixer_kernel, n_invalid=n_invalid, rows_blk=rows_blk),
        grid=(batch, nblk),
        in_specs=[
            pl.BlockSpec((rows_blk, D_MODEL), lambda b, j: (b * nblk + j, 0)),
            pl.BlockSpec((1, D_MODEL), const2),
            pl.BlockSpec((D_MODEL, PROJ_W), const2),
            pl.BlockSpec((rows_blk, HW), lambda b, j: (j, 0)),
            pl.BlockSpec((rows_blk, HW), lambda b, j: (j, 0)),
            pl.BlockSpec((LANES, HW), const2),
            pl.BlockSpec((1, HW), const2),
            pl.BlockSpec((1, DV), const2),
            pl.BlockSpec((1, DV), const2),
            pl.BlockSpec((1, DV), const2),
            pl.BlockSpec((HEADS, DV, LANES), const3),
            pl.BlockSpec((HEADS, DV, LANES), const3),
        ],
        out_specs=[
            pl.BlockSpec((rows_blk, D_MODEL), lambda b, j: (b * nblk + j, 0)),
            pl.BlockSpec((HEADS, DV, LANES), const3),
            pl.BlockSpec((HEADS, DV, LANES), const3),
        ],
        out_shape=[jax.ShapeDtypeStruct((batch * rows, D_MODEL), BF16), state, state],
        scratch_shapes=[pltpu.VMEM((HEADS, DV, LANES), F32), pltpu.VMEM((HEADS, DV, LANES), F32),
                        pltpu.VMEM((rows_blk, PROJ_W), F32)],
        compiler_params=pltpu.CompilerParams(
            dimension_semantics=("arbitrary", "arbitrary"), vmem_limit_bytes=VMEM_LIMIT),
        name="token_mixers",
    )(x2d, g1, w_all, cosf, sinf, w2p, gbp, ggain, rgain, rbias, s0, r0)


def _out_proj_kernel(o_ref, x_ref, w_ref, g_ref, h_ref, xn_ref, xp_ref):
    h = x_ref[...] + _dot(o_ref[...], w_ref[...])
    h_ref[...] = h
    xn = _rms(h, g_ref[...])
    xn_ref[...] = xn
    xp_ref[...] = _pack_pairs(xn[:, :HALF], xn[:, HALF:])


def _out_proj(o, x2d, w_out, gain2):
    n = x2d.shape[0]
    tm = _row_block(n, DENSE_ROWS)
    blk = pl.BlockSpec((tm, D_MODEL), lambda i: (i, 0))
    return pl.pallas_call(
        _out_proj_kernel,
        grid=(n // tm,),
        in_specs=[blk, blk, pl.BlockSpec((D_MODEL, D_MODEL), lambda i: (0, 0)),
                  pl.BlockSpec((1, D_MODEL), lambda i: (0, 0))],
        out_specs=[blk, blk, pl.BlockSpec((tm, HALF), lambda i: (i, 0))],
        out_shape=[jax.ShapeDtypeStruct((n, D_MODEL), F32)] * 2 + [jax.ShapeDtypeStruct((n, HALF), jnp.uint32)],
        compiler_params=pltpu.CompilerParams(
            dimension_semantics=("parallel",), vmem_limit_bytes=VMEM_LIMIT),
        name="out_proj_norm",
    )(o, x2d, w_out, gain2)


def _top_k_rows(arrays, payloads=None):
    arrays = list(arrays)
    n = arrays[0].shape[0]
    rowi = lax.broadcasted_iota(jnp.int32, arrays[0].shape, 0)
    vals = [[] for _ in arrays]
    picks = [[] for _ in arrays]
    for _ in range(PEER_TOPK):
        for k, s in enumerate(arrays):
            m = jnp.max(s, axis=0, keepdims=True)
            am = jnp.min(jnp.where(s == m, rowi, n), axis=0, keepdims=True)
            sel = rowi == am
            vals[k].append(m)
            if payloads is None:
                picks[k].append(am)
            else:
                picks[k].append(jnp.min(jnp.where(sel, payloads[k], jnp.iinfo(jnp.int32).max),
                                        axis=0, keepdims=True))
            arrays[k] = jnp.where(sel, -jnp.inf, s)
    return [(jnp.concatenate(v, axis=0), jnp.concatenate(p, axis=0)) for v, p in zip(vals, picks)]


def _candidate_cells():
    cells = [(a, b) for a in range(PEER_TOPK) for b in range(PEER_TOPK) if (a + 1) * (b + 1) <= PEER_TOPK]
    n = -(-len(cells) // 8) * 8
    sel_a = np.zeros((n, PEER_TOPK), np.float32)
    sel_b = np.zeros((n, PEER_TOPK), np.float32)
    pad = np.full((n, 1), -np.inf, np.float32)
    for r, (a, b) in enumerate(cells):
        sel_a[r, a] = 1.0
        sel_b[r, b] = 1.0
        pad[r, 0] = 0.0
    return sel_a, sel_b, pad


def _route_kernel(xn_ref, wq_ref, sk_ref, sa_ref, sb_ref, pad_ref, idx_ref, g_ref, q_scr, i_scr, g_scr):
    tr = xn_ref.shape[0]
    q = _dot(xn_ref[...].astype(BF16), wq_ref[...])
    for hc in range(2 * PEER_HEADS):
        q_scr[hc] = q[:, hc * LANES:(hc + 1) * LANES].astype(BF16)

    def sub_body(sb, carry):
        rows = pl.ds(pl.multiple_of(sb * LANES, LANES), LANES)

        def head_pair(hp, c2):
            heads = [2 * hp, 2 * hp + 1]
            scores = [_dot_nt(sk_ref[2 * h + c], q_scr[2 * h + c, rows, :])
                      for h in heads for c in range(2)]
            first = _top_k_rows(scores)
            sa, sb = sa_ref[...], sb_ref[...]
            cands, cidxs = [], []
            for k in range(len(heads)):
                (v0, i0), (v1, i1) = first[2 * k], first[2 * k + 1]
                cands.append((_dot(sa, v0, precision=lax.Precision.HIGHEST)
                              + _dot(sb, v1, precision=lax.Precision.HIGHEST)) + pad_ref[...])
                cidxs.append((_dot(sa, i0.astype(F32)) * PEER_NKEYS
                              + _dot(sb, i1.astype(F32))).astype(jnp.int32))
            for h, (best, e) in zip(heads, _top_k_rows(cands, cidxs)):
                ex = jnp.exp(best - best[0:1])
                i_scr[h] = e
                g_scr[h] = ex / jnp.sum(ex, axis=0, keepdims=True)
            return c2

        lax.fori_loop(0, PEER_HEADS // 2, head_pair, 0)
        idx_ref[rows, :] = i_scr[...].reshape(PEER_SEL, LANES).T
        g_ref[rows, :] = g_scr[...].reshape(PEER_SEL, LANES).T
        return carry

    lax.fori_loop(0, tr // LANES, sub_body, 0)


def _route(xn2, w_q, subkeys):
    n = xn2.shape[0]
    sel_a, sel_b, pad = _candidate_cells()
    tr = _row_block(n, DENSE_ROWS)
    qw = 2 * PEER_HEADS * LANES
    return pl.pallas_call(
        _route_kernel,
        grid=(n // tr,),
        in_specs=[pl.BlockSpec((tr, D_MODEL), lambda i: (i, 0)),
                  pl.BlockSpec((D_MODEL, qw), lambda i: (0, 0)),
                  pl.BlockSpec((2 * PEER_HEADS, PEER_NKEYS, LANES), lambda i: (0, 0, 0)),
                  pl.BlockSpec(sel_a.shape, lambda i: (0, 0)),
                  pl.BlockSpec(sel_b.shape, lambda i: (0, 0)),
                  pl.BlockSpec(pad.shape, lambda i: (0, 0))],
        out_specs=[pl.BlockSpec((tr, PEER_SEL), lambda i: (i, 0))] * 2,
        out_shape=[jax.ShapeDtypeStruct((n, PEER_SEL), jnp.int32),
                   jax.ShapeDtypeStruct((n, PEER_SEL), F32)],
        scratch_shapes=[pltpu.VMEM((2 * PEER_HEADS, tr, LANES), BF16),
                        pltpu.VMEM((PEER_HEADS, PEER_TOPK, LANES), jnp.int32),
                        pltpu.VMEM((PEER_HEADS, PEER_TOPK, LANES), F32)],
        compiler_params=pltpu.CompilerParams(
            dimension_semantics=("parallel",), vmem_limit_bytes=VMEM_LIMIT),
        name="peer_route",
    )(xn2, w_q, subkeys, sel_a, sel_b, pad)


def _sc_worker_id():
    return lax.axis_index("s") * SC_CORES + lax.axis_index("c")


def _sc_mesh():
    return plsc.VectorSubcoreMesh(core_axis_name="c", subcore_axis_name="s")


def _sc_gather_ring(table_hbm, idx_v, bufs, sems, unit, compute):
    def gather(g, k):
        off = pl.multiple_of(g * SC_ROWS, SC_ROWS)
        return pltpu.make_async_copy(table_hbm.at[idx_v.at[pl.ds(off, SC_ROWS)]], bufs[k], sems[k])

    for k in range(SC_NBUF):
        gather(k, k).start()

    units_per_iter = SC_NBUF // unit

    def ring_iter(i, carry):
        for m in range(units_per_iter):
            ks = range(m * unit, (m + 1) * unit)
            for k in ks:
                gather(i * SC_NBUF + k, k).wait()
            compute(i * units_per_iter + m, [bufs[k] for k in ks])
            for k in ks:
                g_next = (i + 1) * SC_NBUF + k

                @pl.when(g_next < SC_GATHERS)
                def _():
                    gather(g_next, k).start()
        return carry

    lax.fori_loop(0, SC_GATHERS // SC_NBUF, ring_iter, 0)


def _bf16_bits(x):
    b = lax.bitcast_convert_type(x, jnp.uint32)
    return (b + jnp.uint32(0x7FFF) + ((b >> 16) & jnp.uint32(1))) >> 16


def _pack_pairs(lo, hi):
    return _bf16_bits(lo) | (_bf16_bits(hi) << 16)


def _pack_table(t):
    return _pack_pairs(t[:, :HALF], t[:, HALF:]).reshape(t.shape[0], ROW_TILES, LANES)


def _mul_add_words(w1, m1, w2, m2):
    bf = lambda a: plsc.bitcast(a, jnp.bfloat16)
    p = plsc.bitcast(bf(w1) * bf(m1) + bf(w2) * bf(m2), jnp.uint32)
    return lax.bitcast_convert_type(p << 16, F32), lax.bitcast_convert_type(p, F32)


def _sc_scratch(staged_cols, out_cols):
    return ([pltpu.VMEM((SC_TOK_BATCH * PEER_SEL,), jnp.int32),
             pltpu.VMEM((SC_TOK_BATCH, staged_cols), jnp.uint32),
             pltpu.VMEM((SC_TOK_BATCH, out_cols), F32)]
            + [pltpu.VMEM((SC_ROWS, ROW_TILES, LANES), jnp.uint32) for _ in range(SC_NBUF)]
            + [pltpu.SemaphoreType.DMA for _ in range(SC_NBUF)])


def _sc_batches(tpw, idx_hbm, staged_hbm, out_hbm, idx_v, staged_v, out_v, run):
    tok0 = _sc_worker_id() * tpw

    def batch(tb, carry):
        t0 = pl.multiple_of(tok0 + tb * SC_TOK_BATCH, SC_TOK_BATCH)
        pltpu.sync_copy(idx_hbm.at[pl.ds(pl.multiple_of(t0 * PEER_SEL, SC_TOK_BATCH * PEER_SEL),
                                         SC_TOK_BATCH * PEER_SEL)], idx_v)
        pltpu.sync_copy(staged_hbm.at[pl.ds(t0, SC_TOK_BATCH)], staged_v)
        run()
        pltpu.sync_copy(out_v, out_hbm.at[pl.ds(t0, SC_TOK_BATCH)])
        return carry

    lax.fori_loop(0, tpw // SC_TOK_BATCH, batch, 0)


def _peer_dot_sc(u3, idx_flat, xn2):
    t = xn2.shape[0]
    unit_rows = 16
    unit = unit_rows // SC_ROWS
    units_per_tok = PEER_SEL // unit_rows

    @functools.partial(
        pl.kernel,
        mesh=_sc_mesh(),
        out_type=jax.ShapeDtypeStruct((t, PEER_SEL * SC_LANES), F32),
        scratch_types=_sc_scratch(HALF, PEER_SEL * SC_LANES),
        compiler_params=pltpu.CompilerParams(needs_layout_passes=False),
        name="peer_dot",
    )
    def body(u_hbm, idx_hbm, x_hbm, out_hbm, idx_v, x_v, out_v, *rest):
        bufs, sems = rest[:SC_NBUF], rest[SC_NBUF:]

        def compute(u, ub):
            tl = u // units_per_tok
            rb = (u % units_per_tok) * unit_rows

            def cbody(c, acc):
                acc = list(acc)
                c4 = c // (LANES // (2 * SC_LANES))
                lane0 = pl.multiple_of((c % (LANES // (2 * SC_LANES))) * 2 * SC_LANES, 2 * SC_LANES)
                col = pl.multiple_of(c * 2 * SC_LANES, 2 * SC_LANES)
                xw1 = x_v[tl, pl.ds(col, SC_LANES)]
                xw2 = x_v[tl, pl.ds(col + SC_LANES, SC_LANES)]
                for q, buf in enumerate(ub):
                    for r in range(SC_ROWS):
                        a = q * SC_ROWS + r
                        lo, hi = _mul_add_words(buf[r, c4, pl.ds(lane0, SC_LANES)], xw1,
                                                buf[r, c4, pl.ds(lane0 + SC_LANES, SC_LANES)], xw2)
                        acc[a] = acc[a] + lo + hi
                return tuple(acc)

            acc = lax.fori_loop(0, HALF // (2 * SC_LANES), cbody,
                                tuple(jnp.zeros((SC_LANES,), F32) for _ in range(unit_rows)))
            for a in range(unit_rows):
                o = pl.multiple_of((rb + a) * SC_LANES, SC_LANES)
                out_v[tl, pl.ds(o, SC_LANES)] = acc[a]

        _sc_batches(t // SC_WORKERS, idx_hbm, x_hbm, out_hbm, idx_v, x_v, out_v,
                    lambda: _sc_gather_ring(u_hbm, idx_v, bufs, sems, unit, compute))

    return body(u3, idx_flat, xn2)


def _peer_combine_sc(v3, idx_flat, coef_b):
    t = coef_b.shape[0]
    unit = 2
    units_per_tok = PEER_SEL // (unit * SC_ROWS)
    nvec = LANES // SC_LANES

    @functools.partial(
        pl.kernel,
        mesh=_sc_mesh(),
        out_type=jax.ShapeDtypeStruct((t, D_MODEL), F32),
        scratch_types=_sc_scratch(PEER_SEL * SC_LANES, D_MODEL),
        compiler_params=pltpu.CompilerParams(needs_layout_passes=False),
        name="peer_combine",
    )
    def body(v_hbm, idx_hbm, coef_hbm, out_hbm, idx_v, coef_v, out_v, *rest):
        bufs, sems = rest[:SC_NBUF], rest[SC_NBUF:]

        def compute(u, ub):
            tl = u // units_per_tok
            rb = (u % units_per_tok) * unit * SC_ROWS
            for g in range(ROW_TILES):
                cols = ([pl.ds(g * LANES + l * SC_LANES, SC_LANES) for l in range(nvec)]
                        + [pl.ds(HALF + g * LANES + l * SC_LANES, SC_LANES) for l in range(nvec)])

                def rbody(r, acc):
                    acc = list(acc)
                    co = pl.multiple_of((rb + r) * SC_LANES, SC_LANES)
                    cf1 = coef_v[tl, pl.ds(co, SC_LANES)]
                    cf2 = coef_v[tl, pl.ds(co + SC_ROWS * SC_LANES, SC_LANES)]
                    for l in range(nvec):
                        lo, hi = _mul_add_words(ub[0][r, g, pl.ds(l * SC_LANES, SC_LANES)], cf1,
                                                ub[1][r, g, pl.ds(l * SC_LANES, SC_LANES)], cf2)
                        acc[l] = acc[l] + lo
                        acc[nvec + l] = acc[nvec + l] + hi
                    return tuple(acc)

                acc = lax.fori_loop(0, SC_ROWS, rbody, tuple(out_v[tl, c] for c in cols))
                for k, c in enumerate(cols):
                    out_v[tl, c] = acc[k]

        def run():
            def zero(tl, c2):
                for k in range(D_MODEL // SC_LANES):
                    out_v[tl, pl.ds(k * SC_LANES, SC_LANES)] = jnp.zeros((SC_LANES,), F32)
                return c2

            lax.fori_loop(0, SC_TOK_BATCH, zero, 0)
            _sc_gather_ring(v_hbm, idx_v, bufs, sems, unit, compute)

        _sc_batches(t // SC_WORKERS, idx_hbm, coef_hbm, out_hbm, idx_v, coef_v, out_v, run)

    return body(v3, idx_flat, coef_b)


def _coef_kernel(part_ref, g_ref, e_ref, coef_ref):
    act = lax.dot_general(part_ref[...], e_ref[...], (((1,), (1,)), ((), ())),
                          precision=lax.Precision.HIGHEST, preferred_element_type=F32)
    gelu = 0.5 * act * (1.0 + lax.erf(act * np.float32(math.sqrt(0.5))))
    c = _dot(g_ref[...] * gelu, e_ref[...], precision=lax.Precision.HIGHEST)
    coef_ref[...] = _pack_pairs(c, c)


def _coef(part, g):
    n = part.shape[0]
    tr = _row_block(n, DENSE_ROWS)
    w = PEER_SEL * SC_LANES
    expand = (jnp.arange(w, dtype=jnp.int32)[None, :] // SC_LANES
              == jnp.arange(PEER_SEL, dtype=jnp.int32)[:, None]).astype(F32)
    wide = pl.BlockSpec((tr, w), lambda i: (i, 0))
    return pl.pallas_call(
        _coef_kernel,
        grid=(n // tr,),
        in_specs=[wide, pl.BlockSpec((tr, PEER_SEL), lambda i: (i, 0)),
                  pl.BlockSpec((PEER_SEL, w), lambda i: (0, 0))],
        out_specs=wide,
        out_shape=jax.ShapeDtypeStruct((n, w), jnp.uint32),
        compiler_params=pltpu.CompilerParams(
            dimension_semantics=("parallel",), vmem_limit_bytes=VMEM_LIMIT),
        name="peer_coef",
    )(part, g, expand)


def _final_kernel(h_ref, p_ref, g_ref, o_ref):
    o_ref[...] = _rms(h_ref[...] + p_ref[...], g_ref[...])


def _final(h2, peer_out, gain):
    n = h2.shape[0]
    tm = _row_block(n, 2 * DENSE_ROWS)
    blk = pl.BlockSpec((tm, D_MODEL), lambda i: (i, 0))
    return pl.pallas_call(
        _final_kernel,
        grid=(n // tm,),
        in_specs=[blk, blk, pl.BlockSpec((1, D_MODEL), lambda i: (0, 0))],
        out_specs=blk,
        out_shape=jax.ShapeDtypeStruct((n, D_MODEL), F32),
        compiler_params=pltpu.CompilerParams(
            dimension_semantics=("parallel",), vmem_limit_bytes=VMEM_LIMIT),
        name="final_norm",
    )(h2, peer_out, gain)


def _pad_heads(w, dk):
    lead = w.shape[:-1]
    w = w.reshape(lead + (HEADS, dk))
    w = jnp.pad(w, [(0, 0)] * len(lead) + [(0, 0), (0, LANES - dk)])
    return w.reshape(lead + (HEADS * LANES,))


def _rot_heads(w, dk):
    lead = w.shape[:-1]
    w = w.reshape(lead + (HEADS, 2, dk // 2))
    w = jnp.stack([-w[..., 1, :], w[..., 0, :]], axis=-2)
    return w.reshape(lead + (HEADS * dk,))


def _pack_in_weights(w_in):
    gq, gk, gv, gg, ga, rq, rk, rv, rg = jnp.split(w_in, np.cumsum(IN_SPLITS)[:-1].tolist(), axis=-1)
    ga = jnp.pad(ga, ((0, 0), (0, 2 * LANES - GATE_RANK)))
    return jnp.concatenate(
        [_pad_heads(gq, DK), _pad_heads(gk, DK), gv, gg,
         _pad_heads(rq, DK), _pad_heads(rk, DK), rv, rg,
         _pad_heads(_rot_heads(rq, DK), DK), _pad_heads(_rot_heads(rk, DK), DK), ga],
        axis=-1).astype(BF16)


def _rope_tables(pos):
    inv_freq = ROPE_BASE ** (-jnp.arange(0, DK, 2, dtype=F32) / DK)
    ang = pos[:, None] * inv_freq[None, :]
    full = lambda t: _pad_heads(jnp.tile(t, (1, 2 * HEADS)), DK)
    return full(jnp.cos(ang)), full(jnp.sin(ang))


def _segment_lengths(s, first):
    if first:
        lengths = [s * k // sum(FIRST_SEQ_RAMP) for k in FIRST_SEQ_RAMP]
        assert sum(lengths) == s
    else:
        lengths = [s // SEGS_PER_SEQ] * SEGS_PER_SEQ
    assert all(n % SC_CALL_TOKENS == 0 for n in lengths)
    return lengths


def kernel(x, meta_tokens, norm1_gain, w_in, gla_gate_w2, gla_gate_b, gla_norm_gain, ret_norm_gain, ret_norm_bias, w_out, norm2_gain, peer_w_q, peer_subkeys, peer_u, peer_v, final_gain):
    b, s, d = x.shape
    assert d == D_MODEL and (s // SEGS_PER_SEQ) % SC_CALL_TOKENS == 0
    assert norm1_gain.shape[0] == 1, "single-layer block"
    x2d = x.reshape(b * s, d)

    w_all = _pack_in_weights(w_in[0])
    w2p = jnp.pad(_pad_heads(gla_gate_w2[0], DK), ((0, LANES - GATE_RANK), (0, 0))).astype(BF16)
    gbp = _pad_heads(gla_gate_b[0][None, :], DK)
    g1 = norm1_gain[0][None, :]
    ggain = gla_norm_gain[0][None, :]
    rgain = ret_norm_gain[0][None, :]
    rbias = ret_norm_bias[0][None, :]

    h0 = jnp.concatenate([jnp.zeros((N_PAD, d), x.dtype), meta_tokens.astype(x.dtype)], axis=0)
    cos0, sin0 = _rope_tables(jnp.arange(CHUNK, dtype=F32) - N_PAD)
    cos1, sin1 = _rope_tables(jnp.arange(s, dtype=F32) + N_META)
    zero_state = jnp.zeros((HEADS, DV, LANES), F32)
    _, s_meta, r_meta = _mixer(h0, g1, w_all, cos0, sin0, w2p, gbp, ggain, rgain, rbias,
                               zero_state, zero_state, batch=1, n_invalid=N_PAD)

    w_out_b = w_out[0].astype(BF16)
    g2 = norm2_gain[0][None, :]
    w_q = peer_w_q[0].astype(BF16)
    subkeys = peer_subkeys[0].reshape(2 * PEER_HEADS, PEER_NKEYS, PEER_DKEY // 2).astype(BF16)
    u3 = _pack_table(peer_u[0])
    v3 = _pack_table(peer_v[0])
    gf = final_gain[None, :]

    outs = []
    prev_g = None
    pending = []
    waiting = None
    for bi in range(b):
        s_state, r_state = s_meta, r_meta
        start = 0
        for seg in _segment_lengths(s, first=bi == 0):
            xs = x2d[bi * s + start:bi * s + start + seg]
            pos = slice(start, start + seg)
            start += seg
            if prev_g is not None:
                xs, _ = lax.optimization_barrier((xs, prev_g))
            o, s_state, r_state = _mixer(xs, g1, w_all, cos1[pos], sin1[pos], w2p, gbp,
                                         ggain, rgain, rbias, s_state, r_state, batch=1, n_invalid=0)
            h2, xn2, xn2_pairs = _out_proj(o, xs, w_out_b, g2)
            idx, g = _route(xn2, w_q, subkeys)
            prev_g = g
            idx_flat = idx.reshape(-1)
            if pending:
                h2_old, peer_old = pending.pop(0)
                idx_flat, peer_old = lax.optimization_barrier((idx_flat, peer_old))
                outs.append(_final(h2_old, peer_old, gf))
            part = _peer_dot_sc(u3, idx_flat, xn2_pairs)
            if waiting is not None:
                h2_w, idx_w, part_w, g_w = waiting
                part_w, _ = lax.optimization_barrier((part_w, g))
                pending.append((h2_w, _peer_combine_sc(v3, idx_w, _coef(part_w, g_w))))
            waiting = (h2, idx_flat, part, g)
    h2_w, idx_w, part_w, g_w = waiting
    pending.append((h2_w, _peer_combine_sc(v3, idx_w, _coef(part_w, g_w))))
    outs.extend(_final(h2_old, peer_old, gf) for h2_old, peer_old in pending)
    return jnp.concatenate(outs, axis=0).reshape(b, s, d)
```

```python
import functools
import math

import jax
import jax.numpy as jnp
import numpy as np
from jax import lax
from jax.experimental import pallas as pl
from jax.experimental.pallas import tpu as pltpu
from jax.experimental.pallas import tpu_sc as plsc

F32 = jnp.float32
BF16 = jnp.bfloat16

D_MODEL = 1024
N_META = 16
CHUNK = 64
N_PAD = CHUNK - N_META
EPS = 1e-6

HEADS = 4
DV = 128
DK = 64
LANES = 128
GATE_RANK = 16
GATE_TAU = 16.0
ROPE_BASE = 10000.0
IN_SPLITS = (HEADS * DK, HEADS * DK, HEADS * DV, HEADS * DV, GATE_RANK,
             HEADS * DK, HEADS * DK, HEADS * DV, HEADS * DV)

PEER_HEADS = 8
PEER_NKEYS = 128
PEER_DKEY = 256
PEER_TOPK = 16
PEER_SEL = PEER_HEADS * PEER_TOPK

HW = HEADS * LANES
OFF_GQ, OFF_GK, OFF_GV, OFF_GG = 0, HW, 2 * HW, 3 * HW
OFF_RQ, OFF_RK, OFF_RV, OFF_RG = 4 * HW, 5 * HW, 6 * HW, 7 * HW
OFF_RQR, OFF_RKR, OFF_GA = 8 * HW, 9 * HW, 10 * HW
PROJ_W = 10 * HW + 2 * LANES
PROJ_TN = PROJ_W // 3

LOG_GAMMA = tuple(math.log(1.0 - 2.0 ** (-5.0 - h)) for h in range(HEADS))

SC_CORES, SC_SUBCORES, SC_LANES = 2, 16, 16
SC_WORKERS = SC_CORES * SC_SUBCORES
SC_TOK_BATCH = 16
SC_ROWS = 16
SC_NBUF = 8
SC_GATHERS = SC_TOK_BATCH * PEER_SEL // SC_ROWS
HALF = D_MODEL // 2
ROW_TILES = HALF // LANES

SC_CALL_TOKENS = SC_WORKERS * SC_TOK_BATCH
SEGS_PER_SEQ = 2
FIRST_SEQ_RAMP = (1, 1, 2, 3, 4, 5)

MIXER_ROWS = 256
DENSE_ROWS = 512
VMEM_LIMIT = 48 * 1024 * 1024


def _row_block(n, target):
    blk = math.gcd(n, target)
    assert blk % 8 == 0 or blk == n, (n, target)
    return blk


def _dot(a, b, **kw):
    return jnp.dot(a, b, preferred_element_type=F32, **kw)


def _dot_nt(a, b):
    return lax.dot_general(a, b, (((1,), (1,)), ((), ())), preferred_element_type=F32)


def _dot_tn(a, b):
    return lax.dot_general(a, b, (((0,), (0,)), ((), ())), preferred_element_type=F32)


def _rms(x, g):
    return x * lax.rsqrt(jnp.mean(x * x, axis=-1, keepdims=True) + EPS) * g


def _silu(x):
    return x * jax.nn.sigmoid(x)


def _mixer_kernel(x_ref, g1_ref, wall_ref, cos_ref, sin_ref, w2_ref, gb_ref, ggain_ref, rgain_ref,
                  rbias_ref, s0_ref, r0_ref, o_ref, s_out_ref, r_out_ref, s_scr, r_scr, proj_ref,
                  *, n_invalid, rows_blk):
    j = pl.program_id(1)
    xn = _rms(x_ref[...], g1_ref[...]).astype(BF16)
    for jn in range(PROJ_W // PROJ_TN):
        cols = slice(jn * PROJ_TN, (jn + 1) * PROJ_TN)
        proj_ref[:, cols] = _dot(xn, wall_ref[:, cols])

    @pl.when(j == 0)
    def _():
        s_scr[...] = s0_ref[...]
        r_scr[...] = r0_ref[...]

    row = lax.broadcasted_iota(jnp.int32, (CHUNK, CHUNK), 0)
    col = lax.broadcasted_iota(jnp.int32, (CHUNK, CHUNK), 1)
    causal = row >= col
    tril = causal.astype(F32)
    diff = jnp.maximum(row - col, 0).astype(F32)
    rowf = lax.broadcasted_iota(jnp.int32, (CHUNK, LANES), 0).astype(F32)
    row1 = lax.broadcasted_iota(jnp.int32, (CHUNK, 1), 0)

    def chunk_body(c, carry):
        r0 = pl.multiple_of(c * CHUNK, CHUNK)
        rows = pl.ds(r0, CHUNK)
        valid = ((j * rows_blk + r0 + row1) >= n_invalid).astype(F32)

        ga = proj_ref[rows, OFF_GA:OFF_GA + LANES].astype(BF16)
        pre = _dot(ga, w2_ref[...]) + gb_ref[...]
        log_a = jax.nn.log_sigmoid(pre) * (1.0 / GATE_TAU)
        bcum = _dot(tril, log_a, precision=lax.Precision.HIGHEST)
        b_last = bcum[CHUNK - 1:CHUNK, :]
        gq = proj_ref[rows, OFF_GQ:OFF_GQ + HW] * (DK ** -0.5)
        gk = proj_ref[rows, OFF_GK:OFF_GK + HW] * valid
        q_t = (gq * jnp.exp(bcum)).astype(BF16)
        k_t = (gk * jnp.exp(-bcum)).astype(BF16)
        k_end = (gk * jnp.exp(b_last - bcum)).astype(BF16)
        decay = jnp.exp(b_last)
        for h in range(HEADS):
            sl = slice(h * LANES, (h + 1) * LANES)
            v = proj_ref[rows, OFF_GV + h * DV:OFF_GV + (h + 1) * DV].astype(BF16)
            att = jnp.where(causal, _dot_nt(q_t[:, sl], k_t[:, sl]), 0.0).astype(BF16)
            s_prev = s_scr[h]
            o = _dot(att, v) + _dot_nt(q_t[:, sl], s_prev.astype(BF16))
            s_scr[h] = s_prev * decay[:, sl] + _dot_tn(v, k_end[:, sl])
            o = o * lax.rsqrt(jnp.mean(o * o, axis=-1, keepdims=True) + EPS) * ggain_ref[...]
            gate = proj_ref[rows, OFF_GG + h * DV:OFF_GG + (h + 1) * DV]
            o_ref[rows, h * DV:(h + 1) * DV] = (o * _silu(gate)).astype(o_ref.dtype)

        cosb = cos_ref[rows, :]
        sinb = sin_ref[rows, :]
        rq = proj_ref[rows, OFF_RQ:OFF_RQ + HW]
        rqr = proj_ref[rows, OFF_RQR:OFF_RQR + HW]
        rk = proj_ref[rows, OFF_RK:OFF_RK + HW] * valid
        rkr = proj_ref[rows, OFF_RKR:OFF_RKR + HW] * valid
        q_r = (rq * cosb + rqr * sinb).astype(BF16)
        k_r = (rk * cosb + rkr * sinb) * (DK ** -0.5)
        for h in range(HEADS):
            sl = slice(h * LANES, (h + 1) * LANES)
            lg = LOG_GAMMA[h]
            dmat = jnp.where(causal, jnp.exp(lg * diff), 0.0)
            xi = jnp.exp(lg * (rowf + 1.0))
            zeta = jnp.exp(lg * (CHUNK - 1.0 - rowf))
            v = proj_ref[rows, OFF_RV + h * DV:OFF_RV + (h + 1) * DV].astype(BF16)
            k_h = k_r[:, sl]
            att = (_dot_nt(q_r[:, sl], k_h.astype(BF16)) * dmat).astype(BF16)
            r_prev = r_scr[h]
            o = _dot(att, v) + _dot_nt(q_r[:, sl], r_prev.astype(BF16)) * xi
            r_scr[h] = r_prev * math.exp(lg * CHUNK) + _dot_tn(v, (k_h * zeta).astype(BF16))
            mu = jnp.mean(o, axis=-1, keepdims=True)
            var = jnp.mean(jnp.square(o - mu), axis=-1, keepdims=True)
            o = (o - mu) * lax.rsqrt(var + EPS) * rgain_ref[...] + rbias_ref[...]
            gate = proj_ref[rows, OFF_RG + h * DV:OFF_RG + (h + 1) * DV]
            o_ref[rows, HW + h * DV:HW + (h + 1) * DV] = (o * _silu(gate)).astype(o_ref.dtype)
        return carry

    lax.fori_loop(0, rows_blk // CHUNK, chunk_body, 0)
    s_out_ref[...] = s_scr[...]
    r_out_ref[...] = r_scr[...]


def _mixer(x2d, g1, w_all, cosf, sinf, w2p, gbp, ggain, rgain, rbias, s0, r0, *, batch, n_invalid):
    rows = x2d.shape[0] // batch
    rows_blk = _row_block(rows, MIXER_ROWS)
    nblk = rows // rows_blk
    const2 = lambda b, j: (0, 0)
    const3 = lambda b, j: (0, 0, 0)
    state = jax.ShapeDtypeStruct((HEADS, DV, LANES), F32)
    return pl.pallas_call(
        functools.partial(_mixer_kernel, n_invalid=n_invalid, rows_blk=rows_blk),
        grid=(batch, nblk),
        in_specs=[
            pl.BlockSpec((rows_blk, D_MODEL), lambda b, j: (b * nblk + j, 0)),
            pl.BlockSpec((1, D_MODEL), const2),
            pl.BlockSpec((D_MODEL, PROJ_W), const2),
            pl.BlockSpec((rows_blk, HW), lambda b, j: (j, 0)),
            pl.BlockSpec((rows_blk, HW), lambda b, j: (j, 0)),
            pl.BlockSpec((LANES, HW), const2),
            pl.BlockSpec((1, HW), const2),
            pl.BlockSpec((1, DV), const2),
            pl.BlockSpec((1, DV), const2),
            pl.BlockSpec((1, DV), const2),
            pl.BlockSpec((HEADS, DV, LANES), const3),
            pl.BlockSpec((HEADS, DV, LANES), const3),
        ],
        out_specs=[
            pl.BlockSpec((rows_blk, D_MODEL), lambda b, j: (b * nblk + j, 0)),
            pl.BlockSpec((HEADS, DV, LANES), const3),
            pl.BlockSpec((HEADS, DV, LANES), const3),
        ],
        out_shape=[jax.ShapeDtypeStruct((batch * rows, D_MODEL), BF16), state, state],
        scratch_shapes=[pltpu.VMEM((HEADS, DV, LANES), F32), pltpu.VMEM((HEADS, DV, LANES), F32),
                        pltpu.VMEM((rows_blk, PROJ_W), F32)],
        compiler_params=pltpu.CompilerParams(
            dimension_semantics=("arbitrary", "arbitrary"), vmem_limit_bytes=VMEM_LIMIT),
        name="token_mixers",
    )(x2d, g1, w_all, cosf, sinf, w2p, gbp, ggain, rgain, rbias, s0, r0)


def _out_proj_kernel(o_ref, x_ref, w_ref, g_ref, h_ref, xn_ref, xp_ref):
    h = x_ref[...] + _dot(o_ref[...], w_ref[...])
    h_ref[...] = h
    xn = _rms(h, g_ref[...])
    xn_ref[...] = xn
    xp_ref[...] = _pack_pairs(xn[:, :HALF], xn[:, HALF:])


def _out_proj(o, x2d, w_out, gain2):
    n = x2d.shape[0]
    tm = _row_block(n, DENSE_ROWS)
    blk = pl.BlockSpec((tm, D_MODEL), lambda i: (i, 0))
    return pl.pallas_call(
        _out_proj_kernel,
        grid=(n // tm,),
        in_specs=[blk, blk, pl.BlockSpec((D_MODEL, D_MODEL), lambda i: (0, 0)),
                  pl.BlockSpec((1, D_MODEL), lambda i: (0, 0))],
        out_specs=[blk, blk, pl.BlockSpec((tm, HALF), lambda i: (i, 0))],
        out_shape=[jax.ShapeDtypeStruct((n, D_MODEL), F32)] * 2 + [jax.ShapeDtypeStruct((n, HALF), jnp.uint32)],
        compiler_params=pltpu.CompilerParams(
            dimension_semantics=("parallel",), vmem_limit_bytes=VMEM_LIMIT),
        name="out_proj_norm",
    )(o, x2d, w_out, gain2)


def _top_k_rows(arrays, payloads=None):
    arrays = list(arrays)
    n = arrays[0].shape[0]
    rowi = lax.broadcasted_iota(jnp.int32, arrays[0].shape, 0)
    vals = [[] for _ in arrays]
    picks = [[] for _ in arrays]
    for _ in range(PEER_TOPK):
        for k, s in enumerate(arrays):
            m = jnp.max(s, axis=0, keepdims=True)
            am = jnp.min(jnp.where(s == m, rowi, n), axis=0, keepdims=True)
            sel = rowi == am
            vals[k].append(m)
            if payloads is None:
                picks[k].append(am)
            else:
                picks[k].append(jnp.min(jnp.where(sel, payloads[k], jnp.iinfo(jnp.int32).max),
                                        axis=0, keepdims=True))
            arrays[k] = jnp.where(sel, -jnp.inf, s)
    return [(jnp.concatenate(v, axis=0), jnp.concatenate(p, axis=0)) for v, p in zip(vals, picks)]


def _candidate_cells():
    cells = [(a, b) for a in range(PEER_TOPK) for b in range(PEER_TOPK) if (a + 1) * (b + 1) <= PEER_TOPK]
    n = -(-len(cells) // 8) * 8
    sel_a = np.zeros((n, PEER_TOPK), np.float32)
    sel_b = np.zeros((n, PEER_TOPK), np.float32)
    pad = np.full((n, 1), -np.inf, np.float32)
    for r, (a, b) in enumerate(cells):
        sel_a[r, a] = 1.0
        sel_b[r, b] = 1.0
        pad[r, 0] = 0.0
    return sel_a, sel_b, pad


def _route_kernel(xn_ref, wq_ref, sk_ref, sa_ref, sb_ref, pad_ref, idx_ref, g_ref, q_scr, i_scr, g_scr):
    tr = xn_ref.shape[0]
    q = _dot(xn_ref[...].astype(BF16), wq_ref[...])
    for hc in range(2 * PEER_HEADS):
        q_scr[hc] = q[:, hc * LANES:(hc + 1) * LANES].astype(BF16)

    def sub_body(sb, carry):
        rows = pl.ds(pl.multiple_of(sb * LANES, LANES), LANES)

        def head_pair(hp, c2):
            heads = [2 * hp, 2 * hp + 1]
            scores = [_dot_nt(sk_ref[2 * h + c], q_scr[2 * h + c, rows, :])
                      for h in heads for c in range(2)]
            first = _top_k_rows(scores)
            sa, sb = sa_ref[...], sb_ref[...]
            cands, cidxs = [], []
            for k in range(len(heads)):
                (v0, i0), (v1, i1) = first[2 * k], first[2 * k + 1]
                cands.append((_dot(sa, v0, precision=lax.Precision.HIGHEST)
                              + _dot(sb, v1, precision=lax.Precision.HIGHEST)) + pad_ref[...])
                cidxs.append((_dot(sa, i0.astype(F32)) * PEER_NKEYS
                              + _dot(sb, i1.astype(F32))).astype(jnp.int32))
            for h, (best, e) in zip(heads, _top_k_rows(cands, cidxs)):
                ex = jnp.exp(best - best[0:1])
                i_scr[h] = e
                g_scr[h] = ex / jnp.sum(ex, axis=0, keepdims=True)
            return c2

        lax.fori_loop(0, PEER_HEADS // 2, head_pair, 0)
        idx_ref[rows, :] = i_scr[...].reshape(PEER_SEL, LANES).T
        g_ref[rows, :] = g_scr[...].reshape(PEER_SEL, LANES).T
        return carry

    lax.fori_loop(0, tr // LANES, sub_body, 0)


def _route(xn2, w_q, subkeys):
    n = xn2.shape[0]
    sel_a, sel_b, pad = _candidate_cells()
    tr = _row_block(n, DENSE_ROWS)
    qw = 2 * PEER_HEADS * LANES
    return pl.pallas_call(
        _route_kernel,
        grid=(n // tr,),
        in_specs=[pl.BlockSpec((tr, D_MODEL), lambda i: (i, 0)),
                  pl.BlockSpec((D_MODEL, qw), lambda i: (0, 0)),
                  pl.BlockSpec((2 * PEER_HEADS, PEER_NKEYS, LANES), lambda i: (0, 0, 0)),
                  pl.BlockSpec(sel_a.shape, lambda i: (0, 0)),
                  pl.BlockSpec(sel_b.shape, lambda i: (0, 0)),
                  pl.BlockSpec(pad.shape, lambda i: (0, 0))],
        out_specs=[pl.BlockSpec((tr, PEER_SEL), lambda i: (i, 0))] * 2,
        out_shape=[jax.ShapeDtypeStruct((n, PEER_SEL), jnp.int32),
                   jax.ShapeDtypeStruct((n, PEER_SEL), F32)],
        scratch_shapes=[pltpu.VMEM((2 * PEER_HEADS, tr, LANES), BF16),
                        pltpu.VMEM((PEER_HEADS, PEER_TOPK, LANES), jnp.int32),
                        pltpu.VMEM((PEER_HEADS, PEER_TOPK, LANES), F32)],
        compiler_params=pltpu.CompilerParams(
            dimension_semantics=("parallel",), vmem_limit_bytes=VMEM_LIMIT),
        name="peer_route",
    )(xn2, w_q, subkeys, sel_a, sel_b, pad)


def _sc_worker_id():
    return lax.axis_index("s") * SC_CORES + lax.axis_index("c")


def _sc_mesh():
    return plsc.VectorSubcoreMesh(core_axis_name="c", subcore_axis_name="s")


def _sc_gather_ring(table_hbm, idx_v, bufs, sems, unit, compute):
    def gather(g, k):
        off = pl.multiple_of(g * SC_ROWS, SC_ROWS)
        return pltpu.make_async_copy(table_hbm.at[idx_v.at[pl.ds(off, SC_ROWS)]], bufs[k], sems[k])

    for k in range(SC_NBUF):
        gather(k, k).start()

    units_per_iter = SC_NBUF // unit

    def ring_iter(i, carry):
        for m in range(units_per_iter):
            ks = range(m * unit, (m + 1) * unit)
            for k in ks:
                gather(i * SC_NBUF + k, k).wait()
            compute(i * units_per_iter + m, [bufs[k] for k in ks])
            for k in ks:
                g_next = (i + 1) * SC_NBUF + k

                @pl.when(g_next < SC_GATHERS)
                def _():
                    gather(g_next, k).start()
        return carry

    lax.fori_loop(0, SC_GATHERS // SC_NBUF, ring_iter, 0)


def _bf16_bits(x):
    b = lax.bitcast_convert_type(x, jnp.uint32)
    return (b + jnp.uint32(0x7FFF) + ((b >> 16) & jnp.uint32(1))) >> 16


def _pack_pairs(lo, hi):
    return _bf16_bits(lo) | (_bf16_bits(hi) << 16)


def _pack_table(t):
    return _pack_pairs(t[:, :HALF], t[:, HALF:]).reshape(t.shape[0], ROW_TILES, LANES)


def _mul_add_words(w1, m1, w2, m2):
    bf = lambda a: plsc.bitcast(a, jnp.bfloat16)
    p = plsc.bitcast(bf(w1) * bf(m1) + bf(w2) * bf(m2), jnp.uint32)
    return lax.bitcast_convert_type(p << 16, F32), lax.bitcast_convert_type(p, F32)


def _sc_scratch(staged_cols, out_cols):
    return ([pltpu.VMEM((SC_TOK_BATCH * PEER_SEL,), jnp.int32),
             pltpu.VMEM((SC_TOK_BATCH, staged_cols), jnp.uint32),
             pltpu.VMEM((SC_TOK_BATCH, out_cols), F32)]
            + [pltpu.VMEM((SC_ROWS, ROW_TILES, LANES), jnp.uint32) for _ in range(SC_NBUF)]
            + [pltpu.SemaphoreType.DMA for _ in range(SC_NBUF)])


def _sc_batches(tpw, idx_hbm, staged_hbm, out_hbm, idx_v, staged_v, out_v, run):
    tok0 = _sc_worker_id() * tpw

    def batch(tb, carry):
        t0 = pl.multiple_of(tok0 + tb * SC_TOK_BATCH, SC_TOK_BATCH)
        pltpu.sync_copy(idx_hbm.at[pl.ds(pl.multiple_of(t0 * PEER_SEL, SC_TOK_BATCH * PEER_SEL),
                                         SC_TOK_BATCH * PEER_SEL)], idx_v)
        pltpu.sync_copy(staged_hbm.at[pl.ds(t0, SC_TOK_BATCH)], staged_v)
        run()
        pltpu.sync_copy(out_v, out_hbm.at[pl.ds(t0, SC_TOK_BATCH)])
        return carry

    lax.fori_loop(0, tpw // SC_TOK_BATCH, batch, 0)


def _peer_dot_sc(u3, idx_flat, xn2):
    t = xn2.shape[0]
    unit_rows = 16
    unit = unit_rows // SC_ROWS
    units_per_tok = PEER_SEL // unit_rows

    @functools.partial(
        pl.kernel,
        mesh=_sc_mesh(),
        out_type=jax.ShapeDtypeStruct((t, PEER_SEL * SC_LANES), F32),
        scratch_types=_sc_scratch(HALF, PEER_SEL * SC_LANES),
        compiler_params=pltpu.CompilerParams(needs_layout_passes=False),
        name="peer_dot",
    )
    def body(u_hbm, idx_hbm, x_hbm, out_hbm, idx_v, x_v, out_v, *rest):
        bufs, sems = rest[:SC_NBUF], rest[SC_NBUF:]

        def compute(u, ub):
            tl = u // units_per_tok
            rb = (u % units_per_tok) * unit_rows

            def cbody(c, acc):
                acc = list(acc)
                c4 = c // (LANES // (2 * SC_LANES))
                lane0 = pl.multiple_of((c % (LANES // (2 * SC_LANES))) * 2 * SC_LANES, 2 * SC_LANES)
                col = pl.multiple_of(c * 2 * SC_LANES, 2 * SC_LANES)
                xw1 = x_v[tl, pl.ds(col, SC_LANES)]
                xw2 = x_v[tl, pl.ds(col + SC_LANES, SC_LANES)]
                for q, buf in enumerate(ub):
                    for r in range(SC_ROWS):
                        a = q * SC_ROWS + r
                        lo, hi = _mul_add_words(buf[r, c4, pl.ds(lane0, SC_LANES)], xw1,
                                                buf[r, c4, pl.ds(lane0 + SC_LANES, SC_LANES)], xw2)
                        acc[a] = acc[a] + lo + hi
                return tuple(acc)

            acc = lax.fori_loop(0, HALF // (2 * SC_LANES), cbody,
                                tuple(jnp.zeros((SC_LANES,), F32) for _ in range(unit_rows)))
            for a in range(unit_rows):
                o = pl.multiple_of((rb + a) * SC_LANES, SC_LANES)
                out_v[tl, pl.ds(o, SC_LANES)] = acc[a]

        _sc_batches(t // SC_WORKERS, idx_hbm, x_hbm, out_hbm, idx_v, x_v, out_v,
                    lambda: _sc_gather_ring(u_hbm, idx_v, bufs, sems, unit, compute))

    return body(u3, idx_flat, xn2)


def _peer_combine_sc(v3, idx_flat, coef_b):
    t = coef_b.shape[0]
    unit = 2
    units_per_tok = PEER_SEL // (unit * SC_ROWS)
    nvec = LANES // SC_LANES

    @functools.partial(
        pl.kernel,
        mesh=_sc_mesh(),
        out_type=jax.ShapeDtypeStruct((t, D_MODEL), F32),
        scratch_types=_sc_scratch(PEER_SEL * SC_LANES, D_MODEL),
        compiler_params=pltpu.CompilerParams(needs_layout_passes=False),
        name="peer_combine",
    )
    def body(v_hbm, idx_hbm, coef_hbm, out_hbm, idx_v, coef_v, out_v, *rest):
        bufs, sems = rest[:SC_NBUF], rest[SC_NBUF:]

        def compute(u, ub):
            tl = u // units_per_tok
            rb = (u % units_per_tok) * unit * SC_ROWS
            for g in range(ROW_TILES):
                cols = ([pl.ds(g * LANES + l * SC_LANES, SC_LANES) for l in range(nvec)]
                        + [pl.ds(HALF + g * LANES + l * SC_LANES, SC_LANES) for l in range(nvec)])

                def rbody(r, acc):
                    acc = list(acc)
                    co = pl.multiple_of((rb + r) * SC_LANES, SC_LANES)
                    cf1 = coef_v[tl, pl.ds(co, SC_LANES)]
                    cf2 = coef_v[tl, pl.ds(co + SC_ROWS * SC_LANES, SC_LANES)]
                    for l in range(nvec):
                        lo, hi = _mul_add_words(ub[0][r, g, pl.ds(l * SC_LANES, SC_LANES)], cf1,
                                                ub[1][r, g, pl.ds(l * SC_LANES, SC_LANES)], cf2)
                        acc[l] = acc[l] + lo
                        acc[nvec + l] = acc[nvec + l] + hi
                    return tuple(acc)

                acc = lax.fori_loop(0, SC_ROWS, rbody, tuple(out_v[tl, c] for c in cols))
                for k, c in enumerate(cols):
                    out_v[tl, c] = acc[k]

        def run():
            def zero(tl, c2):
                for k in range(D_MODEL // SC_LANES):
                    out_v[tl, pl.ds(k * SC_LANES, SC_LANES)] = jnp.zeros((SC_LANES,), F32)
                return c2

            lax.fori_loop(0, SC_TOK_BATCH, zero, 0)
            _sc_gather_ring(v_hbm, idx_v, bufs, sems, unit, compute)

        _sc_batches(t // SC_WORKERS, idx_hbm, coef_hbm, out_hbm, idx_v, coef_v, out_v, run)

    return body(v3, idx_flat, coef_b)


def _coef_kernel(part_ref, g_ref, e_ref, coef_ref):
    act = lax.dot_general(part_ref[...], e_ref[...], (((1,), (1,)), ((), ())),
                          precision=lax.Precision.HIGHEST, preferred_element_type=F32)
    gelu = 0.5 * act * (1.0 + lax.erf(act * np.float32(math.sqrt(0.5))))
    c = _dot(g_ref[...] * gelu, e_ref[...], precision=lax.Precision.HIGHEST)
    coef_ref[...] = _pack_pairs(c, c)


def _coef(part, g):
    n = part.shape[0]
    tr = _row_block(n, DENSE_ROWS)
    w = PEER_SEL * SC_LANES
    expand = (jnp.arange(w, dtype=jnp.int32)[None, :] // SC_LANES
              == jnp.arange(PEER_SEL, dtype=jnp.int32)[:, None]).astype(F32)
    wide = pl.BlockSpec((tr, w), lambda i: (i, 0))
    return pl.pallas_call(
        _coef_kernel,
        grid=(n // tr,),
        in_specs=[wide, pl.BlockSpec((tr, PEER_SEL), lambda i: (i, 0)),
                  pl.BlockSpec((PEER_SEL, w), lambda i: (0, 0))],
        out_specs=wide,
        out_shape=jax.ShapeDtypeStruct((n, w), jnp.uint32),
        compiler_params=pltpu.CompilerParams(
            dimension_semantics=("parallel",), vmem_limit_bytes=VMEM_LIMIT),
        name="peer_coef",
    )(part, g, expand)


def _final_kernel(h_ref, p_ref, g_ref, full_ref, o_ref):
    del full_ref
    o_ref[...] = _rms(h_ref[...] + p_ref[...], g_ref[...])


def _final(h2, peer_out, gain, out_full, row0):
    n = h2.shape[0]
    tm = math.gcd(_row_block(n, 2 * DENSE_ROWS), row0) if row0 else _row_block(n, 2 * DENSE_ROWS)
    assert tm % 8 == 0
    blk = pl.BlockSpec((tm, D_MODEL), lambda i: (i, 0))
    return pl.pallas_call(
        _final_kernel,
        grid=(n // tm,),
        in_specs=[blk, blk, pl.BlockSpec((1, D_MODEL), lambda i: (0, 0)),
                  pl.BlockSpec(memory_space=pl.ANY)],
        out_specs=pl.BlockSpec((tm, D_MODEL), lambda i: (row0 // tm + i, 0)),
        out_shape=jax.ShapeDtypeStruct(out_full.shape, F32),
        input_output_aliases={3: 0},
        compiler_params=pltpu.CompilerParams(
            dimension_semantics=("parallel",), vmem_limit_bytes=VMEM_LIMIT),
        name="final_norm",
    )(h2, peer_out, gain, out_full)


def _pad_heads(w, dk):
    lead = w.shape[:-1]
    w = w.reshape(lead + (HEADS, dk))
    w = jnp.pad(w, [(0, 0)] * len(lead) + [(0, 0), (0, LANES - dk)])
    return w.reshape(lead + (HEADS * LANES,))


def _rot_heads(w, dk):
    lead = w.shape[:-1]
    w = w.reshape(lead + (HEADS, 2, dk // 2))
    w = jnp.stack([-w[..., 1, :], w[..., 0, :]], axis=-2)
    return w.reshape(lead + (HEADS * dk,))


def _pack_in_weights(w_in):
    gq, gk, gv, gg, ga, rq, rk, rv, rg = jnp.split(w_in, np.cumsum(IN_SPLITS)[:-1].tolist(), axis=-1)
    ga = jnp.pad(ga, ((0, 0), (0, 2 * LANES - GATE_RANK)))
    return jnp.concatenate(
        [_pad_heads(gq, DK), _pad_heads(gk, DK), gv, gg,
         _pad_heads(rq, DK), _pad_heads(rk, DK), rv, rg,
         _pad_heads(_rot_heads(rq, DK), DK), _pad_heads(_rot_heads(rk, DK), DK), ga],
        axis=-1).astype(BF16)


def _rope_tables(pos):
    inv_freq = ROPE_BASE ** (-jnp.arange(0, DK, 2, dtype=F32) / DK)
    ang = pos[:, None] * inv_freq[None, :]
    full = lambda t: _pad_heads(jnp.tile(t, (1, 2 * HEADS)), DK)
    return full(jnp.cos(ang)), full(jnp.sin(ang))


def _segment_lengths(s, first):
    if first:
        lengths = [s * k // sum(FIRST_SEQ_RAMP) for k in FIRST_SEQ_RAMP]
        assert sum(lengths) == s
    else:
        lengths = [s // SEGS_PER_SEQ] * SEGS_PER_SEQ
    assert all(n % SC_CALL_TOKENS == 0 for n in lengths)
    return lengths


def kernel(x, meta_tokens, norm1_gain, w_in, gla_gate_w2, gla_gate_b, gla_norm_gain, ret_norm_gain, ret_norm_bias, w_out, norm2_gain, peer_w_q, peer_subkeys, peer_u, peer_v, final_gain):
    b, s, d = x.shape
    assert d == D_MODEL and (s // SEGS_PER_SEQ) % SC_CALL_TOKENS == 0
    assert norm1_gain.shape[0] == 1, "single-layer block"
    x2d = x.reshape(b * s, d)

    w_all = _pack_in_weights(w_in[0])
    w2p = jnp.pad(_pad_heads(gla_gate_w2[0], DK), ((0, LANES - GATE_RANK), (0, 0))).astype(BF16)
    gbp = _pad_heads(gla_gate_b[0][None, :], DK)
    g1 = norm1_gain[0][None, :]
    ggain = gla_norm_gain[0][None, :]
    rgain = ret_norm_gain[0][None, :]
    rbias = ret_norm_bias[0][None, :]

    h0 = jnp.concatenate([jnp.zeros((N_PAD, d), x.dtype), meta_tokens.astype(x.dtype)], axis=0)
    cos0, sin0 = _rope_tables(jnp.arange(CHUNK, dtype=F32) - N_PAD)
    cos1, sin1 = _rope_tables(jnp.arange(s, dtype=F32) + N_META)
    zero_state = jnp.zeros((HEADS, DV, LANES), F32)
    _, s_meta, r_meta = _mixer(h0, g1, w_all, cos0, sin0, w2p, gbp, ggain, rgain, rbias,
                               zero_state, zero_state, batch=1, n_invalid=N_PAD)

    w_out_b = w_out[0].astype(BF16)
    g2 = norm2_gain[0][None, :]
    w_q = peer_w_q[0].astype(BF16)
    subkeys = peer_subkeys[0].reshape(2 * PEER_HEADS, PEER_NKEYS, PEER_DKEY // 2).astype(BF16)
    u3 = _pack_table(peer_u[0])
    v3 = _pack_table(peer_v[0])
    gf = final_gain[None, :]

    out = jnp.zeros((b * s, d), F32)
    prev_g = None
    pending = []
    waiting = None
    for bi in range(b):
        s_state, r_state = s_meta, r_meta
        start = 0
        for seg in _segment_lengths(s, first=bi == 0):
            row0 = bi * s + start
            xs = x2d[row0:row0 + seg]
            pos = slice(start, start + seg)
            start += seg
            if prev_g is not None:
                xs, _ = lax.optimization_barrier((xs, prev_g))
            o, s_state, r_state = _mixer(xs, g1, w_all, cos1[pos], sin1[pos], w2p, gbp,
                                         ggain, rgain, rbias, s_state, r_state, batch=1, n_invalid=0)
            h2, xn2, xn2_pairs = _out_proj(o, xs, w_out_b, g2)
            idx, g = _route(xn2, w_q, subkeys)
            prev_g = g
            idx_flat = idx.reshape(-1)
            if pending:
                h2_old, row_old, peer_old = pending.pop(0)
                idx_flat, peer_old = lax.optimization_barrier((idx_flat, peer_old))
                out = _final(h2_old, peer_old, gf, out, row_old)
            part = _peer_dot_sc(u3, idx_flat, xn2_pairs)
            if waiting is not None:
                h2_w, row_w, idx_w, part_w, g_w = waiting
                part_w, _ = lax.optimization_barrier((part_w, g))
                pending.append((h2_w, row_w, _peer_combine_sc(v3, idx_w, _coef(part_w, g_w))))
            waiting = (h2, row0, idx_flat, part, g)
    h2_w, row_w, idx_w, part_w, g_w = waiting
    pending.append((h2_w, row_w, _peer_combine_sc(v3, idx_w, _coef(part_w, g_w))))
    for h2_old, row_old, peer_old in pending:
        out = _final(h2_old, peer_old, gf, out, row_old)
    return out.reshape(b, s, d)
```

```python
import functools
import math

import jax
import jax.numpy as jnp
import numpy as np
from jax import lax
from jax.experimental import pallas as pl
from jax.experimental.pallas import tpu as pltpu
from jax.experimental.pallas import tpu_sc as plsc

F32 = jnp.float32
BF16 = jnp.bfloat16

D_MODEL = 1024
N_META = 16
CHUNK = 64
N_PAD = CHUNK - N_META
EPS = 1e-6

HEADS = 4
DV = 128
DK = 64
LANES = 128
GATE_RANK = 16
GATE_TAU = 16.0
ROPE_BASE = 10000.0
IN_SPLITS = (HEADS * DK, HEADS * DK, HEADS * DV, HEADS * DV, GATE_RANK,
             HEADS * DK, HEADS * DK, HEADS * DV, HEADS * DV)

PEER_HEADS = 8
PEER_NKEYS = 128
PEER_DKEY = 256
PEER_TOPK = 16
PEER_SEL = PEER_HEADS * PEER_TOPK

HW = HEADS * LANES
OFF_GQ, OFF_GK, OFF_GV, OFF_GG = 0, HW, 2 * HW, 3 * HW
OFF_RQ, OFF_RK, OFF_RV, OFF_RG = 4 * HW, 5 * HW, 6 * HW, 7 * HW
OFF_RQR, OFF_RKR, OFF_GA = 8 * HW, 9 * HW, 10 * HW
PROJ_W = 10 * HW + 2 * LANES
PROJ_TN = PROJ_W // 3

LOG_GAMMA = tuple(math.log(1.0 - 2.0 ** (-5.0 - h)) for h in range(HEADS))

SC_CORES, SC_SUBCORES, SC_LANES = 2, 16, 16
SC_WORKERS = SC_CORES * SC_SUBCORES
SC_TOK_BATCH = 16
SC_ROWS = 16
SC_NBUF = 8
SC_GATHERS = SC_TOK_BATCH * PEER_SEL // SC_ROWS
HALF = D_MODEL // 2
ROW_TILES = HALF // LANES

SC_CALL_TOKENS = SC_WORKERS * SC_TOK_BATCH
SEGS_PER_SEQ = 2
FIRST_SEQ_RAMP = (1, 1, 2, 2, 3, 3, 4)

MIXER_ROWS = 256
DENSE_ROWS = 512
VMEM_LIMIT = 48 * 1024 * 1024


def _row_block(n, target):
    blk = math.gcd(n, target)
    assert blk % 8 == 0 or blk == n, (n, target)
    return blk


def _dot(a, b, **kw):
    return jnp.dot(a, b, preferred_element_type=F32, **kw)


def _dot_nt(a, b):
    return lax.dot_general(a, b, (((1,), (1,)), ((), ())), preferred_element_type=F32)


def _dot_tn(a, b):
    return lax.dot_general(a, b, (((0,), (0,)), ((), ())), preferred_element_type=F32)


def _rms(x, g):
    return x * lax.rsqrt(jnp.mean(x * x, axis=-1, keepdims=True) + EPS) * g


def _silu(x):
    return x * jax.nn.sigmoid(x)


def _mixer_kernel(x_ref, g1_ref, wall_ref, cos_ref, sin_ref, w2_ref, gb_ref, ggain_ref, rgain_ref,
                  rbias_ref, s0_ref, r0_ref, o_ref, s_out_ref, r_out_ref, s_scr, r_scr, proj_ref,
                  *, n_invalid, rows_blk):
    j = pl.program_id(1)
    xn = _rms(x_ref[...], g1_ref[...]).astype(BF16)
    for jn in range(PROJ_W // PROJ_TN):
        cols = slice(jn * PROJ_TN, (jn + 1) * PROJ_TN)
        proj_ref[:, cols] = _dot(xn, wall_ref[:, cols])

    @pl.when(j == 0)
    def _():
        s_scr[...] = s0_ref[...]
        r_scr[...] = r0_ref[...]

    row = lax.broadcasted_iota(jnp.int32, (CHUNK, CHUNK), 0)
    col = lax.broadcasted_iota(jnp.int32, (CHUNK, CHUNK), 1)
    causal = row >= col
    tril = causal.astype(F32)
    diff = jnp.maximum(row - col, 0).astype(F32)
    rowf = lax.broadcasted_iota(jnp.int32, (CHUNK, LANES), 0).astype(F32)
    row1 = lax.broadcasted_iota(jnp.int32, (CHUNK, 1), 0)

    def chunk_body(c, carry):
        r0 = pl.multiple_of(c * CHUNK, CHUNK)
        rows = pl.ds(r0, CHUNK)
        valid = ((j * rows_blk + r0 + row1) >= n_invalid).astype(F32)

        ga = proj_ref[rows, OFF_GA:OFF_GA + LANES].astype(BF16)
        pre = _dot(ga, w2_ref[...]) + gb_ref[...]
        log_a = jax.nn.log_sigmoid(pre) * (1.0 / GATE_TAU)
        bcum = _dot(tril, log_a, precision=lax.Precision.HIGHEST)
        b_last = bcum[CHUNK - 1:CHUNK, :]
        gq = proj_ref[rows, OFF_GQ:OFF_GQ + HW] * (DK ** -0.5)
        gk = proj_ref[rows, OFF_GK:OFF_GK + HW] * valid
        q_t = (gq * jnp.exp(bcum)).astype(BF16)
        k_t = (gk * jnp.exp(-bcum)).astype(BF16)
        k_end = (gk * jnp.exp(b_last - bcum)).astype(BF16)
        decay = jnp.exp(b_last)
        for h in range(HEADS):
            sl = slice(h * LANES, (h + 1) * LANES)
            v = proj_ref[rows, OFF_GV + h * DV:OFF_GV + (h + 1) * DV].astype(BF16)
            att = jnp.where(causal, _dot_nt(q_t[:, sl], k_t[:, sl]), 0.0).astype(BF16)
            s_prev = s_scr[h]
            o = _dot(att, v) + _dot_nt(q_t[:, sl], s_prev.astype(BF16))
            s_scr[h] = s_prev * decay[:, sl] + _dot_tn(v, k_end[:, sl])
            o = o * lax.rsqrt(jnp.mean(o * o, axis=-1, keepdims=True) + EPS) * ggain_ref[...]
            gate = proj_ref[rows, OFF_GG + h * DV:OFF_GG + (h + 1) * DV]
            o_ref[rows, h * DV:(h + 1) * DV] = (o * _silu(gate)).astype(o_ref.dtype)

        cosb = cos_ref[rows, :]
        sinb = sin_ref[rows, :]
        rq = proj_ref[rows, OFF_RQ:OFF_RQ + HW]
        rqr = proj_ref[rows, OFF_RQR:OFF_RQR + HW]
        rk = proj_ref[rows, OFF_RK:OFF_RK + HW] * valid
        rkr = proj_ref[rows, OFF_RKR:OFF_RKR + HW] * valid
        q_r = (rq * cosb + rqr * sinb).astype(BF16)
        k_r = (rk * cosb + rkr * sinb) * (DK ** -0.5)
        for h in range(HEADS):
            sl = slice(h * LANES, (h + 1) * LANES)
            lg = LOG_GAMMA[h]
            dmat = jnp.where(causal, jnp.exp(lg * diff), 0.0)
            xi = jnp.exp(lg * (rowf + 1.0))
            zeta = jnp.exp(lg * (CHUNK - 1.0 - rowf))
            v = proj_ref[rows, OFF_RV + h * DV:OFF_RV + (h + 1) * DV].astype(BF16)
            k_h = k_r[:, sl]
            att = (_dot_nt(q_r[:, sl], k_h.astype(BF16)) * dmat).astype(BF16)
            r_prev = r_scr[h]
            o = _dot(att, v) + _dot_nt(q_r[:, sl], r_prev.astype(BF16)) * xi
            r_scr[h] = r_prev * math.exp(lg * CHUNK) + _dot_tn(v, (k_h * zeta).astype(BF16))
            mu = jnp.mean(o, axis=-1, keepdims=True)
            var = jnp.mean(jnp.square(o - mu), axis=-1, keepdims=True)
            o = (o - mu) * lax.rsqrt(var + EPS) * rgain_ref[...] + rbias_ref[...]
            gate = proj_ref[rows, OFF_RG + h * DV:OFF_RG + (h + 1) * DV]
            o_ref[rows, HW + h * DV:HW + (h + 1) * DV] = (o * _silu(gate)).astype(o_ref.dtype)
        return carry

    lax.fori_loop(0, rows_blk // CHUNK, chunk_body, 0)
    s_out_ref[...] = s_scr[...]
    r_out_ref[...] = r_scr[...]


def _mixer(x2d, g1, w_all, cosf, sinf, w2p, gbp, ggain, rgain, rbias, s0, r0, *, batch, n_invalid):
    rows = x2d.shape[0] // batch
    rows_blk = _row_block(rows, MIXER_ROWS)
    nblk = rows // rows_blk
    const2 = lambda b, j: (0, 0)
    const3 = lambda b, j: (0, 0, 0)
    state = jax.ShapeDtypeStruct((HEADS, DV, LANES), F32)
    return pl.pallas_call(
        functools.partial(_mixer_kernel, n_invalid=n_invalid, rows_blk=rows_blk),
        grid=(batch, nblk),
        in_specs=[
            pl.BlockSpec((rows_blk, D_MODEL), lambda b, j: (b * nblk + j, 0)),
            pl.BlockSpec((1, D_MODEL), const2),
            pl.BlockSpec((D_MODEL, PROJ_W), const2),
            pl.BlockSpec((rows_blk, HW), lambda b, j: (j, 0)),
            pl.BlockSpec((rows_blk, HW), lambda b, j: (j, 0)),
            pl.BlockSpec((LANES, HW), const2),
            pl.BlockSpec((1, HW), const2),
            pl.BlockSpec((1, DV), const2),
            pl.BlockSpec((1, DV), const2),
            pl.BlockSpec((1, DV), const2),
            pl.BlockSpec((HEADS, DV, LANES), const3),
            pl.BlockSpec((HEADS, DV, LANES), const3),
        ],
        out_specs=[
            pl.BlockSpec((rows_blk, D_MODEL), lambda b, j: (b * nblk + j, 0)),
            pl.BlockSpec((HEADS, DV, LANES), const3),
            pl.BlockSpec((HEADS, DV, LANES), const3),
        ],
        out_shape=[jax.ShapeDtypeStruct((batch * rows, D_MODEL), BF16), state, state],
        scratch_shapes=[pltpu.VMEM((HEADS, DV, LANES), F32), pltpu.VMEM((HEADS, DV, LANES), F32),
                        pltpu.VMEM((rows_blk, PROJ_W), F32)],
        compiler_params=pltpu.CompilerParams(
            dimension_semantics=("arbitrary", "arbitrary"), vmem_limit_bytes=VMEM_LIMIT),
        name="token_mixers",
    )(x2d, g1, w_all, cosf, sinf, w2p, gbp, ggain, rgain, rbias, s0, r0)


def _out_proj_kernel(o_ref, x_ref, w_ref, g_ref, h_ref, xn_ref, xp_ref):
    h = x_ref[...] + _dot(o_ref[...], w_ref[...])
    h_ref[...] = h
    xn = _rms(h, g_ref[...])
    xn_ref[...] = xn
    xp_ref[...] = _pack_pairs(xn[:, :HALF], xn[:, HALF:])


def _out_proj(o, x2d, w_out, gain2):
    n = x2d.shape[0]
    tm = _row_block(n, DENSE_ROWS)
    blk = pl.BlockSpec((tm, D_MODEL), lambda i: (i, 0))
    return pl.pallas_call(
        _out_proj_kernel,
        grid=(n // tm,),
        in_specs=[blk, blk, pl.BlockSpec((D_MODEL, D_MODEL), lambda i: (0, 0)),
                  pl.BlockSpec((1, D_MODEL), lambda i: (0, 0))],
        out_specs=[blk, blk, pl.BlockSpec((tm, HALF), lambda i: (i, 0))],
        out_shape=[jax.ShapeDtypeStruct((n, D_MODEL), F32)] * 2 + [jax.ShapeDtypeStruct((n, HALF), jnp.uint32)],
        compiler_params=pltpu.CompilerParams(
            dimension_semantics=("parallel",), vmem_limit_bytes=VMEM_LIMIT),
        name="out_proj_norm",
    )(o, x2d, w_out, gain2)


def _top_k_rows(arrays, payloads=None):
    arrays = list(arrays)
    n = arrays[0].shape[0]
    rowi = lax.broadcasted_iota(jnp.int32, arrays[0].shape, 0)
    vals = [[] for _ in arrays]
    picks = [[] for _ in arrays]
    for _ in range(PEER_TOPK):
        for k, s in enumerate(arrays):
            m = jnp.max(s, axis=0, keepdims=True)
            am = jnp.min(jnp.where(s == m, rowi, n), axis=0, keepdims=True)
            sel = rowi == am
            vals[k].append(m)
            if payloads is None:
                picks[k].append(am)
            else:
                picks[k].append(jnp.min(jnp.where(sel, payloads[k], jnp.iinfo(jnp.int32).max),
                                        axis=0, keepdims=True))
            arrays[k] = jnp.where(sel, -jnp.inf, s)
    return [(jnp.concatenate(v, axis=0), jnp.concatenate(p, axis=0)) for v, p in zip(vals, picks)]


def _candidate_cells():
    cells = [(a, b) for a in range(PEER_TOPK) for b in range(PEER_TOPK) if (a + 1) * (b + 1) <= PEER_TOPK]
    n = -(-len(cells) // 8) * 8
    sel_a = np.zeros((n, PEER_TOPK), np.float32)
    sel_b = np.zeros((n, PEER_TOPK), np.float32)
    pad = np.full((n, 1), -np.inf, np.float32)
    for r, (a, b) in enumerate(cells):
        sel_a[r, a] = 1.0
        sel_b[r, b] = 1.0
        pad[r, 0] = 0.0
    return sel_a, sel_b, pad


def _route_kernel(xn_ref, wq_ref, sk_ref, sa_ref, sb_ref, pad_ref, idx_ref, g_ref, q_scr, i_scr, g_scr):
    tr = xn_ref.shape[0]
    q = _dot(xn_ref[...].astype(BF16), wq_ref[...])
    for hc in range(2 * PEER_HEADS):
        q_scr[hc] = q[:, hc * LANES:(hc + 1) * LANES].astype(BF16)

    def sub_body(sb, carry):
        rows = pl.ds(pl.multiple_of(sb * LANES, LANES), LANES)

        def head_pair(hp, c2):
            heads = [2 * hp, 2 * hp + 1]
            scores = [_dot_nt(sk_ref[2 * h + c], q_scr[2 * h + c, rows, :])
                      for h in heads for c in range(2)]
            first = _top_k_rows(scores)
            sa, sb = sa_ref[...], sb_ref[...]
            cands, cidxs = [], []
            for k in range(len(heads)):
                (v0, i0), (v1, i1) = first[2 * k], first[2 * k + 1]
                cands.append((_dot(sa, v0, precision=lax.Precision.HIGHEST)
                              + _dot(sb, v1, precision=lax.Precision.HIGHEST)) + pad_ref[...])
                cidxs.append((_dot(sa, i0.astype(F32)) * PEER_NKEYS
                              + _dot(sb, i1.astype(F32))).astype(jnp.int32))
            for h, (best, e) in zip(heads, _top_k_rows(cands, cidxs)):
                ex = jnp.exp(best - best[0:1])
                i_scr[h] = e
                g_scr[h] = ex / jnp.sum(ex, axis=0, keepdims=True)
            return c2

        lax.fori_loop(0, PEER_HEADS // 2, head_pair, 0)
        idx_ref[rows, :] = i_scr[...].reshape(PEER_SEL, LANES).T
        g_ref[rows, :] = g_scr[...].reshape(PEER_SEL, LANES).T
        return carry

    lax.fori_loop(0, tr // LANES, sub_body, 0)


def _route(xn2, w_q, subkeys):
    n = xn2.shape[0]
    sel_a, sel_b, pad = _candidate_cells()
    tr = _row_block(n, DENSE_ROWS)
    qw = 2 * PEER_HEADS * LANES
    return pl.pallas_call(
        _route_kernel,
        grid=(n // tr,),
        in_specs=[pl.BlockSpec((tr, D_MODEL), lambda i: (i, 0)),
                  pl.BlockSpec((D_MODEL, qw), lambda i: (0, 0)),
                  pl.BlockSpec((2 * PEER_HEADS, PEER_NKEYS, LANES), lambda i: (0, 0, 0)),
                  pl.BlockSpec(sel_a.shape, lambda i: (0, 0)),
                  pl.BlockSpec(sel_b.shape, lambda i: (0, 0)),
                  pl.BlockSpec(pad.shape, lambda i: (0, 0))],
        out_specs=[pl.BlockSpec((tr, PEER_SEL), lambda i: (i, 0))] * 2,
        out_shape=[jax.ShapeDtypeStruct((n, PEER_SEL), jnp.int32),
                   jax.ShapeDtypeStruct((n, PEER_SEL), F32)],
        scratch_shapes=[pltpu.VMEM((2 * PEER_HEADS, tr, LANES), BF16),
                        pltpu.VMEM((PEER_HEADS, PEER_TOPK, LANES), jnp.int32),
                        pltpu.VMEM((PEER_HEADS, PEER_TOPK, LANES), F32)],
        compiler_params=pltpu.CompilerParams(
            dimension_semantics=("parallel",), vmem_limit_bytes=VMEM_LIMIT),
        name="peer_route",
    )(xn2, w_q, subkeys, sel_a, sel_b, pad)


def _sc_worker_id():
    return lax.axis_index("s") * SC_CORES + lax.axis_index("c")


def _sc_mesh():
    return plsc.VectorSubcoreMesh(core_axis_name="c", subcore_axis_name="s")


def _sc_gather_ring(table_hbm, idx_v, bufs, sems, unit, compute):
    def gather(g, k):
        off = pl.multiple_of(g * SC_ROWS, SC_ROWS)
        return pltpu.make_async_copy(table_hbm.at[idx_v.at[pl.ds(off, SC_ROWS)]], bufs[k], sems[k])

    for k in range(SC_NBUF):
        gather(k, k).start()

    units_per_iter = SC_NBUF // unit

    def ring_iter(i, carry):
        for m in range(units_per_iter):
            ks = range(m * unit, (m + 1) * unit)
            for k in ks:
                gather(i * SC_NBUF + k, k).wait()
            compute(i * units_per_iter + m, [bufs[k] for k in ks])
            for k in ks:
                g_next = (i + 1) * SC_NBUF + k

                @pl.when(g_next < SC_GATHERS)
                def _():
                    gather(g_next, k).start()
        return carry

    lax.fori_loop(0, SC_GATHERS // SC_NBUF, ring_iter, 0)


def _bf16_bits(x):
    b = lax.bitcast_convert_type(x, jnp.uint32)
    return (b + jnp.uint32(0x7FFF) + ((b >> 16) & jnp.uint32(1))) >> 16


def _pack_pairs(lo, hi):
    return _bf16_bits(lo) | (_bf16_bits(hi) << 16)


def _pack_table(t):
    return _pack_pairs(t[:, :HALF], t[:, HALF:]).reshape(t.shape[0], ROW_TILES, LANES)


def _mul_add_words(w1, m1, w2, m2):
    bf = lambda a: plsc.bitcast(a, jnp.bfloat16)
    p = plsc.bitcast(bf(w1) * bf(m1) + bf(w2) * bf(m2), jnp.uint32)
    return lax.bitcast_convert_type(p << 16, F32), lax.bitcast_convert_type(p, F32)


def _sc_scratch(staged_cols, out_cols):
    return ([pltpu.VMEM((SC_TOK_BATCH * PEER_SEL,), jnp.int32),
             pltpu.VMEM((SC_TOK_BATCH, staged_cols), jnp.uint32),
             pltpu.VMEM((SC_TOK_BATCH, out_cols), F32)]
            + [pltpu.VMEM((SC_ROWS, ROW_TILES, LANES), jnp.uint32) for _ in range(SC_NBUF)]
            + [pltpu.SemaphoreType.DMA for _ in range(SC_NBUF)])


def _sc_batches(tpw, idx_hbm, staged_hbm, out_hbm, idx_v, staged_v, out_v, run):
    tok0 = _sc_worker_id() * tpw

    def batch(tb, carry):
        t0 = pl.multiple_of(tok0 + tb * SC_TOK_BATCH, SC_TOK_BATCH)
        pltpu.sync_copy(idx_hbm.at[pl.ds(pl.multiple_of(t0 * PEER_SEL, SC_TOK_BATCH * PEER_SEL),
                                         SC_TOK_BATCH * PEER_SEL)], idx_v)
        pltpu.sync_copy(staged_hbm.at[pl.ds(t0, SC_TOK_BATCH)], staged_v)
        run()
        pltpu.sync_copy(out_v, out_hbm.at[pl.ds(t0, SC_TOK_BATCH)])
        return carry

    lax.fori_loop(0, tpw // SC_TOK_BATCH, batch, 0)


def _peer_dot_sc(u3, idx_flat, xn2):
    t = xn2.shape[0]
    unit_rows = 16
    unit = unit_rows // SC_ROWS
    units_per_tok = PEER_SEL // unit_rows

    @functools.partial(
        pl.kernel,
        mesh=_sc_mesh(),
        out_type=jax.ShapeDtypeStruct((t, PEER_SEL * SC_LANES), F32),
        scratch_types=_sc_scratch(HALF, PEER_SEL * SC_LANES),
        compiler_params=pltpu.CompilerParams(needs_layout_passes=False),
        name="peer_dot",
    )
    def body(u_hbm, idx_hbm, x_hbm, out_hbm, idx_v, x_v, out_v, *rest):
        bufs, sems = rest[:SC_NBUF], rest[SC_NBUF:]

        def compute(u, ub):
            tl = u // units_per_tok
            rb = (u % units_per_tok) * unit_rows

            def cbody(c, acc):
                acc = list(acc)
                c4 = c // (LANES // (2 * SC_LANES))
                lane0 = pl.multiple_of((c % (LANES // (2 * SC_LANES))) * 2 * SC_LANES, 2 * SC_LANES)
                col = pl.multiple_of(c * 2 * SC_LANES, 2 * SC_LANES)
                xw1 = x_v[tl, pl.ds(col, SC_LANES)]
                xw2 = x_v[tl, pl.ds(col + SC_LANES, SC_LANES)]
                for q, buf in enumerate(ub):
                    for r in range(SC_ROWS):
                        a = q * SC_ROWS + r
                        lo, hi = _mul_add_words(buf[r, c4, pl.ds(lane0, SC_LANES)], xw1,
                                                buf[r, c4, pl.ds(lane0 + SC_LANES, SC_LANES)], xw2)
                        acc[a] = acc[a] + lo + hi
                return tuple(acc)

            acc = lax.fori_loop(0, HALF // (2 * SC_LANES), cbody,
                                tuple(jnp.zeros((SC_LANES,), F32) for _ in range(unit_rows)))
            for a in range(unit_rows):
                o = pl.multiple_of((rb + a) * SC_LANES, SC_LANES)
                out_v[tl, pl.ds(o, SC_LANES)] = acc[a]

        _sc_batches(t // SC_WORKERS, idx_hbm, x_hbm, out_hbm, idx_v, x_v, out_v,
                    lambda: _sc_gather_ring(u_hbm, idx_v, bufs, sems, unit, compute))

    return body(u3, idx_flat, xn2)


def _peer_combine_sc(v3, idx_flat, coef_b):
    t = coef_b.shape[0]
    unit = 2
    units_per_tok = PEER_SEL // (unit * SC_ROWS)
    nvec = LANES // SC_LANES

    @functools.partial(
        pl.kernel,
        mesh=_sc_mesh(),
        out_type=jax.ShapeDtypeStruct((t, D_MODEL), F32),
        scratch_types=_sc_scratch(PEER_SEL * SC_LANES, D_MODEL),
        compiler_params=pltpu.CompilerParams(needs_layout_passes=False),
        name="peer_combine",
    )
    def body(v_hbm, idx_hbm, coef_hbm, out_hbm, idx_v, coef_v, out_v, *rest):
        bufs, sems = rest[:SC_NBUF], rest[SC_NBUF:]

        def compute(u, ub):
            tl = u // units_per_tok
            rb = (u % units_per_tok) * unit * SC_ROWS
            for g in range(ROW_TILES):
                cols = ([pl.ds(g * LANES + l * SC_LANES, SC_LANES) for l in range(nvec)]
                        + [pl.ds(HALF + g * LANES + l * SC_LANES, SC_LANES) for l in range(nvec)])

                def rbody(r, acc):
                    acc = list(acc)
                    co = pl.multiple_of((rb + r) * SC_LANES, SC_LANES)
                    cf1 = coef_v[tl, pl.ds(co, SC_LANES)]
                    cf2 = coef_v[tl, pl.ds(co + SC_ROWS * SC_LANES, SC_LANES)]
                    for l in range(nvec):
                        lo, hi = _mul_add_words(ub[0][r, g, pl.ds(l * SC_LANES, SC_LANES)], cf1,
                                                ub[1][r, g, pl.ds(l * SC_LANES, SC_LANES)], cf2)
                        acc[l] = acc[l] + lo
                        acc[nvec + l] = acc[nvec + l] + hi
                    return tuple(acc)

                acc = lax.fori_loop(0, SC_ROWS, rbody, tuple(out_v[tl, c] for c in cols))
                for k, c in enumerate(cols):
                    out_v[tl, c] = acc[k]

        def run():
            def zero(tl, c2):
                for k in range(D_MODEL // SC_LANES):
                    out_v[tl, pl.ds(k * SC_LANES, SC_LANES)] = jnp.zeros((SC_LANES,), F32)
                return c2

            lax.fori_loop(0, SC_TOK_BATCH, zero, 0)
            _sc_gather_ring(v_hbm, idx_v, bufs, sems, unit, compute)

        _sc_batches(t // SC_WORKERS, idx_hbm, coef_hbm, out_hbm, idx_v, coef_v, out_v, run)

    return body(v3, idx_flat, coef_b)


def _coef_kernel(part_ref, g_ref, e_ref, coef_ref):
    act = lax.dot_general(part_ref[...], e_ref[...], (((1,), (1,)), ((), ())),
                          precision=lax.Precision.HIGHEST, preferred_element_type=F32)
    gelu = 0.5 * act * (1.0 + lax.erf(act * np.float32(math.sqrt(0.5))))
    c = _dot(g_ref[...] * gelu, e_ref[...], precision=lax.Precision.HIGHEST)
    coef_ref[...] = _pack_pairs(c, c)


def _coef(part, g):
    n = part.shape[0]
    tr = _row_block(n, DENSE_ROWS)
    w = PEER_SEL * SC_LANES
    expand = (jnp.arange(w, dtype=jnp.int32)[None, :] // SC_LANES
              == jnp.arange(PEER_SEL, dtype=jnp.int32)[:, None]).astype(F32)
    wide = pl.BlockSpec((tr, w), lambda i: (i, 0))
    return pl.pallas_call(
        _coef_kernel,
        grid=(n // tr,),
        in_specs=[wide, pl.BlockSpec((tr, PEER_SEL), lambda i: (i, 0)),
                  pl.BlockSpec((PEER_SEL, w), lambda i: (0, 0))],
        out_specs=wide,
        out_shape=jax.ShapeDtypeStruct((n, w), jnp.uint32),
        compiler_params=pltpu.CompilerParams(
            dimension_semantics=("parallel",), vmem_limit_bytes=VMEM_LIMIT),
        name="peer_coef",
    )(part, g, expand)


def _final_kernel(h_ref, p_ref, g_ref, full_ref, o_ref):
    del full_ref
    o_ref[...] = _rms(h_ref[...] + p_ref[...], g_ref[...])


def _final(h2, peer_out, gain, out_full, row0):
    n = h2.shape[0]
    tm = math.gcd(_row_block(n, 2 * DENSE_ROWS), row0) if row0 else _row_block(n, 2 * DENSE_ROWS)
    assert tm % 8 == 0
    blk = pl.BlockSpec((tm, D_MODEL), lambda i: (i, 0))
    return pl.pallas_call(
        _final_kernel,
        grid=(n // tm,),
        in_specs=[blk, blk, pl.BlockSpec((1, D_MODEL), lambda i: (0, 0)),
                  pl.BlockSpec(memory_space=pl.ANY)],
        out_specs=pl.BlockSpec((tm, D_MODEL), lambda i: (row0 // tm + i, 0)),
        out_shape=jax.ShapeDtypeStruct(out_full.shape, F32),
        input_output_aliases={3: 0},
        compiler_params=pltpu.CompilerParams(
            dimension_semantics=("parallel",), vmem_limit_bytes=VMEM_LIMIT),
        name="final_norm",
    )(h2, peer_out, gain, out_full)


def _pad_heads(w, dk):
    lead = w.shape[:-1]
    w = w.reshape(lead + (HEADS, dk))
    w = jnp.pad(w, [(0, 0)] * len(lead) + [(0, 0), (0, LANES - dk)])
    return w.reshape(lead + (HEADS * LANES,))


def _rot_heads(w, dk):
    lead = w.shape[:-1]
    w = w.reshape(lead + (HEADS, 2, dk // 2))
    w = jnp.stack([-w[..., 1, :], w[..., 0, :]], axis=-2)
    return w.reshape(lead + (HEADS * dk,))


def _pack_in_weights(w_in):
    gq, gk, gv, gg, ga, rq, rk, rv, rg = jnp.split(w_in, np.cumsum(IN_SPLITS)[:-1].tolist(), axis=-1)
    ga = jnp.pad(ga, ((0, 0), (0, 2 * LANES - GATE_RANK)))
    return jnp.concatenate(
        [_pad_heads(gq, DK), _pad_heads(gk, DK), gv, gg,
         _pad_heads(rq, DK), _pad_heads(rk, DK), rv, rg,
         _pad_heads(_rot_heads(rq, DK), DK), _pad_heads(_rot_heads(rk, DK), DK), ga],
        axis=-1).astype(BF16)


def _rope_tables(pos):
    inv_freq = ROPE_BASE ** (-jnp.arange(0, DK, 2, dtype=F32) / DK)
    ang = pos[:, None] * inv_freq[None, :]
    full = lambda t: _pad_heads(jnp.tile(t, (1, 2 * HEADS)), DK)
    return full(jnp.cos(ang)), full(jnp.sin(ang))


def _segment_lengths(s, first):
    if first:
        lengths = [s * k // sum(FIRST_SEQ_RAMP) for k in FIRST_SEQ_RAMP]
        assert sum(lengths) == s
    else:
        lengths = [s // SEGS_PER_SEQ] * SEGS_PER_SEQ
    assert all(n % SC_CALL_TOKENS == 0 for n in lengths)
    return lengths


def kernel(x, meta_tokens, norm1_gain, w_in, gla_gate_w2, gla_gate_b, gla_norm_gain, ret_norm_gain, ret_norm_bias, w_out, norm2_gain, peer_w_q, peer_subkeys, peer_u, peer_v, final_gain):
    b, s, d = x.shape
    assert d == D_MODEL and (s // SEGS_PER_SEQ) % SC_CALL_TOKENS == 0
    assert norm1_gain.shape[0] == 1, "single-layer block"
    x2d = x.reshape(b * s, d)

    w_all = _pack_in_weights(w_in[0])
    w2p = jnp.pad(_pad_heads(gla_gate_w2[0], DK), ((0, LANES - GATE_RANK), (0, 0))).astype(BF16)
    gbp = _pad_heads(gla_gate_b[0][None, :], DK)
    g1 = norm1_gain[0][None, :]
    ggain = gla_norm_gain[0][None, :]
    rgain = ret_norm_gain[0][None, :]
    rbias = ret_norm_bias[0][None, :]

    h0 = jnp.concatenate([jnp.zeros((N_PAD, d), x.dtype), meta_tokens.astype(x.dtype)], axis=0)
    cos0, sin0 = _rope_tables(jnp.arange(CHUNK, dtype=F32) - N_PAD)
    cos1, sin1 = _rope_tables(jnp.arange(s, dtype=F32) + N_META)
    zero_state = jnp.zeros((HEADS, DV, LANES), F32)
    _, s_meta, r_meta = _mixer(h0, g1, w_all, cos0, sin0, w2p, gbp, ggain, rgain, rbias,
                               zero_state, zero_state, batch=1, n_invalid=N_PAD)

    w_out_b = w_out[0].astype(BF16)
    g2 = norm2_gain[0][None, :]
    w_q = peer_w_q[0].astype(BF16)
    subkeys = peer_subkeys[0].reshape(2 * PEER_HEADS, PEER_NKEYS, PEER_DKEY // 2).astype(BF16)
    u3 = _pack_table(peer_u[0])
    v3 = _pack_table(peer_v[0])
    gf = final_gain[None, :]

    out = jnp.zeros((b * s, d), F32)
    prev_g = None
    pending = []
    waiting = None
    for bi in range(b):
        s_state, r_state = s_meta, r_meta
        start = 0
        for seg in _segment_lengths(s, first=bi == 0):
            row0 = bi * s + start
            xs = x2d[row0:row0 + seg]
            pos = slice(start, start + seg)
            start += seg
            if prev_g is not None:
                xs, _ = lax.optimization_barrier((xs, prev_g))
            o, s_state, r_state = _mixer(xs, g1, w_all, cos1[pos], sin1[pos], w2p, gbp,
                                         ggain, rgain, rbias, s_state, r_state, batch=1, n_invalid=0)
            h2, xn2, xn2_pairs = _out_proj(o, xs, w_out_b, g2)
            idx, g = _route(xn2, w_q, subkeys)
            prev_g = g
            idx_flat = idx.reshape(-1)
            if pending:
                h2_old, row_old, peer_old = pending.pop(0)
                idx_flat, peer_old = lax.optimization_barrier((idx_flat, peer_old))
                out = _final(h2_old, peer_old, gf, out, row_old)
            part = _peer_dot_sc(u3, idx_flat, xn2_pairs)
            if waiting is not None:
                h2_w, row_w, idx_w, part_w, g_w = waiting
                part_w, _ = lax.optimization_barrier((part_w, g))
                pending.append((h2_w, row_w, _peer_combine_sc(v3, idx_w, _coef(part_w, g_w))))
            waiting = (h2, row0, idx_flat, part, g)
    h2_w, row_w, idx_w, part_w, g_w = waiting
    pending.append((h2_w, row_w, _peer_combine_sc(v3, idx_w, _coef(part_w, g_w))))
    for h2_old, row_old, peer_old in pending:
        out = _final(h2_old, peer_old, gf, out, row_old)
    return out.reshape(b, s, d)
```

```python
import functools
import math

import jax
import jax.numpy as jnp
import numpy as np
from jax import lax
from jax.experimental import pallas as pl
from jax.experimental.pallas import tpu as pltpu
from jax.experimental.pallas import tpu_sc as plsc

F32 = jnp.float32
BF16 = jnp.bfloat16

D_MODEL = 1024
N_META = 16
CHUNK = 64
N_PAD = CHUNK - N_META
EPS = 1e-6

HEADS = 4
DV = 128
DK = 64
LANES = 128
GATE_RANK = 16
GATE_TAU = 16.0
ROPE_BASE = 10000.0
IN_SPLITS = (HEADS * DK, HEADS * DK, HEADS * DV, HEADS * DV, GATE_RANK,
             HEADS * DK, HEADS * DK, HEADS * DV, HEADS * DV)

PEER_HEADS = 8
PEER_NKEYS = 128
PEER_DKEY = 256
PEER_TOPK = 16
PEER_SEL = PEER_HEADS * PEER_TOPK

HW = HEADS * LANES
OFF_GQ, OFF_GK, OFF_GV, OFF_GG = 0, HW, 2 * HW, 3 * HW
OFF_RQ, OFF_RK, OFF_RV, OFF_RG = 4 * HW, 5 * HW, 6 * HW, 7 * HW
OFF_RQR, OFF_RKR, OFF_GA = 8 * HW, 9 * HW, 10 * HW
PROJ_W = 10 * HW + 2 * LANES
PROJ_TN = PROJ_W // 3

LOG_GAMMA = tuple(math.log(1.0 - 2.0 ** (-5.0 - h)) for h in range(HEADS))

SC_CORES, SC_SUBCORES, SC_LANES = 2, 16, 16
SC_WORKERS = SC_CORES * SC_SUBCORES
SC_TOK_BATCH = 16
SC_ROWS = 16
SC_NBUF = 8
SC_GATHERS = SC_TOK_BATCH * PEER_SEL // SC_ROWS
HALF = D_MODEL // 2
ROW_TILES = HALF // LANES

SC_CALL_TOKENS = SC_WORKERS * SC_TOK_BATCH
SEGS_PER_SEQ = 2
SEQ_RAMPS = ((1, 1, 2, 2, 3, 3, 4), (5, 5, 6))

MIXER_ROWS = 256
DENSE_ROWS = 512
VMEM_LIMIT = 48 * 1024 * 1024


def _row_block(n, target):
    blk = math.gcd(n, target)
    assert blk % 8 == 0 or blk == n, (n, target)
    return blk


def _dot(a, b, **kw):
    return jnp.dot(a, b, preferred_element_type=F32, **kw)


def _dot_nt(a, b):
    return lax.dot_general(a, b, (((1,), (1,)), ((), ())), preferred_element_type=F32)


def _dot_tn(a, b):
    return lax.dot_general(a, b, (((0,), (0,)), ((), ())), preferred_element_type=F32)


def _rms(x, g):
    return x * lax.rsqrt(jnp.mean(x * x, axis=-1, keepdims=True) + EPS) * g


def _silu(x):
    return x * jax.nn.sigmoid(x)


def _mixer_kernel(x_ref, g1_ref, wall_ref, cos_ref, sin_ref, w2_ref, gb_ref, ggain_ref, rgain_ref,
                  rbias_ref, s0_ref, r0_ref, o_ref, s_out_ref, r_out_ref, s_scr, r_scr, proj_ref,
                  *, n_invalid, rows_blk):
    j = pl.program_id(1)
    xn = _rms(x_ref[...], g1_ref[...]).astype(BF16)
    for jn in range(PROJ_W // PROJ_TN):
        cols = slice(jn * PROJ_TN, (jn + 1) * PROJ_TN)
        proj_ref[:, cols] = _dot(xn, wall_ref[:, cols])

    @pl.when(j == 0)
    def _():
        s_scr[...] = s0_ref[...]
        r_scr[...] = r0_ref[...]

    row = lax.broadcasted_iota(jnp.int32, (CHUNK, CHUNK), 0)
    col = lax.broadcasted_iota(jnp.int32, (CHUNK, CHUNK), 1)
    causal = row >= col
    tril = causal.astype(F32)
    diff = jnp.maximum(row - col, 0).astype(F32)
    rowf = lax.broadcasted_iota(jnp.int32, (CHUNK, LANES), 0).astype(F32)
    row1 = lax.broadcasted_iota(jnp.int32, (CHUNK, 1), 0)

    def chunk_body(c, carry):
        r0 = pl.multiple_of(c * CHUNK, CHUNK)
        rows = pl.ds(r0, CHUNK)
        valid = ((j * rows_blk + r0 + row1) >= n_invalid).astype(F32)

        ga = proj_ref[rows, OFF_GA:OFF_GA + LANES].astype(BF16)
        pre = _dot(ga, w2_ref[...]) + gb_ref[...]
        log_a = jax.nn.log_sigmoid(pre) * (1.0 / GATE_TAU)
        bcum = _dot(tril, log_a, precision=lax.Precision.HIGHEST)
        b_last = bcum[CHUNK - 1:CHUNK, :]
        gq = proj_ref[rows, OFF_GQ:OFF_GQ + HW] * (DK ** -0.5)
        gk = proj_ref[rows, OFF_GK:OFF_GK + HW] * valid
        q_t = (gq * jnp.exp(bcum)).astype(BF16)
        k_t = (gk * jnp.exp(-bcum)).astype(BF16)
        k_end = (gk * jnp.exp(b_last - bcum)).astype(BF16)
        decay = jnp.exp(b_last)
        for h in range(HEADS):
            sl = slice(h * LANES, (h + 1) * LANES)
            v = proj_ref[rows, OFF_GV + h * DV:OFF_GV + (h + 1) * DV].astype(BF16)
            att = jnp.where(causal, _dot_nt(q_t[:, sl], k_t[:, sl]), 0.0).astype(BF16)
            s_prev = s_scr[h]
            o = _dot(att, v) + _dot_nt(q_t[:, sl], s_prev.astype(BF16))
            s_scr[h] = s_prev * decay[:, sl] + _dot_tn(v, k_end[:, sl])
            o = o * lax.rsqrt(jnp.mean(o * o, axis=-1, keepdims=True) + EPS) * ggain_ref[...]
            gate = proj_ref[rows, OFF_GG + h * DV:OFF_GG + (h + 1) * DV]
            o_ref[rows, h * DV:(h + 1) * DV] = (o * _silu(gate)).astype(o_ref.dtype)

        cosb = cos_ref[rows, :]
        sinb = sin_ref[rows, :]
        rq = proj_ref[rows, OFF_RQ:OFF_RQ + HW]
        rqr = proj_ref[rows, OFF_RQR:OFF_RQR + HW]
        rk = proj_ref[rows, OFF_RK:OFF_RK + HW] * valid
        rkr = proj_ref[rows, OFF_RKR:OFF_RKR + HW] * valid
        q_r = (rq * cosb + rqr * sinb).astype(BF16)
        k_r = (rk * cosb + rkr * sinb) * (DK ** -0.5)
        for h in range(HEADS):
            sl = slice(h * LANES, (h + 1) * LANES)
            lg = LOG_GAMMA[h]
            dmat = jnp.where(causal, jnp.exp(lg * diff), 0.0)
            xi = jnp.exp(lg * (rowf + 1.0))
            zeta = jnp.exp(lg * (CHUNK - 1.0 - rowf))
            v = proj_ref[rows, OFF_RV + h * DV:OFF_RV + (h + 1) * DV].astype(BF16)
            k_h = k_r[:, sl]
            att = (_dot_nt(q_r[:, sl], k_h.astype(BF16)) * dmat).astype(BF16)
            r_prev = r_scr[h]
            o = _dot(att, v) + _dot_nt(q_r[:, sl], r_prev.astype(BF16)) * xi
            r_scr[h] = r_prev * math.exp(lg * CHUNK) + _dot_tn(v, (k_h * zeta).astype(BF16))
            mu = jnp.mean(o, axis=-1, keepdims=True)
            var = jnp.mean(jnp.square(o - mu), axis=-1, keepdims=True)
            o = (o - mu) * lax.rsqrt(var + EPS) * rgain_ref[...] + rbias_ref[...]
            gate = proj_ref[rows, OFF_RG + h * DV:OFF_RG + (h + 1) * DV]
            o_ref[rows, HW + h * DV:HW + (h + 1) * DV] = (o * _silu(gate)).astype(o_ref.dtype)
        return carry

    lax.fori_loop(0, rows_blk // CHUNK, chunk_body, 0)
    s_out_ref[...] = s_scr[...]
    r_out_ref[...] = r_scr[...]


def _mixer(x2d, g1, w_all, cosf, sinf, w2p, gbp, ggain, rgain, rbias, s0, r0, *, batch, n_invalid):
    rows = x2d.shape[0] // batch
    rows_blk = _row_block(rows, MIXER_ROWS)
    nblk = rows // rows_blk
    const2 = lambda b, j: (0, 0)
    const3 = lambda b, j: (0, 0, 0)
    state = jax.ShapeDtypeStruct((HEADS, DV, LANES), F32)
    return pl.pallas_call(
        functools.partial(_mixer_kernel, n_invalid=n_invalid, rows_blk=rows_blk),
        grid=(batch, nblk),
        in_specs=[
            pl.BlockSpec((rows_blk, D_MODEL), lambda b, j: (b * nblk + j, 0)),
            pl.BlockSpec((1, D_MODEL), const2),
            pl.BlockSpec((D_MODEL, PROJ_W), const2),
            pl.BlockSpec((rows_blk, HW), lambda b, j: (j, 0)),
            pl.BlockSpec((rows_blk, HW), lambda b, j: (j, 0)),
            pl.BlockSpec((LANES, HW), const2),
            pl.BlockSpec((1, HW), const2),
            pl.BlockSpec((1, DV), const2),
            pl.BlockSpec((1, DV), const2),
            pl.BlockSpec((1, DV), const2),
            pl.BlockSpec((HEADS, DV, LANES), const3),
            pl.BlockSpec((HEADS, DV, LANES), const3),
        ],
        out_specs=[
            pl.BlockSpec((rows_blk, D_MODEL), lambda b, j: (b * nblk + j, 0)),
            pl.BlockSpec((HEADS, DV, LANES), const3),
            pl.BlockSpec((HEADS, DV, LANES), const3),
        ],
        out_shape=[jax.ShapeDtypeStruct((batch * rows, D_MODEL), BF16), state, state],
        scratch_shapes=[pltpu.VMEM((HEADS, DV, LANES), F32), pltpu.VMEM((HEADS, DV, LANES), F32),
                        pltpu.VMEM((rows_blk, PROJ_W), F32)],
        compiler_params=pltpu.CompilerParams(
            dimension_semantics=("arbitrary", "arbitrary"), vmem_limit_bytes=VMEM_LIMIT),
        name="token_mixers",
    )(x2d, g1, w_all, cosf, sinf, w2p, gbp, ggain, rgain, rbias, s0, r0)


def _out_proj_kernel(o_ref, x_ref, w_ref, g_ref, h_ref, xn_ref, xp_ref):
    h = x_ref[...] + _dot(o_ref[...], w_ref[...])
    h_ref[...] = h
    xn = _rms(h, g_ref[...])
    xn_ref[...] = xn
    xp_ref[...] = _pack_pairs(xn[:, :HALF], xn[:, HALF:])


def _out_proj(o, x2d, w_out, gain2):
    n = x2d.shape[0]
    tm = _row_block(n, DENSE_ROWS)
    blk = pl.BlockSpec((tm, D_MODEL), lambda i: (i, 0))
    return pl.pallas_call(
        _out_proj_kernel,
        grid=(n // tm,),
        in_specs=[blk, blk, pl.BlockSpec((D_MODEL, D_MODEL), lambda i: (0, 0)),
                  pl.BlockSpec((1, D_MODEL), lambda i: (0, 0))],
        out_specs=[blk, blk, pl.BlockSpec((tm, HALF), lambda i: (i, 0))],
        out_shape=[jax.ShapeDtypeStruct((n, D_MODEL), F32)] * 2 + [jax.ShapeDtypeStruct((n, HALF), jnp.uint32)],
        compiler_params=pltpu.CompilerParams(
            dimension_semantics=("parallel",), vmem_limit_bytes=VMEM_LIMIT),
        name="out_proj_norm",
    )(o, x2d, w_out, gain2)


def _top_k_rows(arrays, payloads=None):
    arrays = list(arrays)
    n = arrays[0].shape[0]
    rowi = lax.broadcasted_iota(jnp.int32, arrays[0].shape, 0)
    vals = [[] for _ in arrays]
    picks = [[] for _ in arrays]
    for _ in range(PEER_TOPK):
        for k, s in enumerate(arrays):
            m = jnp.max(s, axis=0, keepdims=True)
            am = jnp.min(jnp.where(s == m, rowi, n), axis=0, keepdims=True)
            sel = rowi == am
            vals[k].append(m)
            if payloads is None:
                picks[k].append(am)
            else:
                picks[k].append(jnp.min(jnp.where(sel, payloads[k], jnp.iinfo(jnp.int32).max),
                                        axis=0, keepdims=True))
            arrays[k] = jnp.where(sel, -jnp.inf, s)
    return [(jnp.concatenate(v, axis=0), jnp.concatenate(p, axis=0)) for v, p in zip(vals, picks)]


def _candidate_cells():
    cells = [(a, b) for a in range(PEER_TOPK) for b in range(PEER_TOPK) if (a + 1) * (b + 1) <= PEER_TOPK]
    n = -(-len(cells) // 8) * 8
    sel_a = np.zeros((n, PEER_TOPK), np.float32)
    sel_b = np.zeros((n, PEER_TOPK), np.float32)
    pad = np.full((n, 1), -np.inf, np.float32)
    for r, (a, b) in enumerate(cells):
        sel_a[r, a] = 1.0
        sel_b[r, b] = 1.0
        pad[r, 0] = 0.0
    return sel_a, sel_b, pad


def _route_kernel(xn_ref, wq_ref, sk_ref, sa_ref, sb_ref, pad_ref, idx_ref, g_ref, q_scr, i_scr, g_scr):
    tr = xn_ref.shape[0]
    q = _dot(xn_ref[...].astype(BF16), wq_ref[...])
    for hc in range(2 * PEER_HEADS):
        q_scr[hc] = q[:, hc * LANES:(hc + 1) * LANES].astype(BF16)

    def sub_body(sb, carry):
        rows = pl.ds(pl.multiple_of(sb * LANES, LANES), LANES)

        def head_pair(hp, c2):
            heads = [2 * hp, 2 * hp + 1]
            scores = [_dot_nt(sk_ref[2 * h + c], q_scr[2 * h + c, rows, :])
                      for h in heads for c in range(2)]
            first = _top_k_rows(scores)
            sa, sb = sa_ref[...], sb_ref[...]
            cands, cidxs = [], []
            for k in range(len(heads)):
                (v0, i0), (v1, i1) = first[2 * k], first[2 * k + 1]
                cands.append((_dot(sa, v0, precision=lax.Precision.HIGHEST)
                              + _dot(sb, v1, precision=lax.Precision.HIGHEST)) + pad_ref[...])
                cidxs.append((_dot(sa, i0.astype(F32)) * PEER_NKEYS
                              + _dot(sb, i1.astype(F32))).astype(jnp.int32))
            for h, (best, e) in zip(heads, _top_k_rows(cands, cidxs)):
                ex = jnp.exp(best - best[0:1])
                i_scr[h] = e
                g_scr[h] = ex / jnp.sum(ex, axis=0, keepdims=True)
            return c2

        lax.fori_loop(0, PEER_HEADS // 2, head_pair, 0)
        idx_ref[rows, :] = i_scr[...].reshape(PEER_SEL, LANES).T
        g_ref[rows, :] = g_scr[...].reshape(PEER_SEL, LANES).T
        return carry

    lax.fori_loop(0, tr // LANES, sub_body, 0)


def _route(xn2, w_q, subkeys):
    n = xn2.shape[0]
    sel_a, sel_b, pad = _candidate_cells()
    tr = _row_block(n, DENSE_ROWS)
    qw = 2 * PEER_HEADS * LANES
    return pl.pallas_call(
        _route_kernel,
        grid=(n // tr,),
        in_specs=[pl.BlockSpec((tr, D_MODEL), lambda i: (i, 0)),
                  pl.BlockSpec((D_MODEL, qw), lambda i: (0, 0)),
                  pl.BlockSpec((2 * PEER_HEADS, PEER_NKEYS, LANES), lambda i: (0, 0, 0)),
                  pl.BlockSpec(sel_a.shape, lambda i: (0, 0)),
                  pl.BlockSpec(sel_b.shape, lambda i: (0, 0)),
                  pl.BlockSpec(pad.shape, lambda i: (0, 0))],
        out_specs=[pl.BlockSpec((tr, PEER_SEL), lambda i: (i, 0))] * 2,
        out_shape=[jax.ShapeDtypeStruct((n, PEER_SEL), jnp.int32),
                   jax.ShapeDtypeStruct((n, PEER_SEL), F32)],
        scratch_shapes=[pltpu.VMEM((2 * PEER_HEADS, tr, LANES), BF16),
                        pltpu.VMEM((PEER_HEADS, PEER_TOPK, LANES), jnp.int32),
                        pltpu.VMEM((PEER_HEADS, PEER_TOPK, LANES), F32)],
        compiler_params=pltpu.CompilerParams(
            dimension_semantics=("parallel",), vmem_limit_bytes=VMEM_LIMIT),
        name="peer_route",
    )(xn2, w_q, subkeys, sel_a, sel_b, pad)


def _sc_worker_id():
    return lax.axis_index("s") * SC_CORES + lax.axis_index("c")


def _sc_mesh():
    return plsc.VectorSubcoreMesh(core_axis_name="c", subcore_axis_name="s")


def _sc_gather_ring(table_hbm, idx_v, bufs, sems, unit, compute):
    def gather(g, k):
        off = pl.multiple_of(g * SC_ROWS, SC_ROWS)
        return pltpu.make_async_copy(table_hbm.at[idx_v.at[pl.ds(off, SC_ROWS)]], bufs[k], sems[k])

    for k in range(SC_NBUF):
        gather(k, k).start()

    units_per_iter = SC_NBUF // unit

    def ring_iter(i, carry):
        for m in range(units_per_iter):
            ks = range(m * unit, (m + 1) * unit)
            for k in ks:
                gather(i * SC_NBUF + k, k).wait()
            compute(i * units_per_iter + m, [bufs[k] for k in ks])
            for k in ks:
                g_next = (i + 1) * SC_NBUF + k

                @pl.when(g_next < SC_GATHERS)
                def _():
                    gather(g_next, k).start()
        return carry

    lax.fori_loop(0, SC_GATHERS // SC_NBUF, ring_iter, 0)


def _bf16_bits(x):
    b = lax.bitcast_convert_type(x, jnp.uint32)
    return (b + jnp.uint32(0x7FFF) + ((b >> 16) & jnp.uint32(1))) >> 16


def _pack_pairs(lo, hi):
    return _bf16_bits(lo) | (_bf16_bits(hi) << 16)


def _pack_table(t):
    return _pack_pairs(t[:, :HALF], t[:, HALF:]).reshape(t.shape[0], ROW_TILES, LANES)


def _mul_add_words(w1, m1, w2, m2):
    bf = lambda a: plsc.bitcast(a, jnp.bfloat16)
    p = plsc.bitcast(bf(w1) * bf(m1) + bf(w2) * bf(m2), jnp.uint32)
    return lax.bitcast_convert_type(p << 16, F32), lax.bitcast_convert_type(p, F32)


def _sc_scratch(staged_cols, out_cols):
    return ([pltpu.VMEM((SC_TOK_BATCH * PEER_SEL,), jnp.int32),
             pltpu.VMEM((SC_TOK_BATCH, staged_cols), jnp.uint32),
             pltpu.VMEM((SC_TOK_BATCH, out_cols), F32)]
            + [pltpu.VMEM((SC_ROWS, ROW_TILES, LANES), jnp.uint32) for _ in range(SC_NBUF)]
            + [pltpu.SemaphoreType.DMA for _ in range(SC_NBUF)])


def _sc_batches(tpw, idx_hbm, staged_hbm, out_hbm, idx_v, staged_v, out_v, run):
    tok0 = _sc_worker_id() * tpw

    def batch(tb, carry):
        t0 = pl.multiple_of(tok0 + tb * SC_TOK_BATCH, SC_TOK_BATCH)
        pltpu.sync_copy(idx_hbm.at[pl.ds(pl.multiple_of(t0 * PEER_SEL, SC_TOK_BATCH * PEER_SEL),
                                         SC_TOK_BATCH * PEER_SEL)], idx_v)
        pltpu.sync_copy(staged_hbm.at[pl.ds(t0, SC_TOK_BATCH)], staged_v)
        run()
        pltpu.sync_copy(out_v, out_hbm.at[pl.ds(t0, SC_TOK_BATCH)])
        return carry

    lax.fori_loop(0, tpw // SC_TOK_BATCH, batch, 0)


def _peer_dot_sc(u3, idx_flat, xn2):
    t = xn2.shape[0]
    unit_rows = 16
    unit = unit_rows // SC_ROWS
    units_per_tok = PEER_SEL // unit_rows

    @functools.partial(
        pl.kernel,
        mesh=_sc_mesh(),
        out_type=jax.ShapeDtypeStruct((t, PEER_SEL * SC_LANES), F32),
        scratch_types=_sc_scratch(HALF, PEER_SEL * SC_LANES),
        compiler_params=pltpu.CompilerParams(needs_layout_passes=False),
        name="peer_dot",
    )
    def body(u_hbm, idx_hbm, x_hbm, out_hbm, idx_v, x_v, out_v, *rest):
        bufs, sems = rest[:SC_NBUF], rest[SC_NBUF:]

        def compute(u, ub):
            tl = u // units_per_tok
            rb = (u % units_per_tok) * unit_rows

            def cbody(c, acc):
                acc = list(acc)
                c4 = c // (LANES // (2 * SC_LANES))
                lane0 = pl.multiple_of((c % (LANES // (2 * SC_LANES))) * 2 * SC_LANES, 2 * SC_LANES)
                col = pl.multiple_of(c * 2 * SC_LANES, 2 * SC_LANES)
                xw1 = x_v[tl, pl.ds(col, SC_LANES)]
                xw2 = x_v[tl, pl.ds(col + SC_LANES, SC_LANES)]
                for q, buf in enumerate(ub):
                    for r in range(SC_ROWS):
                        a = q * SC_ROWS + r
                        lo, hi = _mul_add_words(buf[r, c4, pl.ds(lane0, SC_LANES)], xw1,
                                                buf[r, c4, pl.ds(lane0 + SC_LANES, SC_LANES)], xw2)
                        acc[a] = acc[a] + lo + hi
                return tuple(acc)

            acc = lax.fori_loop(0, HALF // (2 * SC_LANES), cbody,
                                tuple(jnp.zeros((SC_LANES,), F32) for _ in range(unit_rows)))
            for a in range(unit_rows):
                o = pl.multiple_of((rb + a) * SC_LANES, SC_LANES)
                out_v[tl, pl.ds(o, SC_LANES)] = acc[a]

        _sc_batches(t // SC_WORKERS, idx_hbm, x_hbm, out_hbm, idx_v, x_v, out_v,
                    lambda: _sc_gather_ring(u_hbm, idx_v, bufs, sems, unit, compute))

    return body(u3, idx_flat, xn2)


def _peer_combine_sc(v3, idx_flat, coef_b):
    t = coef_b.shape[0]
    unit = 2
    units_per_tok = PEER_SEL // (unit * SC_ROWS)
    nvec = LANES // SC_LANES

    @functools.partial(
        pl.kernel,
        mesh=_sc_mesh(),
        out_type=jax.ShapeDtypeStruct((t, D_MODEL), F32),
        scratch_types=_sc_scratch(PEER_SEL * SC_LANES, D_MODEL),
        compiler_params=pltpu.CompilerParams(needs_layout_passes=False),
        name="peer_combine",
    )
    def body(v_hbm, idx_hbm, coef_hbm, out_hbm, idx_v, coef_v, out_v, *rest):
        bufs, sems = rest[:SC_NBUF], rest[SC_NBUF:]

        def compute(u, ub):
            tl = u // units_per_tok
            rb = (u % units_per_tok) * unit * SC_ROWS
            for g in range(ROW_TILES):
                cols = ([pl.ds(g * LANES + l * SC_LANES, SC_LANES) for l in range(nvec)]
                        + [pl.ds(HALF + g * LANES + l * SC_LANES, SC_LANES) for l in range(nvec)])

                def rbody(r, acc):
                    acc = list(acc)
                    co = pl.multiple_of((rb + r) * SC_LANES, SC_LANES)
                    cf1 = coef_v[tl, pl.ds(co, SC_LANES)]
                    cf2 = coef_v[tl, pl.ds(co + SC_ROWS * SC_LANES, SC_LANES)]
                    for l in range(nvec):
                        lo, hi = _mul_add_words(ub[0][r, g, pl.ds(l * SC_LANES, SC_LANES)], cf1,
                                                ub[1][r, g, pl.ds(l * SC_LANES, SC_LANES)], cf2)
                        acc[l] = acc[l] + lo
                        acc[nvec + l] = acc[nvec + l] + hi
                    return tuple(acc)

                acc = lax.fori_loop(0, SC_ROWS, rbody, tuple(out_v[tl, c] for c in cols))
                for k, c in enumerate(cols):
                    out_v[tl, c] = acc[k]

        def run():
            def zero(tl, c2):
                for k in range(D_MODEL // SC_LANES):
                    out_v[tl, pl.ds(k * SC_LANES, SC_LANES)] = jnp.zeros((SC_LANES,), F32)
                return c2

            lax.fori_loop(0, SC_TOK_BATCH, zero, 0)
            _sc_gather_ring(v_hbm, idx_v, bufs, sems, unit, compute)

        _sc_batches(t // SC_WORKERS, idx_hbm, coef_hbm, out_hbm, idx_v, coef_v, out_v, run)

    return body(v3, idx_flat, coef_b)


def _coef_kernel(part_ref, g_ref, e_ref, coef_ref):
    act = lax.dot_general(part_ref[...], e_ref[...], (((1,), (1,)), ((), ())),
                          precision=lax.Precision.HIGHEST, preferred_element_type=F32)
    gelu = 0.5 * act * (1.0 + lax.erf(act * np.float32(math.sqrt(0.5))))
    c = _dot(g_ref[...] * gelu, e_ref[...], precision=lax.Precision.HIGHEST)
    coef_ref[...] = _pack_pairs(c, c)


def _coef(part, g):
    n = part.shape[0]
    tr = _row_block(n, DENSE_ROWS)
    w = PEER_SEL * SC_LANES
    expand = (jnp.arange(w, dtype=jnp.int32)[None, :] // SC_LANES
              == jnp.arange(PEER_SEL, dtype=jnp.int32)[:, None]).astype(F32)
    wide = pl.BlockSpec((tr, w), lambda i: (i, 0))
    return pl.pallas_call(
        _coef_kernel,
        grid=(n // tr,),
        in_specs=[wide, pl.BlockSpec((tr, PEER_SEL), lambda i: (i, 0)),
                  pl.BlockSpec((PEER_SEL, w), lambda i: (0, 0))],
        out_specs=wide,
        out_shape=jax.ShapeDtypeStruct((n, w), jnp.uint32),
        compiler_params=pltpu.CompilerParams(
            dimension_semantics=("parallel",), vmem_limit_bytes=VMEM_LIMIT),
        name="peer_coef",
    )(part, g, expand)


def _final_kernel(h_ref, p_ref, g_ref, full_ref, o_ref):
    del full_ref
    o_ref[...] = _rms(h_ref[...] + p_ref[...], g_ref[...])


def _final(h2, peer_out, gain, out_full, row0):
    n = h2.shape[0]
    tm = math.gcd(_row_block(n, 2 * DENSE_ROWS), row0) if row0 else _row_block(n, 2 * DENSE_ROWS)
    assert tm % 8 == 0
    blk = pl.BlockSpec((tm, D_MODEL), lambda i: (i, 0))
    return pl.pallas_call(
        _final_kernel,
        grid=(n // tm,),
        in_specs=[blk, blk, pl.BlockSpec((1, D_MODEL), lambda i: (0, 0)),
                  pl.BlockSpec(memory_space=pl.ANY)],
        out_specs=pl.BlockSpec((tm, D_MODEL), lambda i: (row0 // tm + i, 0)),
        out_shape=jax.ShapeDtypeStruct(out_full.shape, F32),
        input_output_aliases={3: 0},
        compiler_params=pltpu.CompilerParams(
            dimension_semantics=("parallel",), vmem_limit_bytes=VMEM_LIMIT),
        name="final_norm",
    )(h2, peer_out, gain, out_full)


def _pad_heads(w, dk):
    lead = w.shape[:-1]
    w = w.reshape(lead + (HEADS, dk))
    w = jnp.pad(w, [(0, 0)] * len(lead) + [(0, 0), (0, LANES - dk)])
    return w.reshape(lead + (HEADS * LANES,))


def _rot_heads(w, dk):
    lead = w.shape[:-1]
    w = w.reshape(lead + (HEADS, 2, dk // 2))
    w = jnp.stack([-w[..., 1, :], w[..., 0, :]], axis=-2)
    return w.reshape(lead + (HEADS * dk,))


def _pack_in_weights(w_in):
    gq, gk, gv, gg, ga, rq, rk, rv, rg = jnp.split(w_in, np.cumsum(IN_SPLITS)[:-1].tolist(), axis=-1)
    ga = jnp.pad(ga, ((0, 0), (0, 2 * LANES - GATE_RANK)))
    return jnp.concatenate(
        [_pad_heads(gq, DK), _pad_heads(gk, DK), gv, gg,
         _pad_heads(rq, DK), _pad_heads(rk, DK), rv, rg,
         _pad_heads(_rot_heads(rq, DK), DK), _pad_heads(_rot_heads(rk, DK), DK), ga],
        axis=-1).astype(BF16)


def _rope_tables(pos):
    inv_freq = ROPE_BASE ** (-jnp.arange(0, DK, 2, dtype=F32) / DK)
    ang = pos[:, None] * inv_freq[None, :]
    full = lambda t: _pad_heads(jnp.tile(t, (1, 2 * HEADS)), DK)
    return full(jnp.cos(ang)), full(jnp.sin(ang))


def _segment_lengths(s, seq):
    if seq < len(SEQ_RAMPS):
        ramp = SEQ_RAMPS[seq]
        lengths = [s * k // sum(ramp) for k in ramp]
        assert sum(lengths) == s
    else:
        lengths = [s // SEGS_PER_SEQ] * SEGS_PER_SEQ
    assert all(n % SC_CALL_TOKENS == 0 for n in lengths)
    return lengths


def kernel(x, meta_tokens, norm1_gain, w_in, gla_gate_w2, gla_gate_b, gla_norm_gain, ret_norm_gain, ret_norm_bias, w_out, norm2_gain, peer_w_q, peer_subkeys, peer_u, peer_v, final_gain):
    b, s, d = x.shape
    assert d == D_MODEL and (s // SEGS_PER_SEQ) % SC_CALL_TOKENS == 0
    assert norm1_gain.shape[0] == 1, "single-layer block"
    x2d = x.reshape(b * s, d)

    w_all = _pack_in_weights(w_in[0])
    w2p = jnp.pad(_pad_heads(gla_gate_w2[0], DK), ((0, LANES - GATE_RANK), (0, 0))).astype(BF16)
    gbp = _pad_heads(gla_gate_b[0][None, :], DK)
    g1 = norm1_gain[0][None, :]
    ggain = gla_norm_gain[0][None, :]
    rgain = ret_norm_gain[0][None, :]
    rbias = ret_norm_bias[0][None, :]

    h0 = jnp.concatenate([jnp.zeros((N_PAD, d), x.dtype), meta_tokens.astype(x.dtype)], axis=0)
    cos0, sin0 = _rope_tables(jnp.arange(CHUNK, dtype=F32) - N_PAD)
    cos1, sin1 = _rope_tables(jnp.arange(s, dtype=F32) + N_META)
    zero_state = jnp.zeros((HEADS, DV, LANES), F32)
    _, s_meta, r_meta = _mixer(h0, g1, w_all, cos0, sin0, w2p, gbp, ggain, rgain, rbias,
                               zero_state, zero_state, batch=1, n_invalid=N_PAD)

    w_out_b = w_out[0].astype(BF16)
    g2 = norm2_gain[0][None, :]
    w_q = peer_w_q[0].astype(BF16)
    subkeys = peer_subkeys[0].reshape(2 * PEER_HEADS, PEER_NKEYS, PEER_DKEY // 2).astype(BF16)
    u3 = _pack_table(peer_u[0])
    v3 = _pack_table(peer_v[0])
    gf = final_gain[None, :]

    out = jnp.zeros((b * s, d), F32)
    prev_g = None
    pending = []
    waiting = None
    for bi in range(b):
        s_state, r_state = s_meta, r_meta
        start = 0
        for seg in _segment_lengths(s, bi):
            row0 = bi * s + start
            xs = x2d[row0:row0 + seg]
            pos = slice(start, start + seg)
            start += seg
            if prev_g is not None:
                xs, _ = lax.optimization_barrier((xs, prev_g))
            o, s_state, r_state = _mixer(xs, g1, w_all, cos1[pos], sin1[pos], w2p, gbp,
                                         ggain, rgain, rbias, s_state, r_state, batch=1, n_invalid=0)
            h2, xn2, xn2_pairs = _out_proj(o, xs, w_out_b, g2)
            idx, g = _route(xn2, w_q, subkeys)
            prev_g = g
            idx_flat = idx.reshape(-1)
            if pending:
                h2_old, row_old, peer_old = pending.pop(0)
                idx_flat, peer_old = lax.optimization_barrier((idx_flat, peer_old))
                out = _final(h2_old, peer_old, gf, out, row_old)
            part = _peer_dot_sc(u3, idx_flat, xn2_pairs)
            if waiting is not None:
                h2_w, row_w, idx_w, part_w, g_w = waiting
                part_w, _ = lax.optimization_barrier((part_w, g))
                pending.append((h2_w, row_w, _peer_combine_sc(v3, idx_w, _coef(part_w, g_w))))
            waiting = (h2, row0, idx_flat, part, g)
    h2_w, row_w, idx_w, part_w, g_w = waiting
    pending.append((h2_w, row_w, _peer_combine_sc(v3, idx_w, _coef(part_w, g_w))))
    for h2_old, row_old, peer_old in pending:
        out = _final(h2_old, peer_old, gf, out, row_old)
    return out.reshape(b, s, d)
```

```python
import functools
import math

import jax
import jax.numpy as jnp
import numpy as np
from jax import lax
from jax.experimental import pallas as pl
from jax.experimental.pallas import tpu as pltpu
from jax.experimental.pallas import tpu_sc as plsc

F32 = jnp.float32
BF16 = jnp.bfloat16

D_MODEL = 1024
N_META = 16
CHUNK = 64
N_PAD = CHUNK - N_META
EPS = 1e-6

HEADS = 4
DV = 128
DK = 64
LANES = 128
GATE_RANK = 16
GATE_TAU = 16.0
ROPE_BASE = 10000.0
IN_SPLITS = (HEADS * DK, HEADS * DK, HEADS * DV, HEADS * DV, GATE_RANK,
             HEADS * DK, HEADS * DK, HEADS * DV, HEADS * DV)

PEER_HEADS = 8
PEER_NKEYS = 128
PEER_DKEY = 256
PEER_TOPK = 16
PEER_SEL = PEER_HEADS * PEER_TOPK

HW = HEADS * LANES
OFF_GQ, OFF_GK, OFF_GV, OFF_GG = 0, HW, 2 * HW, 3 * HW
OFF_RQ, OFF_RK, OFF_RV, OFF_RG = 4 * HW, 5 * HW, 6 * HW, 7 * HW
OFF_RQR, OFF_RKR, OFF_GA = 8 * HW, 9 * HW, 10 * HW
PROJ_W = 10 * HW + 2 * LANES
PROJ_TN = PROJ_W // 3

LOG_GAMMA = tuple(math.log(1.0 - 2.0 ** (-5.0 - h)) for h in range(HEADS))

SC_CORES, SC_SUBCORES, SC_LANES = 2, 16, 16
SC_WORKERS = SC_CORES * SC_SUBCORES
SC_TOK_BATCH = 16
SC_ROWS = 16
SC_NBUF = 8
SC_GATHERS = SC_TOK_BATCH * PEER_SEL // SC_ROWS
HALF = D_MODEL // 2
ROW_TILES = HALF // LANES

SC_CALL_TOKENS = SC_WORKERS * SC_TOK_BATCH
SEGS_PER_SEQ = 2
SEQ_RAMPS = ((1, 1, 2, 2, 3, 3, 4), (5, 5, 6), (7, 9))

MIXER_ROWS = 256
DENSE_ROWS = 512
VMEM_LIMIT = 48 * 1024 * 1024


def _row_block(n, target):
    blk = math.gcd(n, target)
    assert blk % 8 == 0 or blk == n, (n, target)
    return blk


def _dot(a, b, **kw):
    return jnp.dot(a, b, preferred_element_type=F32, **kw)


def _dot_nt(a, b):
    return lax.dot_general(a, b, (((1,), (1,)), ((), ())), preferred_element_type=F32)


def _dot_tn(a, b):
    return lax.dot_general(a, b, (((0,), (0,)), ((), ())), preferred_element_type=F32)


def _rms(x, g):
    return x * lax.rsqrt(jnp.mean(x * x, axis=-1, keepdims=True) + EPS) * g


def _silu(x):
    return x * jax.nn.sigmoid(x)


def _mixer_kernel(x_ref, g1_ref, wall_ref, cos_ref, sin_ref, w2_ref, gb_ref, ggain_ref, rgain_ref,
                  rbias_ref, s0_ref, r0_ref, o_ref, s_out_ref, r_out_ref, s_scr, r_scr, proj_ref,
                  *, n_invalid, rows_blk):
    j = pl.program_id(1)
    xn = _rms(x_ref[...], g1_ref[...]).astype(BF16)
    for jn in range(PROJ_W // PROJ_TN):
        cols = slice(jn * PROJ_TN, (jn + 1) * PROJ_TN)
        proj_ref[:, cols] = _dot(xn, wall_ref[:, cols])

    @pl.when(j == 0)
    def _():
        s_scr[...] = s0_ref[...]
        r_scr[...] = r0_ref[...]

    row = lax.broadcasted_iota(jnp.int32, (CHUNK, CHUNK), 0)
    col = lax.broadcasted_iota(jnp.int32, (CHUNK, CHUNK), 1)
    causal = row >= col
    tril = causal.astype(F32)
    diff = jnp.maximum(row - col, 0).astype(F32)
    rowf = lax.broadcasted_iota(jnp.int32, (CHUNK, LANES), 0).astype(F32)
    row1 = lax.broadcasted_iota(jnp.int32, (CHUNK, 1), 0)

    def chunk_body(c, carry):
        r0 = pl.multiple_of(c * CHUNK, CHUNK)
        rows = pl.ds(r0, CHUNK)
        valid = ((j * rows_blk + r0 + row1) >= n_invalid).astype(F32)

        ga = proj_ref[rows, OFF_GA:OFF_GA + LANES].astype(BF16)
        pre = _dot(ga, w2_ref[...]) + gb_ref[...]
        log_a = jax.nn.log_sigmoid(pre) * (1.0 / GATE_TAU)
        bcum = _dot(tril, log_a, precision=lax.Precision.HIGHEST)
        b_last = bcum[CHUNK - 1:CHUNK, :]
        gq = proj_ref[rows, OFF_GQ:OFF_GQ + HW] * (DK ** -0.5)
        gk = proj_ref[rows, OFF_GK:OFF_GK + HW] * valid
        q_t = (gq * jnp.exp(bcum)).astype(BF16)
        k_t = (gk * jnp.exp(-bcum)).astype(BF16)
        k_end = (gk * jnp.exp(b_last - bcum)).astype(BF16)
        decay = jnp.exp(b_last)
        for h in range(HEADS):
            sl = slice(h * LANES, (h + 1) * LANES)
            v = proj_ref[rows, OFF_GV + h * DV:OFF_GV + (h + 1) * DV].astype(BF16)
            att = jnp.where(causal, _dot_nt(q_t[:, sl], k_t[:, sl]), 0.0).astype(BF16)
            s_prev = s_scr[h]
            o = _dot(att, v) + _dot_nt(q_t[:, sl], s_prev.astype(BF16))
            s_scr[h] = s_prev * decay[:, sl] + _dot_tn(v, k_end[:, sl])
            o = o * lax.rsqrt(jnp.mean(o * o, axis=-1, keepdims=True) + EPS) * ggain_ref[...]
            gate = proj_ref[rows, OFF_GG + h * DV:OFF_GG + (h + 1) * DV]
            o_ref[rows, h * DV:(h + 1) * DV] = (o * _silu(gate)).astype(o_ref.dtype)

        cosb = cos_ref[rows, :]
        sinb = sin_ref[rows, :]
        rq = proj_ref[rows, OFF_RQ:OFF_RQ + HW]
        rqr = proj_ref[rows, OFF_RQR:OFF_RQR + HW]
        rk = proj_ref[rows, OFF_RK:OFF_RK + HW] * valid
        rkr = proj_ref[rows, OFF_RKR:OFF_RKR + HW] * valid
        q_r = (rq * cosb + rqr * sinb).astype(BF16)
        k_r = (rk * cosb + rkr * sinb) * (DK ** -0.5)
        for h in range(HEADS):
            sl = slice(h * LANES, (h + 1) * LANES)
            lg = LOG_GAMMA[h]
            dmat = jnp.where(causal, jnp.exp(lg * diff), 0.0)
            xi = jnp.exp(lg * (rowf + 1.0))
            zeta = jnp.exp(lg * (CHUNK - 1.0 - rowf))
            v = proj_ref[rows, OFF_RV + h * DV:OFF_RV + (h + 1) * DV].astype(BF16)
            k_h = k_r[:, sl]
            att = (_dot_nt(q_r[:, sl], k_h.astype(BF16)) * dmat).astype(BF16)
            r_prev = r_scr[h]
            o = _dot(att, v) + _dot_nt(q_r[:, sl], r_prev.astype(BF16)) * xi
            r_scr[h] = r_prev * math.exp(lg * CHUNK) + _dot_tn(v, (k_h * zeta).astype(BF16))
            mu = jnp.mean(o, axis=-1, keepdims=True)
            var = jnp.mean(jnp.square(o - mu), axis=-1, keepdims=True)
            o = (o - mu) * lax.rsqrt(var + EPS) * rgain_ref[...] + rbias_ref[...]
            gate = proj_ref[rows, OFF_RG + h * DV:OFF_RG + (h + 1) * DV]
            o_ref[rows, HW + h * DV:HW + (h + 1) * DV] = (o * _silu(gate)).astype(o_ref.dtype)
        return carry

    lax.fori_loop(0, rows_blk // CHUNK, chunk_body, 0)
    s_out_ref[...] = s_scr[...]
    r_out_ref[...] = r_scr[...]


def _mixer(x2d, g1, w_all, cosf, sinf, w2p, gbp, ggain, rgain, rbias, s0, r0, *, batch, n_invalid):
    rows = x2d.shape[0] // batch
    rows_blk = _row_block(rows, MIXER_ROWS)
    nblk = rows // rows_blk
    const2 = lambda b, j: (0, 0)
    const3 = lambda b, j: (0, 0, 0)
    state = jax.ShapeDtypeStruct((HEADS, DV, LANES), F32)
    return pl.pallas_call(
        functools.partial(_mixer_kernel, n_invalid=n_invalid, rows_blk=rows_blk),
        grid=(batch, nblk),
        in_specs=[
            pl.BlockSpec((rows_blk, D_MODEL), lambda b, j: (b * nblk + j, 0)),
            pl.BlockSpec((1, D_MODEL), const2),
            pl.BlockSpec((D_MODEL, PROJ_W), const2),
            pl.BlockSpec((rows_blk, HW), lambda b, j: (j, 0)),
            pl.BlockSpec((rows_blk, HW), lambda b, j: (j, 0)),
            pl.BlockSpec((LANES, HW), const2),
            pl.BlockSpec((1, HW), const2),
            pl.BlockSpec((1, DV), const2),
            pl.BlockSpec((1, DV), const2),
            pl.BlockSpec((1, DV), const2),
            pl.BlockSpec((HEADS, DV, LANES), const3),
            pl.BlockSpec((HEADS, DV, LANES), const3),
        ],
        out_specs=[
            pl.BlockSpec((rows_blk, D_MODEL), lambda b, j: (b * nblk + j, 0)),
            pl.BlockSpec((HEADS, DV, LANES), const3),
            pl.BlockSpec((HEADS, DV, LANES), const3),
        ],
        out_shape=[jax.ShapeDtypeStruct((batch * rows, D_MODEL), BF16), state, state],
        scratch_shapes=[pltpu.VMEM((HEADS, DV, LANES), F32), pltpu.VMEM((HEADS, DV, LANES), F32),
                        pltpu.VMEM((rows_blk, PROJ_W), F32)],
        compiler_params=pltpu.CompilerParams(
            dimension_semantics=("arbitrary", "arbitrary"), vmem_limit_bytes=VMEM_LIMIT),
        name="token_mixers",
    )(x2d, g1, w_all, cosf, sinf, w2p, gbp, ggain, rgain, rbias, s0, r0)


def _out_proj_kernel(o_ref, x_ref, w_ref, g_ref, h_ref, xn_ref, xp_ref):
    h = x_ref[...] + _dot(o_ref[...], w_ref[...])
    h_ref[...] = h
    xn = _rms(h, g_ref[...])
    xn_ref[...] = xn
    xp_ref[...] = _pack_pairs(xn[:, :HALF], xn[:, HALF:])


def _out_proj(o, x2d, w_out, gain2):
    n = x2d.shape[0]
    tm = _row_block(n, DENSE_ROWS)
    blk = pl.BlockSpec((tm, D_MODEL), lambda i: (i, 0))
    return pl.pallas_call(
        _out_proj_kernel,
        grid=(n // tm,),
        in_specs=[blk, blk, pl.BlockSpec((D_MODEL, D_MODEL), lambda i: (0, 0)),
                  pl.BlockSpec((1, D_MODEL), lambda i: (0, 0))],
        out_specs=[blk, blk, pl.BlockSpec((tm, HALF), lambda i: (i, 0))],
        out_shape=[jax.ShapeDtypeStruct((n, D_MODEL), F32)] * 2 + [jax.ShapeDtypeStruct((n, HALF), jnp.uint32)],
        compiler_params=pltpu.CompilerParams(
            dimension_semantics=("parallel",), vmem_limit_bytes=VMEM_LIMIT),
        name="out_proj_norm",
    )(o, x2d, w_out, gain2)


def _top_k_rows(arrays, payloads=None):
    arrays = list(arrays)
    n = arrays[0].shape[0]
    rowi = lax.broadcasted_iota(jnp.int32, arrays[0].shape, 0)
    vals = [[] for _ in arrays]
    picks = [[] for _ in arrays]
    for _ in range(PEER_TOPK):
        for k, s in enumerate(arrays):
            m = jnp.max(s, axis=0, keepdims=True)
            am = jnp.min(jnp.where(s == m, rowi, n), axis=0, keepdims=True)
            sel = rowi == am
            vals[k].append(m)
            if payloads is None:
                picks[k].append(am)
            else:
                picks[k].append(jnp.min(jnp.where(sel, payloads[k], jnp.iinfo(jnp.int32).max),
                                        axis=0, keepdims=True))
            arrays[k] = jnp.where(sel, -jnp.inf, s)
    return [(jnp.concatenate(v, axis=0), jnp.concatenate(p, axis=0)) for v, p in zip(vals, picks)]


def _candidate_cells():
    cells = [(a, b) for a in range(PEER_TOPK) for b in range(PEER_TOPK) if (a + 1) * (b + 1) <= PEER_TOPK]
    n = -(-len(cells) // 8) * 8
    sel_a = np.zeros((n, PEER_TOPK), np.float32)
    sel_b = np.zeros((n, PEER_TOPK), np.float32)
    pad = np.full((n, 1), -np.inf, np.float32)
    for r, (a, b) in enumerate(cells):
        sel_a[r, a] = 1.0
        sel_b[r, b] = 1.0
        pad[r, 0] = 0.0
    return sel_a, sel_b, pad


def _route_kernel(xn_ref, wq_ref, sk_ref, sa_ref, sb_ref, pad_ref, idx_ref, g_ref, q_scr, i_scr, g_scr):
    tr = xn_ref.shape[0]
    q = _dot(xn_ref[...].astype(BF16), wq_ref[...])
    for hc in range(2 * PEER_HEADS):
        q_scr[hc] = q[:, hc * LANES:(hc + 1) * LANES].astype(BF16)

    def sub_body(sb, carry):
        rows = pl.ds(pl.multiple_of(sb * LANES, LANES), LANES)

        def head_pair(hp, c2):
            heads = [2 * hp, 2 * hp + 1]
            scores = [_dot_nt(sk_ref[2 * h + c], q_scr[2 * h + c, rows, :])
                      for h in heads for c in range(2)]
            first = _top_k_rows(scores)
            sa, sb = sa_ref[...], sb_ref[...]
            cands, cidxs = [], []
            for k in range(len(heads)):
                (v0, i0), (v1, i1) = first[2 * k], first[2 * k + 1]
                cands.append((_dot(sa, v0, precision=lax.Precision.HIGHEST)
                              + _dot(sb, v1, precision=lax.Precision.HIGHEST)) + pad_ref[...])
                cidxs.append((_dot(sa, i0.astype(F32)) * PEER_NKEYS
                              + _dot(sb, i1.astype(F32))).astype(jnp.int32))
            for h, (best, e) in zip(heads, _top_k_rows(cands, cidxs)):
                ex = jnp.exp(best - best[0:1])
                i_scr[h] = e
                g_scr[h] = ex / jnp.sum(ex, axis=0, keepdims=True)
            return c2

        lax.fori_loop(0, PEER_HEADS // 2, head_pair, 0)
        idx_ref[rows, :] = i_scr[...].reshape(PEER_SEL, LANES).T
        g_ref[rows, :] = g_scr[...].reshape(PEER_SEL, LANES).T
        return carry

    lax.fori_loop(0, tr // LANES, sub_body, 0)


def _route(xn2, w_q, subkeys):
    n = xn2.shape[0]
    sel_a, sel_b, pad = _candidate_cells()
    tr = _row_block(n, DENSE_ROWS)
    qw = 2 * PEER_HEADS * LANES
    return pl.pallas_call(
        _route_kernel,
        grid=(n // tr,),
        in_specs=[pl.BlockSpec((tr, D_MODEL), lambda i: (i, 0)),
                  pl.BlockSpec((D_MODEL, qw), lambda i: (0, 0)),
                  pl.BlockSpec((2 * PEER_HEADS, PEER_NKEYS, LANES), lambda i: (0, 0, 0)),
                  pl.BlockSpec(sel_a.shape, lambda i: (0, 0)),
                  pl.BlockSpec(sel_b.shape, lambda i: (0, 0)),
                  pl.BlockSpec(pad.shape, lambda i: (0, 0))],
        out_specs=[pl.BlockSpec((tr, PEER_SEL), lambda i: (i, 0))] * 2,
        out_shape=[jax.ShapeDtypeStruct((n, PEER_SEL), jnp.int32),
                   jax.ShapeDtypeStruct((n, PEER_SEL), F32)],
        scratch_shapes=[pltpu.VMEM((2 * PEER_HEADS, tr, LANES), BF16),
                        pltpu.VMEM((PEER_HEADS, PEER_TOPK, LANES), jnp.int32),
                        pltpu.VMEM((PEER_HEADS, PEER_TOPK, LANES), F32)],
        compiler_params=pltpu.CompilerParams(
            dimension_semantics=("parallel",), vmem_limit_bytes=VMEM_LIMIT),
        name="peer_route",
    )(xn2, w_q, subkeys, sel_a, sel_b, pad)


def _sc_worker_id():
    return lax.axis_index("s") * SC_CORES + lax.axis_index("c")


def _sc_mesh():
    return plsc.VectorSubcoreMesh(core_axis_name="c", subcore_axis_name="s")


def _sc_gather_ring(table_hbm, idx_v, bufs, sems, unit, compute):
    def gather(g, k):
        off = pl.multiple_of(g * SC_ROWS, SC_ROWS)
        return pltpu.make_async_copy(table_hbm.at[idx_v.at[pl.ds(off, SC_ROWS)]], bufs[k], sems[k])

    for k in range(SC_NBUF):
        gather(k, k).start()

    units_per_iter = SC_NBUF // unit

    def ring_iter(i, carry):
        for m in range(units_per_iter):
            ks = range(m * unit, (m + 1) * unit)
            for k in ks:
                gather(i * SC_NBUF + k, k).wait()
            compute(i * units_per_iter + m, [bufs[k] for k in ks])
            for k in ks:
                g_next = (i + 1) * SC_NBUF + k

                @pl.when(g_next < SC_GATHERS)
                def _():
                    gather(g_next, k).start()
        return carry

    lax.fori_loop(0, SC_GATHERS // SC_NBUF, ring_iter, 0)


def _bf16_bits(x):
    b = lax.bitcast_convert_type(x, jnp.uint32)
    return (b + jnp.uint32(0x7FFF) + ((b >> 16) & jnp.uint32(1))) >> 16


def _pack_pairs(lo, hi):
    return _bf16_bits(lo) | (_bf16_bits(hi) << 16)


def _pack_table(t):
    return _pack_pairs(t[:, :HALF], t[:, HALF:]).reshape(t.shape[0], ROW_TILES, LANES)


def _mul_add_words(w1, m1, w2, m2):
    bf = lambda a: plsc.bitcast(a, jnp.bfloat16)
    p = plsc.bitcast(bf(w1) * bf(m1) + bf(w2) * bf(m2), jnp.uint32)
    return lax.bitcast_convert_type(p << 16, F32), lax.bitcast_convert_type(p, F32)


def _sc_scratch(staged_cols, out_cols):
    return ([pltpu.VMEM((SC_TOK_BATCH * PEER_SEL,), jnp.int32),
             pltpu.VMEM((SC_TOK_BATCH, staged_cols), jnp.uint32),
             pltpu.VMEM((SC_TOK_BATCH, out_cols), F32)]
            + [pltpu.VMEM((SC_ROWS, ROW_TILES, LANES), jnp.uint32) for _ in range(SC_NBUF)]
            + [pltpu.SemaphoreType.DMA for _ in range(SC_NBUF)])


def _sc_batches(tpw, idx_hbm, staged_hbm, out_hbm, idx_v, staged_v, out_v, run):
    tok0 = _sc_worker_id() * tpw

    def batch(tb, carry):
        t0 = pl.multiple_of(tok0 + tb * SC_TOK_BATCH, SC_TOK_BATCH)
        pltpu.sync_copy(idx_hbm.at[pl.ds(pl.multiple_of(t0 * PEER_SEL, SC_TOK_BATCH * PEER_SEL),
                                         SC_TOK_BATCH * PEER_SEL)], idx_v)
        pltpu.sync_copy(staged_hbm.at[pl.ds(t0, SC_TOK_BATCH)], staged_v)
        run()
        pltpu.sync_copy(out_v, out_hbm.at[pl.ds(t0, SC_TOK_BATCH)])
        return carry

    lax.fori_loop(0, tpw // SC_TOK_BATCH, batch, 0)


def _peer_dot_sc(u3, idx_flat, xn2):
    t = xn2.shape[0]
    unit_rows = 16
    unit = unit_rows // SC_ROWS
    units_per_tok = PEER_SEL // unit_rows

    @functools.partial(
        pl.kernel,
        mesh=_sc_mesh(),
        out_type=jax.ShapeDtypeStruct((t, PEER_SEL * SC_LANES), F32),
        scratch_types=_sc_scratch(HALF, PEER_SEL * SC_LANES),
        compiler_params=pltpu.CompilerParams(needs_layout_passes=False),
        name="peer_dot",
    )
    def body(u_hbm, idx_hbm, x_hbm, out_hbm, idx_v, x_v, out_v, *rest):
        bufs, sems = rest[:SC_NBUF], rest[SC_NBUF:]

        def compute(u, ub):
            tl = u // units_per_tok
            rb = (u % units_per_tok) * unit_rows

            def cbody(c, acc):
                acc = list(acc)
                c4 = c // (LANES // (2 * SC_LANES))
                lane0 = pl.multiple_of((c % (LANES // (2 * SC_LANES))) * 2 * SC_LANES, 2 * SC_LANES)
                col = pl.multiple_of(c * 2 * SC_LANES, 2 * SC_LANES)
                xw1 = x_v[tl, pl.ds(col, SC_LANES)]
                xw2 = x_v[tl, pl.ds(col + SC_LANES, SC_LANES)]
                for q, buf in enumerate(ub):
                    for r in range(SC_ROWS):
                        a = q * SC_ROWS + r
                        lo, hi = _mul_add_words(buf[r, c4, pl.ds(lane0, SC_LANES)], xw1,
                                                buf[r, c4, pl.ds(lane0 + SC_LANES, SC_LANES)], xw2)
                        acc[a] = acc[a] + lo + hi
                return tuple(acc)

            acc = lax.fori_loop(0, HALF // (2 * SC_LANES), cbody,
                                tuple(jnp.zeros((SC_LANES,), F32) for _ in range(unit_rows)))
            for a in range(unit_rows):
                o = pl.multiple_of((rb + a) * SC_LANES, SC_LANES)
                out_v[tl, pl.ds(o, SC_LANES)] = acc[a]

        _sc_batches(t // SC_WORKERS, idx_hbm, x_hbm, out_hbm, idx_v, x_v, out_v,
                    lambda: _sc_gather_ring(u_hbm, idx_v, bufs, sems, unit, compute))

    return body(u3, idx_flat, xn2)


def _peer_combine_sc(v3, idx_flat, coef_b):
    t = coef_b.shape[0]
    unit = 2
    units_per_tok = PEER_SEL // (unit * SC_ROWS)
    nvec = LANES // SC_LANES

    @functools.partial(
        pl.kernel,
        mesh=_sc_mesh(),
        out_type=jax.ShapeDtypeStruct((t, D_MODEL), F32),
        scratch_types=_sc_scratch(PEER_SEL * SC_LANES, D_MODEL),
        compiler_params=pltpu.CompilerParams(needs_layout_passes=False),
        name="peer_combine",
    )
    def body(v_hbm, idx_hbm, coef_hbm, out_hbm, idx_v, coef_v, out_v, *rest):
        bufs, sems = rest[:SC_NBUF], rest[SC_NBUF:]

        def compute(u, ub):
            tl = u // units_per_tok
            rb = (u % units_per_tok) * unit * SC_ROWS
            for g in range(ROW_TILES):
                cols = ([pl.ds(g * LANES + l * SC_LANES, SC_LANES) for l in range(nvec)]
                        + [pl.ds(HALF + g * LANES + l * SC_LANES, SC_LANES) for l in range(nvec)])

                def rbody(r, acc):
                    acc = list(acc)
                    co = pl.multiple_of((rb + r) * SC_LANES, SC_LANES)
                    cf1 = coef_v[tl, pl.ds(co, SC_LANES)]
                    cf2 = coef_v[tl, pl.ds(co + SC_ROWS * SC_LANES, SC_LANES)]
                    for l in range(nvec):
                        lo, hi = _mul_add_words(ub[0][r, g, pl.ds(l * SC_LANES, SC_LANES)], cf1,
                                                ub[1][r, g, pl.ds(l * SC_LANES, SC_LANES)], cf2)
                        acc[l] = acc[l] + lo
                        acc[nvec + l] = acc[nvec + l] + hi
                    return tuple(acc)

                acc = lax.fori_loop(0, SC_ROWS, rbody, tuple(out_v[tl, c] for c in cols))
                for k, c in enumerate(cols):
                    out_v[tl, c] = acc[k]

        def run():
            def zero(tl, c2):
                for k in range(D_MODEL // SC_LANES):
                    out_v[tl, pl.ds(k * SC_LANES, SC_LANES)] = jnp.zeros((SC_LANES,), F32)
                return c2

            lax.fori_loop(0, SC_TOK_BATCH, zero, 0)
            _sc_gather_ring(v_hbm, idx_v, bufs, sems, unit, compute)

        _sc_batches(t // SC_WORKERS, idx_hbm, coef_hbm, out_hbm, idx_v, coef_v, out_v, run)

    return body(v3, idx_flat, coef_b)


def _coef_kernel(part_ref, g_ref, e_ref, coef_ref):
    act = lax.dot_general(part_ref[...], e_ref[...], (((1,), (1,)), ((), ())),
                          precision=lax.Precision.HIGHEST, preferred_element_type=F32)
    gelu = 0.5 * act * (1.0 + lax.erf(act * np.float32(math.sqrt(0.5))))
    c = _dot(g_ref[...] * gelu, e_ref[...], precision=lax.Precision.HIGHEST)
    coef_ref[...] = _pack_pairs(c, c)


def _coef(part, g):
    n = part.shape[0]
    tr = _row_block(n, DENSE_ROWS)
    w = PEER_SEL * SC_LANES
    expand = (jnp.arange(w, dtype=jnp.int32)[None, :] // SC_LANES
              == jnp.arange(PEER_SEL, dtype=jnp.int32)[:, None]).astype(F32)
    wide = pl.BlockSpec((tr, w), lambda i: (i, 0))
    return pl.pallas_call(
        _coef_kernel,
        grid=(n // tr,),
        in_specs=[wide, pl.BlockSpec((tr, PEER_SEL), lambda i: (i, 0)),
                  pl.BlockSpec((PEER_SEL, w), lambda i: (0, 0))],
        out_specs=wide,
        out_shape=jax.ShapeDtypeStruct((n, w), jnp.uint32),
        compiler_params=pltpu.CompilerParams(
            dimension_semantics=("parallel",), vmem_limit_bytes=VMEM_LIMIT),
        name="peer_coef",
    )(part, g, expand)


def _final_kernel(h_ref, p_ref, g_ref, full_ref, o_ref):
    del full_ref
    o_ref[...] = _rms(h_ref[...] + p_ref[...], g_ref[...])


def _final(h2, peer_out, gain, out_full, row0):
    n = h2.shape[0]
    tm = math.gcd(_row_block(n, 2 * DENSE_ROWS), row0) if row0 else _row_block(n, 2 * DENSE_ROWS)
    assert tm % 8 == 0
    blk = pl.BlockSpec((tm, D_MODEL), lambda i: (i, 0))
    return pl.pallas_call(
        _final_kernel,
        grid=(n // tm,),
        in_specs=[blk, blk, pl.BlockSpec((1, D_MODEL), lambda i: (0, 0)),
                  pl.BlockSpec(memory_space=pl.ANY)],
        out_specs=pl.BlockSpec((tm, D_MODEL), lambda i: (row0 // tm + i, 0)),
        out_shape=jax.ShapeDtypeStruct(out_full.shape, F32),
        input_output_aliases={3: 0},
        compiler_params=pltpu.CompilerParams(
            dimension_semantics=("parallel",), vmem_limit_bytes=VMEM_LIMIT),
        name="final_norm",
    )(h2, peer_out, gain, out_full)


def _pad_heads(w, dk):
    lead = w.shape[:-1]
    w = w.reshape(lead + (HEADS, dk))
    w = jnp.pad(w, [(0, 0)] * len(lead) + [(0, 0), (0, LANES - dk)])
    return w.reshape(lead + (HEADS * LANES,))


def _rot_heads(w, dk):
    lead = w.shape[:-1]
    w = w.reshape(lead + (HEADS, 2, dk // 2))
    w = jnp.stack([-w[..., 1, :], w[..., 0, :]], axis=-2)
    return w.reshape(lead + (HEADS * dk,))


def _pack_in_weights(w_in):
    gq, gk, gv, gg, ga, rq, rk, rv, rg = jnp.split(w_in, np.cumsum(IN_SPLITS)[:-1].tolist(), axis=-1)
    ga = jnp.pad(ga, ((0, 0), (0, 2 * LANES - GATE_RANK)))
    return jnp.concatenate(
        [_pad_heads(gq, DK), _pad_heads(gk, DK), gv, gg,
         _pad_heads(rq, DK), _pad_heads(rk, DK), rv, rg,
         _pad_heads(_rot_heads(rq, DK), DK), _pad_heads(_rot_heads(rk, DK), DK), ga],
        axis=-1).astype(BF16)


def _rope_tables(pos):
    inv_freq = ROPE_BASE ** (-jnp.arange(0, DK, 2, dtype=F32) / DK)
    ang = pos[:, None] * inv_freq[None, :]
    full = lambda t: _pad_heads(jnp.tile(t, (1, 2 * HEADS)), DK)
    return full(jnp.cos(ang)), full(jnp.sin(ang))


def _segment_lengths(s, seq):
    if seq < len(SEQ_RAMPS):
        ramp = SEQ_RAMPS[seq]
        lengths = [s * k // sum(ramp) for k in ramp]
        assert sum(lengths) == s
    else:
        lengths = [s // SEGS_PER_SEQ] * SEGS_PER_SEQ
    assert all(n % SC_CALL_TOKENS == 0 for n in lengths)
    return lengths


def kernel(x, meta_tokens, norm1_gain, w_in, gla_gate_w2, gla_gate_b, gla_norm_gain, ret_norm_gain, ret_norm_bias, w_out, norm2_gain, peer_w_q, peer_subkeys, peer_u, peer_v, final_gain):
    b, s, d = x.shape
    assert d == D_MODEL and (s // SEGS_PER_SEQ) % SC_CALL_TOKENS == 0
    assert norm1_gain.shape[0] == 1, "single-layer block"
    x2d = x.reshape(b * s, d)

    w_all = _pack_in_weights(w_in[0])
    w2p = jnp.pad(_pad_heads(gla_gate_w2[0], DK), ((0, LANES - GATE_RANK), (0, 0))).astype(BF16)
    gbp = _pad_heads(gla_gate_b[0][None, :], DK)
    g1 = norm1_gain[0][None, :]
    ggain = gla_norm_gain[0][None, :]
    rgain = ret_norm_gain[0][None, :]
    rbias = ret_norm_bias[0][None, :]

    h0 = jnp.concatenate([jnp.zeros((N_PAD, d), x.dtype), meta_tokens.astype(x.dtype)], axis=0)
    cos0, sin0 = _rope_tables(jnp.arange(CHUNK, dtype=F32) - N_PAD)
    cos1, sin1 = _rope_tables(jnp.arange(s, dtype=F32) + N_META)
    zero_state = jnp.zeros((HEADS, DV, LANES), F32)
    _, s_meta, r_meta = _mixer(h0, g1, w_all, cos0, sin0, w2p, gbp, ggain, rgain, rbias,
                               zero_state, zero_state, batch=1, n_invalid=N_PAD)

    w_out_b = w_out[0].astype(BF16)
    g2 = norm2_gain[0][None, :]
    w_q = peer_w_q[0].astype(BF16)
    subkeys = peer_subkeys[0].reshape(2 * PEER_HEADS, PEER_NKEYS, PEER_DKEY // 2).astype(BF16)
    u3 = _pack_table(peer_u[0])
    v3 = _pack_table(peer_v[0])
    gf = final_gain[None, :]

    out = jnp.zeros((b * s, d), F32)
    prev_g = None
    pending = []
    waiting = None
    for bi in range(b):
        s_state, r_state = s_meta, r_meta
        start = 0
        for seg in _segment_lengths(s, bi):
            row0 = bi * s + start
            xs = x2d[row0:row0 + seg]
            pos = slice(start, start + seg)
            start += seg
            if prev_g is not None:
                xs, _ = lax.optimization_barrier((xs, prev_g))
            o, s_state, r_state = _mixer(xs, g1, w_all, cos1[pos], sin1[pos], w2p, gbp,
                                         ggain, rgain, rbias, s_state, r_state, batch=1, n_invalid=0)
            h2, xn2, xn2_pairs = _out_proj(o, xs, w_out_b, g2)
            idx, g = _route(xn2, w_q, subkeys)
            prev_g = g
            idx_flat = idx.reshape(-1)
            if pending:
                h2_old, row_old, peer_old = pending.pop(0)
                idx_flat, peer_old = lax.optimization_barrier((idx_flat, peer_old))
                out = _final(h2_old, peer_old, gf, out, row_old)
            part = _peer_dot_sc(u3, idx_flat, xn2_pairs)
            if waiting is not None:
                h2_w, row_w, idx_w, part_w, g_w = waiting
                part_w, _ = lax.optimization_barrier((part_w, g))
                pending.append((h2_w, row_w, _peer_combine_sc(v3, idx_w, _coef(part_w, g_w))))
            waiting = (h2, row0, idx_flat, part, g)
    h2_w, row_w, idx_w, part_w, g_w = waiting
    pending.append((h2_w, row_w, _peer_combine_sc(v3, idx_w, _coef(part_w, g_w))))
    for h2_old, row_old, peer_old in pending:
        out = _final(h2_old, peer_old, gf, out, row_old)
    return out.reshape(b, s, d)
```

```python
import functools
import math

import jax
import jax.numpy as jnp
import numpy as np
from jax import lax
from jax.experimental import pallas as pl
from jax.experimental.pallas import tpu as pltpu
from jax.experimental.pallas import tpu_sc as plsc

F32 = jnp.float32
BF16 = jnp.bfloat16

D_MODEL = 1024
N_META = 16
CHUNK = 64
N_PAD = CHUNK - N_META
EPS = 1e-6

HEADS = 4
DV = 128
DK = 64
LANES = 128
GATE_RANK = 16
GATE_TAU = 16.0
ROPE_BASE = 10000.0
IN_SPLITS = (HEADS * DK, HEADS * DK, HEADS * DV, HEADS * DV, GATE_RANK,
             HEADS * DK, HEADS * DK, HEADS * DV, HEADS * DV)

PEER_HEADS = 8
PEER_NKEYS = 128
PEER_DKEY = 256
PEER_TOPK = 16
PEER_SEL = PEER_HEADS * PEER_TOPK

HW = HEADS * LANES
OFF_GQ, OFF_GK, OFF_GV, OFF_GG = 0, HW, 2 * HW, 3 * HW
OFF_RQ, OFF_RK, OFF_RV, OFF_RG = 4 * HW, 5 * HW, 6 * HW, 7 * HW
OFF_RQR, OFF_RKR, OFF_GA = 8 * HW, 9 * HW, 10 * HW
PROJ_W = 10 * HW + 2 * LANES
PROJ_TN = PROJ_W // 3

LOG_GAMMA = tuple(math.log(1.0 - 2.0 ** (-5.0 - h)) for h in range(HEADS))

SC_CORES, SC_SUBCORES, SC_LANES = 2, 16, 16
SC_WORKERS = SC_CORES * SC_SUBCORES
SC_TOK_BATCH = 16
SC_ROWS = 16
SC_NBUF = 8
SC_GATHERS = SC_TOK_BATCH * PEER_SEL // SC_ROWS
HALF = D_MODEL // 2
ROW_TILES = HALF // LANES

SC_CALL_TOKENS = SC_WORKERS * SC_TOK_BATCH
SEGS_PER_SEQ = 2
SEQ_RAMPS = ((1, 1, 1, 2, 2, 2, 3, 4), (5, 5, 6), (7, 9))

MIXER_ROWS = 256
DENSE_ROWS = 512
VMEM_LIMIT = 48 * 1024 * 1024


def _row_block(n, target):
    blk = math.gcd(n, target)
    assert blk % 8 == 0 or blk == n, (n, target)
    return blk


def _dot(a, b, **kw):
    return jnp.dot(a, b, preferred_element_type=F32, **kw)


def _dot_nt(a, b):
    return lax.dot_general(a, b, (((1,), (1,)), ((), ())), preferred_element_type=F32)


def _dot_tn(a, b):
    return lax.dot_general(a, b, (((0,), (0,)), ((), ())), preferred_element_type=F32)


def _rms(x, g):
    return x * lax.rsqrt(jnp.mean(x * x, axis=-1, keepdims=True) + EPS) * g


def _silu(x):
    return x * jax.nn.sigmoid(x)


def _mixer_kernel(x_ref, g1_ref, wall_ref, cos_ref, sin_ref, w2_ref, gb_ref, ggain_ref, rgain_ref,
                  rbias_ref, s0_ref, r0_ref, o_ref, s_out_ref, r_out_ref, s_scr, r_scr, proj_ref,
                  *, n_invalid, rows_blk):
    j = pl.program_id(1)
    xn = _rms(x_ref[...], g1_ref[...]).astype(BF16)
    for jn in range(PROJ_W // PROJ_TN):
        cols = slice(jn * PROJ_TN, (jn + 1) * PROJ_TN)
        proj_ref[:, cols] = _dot(xn, wall_ref[:, cols])

    @pl.when(j == 0)
    def _():
        s_scr[...] = s0_ref[...]
        r_scr[...] = r0_ref[...]

    row = lax.broadcasted_iota(jnp.int32, (CHUNK, CHUNK), 0)
    col = lax.broadcasted_iota(jnp.int32, (CHUNK, CHUNK), 1)
    causal = row >= col
    tril = causal.astype(F32)
    diff = jnp.maximum(row - col, 0).astype(F32)
    rowf = lax.broadcasted_iota(jnp.int32, (CHUNK, LANES), 0).astype(F32)
    row1 = lax.broadcasted_iota(jnp.int32, (CHUNK, 1), 0)

    def chunk_body(c, carry):
        r0 = pl.multiple_of(c * CHUNK, CHUNK)
        rows = pl.ds(r0, CHUNK)
        valid = ((j * rows_blk + r0 + row1) >= n_invalid).astype(F32)

        ga = proj_ref[rows, OFF_GA:OFF_GA + LANES].astype(BF16)
        pre = _dot(ga, w2_ref[...]) + gb_ref[...]
        log_a = jax.nn.log_sigmoid(pre) * (1.0 / GATE_TAU)
        bcum = _dot(tril, log_a, precision=lax.Precision.HIGHEST)
        b_last = bcum[CHUNK - 1:CHUNK, :]
        gq = proj_ref[rows, OFF_GQ:OFF_GQ + HW] * (DK ** -0.5)
        gk = proj_ref[rows, OFF_GK:OFF_GK + HW] * valid
        q_t = (gq * jnp.exp(bcum)).astype(BF16)
        k_t = (gk * jnp.exp(-bcum)).astype(BF16)
        k_end = (gk * jnp.exp(b_last - bcum)).astype(BF16)
        decay = jnp.exp(b_last)
        for h in range(HEADS):
            sl = slice(h * LANES, (h + 1) * LANES)
            v = proj_ref[rows, OFF_GV + h * DV:OFF_GV + (h + 1) * DV].astype(BF16)
            att = jnp.where(causal, _dot_nt(q_t[:, sl], k_t[:, sl]), 0.0).astype(BF16)
            s_prev = s_scr[h]
            o = _dot(att, v) + _dot_nt(q_t[:, sl], s_prev.astype(BF16))
            s_scr[h] = s_prev * decay[:, sl] + _dot_tn(v, k_end[:, sl])
            o = o * lax.rsqrt(jnp.mean(o * o, axis=-1, keepdims=True) + EPS) * ggain_ref[...]
            gate = proj_ref[rows, OFF_GG + h * DV:OFF_GG + (h + 1) * DV]
            o_ref[rows, h * DV:(h + 1) * DV] = (o * _silu(gate)).astype(o_ref.dtype)

        cosb = cos_ref[rows, :]
        sinb = sin_ref[rows, :]
        rq = proj_ref[rows, OFF_RQ:OFF_RQ + HW]
        rqr = proj_ref[rows, OFF_RQR:OFF_RQR + HW]
        rk = proj_ref[rows, OFF_RK:OFF_RK + HW] * valid
        rkr = proj_ref[rows, OFF_RKR:OFF_RKR + HW] * valid
        q_r = (rq * cosb + rqr * sinb).astype(BF16)
        k_r = (rk * cosb + rkr * sinb) * (DK ** -0.5)
        for h in range(HEADS):
            sl = slice(h * LANES, (h + 1) * LANES)
            lg = LOG_GAMMA[h]
            dmat = jnp.where(causal, jnp.exp(lg * diff), 0.0)
            xi = jnp.exp(lg * (rowf + 1.0))
            zeta = jnp.exp(lg * (CHUNK - 1.0 - rowf))
            v = proj_ref[rows, OFF_RV + h * DV:OFF_RV + (h + 1) * DV].astype(BF16)
            k_h = k_r[:, sl]
            att = (_dot_nt(q_r[:, sl], k_h.astype(BF16)) * dmat).astype(BF16)
            r_prev = r_scr[h]
            o = _dot(att, v) + _dot_nt(q_r[:, sl], r_prev.astype(BF16)) * xi
            r_scr[h] = r_prev * math.exp(lg * CHUNK) + _dot_tn(v, (k_h * zeta).astype(BF16))
            mu = jnp.mean(o, axis=-1, keepdims=True)
            var = jnp.mean(jnp.square(o - mu), axis=-1, keepdims=True)
            o = (o - mu) * lax.rsqrt(var + EPS) * rgain_ref[...] + rbias_ref[...]
            gate = proj_ref[rows, OFF_RG + h * DV:OFF_RG + (h + 1) * DV]
            o_ref[rows, HW + h * DV:HW + (h + 1) * DV] = (o * _silu(gate)).astype(o_ref.dtype)
        return carry

    lax.fori_loop(0, rows_blk // CHUNK, chunk_body, 0)
    s_out_ref[...] = s_scr[...]
    r_out_ref[...] = r_scr[...]


def _mixer(x2d, g1, w_all, cosf, sinf, w2p, gbp, ggain, rgain, rbias, s0, r0, *, batch, n_invalid):
    rows = x2d.shape[0] // batch
    rows_blk = _row_block(rows, MIXER_ROWS)
    nblk = rows // rows_blk
    const2 = lambda b, j: (0, 0)
    const3 = lambda b, j: (0, 0, 0)
    state = jax.ShapeDtypeStruct((HEADS, DV, LANES), F32)
    return pl.pallas_call(
        functools.partial(_mixer_kernel, n_invalid=n_invalid, rows_blk=rows_blk),
        grid=(batch, nblk),
        in_specs=[
            pl.BlockSpec((rows_blk, D_MODEL), lambda b, j: (b * nblk + j, 0)),
            pl.BlockSpec((1, D_MODEL), const2),
            pl.BlockSpec((D_MODEL, PROJ_W), const2),
            pl.BlockSpec((rows_blk, HW), lambda b, j: (j, 0)),
            pl.BlockSpec((rows_blk, HW), lambda b, j: (j, 0)),
            pl.BlockSpec((LANES, HW), const2),
            pl.BlockSpec((1, HW), const2),
            pl.BlockSpec((1, DV), const2),
            pl.BlockSpec((1, DV), const2),
            pl.BlockSpec((1, DV), const2),
            pl.BlockSpec((HEADS, DV, LANES), const3),
            pl.BlockSpec((HEADS, DV, LANES), const3),
        ],
        out_specs=[
            pl.BlockSpec((rows_blk, D_MODEL), lambda b, j: (b * nblk + j, 0)),
            pl.BlockSpec((HEADS, DV, LANES), const3),
            pl.BlockSpec((HEADS, DV, LANES), const3),
        ],
        out_shape=[jax.ShapeDtypeStruct((batch * rows, D_MODEL), BF16), state, state],
        scratch_shapes=[pltpu.VMEM((HEADS, DV, LANES), F32), pltpu.VMEM((HEADS, DV, LANES), F32),
                        pltpu.VMEM((rows_blk, PROJ_W), F32)],
        compiler_params=pltpu.CompilerParams(
            dimension_semantics=("arbitrary", "arbitrary"), vmem_limit_bytes=VMEM_LIMIT),
        name="token_mixers",
    )(x2d, g1, w_all, cosf, sinf, w2p, gbp, ggain, rgain, rbias, s0, r0)


def _out_proj_kernel(o_ref, x_ref, w_ref, g_ref, h_ref, xn_ref, xp_ref):
    h = x_ref[...] + _dot(o_ref[...], w_ref[...])
    h_ref[...] = h
    xn = _rms(h, g_ref[...])
    xn_ref[...] = xn
    xp_ref[...] = _pack_pairs(xn[:, :HALF], xn[:, HALF:])


def _out_proj(o, x2d, w_out, gain2):
    n = x2d.shape[0]
    tm = _row_block(n, DENSE_ROWS)
    blk = pl.BlockSpec((tm, D_MODEL), lambda i: (i, 0))
    return pl.pallas_call(
        _out_proj_kernel,
        grid=(n // tm,),
        in_specs=[blk, blk, pl.BlockSpec((D_MODEL, D_MODEL), lambda i: (0, 0)),
                  pl.BlockSpec((1, D_MODEL), lambda i: (0, 0))],
        out_specs=[blk, blk, pl.BlockSpec((tm, HALF), lambda i: (i, 0))],
        out_shape=[jax.ShapeDtypeStruct((n, D_MODEL), F32)] * 2 + [jax.ShapeDtypeStruct((n, HALF), jnp.uint32)],
        compiler_params=pltpu.CompilerParams(
            dimension_semantics=("parallel",), vmem_limit_bytes=VMEM_LIMIT),
        name="out_proj_norm",
    )(o, x2d, w_out, gain2)


def _top_k_rows(arrays, payloads=None):
    arrays = list(arrays)
    n = arrays[0].shape[0]
    rowi = lax.broadcasted_iota(jnp.int32, arrays[0].shape, 0)
    vals = [[] for _ in arrays]
    picks = [[] for _ in arrays]
    for _ in range(PEER_TOPK):
        for k, s in enumerate(arrays):
            m = jnp.max(s, axis=0, keepdims=True)
            am = jnp.min(jnp.where(s == m, rowi, n), axis=0, keepdims=True)
            sel = rowi == am
            vals[k].append(m)
            if payloads is None:
                picks[k].append(am)
            else:
                picks[k].append(jnp.min(jnp.where(sel, payloads[k], jnp.iinfo(jnp.int32).max),
                                        axis=0, keepdims=True))
            arrays[k] = jnp.where(sel, -jnp.inf, s)
    return [(jnp.concatenate(v, axis=0), jnp.concatenate(p, axis=0)) for v, p in zip(vals, picks)]


def _candidate_cells():
    cells = [(a, b) for a in range(PEER_TOPK) for b in range(PEER_TOPK) if (a + 1) * (b + 1) <= PEER_TOPK]
    n = -(-len(cells) // 8) * 8
    sel_a = np.zeros((n, PEER_TOPK), np.float32)
    sel_b = np.zeros((n, PEER_TOPK), np.float32)
    pad = np.full((n, 1), -np.inf, np.float32)
    for r, (a, b) in enumerate(cells):
        sel_a[r, a] = 1.0
        sel_b[r, b] = 1.0
        pad[r, 0] = 0.0
    return sel_a, sel_b, pad


def _route_kernel(xn_ref, wq_ref, sk_ref, sa_ref, sb_ref, pad_ref, idx_ref, g_ref, q_scr, i_scr, g_scr):
    tr = xn_ref.shape[0]
    q = _dot(xn_ref[...].astype(BF16), wq_ref[...])
    for hc in range(2 * PEER_HEADS):
        q_scr[hc] = q[:, hc * LANES:(hc + 1) * LANES].astype(BF16)

    def sub_body(sb, carry):
        rows = pl.ds(pl.multiple_of(sb * LANES, LANES), LANES)

        def head_pair(hp, c2):
            heads = [2 * hp, 2 * hp + 1]
            scores = [_dot_nt(sk_ref[2 * h + c], q_scr[2 * h + c, rows, :])
                      for h in heads for c in range(2)]
            first = _top_k_rows(scores)
            sa, sb = sa_ref[...], sb_ref[...]
            cands, cidxs = [], []
            for k in range(len(heads)):
                (v0, i0), (v1, i1) = first[2 * k], first[2 * k + 1]
                cands.append((_dot(sa, v0, precision=lax.Precision.HIGHEST)
                              + _dot(sb, v1, precision=lax.Precision.HIGHEST)) + pad_ref[...])
                cidxs.append((_dot(sa, i0.astype(F32)) * PEER_NKEYS
                              + _dot(sb, i1.astype(F32))).astype(jnp.int32))
            for h, (best, e) in zip(heads, _top_k_rows(cands, cidxs)):
                ex = jnp.exp(best - best[0:1])
                i_scr[h] = e
                g_scr[h] = ex / jnp.sum(ex, axis=0, keepdims=True)
            return c2

        lax.fori_loop(0, PEER_HEADS // 2, head_pair, 0)
        idx_ref[rows, :] = i_scr[...].reshape(PEER_SEL, LANES).T
        g_ref[rows, :] = g_scr[...].reshape(PEER_SEL, LANES).T
        return carry

    lax.fori_loop(0, tr // LANES, sub_body, 0)


def _route(xn2, w_q, subkeys):
    n = xn2.shape[0]
    sel_a, sel_b, pad = _candidate_cells()
    tr = _row_block(n, DENSE_ROWS)
    qw = 2 * PEER_HEADS * LANES
    return pl.pallas_call(
        _route_kernel,
        grid=(n // tr,),
        in_specs=[pl.BlockSpec((tr, D_MODEL), lambda i: (i, 0)),
                  pl.BlockSpec((D_MODEL, qw), lambda i: (0, 0)),
                  pl.BlockSpec((2 * PEER_HEADS, PEER_NKEYS, LANES), lambda i: (0, 0, 0)),
                  pl.BlockSpec(sel_a.shape, lambda i: (0, 0)),
                  pl.BlockSpec(sel_b.shape, lambda i: (0, 0)),
                  pl.BlockSpec(pad.shape, lambda i: (0, 0))],
        out_specs=[pl.BlockSpec((tr, PEER_SEL), lambda i: (i, 0))] * 2,
        out_shape=[jax.ShapeDtypeStruct((n, PEER_SEL), jnp.int32),
                   jax.ShapeDtypeStruct((n, PEER_SEL), F32)],
        scratch_shapes=[pltpu.VMEM((2 * PEER_HEADS, tr, LANES), BF16),
                        pltpu.VMEM((PEER_HEADS, PEER_TOPK, LANES), jnp.int32),
                        pltpu.VMEM((PEER_HEADS, PEER_TOPK, LANES), F32)],
        compiler_params=pltpu.CompilerParams(
            dimension_semantics=("parallel",), vmem_limit_bytes=VMEM_LIMIT),
        name="peer_route",
    )(xn2, w_q, subkeys, sel_a, sel_b, pad)


def _sc_worker_id():
    return lax.axis_index("s") * SC_CORES + lax.axis_index("c")


def _sc_mesh():
    return plsc.VectorSubcoreMesh(core_axis_name="c", subcore_axis_name="s")


def _sc_gather_ring(table_hbm, idx_v, bufs, sems, unit, compute):
    def gather(g, k):
        off = pl.multiple_of(g * SC_ROWS, SC_ROWS)
        return pltpu.make_async_copy(table_hbm.at[idx_v.at[pl.ds(off, SC_ROWS)]], bufs[k], sems[k])

    for k in range(SC_NBUF):
        gather(k, k).start()

    units_per_iter = SC_NBUF // unit

    def ring_iter(i, carry):
        for m in range(units_per_iter):
            ks = range(m * unit, (m + 1) * unit)
            for k in ks:
                gather(i * SC_NBUF + k, k).wait()
            compute(i * units_per_iter + m, [bufs[k] for k in ks])
            for k in ks:
                g_next = (i + 1) * SC_NBUF + k

                @pl.when(g_next < SC_GATHERS)
                def _():
                    gather(g_next, k).start()
        return carry

    lax.fori_loop(0, SC_GATHERS // SC_NBUF, ring_iter, 0)


def _bf16_bits(x):
    b = lax.bitcast_convert_type(x, jnp.uint32)
    return (b + jnp.uint32(0x7FFF) + ((b >> 16) & jnp.uint32(1))) >> 16


def _pack_pairs(lo, hi):
    return _bf16_bits(lo) | (_bf16_bits(hi) << 16)


def _pack_table(t):
    return _pack_pairs(t[:, :HALF], t[:, HALF:]).reshape(t.shape[0], ROW_TILES, LANES)


def _mul_add_words(w1, m1, w2, m2):
    bf = lambda a: plsc.bitcast(a, jnp.bfloat16)
    p = plsc.bitcast(bf(w1) * bf(m1) + bf(w2) * bf(m2), jnp.uint32)
    return lax.bitcast_convert_type(p << 16, F32), lax.bitcast_convert_type(p, F32)


def _sc_scratch(staged_cols, out_cols):
    return ([pltpu.VMEM((SC_TOK_BATCH * PEER_SEL,), jnp.int32),
             pltpu.VMEM((SC_TOK_BATCH, staged_cols), jnp.uint32),
             pltpu.VMEM((SC_TOK_BATCH, out_cols), F32)]
            + [pltpu.VMEM((SC_ROWS, ROW_TILES, LANES), jnp.uint32) for _ in range(SC_NBUF)]
            + [pltpu.SemaphoreType.DMA for _ in range(SC_NBUF)])


def _sc_batches(tpw, idx_hbm, staged_hbm, out_hbm, idx_v, staged_v, out_v, run):
    tok0 = _sc_worker_id() * tpw

    def batch(tb, carry):
        t0 = pl.multiple_of(tok0 + tb * SC_TOK_BATCH, SC_TOK_BATCH)
        pltpu.sync_copy(idx_hbm.at[pl.ds(pl.multiple_of(t0 * PEER_SEL, SC_TOK_BATCH * PEER_SEL),
                                         SC_TOK_BATCH * PEER_SEL)], idx_v)
        pltpu.sync_copy(staged_hbm.at[pl.ds(t0, SC_TOK_BATCH)], staged_v)
        run()
        pltpu.sync_copy(out_v, out_hbm.at[pl.ds(t0, SC_TOK_BATCH)])
        return carry

    lax.fori_loop(0, tpw // SC_TOK_BATCH, batch, 0)


def _peer_dot_sc(u3, idx_flat, xn2):
    t = xn2.shape[0]
    unit_rows = 16
    unit = unit_rows // SC_ROWS
    units_per_tok = PEER_SEL // unit_rows

    @functools.partial(
        pl.kernel,
        mesh=_sc_mesh(),
        out_type=jax.ShapeDtypeStruct((t, PEER_SEL * SC_LANES), F32),
        scratch_types=_sc_scratch(HALF, PEER_SEL * SC_LANES),
        compiler_params=pltpu.CompilerParams(needs_layout_passes=False),
        name="peer_dot",
    )
    def body(u_hbm, idx_hbm, x_hbm, out_hbm, idx_v, x_v, out_v, *rest):
        bufs, sems = rest[:SC_NBUF], rest[SC_NBUF:]

        def compute(u, ub):
            tl = u // units_per_tok
            rb = (u % units_per_tok) * unit_rows

            def cbody(c, acc):
                acc = list(acc)
                c4 = c // (LANES // (2 * SC_LANES))
                lane0 = pl.multiple_of((c % (LANES // (2 * SC_LANES))) * 2 * SC_LANES, 2 * SC_LANES)
                col = pl.multiple_of(c * 2 * SC_LANES, 2 * SC_LANES)
                xw1 = x_v[tl, pl.ds(col, SC_LANES)]
                xw2 = x_v[tl, pl.ds(col + SC_LANES, SC_LANES)]
                for q, buf in enumerate(ub):
                    for r in range(SC_ROWS):
                        a = q * SC_ROWS + r
                        lo, hi = _mul_add_words(buf[r, c4, pl.ds(lane0, SC_LANES)], xw1,
                                                buf[r, c4, pl.ds(lane0 + SC_LANES, SC_LANES)], xw2)
                        acc[a] = acc[a] + lo + hi
                return tuple(acc)

            acc = lax.fori_loop(0, HALF // (2 * SC_LANES), cbody,
                                tuple(jnp.zeros((SC_LANES,), F32) for _ in range(unit_rows)))
            for a in range(unit_rows):
                o = pl.multiple_of((rb + a) * SC_LANES, SC_LANES)
                out_v[tl, pl.ds(o, SC_LANES)] = acc[a]

        _sc_batches(t // SC_WORKERS, idx_hbm, x_hbm, out_hbm, idx_v, x_v, out_v,
                    lambda: _sc_gather_ring(u_hbm, idx_v, bufs, sems, unit, compute))

    return body(u3, idx_flat, xn2)


def _peer_combine_sc(v3, idx_flat, coef_b):
    t = coef_b.shape[0]
    unit = 2
    units_per_tok = PEER_SEL // (unit * SC_ROWS)
    nvec = LANES // SC_LANES

    @functools.partial(
        pl.kernel,
        mesh=_sc_mesh(),
        out_type=jax.ShapeDtypeStruct((t, D_MODEL), F32),
        scratch_types=_sc_scratch(PEER_SEL * SC_LANES, D_MODEL),
        compiler_params=pltpu.CompilerParams(needs_layout_passes=False),
        name="peer_combine",
    )
    def body(v_hbm, idx_hbm, coef_hbm, out_hbm, idx_v, coef_v, out_v, *rest):
        bufs, sems = rest[:SC_NBUF], rest[SC_NBUF:]

        def compute(u, ub):
            tl = u // units_per_tok
            rb = (u % units_per_tok) * unit * SC_ROWS
            for g in range(ROW_TILES):
                cols = ([pl.ds(g * LANES + l * SC_LANES, SC_LANES) for l in range(nvec)]
                        + [pl.ds(HALF + g * LANES + l * SC_LANES, SC_LANES) for l in range(nvec)])

                def rbody(r, acc):
                    acc = list(acc)
                    co = pl.multiple_of((rb + r) * SC_LANES, SC_LANES)
                    cf1 = coef_v[tl, pl.ds(co, SC_LANES)]
                    cf2 = coef_v[tl, pl.ds(co + SC_ROWS * SC_LANES, SC_LANES)]
                    for l in range(nvec):
                        lo, hi = _mul_add_words(ub[0][r, g, pl.ds(l * SC_LANES, SC_LANES)], cf1,
                                                ub[1][r, g, pl.ds(l * SC_LANES, SC_LANES)], cf2)
                        acc[l] = acc[l] + lo
                        acc[nvec + l] = acc[nvec + l] + hi
                    return tuple(acc)

                acc = lax.fori_loop(0, SC_ROWS, rbody, tuple(out_v[tl, c] for c in cols))
                for k, c in enumerate(cols):
                    out_v[tl, c] = acc[k]

        def run():
            def zero(tl, c2):
                for k in range(D_MODEL // SC_LANES):
                    out_v[tl, pl.ds(k * SC_LANES, SC_LANES)] = jnp.zeros((SC_LANES,), F32)
                return c2

            lax.fori_loop(0, SC_TOK_BATCH, zero, 0)
            _sc_gather_ring(v_hbm, idx_v, bufs, sems, unit, compute)

        _sc_batches(t // SC_WORKERS, idx_hbm, coef_hbm, out_hbm, idx_v, coef_v, out_v, run)

    return body(v3, idx_flat, coef_b)


def _coef_kernel(part_ref, g_ref, e_ref, coef_ref):
    act = lax.dot_general(part_ref[...], e_ref[...], (((1,), (1,)), ((), ())),
                          precision=lax.Precision.HIGHEST, preferred_element_type=F32)
    gelu = 0.5 * act * (1.0 + lax.erf(act * np.float32(math.sqrt(0.5))))
    c = _dot(g_ref[...] * gelu, e_ref[...], precision=lax.Precision.HIGHEST)
    coef_ref[...] = _pack_pairs(c, c)


def _coef(part, g):
    n = part.shape[0]
    tr = _row_block(n, DENSE_ROWS)
    w = PEER_SEL * SC_LANES
    expand = (jnp.arange(w, dtype=jnp.int32)[None, :] // SC_LANES
              == jnp.arange(PEER_SEL, dtype=jnp.int32)[:, None]).astype(F32)
    wide = pl.BlockSpec((tr, w), lambda i: (i, 0))
    return pl.pallas_call(
        _coef_kernel,
        grid=(n // tr,),
        in_specs=[wide, pl.BlockSpec((tr, PEER_SEL), lambda i: (i, 0)),
                  pl.BlockSpec((PEER_SEL, w), lambda i: (0, 0))],
        out_specs=wide,
        out_shape=jax.ShapeDtypeStruct((n, w), jnp.uint32),
        compiler_params=pltpu.CompilerParams(
            dimension_semantics=("parallel",), vmem_limit_bytes=VMEM_LIMIT),
        name="peer_coef",
    )(part, g, expand)


def _final_kernel(h_ref, p_ref, g_ref, full_ref, o_ref):
    del full_ref
    o_ref[...] = _rms(h_ref[...] + p_ref[...], g_ref[...])


def _final(h2, peer_out, gain, out_full, row0):
    n = h2.shape[0]
    tm = math.gcd(_row_block(n, 2 * DENSE_ROWS), row0) if row0 else _row_block(n, 2 * DENSE_ROWS)
    assert tm % 8 == 0
    blk = pl.BlockSpec((tm, D_MODEL), lambda i: (i, 0))
    return pl.pallas_call(
        _final_kernel,
        grid=(n // tm,),
        in_specs=[blk, blk, pl.BlockSpec((1, D_MODEL), lambda i: (0, 0)),
                  pl.BlockSpec(memory_space=pl.ANY)],
        out_specs=pl.BlockSpec((tm, D_MODEL), lambda i: (row0 // tm + i, 0)),
        out_shape=jax.ShapeDtypeStruct(out_full.shape, F32),
        input_output_aliases={3: 0},
        compiler_params=pltpu.CompilerParams(
            dimension_semantics=("parallel",), vmem_limit_bytes=VMEM_LIMIT),
        name="final_norm",
    )(h2, peer_out, gain, out_full)


def _pad_heads(w, dk):
    lead = w.shape[:-1]
    w = w.reshape(lead + (HEADS, dk))
    w = jnp.pad(w, [(0, 0)] * len(lead) + [(0, 0), (0, LANES - dk)])
    return w.reshape(lead + (HEADS * LANES,))


def _rot_heads(w, dk):
    lead = w.shape[:-1]
    w = w.reshape(lead + (HEADS, 2, dk // 2))
    w = jnp.stack([-w[..., 1, :], w[..., 0, :]], axis=-2)
    return w.reshape(lead + (HEADS * dk,))


def _pack_in_weights(w_in):
    gq, gk, gv, gg, ga, rq, rk, rv, rg = jnp.split(w_in, np.cumsum(IN_SPLITS)[:-1].tolist(), axis=-1)
    ga = jnp.pad(ga, ((0, 0), (0, 2 * LANES - GATE_RANK)))
    return jnp.concatenate(
        [_pad_heads(gq, DK), _pad_heads(gk, DK), gv, gg,
         _pad_heads(rq, DK), _pad_heads(rk, DK), rv, rg,
         _pad_heads(_rot_heads(rq, DK), DK), _pad_heads(_rot_heads(rk, DK), DK), ga],
        axis=-1).astype(BF16)


def _rope_tables(pos):
    inv_freq = ROPE_BASE ** (-jnp.arange(0, DK, 2, dtype=F32) / DK)
    ang = pos[:, None] * inv_freq[None, :]
    full = lambda t: _pad_heads(jnp.tile(t, (1, 2 * HEADS)), DK)
    return full(jnp.cos(ang)), full(jnp.sin(ang))


def _segment_lengths(s, seq):
    if seq < len(SEQ_RAMPS):
        ramp = SEQ_RAMPS[seq]
        lengths = [s * k // sum(ramp) for k in ramp]
        assert sum(lengths) == s
    else:
        lengths = [s // SEGS_PER_SEQ] * SEGS_PER_SEQ
    assert all(n % SC_CALL_TOKENS == 0 for n in lengths)
    return lengths


def kernel(x, meta_tokens, norm1_gain, w_in, gla_gate_w2, gla_gate_b, gla_norm_gain, ret_norm_gain, ret_norm_bias, w_out, norm2_gain, peer_w_q, peer_subkeys, peer_u, peer_v, final_gain):
    b, s, d = x.shape
    assert d == D_MODEL and (s // SEGS_PER_SEQ) % SC_CALL_TOKENS == 0
    assert norm1_gain.shape[0] == 1, "single-layer block"
    x2d = x.reshape(b * s, d)

    w_all = _pack_in_weights(w_in[0])
    w2p = jnp.pad(_pad_heads(gla_gate_w2[0], DK), ((0, LANES - GATE_RANK), (0, 0))).astype(BF16)
    gbp = _pad_heads(gla_gate_b[0][None, :], DK)
    g1 = norm1_gain[0][None, :]
    ggain = gla_norm_gain[0][None, :]
    rgain = ret_norm_gain[0][None, :]
    rbias = ret_norm_bias[0][None, :]

    h0 = jnp.concatenate([jnp.zeros((N_PAD, d), x.dtype), meta_tokens.astype(x.dtype)], axis=0)
    cos0, sin0 = _rope_tables(jnp.arange(CHUNK, dtype=F32) - N_PAD)
    cos1, sin1 = _rope_tables(jnp.arange(s, dtype=F32) + N_META)
    zero_state = jnp.zeros((HEADS, DV, LANES), F32)
    _, s_meta, r_meta = _mixer(h0, g1, w_all, cos0, sin0, w2p, gbp, ggain, rgain, rbias,
                               zero_state, zero_state, batch=1, n_invalid=N_PAD)

    w_out_b = w_out[0].astype(BF16)
    g2 = norm2_gain[0][None, :]
    w_q = peer_w_q[0].astype(BF16)
    subkeys = peer_subkeys[0].reshape(2 * PEER_HEADS, PEER_NKEYS, PEER_DKEY // 2).astype(BF16)
    u3 = _pack_table(peer_u[0])
    v3 = _pack_table(peer_v[0])
    gf = final_gain[None, :]

    out = jnp.zeros((b * s, d), F32)
    prev_g = None
    pending = []
    waiting = None
    for bi in range(b):
        s_state, r_state = s_meta, r_meta
        start = 0
        for seg in _segment_lengths(s, bi):
            row0 = bi * s + start
            xs = x2d[row0:row0 + seg]
            pos = slice(start, start + seg)
            start += seg
            if prev_g is not None:
                xs, _ = lax.optimization_barrier((xs, prev_g))
            o, s_state, r_state = _mixer(xs, g1, w_all, cos1[pos], sin1[pos], w2p, gbp,
                                         ggain, rgain, rbias, s_state, r_state, batch=1, n_invalid=0)
            h2, xn2, xn2_pairs = _out_proj(o, xs, w_out_b, g2)
            idx, g = _route(xn2, w_q, subkeys)
            prev_g = g
            idx_flat = idx.reshape(-1)
            if pending:
                h2_old, row_old, peer_old = pending.pop(0)
                idx_flat, peer_old = lax.optimization_barrier((idx_flat, peer_old))
                out = _final(h2_old, peer_old, gf, out, row_old)
            part = _peer_dot_sc(u3, idx_flat, xn2_pairs)
            if waiting is not None:
                h2_w, row_w, idx_w, part_w, g_w = waiting
                part_w, _ = lax.optimization_barrier((part_w, g))
                pending.append((h2_w, row_w, _peer_combine_sc(v3, idx_w, _coef(part_w, g_w))))
            waiting = (h2, row0, idx_flat, part, g)
    h2_w, row_w, idx_w, part_w, g_w = waiting
    pending.append((h2_w, row_w, _peer_combine_sc(v3, idx_w, _coef(part_w, g_w))))
    for h2_old, row_old, peer_old in pending:
        out = _final(h2_old, peer_old, gf, out, row_old)
    return out.reshape(b, s, d)
```

```python
import functools
import math

import jax
import jax.numpy as jnp
import numpy as np
from jax import lax
from jax.experimental import pallas as pl
from jax.experimental.pallas import tpu as pltpu
from jax.experimental.pallas import tpu_sc as plsc

F32 = jnp.float32
BF16 = jnp.bfloat16

D_MODEL = 1024
N_META = 16
CHUNK = 64
N_PAD = CHUNK - N_META
EPS = 1e-6

HEADS = 4
DV = 128
DK = 64
LANES = 128
GATE_RANK = 16
GATE_TAU = 16.0
ROPE_BASE = 10000.0
IN_SPLITS = (HEADS * DK, HEADS * DK, HEADS * DV, HEADS * DV, GATE_RANK,
             HEADS * DK, HEADS * DK, HEADS * DV, HEADS * DV)

PEER_HEADS = 8
PEER_NKEYS = 128
PEER_DKEY = 256
PEER_TOPK = 16
PEER_SEL = PEER_HEADS * PEER_TOPK

HW = HEADS * LANES
OFF_GQ, OFF_GK, OFF_GV, OFF_GG = 0, HW, 2 * HW, 3 * HW
OFF_RQ, OFF_RK, OFF_RV, OFF_RG = 4 * HW, 5 * HW, 6 * HW, 7 * HW
OFF_RQR, OFF_RKR, OFF_GA = 8 * HW, 9 * HW, 10 * HW
PROJ_W = 10 * HW + 2 * LANES
PROJ_TN = PROJ_W // 3

LOG_GAMMA = tuple(math.log(1.0 - 2.0 ** (-5.0 - h)) for h in range(HEADS))

SC_CORES, SC_SUBCORES, SC_LANES = 2, 16, 16
SC_WORKERS = SC_CORES * SC_SUBCORES
SC_TOK_BATCH = 16
SC_ROWS = 16
SC_NBUF = 8
SC_GATHERS = SC_TOK_BATCH * PEER_SEL // SC_ROWS
HALF = D_MODEL // 2
ROW_TILES = HALF // LANES

SC_CALL_TOKENS = SC_WORKERS * SC_TOK_BATCH
SEGS_PER_SEQ = 2
SEQ_RAMPS = ((1, 1, 2, 2, 3, 3, 4), (5, 5, 6), (7, 9))

MIXER_ROWS = 256
DENSE_ROWS = 512
VMEM_LIMIT = 48 * 1024 * 1024


def _row_block(n, target):
    blk = math.gcd(n, target)
    assert blk % 8 == 0 or blk == n, (n, target)
    return blk


def _dot(a, b, **kw):
    return jnp.dot(a, b, preferred_element_type=F32, **kw)


def _dot_nt(a, b):
    return lax.dot_general(a, b, (((1,), (1,)), ((), ())), preferred_element_type=F32)


def _dot_tn(a, b):
    return lax.dot_general(a, b, (((0,), (0,)), ((), ())), preferred_element_type=F32)


def _rms(x, g):
    return x * lax.rsqrt(jnp.mean(x * x, axis=-1, keepdims=True) + EPS) * g


def _silu(x):
    return x * jax.nn.sigmoid(x)


def _mixer_kernel(x_ref, g1_ref, wall_ref, cos_ref, sin_ref, w2_ref, gb_ref, ggain_ref, rgain_ref,
                  rbias_ref, s0_ref, r0_ref, o_ref, s_out_ref, r_out_ref, s_scr, r_scr, proj_ref,
                  *, n_invalid, rows_blk):
    j = pl.program_id(1)
    xn = _rms(x_ref[...], g1_ref[...]).astype(BF16)
    for jn in range(PROJ_W // PROJ_TN):
        cols = slice(jn * PROJ_TN, (jn + 1) * PROJ_TN)
        proj_ref[:, cols] = _dot(xn, wall_ref[:, cols])

    @pl.when(j == 0)
    def _():
        s_scr[...] = s0_ref[...]
        r_scr[...] = r0_ref[...]

    row = lax.broadcasted_iota(jnp.int32, (CHUNK, CHUNK), 0)
    col = lax.broadcasted_iota(jnp.int32, (CHUNK, CHUNK), 1)
    causal = row >= col
    tril = causal.astype(F32)
    diff = jnp.maximum(row - col, 0).astype(F32)
    rowf = lax.broadcasted_iota(jnp.int32, (CHUNK, LANES), 0).astype(F32)
    row1 = lax.broadcasted_iota(jnp.int32, (CHUNK, 1), 0)

    def chunk_body(c, carry):
        r0 = pl.multiple_of(c * CHUNK, CHUNK)
        rows = pl.ds(r0, CHUNK)
        valid = ((j * rows_blk + r0 + row1) >= n_invalid).astype(F32)

        ga = proj_ref[rows, OFF_GA:OFF_GA + LANES].astype(BF16)
        pre = _dot(ga, w2_ref[...]) + gb_ref[...]
        log_a = jax.nn.log_sigmoid(pre) * (1.0 / GATE_TAU)
        bcum = _dot(tril, log_a, precision=lax.Precision.HIGHEST)
        b_last = bcum[CHUNK - 1:CHUNK, :]
        gq = proj_ref[rows, OFF_GQ:OFF_GQ + HW] * (DK ** -0.5)
        gk = proj_ref[rows, OFF_GK:OFF_GK + HW] * valid
        q_t = (gq * jnp.exp(bcum)).astype(BF16)
        k_t = (gk * jnp.exp(-bcum)).astype(BF16)
        k_end = (gk * jnp.exp(b_last - bcum)).astype(BF16)
        decay = jnp.exp(b_last)
        for h in range(HEADS):
            sl = slice(h * LANES, (h + 1) * LANES)
            v = proj_ref[rows, OFF_GV + h * DV:OFF_GV + (h + 1) * DV].astype(BF16)
            att = jnp.where(causal, _dot_nt(q_t[:, sl], k_t[:, sl]), 0.0).astype(BF16)
            s_prev = s_scr[h]
            o = _dot(att, v) + _dot_nt(q_t[:, sl], s_prev.astype(BF16))
            s_scr[h] = s_prev * decay[:, sl] + _dot_tn(v, k_end[:, sl])
            o = o * lax.rsqrt(jnp.mean(o * o, axis=-1, keepdims=True) + EPS) * ggain_ref[...]
            gate = proj_ref[rows, OFF_GG + h * DV:OFF_GG + (h + 1) * DV]
            o_ref[rows, h * DV:(h + 1) * DV] = (o * _silu(gate)).astype(o_ref.dtype)

        cosb = cos_ref[rows, :]
        sinb = sin_ref[rows, :]
        rq = proj_ref[rows, OFF_RQ:OFF_RQ + HW]
        rqr = proj_ref[rows, OFF_RQR:OFF_RQR + HW]
        rk = proj_ref[rows, OFF_RK:OFF_RK + HW] * valid
        rkr = proj_ref[rows, OFF_RKR:OFF_RKR + HW] * valid
        q_r = (rq * cosb + rqr * sinb).astype(BF16)
        k_r = (rk * cosb + rkr * sinb) * (DK ** -0.5)
        for h in range(HEADS):
            sl = slice(h * LANES, (h + 1) * LANES)
            lg = LOG_GAMMA[h]
            dmat = jnp.where(causal, jnp.exp(lg * diff), 0.0)
            xi = jnp.exp(lg * (rowf + 1.0))
            zeta = jnp.exp(lg * (CHUNK - 1.0 - rowf))
            v = proj_ref[rows, OFF_RV + h * DV:OFF_RV + (h + 1) * DV].astype(BF16)
            k_h = k_r[:, sl]
            att = (_dot_nt(q_r[:, sl], k_h.astype(BF16)) * dmat).astype(BF16)
            r_prev = r_scr[h]
            o = _dot(att, v) + _dot_nt(q_r[:, sl], r_prev.astype(BF16)) * xi
            r_scr[h] = r_prev * math.exp(lg * CHUNK) + _dot_tn(v, (k_h * zeta).astype(BF16))
            mu = jnp.mean(o, axis=-1, keepdims=True)
            var = jnp.mean(jnp.square(o - mu), axis=-1, keepdims=True)
            o = (o - mu) * lax.rsqrt(var + EPS) * rgain_ref[...] + rbias_ref[...]
            gate = proj_ref[rows, OFF_RG + h * DV:OFF_RG + (h + 1) * DV]
            o_ref[rows, HW + h * DV:HW + (h + 1) * DV] = (o * _silu(gate)).astype(o_ref.dtype)
        return carry

    lax.fori_loop(0, rows_blk // CHUNK, chunk_body, 0)
    s_out_ref[...] = s_scr[...]
    r_out_ref[...] = r_scr[...]


def _mixer(x2d, g1, w_all, cosf, sinf, w2p, gbp, ggain, rgain, rbias, s0, r0, *, batch, n_invalid):
    rows = x2d.shape[0] // batch
    rows_blk = _row_block(rows, MIXER_ROWS)
    nblk = rows // rows_blk
    const2 = lambda b, j: (0, 0)
    const3 = lambda b, j: (0, 0, 0)
    state = jax.ShapeDtypeStruct((HEADS, DV, LANES), F32)
    return pl.pallas_call(
        functools.partial(_mixer_kernel, n_invalid=n_invalid, rows_blk=rows_blk),
        grid=(batch, nblk),
        in_specs=[
            pl.BlockSpec((rows_blk, D_MODEL), lambda b, j: (b * nblk + j, 0)),
            pl.BlockSpec((1, D_MODEL), const2),
            pl.BlockSpec((D_MODEL, PROJ_W), const2),
            pl.BlockSpec((rows_blk, HW), lambda b, j: (j, 0)),
            pl.BlockSpec((rows_blk, HW), lambda b, j: (j, 0)),
            pl.BlockSpec((LANES, HW), const2),
            pl.BlockSpec((1, HW), const2),
            pl.BlockSpec((1, DV), const2),
            pl.BlockSpec((1, DV), const2),
            pl.BlockSpec((1, DV), const2),
            pl.BlockSpec((HEADS, DV, LANES), const3),
            pl.BlockSpec((HEADS, DV, LANES), const3),
        ],
        out_specs=[
            pl.BlockSpec((rows_blk, D_MODEL), lambda b, j: (b * nblk + j, 0)),
            pl.BlockSpec((HEADS, DV, LANES), const3),
            pl.BlockSpec((HEADS, DV, LANES), const3),
        ],
        out_shape=[jax.ShapeDtypeStruct((batch * rows, D_MODEL), BF16), state, state],
        scratch_shapes=[pltpu.VMEM((HEADS, DV, LANES), F32), pltpu.VMEM((HEADS, DV, LANES), F32),
                        pltpu.VMEM((rows_blk, PROJ_W), F32)],
        compiler_params=pltpu.CompilerParams(
            dimension_semantics=("arbitrary", "arbitrary"), vmem_limit_bytes=VMEM_LIMIT),
        name="token_mixers",
    )(x2d, g1, w_all, cosf, sinf, w2p, gbp, ggain, rgain, rbias, s0, r0)


def _out_proj_kernel(o_ref, x_ref, w_ref, g_ref, h_ref, xn_ref, xp_ref):
    h = x_ref[...] + _dot(o_ref[...], w_ref[...])
    h_ref[...] = h
    xn = _rms(h, g_ref[...])
    xn_ref[...] = xn
    xp_ref[...] = _pack_pairs(xn[:, :HALF], xn[:, HALF:])


def _out_proj(o, x2d, w_out, gain2):
    n = x2d.shape[0]
    tm = _row_block(n, DENSE_ROWS)
    blk = pl.BlockSpec((tm, D_MODEL), lambda i: (i, 0))
    return pl.pallas_call(
        _out_proj_kernel,
        grid=(n // tm,),
        in_specs=[blk, blk, pl.BlockSpec((D_MODEL, D_MODEL), lambda i: (0, 0)),
                  pl.BlockSpec((1, D_MODEL), lambda i: (0, 0))],
        out_specs=[blk, blk, pl.BlockSpec((tm, HALF), lambda i: (i, 0))],
        out_shape=[jax.ShapeDtypeStruct((n, D_MODEL), F32)] * 2 + [jax.ShapeDtypeStruct((n, HALF), jnp.uint32)],
        compiler_params=pltpu.CompilerParams(
            dimension_semantics=("parallel",), vmem_limit_bytes=VMEM_LIMIT),
        name="out_proj_norm",
    )(o, x2d, w_out, gain2)


def _top_k_rows(arrays, payloads=None):
    arrays = list(arrays)
    n = arrays[0].shape[0]
    rowi = lax.broadcasted_iota(jnp.int32, arrays[0].shape, 0)
    vals = [[] for _ in arrays]
    picks = [[] for _ in arrays]
    for _ in range(PEER_TOPK):
        for k, s in enumerate(arrays):
            m = jnp.max(s, axis=0, keepdims=True)
            am = jnp.min(jnp.where(s == m, rowi, n), axis=0, keepdims=True)
            sel = rowi == am
            vals[k].append(m)
            if payloads is None:
                picks[k].append(am)
            else:
                picks[k].append(jnp.min(jnp.where(sel, payloads[k], jnp.iinfo(jnp.int32).max),
                                        axis=0, keepdims=True))
            arrays[k] = jnp.where(sel, -jnp.inf, s)
    return [(jnp.concatenate(v, axis=0), jnp.concatenate(p, axis=0)) for v, p in zip(vals, picks)]


def _candidate_cells():
    cells = [(a, b) for a in range(PEER_TOPK) for b in range(PEER_TOPK) if (a + 1) * (b + 1) <= PEER_TOPK]
    n = -(-len(cells) // 8) * 8
    sel_a = np.zeros((n, PEER_TOPK), np.float32)
    sel_b = np.zeros((n, PEER_TOPK), np.float32)
    pad = np.full((n, 1), -np.inf, np.float32)
    for r, (a, b) in enumerate(cells):
        sel_a[r, a] = 1.0
        sel_b[r, b] = 1.0
        pad[r, 0] = 0.0
    return sel_a, sel_b, pad


def _route_kernel(xn_ref, wq_ref, sk_ref, sa_ref, sb_ref, pad_ref, idx_ref, g_ref, q_scr, i_scr, g_scr):
    tr = xn_ref.shape[0]
    q = _dot(xn_ref[...].astype(BF16), wq_ref[...])
    for hc in range(2 * PEER_HEADS):
        q_scr[hc] = q[:, hc * LANES:(hc + 1) * LANES].astype(BF16)

    def sub_body(sb, carry):
        rows = pl.ds(pl.multiple_of(sb * LANES, LANES), LANES)

        def head_pair(hp, c2):
            heads = [2 * hp, 2 * hp + 1]
            scores = [_dot_nt(sk_ref[2 * h + c], q_scr[2 * h + c, rows, :])
                      for h in heads for c in range(2)]
            first = _top_k_rows(scores)
            sa, sb = sa_ref[...], sb_ref[...]
            cands, cidxs = [], []
            for k in range(len(heads)):
                (v0, i0), (v1, i1) = first[2 * k], first[2 * k + 1]
                cands.append((_dot(sa, v0, precision=lax.Precision.HIGHEST)
                              + _dot(sb, v1, precision=lax.Precision.HIGHEST)) + pad_ref[...])
                cidxs.append((_dot(sa, i0.astype(F32)) * PEER_NKEYS
                              + _dot(sb, i1.astype(F32))).astype(jnp.int32))
            for h, (best, e) in zip(heads, _top_k_rows(cands, cidxs)):
                ex = jnp.exp(best - best[0:1])
                i_scr[h] = e
                g_scr[h] = ex / jnp.sum(ex, axis=0, keepdims=True)
            return c2

        lax.fori_loop(0, PEER_HEADS // 2, head_pair, 0)
        idx_ref[rows, :] = i_scr[...].reshape(PEER_SEL, LANES).T
        g_ref[rows, :] = g_scr[...].reshape(PEER_SEL, LANES).T
        return carry

    lax.fori_loop(0, tr // LANES, sub_body, 0)


def _route(xn2, w_q, subkeys):
    n = xn2.shape[0]
    sel_a, sel_b, pad = _candidate_cells()
    tr = _row_block(n, DENSE_ROWS)
    qw = 2 * PEER_HEADS * LANES
    return pl.pallas_call(
        _route_kernel,
        grid=(n // tr,),
        in_specs=[pl.BlockSpec((tr, D_MODEL), lambda i: (i, 0)),
                  pl.BlockSpec((D_MODEL, qw), lambda i: (0, 0)),
                  pl.BlockSpec((2 * PEER_HEADS, PEER_NKEYS, LANES), lambda i: (0, 0, 0)),
                  pl.BlockSpec(sel_a.shape, lambda i: (0, 0)),
                  pl.BlockSpec(sel_b.shape, lambda i: (0, 0)),
                  pl.BlockSpec(pad.shape, lambda i: (0, 0))],
        out_specs=[pl.BlockSpec((tr, PEER_SEL), lambda i: (i, 0))] * 2,
        out_shape=[jax.ShapeDtypeStruct((n, PEER_SEL), jnp.int32),
                   jax.ShapeDtypeStruct((n, PEER_SEL), F32)],
        scratch_shapes=[pltpu.VMEM((2 * PEER_HEADS, tr, LANES), BF16),
                        pltpu.VMEM((PEER_HEADS, PEER_TOPK, LANES), jnp.int32),
                        pltpu.VMEM((PEER_HEADS, PEER_TOPK, LANES), F32)],
        compiler_params=pltpu.CompilerParams(
            dimension_semantics=("parallel",), vmem_limit_bytes=VMEM_LIMIT),
        name="peer_route",
    )(xn2, w_q, subkeys, sel_a, sel_b, pad)


def _sc_worker_id():
    return lax.axis_index("s") * SC_CORES + lax.axis_index("c")


def _sc_mesh():
    return plsc.VectorSubcoreMesh(core_axis_name="c", subcore_axis_name="s")


def _sc_gather_ring(table_hbm, idx_v, bufs, sems, unit, compute):
    def gather(g, k):
        off = pl.multiple_of(g * SC_ROWS, SC_ROWS)
        return pltpu.make_async_copy(table_hbm.at[idx_v.at[pl.ds(off, SC_ROWS)]], bufs[k], sems[k])

    for k in range(SC_NBUF):
        gather(k, k).start()

    units_per_iter = SC_NBUF // unit

    def ring_iter(i, carry):
        for m in range(units_per_iter):
            ks = range(m * unit, (m + 1) * unit)
            for k in ks:
                gather(i * SC_NBUF + k, k).wait()
            compute(i * units_per_iter + m, [bufs[k] for k in ks])
            for k in ks:
                g_next = (i + 1) * SC_NBUF + k

                @pl.when(g_next < SC_GATHERS)
                def _():
                    gather(g_next, k).start()
        return carry

    lax.fori_loop(0, SC_GATHERS // SC_NBUF, ring_iter, 0)


def _bf16_bits(x):
    b = lax.bitcast_convert_type(x, jnp.uint32)
    return (b + jnp.uint32(0x7FFF) + ((b >> 16) & jnp.uint32(1))) >> 16


def _pack_pairs(lo, hi):
    return _bf16_bits(lo) | (_bf16_bits(hi) << 16)


def _pack_table(t):
    return _pack_pairs(t[:, :HALF], t[:, HALF:]).reshape(t.shape[0], ROW_TILES, LANES)


def _mul_add_words(w1, m1, w2, m2):
    bf = lambda a: plsc.bitcast(a, jnp.bfloat16)
    p = plsc.bitcast(bf(w1) * bf(m1) + bf(w2) * bf(m2), jnp.uint32)
    return lax.bitcast_convert_type(p << 16, F32), lax.bitcast_convert_type(p, F32)


def _sc_scratch(staged_cols, out_cols):
    return ([pltpu.VMEM((SC_TOK_BATCH * PEER_SEL,), jnp.int32),
             pltpu.VMEM((SC_TOK_BATCH, staged_cols), jnp.uint32),
             pltpu.VMEM((SC_TOK_BATCH, out_cols), F32)]
            + [pltpu.VMEM((SC_ROWS, ROW_TILES, LANES), jnp.uint32) for _ in range(SC_NBUF)]
            + [pltpu.SemaphoreType.DMA for _ in range(SC_NBUF)])


def _sc_batches(tpw, idx_hbm, staged_hbm, out_hbm, idx_v, staged_v, out_v, run):
    tok0 = _sc_worker_id() * tpw

    def batch(tb, carry):
        t0 = pl.multiple_of(tok0 + tb * SC_TOK_BATCH, SC_TOK_BATCH)
        pltpu.sync_copy(idx_hbm.at[pl.ds(pl.multiple_of(t0 * PEER_SEL, SC_TOK_BATCH * PEER_SEL),
                                         SC_TOK_BATCH * PEER_SEL)], idx_v)
        pltpu.sync_copy(staged_hbm.at[pl.ds(t0, SC_TOK_BATCH)], staged_v)
        run()
        pltpu.sync_copy(out_v, out_hbm.at[pl.ds(t0, SC_TOK_BATCH)])
        return carry

    lax.fori_loop(0, tpw // SC_TOK_BATCH, batch, 0)


def _peer_dot_sc(u3, idx_flat, xn2):
    t = xn2.shape[0]
    unit_rows = 16
    unit = unit_rows // SC_ROWS
    units_per_tok = PEER_SEL // unit_rows

    @functools.partial(
        pl.kernel,
        mesh=_sc_mesh(),
        out_type=jax.ShapeDtypeStruct((t, PEER_SEL * SC_LANES), F32),
        scratch_types=_sc_scratch(HALF, PEER_SEL * SC_LANES),
        compiler_params=pltpu.CompilerParams(needs_layout_passes=False),
        name="peer_dot",
    )
    def body(u_hbm, idx_hbm, x_hbm, out_hbm, idx_v, x_v, out_v, *rest):
        bufs, sems = rest[:SC_NBUF], rest[SC_NBUF:]

        def compute(u, ub):
            tl = u // units_per_tok
            rb = (u % units_per_tok) * unit_rows

            def cbody(c, acc):
                acc = list(acc)
                c4 = c // (LANES // (2 * SC_LANES))
                lane0 = pl.multiple_of((c % (LANES // (2 * SC_LANES))) * 2 * SC_LANES, 2 * SC_LANES)
                col = pl.multiple_of(c * 2 * SC_LANES, 2 * SC_LANES)
                xw1 = x_v[tl, pl.ds(col, SC_LANES)]
                xw2 = x_v[tl, pl.ds(col + SC_LANES, SC_LANES)]
                for q, buf in enumerate(ub):
                    for r in range(SC_ROWS):
                        a = q * SC_ROWS + r
                        lo, hi = _mul_add_words(buf[r, c4, pl.ds(lane0, SC_LANES)], xw1,
                                                buf[r, c4, pl.ds(lane0 + SC_LANES, SC_LANES)], xw2)
                        acc[a] = acc[a] + lo + hi
                return tuple(acc)

            acc = lax.fori_loop(0, HALF // (2 * SC_LANES), cbody,
                                tuple(jnp.zeros((SC_LANES,), F32) for _ in range(unit_rows)))
            for a in range(unit_rows):
                o = pl.multiple_of((rb + a) * SC_LANES, SC_LANES)
                out_v[tl, pl.ds(o, SC_LANES)] = acc[a]

        _sc_batches(t // SC_WORKERS, idx_hbm, x_hbm, out_hbm, idx_v, x_v, out_v,
                    lambda: _sc_gather_ring(u_hbm, idx_v, bufs, sems, unit, compute))

    return body(u3, idx_flat, xn2)


def _peer_combine_sc(v3, idx_flat, coef_b):
    t = coef_b.shape[0]
    unit = 2
    units_per_tok = PEER_SEL // (unit * SC_ROWS)
    nvec = LANES // SC_LANES

    @functools.partial(
        pl.kernel,
        mesh=_sc_mesh(),
        out_type=jax.ShapeDtypeStruct((t, D_MODEL), F32),
        scratch_types=_sc_scratch(PEER_SEL * SC_LANES, D_MODEL),
        compiler_params=pltpu.CompilerParams(needs_layout_passes=False),
        name="peer_combine",
    )
    def body(v_hbm, idx_hbm, coef_hbm, out_hbm, idx_v, coef_v, out_v, *rest):
        bufs, sems = rest[:SC_NBUF], rest[SC_NBUF:]

        def compute(u, ub):
            tl = u // units_per_tok
            rb = (u % units_per_tok) * unit * SC_ROWS
            for g in range(ROW_TILES):
                cols = ([pl.ds(g * LANES + l * SC_LANES, SC_LANES) for l in range(nvec)]
                        + [pl.ds(HALF + g * LANES + l * SC_LANES, SC_LANES) for l in range(nvec)])

                def rbody(r, acc):
                    acc = list(acc)
                    co = pl.multiple_of((rb + r) * SC_LANES, SC_LANES)
                    cf1 = coef_v[tl, pl.ds(co, SC_LANES)]
                    cf2 = coef_v[tl, pl.ds(co + SC_ROWS * SC_LANES, SC_LANES)]
                    for l in range(nvec):
                        lo, hi = _mul_add_words(ub[0][r, g, pl.ds(l * SC_LANES, SC_LANES)], cf1,
                                                ub[1][r, g, pl.ds(l * SC_LANES, SC_LANES)], cf2)
                        acc[l] = acc[l] + lo
                        acc[nvec + l] = acc[nvec + l] + hi
                    return tuple(acc)

                acc = lax.fori_loop(0, SC_ROWS, rbody, tuple(out_v[tl, c] for c in cols))
                for k, c in enumerate(cols):
                    out_v[tl, c] = acc[k]

        def run():
            def zero(tl, c2):
                for k in range(D_MODEL // SC_LANES):
                    out_v[tl, pl.ds(k * SC_LANES, SC_LANES)] = jnp.zeros((SC_LANES,), F32)
                return c2

            lax.fori_loop(0, SC_TOK_BATCH, zero, 0)
            _sc_gather_ring(v_hbm, idx_v, bufs, sems, unit, compute)

        _sc_batches(t // SC_WORKERS, idx_hbm, coef_hbm, out_hbm, idx_v, coef_v, out_v, run)

    return body(v3, idx_flat, coef_b)


def _coef_kernel(part_ref, g_ref, e_ref, coef_ref):
    act = lax.dot_general(part_ref[...], e_ref[...], (((1,), (1,)), ((), ())),
                          precision=lax.Precision.HIGHEST, preferred_element_type=F32)
    gelu = 0.5 * act * (1.0 + lax.erf(act * np.float32(math.sqrt(0.5))))
    c = _dot(g_ref[...] * gelu, e_ref[...], precision=lax.Precision.HIGHEST)
    coef_ref[...] = _pack_pairs(c, c)


def _coef(part, g):
    n = part.shape[0]
    tr = _row_block(n, DENSE_ROWS)
    w = PEER_SEL * SC_LANES
    expand = (jnp.arange(w, dtype=jnp.int32)[None, :] // SC_LANES
              == jnp.arange(PEER_SEL, dtype=jnp.int32)[:, None]).astype(F32)
    wide = pl.BlockSpec((tr, w), lambda i: (i, 0))
    return pl.pallas_call(
        _coef_kernel,
        grid=(n // tr,),
        in_specs=[wide, pl.BlockSpec((tr, PEER_SEL), lambda i: (i, 0)),
                  pl.BlockSpec((PEER_SEL, w), lambda i: (0, 0))],
        out_specs=wide,
        out_shape=jax.ShapeDtypeStruct((n, w), jnp.uint32),
        compiler_params=pltpu.CompilerParams(
            dimension_semantics=("parallel",), vmem_limit_bytes=VMEM_LIMIT),
        name="peer_coef",
    )(part, g, expand)


def _final_kernel(h_ref, p_ref, g_ref, full_ref, o_ref):
    del full_ref
    o_ref[...] = _rms(h_ref[...] + p_ref[...], g_ref[...])


def _final(h2, peer_out, gain, out_full, row0):
    n = h2.shape[0]
    tm = math.gcd(_row_block(n, 2 * DENSE_ROWS), row0) if row0 else _row_block(n, 2 * DENSE_ROWS)
    assert tm % 8 == 0
    blk = pl.BlockSpec((tm, D_MODEL), lambda i: (i, 0))
    return pl.pallas_call(
        _final_kernel,
        grid=(n // tm,),
        in_specs=[blk, blk, pl.BlockSpec((1, D_MODEL), lambda i: (0, 0)),
                  pl.BlockSpec(memory_space=pl.ANY)],
        out_specs=pl.BlockSpec((tm, D_MODEL), lambda i: (row0 // tm + i, 0)),
        out_shape=jax.ShapeDtypeStruct(out_full.shape, F32),
        input_output_aliases={3: 0},
        compiler_params=pltpu.CompilerParams(
            dimension_semantics=("parallel",), vmem_limit_bytes=VMEM_LIMIT),
        name="final_norm",
    )(h2, peer_out, gain, out_full)


def _pad_heads(w, dk):
    lead = w.shape[:-1]
    w = w.reshape(lead + (HEADS, dk))
    w = jnp.pad(w, [(0, 0)] * len(lead) + [(0, 0), (0, LANES - dk)])
    return w.reshape(lead + (HEADS * LANES,))


def _rot_heads(w, dk):
    lead = w.shape[:-1]
    w = w.reshape(lead + (HEADS, 2, dk // 2))
    w = jnp.stack([-w[..., 1, :], w[..., 0, :]], axis=-2)
    return w.reshape(lead + (HEADS * dk,))


def _pack_in_weights(w_in):
    gq, gk, gv, gg, ga, rq, rk, rv, rg = jnp.split(w_in, np.cumsum(IN_SPLITS)[:-1].tolist(), axis=-1)
    ga = jnp.pad(ga, ((0, 0), (0, 2 * LANES - GATE_RANK)))
    return jnp.concatenate(
        [_pad_heads(gq, DK), _pad_heads(gk, DK), gv, gg,
         _pad_heads(rq, DK), _pad_heads(rk, DK), rv, rg,
         _pad_heads(_rot_heads(rq, DK), DK), _pad_heads(_rot_heads(rk, DK), DK), ga],
        axis=-1).astype(BF16)


def _rope_tables(pos):
    inv_freq = ROPE_BASE ** (-jnp.arange(0, DK, 2, dtype=F32) / DK)
    ang = pos[:, None] * inv_freq[None, :]
    full = lambda t: _pad_heads(jnp.tile(t, (1, 2 * HEADS)), DK)
    return full(jnp.cos(ang)), full(jnp.sin(ang))


def _segment_lengths(s, seq):
    if seq < len(SEQ_RAMPS):
        ramp = SEQ_RAMPS[seq]
        lengths = [s * k // sum(ramp) for k in ramp]
        assert sum(lengths) == s
    else:
        lengths = [s // SEGS_PER_SEQ] * SEGS_PER_SEQ
    assert all(n % SC_CALL_TOKENS == 0 for n in lengths)
    return lengths


def kernel(x, meta_tokens, norm1_gain, w_in, gla_gate_w2, gla_gate_b, gla_norm_gain, ret_norm_gain, ret_norm_bias, w_out, norm2_gain, peer_w_q, peer_subkeys, peer_u, peer_v, final_gain):
    b, s, d = x.shape
    assert d == D_MODEL and (s // SEGS_PER_SEQ) % SC_CALL_TOKENS == 0
    assert norm1_gain.shape[0] == 1, "single-layer block"
    x2d = x.reshape(b * s, d)

    w_all = _pack_in_weights(w_in[0])
    w2p = jnp.pad(_pad_heads(gla_gate_w2[0], DK), ((0, LANES - GATE_RANK), (0, 0))).astype(BF16)
    gbp = _pad_heads(gla_gate_b[0][None, :], DK)
    g1 = norm1_gain[0][None, :]
    ggain = gla_norm_gain[0][None, :]
    rgain = ret_norm_gain[0][None, :]
    rbias = ret_norm_bias[0][None, :]

    h0 = jnp.concatenate([jnp.zeros((N_PAD, d), x.dtype), meta_tokens.astype(x.dtype)], axis=0)
    cos0, sin0 = _rope_tables(jnp.arange(CHUNK, dtype=F32) - N_PAD)
    cos1, sin1 = _rope_tables(jnp.arange(s, dtype=F32) + N_META)
    zero_state = jnp.zeros((HEADS, DV, LANES), F32)
    _, s_meta, r_meta = _mixer(h0, g1, w_all, cos0, sin0, w2p, gbp, ggain, rgain, rbias,
                               zero_state, zero_state, batch=1, n_invalid=N_PAD)

    w_out_b = w_out[0].astype(BF16)
    g2 = norm2_gain[0][None, :]
    w_q = peer_w_q[0].astype(BF16)
    subkeys = peer_subkeys[0].reshape(2 * PEER_HEADS, PEER_NKEYS, PEER_DKEY // 2).astype(BF16)
    u3 = _pack_table(peer_u[0])
    v3 = _pack_table(peer_v[0])
    gf = final_gain[None, :]

    out = jnp.zeros((b * s, d), F32)
    prev_g = None
    pending = []
    waiting = None
    for bi in range(b):
        s_state, r_state = s_meta, r_meta
        start = 0
        for seg in _segment_lengths(s, bi):
            row0 = bi * s + start
            xs = x2d[row0:row0 + seg]
            pos = slice(start, start + seg)
            start += seg
            if prev_g is not None:
                xs, _ = lax.optimization_barrier((xs, prev_g))
            o, s_state, r_state = _mixer(xs, g1, w_all, cos1[pos], sin1[pos], w2p, gbp,
                                         ggain, rgain, rbias, s_state, r_state, batch=1, n_invalid=0)
            h2, xn2, xn2_pairs = _out_proj(o, xs, w_out_b, g2)
            idx, g = _route(xn2, w_q, subkeys)
            prev_g = g
            idx_flat = idx.reshape(-1)
            if len(pending) >= 2:
                h2_old, row_old, peer_old = pending.pop(0)
                idx_flat, peer_old = lax.optimization_barrier((idx_flat, peer_old))
                out = _final(h2_old, peer_old, gf, out, row_old)
            part = _peer_dot_sc(u3, idx_flat, xn2_pairs)
            if waiting is not None:
                h2_w, row_w, idx_w, part_w, g_w = waiting
                part_w, _ = lax.optimization_barrier((part_w, g))
                pending.append((h2_w, row_w, _peer_combine_sc(v3, idx_w, _coef(part_w, g_w))))
            waiting = (h2, row0, idx_flat, part, g)
    h2_w, row_w, idx_w, part_w, g_w = waiting
    pending.append((h2_w, row_w, _peer_combine_sc(v3, idx_w, _coef(part_w, g_w))))
    for h2_old, row_old, peer_old in pending:
        out = _final(h2_old, peer_old, gf, out, row_old)
    return out.reshape(b, s, d)
```
